```python
import math
import jax, jax.numpy as jnp
from jax import lax
import numpy as np

D_MODEL = 1024
BATCH = 8
SEQ = 4096
DEPTH = 1

HEAD_DIM = 64
MLA_HEADS = 8
MLA_Q_RANK = 256
MLA_KV_RANK = 128
MLA_NOPE_DIM = 64
MLA_ROPE_DIM = 32
MLA_V_DIM = HEAD_DIM
ROPE_THETA = 10000.0
DIL_HEADS = 8
DIL_PAIRS = ((128, 1), (512, 4), (2048, 16))
DIL_BLOCK = 128
DIL_WIDTH = DIL_HEADS * HEAD_DIM
MLA_WIDTH = MLA_HEADS * MLA_V_DIM
MIX_WIDTH = MLA_WIDTH + DIL_WIDTH
IN_SPLITS = (MLA_Q_RANK, MLA_KV_RANK, MLA_ROPE_DIM, DIL_WIDTH, DIL_WIDTH, DIL_WIDTH)
IN_WIDTH = sum(IN_SPLITS)
D_FF = 2816
CONV_WIDTH = 3
Q_BLOCK = 128
DN_ALPHA = (2.0 * DEPTH) ** 0.25
DN_BETA = (8.0 * DEPTH) ** -0.25
LN_EPS = 1e-5
RMS_EPS = 1e-6

kernel_name = "hybrid_mla_dilated_swa_convffn_deepnorm"


def layer_norm(x, g, b):
    xf = x.astype(jnp.float32)
    mu = jnp.mean(xf, axis=-1, keepdims=True)
    var = jnp.mean(jnp.square(xf - mu), axis=-1, keepdims=True)
    y = (xf - mu) * lax.rsqrt(var + LN_EPS) * g.astype(jnp.float32) + b.astype(jnp.float32)
    return y.astype(x.dtype)


def rms_norm(x, g):
    xf = x.astype(jnp.float32)
    y = xf * lax.rsqrt(jnp.mean(jnp.square(xf), axis=-1, keepdims=True) + RMS_EPS)
    return (y * g.astype(jnp.float32)).astype(x.dtype)


def apply_rope(x, pos):
    half = x.shape[-1] // 2
    freqs = ROPE_THETA ** (-jnp.arange(half, dtype=jnp.float32) / half)
    ang = pos.astype(jnp.float32)[:, None] * freqs[None, :]
    cos = jnp.cos(ang)[None, :, None, :]
    sin = jnp.sin(ang)[None, :, None, :]
    xf = x.astype(jnp.float32)
    x1, x2 = xf[..., :half], xf[..., half:]
    out = jnp.concatenate([x1 * cos - x2 * sin, x1 * sin + x2 * cos], axis=-1)
    return out.astype(x.dtype)


def alibi_slopes(n):
    return 2.0 ** (-8.0 * jnp.arange(1, n + 1, dtype=jnp.float32) / n)


def mla_attention(c_q, c_kv, k_rope, g_cq, g_ckv, w_uq, w_uk, w_uv):
    B, S, _ = c_q.shape
    pos = jnp.arange(S)
    c_q = rms_norm(c_q, g_cq)
    c_kv = rms_norm(c_kv, g_ckv)
    q = jnp.einsum('bsr,rhe->bshe', c_q, w_uq)
    q_nope, q_rope = q[..., :MLA_NOPE_DIM], q[..., MLA_NOPE_DIM:]
    k_nope = jnp.einsum('bsr,rhe->bshe', c_kv, w_uk)
    v = jnp.einsum('bsr,rhe->bshe', c_kv, w_uv)
    q_rope = apply_rope(q_rope, pos)
    k_rope = apply_rope(k_rope[:, :, None, :], pos)
    qf = jnp.concatenate([q_nope, q_rope], axis=-1)
    kf = jnp.concatenate([k_nope, jnp.broadcast_to(k_rope, k_nope.shape[:3] + (MLA_ROPE_DIM,))], axis=-1)
    scale = 1.0 / math.sqrt(MLA_NOPE_DIM + MLA_ROPE_DIM)
    nb = S // Q_BLOCK
    q_blocks = qf.reshape(B, nb, Q_BLOCK, MLA_HEADS, -1).transpose(1, 0, 2, 3, 4)
    k_pos = jnp.arange(S)

    def one_block(args):
        qb, bi = args
        s = jnp.einsum('bqhe,bkhe->bhqk', qb, kf).astype(jnp.float32) * scale
        q_pos = bi * Q_BLOCK + jnp.arange(Q_BLOCK)
        causal = q_pos[:, None] >= k_pos[None, :]
        s = jnp.where(causal[None, None], s, -jnp.inf)
        p = jax.nn.softmax(s, axis=-1).astype(v.dtype)
        return jnp.einsum('bhqk,bkhe->bqhe', p, v)

    o = lax.map(one_block, (q_blocks, jnp.arange(nb)))
    return o.transpose(1, 0, 2, 3, 4).reshape(B, S, MLA_HEADS * MLA_V_DIM)


def dilated_branch(q, k, v, slopes, window, dil):
    B, S, H, E = q.shape
    n_back = window // dil
    blk = DIL_BLOCK
    L = -(-S // (dil * blk)) * dil * blk
    M = L // dil
    nb = M // blk

    def to_sub(a):
        a = jnp.pad(a, ((0, 0), (0, L - S), (0, 0), (0, 0)))
        a = a.reshape(B, M, dil, H, E).transpose(0, 2, 1, 3, 4)
        return a.reshape(B, dil, nb, blk, H, E)

    def with_prev(ab):
        prev = jnp.pad(ab, ((0, 0), (0, 0), (1, 0), (0, 0), (0, 0), (0, 0)))[:, :, :-1]
        return jnp.concatenate([prev, ab], axis=3)

    qb = to_sub(q)
    kb = with_prev(to_sub(k))
    vb = with_prev(to_sub(v))
    s = jnp.einsum('bdnqhe,bdnkhe->bdnhqk', qb, kb).astype(jnp.float32) / math.sqrt(E)
    jq = jnp.arange(nb)[:, None] * blk + jnp.arange(blk)[None, :]
    jk = jnp.arange(nb)[:, None] * blk - blk + jnp.arange(2 * blk)[None, :]
    off = jq[:, :, None] - jk[:, None, :]
    valid = (off >= 0) & (off <= n_back) & (jk[:, None, :] >= 0)
    dist = (off * dil).astype(jnp.float32)
    bias = -slopes[None, :, None, None] * dist[:, None]
    s = jnp.where(valid[:, None], s + bias, -jnp.inf)
    lse = jax.nn.logsumexp(s, axis=-1)
    p = jnp.exp(s - lse[..., None]).astype(v.dtype)
    o = jnp.einsum('bdnhqk,bdnkhe->bdnqhe', p, vb)
    o = o.reshape(B, dil, M, H, E).transpose(0, 2, 1, 3, 4).reshape(B, L, H, E)[:, :S]
    lse = lse.transpose(0, 1, 2, 4, 3).reshape(B, dil, M, H).transpose(0, 2, 1, 3).reshape(B, L, H)[:, :S]
    return o, lse


def dilated_attention(q, k, v):
    B, S, H, E = q.shape
    slopes = alibi_slopes(H)
    outs, lses = [], []
    for window, dil in DIL_PAIRS:
        o, lse = dilated_branch(q, k, v, slopes, window, dil)
        outs.append(o.astype(jnp.float32))
        lses.append(lse)
    w = jax.nn.softmax(jnp.stack(lses, axis=0), axis=0)
    o = jnp.sum(w[..., None] * jnp.stack(outs, axis=0), axis=0)
    return o.astype(q.dtype).reshape(B, S, H * E)


def conv_gated_ffn(x, w_up, conv_w, conv_b, w_down):
    S = x.shape[1]
    u = x @ w_up
    y = conv_b
    for j in range(CONV_WIDTH):
        shift = CONV_WIDTH - 1 - j
        us = jnp.pad(u, ((0, 0), (shift, 0), (0, 0)))[:, :S] if shift else u
        y = y + conv_w[j] * us
    a, g = y[..., :D_FF], y[..., D_FF:]
    return (jax.nn.gelu(g) * a) @ w_down


def _fwd_setup_inputs(seed: int = 0) -> dict:
    key = jax.random.key(seed)
    ks = jax.random.split(key, 17)
    f32 = jnp.float32
    x = jax.random.normal(ks[0], (BATCH, SEQ, D_MODEL), f32)
    col_scale = jnp.concatenate([jnp.ones((IN_WIDTH - DIL_WIDTH,), f32),
                                 jnp.full((DIL_WIDTH,), DN_BETA, f32)])
    w_in = jax.random.normal(ks[1], (D_MODEL, IN_WIDTH), f32) * D_MODEL ** -0.5 * col_scale
    g_cq = 1.0 + 0.02 * jax.random.normal(ks[2], (MLA_Q_RANK,), f32)
    g_ckv = 1.0 + 0.02 * jax.random.normal(ks[3], (MLA_KV_RANK,), f32)
    w_uq = jax.random.normal(ks[4], (MLA_Q_RANK, MLA_HEADS, MLA_NOPE_DIM + MLA_ROPE_DIM), f32) * MLA_Q_RANK ** -0.5
    w_uk = jax.random.normal(ks[5], (MLA_KV_RANK, MLA_HEADS, MLA_NOPE_DIM), f32) * MLA_KV_RANK ** -0.5
    w_uv = jax.random.normal(ks[6], (MLA_KV_RANK, MLA_HEADS, MLA_V_DIM), f32) * MLA_KV_RANK ** -0.5 * DN_BETA
    w_o = jax.random.normal(ks[7], (MIX_WIDTH, D_MODEL), f32) * MIX_WIDTH ** -0.5 * DN_BETA
    ln1_g = 1.0 + 0.02 * jax.random.normal(ks[8], (D_MODEL,), f32)
    ln1_b = 0.02 * jax.random.normal(ks[9], (D_MODEL,), f32)
    w_up = jax.random.normal(ks[10], (D_MODEL, 2 * D_FF), f32) * D_MODEL ** -0.5 * DN_BETA
    conv_w = jax.random.normal(ks[11], (CONV_WIDTH, 2 * D_FF), f32) * CONV_WIDTH ** -0.5
    conv_b = 0.01 * jax.random.normal(ks[12], (2 * D_FF,), f32)
    w_down = jax.random.normal(ks[13], (D_FF, D_MODEL), f32) * D_FF ** -0.5 * DN_BETA
    ln2_g = 1.0 + 0.02 * jax.random.normal(ks[14], (D_MODEL,), f32)
    ln2_b = 0.02 * jax.random.normal(ks[15], (D_MODEL,), f32)
    return {"x": x, "w_in": w_in, "g_cq": g_cq, "g_ckv": g_ckv, "w_uq": w_uq,
            "w_uk": w_uk, "w_uv": w_uv, "w_o": w_o, "ln1_g": ln1_g, "ln1_b": ln1_b,
            "w_up": w_up, "conv_w": conv_w, "conv_b": conv_b, "w_down": w_down,
            "ln2_g": ln2_g, "ln2_b": ln2_b}


def _fwd_reference(x, w_in, g_cq, g_ckv, w_uq, w_uk, w_uv, w_o, ln1_g, ln1_b,
              w_up, conv_w, conv_b, w_down, ln2_g, ln2_b):
    B, S, _ = x.shape
    for _layer in range(DEPTH):
        h = x @ w_in
        idx = np.cumsum(IN_SPLITS)[:-1].tolist()
        c_q, c_kv, k_rope, q_d, k_d, v_d = jnp.split(h, idx, axis=-1)
        o_mla = mla_attention(c_q, c_kv, k_rope, g_cq, g_ckv, w_uq, w_uk, w_uv)
        o_dil = dilated_attention(q_d.reshape(B, S, DIL_HEADS, HEAD_DIM),
                                  k_d.reshape(B, S, DIL_HEADS, HEAD_DIM),
                                  v_d.reshape(B, S, DIL_HEADS, HEAD_DIM))
        mix = jnp.concatenate([o_mla, o_dil], axis=-1) @ w_o
        x = layer_norm(DN_ALPHA * x + mix, ln1_g, ln1_b)
        ffn = conv_gated_ffn(x, w_up, conv_w, conv_b, w_down)
        x = layer_norm(DN_ALPHA * x + ffn, ln2_g, ln2_b)
    return x


import jax as _jax
import jax.numpy as _jnp

TWIN_FORMAT = 'train_step'
FWD_PARAMS = ['x', 'w_in', 'g_cq', 'g_ckv', 'w_uq', 'w_uk', 'w_uv', 'w_o', 'ln1_g', 'ln1_b', 'w_up', 'conv_w', 'conv_b', 'w_down', 'ln2_g', 'ln2_b']
TWIN_WEIGHTS = ['w_in', 'g_cq', 'g_ckv', 'w_uq', 'w_uk', 'w_uv', 'w_o', 'ln1_g', 'ln1_b', 'w_up', 'conv_w', 'conv_b', 'w_down', 'ln2_g', 'ln2_b']
TWIN_DIFF_INPUT = 'x'
TWIN_INPUTS = ['x', 'w_in', 'g_cq', 'g_ckv', 'w_uq', 'w_uk', 'w_uv', 'w_o', 'ln1_g', 'ln1_b', 'w_up', 'conv_w', 'conv_b', 'w_down', 'ln2_g', 'ln2_b', 'loss_target', 'm_w_in', 'm_g_cq', 'm_g_ckv', 'm_w_uq', 'm_w_uk', 'm_w_uv', 'm_w_o', 'm_ln1_g', 'm_ln1_b', 'm_w_up', 'm_conv_w', 'm_conv_b', 'm_w_down', 'm_ln2_g', 'm_ln2_b', 'v_w_in', 'v_g_cq', 'v_g_ckv', 'v_w_uq', 'v_w_uk', 'v_w_uv', 'v_w_o', 'v_ln1_g', 'v_ln1_b', 'v_w_up', 'v_conv_w', 'v_conv_b', 'v_w_down', 'v_ln2_g', 'v_ln2_b']
TWIN_OUTPUTS = ['loss', 'grad_x', 'grad_w_in', 'grad_g_cq', 'grad_g_ckv', 'grad_w_uq', 'grad_w_uk', 'grad_w_uv', 'grad_w_o', 'grad_ln1_g', 'grad_ln1_b', 'grad_w_up', 'grad_conv_w', 'grad_conv_b', 'grad_w_down', 'grad_ln2_g', 'grad_ln2_b', 'delta_w_in', 'delta_g_cq', 'delta_g_ckv', 'delta_w_uq', 'delta_w_uk', 'delta_w_uv', 'delta_w_o', 'delta_ln1_g', 'delta_ln1_b', 'delta_w_up', 'delta_conv_w', 'delta_conv_b', 'delta_w_down', 'delta_ln2_g', 'delta_ln2_b', 'new_m_w_in', 'new_m_g_cq', 'new_m_g_ckv', 'new_m_w_uq', 'new_m_w_uk', 'new_m_w_uv', 'new_m_w_o', 'new_m_ln1_g', 'new_m_ln1_b', 'new_m_w_up', 'new_m_conv_w', 'new_m_conv_b', 'new_m_w_down', 'new_m_ln2_g', 'new_m_ln2_b', 'new_v_w_in', 'new_v_g_cq', 'new_v_g_ckv', 'new_v_w_uq', 'new_v_w_uk', 'new_v_w_uv', 'new_v_w_o', 'new_v_ln1_g', 'new_v_ln1_b', 'new_v_w_up', 'new_v_conv_w', 'new_v_conv_b', 'new_v_w_down', 'new_v_ln2_g', 'new_v_ln2_b']
TWIN_LEAF_KINDS = {'loss': 'loss', 'grad_x': 'grad_x', 'grad_w_in': 'grad_w', 'grad_g_cq': 'grad_w', 'grad_g_ckv': 'grad_w', 'grad_w_uq': 'grad_w', 'grad_w_uk': 'grad_w', 'grad_w_uv': 'grad_w', 'grad_w_o': 'grad_w', 'grad_ln1_g': 'grad_w', 'grad_ln1_b': 'grad_w', 'grad_w_up': 'grad_w', 'grad_conv_w': 'grad_w', 'grad_conv_b': 'grad_w', 'grad_w_down': 'grad_w', 'grad_ln2_g': 'grad_w', 'grad_ln2_b': 'grad_w', 'delta_w_in': 'delta_w', 'delta_g_cq': 'delta_w', 'delta_g_ckv': 'delta_w', 'delta_w_uq': 'delta_w', 'delta_w_uk': 'delta_w', 'delta_w_uv': 'delta_w', 'delta_w_o': 'delta_w', 'delta_ln1_g': 'delta_w', 'delta_ln1_b': 'delta_w', 'delta_w_up': 'delta_w', 'delta_conv_w': 'delta_w', 'delta_conv_b': 'delta_w', 'delta_w_down': 'delta_w', 'delta_ln2_g': 'delta_w', 'delta_ln2_b': 'delta_w', 'new_m_w_in': 'new_m', 'new_m_g_cq': 'new_m', 'new_m_g_ckv': 'new_m', 'new_m_w_uq': 'new_m', 'new_m_w_uk': 'new_m', 'new_m_w_uv': 'new_m', 'new_m_w_o': 'new_m', 'new_m_ln1_g': 'new_m', 'new_m_ln1_b': 'new_m', 'new_m_w_up': 'new_m', 'new_m_conv_w': 'new_m', 'new_m_conv_b': 'new_m', 'new_m_w_down': 'new_m', 'new_m_ln2_g': 'new_m', 'new_m_ln2_b': 'new_m', 'new_v_w_in': 'new_v', 'new_v_g_cq': 'new_v', 'new_v_g_ckv': 'new_v', 'new_v_w_uq': 'new_v', 'new_v_w_uk': 'new_v', 'new_v_w_uv': 'new_v', 'new_v_w_o': 'new_v', 'new_v_ln1_g': 'new_v', 'new_v_ln1_b': 'new_v', 'new_v_w_up': 'new_v', 'new_v_conv_w': 'new_v', 'new_v_conv_b': 'new_v', 'new_v_w_down': 'new_v', 'new_v_ln2_g': 'new_v', 'new_v_ln2_b': 'new_v'}


def _forward(args):
    return _fwd_reference(*[args[k] for k in FWD_PARAMS])


def _output_shape():
    out = _jax.eval_shape(lambda: _forward(_fwd_setup_inputs(0)))
    return out.shape, out.dtype

N_MICROBATCH = 1
ADAM_LR = 0.001
ADAM_B1 = 0.9
ADAM_B2 = 0.999
ADAM_EPS = 1e-08
ADAM_WD = 0.01
ADAM_STEP = 10
PER_EXAMPLE_BATCH_AXIS = {'x': 0, 'loss_target': 0}
SHARED_INPUTS = []
_WEIGHT_DTYPES = {'w_in': _jnp.float32, 'g_cq': _jnp.float32, 'g_ckv': _jnp.float32, 'w_uq': _jnp.float32, 'w_uk': _jnp.float32, 'w_uv': _jnp.float32, 'w_o': _jnp.float32, 'ln1_g': _jnp.float32, 'ln1_b': _jnp.float32, 'w_up': _jnp.float32, 'conv_w': _jnp.float32, 'conv_b': _jnp.float32, 'w_down': _jnp.float32, 'ln2_g': _jnp.float32, 'ln2_b': _jnp.float32}
MOMENT_SCALE = {'w_in': 2.227154e-02, 'g_cq': 1.440684e-02, 'g_ckv': 2.767620e-02, 'w_uq': 8.147621e-03, 'w_uk': 8.425949e-03, 'w_uv': 2.005159e-02, 'w_o': 2.820723e-02, 'ln1_g': 1.028991e+00, 'ln1_b': 4.493170e-01, 'w_up': 2.017649e-02, 'conv_w': 1.174023e-02, 'conv_b': 2.080691e-02, 'w_down': 3.286945e-02, 'ln2_g': 3.200208e+01, 'ln2_b': 6.367088e-01}


def _to_microbatches(a, axis):
    t = _jnp.moveaxis(a, axis, 0)
    t = t.reshape((N_MICROBATCH, t.shape[0] // N_MICROBATCH) + t.shape[1:])
    return _jnp.moveaxis(t, 1, axis + 1)


def setup_inputs(seed: int = 0) -> dict:
    inp = _fwd_setup_inputs(seed)
    key = _jax.random.fold_in(_jax.random.key(seed), 7919)
    shape, _ = _output_shape()
    out = dict(inp)
    out["loss_target"] = _jax.random.normal(_jax.random.fold_in(key, 0), shape, _jnp.float32)
    for i, name in enumerate(TWIN_WEIGHTS):
        w = inp[name].astype(_jnp.float32)
        if MOMENT_SCALE is None:
            s = _jnp.sqrt(_jnp.mean(_jnp.square(w)) + 1e-30)
        else:
            s = MOMENT_SCALE[name]
        km, kv = _jax.random.split(_jax.random.fold_in(key, i + 1))
        out[name] = w
        out["m_" + name] = s * _jax.random.normal(km, w.shape, _jnp.float32)
        out["v_" + name] = (s * s) * _jax.random.uniform(kv, w.shape, _jnp.float32, 0.5, 1.5)
    if N_MICROBATCH > 1:
        for name, axis in PER_EXAMPLE_BATCH_AXIS.items():
            out[name] = _to_microbatches(out[name], axis)
    return {'x': out['x'], 'w_in': out['w_in'], 'g_cq': out['g_cq'], 'g_ckv': out['g_ckv'], 'w_uq': out['w_uq'], 'w_uk': out['w_uk'], 'w_uv': out['w_uv'], 'w_o': out['w_o'], 'ln1_g': out['ln1_g'], 'ln1_b': out['ln1_b'], 'w_up': out['w_up'], 'conv_w': out['conv_w'], 'conv_b': out['conv_b'], 'w_down': out['w_down'], 'ln2_g': out['ln2_g'], 'ln2_b': out['ln2_b'], 'loss_target': out['loss_target'], 'm_w_in': out['m_w_in'], 'm_g_cq': out['m_g_cq'], 'm_g_ckv': out['m_g_ckv'], 'm_w_uq': out['m_w_uq'], 'm_w_uk': out['m_w_uk'], 'm_w_uv': out['m_w_uv'], 'm_w_o': out['m_w_o'], 'm_ln1_g': out['m_ln1_g'], 'm_ln1_b': out['m_ln1_b'], 'm_w_up': out['m_w_up'], 'm_conv_w': out['m_conv_w'], 'm_conv_b': out['m_conv_b'], 'm_w_down': out['m_w_down'], 'm_ln2_g': out['m_ln2_g'], 'm_ln2_b': out['m_ln2_b'], 'v_w_in': out['v_w_in'], 'v_g_cq': out['v_g_cq'], 'v_g_ckv': out['v_g_ckv'], 'v_w_uq': out['v_w_uq'], 'v_w_uk': out['v_w_uk'], 'v_w_uv': out['v_w_uv'], 'v_w_o': out['v_w_o'], 'v_ln1_g': out['v_ln1_g'], 'v_ln1_b': out['v_ln1_b'], 'v_w_up': out['v_w_up'], 'v_conv_w': out['v_conv_w'], 'v_conv_b': out['v_conv_b'], 'v_w_down': out['v_w_down'], 'v_ln2_g': out['v_ln2_g'], 'v_ln2_b': out['v_ln2_b']}


def _loss(weights, diff, rest, loss_target):
    with _jax.named_scope("forward"):
        args = {**rest, TWIN_DIFF_INPUT: diff, **{k: w.astype(_WEIGHT_DTYPES[k]) for k, w in weights.items()}}
        y = _forward(args)
    with _jax.named_scope("loss_head"):
        err = _jnp.square(y.astype(_jnp.float32) - loss_target)
        return 0.5 * _jnp.sum(_jnp.mean(err, axis=-1)) if err.ndim else 0.5 * err


def _adamw(w, g, m, v):
    m = ADAM_B1 * m + (1.0 - ADAM_B1) * g
    v = ADAM_B2 * v + (1.0 - ADAM_B2) * _jnp.square(g)
    m_hat = m / (1.0 - ADAM_B1 ** ADAM_STEP)
    v_hat = v / (1.0 - ADAM_B2 ** ADAM_STEP)
    delta = -ADAM_LR * (m_hat / (_jnp.sqrt(v_hat) + ADAM_EPS) + ADAM_WD * w)
    return delta, m, v


def reference(x, w_in, g_cq, g_ckv, w_uq, w_uk, w_uv, w_o, ln1_g, ln1_b, w_up, conv_w, conv_b, w_down, ln2_g, ln2_b, loss_target, m_w_in, m_g_cq, m_g_ckv, m_w_uq, m_w_uk, m_w_uv, m_w_o, m_ln1_g, m_ln1_b, m_w_up, m_conv_w, m_conv_b, m_w_down, m_ln2_g, m_ln2_b, v_w_in, v_g_cq, v_g_ckv, v_w_uq, v_w_uk, v_w_uv, v_w_o, v_ln1_g, v_ln1_b, v_w_up, v_conv_w, v_conv_b, v_w_down, v_ln2_g, v_ln2_b):
    given = dict(x=x, w_in=w_in, g_cq=g_cq, g_ckv=g_ckv, w_uq=w_uq, w_uk=w_uk, w_uv=w_uv, w_o=w_o, ln1_g=ln1_g, ln1_b=ln1_b, w_up=w_up, conv_w=conv_w, conv_b=conv_b, w_down=w_down, ln2_g=ln2_g, ln2_b=ln2_b, loss_target=loss_target, m_w_in=m_w_in, m_g_cq=m_g_cq, m_g_ckv=m_g_ckv, m_w_uq=m_w_uq, m_w_uk=m_w_uk, m_w_uv=m_w_uv, m_w_o=m_w_o, m_ln1_g=m_ln1_g, m_ln1_b=m_ln1_b, m_w_up=m_w_up, m_conv_w=m_conv_w, m_conv_b=m_conv_b, m_w_down=m_w_down, m_ln2_g=m_ln2_g, m_ln2_b=m_ln2_b, v_w_in=v_w_in, v_g_cq=v_g_cq, v_g_ckv=v_g_ckv, v_w_uq=v_w_uq, v_w_uk=v_w_uk, v_w_uv=v_w_uv, v_w_o=v_w_o, v_ln1_g=v_ln1_g, v_ln1_b=v_ln1_b, v_w_up=v_w_up, v_conv_w=v_conv_w, v_conv_b=v_conv_b, v_w_down=v_w_down, v_ln2_g=v_ln2_g, v_ln2_b=v_ln2_b)
    weights = {n: given[n] for n in TWIN_WEIGHTS}
    shared = {n: given[n] for n in SHARED_INPUTS}
    per_example = {n: given[n] for n in ['x']}
    grad_fn = _jax.value_and_grad(_loss, argnums=(0, 1))

    def one_microbatch(ex, loss_target):
        ex = dict(ex)
        diff = ex.pop(TWIN_DIFF_INPUT)
        return grad_fn(weights, diff, {**shared, **ex}, loss_target)

    if N_MICROBATCH == 1:
        loss, (grad_w, grad_x) = one_microbatch(per_example, given["loss_target"])
    else:
        def body(carry, xs):
            loss_sum, grad_sum = carry
            l_k, (gw_k, gx_k) = one_microbatch(xs[0], xs[1])
            with _jax.named_scope("update"):
                return (loss_sum + l_k, _jax.tree.map(_jnp.add, grad_sum, gw_k)), gx_k

        init = (_jnp.zeros((), _jnp.float32), _jax.tree.map(_jnp.zeros_like, weights))
        (loss, grad_w), grad_x = _jax.lax.scan(body, init, (per_example, given["loss_target"]))
    with _jax.named_scope("update"):
        delta_w, new_m, new_v = {}, {}, {}
        for n in TWIN_WEIGHTS:
            delta_w[n], new_m[n], new_v[n] = _adamw(weights[n], grad_w[n], given["m_" + n], given["v_" + n])
    return (loss, grad_x, *[grad_w[n] for n in TWIN_WEIGHTS], *[delta_w[n] for n in TWIN_WEIGHTS],
            *[new_m[n] for n in TWIN_WEIGHTS], *[new_v[n] for n in TWIN_WEIGHTS])
```

```python
import functools
import math

import jax
import jax.numpy as jnp
from jax import lax
from jax.experimental import pallas as pl
from jax.experimental.pallas import tpu as pltpu

F32 = jnp.float32
MXU_DTYPE = jnp.bfloat16
NEG = -1e30

D_MODEL = 1024
HEADS = 8
HEAD_DIM = 64
Q_RANK = 256
KV_RANK = 128
NOPE = 64
ROPE = 32
QK_PAD = 128
IN_WIDTH = 1952
IN_EXT = 2048
D_FF = 2816
DIL_PAIRS = ((128, 1), (512, 4), (2048, 16))
DIL_BLOCK = 128
ROPE_THETA = 10000.0
DN_ALPHA = 2.0 ** 0.25
LN_EPS = 1e-5
RMS_EPS = 1e-6
MLA_SCALE = 1.0 / math.sqrt(NOPE + ROPE)
DIL_SCALE = 1.0 / math.sqrt(HEAD_DIM)

ADAM_LR = 0.001
ADAM_B1 = 0.9
ADAM_B2 = 0.999
ADAM_EPS = 1e-08
ADAM_WD = 0.01
ADAM_STEP = 10

LANES = 128
SUBLANES = 8
VMEM_LIMIT_BYTES = 56 * 1024 * 1024

MESH = pl.DeviceIdType.MESH


def _params(*sem):
    return pltpu.CompilerParams(dimension_semantics=sem, vmem_limit_bytes=VMEM_LIMIT_BYTES)


def _dot(a, b):
    return jnp.dot(a, b, preferred_element_type=F32)


def _dot_nt(a, b):
    return lax.dot_general(a, b, (((1,), (1,)), ((), ())), preferred_element_type=F32)


def _dot_tn(a, b):
    return lax.dot_general(a, b, (((0,), (0,)), ((), ())), preferred_element_type=F32)


def _mx(a):
    return a.astype(MXU_DTYPE)


def _mm_nn(a, b, *, name, tm, tn, tk, out_dtype=F32, add=None, add_scale=1.0):
    m, kdim = a.shape
    n = b.shape[1]
    nk = kdim // tk

    def body(*refs):
        if add is None:
            a_ref, b_ref, o_ref, acc = refs
        else:
            a_ref, b_ref, c_ref, o_ref, acc = refs
        k = pl.program_id(2)

        @pl.when(k == 0)
        def _():
            acc[...] = jnp.zeros_like(acc)

        acc[...] += _dot(_mx(a_ref[...]), _mx(b_ref[...]))

        @pl.when(k == nk - 1)
        def _():
            r = acc[...]
            if add is not None:
                r = r + add_scale * c_ref[...]
            o_ref[...] = r.astype(out_dtype)

    in_specs = [pl.BlockSpec((tm, tk), lambda i, j, k: (i, k)),
                pl.BlockSpec((tk, tn), lambda i, j, k: (k, j))]
    args = [a, b]
    if add is not None:
        in_specs.append(pl.BlockSpec((tm, tn), lambda i, j, k: (i, j)))
        args.append(add)
    return pl.pallas_call(
        body, name=name,
        out_shape=jax.ShapeDtypeStruct((m, n), out_dtype),
        grid=(m // tm, n // tn, nk),
        in_specs=in_specs,
        out_specs=pl.BlockSpec((tm, tn), lambda i, j, k: (i, j)),
        scratch_shapes=[pltpu.VMEM((tm, tn), F32)],
        compiler_params=_params("parallel", "parallel", "arbitrary"),
    )(*args)


def _mm_tn(a, b, *, name, tm, tn, ts, out_dtype=F32):
    s, m = a.shape
    n = b.shape[1]
    ns = s // ts

    def body(a_ref, b_ref, o_ref, acc):
        k = pl.program_id(2)

        @pl.when(k == 0)
        def _():
            acc[...] = jnp.zeros_like(acc)

        acc[...] += _dot_tn(_mx(a_ref[...]), _mx(b_ref[...]))

        @pl.when(k == ns - 1)
        def _():
            o_ref[...] = acc[...].astype(out_dtype)

    return pl.pallas_call(
        body, name=name,
        out_shape=jax.ShapeDtypeStruct((m, n), out_dtype),
        grid=(m // tm, n // tn, ns),
        in_specs=[pl.BlockSpec((ts, tm), lambda i, j, k: (k, i)),
                  pl.BlockSpec((ts, tn), lambda i, j, k: (k, j))],
        out_specs=pl.BlockSpec((tm, tn), lambda i, j, k: (i, j)),
        scratch_shapes=[pltpu.VMEM((tm, tn), F32)],
        compiler_params=_params("parallel", "parallel", "arbitrary"),
    )(a, b)


def _rope_tables(s):
    half = ROPE // 2
    freqs = ROPE_THETA ** (-jnp.arange(half, dtype=F32) / half)
    ang = jnp.arange(s).astype(F32)[:, None] * freqs[None, :]
    cos, sin = jnp.cos(ang), jnp.sin(ang)
    z = lambda w: jnp.zeros((s, w), F32)
    c = jnp.concatenate([jnp.ones((s, NOPE), F32), cos, cos, z(32)], axis=1)
    s1 = jnp.concatenate([z(NOPE + half), sin, z(32)], axis=1)
    s2 = jnp.concatenate([z(NOPE), -sin, z(half + 32)], axis=1)
    mask = jnp.concatenate([z(NOPE), jnp.ones((s, ROPE), F32), z(32)], axis=1)
    return c, s1, s2, mask


def _rope(x, c, s1, s2):
    return x * c + pltpu.roll(x, 16, 1) * s1 + pltpu.roll(x, LANES - 16, 1) * s2


def _unrope(dy, c, s1, s2):
    return dy * c + pltpu.roll(dy * s1, LANES - 16, 1) + pltpu.roll(dy * s2, 16, 1)


def _rms(x):
    r = lax.rsqrt(jnp.mean(x * x, axis=-1, keepdims=True) + RMS_EPS)
    return x * r, r


def _mla_prep_fwd(h, g_cq, g_ckv, wq, wk, wv, tabs, *, tm):
    s = h.shape[0]
    c_t, s1_t, s2_t, _ = tabs

    def body(h_ref, gq_ref, gkv_ref, wq_ref, wk_ref, wv_ref, c_ref, s1_ref, s2_ref, q_ref, k_ref, v_ref):
        cq = h_ref[:, 0:Q_RANK]
        ckv = h_ref[:, Q_RANK:Q_RANK + KV_RANK]
        kr = h_ref[:, Q_RANK + KV_RANK:Q_RANK + KV_RANK + QK_PAD]
        c, s1, s2 = c_ref[...], s1_ref[...], s2_ref[...]
        cqn = _mx(_rms(cq)[0] * gq_ref[...])
        ckvn = _mx(_rms(ckv)[0] * gkv_ref[...])
        kr_rot = _rope(kr, c, s1, s2)
        for hd in range(HEADS):
            q_ref[hd] = _rope(_dot(cqn, wq_ref[hd]), c, s1, s2).astype(q_ref.dtype)
            k_ref[hd] = (_dot(ckvn, wk_ref[hd]) + kr_rot).astype(k_ref.dtype)
            v_ref[hd] = _dot(ckvn, wv_ref[hd]).astype(v_ref.dtype)

    full = lambda shp: pl.BlockSpec(shp, lambda i: (0,) * len(shp))
    row = lambda w: pl.BlockSpec((tm, w), lambda i: (i, 0))
    return pl.pallas_call(
        body, name="mla_prep_fwd",
        out_shape=(jax.ShapeDtypeStruct((HEADS, s, QK_PAD), MXU_DTYPE),
                   jax.ShapeDtypeStruct((HEADS, s, QK_PAD), MXU_DTYPE),
                   jax.ShapeDtypeStruct((HEADS, s, HEAD_DIM), MXU_DTYPE)),
        grid=(s // tm,),
        in_specs=[row(4 * LANES), full((1, Q_RANK)), full((1, KV_RANK)),
                  full((HEADS, Q_RANK, QK_PAD)), full((HEADS, KV_RANK, QK_PAD)), full((HEADS, KV_RANK, HEAD_DIM)),
                  row(LANES), row(LANES), row(LANES)],
        out_specs=(pl.BlockSpec((HEADS, tm, QK_PAD), lambda i: (0, i, 0)),
                   pl.BlockSpec((HEADS, tm, QK_PAD), lambda i: (0, i, 0)),
                   pl.BlockSpec((HEADS, tm, HEAD_DIM), lambda i: (0, i, 0))),
        compiler_params=_params("parallel"),
    )(h, g_cq, g_ckv, wq, wk, wv, c_t, s1_t, s2_t)


def _mla_prep_bwd(h, dq, dk, dv, g_cq, g_ckv, wq_t, wk_t, wv_t, tabs, *, tm):
    s = h.shape[0]
    c_t, s1_t, s2_t, mask_t = tabs

    def body(h_ref, dq_ref, dk_ref, dv_ref, gq_ref, gkv_ref, wqt_ref, wkt_ref, wvt_ref,
             c_ref, s1_ref, s2_ref, mask_ref, dh_ref, dwq_ref, dwk_ref, dwv_ref, dgq_ref, dgkv_ref):
        i = pl.program_id(0)

        @pl.when(i == 0)
        def _():
            dwq_ref[...] = jnp.zeros_like(dwq_ref)
            dwk_ref[...] = jnp.zeros_like(dwk_ref)
            dwv_ref[...] = jnp.zeros_like(dwv_ref)
            dgq_ref[...] = jnp.zeros_like(dgq_ref)
            dgkv_ref[...] = jnp.zeros_like(dgkv_ref)

        cq = h_ref[:, 0:Q_RANK]
        ckv = h_ref[:, Q_RANK:Q_RANK + KV_RANK]
        c, s1, s2 = c_ref[...], s1_ref[...], s2_ref[...]
        cqh, rq = _rms(cq)
        ckvh, rkv = _rms(ckv)
        gq, gkv = gq_ref[...], gkv_ref[...]
        cqn = _mx(cqh * gq)
        ckvn = _mx(ckvh * gkv)
        dcqn = jnp.zeros((tm, Q_RANK), F32)
        dckvn = jnp.zeros((tm, KV_RANK), F32)
        dkr = jnp.zeros((tm, QK_PAD), F32)
        for hd in range(HEADS):
            dqh = _mx(_unrope(dq_ref[hd], c, s1, s2))
            dcqn = dcqn + _dot(dqh, wqt_ref[hd])
            dwq_ref[hd] += _dot_tn(cqn, dqh)
            dkh = dk_ref[hd]
            dkr = dkr + dkh
            dkh = _mx(dkh)
            dckvn = dckvn + _dot(dkh, wkt_ref[hd])
            dwk_ref[hd] += _dot_tn(ckvn, dkh)
            dvh = _mx(dv_ref[hd])
            dckvn = dckvn + _dot(dvh, wvt_ref[hd])
            dwv_ref[hd] += _dot_tn(ckvn, dvh)
        dgq_ref[...] += jnp.sum(dcqn * cqh, axis=0, keepdims=True)
        dgkv_ref[...] += jnp.sum(dckvn * ckvh, axis=0, keepdims=True)
        gd = dcqn * gq
        dh_ref[:, 0:Q_RANK] = rq * (gd - cqh * jnp.mean(gd * cqh, axis=-1, keepdims=True))
        gd = dckvn * gkv
        dh_ref[:, Q_RANK:Q_RANK + KV_RANK] = rkv * (gd - ckvh * jnp.mean(gd * ckvh, axis=-1, keepdims=True))
        dh_ref[:, Q_RANK + KV_RANK:Q_RANK + KV_RANK + QK_PAD] = _unrope(dkr, c, s1, s2) * mask_ref[...]

    full = lambda shp: pl.BlockSpec(shp, lambda i: (0,) * len(shp))
    row = lambda w: pl.BlockSpec((tm, w), lambda i: (i, 0))
    hrow = lambda w: pl.BlockSpec((HEADS, tm, w), lambda i: (0, i, 0))
    return pl.pallas_call(
        body, name="mla_prep_bwd",
        out_shape=(jax.ShapeDtypeStruct((s, 4 * LANES), F32),
                   jax.ShapeDtypeStruct((HEADS, Q_RANK, QK_PAD), F32),
                   jax.ShapeDtypeStruct((HEADS, KV_RANK, QK_PAD), F32),
                   jax.ShapeDtypeStruct((HEADS, KV_RANK, HEAD_DIM), F32),
                   jax.ShapeDtypeStruct((1, Q_RANK), F32),
                   jax.ShapeDtypeStruct((1, KV_RANK), F32)),
        grid=(s // tm,),
        in_specs=[row(4 * LANES), hrow(QK_PAD), hrow(QK_PAD), hrow(HEAD_DIM),
                  full((1, Q_RANK)), full((1, KV_RANK)),
                  full((HEADS, QK_PAD, Q_RANK)), full((HEADS, QK_PAD, KV_RANK)), full((HEADS, HEAD_DIM, KV_RANK)),
                  row(LANES), row(LANES), row(LANES), row(LANES)],
        out_specs=(row(4 * LANES), full((HEADS, Q_RANK, QK_PAD)), full((HEADS, KV_RANK, QK_PAD)),
                   full((HEADS, KV_RANK, HEAD_DIM)), full((1, Q_RANK)), full((1, KV_RANK))),
        compiler_params=_params("arbitrary"),
    )(h, dq, dk, dv, g_cq, g_ckv, wq_t, wk_t, wv_t, c_t, s1_t, s2_t, mask_t)


def _causal_mask(t):
    r = lax.broadcasted_iota(jnp.int32, (t, t), 0)
    c = lax.broadcasted_iota(jnp.int32, (t, t), 1)
    return r >= c


def _mla_attn_fwd(q, k, v, *, t):
    hds, s, _ = q.shape
    n = s // t

    def body(q_ref, k_ref, v_ref, o_ref, lse_ref, m_sc, l_sc, acc_sc):
        qi, ki = pl.program_id(1), pl.program_id(2)

        @pl.when(ki == 0)
        def _():
            m_sc[...] = jnp.full_like(m_sc, NEG)
            l_sc[...] = jnp.zeros_like(l_sc)
            acc_sc[...] = jnp.zeros_like(acc_sc)

        def step(masked):
            sc = _dot_nt(q_ref[0], k_ref[0]) * MLA_SCALE
            if masked:
                sc = jnp.where(_causal_mask(t), sc, NEG)
            m_prev = m_sc[...]
            m_new = jnp.maximum(m_prev, jnp.max(sc, axis=-1, keepdims=True))
            p = jnp.exp(sc - m_new)
            a = jnp.exp(m_prev - m_new)
            l_sc[...] = a * l_sc[...] + jnp.sum(p, axis=-1, keepdims=True)
            acc_sc[...] = a * acc_sc[...] + _dot(_mx(p), v_ref[0])
            m_sc[...] = m_new

        @pl.when(ki < qi)
        def _():
            step(False)

        @pl.when(ki == qi)
        def _():
            step(True)
            o_ref[0] = acc_sc[...] / l_sc[...]
            lse_ref[0] = m_sc[...] + jnp.log(l_sc[...])

    qspec = lambda w: pl.BlockSpec((1, t, w), lambda h, i, j: (h, i, 0))
    kspec = lambda w: pl.BlockSpec((1, t, w), lambda h, i, j: (h, jnp.minimum(i, j), 0))
    return pl.pallas_call(
        body, name="mla_attn_fwd",
        out_shape=(jax.ShapeDtypeStruct((hds, s, HEAD_DIM), F32), jax.ShapeDtypeStruct((hds, s, 1), F32)),
        grid=(hds, n, n),
        in_specs=[qspec(QK_PAD), kspec(QK_PAD), kspec(HEAD_DIM)],
        out_specs=(qspec(HEAD_DIM), qspec(1)),
        scratch_shapes=[pltpu.VMEM((t, 1), F32), pltpu.VMEM((t, 1), F32), pltpu.VMEM((t, HEAD_DIM), F32)],
        compiler_params=_params("parallel", "parallel", "arbitrary"),
    )(q, k, v)


def _rowdot(a, b, *, ts, name):
    hds, s, e = a.shape

    def body(a_ref, b_ref, o_ref):
        o_ref[0] = jnp.sum(a_ref[0] * b_ref[0], axis=-1, keepdims=True)

    spec = lambda w: pl.BlockSpec((1, ts, w), lambda h, i: (h, i, 0))
    return pl.pallas_call(
        body, name=name,
        out_shape=jax.ShapeDtypeStruct((hds, s, 1), F32),
        grid=(hds, s // ts),
        in_specs=[spec(e), spec(e)],
        out_specs=spec(1),
        compiler_params=_params("parallel", "parallel"),
    )(a, b)


def _mla_attn_dq(q, k, v, do, lse, dd, *, t):
    hds, s, _ = q.shape
    n = s // t

    def body(q_ref, k_ref, v_ref, do_ref, lse_ref, dd_ref, dq_ref, acc):
        qi, ki = pl.program_id(1), pl.program_id(2)

        @pl.when(ki == 0)
        def _():
            acc[...] = jnp.zeros_like(acc)

        def step(masked):
            sc = _dot_nt(q_ref[0], k_ref[0]) * MLA_SCALE
            if masked:
                sc = jnp.where(_causal_mask(t), sc, NEG)
            p = jnp.exp(sc - lse_ref[0])
            dp = _dot_nt(_mx(do_ref[0]), v_ref[0])
            ds = p * (dp - dd_ref[0]) * MLA_SCALE
            acc[...] += _dot(_mx(ds), k_ref[0])

        @pl.when(ki < qi)
        def _():
            step(False)

        @pl.when(ki == qi)
        def _():
            step(True)
            dq_ref[0] = acc[...]

    qspec = lambda w: pl.BlockSpec((1, t, w), lambda h, i, j: (h, i, 0))
    kspec = lambda w: pl.BlockSpec((1, t, w), lambda h, i, j: (h, jnp.minimum(i, j), 0))
    return pl.pallas_call(
        body, name="mla_attn_dq",
        out_shape=jax.ShapeDtypeStruct((hds, s, QK_PAD), F32),
        grid=(hds, n, n),
        in_specs=[qspec(QK_PAD), kspec(QK_PAD), kspec(HEAD_DIM), qspec(HEAD_DIM), qspec(1), qspec(1)],
        out_specs=qspec(QK_PAD),
        scratch_shapes=[pltpu.VMEM((t, QK_PAD), F32)],
        compiler_params=_params("parallel", "parallel", "arbitrary"),
    )(q, k, v, do, lse, dd)


def _mla_attn_dkv(q, k, v, do, lse, dd, *, t):
    hds, s, _ = q.shape
    n = s // t

    def body(q_ref, k_ref, v_ref, do_ref, lse_ref, dd_ref, dk_ref, dv_ref, dk_acc, dv_acc):
        ki, qi = pl.program_id(1), pl.program_id(2)

        @pl.when(qi == 0)
        def _():
            dk_acc[...] = jnp.zeros_like(dk_acc)
            dv_acc[...] = jnp.zeros_like(dv_acc)

        def step(masked):
            sc = _dot_nt(q_ref[0], k_ref[0]) * MLA_SCALE
            if masked:
                sc = jnp.where(_causal_mask(t), sc, NEG)
            p = jnp.exp(sc - lse_ref[0])
            dob = _mx(do_ref[0])
            dv_acc[...] += _dot_tn(_mx(p), dob)
            dp = _dot_nt(dob, v_ref[0])
            ds = p * (dp - dd_ref[0]) * MLA_SCALE
            dk_acc[...] += _dot_tn(_mx(ds), q_ref[0])

        @pl.when(qi == ki)
        def _():
            step(True)

        @pl.when(qi > ki)
        def _():
            step(False)

        @pl.when(qi == n - 1)
        def _():
            dk_ref[0] = dk_acc[...]
            dv_ref[0] = dv_acc[...]

    qspec = lambda w: pl.BlockSpec((1, t, w), lambda h, j, i: (h, jnp.maximum(i, j), 0))
    kspec = lambda w: pl.BlockSpec((1, t, w), lambda h, j, i: (h, j, 0))
    return pl.pallas_call(
        body, name="mla_attn_dkv",
        out_shape=(jax.ShapeDtypeStruct((hds, s, QK_PAD), F32), jax.ShapeDtypeStruct((hds, s, HEAD_DIM), F32)),
        grid=(hds, n, n),
        in_specs=[qspec(QK_PAD), kspec(QK_PAD), kspec(HEAD_DIM), qspec(HEAD_DIM), qspec(1), qspec(1)],
        out_specs=(kspec(QK_PAD), kspec(HEAD_DIM)),
        scratch_shapes=[pltpu.VMEM((t, QK_PAD), F32), pltpu.VMEM((t, HEAD_DIM), F32)],
        compiler_params=_params("parallel", "parallel", "arbitrary"),
    )(q, k, v, do, lse, dd)


def _perm(a, dil):
    if dil == 1:
        return a
    hds, s, e = a.shape
    return a.reshape(hds, s // dil, dil, e).transpose(0, 2, 1, 3).reshape(hds, s, e)


def _unperm(a, dil):
    if dil == 1:
        return a
    hds, s, e = a.shape
    return a.reshape(hds, dil, s // dil, e).transpose(0, 2, 1, 3).reshape(hds, s, e)


def _dil_bias(dil):
    slopes = 2.0 ** (-8.0 * jnp.arange(1, HEADS + 1, dtype=F32) / HEADS)
    iq = jnp.arange(DIL_BLOCK)[:, None]
    ik = jnp.arange(DIL_BLOCK)[None, :]
    off_c = iq - ik
    off_p = iq - ik + DIL_BLOCK
    b_c = -slopes[:, None, None] * (off_c * dil).astype(F32)[None]
    b_p = -slopes[:, None, None] * (off_p * dil).astype(F32)[None]
    b_c = jnp.where((off_c >= 0)[None], b_c, NEG)
    b_p = jnp.where((off_p <= DIL_BLOCK)[None], b_p, NEG)
    return b_c, b_p


def _dil_fwd(q, k, v, dil, *, name):
    hds, s, e = q.shape
    blk = DIL_BLOCK
    nblk = s // blk
    nb = nblk // dil
    b_c, b_p = _dil_bias(dil)

    def body(q_ref, kc_ref, kp_ref, vc_ref, vp_ref, bc_ref, bp_ref, o_ref, lse_ref):
        b = pl.program_id(1)
        first = (b % nb) == 0
        qb = q_ref[0]
        s_c = _dot_nt(qb, kc_ref[0]) * DIL_SCALE + bc_ref[0]
        s_p = jnp.where(first, NEG, _dot_nt(qb, kp_ref[0]) * DIL_SCALE + bp_ref[0])
        m = jnp.maximum(jnp.max(s_c, axis=-1, keepdims=True), jnp.max(s_p, axis=-1, keepdims=True))
        p_c = jnp.exp(s_c - m)
        p_p = jnp.exp(s_p - m)
        l = jnp.sum(p_c, axis=-1, keepdims=True) + jnp.sum(p_p, axis=-1, keepdims=True)
        o_ref[0] = (_dot(_mx(p_c), vc_ref[0]) + _dot(_mx(p_p), vp_ref[0])) / l
        lse_ref[0] = m + jnp.log(l)

    cur = lambda w: pl.BlockSpec((1, blk, w), lambda h, b: (h, b, 0))
    prev = lambda w: pl.BlockSpec((1, blk, w), lambda h, b: (h, jnp.maximum(b - 1, 0), 0))
    bias = pl.BlockSpec((1, blk, blk), lambda h, b: (h, 0, 0))
    return pl.pallas_call(
        body, name=name,
        out_shape=(jax.ShapeDtypeStruct((hds, s, e), F32), jax.ShapeDtypeStruct((hds, s, 1), F32)),
        grid=(hds, nblk),
        in_specs=[cur(e), cur(e), prev(e), cur(e), prev(e), bias, bias],
        out_specs=(cur(e), cur(1)),
        compiler_params=_params("parallel", "parallel"),
    )(q, k, k, v, v, b_c, b_p)


def _dil_combine(os_, lses, *, ts):
    hds, s, e = os_[0].shape

    def body(o0, o1, o2, l0, l1, l2, o_ref, l_ref):
        a0, a1, a2 = l0[0], l1[0], l2[0]
        m = jnp.maximum(jnp.maximum(a0, a1), a2)
        e0, e1, e2 = jnp.exp(a0 - m), jnp.exp(a1 - m), jnp.exp(a2 - m)
        tot = e0 + e1 + e2
        o_ref[0] = ((e0 / tot) * o0[0] + (e1 / tot) * o1[0]) + (e2 / tot) * o2[0]
        l_ref[0] = m + jnp.log(tot)

    spec = lambda w: pl.BlockSpec((1, ts, w), lambda h, i: (h, i, 0))
    return pl.pallas_call(
        body, name="dil_combine",
        out_shape=(jax.ShapeDtypeStruct((hds, s, e), F32), jax.ShapeDtypeStruct((hds, s, 1), F32)),
        grid=(hds, s // ts),
        in_specs=[spec(e)] * 3 + [spec(1)] * 3,
        out_specs=(spec(e), spec(1)),
        compiler_params=_params("parallel", "parallel"),
    )(*os_, *lses)


def _dil_bwd(q, k, v, do, lj, dd, dil, *, name):
    hds, s, e = q.shape
    blk = DIL_BLOCK
    nblk = s // blk
    nb = nblk // dil
    b_c, b_p = _dil_bias(dil)

    def body(q_ref, qn_ref, kc_ref, kp_ref, vc_ref, vp_ref, do_ref, don_ref, l_ref, ln_ref, d_ref, dn_ref,
             bc_ref, bp_ref, dq_ref, dk_ref, dv_ref):
        b = pl.program_id(1)
        first = (b % nb) == 0
        nxt = jnp.logical_and(b + 1 < nblk, ((b + 1) % nb) != 0)
        qb, kc, kp, vc, vp = q_ref[0], kc_ref[0], kp_ref[0], vc_ref[0], vp_ref[0]
        dob = _mx(do_ref[0])
        p_c = jnp.exp(_dot_nt(qb, kc) * DIL_SCALE + bc_ref[0] - l_ref[0])
        p_p = jnp.where(first, 0.0, jnp.exp(_dot_nt(qb, kp) * DIL_SCALE + bp_ref[0] - l_ref[0]))
        ds_c = _mx(p_c * (_dot_nt(dob, vc) - d_ref[0]) * DIL_SCALE)
        ds_p = _mx(p_p * (_dot_nt(dob, vp) - d_ref[0]) * DIL_SCALE)
        dq_ref[0] = _dot(ds_c, kc) + _dot(ds_p, kp)
        qn = qn_ref[0]
        donb = _mx(don_ref[0])
        p_n = jnp.where(nxt, jnp.exp(_dot_nt(qn, kc) * DIL_SCALE + bp_ref[0] - ln_ref[0]), 0.0)
        ds_n = _mx(p_n * (_dot_nt(donb, vc) - dn_ref[0]) * DIL_SCALE)
        dk_ref[0] = _dot_tn(ds_c, qb) + _dot_tn(ds_n, qn)
        dv_ref[0] = _dot_tn(_mx(p_c), dob) + _dot_tn(_mx(p_n), donb)

    cur = lambda w: pl.BlockSpec((1, blk, w), lambda h, b: (h, b, 0))
    prev = lambda w: pl.BlockSpec((1, blk, w), lambda h, b: (h, jnp.maximum(b - 1, 0), 0))
    nxt_ = lambda w: pl.BlockSpec((1, blk, w), lambda h, b: (h, jnp.minimum(b + 1, nblk - 1), 0))
    bias = pl.BlockSpec((1, blk, blk), lambda h, b: (h, 0, 0))
    out = jax.ShapeDtypeStruct((hds, s, e), F32)
    return pl.pallas_call(
        body, name=name,
        out_shape=(out, out, out),
        grid=(hds, nblk),
        in_specs=[cur(e), nxt_(e), cur(e), prev(e), cur(e), prev(e), cur(e), nxt_(e),
                  cur(1), nxt_(1), cur(1), nxt_(1), bias, bias],
        out_specs=(cur(e), cur(e), cur(e)),
        compiler_params=_params("parallel", "parallel"),
    )(q, q, k, k, v, v, do, do, lj, lj, dd, dd, b_c, b_p)


def _add3(a, b, c, *, ts, name):
    hds, s, e = a.shape

    def body(a_ref, b_ref, c_ref, o_ref):
        o_ref[...] = (a_ref[...] + b_ref[...]) + c_ref[...]

    spec = pl.BlockSpec((1, ts, e), lambda h, i: (h, i, 0))
    return pl.pallas_call(
        body, name=name,
        out_shape=jax.ShapeDtypeStruct((hds, s, e), F32),
        grid=(hds, s // ts),
        in_specs=[spec] * 3, out_specs=spec,
        compiler_params=_params("parallel", "parallel"),
    )(a, b, c)


def _ln_fwd(z, g, b):
    mu = jnp.mean(z, axis=-1, keepdims=True)
    zc = z - mu
    var = jnp.mean(zc * zc, axis=-1, keepdims=True)
    rstd = lax.rsqrt(var + LN_EPS)
    xhat = zc * rstd
    return xhat * g + b, xhat, rstd


def _ln_bwd(dy, xhat, rstd, g):
    dxh = dy * g
    return rstd * (dxh - jnp.mean(dxh, axis=-1, keepdims=True) - xhat * jnp.mean(dxh * xhat, axis=-1, keepdims=True))


def _out_ln1(attn, w_o, x, g, b, *, tm):
    s = x.shape[0]

    def body(a_ref, w_ref, x_ref, g_ref, b_ref, x1_ref, xh_ref, r_ref):
        z = DN_ALPHA * x_ref[...] + _dot(a_ref[...], w_ref[...])
        y, xhat, rstd = _ln_fwd(z, g_ref[...], b_ref[...])
        x1_ref[...] = y
        xh_ref[...] = xhat
        r_ref[...] = rstd

    row = lambda w: pl.BlockSpec((tm, w), lambda i: (i, 0))
    full = lambda shp: pl.BlockSpec(shp, lambda i: (0,) * len(shp))
    act = jax.ShapeDtypeStruct((s, D_MODEL), F32)
    return pl.pallas_call(
        body, name="out_ln1",
        out_shape=(act, act, jax.ShapeDtypeStruct((s, 1), F32)),
        grid=(s // tm,),
        in_specs=[row(D_MODEL), full((D_MODEL, D_MODEL)), row(D_MODEL), full((1, D_MODEL)), full((1, D_MODEL))],
        out_specs=(row(D_MODEL), row(D_MODEL), row(1)),
        compiler_params=_params("parallel"),
    )(attn, w_o, x, g, b)


def _down_ln2_loss(act, w_down, x1, g, b, target, *, tm):
    s = x1.shape[0]

    def body(a_ref, w_ref, x1_ref, g_ref, b_ref, t_ref, dz_ref, loss_ref, dg_ref, db_ref):
        i = pl.program_id(0)

        @pl.when(i == 0)
        def _():
            loss_ref[...] = jnp.zeros_like(loss_ref)
            dg_ref[...] = jnp.zeros_like(dg_ref)
            db_ref[...] = jnp.zeros_like(db_ref)

        gam = g_ref[...]
        z = DN_ALPHA * x1_ref[...] + _dot(a_ref[...], w_ref[...])
        y, xhat, rstd = _ln_fwd(z, gam, b_ref[...])
        err = y - t_ref[...]
        loss_ref[...] += 0.5 * jnp.sum(jnp.mean(err * err, axis=-1, keepdims=True))
        dy = err * (1.0 / D_MODEL)
        dg_ref[...] += jnp.sum(dy * xhat, axis=0, keepdims=True)
        db_ref[...] += jnp.sum(dy, axis=0, keepdims=True)
        dz_ref[...] = _ln_bwd(dy, xhat, rstd, gam)

    row = lambda w: pl.BlockSpec((tm, w), lambda i: (i, 0))
    full = lambda shp: pl.BlockSpec(shp, lambda i: (0,) * len(shp))
    vec = jax.ShapeDtypeStruct((1, D_MODEL), F32)
    return pl.pallas_call(
        body, name="down_ln2_loss",
        out_shape=(jax.ShapeDtypeStruct((s, D_MODEL), F32), jax.ShapeDtypeStruct((1, LANES), F32), vec, vec),
        grid=(s // tm,),
        in_specs=[row(D_FF), full((D_FF, D_MODEL)), row(D_MODEL), full((1, D_MODEL)), full((1, D_MODEL)), row(D_MODEL)],
        out_specs=(row(D_MODEL), full((1, LANES)), full((1, D_MODEL)), full((1, D_MODEL))),
        compiler_params=_params("arbitrary"),
    )(act, w_down, x1, g, b, target)


def _up_bwd_ln1(du_a, du_g, w_up_t, dz2, xhat1, rstd1, g, *, tm):
    s = dz2.shape[0]

    def body(dua_ref, dug_ref, wa_ref, wg_ref, dz2_ref, xh_ref, r_ref, g_ref, dz1_ref, dg_ref, db_ref):
        i = pl.program_id(0)

        @pl.when(i == 0)
        def _():
            dg_ref[...] = jnp.zeros_like(dg_ref)
            db_ref[...] = jnp.zeros_like(db_ref)

        dx1 = DN_ALPHA * dz2_ref[...] + (_dot(dua_ref[...], wa_ref[...]) + _dot(dug_ref[...], wg_ref[...]))
        xhat = xh_ref[...]
        dg_ref[...] += jnp.sum(dx1 * xhat, axis=0, keepdims=True)
        db_ref[...] += jnp.sum(dx1, axis=0, keepdims=True)
        dz1_ref[...] = _ln_bwd(dx1, xhat, r_ref[...], g_ref[...])

    row = lambda w: pl.BlockSpec((tm, w), lambda i: (i, 0))
    full = lambda shp: pl.BlockSpec(shp, lambda i: (0,) * len(shp))
    vec = jax.ShapeDtypeStruct((1, D_MODEL), F32)
    return pl.pallas_call(
        body, name="up_bwd_ln1",
        out_shape=(jax.ShapeDtypeStruct((s, D_MODEL), F32), vec, vec),
        grid=(s // tm,),
        in_specs=[row(D_FF), row(D_FF),
                  pl.BlockSpec((D_FF, D_MODEL), lambda i: (0, 0)), pl.BlockSpec((D_FF, D_MODEL), lambda i: (1, 0)),
                  row(D_MODEL), row(D_MODEL), row(1), full((1, D_MODEL))],
        out_specs=(row(D_MODEL), full((1, D_MODEL)), full((1, D_MODEL))),
        compiler_params=_params("arbitrary"),
    )(du_a, du_g, w_up_t, w_up_t, dz2, xhat1, rstd1, g)


GELU_C = math.sqrt(2.0 / math.pi)


def _gelu(x):
    cdf = 0.5 * (1.0 + jnp.tanh(GELU_C * (x + 0.044715 * (x * x * x))))
    return x * cdf


def _gelu_grad(x):
    t = jnp.tanh(GELU_C * (x + 0.044715 * (x * x * x)))
    return 0.5 * (1.0 + t) + 0.5 * x * (1.0 - t * t) * (GELU_C * (1.0 + 3.0 * 0.044715 * (x * x)))


def _shift_down(u, halo):
    t = u.shape[0]
    row = lax.broadcasted_iota(jnp.int32, u.shape, 0)
    h7, h6 = halo[7:8, :], halo[6:7, :]
    s1 = jnp.where(row == 0, h7, pltpu.roll(u, 1, 0))
    s2 = jnp.where(row == 0, h6, jnp.where(row == 1, h7, pltpu.roll(u, 2, 0)))
    return s1, s2


def _shift_up(d, nxt):
    t = d.shape[0]
    row = lax.broadcasted_iota(jnp.int32, d.shape, 0)
    n0, n1 = nxt[0:1, :], nxt[1:2, :]
    s1 = jnp.where(row == t - 1, n0, pltpu.roll(d, t - 1, 0))
    s2 = jnp.where(row == t - 1, n1, jnp.where(row == t - 2, n0, pltpu.roll(d, t - 2, 0)))
    return s1, s2


def _conv(u, s1, s2, w, b):
    return ((b + w[0:1, :] * s2) + w[1:2, :] * s1) + w[2:3, :] * u


def _gate_fwd(u, conv_w, conv_b, *, tm, tn):
    s = u.shape[0]
    nj = D_FF // tn
    hb = tm // SUBLANES

    def body(ua_ref, ug_ref, ha_ref, hg_ref, wa_ref, wg_ref, ba_ref, bg_ref, o_ref):
        keep = pl.program_id(0) > 0
        ua, ug = ua_ref[...], ug_ref[...]
        ha = jnp.where(keep, ha_ref[...], 0.0)
        hg = jnp.where(keep, hg_ref[...], 0.0)
        a = _conv(ua, *_shift_down(ua, ha), wa_ref[...], ba_ref[...])
        g = _conv(ug, *_shift_down(ug, hg), wg_ref[...], bg_ref[...])
        o_ref[...] = (_gelu(g) * a).astype(o_ref.dtype)

    main = lambda off: pl.BlockSpec((tm, tn), lambda i, j: (i, j + off))
    halo = lambda off: pl.BlockSpec((SUBLANES, tn), lambda i, j: (jnp.maximum(i * hb - 1, 0), j + off))
    wspec = lambda r, off: pl.BlockSpec((r, tn), lambda i, j: (0, j + off))
    return pl.pallas_call(
        body, name="gate_fwd",
        out_shape=jax.ShapeDtypeStruct((s, D_FF), MXU_DTYPE),
        grid=(s // tm, nj),
        in_specs=[main(0), main(nj), halo(0), halo(nj), wspec(3, 0), wspec(3, nj), wspec(1, 0), wspec(1, nj)],
        out_specs=pl.BlockSpec((tm, tn), lambda i, j: (i, j)),
        compiler_params=_params("parallel", "parallel"),
    )(u, u, u, u, conv_w, conv_w, conv_b, conv_b)


def _gate_bwd(u, dact, conv_w, conv_b, *, tm, tn):
    s = u.shape[0]
    nj = D_FF // tn
    ni = s // tm
    hb = tm // SUBLANES

    def body(ua_ref, ug_ref, ha_ref, hg_ref, na_ref, ng_ref, d_ref, dn_ref, wa_ref, wg_ref, ba_ref, bg_ref,
             dua_ref, dug_ref, dwa_ref, dwg_ref, dba_ref, dbg_ref):
        i = pl.program_id(1)

        @pl.when(i == 0)
        def _():
            for r in (dwa_ref, dwg_ref, dba_ref, dbg_ref):
                r[...] = jnp.zeros_like(r)

        wa, wg, ba, bg = wa_ref[...], wg_ref[...], ba_ref[...], bg_ref[...]
        ua, ug = ua_ref[...], ug_ref[...]
        ha = jnp.where(i > 0, ha_ref[...], 0.0)
        hg = jnp.where(i > 0, hg_ref[...], 0.0)
        sa1, sa2 = _shift_down(ua, ha)
        sg1, sg2 = _shift_down(ug, hg)
        a = _conv(ua, sa1, sa2, wa, ba)
        g = _conv(ug, sg1, sg2, wg, bg)
        d = d_ref[...]
        dya = d * _gelu(g)
        dyg = d * a * _gelu_grad(g)
        na, ng = na_ref[...], ng_ref[...]
        a_n = _conv(na, *_shift_down(na, ua[tm - SUBLANES:, :]), wa, ba)
        g_n = _conv(ng, *_shift_down(ng, ug[tm - SUBLANES:, :]), wg, bg)
        dn = jnp.where(i < ni - 1, dn_ref[...], 0.0)
        dya_n = dn * _gelu(g_n)
        dyg_n = dn * a_n * _gelu_grad(g_n)
        da1, da2 = _shift_up(dya, dya_n)
        dg1, dg2 = _shift_up(dyg, dyg_n)
        dua_ref[...] = (wa[2:3, :] * dya + wa[1:2, :] * da1 + wa[0:1, :] * da2).astype(dua_ref.dtype)
        dug_ref[...] = (wg[2:3, :] * dyg + wg[1:2, :] * dg1 + wg[0:1, :] * dg2).astype(dug_ref.dtype)
        ssum = lambda v: jnp.sum(v, axis=0, keepdims=True)
        dwa_ref[...] += jnp.concatenate([ssum(dya * sa2), ssum(dya * sa1), ssum(dya * ua)], axis=0)
        dwg_ref[...] += jnp.concatenate([ssum(dyg * sg2), ssum(dyg * sg1), ssum(dyg * ug)], axis=0)
        dba_ref[...] += ssum(dya)
        dbg_ref[...] += ssum(dyg)

    main = lambda off: pl.BlockSpec((tm, tn), lambda j, i: (i, j + off))
    halo = lambda off: pl.BlockSpec((SUBLANES, tn), lambda j, i: (jnp.maximum(i * hb - 1, 0), j + off))
    nxt = lambda off: pl.BlockSpec((SUBLANES, tn), lambda j, i: (jnp.minimum((i + 1) * hb, s // SUBLANES - 1), j + off))
    wspec = lambda r, off: pl.BlockSpec((r, tn), lambda j, i: (0, j + off))
    return pl.pallas_call(
        body, name="gate_bwd",
        out_shape=(jax.ShapeDtypeStruct((s, D_FF), MXU_DTYPE), jax.ShapeDtypeStruct((s, D_FF), MXU_DTYPE),
                   jax.ShapeDtypeStruct((3, D_FF), F32), jax.ShapeDtypeStruct((3, D_FF), F32),
                   jax.ShapeDtypeStruct((1, D_FF), F32), jax.ShapeDtypeStruct((1, D_FF), F32)),
        grid=(nj, ni),
        in_specs=[main(0), main(nj), halo(0), halo(nj), nxt(0), nxt(nj), main(0), nxt(0),
                  wspec(3, 0), wspec(3, nj), wspec(1, 0), wspec(1, nj)],
        out_specs=(main(0), main(0), wspec(3, 0), wspec(3, 0), wspec(1, 0), wspec(1, 0)),
        compiler_params=_params("parallel", "arbitrary"),
    )(u, u, u, u, u, u, dact, dact, conv_w, conv_w, conv_b, conv_b)


def _prep_weights(w_in, w_uq, w_uk, w_uv, w_o, w_up, w_down):
    c = lambda a: a.astype(MXU_DTYPE)
    w_in = c(w_in)
    z = lambda w: jnp.zeros((D_MODEL, w), MXU_DTYPE)
    r0 = Q_RANK + KV_RANK
    w_in_ext = jnp.concatenate([w_in[:, :r0], z(NOPE), w_in[:, r0:r0 + ROPE], z(32), w_in[:, r0 + ROPE:]], axis=1)
    wq = jnp.pad(c(w_uq).transpose(1, 0, 2), ((0, 0), (0, 0), (0, QK_PAD - NOPE - ROPE)))
    wk = jnp.pad(c(w_uk).transpose(1, 0, 2), ((0, 0), (0, 0), (0, QK_PAD - NOPE)))
    wv = c(w_uv).transpose(1, 0, 2)
    t3 = lambda a: a.transpose(0, 2, 1)
    w_o, w_up, w_down = c(w_o), c(w_up), c(w_down)
    return dict(w_in=w_in_ext, w_in_t=w_in_ext.T, wq=wq, wq_t=t3(wq), wk=wk, wk_t=t3(wk), wv=wv, wv_t=t3(wv),
                w_o=w_o, w_o_t=w_o.T, w_up=w_up, w_up_t=w_up.T, w_down=w_down, w_down_t=w_down.T)


def _local_step(x, target, w, g_cq, g_ckv, ln1_g, ln1_b, conv_w, conv_b, ln2_g, ln2_b):
    s = x.shape[0]
    tabs = _rope_tables(s)
    r2 = lambda a: a.reshape(1, -1)
    heads = lambda a: a.reshape(s, HEADS, HEAD_DIM).transpose(1, 0, 2)
    unheads = lambda a: a.transpose(1, 0, 2).reshape(s, HEADS * HEAD_DIM)
    cb = r2(conv_b)
    dils = [d for _, d in DIL_PAIRS]

    h = _mm_nn(x, w["w_in"], name="in_proj", tm=512, tn=1024, tk=D_MODEL)
    q, k, v = _mla_prep_fwd(h, r2(g_cq), r2(g_ckv), w["wq"], w["wk"], w["wv"], tabs, tm=256)
    o_mla, lse_mla = _mla_attn_fwd(q, k, v, t=512)
    qd, kd, vd = (_mx(heads(h[:, 512 * (i + 1):512 * (i + 2)])) for i in range(3))
    qp = [_perm(qd, d) for d in dils]
    kp = [_perm(kd, d) for d in dils]
    vp = [_perm(vd, d) for d in dils]
    o_bs, lse_bs = [], []
    for i, d in enumerate(dils):
        o_b, l_b = _dil_fwd(qp[i], kp[i], vp[i], d, name=f"dil_fwd_{d}")
        o_bs.append(_unperm(o_b, d))
        lse_bs.append(_unperm(l_b, d))
    o_dil, lj = _dil_combine(o_bs, lse_bs, ts=512)
    attn = _mx(jnp.concatenate([unheads(o_mla), unheads(o_dil)], axis=1))
    x1, xhat1, rstd1 = _out_ln1(attn, w["w_o"], x, r2(ln1_g), r2(ln1_b), tm=256)
    u = _mm_nn(x1, w["w_up"], name="up_proj", tm=512, tn=1408, tk=D_MODEL)
    act = _gate_fwd(u, conv_w, cb, tm=256, tn=1408)
    dz2, loss, dg2, db2 = _down_ln2_loss(act, w["w_down"], x1, r2(ln2_g), r2(ln2_b), target, tm=256)

    dact = _mm_nn(dz2, w["w_down_t"], name="down_bwd", tm=512, tn=1408, tk=D_MODEL)
    dw_down = _mm_tn(act, dz2, name="dw_down", tm=1408, tn=D_MODEL, ts=512)
    du_a, du_g, dcw_a, dcw_g, dcb_a, dcb_g = _gate_bwd(u, dact, conv_w, cb, tm=256, tn=1408)
    dz1, dg1, db1 = _up_bwd_ln1(du_a, du_g, w["w_up_t"], dz2, xhat1, rstd1, r2(ln1_g), tm=256)
    dw_up = jnp.concatenate([_mm_tn(x1, du_a, name="dw_up_a", tm=D_MODEL, tn=1408, ts=512),
                             _mm_tn(x1, du_g, name="dw_up_g", tm=D_MODEL, tn=1408, ts=512)], axis=1)
    dattn = _mm_nn(dz1, w["w_o_t"], name="o_bwd", tm=512, tn=D_MODEL, tk=D_MODEL)
    dw_o = _mm_tn(attn, dz1, name="dw_o", tm=D_MODEL, tn=D_MODEL, ts=512)
    do_mla, do_dil = heads(dattn[:, :512]), heads(dattn[:, 512:])
    dd_mla = _rowdot(do_mla, o_mla, ts=512, name="rowdot_mla")
    dq = _mla_attn_dq(q, k, v, do_mla, lse_mla, dd_mla, t=512)
    dk, dv = _mla_attn_dkv(q, k, v, do_mla, lse_mla, dd_mla, t=512)
    dd_dil = _rowdot(do_dil, o_dil, ts=512, name="rowdot_dil")
    parts = []
    for i, d in enumerate(dils):
        g3 = _dil_bwd(qp[i], kp[i], vp[i], _perm(do_dil, d), _perm(lj, d), _perm(dd_dil, d), d, name=f"dil_bwd_{d}")
        parts.append([_unperm(g, d) for g in g3])
    dqd, dkd, dvd = (_add3(parts[0][j], parts[1][j], parts[2][j], ts=512, name=f"dil_sum_{j}") for j in range(3))
    dh_mla, dwq, dwk, dwv, dgq, dgkv = _mla_prep_bwd(h, dq, dk, dv, r2(g_cq), r2(g_ckv),
                                                     w["wq_t"], w["wk_t"], w["wv_t"], tabs, tm=256)
    dh = _mx(jnp.concatenate([dh_mla, unheads(dqd), unheads(dkd), unheads(dvd)], axis=1))
    grad_x = _mm_nn(dh, w["w_in_t"], name="in_bwd", tm=512, tn=D_MODEL, tk=D_MODEL, add=dz1, add_scale=DN_ALPHA)
    dw_ext = _mm_tn(x, dh, name="dw_in", tm=D_MODEL, tn=1024, ts=512)
    r0 = Q_RANK + KV_RANK
    grads = dict(
        w_in=jnp.concatenate([dw_ext[:, :r0], dw_ext[:, r0 + NOPE:r0 + NOPE + ROPE], dw_ext[:, 512:]], axis=1),
        g_cq=dgq[0], g_ckv=dgkv[0],
        w_uq=dwq[:, :, :NOPE + ROPE].transpose(1, 0, 2),
        w_uk=dwk[:, :, :NOPE].transpose(1, 0, 2),
        w_uv=dwv.transpose(1, 0, 2),
        w_o=dw_o, ln1_g=dg1[0], ln1_b=db1[0], w_up=dw_up,
        conv_w=jnp.concatenate([dcw_a, dcw_g], axis=1), conv_b=jnp.concatenate([dcb_a, dcb_g], axis=1)[0],
        w_down=dw_down, ln2_g=dg2[0], ln2_b=db2[0])
    return loss[0, 0], grad_x, grads


N_CHIPS = 4
SHARDED = ("w_in", "w_uq", "w_o", "w_up", "conv_w", "w_down")
COL_SHARDED = ("w_in", "w_up", "conv_w")
SHARD_SHAPE = dict(w_in=(D_MODEL, IN_WIDTH // 4), w_uq=(Q_RANK // 4, HEADS, NOPE + ROPE), w_o=(D_MODEL // 4, D_MODEL),
                   w_up=(D_MODEL, 2 * D_FF // 4), conv_w=(3, 2 * D_FF // 4), w_down=(D_FF // 4, D_MODEL))
SMALL = ("g_cq", "g_ckv", "w_uk", "w_uv", "ln1_g", "ln1_b", "conv_b", "ln2_g", "ln2_b")
SMALL_SHAPE = dict(g_cq=(Q_RANK,), g_ckv=(KV_RANK,), w_uk=(KV_RANK, HEADS, NOPE), w_uv=(KV_RANK, HEADS, HEAD_DIM),
                   ln1_g=(D_MODEL,), ln1_b=(D_MODEL,), conv_b=(2 * D_FF,), ln2_g=(D_MODEL,), ln2_b=(D_MODEL,))
ROW_TILE = 512


def _size(shape):
    return math.prod(shape)


def _padded_rows(n_elems, mult):
    return -(-n_elems // (LANES * mult)) * mult


SHARD_ROWS = {n: _padded_rows(_size(SHARD_SHAPE[n]), SUBLANES) for n in SHARDED}
R_SHARD = -(-sum(SHARD_ROWS.values()) // (2 * ROW_TILE)) * (2 * ROW_TILE)
R_HALF = R_SHARD // 2
R_SMALL = -(-sum(_size(SMALL_SHAPE[n]) for n in SMALL) // (LANES * LANES)) * LANES
GATHERED = ("w_in", "w_uq", "w_o", "w_up", "w_down")
R_GATHER = sum(SHARD_ROWS[n] for n in GATHERED)


def _rows(a, rows=None):
    flat = a.reshape(-1)
    rows = -(-flat.shape[0] // LANES) if rows is None else rows
    return jnp.pad(flat, (0, rows * LANES - flat.shape[0])).reshape(rows, LANES)


def _pack_shard(t):
    parts = [_rows(t[n].astype(F32), SHARD_ROWS[n]) for n in SHARDED]
    used = sum(SHARD_ROWS.values())
    return jnp.concatenate(parts + [jnp.zeros((R_SHARD - used, LANES), F32)], axis=0)


def _unpack_shard(buf):
    out, r = {}, 0
    for n in SHARDED:
        out[n] = buf[r:r + SHARD_ROWS[n]].reshape(-1)[:_size(SHARD_SHAPE[n])].reshape(SHARD_SHAPE[n])
        r += SHARD_ROWS[n]
    return out


def _chip_blocks(name, full):
    shp = SHARD_SHAPE[name]
    if name in COL_SHARDED:
        a = full.reshape(shp[0], N_CHIPS, shp[1]).transpose(1, 0, 2)
    else:
        a = full.reshape((N_CHIPS,) + shp)
    a = a.reshape(N_CHIPS, -1)
    rows = SHARD_ROWS[name]
    return jnp.pad(a, ((0, 0), (0, rows * LANES - a.shape[1]))).reshape(N_CHIPS, rows, LANES)


def _from_chip_blocks(name, blocks):
    shp = SHARD_SHAPE[name]
    a = blocks.reshape(N_CHIPS, -1)[:, :_size(shp)].reshape((N_CHIPS,) + shp)
    if name in COL_SHARDED:
        return a.transpose(1, 0, 2).reshape(shp[0], N_CHIPS * shp[1])
    return a.reshape((N_CHIPS * shp[0],) + shp[1:])


def _pack_grads(g):
    parts = [_chip_blocks(n, g[n]) for n in SHARDED]
    used = sum(SHARD_ROWS.values())
    return jnp.concatenate(parts + [jnp.zeros((N_CHIPS, R_SHARD - used, LANES), F32)], axis=1)


def _pack_small(t):
    flat = jnp.concatenate([t[n].astype(F32).reshape(-1) for n in SMALL])
    return _rows(flat, R_SMALL)


def _unpack_small(buf):
    flat, out, r = buf.reshape(-1), {}, 0
    for n in SMALL:
        out[n] = flat[r:r + _size(SMALL_SHAPE[n])].reshape(SMALL_SHAPE[n])
        r += _size(SMALL_SHAPE[n])
    return out


ANY = pl.BlockSpec(memory_space=pl.ANY)
COMM_PARAMS = pltpu.CompilerParams(has_side_effects=True)


def _coords():
    return lax.axis_index("x"), lax.axis_index("y"), lax.axis_index("c")


def _other_chips(x, y):
    return [(1 - x, y), (x, 1 - y), (1 - x, 1 - y)]


def _remote(src, dst, send_sems, recv_sems, k, to):
    return pltpu.make_async_remote_copy(src_ref=src, dst_ref=dst, send_sem=send_sems.at[k], recv_sem=recv_sems.at[k],
                                        device_id=to, device_id_type=MESH)


def _gather_weights(wp, cwp):
    def body(wp_ref, cw_ref, wout_ref, cwout_ref, send_sems, recv_sems, loc_sems):
        x, y, c = _coords()
        me = 2 * x + y
        chips = _other_chips(x, y)
        srcs, outs = (wp_ref, cw_ref), (wout_ref, cwout_ref)
        loc = [pltpu.make_async_copy(srcs[a], outs[a].at[me], loc_sems.at[a]) for a in range(2)]
        for cp in loc:
            cp.start()
        sends = [_remote(srcs[a], outs[a].at[me], send_sems, recv_sems, 2 * j + a, (px, py, c))
                 for j, (px, py) in enumerate(chips) for a in range(2)]
        for cp in sends:
            cp.start()
        for j, (px, py) in enumerate(chips):
            for a in range(2):
                _remote(srcs[a], outs[a].at[2 * px + py], send_sems, recv_sems, 2 * j + a, (px, py, c)).wait_recv()
        for cp in sends:
            cp.wait_send()
        for cp in loc:
            cp.wait()

    return pl.pallas_call(
        body, name="gather_weights",
        out_shape=(jax.ShapeDtypeStruct((N_CHIPS,) + wp.shape, wp.dtype), jax.ShapeDtypeStruct((N_CHIPS,) + cwp.shape, cwp.dtype)),
        in_specs=[ANY, ANY], out_specs=(ANY, ANY),
        scratch_shapes=[pltpu.SemaphoreType.DMA((6,)), pltpu.SemaphoreType.DMA((6,)), pltpu.SemaphoreType.DMA((2,))],
        compiler_params=COMM_PARAMS,
    )(wp, cwp)


def _exchange_sibling_halves(gs, gr):
    def body(gs_ref, gr_ref, os_ref, or_ref, send_sems, recv_sems):
        x, y, c = _coords()
        sib = (x, y, 1 - c)
        cps = [_remote(gs_ref.at[k, 1 - c], os_ref.at[k], send_sems, recv_sems, k, sib) for k in range(N_CHIPS)]
        cps.append(_remote(gr_ref, or_ref, send_sems, recv_sems, N_CHIPS, sib))
        for cp in cps:
            cp.start()
        for cp in cps:
            cp.wait_recv()
        for cp in cps:
            cp.wait_send()

    return pl.pallas_call(
        body, name="exchange_sibling_halves",
        out_shape=(jax.ShapeDtypeStruct((N_CHIPS, R_HALF, LANES), F32), jax.ShapeDtypeStruct(gr.shape, F32)),
        in_specs=[ANY, ANY], out_specs=(ANY, ANY),
        scratch_shapes=[pltpu.SemaphoreType.DMA((N_CHIPS + 1,)), pltpu.SemaphoreType.DMA((N_CHIPS + 1,))],
        compiler_params=COMM_PARAMS,
    )(gs, gr)


def _exchange_chips(ps, pr):
    def body(ps_ref, pr_ref, ss_ref, sr_ref, send_sems, recv_sems, loc_sems):
        x, y, c = _coords()
        me = 2 * x + y
        chips = _other_chips(x, y)
        loc = [pltpu.make_async_copy(ps_ref.at[me], ss_ref.at[me], loc_sems.at[0]),
               pltpu.make_async_copy(pr_ref, sr_ref.at[me], loc_sems.at[1])]
        for cp in loc:
            cp.start()
        sends = []
        for j, (px, py) in enumerate(chips):
            sends.append(_remote(ps_ref.at[2 * px + py], ss_ref.at[me], send_sems, recv_sems, 2 * j, (px, py, c)))
            sends.append(_remote(pr_ref, sr_ref.at[me], send_sems, recv_sems, 2 * j + 1, (px, py, c)))
        for cp in sends:
            cp.start()
        for j, (px, py) in enumerate(chips):
            k = 2 * px + py
            _remote(ps_ref.at[me], ss_ref.at[k], send_sems, recv_sems, 2 * j, (px, py, c)).wait_recv()
            _remote(pr_ref, sr_ref.at[k], send_sems, recv_sems, 2 * j + 1, (px, py, c)).wait_recv()
        for cp in sends:
            cp.wait_send()
        for cp in loc:
            cp.wait()

    return pl.pallas_call(
        body, name="exchange_chips",
        out_shape=(jax.ShapeDtypeStruct(ps.shape, F32), jax.ShapeDtypeStruct((N_CHIPS,) + pr.shape, F32)),
        in_specs=[ANY, ANY], out_specs=(ANY, ANY),
        scratch_shapes=[pltpu.SemaphoreType.DMA((6,)), pltpu.SemaphoreType.DMA((6,)), pltpu.SemaphoreType.DMA((2,))],
        compiler_params=COMM_PARAMS,
    )(ps, pr)


def _exchange_sibling_result(gh):
    def body(gh_ref, out_ref, send_sems, recv_sems, loc_sem):
        x, y, c = _coords()
        sib = (x, y, 1 - c)
        loc = pltpu.make_async_copy(gh_ref, out_ref.at[c], loc_sem.at[0])
        loc.start()
        cp = _remote(gh_ref, out_ref.at[c], send_sems, recv_sems, 0, sib)
        cp.start()
        _remote(gh_ref, out_ref.at[1 - c], send_sems, recv_sems, 0, sib).wait_recv()
        cp.wait_send()
        loc.wait()

    return pl.pallas_call(
        body, name="exchange_sibling_result",
        out_shape=jax.ShapeDtypeStruct((2,) + gh.shape, F32),
        in_specs=[ANY], out_specs=ANY,
        scratch_shapes=[pltpu.SemaphoreType.DMA((1,)), pltpu.SemaphoreType.DMA((1,)), pltpu.SemaphoreType.DMA((1,))],
        compiler_params=COMM_PARAMS,
    )(gh)


def _add_own_half(gs, recv, c_arr):
    def body(c_ref, a_ref, b_ref, o_ref):
        o_ref[0] = a_ref[0, 0] + b_ref[0]

    return pl.pallas_call(
        body, name="add_own_half",
        out_shape=jax.ShapeDtypeStruct(recv.shape, F32),
        grid_spec=pltpu.PrefetchScalarGridSpec(
            num_scalar_prefetch=1, grid=(N_CHIPS, R_HALF // ROW_TILE),
            in_specs=[pl.BlockSpec((1, 1, ROW_TILE, LANES), lambda k, i, c_ref: (k, c_ref[0], i, 0)),
                      pl.BlockSpec((1, ROW_TILE, LANES), lambda k, i, c_ref: (k, i, 0))],
            out_specs=pl.BlockSpec((1, ROW_TILE, LANES), lambda k, i, c_ref: (k, i, 0))),
        compiler_params=_params("parallel", "parallel"),
    )(c_arr, gs, recv)


def _add2(a, b, *, name):
    def body(a_ref, b_ref, o_ref):
        o_ref[...] = a_ref[...] + b_ref[...]

    return pl.pallas_call(body, name=name, out_shape=jax.ShapeDtypeStruct(a.shape, F32))(a, b)


def _sum_slots(slots, *, tr, name):
    r = slots.shape[1]

    def body(s_ref, o_ref):
        o_ref[...] = ((s_ref[0] + s_ref[1]) + s_ref[2]) + s_ref[3]

    return pl.pallas_call(
        body, name=name,
        out_shape=jax.ShapeDtypeStruct((r, LANES), F32),
        grid=(r // tr,),
        in_specs=[pl.BlockSpec((N_CHIPS, tr, LANES), lambda i: (0, i, 0))],
        out_specs=pl.BlockSpec((tr, LANES), lambda i: (i, 0)),
        compiler_params=_params("parallel"),
    )(slots)


def _adamw(w, g, m, v, *, tr, name):
    r = w.shape[0]

    def body(w_ref, g_ref, m_ref, v_ref, d_ref, nm_ref, nv_ref):
        g_ = g_ref[...]
        m_ = ADAM_B1 * m_ref[...] + (1.0 - ADAM_B1) * g_
        v_ = ADAM_B2 * v_ref[...] + (1.0 - ADAM_B2) * (g_ * g_)
        m_hat = m_ / (1.0 - ADAM_B1 ** ADAM_STEP)
        v_hat = v_ / (1.0 - ADAM_B2 ** ADAM_STEP)
        d_ref[...] = -ADAM_LR * (m_hat / (jnp.sqrt(v_hat) + ADAM_EPS) + ADAM_WD * w_ref[...])
        nm_ref[...] = m_
        nv_ref[...] = v_

    spec = pl.BlockSpec((tr, LANES), lambda i: (i, 0))
    out = jax.ShapeDtypeStruct((r, LANES), F32)
    return pl.pallas_call(
        body, name=name, out_shape=(out, out, out), grid=(r // tr,),
        in_specs=[spec] * 4, out_specs=(spec,) * 3,
        compiler_params=_params("parallel"),
    )(w, g, m, v)


WEIGHTS = ("w_in", "g_cq", "g_ckv", "w_uq", "w_uk", "w_uv", "w_o", "ln1_g", "ln1_b", "w_up", "conv_w", "conv_b",
           "w_down", "ln2_g", "ln2_b")


def kernel(x, w_in, g_cq, g_ckv, w_uq, w_uk, w_uv, w_o, ln1_g, ln1_b, w_up, conv_w, conv_b, w_down, ln2_g, ln2_b, loss_target, m_w_in, m_g_cq, m_g_ckv, m_w_uq, m_w_uk, m_w_uv, m_w_o, m_ln1_g, m_ln1_b, m_w_up, m_conv_w, m_conv_b, m_w_down, m_ln2_g, m_ln2_b, v_w_in, v_g_cq, v_g_ckv, v_w_uq, v_w_uk, v_w_uv, v_w_o, v_ln1_g, v_ln1_b, v_w_up, v_conv_w, v_conv_b, v_w_down, v_ln2_g, v_ln2_b):
    wts = dict(zip(WEIGHTS, (w_in, g_cq, g_ckv, w_uq, w_uk, w_uv, w_o, ln1_g, ln1_b, w_up, conv_w, conv_b, w_down, ln2_g, ln2_b)))
    mom = dict(zip(WEIGHTS, (m_w_in, m_g_cq, m_g_ckv, m_w_uq, m_w_uk, m_w_uv, m_w_o, m_ln1_g, m_ln1_b, m_w_up, m_conv_w, m_conv_b, m_w_down, m_ln2_g, m_ln2_b)))
    var = dict(zip(WEIGHTS, (v_w_in, v_g_cq, v_g_ckv, v_w_uq, v_w_uk, v_w_uv, v_w_o, v_ln1_g, v_ln1_b, v_w_up, v_conv_w, v_conv_b, v_w_down, v_ln2_g, v_ln2_b)))

    wp = jnp.concatenate([_rows(_mx(wts[n]), SHARD_ROWS[n]) for n in GATHERED], axis=0)
    cwp = _rows(conv_w, SHARD_ROWS["conv_w"])
    wfull, cwfull = _gather_weights(wp, cwp)
    full, r = {}, 0
    for n in GATHERED:
        full[n] = _from_chip_blocks(n, wfull[:, r:r + SHARD_ROWS[n]])
        r += SHARD_ROWS[n]
    conv_w_full = _from_chip_blocks("conv_w", cwfull)
    w = _prep_weights(full["w_in"], full["w_uq"], w_uk, w_uv, full["w_o"], full["w_up"], full["w_down"])

    loss, grad_x, g = _local_step(x[0], loss_target[0], w, g_cq, g_ckv, ln1_g, ln1_b, conv_w_full, conv_b, ln2_g, ln2_b)
    loss = lax.psum(loss, ("x", "y", "c"))

    gs = _pack_grads(g).reshape(N_CHIPS, 2, R_HALF, LANES)
    gr = _pack_small(g)
    c_arr = lax.axis_index("c").astype(jnp.int32).reshape(1)
    recv_s, recv_r = _exchange_sibling_halves(gs, gr)
    ps = _add_own_half(gs, recv_s, c_arr)
    pr = _add2(gr, recv_r, name="add_small")
    slots_s, slots_r = _exchange_chips(ps, pr)
    g_half = _sum_slots(slots_s, tr=ROW_TILE, name="sum_chips")
    g_small = _sum_slots(slots_r, tr=R_SMALL, name="sum_chips_small")
    g_shard = _exchange_sibling_result(g_half).reshape(R_SHARD, LANES)

    d_s, m_s, v_s = _adamw(_pack_shard(wts), g_shard, _pack_shard(mom), _pack_shard(var), tr=ROW_TILE, name="adamw_shard")
    d_r, m_r, v_r = _adamw(_pack_small(wts), g_small, _pack_small(mom), _pack_small(var), tr=R_SMALL, name="adamw_small")
    outs = []
    for shard_buf, small_buf in ((g_shard, g_small), (d_s, d_r), (m_s, m_r), (v_s, v_r)):
        t = {**_unpack_shard(shard_buf), **_unpack_small(small_buf)}
        outs.extend(t[n] for n in WEIGHTS)
    return (loss, grad_x[None], *outs)
```

```python
import functools
import math

import jax
import jax.numpy as jnp
from jax import lax
from jax.experimental import pallas as pl
from jax.experimental.pallas import tpu as pltpu

F32 = jnp.float32
MXU_DTYPE = jnp.bfloat16
GRAD_WIRE_DTYPE = jnp.bfloat16
NEG = -1e30

D_MODEL = 1024
HEADS = 8
HEAD_DIM = 64
Q_RANK = 256
KV_RANK = 128
NOPE = 64
ROPE = 32
QK_PAD = 128
IN_WIDTH = 1952
IN_EXT = 2048
D_FF = 2816
DIL_PAIRS = ((128, 1), (512, 4), (2048, 16))
DIL_BLOCK = 128
ROPE_THETA = 10000.0
DN_ALPHA = 2.0 ** 0.25
LN_EPS = 1e-5
RMS_EPS = 1e-6
MLA_SCALE = 1.0 / math.sqrt(NOPE + ROPE)
DIL_SCALE = 1.0 / math.sqrt(HEAD_DIM)

ADAM_LR = 0.001
ADAM_B1 = 0.9
ADAM_B2 = 0.999
ADAM_EPS = 1e-08
ADAM_WD = 0.01
ADAM_STEP = 10

LANES = 128
SUBLANES = 8
VMEM_LIMIT_BYTES = 56 * 1024 * 1024

MESH = pl.DeviceIdType.MESH


def _params(*sem):
    return pltpu.CompilerParams(dimension_semantics=sem, vmem_limit_bytes=VMEM_LIMIT_BYTES)


def _dot(a, b):
    return jnp.dot(a, b, preferred_element_type=F32)


def _dot_nt(a, b):
    return lax.dot_general(a, b, (((1,), (1,)), ((), ())), preferred_element_type=F32)


def _dot_tn(a, b):
    return lax.dot_general(a, b, (((0,), (0,)), ((), ())), preferred_element_type=F32)


def _mx(a):
    return a.astype(MXU_DTYPE)


def _mm_nn(a, b, *, name, tm, tn, tk, out_dtype=F32, add=None, add_scale=1.0):
    m, kdim = a.shape
    n = b.shape[1]
    nk = kdim // tk

    def body(*refs):
        if add is None:
            a_ref, b_ref, o_ref, acc = refs
        else:
            a_ref, b_ref, c_ref, o_ref, acc = refs
        k = pl.program_id(2)

        @pl.when(k == 0)
        def _():
            acc[...] = jnp.zeros_like(acc)

        acc[...] += _dot(_mx(a_ref[...]), _mx(b_ref[...]))

        @pl.when(k == nk - 1)
        def _():
            r = acc[...]
            if add is not None:
                r = r + add_scale * c_ref[...]
            o_ref[...] = r.astype(out_dtype)

    in_specs = [pl.BlockSpec((tm, tk), lambda i, j, k: (i, k)),
                pl.BlockSpec((tk, tn), lambda i, j, k: (k, j))]
    args = [a, b]
    if add is not None:
        in_specs.append(pl.BlockSpec((tm, tn), lambda i, j, k: (i, j)))
        args.append(add)
    return pl.pallas_call(
        body, name=name,
        out_shape=jax.ShapeDtypeStruct((m, n), out_dtype),
        grid=(m // tm, n // tn, nk),
        in_specs=in_specs,
        out_specs=pl.BlockSpec((tm, tn), lambda i, j, k: (i, j)),
        scratch_shapes=[pltpu.VMEM((tm, tn), F32)],
        compiler_params=_params("parallel", "parallel", "arbitrary"),
    )(*args)


def _mm_tn(a, b, *, name, tm, tn, ts, out_dtype=F32):
    s, m = a.shape
    n = b.shape[1]
    ns = s // ts

    def body(a_ref, b_ref, o_ref, acc):
        k = pl.program_id(2)

        @pl.when(k == 0)
        def _():
            acc[...] = jnp.zeros_like(acc)

        acc[...] += _dot_tn(_mx(a_ref[...]), _mx(b_ref[...]))

        @pl.when(k == ns - 1)
        def _():
            o_ref[...] = acc[...].astype(out_dtype)

    return pl.pallas_call(
        body, name=name,
        out_shape=jax.ShapeDtypeStruct((m, n), out_dtype),
        grid=(m // tm, n // tn, ns),
        in_specs=[pl.BlockSpec((ts, tm), lambda i, j, k: (k, i)),
                  pl.BlockSpec((ts, tn), lambda i, j, k: (k, j))],
        out_specs=pl.BlockSpec((tm, tn), lambda i, j, k: (i, j)),
        scratch_shapes=[pltpu.VMEM((tm, tn), F32)],
        compiler_params=_params("parallel", "parallel", "arbitrary"),
    )(a, b)


def _rope_tables(s):
    half = ROPE // 2
    freqs = ROPE_THETA ** (-jnp.arange(half, dtype=F32) / half)
    ang = jnp.arange(s).astype(F32)[:, None] * freqs[None, :]
    cos, sin = jnp.cos(ang), jnp.sin(ang)
    z = lambda w: jnp.zeros((s, w), F32)
    c = jnp.concatenate([jnp.ones((s, NOPE), F32), cos, cos, z(32)], axis=1)
    s1 = jnp.concatenate([z(NOPE + half), sin, z(32)], axis=1)
    s2 = jnp.concatenate([z(NOPE), -sin, z(half + 32)], axis=1)
    mask = jnp.concatenate([z(NOPE), jnp.ones((s, ROPE), F32), z(32)], axis=1)
    return c, s1, s2, mask


def _rope(x, c, s1, s2):
    return x * c + pltpu.roll(x, 16, 1) * s1 + pltpu.roll(x, LANES - 16, 1) * s2


def _unrope(dy, c, s1, s2):
    return dy * c + pltpu.roll(dy * s1, LANES - 16, 1) + pltpu.roll(dy * s2, 16, 1)


def _rms(x):
    r = lax.rsqrt(jnp.mean(x * x, axis=-1, keepdims=True) + RMS_EPS)
    return x * r, r


def _mla_prep_fwd(h, g_cq, g_ckv, wq, wk, wv, wv_t, tabs, *, tm):
    s = h.shape[0]
    c_t, s1_t, s2_t, _ = tabs

    def body(h_ref, gq_ref, gkv_ref, wq_ref, wk_ref, wv_ref, wvt_ref, c_ref, s1_ref, s2_ref,
             q_ref, k_ref, v_ref, vt_ref):
        cq = h_ref[:, 0:Q_RANK]
        ckv = h_ref[:, Q_RANK:Q_RANK + KV_RANK]
        kr = h_ref[:, Q_RANK + KV_RANK:Q_RANK + KV_RANK + QK_PAD]
        c, s1, s2 = c_ref[...], s1_ref[...], s2_ref[...]
        cqn = _mx(_rms(cq)[0] * gq_ref[...])
        ckvn = _mx(_rms(ckv)[0] * gkv_ref[...])
        kr_rot = _rope(kr, c, s1, s2)
        for hd in range(HEADS):
            q_ref[hd] = _rope(_dot(cqn, wq_ref[hd]), c, s1, s2).astype(q_ref.dtype)
            k_ref[hd] = (_dot(ckvn, wk_ref[hd]) + kr_rot).astype(k_ref.dtype)
            v_ref[hd] = _dot(ckvn, wv_ref[hd]).astype(v_ref.dtype)
            vt_ref[hd] = _dot_nt(wvt_ref[hd], ckvn).astype(vt_ref.dtype)

    full = lambda shp: pl.BlockSpec(shp, lambda i: (0,) * len(shp))
    row = lambda w: pl.BlockSpec((tm, w), lambda i: (i, 0))
    return pl.pallas_call(
        body, name="mla_prep_fwd",
        out_shape=(jax.ShapeDtypeStruct((HEADS, s, QK_PAD), MXU_DTYPE),
                   jax.ShapeDtypeStruct((HEADS, s, QK_PAD), MXU_DTYPE),
                   jax.ShapeDtypeStruct((HEADS, s, HEAD_DIM), MXU_DTYPE),
                   jax.ShapeDtypeStruct((HEADS, HEAD_DIM, s), MXU_DTYPE)),
        grid=(s // tm,),
        in_specs=[row(4 * LANES), full((1, Q_RANK)), full((1, KV_RANK)),
                  full((HEADS, Q_RANK, QK_PAD)), full((HEADS, KV_RANK, QK_PAD)), full((HEADS, KV_RANK, HEAD_DIM)),
                  full((HEADS, HEAD_DIM, KV_RANK)), row(LANES), row(LANES), row(LANES)],
        out_specs=(pl.BlockSpec((HEADS, tm, QK_PAD), lambda i: (0, i, 0)),
                   pl.BlockSpec((HEADS, tm, QK_PAD), lambda i: (0, i, 0)),
                   pl.BlockSpec((HEADS, tm, HEAD_DIM), lambda i: (0, i, 0)),
                   pl.BlockSpec((HEADS, HEAD_DIM, tm), lambda i: (0, 0, i))),
        compiler_params=_params("parallel"),
    )(h, g_cq, g_ckv, wq, wk, wv, wv_t, c_t, s1_t, s2_t)


def _mla_prep_bwd(h, dq, dk, dv, g_cq, g_ckv, wq_t, wk_t, wv_t, tabs, *, tm):
    s = h.shape[0]
    c_t, s1_t, s2_t, mask_t = tabs

    def body(h_ref, dq_ref, dk_ref, dv_ref, gq_ref, gkv_ref, wqt_ref, wkt_ref, wvt_ref,
             c_ref, s1_ref, s2_ref, mask_ref, dh_ref, dwq_ref, dwk_ref, dwv_ref, dgq_ref, dgkv_ref):
        i = pl.program_id(0)

        @pl.when(i == 0)
        def _():
            dwq_ref[...] = jnp.zeros_like(dwq_ref)
            dwk_ref[...] = jnp.zeros_like(dwk_ref)
            dwv_ref[...] = jnp.zeros_like(dwv_ref)
            dgq_ref[...] = jnp.zeros_like(dgq_ref)
            dgkv_ref[...] = jnp.zeros_like(dgkv_ref)

        cq = h_ref[:, 0:Q_RANK]
        ckv = h_ref[:, Q_RANK:Q_RANK + KV_RANK]
        c, s1, s2 = c_ref[...], s1_ref[...], s2_ref[...]
        cqh, rq = _rms(cq)
        ckvh, rkv = _rms(ckv)
        gq, gkv = gq_ref[...], gkv_ref[...]
        cqn = _mx(cqh * gq)
        ckvn = _mx(ckvh * gkv)
        dcqn = jnp.zeros((tm, Q_RANK), F32)
        dckvn = jnp.zeros((tm, KV_RANK), F32)
        dkr = jnp.zeros((tm, QK_PAD), F32)
        for hd in range(HEADS):
            dqh = _mx(_unrope(dq_ref[hd], c, s1, s2))
            dcqn = dcqn + _dot(dqh, wqt_ref[hd])
            dwq_ref[hd] += _dot_tn(cqn, dqh)
            dkh = dk_ref[hd]
            dkr = dkr + dkh
            dkh = _mx(dkh)
            dckvn = dckvn + _dot(dkh, wkt_ref[hd])
            dwk_ref[hd] += _dot_tn(ckvn, dkh)
            dvh = _mx(dv_ref[hd])
            dckvn = dckvn + _dot(dvh, wvt_ref[hd])
            dwv_ref[hd] += _dot_tn(ckvn, dvh)
        dgq_ref[...] += jnp.sum(dcqn * cqh, axis=0, keepdims=True)
        dgkv_ref[...] += jnp.sum(dckvn * ckvh, axis=0, keepdims=True)
        gd = dcqn * gq
        dh_ref[:, 0:Q_RANK] = rq * (gd - cqh * jnp.mean(gd * cqh, axis=-1, keepdims=True))
        gd = dckvn * gkv
        dh_ref[:, Q_RANK:Q_RANK + KV_RANK] = rkv * (gd - ckvh * jnp.mean(gd * ckvh, axis=-1, keepdims=True))
        dh_ref[:, Q_RANK + KV_RANK:Q_RANK + KV_RANK + QK_PAD] = _unrope(dkr, c, s1, s2) * mask_ref[...]

    full = lambda shp: pl.BlockSpec(shp, lambda i: (0,) * len(shp))
    row = lambda w: pl.BlockSpec((tm, w), lambda i: (i, 0))
    hrow = lambda w: pl.BlockSpec((HEADS, tm, w), lambda i: (0, i, 0))
    return pl.pallas_call(
        body, name="mla_prep_bwd",
        out_shape=(jax.ShapeDtypeStruct((s, 4 * LANES), F32),
                   jax.ShapeDtypeStruct((HEADS, Q_RANK, QK_PAD), F32),
                   jax.ShapeDtypeStruct((HEADS, KV_RANK, QK_PAD), F32),
                   jax.ShapeDtypeStruct((HEADS, KV_RANK, HEAD_DIM), F32),
                   jax.ShapeDtypeStruct((1, Q_RANK), F32),
                   jax.ShapeDtypeStruct((1, KV_RANK), F32)),
        grid=(s // tm,),
        in_specs=[row(4 * LANES), hrow(QK_PAD), hrow(QK_PAD), hrow(HEAD_DIM),
                  full((1, Q_RANK)), full((1, KV_RANK)),
                  full((HEADS, QK_PAD, Q_RANK)), full((HEADS, QK_PAD, KV_RANK)), full((HEADS, HEAD_DIM, KV_RANK)),
                  row(LANES), row(LANES), row(LANES), row(LANES)],
        out_specs=(row(4 * LANES), full((HEADS, Q_RANK, QK_PAD)), full((HEADS, KV_RANK, QK_PAD)),
                   full((HEADS, KV_RANK, HEAD_DIM)), full((1, Q_RANK)), full((1, KV_RANK))),
        compiler_params=_params("arbitrary"),
    )(h, dq, dk, dv, g_cq, g_ckv, wq_t, wk_t, wv_t, c_t, s1_t, s2_t, mask_t)


def _bdot(a, b, ca, cb):
    return lax.dot_general(a, b, (((ca,), (cb,)), ((0,), (0,))), preferred_element_type=F32)


def _causal_mask_t(t):
    kk = lax.broadcasted_iota(jnp.int32, (t, t), 0)
    qq = lax.broadcasted_iota(jnp.int32, (t, t), 1)
    return (qq >= kk)[None]


def _mla_attn_fwd(q, k, v_t, *, t, g):
    hds, s, _ = q.shape
    n = s // t

    def body(q_ref, k_ref, vt_ref, o_ref, lse_ref, m_sc, l_sc, acc_sc):
        qi, ki = pl.program_id(1), pl.program_id(2)

        @pl.when(ki == 0)
        def _():
            m_sc[...] = jnp.full_like(m_sc, NEG)
            l_sc[...] = jnp.zeros_like(l_sc)
            acc_sc[...] = jnp.zeros_like(acc_sc)

        def step(masked):
            sc = _bdot(k_ref[...], q_ref[...], 2, 2) * MLA_SCALE
            if masked:
                sc = jnp.where(_causal_mask_t(t), sc, NEG)
            m_prev = m_sc[...]
            m_new = jnp.maximum(m_prev, jnp.max(sc, axis=1, keepdims=True))
            p = jnp.exp(sc - m_new)
            a = jnp.exp(m_prev - m_new)
            l_sc[...] = a * l_sc[...] + jnp.sum(p, axis=1, keepdims=True)
            acc_sc[...] = a * acc_sc[...] + _bdot(vt_ref[...], _mx(p), 2, 1)
            m_sc[...] = m_new

        @pl.when(ki < qi)
        def _():
            step(False)

        @pl.when(ki == qi)
        def _():
            step(True)
            o_ref[...] = acc_sc[...] / l_sc[...]
            lse_ref[...] = m_sc[...] + jnp.log(l_sc[...])

    qspec = pl.BlockSpec((g, t, QK_PAD), lambda h, i, j: (h, i, 0))
    kspec = pl.BlockSpec((g, t, QK_PAD), lambda h, i, j: (h, jnp.minimum(i, j), 0))
    vspec = pl.BlockSpec((g, HEAD_DIM, t), lambda h, i, j: (h, 0, jnp.minimum(i, j)))
    return pl.pallas_call(
        body, name="mla_attn_fwd",
        out_shape=(jax.ShapeDtypeStruct((hds, HEAD_DIM, s), F32), jax.ShapeDtypeStruct((hds, 1, s), F32)),
        grid=(hds // g, n, n),
        in_specs=[qspec, kspec, vspec],
        out_specs=(pl.BlockSpec((g, HEAD_DIM, t), lambda h, i, j: (h, 0, i)),
                   pl.BlockSpec((g, 1, t), lambda h, i, j: (h, 0, i))),
        scratch_shapes=[pltpu.VMEM((g, 1, t), F32), pltpu.VMEM((g, 1, t), F32), pltpu.VMEM((g, HEAD_DIM, t), F32)],
        compiler_params=_params("parallel", "parallel", "arbitrary"),
    )(q, k, v_t)


def _head_rowdot(a, b, *, tm):
    s, width = a.shape
    nh = width // HEAD_DIM

    def body(a_ref, b_ref, o_ref):
        prod = a_ref[...] * b_ref[...]
        for hd in range(nh):
            o_ref[:, hd:hd + 1] = jnp.sum(prod[:, hd * HEAD_DIM:(hd + 1) * HEAD_DIM], axis=-1, keepdims=True)

    return pl.pallas_call(
        body, name="head_rowdot",
        out_shape=jax.ShapeDtypeStruct((s, nh), F32),
        grid=(s // tm,),
        in_specs=[pl.BlockSpec((tm, width), lambda i: (i, 0))] * 2,
        out_specs=pl.BlockSpec((tm, nh), lambda i: (i, 0)),
        compiler_params=_params("parallel"),
    )(a, b)


def _mla_attn_bwd(q, k, v, do, lse, dd, *, t, g):
    hds, s, _ = q.shape
    n = s // t

    def body(q_ref, k_ref, v_ref, do_ref, lse_ref, dd_ref, dq_ref, dk_ref, dv_ref, dq_sc, dk_sc, dv_sc):
        ki, qi = pl.program_id(1), pl.program_id(2)

        @pl.when(jnp.logical_and(ki == 0, qi == 0))
        def _():
            dq_sc[...] = jnp.zeros_like(dq_sc)

        @pl.when(qi == 0)
        def _():
            dk_sc[...] = jnp.zeros_like(dk_sc)
            dv_sc[...] = jnp.zeros_like(dv_sc)

        def step(masked):
            qb, kb, dob = q_ref[...], k_ref[...], do_ref[...]
            sc = _bdot(kb, qb, 2, 2) * MLA_SCALE
            if masked:
                sc = jnp.where(_causal_mask_t(t), sc, NEG)
            p = jnp.exp(sc - lse_ref[...])
            dv_sc[...] += _bdot(_mx(p), dob, 2, 1)
            dp = _bdot(v_ref[...], dob, 2, 2)
            ds = _mx(p * (dp - dd_ref[...]) * MLA_SCALE)
            dk_sc[...] += _bdot(ds, qb, 2, 1)
            dq_sc[qi] += _bdot(ds, kb, 1, 1)

        @pl.when(qi == ki)
        def _():
            step(True)

        @pl.when(qi > ki)
        def _():
            step(False)

        @pl.when(qi == n - 1)
        def _():
            dk_ref[...] = dk_sc[...]
            dv_ref[...] = dv_sc[...]

        @pl.when(jnp.logical_and(ki == n - 1, qi == n - 1))
        def _():
            for j in range(n):
                dq_ref[:, j * t:(j + 1) * t, :] = dq_sc[j]

    qs = lambda w: pl.BlockSpec((g, t, w), lambda h, j, i: (h, jnp.maximum(i, j), 0))
    ks = lambda w: pl.BlockSpec((g, t, w), lambda h, j, i: (h, j, 0))
    rowq = pl.BlockSpec((g, 1, t), lambda h, j, i: (h, 0, jnp.maximum(i, j)))
    return pl.pallas_call(
        body, name="mla_attn_bwd",
        out_shape=(jax.ShapeDtypeStruct((hds, s, QK_PAD), F32), jax.ShapeDtypeStruct((hds, s, QK_PAD), F32),
                   jax.ShapeDtypeStruct((hds, s, HEAD_DIM), F32)),
        grid=(hds // g, n, n),
        in_specs=[qs(QK_PAD), ks(QK_PAD), ks(HEAD_DIM), qs(HEAD_DIM), rowq, rowq],
        out_specs=(pl.BlockSpec((g, s, QK_PAD), lambda h, j, i: (h, 0, 0)), ks(QK_PAD), ks(HEAD_DIM)),
        scratch_shapes=[pltpu.VMEM((n, g, t, QK_PAD), F32), pltpu.VMEM((g, t, QK_PAD), F32), pltpu.VMEM((g, t, HEAD_DIM), F32)],
        compiler_params=_params("parallel", "arbitrary", "arbitrary"),
    )(q, k, v, do, lse, dd)


def _perm(a, dil):
    if dil == 1:
        return a
    hds, s, e = a.shape
    return a.reshape(hds, s // dil, dil, e).transpose(0, 2, 1, 3).reshape(hds, s, e)


def _unperm(a, dil):
    if dil == 1:
        return a
    hds, s, e = a.shape
    return a.reshape(hds, dil, s // dil, e).transpose(0, 2, 1, 3).reshape(hds, s, e)


def _dil_bias(dil):
    slopes = 2.0 ** (-8.0 * jnp.arange(1, HEADS + 1, dtype=F32) / HEADS)
    iq = jnp.arange(DIL_BLOCK)[:, None]
    ik = jnp.arange(DIL_BLOCK)[None, :]
    off_c = iq - ik
    off_p = iq - ik + DIL_BLOCK
    b_c = -slopes[:, None, None] * (off_c * dil).astype(F32)[None]
    b_p = -slopes[:, None, None] * (off_p * dil).astype(F32)[None]
    b_c = jnp.where((off_c >= 0)[None], b_c, NEG)
    b_p = jnp.where((off_p <= DIL_BLOCK)[None], b_p, NEG)
    return b_c, b_p


def _dil_fwd(q, k, v, dil, *, name):
    hds, s, e = q.shape
    blk = DIL_BLOCK
    nblk = s // blk
    nb = nblk // dil
    b_c, b_p = _dil_bias(dil)

    def body(q_ref, kc_ref, kp_ref, vc_ref, vp_ref, bc_ref, bp_ref, o_ref, lse_ref):
        b = pl.program_id(0)
        first = (b % nb) == 0
        qb = q_ref[...]
        s_c = _bdot(qb, kc_ref[...], 2, 2) * DIL_SCALE + bc_ref[...]
        s_p = jnp.where(first, NEG, _bdot(qb, kp_ref[...], 2, 2) * DIL_SCALE + bp_ref[...])
        m = jnp.maximum(jnp.max(s_c, axis=-1, keepdims=True), jnp.max(s_p, axis=-1, keepdims=True))
        p_c = jnp.exp(s_c - m)
        p_p = jnp.exp(s_p - m)
        l = jnp.sum(p_c, axis=-1, keepdims=True) + jnp.sum(p_p, axis=-1, keepdims=True)
        o_ref[...] = (_bdot(_mx(p_c), vc_ref[...], 2, 1) + _bdot(_mx(p_p), vp_ref[...], 2, 1)) / l
        lse_ref[...] = m + jnp.log(l)

    cur = lambda w: pl.BlockSpec((hds, blk, w), lambda b: (0, b, 0))
    prev = lambda w: pl.BlockSpec((hds, blk, w), lambda b: (0, jnp.maximum(b - 1, 0), 0))
    bias = pl.BlockSpec((hds, blk, blk), lambda b: (0, 0, 0))
    return pl.pallas_call(
        body, name=name,
        out_shape=(jax.ShapeDtypeStruct((hds, s, e), F32), jax.ShapeDtypeStruct((hds, s, 1), F32)),
        grid=(nblk,),
        in_specs=[cur(e), cur(e), prev(e), cur(e), prev(e), bias, bias],
        out_specs=(cur(e), cur(1)),
        compiler_params=_params("parallel"),
    )(q, k, k, v, v, b_c, b_p)


def _dil_combine(os_, lses, *, ts):
    hds, s, e = os_[0].shape

    def body(o0, o1, o2, l0, l1, l2, o_ref, l_ref):
        a0, a1, a2 = l0[0], l1[0], l2[0]
        m = jnp.maximum(jnp.maximum(a0, a1), a2)
        e0, e1, e2 = jnp.exp(a0 - m), jnp.exp(a1 - m), jnp.exp(a2 - m)
        tot = e0 + e1 + e2
        o_ref[0] = ((e0 / tot) * o0[0] + (e1 / tot) * o1[0]) + (e2 / tot) * o2[0]
        l_ref[0] = m + jnp.log(tot)

    spec = lambda w: pl.BlockSpec((1, ts, w), lambda h, i: (h, i, 0))
    return pl.pallas_call(
        body, name="dil_combine",
        out_shape=(jax.ShapeDtypeStruct((hds, s, e), F32), jax.ShapeDtypeStruct((hds, s, 1), F32)),
        grid=(hds, s // ts),
        in_specs=[spec(e)] * 3 + [spec(1)] * 3,
        out_specs=(spec(e), spec(1)),
        compiler_params=_params("parallel", "parallel"),
    )(*os_, *lses)


def _dil_bwd(q, k, v, do, lj, dd, dil, *, name):
    hds, s, e = q.shape
    blk = DIL_BLOCK
    nblk = s // blk
    nb = nblk // dil
    b_c, b_p = _dil_bias(dil)

    def body(q_ref, qn_ref, kc_ref, kp_ref, vc_ref, vp_ref, do_ref, don_ref, l_ref, ln_ref, d_ref, dn_ref,
             bc_ref, bp_ref, dq_ref, dk_ref, dv_ref):
        b = pl.program_id(0)
        first = (b % nb) == 0
        nxt = jnp.logical_and(b + 1 < nblk, ((b + 1) % nb) != 0)
        qb, kc, kp, vc, vp = q_ref[...], kc_ref[...], kp_ref[...], vc_ref[...], vp_ref[...]
        dob = _mx(do_ref[...])
        bc, bp = bc_ref[...], bp_ref[...]
        p_c = jnp.exp(_bdot(qb, kc, 2, 2) * DIL_SCALE + bc - l_ref[...])
        p_p = jnp.where(first, 0.0, jnp.exp(_bdot(qb, kp, 2, 2) * DIL_SCALE + bp - l_ref[...]))
        ds_c = _mx(p_c * (_bdot(dob, vc, 2, 2) - d_ref[...]) * DIL_SCALE)
        ds_p = _mx(p_p * (_bdot(dob, vp, 2, 2) - d_ref[...]) * DIL_SCALE)
        dq_ref[...] = _bdot(ds_c, kc, 2, 1) + _bdot(ds_p, kp, 2, 1)
        qn = qn_ref[...]
        donb = _mx(don_ref[...])
        p_n = jnp.where(nxt, jnp.exp(_bdot(qn, kc, 2, 2) * DIL_SCALE + bp - ln_ref[...]), 0.0)
        ds_n = _mx(p_n * (_bdot(donb, vc, 2, 2) - dn_ref[...]) * DIL_SCALE)
        dk_ref[...] = _bdot(ds_c, qb, 1, 1) + _bdot(ds_n, qn, 1, 1)
        dv_ref[...] = _bdot(_mx(p_c), dob, 1, 1) + _bdot(_mx(p_n), donb, 1, 1)

    cur = lambda w: pl.BlockSpec((hds, blk, w), lambda b: (0, b, 0))
    prev = lambda w: pl.BlockSpec((hds, blk, w), lambda b: (0, jnp.maximum(b - 1, 0), 0))
    nxt_ = lambda w: pl.BlockSpec((hds, blk, w), lambda b: (0, jnp.minimum(b + 1, nblk - 1), 0))
    bias = pl.BlockSpec((hds, blk, blk), lambda b: (0, 0, 0))
    out = jax.ShapeDtypeStruct((hds, s, e), F32)
    return pl.pallas_call(
        body, name=name,
        out_shape=(out, out, out),
        grid=(nblk,),
        in_specs=[cur(e), nxt_(e), cur(e), prev(e), cur(e), prev(e), cur(e), nxt_(e),
                  cur(1), nxt_(1), cur(1), nxt_(1), bias, bias],
        out_specs=(cur(e), cur(e), cur(e)),
        compiler_params=_params("parallel"),
    )(q, q, k, k, v, v, do, do, lj, lj, dd, dd, b_c, b_p)


def _add3(a, b, c, *, ts, name):
    hds, s, e = a.shape

    def body(a_ref, b_ref, c_ref, o_ref):
        o_ref[...] = (a_ref[...] + b_ref[...]) + c_ref[...]

    spec = pl.BlockSpec((1, ts, e), lambda h, i: (h, i, 0))
    return pl.pallas_call(
        body, name=name,
        out_shape=jax.ShapeDtypeStruct((hds, s, e), F32),
        grid=(hds, s // ts),
        in_specs=[spec] * 3, out_specs=spec,
        compiler_params=_params("parallel", "parallel"),
    )(a, b, c)


def _ln_fwd(z, g, b):
    mu = jnp.mean(z, axis=-1, keepdims=True)
    zc = z - mu
    var = jnp.mean(zc * zc, axis=-1, keepdims=True)
    rstd = lax.rsqrt(var + LN_EPS)
    xhat = zc * rstd
    return xhat * g + b, xhat, rstd


def _ln_bwd(dy, xhat, rstd, g):
    dxh = dy * g
    return rstd * (dxh - jnp.mean(dxh, axis=-1, keepdims=True) - xhat * jnp.mean(dxh * xhat, axis=-1, keepdims=True))


def _out_ln1(attn, w_o, x, g, b, *, tm):
    s = x.shape[0]

    def body(a_ref, w_ref, x_ref, g_ref, b_ref, x1_ref, xh_ref, r_ref):
        z = DN_ALPHA * x_ref[...] + _dot(a_ref[...], w_ref[...])
        y, xhat, rstd = _ln_fwd(z, g_ref[...], b_ref[...])
        x1_ref[...] = y
        xh_ref[...] = xhat
        r_ref[...] = rstd

    row = lambda w: pl.BlockSpec((tm, w), lambda i: (i, 0))
    full = lambda shp: pl.BlockSpec(shp, lambda i: (0,) * len(shp))
    act = jax.ShapeDtypeStruct((s, D_MODEL), F32)
    return pl.pallas_call(
        body, name="out_ln1",
        out_shape=(act, act, jax.ShapeDtypeStruct((s, 1), F32)),
        grid=(s // tm,),
        in_specs=[row(D_MODEL), full((D_MODEL, D_MODEL)), row(D_MODEL), full((1, D_MODEL)), full((1, D_MODEL))],
        out_specs=(row(D_MODEL), row(D_MODEL), row(1)),
        compiler_params=_params("parallel"),
    )(attn, w_o, x, g, b)


def _down_ln2_loss(act, w_down, x1, g, b, target, *, tm):
    s = x1.shape[0]

    def body(a_ref, w_ref, x1_ref, g_ref, b_ref, t_ref, dz_ref, loss_ref, dg_ref, db_ref):
        i = pl.program_id(0)

        @pl.when(i == 0)
        def _():
            loss_ref[...] = jnp.zeros_like(loss_ref)
            dg_ref[...] = jnp.zeros_like(dg_ref)
            db_ref[...] = jnp.zeros_like(db_ref)

        gam = g_ref[...]
        z = DN_ALPHA * x1_ref[...] + _dot(a_ref[...], w_ref[...])
        y, xhat, rstd = _ln_fwd(z, gam, b_ref[...])
        err = y - t_ref[...]
        loss_ref[...] += 0.5 * jnp.sum(jnp.mean(err * err, axis=-1, keepdims=True))
        dy = err * (1.0 / D_MODEL)
        dg_ref[...] += jnp.sum(dy * xhat, axis=0, keepdims=True)
        db_ref[...] += jnp.sum(dy, axis=0, keepdims=True)
        dz_ref[...] = _ln_bwd(dy, xhat, rstd, gam)

    row = lambda w: pl.BlockSpec((tm, w), lambda i: (i, 0))
    full = lambda shp: pl.BlockSpec(shp, lambda i: (0,) * len(shp))
    vec = jax.ShapeDtypeStruct((1, D_MODEL), F32)
    return pl.pallas_call(
        body, name="down_ln2_loss",
        out_shape=(jax.ShapeDtypeStruct((s, D_MODEL), F32), jax.ShapeDtypeStruct((1, LANES), F32), vec, vec),
        grid=(s // tm,),
        in_specs=[row(D_FF), full((D_FF, D_MODEL)), row(D_MODEL), full((1, D_MODEL)), full((1, D_MODEL)), row(D_MODEL)],
        out_specs=(row(D_MODEL), full((1, LANES)), full((1, D_MODEL)), full((1, D_MODEL))),
        compiler_params=_params("arbitrary"),
    )(act, w_down, x1, g, b, target)


def _up_bwd_ln1(du_a, du_g, w_up_t, dz2, xhat1, rstd1, g, *, tm):
    s = dz2.shape[0]

    def body(dua_ref, dug_ref, wa_ref, wg_ref, dz2_ref, xh_ref, r_ref, g_ref, dz1_ref, dg_ref, db_ref):
        i = pl.program_id(0)

        @pl.when(i == 0)
        def _():
            dg_ref[...] = jnp.zeros_like(dg_ref)
            db_ref[...] = jnp.zeros_like(db_ref)

        dx1 = DN_ALPHA * dz2_ref[...] + (_dot(dua_ref[...], wa_ref[...]) + _dot(dug_ref[...], wg_ref[...]))
        xhat = xh_ref[...]
        dg_ref[...] += jnp.sum(dx1 * xhat, axis=0, keepdims=True)
        db_ref[...] += jnp.sum(dx1, axis=0, keepdims=True)
        dz1_ref[...] = _ln_bwd(dx1, xhat, r_ref[...], g_ref[...])

    row = lambda w: pl.BlockSpec((tm, w), lambda i: (i, 0))
    full = lambda shp: pl.BlockSpec(shp, lambda i: (0,) * len(shp))
    vec = jax.ShapeDtypeStruct((1, D_MODEL), F32)
    return pl.pallas_call(
        body, name="up_bwd_ln1",
        out_shape=(jax.ShapeDtypeStruct((s, D_MODEL), F32), vec, vec),
        grid=(s // tm,),
        in_specs=[row(D_FF), row(D_FF),
                  pl.BlockSpec((D_FF, D_MODEL), lambda i: (0, 0)), pl.BlockSpec((D_FF, D_MODEL), lambda i: (1, 0)),
                  row(D_MODEL), row(D_MODEL), row(1), full((1, D_MODEL))],
        out_specs=(row(D_MODEL), full((1, D_MODEL)), full((1, D_MODEL))),
        compiler_params=_params("arbitrary"),
    )(du_a, du_g, w_up_t, w_up_t, dz2, xhat1, rstd1, g)


GELU_C = math.sqrt(2.0 / math.pi)


def _gelu(x):
    cdf = 0.5 * (1.0 + jnp.tanh(GELU_C * (x + 0.044715 * (x * x * x))))
    return x * cdf


def _gelu_grad(x):
    t = jnp.tanh(GELU_C * (x + 0.044715 * (x * x * x)))
    return 0.5 * (1.0 + t) + 0.5 * x * (1.0 - t * t) * (GELU_C * (1.0 + 3.0 * 0.044715 * (x * x)))


def _shift_down(u, halo):
    t = u.shape[0]
    row = lax.broadcasted_iota(jnp.int32, u.shape, 0)
    h7, h6 = halo[7:8, :], halo[6:7, :]
    s1 = jnp.where(row == 0, h7, pltpu.roll(u, 1, 0))
    s2 = jnp.where(row == 0, h6, jnp.where(row == 1, h7, pltpu.roll(u, 2, 0)))
    return s1, s2


def _shift_up(d, nxt):
    t = d.shape[0]
    row = lax.broadcasted_iota(jnp.int32, d.shape, 0)
    n0, n1 = nxt[0:1, :], nxt[1:2, :]
    s1 = jnp.where(row == t - 1, n0, pltpu.roll(d, t - 1, 0))
    s2 = jnp.where(row == t - 1, n1, jnp.where(row == t - 2, n0, pltpu.roll(d, t - 2, 0)))
    return s1, s2


def _conv(u, s1, s2, w, b):
    return ((b + w[0:1, :] * s2) + w[1:2, :] * s1) + w[2:3, :] * u


def _gate_fwd(u, conv_w, conv_b, *, tm, tn):
    s = u.shape[0]
    nj = D_FF // tn
    hb = tm // SUBLANES

    def body(ua_ref, ug_ref, ha_ref, hg_ref, wa_ref, wg_ref, ba_ref, bg_ref, o_ref):
        keep = pl.program_id(0) > 0
        ua, ug = ua_ref[...], ug_ref[...]
        ha = jnp.where(keep, ha_ref[...], 0.0)
        hg = jnp.where(keep, hg_ref[...], 0.0)
        a = _conv(ua, *_shift_down(ua, ha), wa_ref[...], ba_ref[...])
        g = _conv(ug, *_shift_down(ug, hg), wg_ref[...], bg_ref[...])
        o_ref[...] = (_gelu(g) * a).astype(o_ref.dtype)

    main = lambda off: pl.BlockSpec((tm, tn), lambda i, j: (i, j + off))
    halo = lambda off: pl.BlockSpec((SUBLANES, tn), lambda i, j: (jnp.maximum(i * hb - 1, 0), j + off))
    wspec = lambda r, off: pl.BlockSpec((r, tn), lambda i, j: (0, j + off))
    return pl.pallas_call(
        body, name="gate_fwd",
        out_shape=jax.ShapeDtypeStruct((s, D_FF), MXU_DTYPE),
        grid=(s // tm, nj),
        in_specs=[main(0), main(nj), halo(0), halo(nj), wspec(3, 0), wspec(3, nj), wspec(1, 0), wspec(1, nj)],
        out_specs=pl.BlockSpec((tm, tn), lambda i, j: (i, j)),
        compiler_params=_params("parallel", "parallel"),
    )(u, u, u, u, conv_w, conv_w, conv_b, conv_b)


def _gate_bwd(u, dact, conv_w, conv_b, *, tm, tn):
    s = u.shape[0]
    nj = D_FF // tn
    ni = s // tm
    hb = tm // SUBLANES

    def body(ua_ref, ug_ref, ha_ref, hg_ref, na_ref, ng_ref, d_ref, dn_ref, wa_ref, wg_ref, ba_ref, bg_ref,
             dua_ref, dug_ref, dwa_ref, dwg_ref, dba_ref, dbg_ref):
        i = pl.program_id(1)

        @pl.when(i == 0)
        def _():
            for r in (dwa_ref, dwg_ref, dba_ref, dbg_ref):
                r[...] = jnp.zeros_like(r)

        wa, wg, ba, bg = wa_ref[...], wg_ref[...], ba_ref[...], bg_ref[...]
        ua, ug = ua_ref[...], ug_ref[...]
        ha = jnp.where(i > 0, ha_ref[...], 0.0)
        hg = jnp.where(i > 0, hg_ref[...], 0.0)
        sa1, sa2 = _shift_down(ua, ha)
        sg1, sg2 = _shift_down(ug, hg)
        a = _conv(ua, sa1, sa2, wa, ba)
        g = _conv(ug, sg1, sg2, wg, bg)
        d = d_ref[...]
        dya = d * _gelu(g)
        dyg = d * a * _gelu_grad(g)
        na, ng = na_ref[...], ng_ref[...]
        a_n = _conv(na, *_shift_down(na, ua[tm - SUBLANES:, :]), wa, ba)
        g_n = _conv(ng, *_shift_down(ng, ug[tm - SUBLANES:, :]), wg, bg)
        dn = jnp.where(i < ni - 1, dn_ref[...], 0.0)
        dya_n = dn * _gelu(g_n)
        dyg_n = dn * a_n * _gelu_grad(g_n)
        da1, da2 = _shift_up(dya, dya_n)
        dg1, dg2 = _shift_up(dyg, dyg_n)
        dua_ref[...] = (wa[2:3, :] * dya + wa[1:2, :] * da1 + wa[0:1, :] * da2).astype(dua_ref.dtype)
        dug_ref[...] = (wg[2:3, :] * dyg + wg[1:2, :] * dg1 + wg[0:1, :] * dg2).astype(dug_ref.dtype)
        ssum = lambda v: jnp.sum(v, axis=0, keepdims=True)
        dwa_ref[...] += jnp.concatenate([ssum(dya * sa2), ssum(dya * sa1), ssum(dya * ua)], axis=0)
        dwg_ref[...] += jnp.concatenate([ssum(dyg * sg2), ssum(dyg * sg1), ssum(dyg * ug)], axis=0)
        dba_ref[...] += ssum(dya)
        dbg_ref[...] += ssum(dyg)

    main = lambda off: pl.BlockSpec((tm, tn), lambda j, i: (i, j + off))
    halo = lambda off: pl.BlockSpec((SUBLANES, tn), lambda j, i: (jnp.maximum(i * hb - 1, 0), j + off))
    nxt = lambda off: pl.BlockSpec((SUBLANES, tn), lambda j, i: (jnp.minimum((i + 1) * hb, s // SUBLANES - 1), j + off))
    wspec = lambda r, off: pl.BlockSpec((r, tn), lambda j, i: (0, j + off))
    return pl.pallas_call(
        body, name="gate_bwd",
        out_shape=(jax.ShapeDtypeStruct((s, D_FF), MXU_DTYPE), jax.ShapeDtypeStruct((s, D_FF), MXU_DTYPE),
                   jax.ShapeDtypeStruct((3, D_FF), F32), jax.ShapeDtypeStruct((3, D_FF), F32),
                   jax.ShapeDtypeStruct((1, D_FF), F32), jax.ShapeDtypeStruct((1, D_FF), F32)),
        grid=(nj, ni),
        in_specs=[main(0), main(nj), halo(0), halo(nj), nxt(0), nxt(nj), main(0), nxt(0),
                  wspec(3, 0), wspec(3, nj), wspec(1, 0), wspec(1, nj)],
        out_specs=(main(0), main(0), wspec(3, 0), wspec(3, 0), wspec(1, 0), wspec(1, 0)),
        compiler_params=_params("parallel", "arbitrary"),
    )(u, u, u, u, u, u, dact, dact, conv_w, conv_w, conv_b, conv_b)


def _prep_weights(w_in, w_uq, w_uk, w_uv, w_o, w_up, w_down):
    c = lambda a: a.astype(MXU_DTYPE)
    w_in = c(w_in)
    z = lambda w: jnp.zeros((D_MODEL, w), MXU_DTYPE)
    r0 = Q_RANK + KV_RANK
    w_in_ext = jnp.concatenate([w_in[:, :r0], z(NOPE), w_in[:, r0:r0 + ROPE], z(32), w_in[:, r0 + ROPE:]], axis=1)
    wq = jnp.pad(c(w_uq).transpose(1, 0, 2), ((0, 0), (0, 0), (0, QK_PAD - NOPE - ROPE)))
    wk = jnp.pad(c(w_uk).transpose(1, 0, 2), ((0, 0), (0, 0), (0, QK_PAD - NOPE)))
    wv = c(w_uv).transpose(1, 0, 2)
    t3 = lambda a: a.transpose(0, 2, 1)
    w_o, w_up, w_down = c(w_o), c(w_up), c(w_down)
    return dict(w_in=w_in_ext, w_in_t=w_in_ext.T, wq=wq, wq_t=t3(wq), wk=wk, wk_t=t3(wk), wv=wv, wv_t=t3(wv),
                w_o=w_o, w_o_t=w_o.T, w_up=w_up, w_up_t=w_up.T, w_down=w_down, w_down_t=w_down.T)


def _local_step(x, target, w, g_cq, g_ckv, ln1_g, ln1_b, conv_w, conv_b, ln2_g, ln2_b):
    s = x.shape[0]
    tabs = _rope_tables(s)
    r2 = lambda a: a.reshape(1, -1)
    heads = lambda a: a.reshape(s, HEADS, HEAD_DIM).transpose(1, 0, 2)
    unheads = lambda a: a.transpose(1, 0, 2).reshape(s, HEADS * HEAD_DIM)
    cb = r2(conv_b)
    dils = [d for _, d in DIL_PAIRS]

    h = _mm_nn(x, w["w_in"], name="in_proj", tm=512, tn=1024, tk=D_MODEL)
    q, k, v, v_t = _mla_prep_fwd(h, r2(g_cq), r2(g_ckv), w["wq"], w["wk"], w["wv"], w["wv_t"], tabs, tm=256)
    o_mla_t, lse_mla = _mla_attn_fwd(q, k, v_t, t=512, g=HEADS)
    qd, kd, vd = (_mx(heads(h[:, 512 * (i + 1):512 * (i + 2)])) for i in range(3))
    qp = [_perm(qd, d) for d in dils]
    kp = [_perm(kd, d) for d in dils]
    vp = [_perm(vd, d) for d in dils]
    o_bs, lse_bs = [], []
    for i, d in enumerate(dils):
        o_b, l_b = _dil_fwd(qp[i], kp[i], vp[i], d, name=f"dil_fwd_{d}")
        o_bs.append(_unperm(o_b, d))
        lse_bs.append(_unperm(l_b, d))
    o_dil, lj = _dil_combine(o_bs, lse_bs, ts=512)
    attn_f = jnp.concatenate([o_mla_t.transpose(2, 0, 1).reshape(s, HEADS * HEAD_DIM), unheads(o_dil)], axis=1)
    attn = _mx(attn_f)
    x1, xhat1, rstd1 = _out_ln1(attn, w["w_o"], x, r2(ln1_g), r2(ln1_b), tm=256)
    u = _mm_nn(x1, w["w_up"], name="up_proj", tm=512, tn=1408, tk=D_MODEL)
    act = _gate_fwd(u, conv_w, cb, tm=256, tn=1408)
    dz2, loss, dg2, db2 = _down_ln2_loss(act, w["w_down"], x1, r2(ln2_g), r2(ln2_b), target, tm=256)

    dact = _mm_nn(dz2, w["w_down_t"], name="down_bwd", tm=512, tn=1408, tk=D_MODEL)
    dw_down = _mm_tn(act, dz2, name="dw_down", tm=1408, tn=D_MODEL, ts=512)
    du_a, du_g, dcw_a, dcw_g, dcb_a, dcb_g = _gate_bwd(u, dact, conv_w, cb, tm=256, tn=1408)
    dz1, dg1, db1 = _up_bwd_ln1(du_a, du_g, w["w_up_t"], dz2, xhat1, rstd1, r2(ln1_g), tm=256)
    dw_up = jnp.concatenate([_mm_tn(x1, du_a, name="dw_up_a", tm=D_MODEL, tn=1408, ts=512),
                             _mm_tn(x1, du_g, name="dw_up_g", tm=D_MODEL, tn=1408, ts=512)], axis=1)
    dattn = _mm_nn(dz1, w["w_o_t"], name="o_bwd", tm=512, tn=D_MODEL, tk=D_MODEL)
    dw_o = _mm_tn(attn, dz1, name="dw_o", tm=D_MODEL, tn=D_MODEL, ts=512)
    do_mla, do_dil = _mx(heads(dattn[:, :512])), heads(dattn[:, 512:])
    dd_all = _head_rowdot(dattn, attn_f, tm=256).T
    dd_mla, dd_dil = dd_all[:HEADS].reshape(HEADS, 1, s), dd_all[HEADS:].reshape(HEADS, s, 1)
    dq, dk, dv = _mla_attn_bwd(q, k, v, do_mla, lse_mla, dd_mla, t=512, g=4)
    parts = []
    for i, d in enumerate(dils):
        g3 = _dil_bwd(qp[i], kp[i], vp[i], _perm(do_dil, d), _perm(lj, d), _perm(dd_dil, d), d, name=f"dil_bwd_{d}")
        parts.append([_unperm(g, d) for g in g3])
    dqd, dkd, dvd = (_add3(parts[0][j], parts[1][j], parts[2][j], ts=512, name=f"dil_sum_{j}") for j in range(3))
    dh_mla, dwq, dwk, dwv, dgq, dgkv = _mla_prep_bwd(h, dq, dk, dv, r2(g_cq), r2(g_ckv),
                                                     w["wq_t"], w["wk_t"], w["wv_t"], tabs, tm=256)
    dh = _mx(jnp.concatenate([dh_mla, unheads(dqd), unheads(dkd), unheads(dvd)], axis=1))
    grad_x = _mm_nn(dh, w["w_in_t"], name="in_bwd", tm=512, tn=D_MODEL, tk=D_MODEL, add=dz1, add_scale=DN_ALPHA)
    dw_ext = _mm_tn(x, dh, name="dw_in", tm=D_MODEL, tn=1024, ts=512)
    r0 = Q_RANK + KV_RANK
    grads = dict(
        w_in=jnp.concatenate([dw_ext[:, :r0], dw_ext[:, r0 + NOPE:r0 + NOPE + ROPE], dw_ext[:, 512:]], axis=1),
        g_cq=dgq[0], g_ckv=dgkv[0],
        w_uq=dwq[:, :, :NOPE + ROPE].transpose(1, 0, 2),
        w_uk=dwk[:, :, :NOPE].transpose(1, 0, 2),
        w_uv=dwv.transpose(1, 0, 2),
        w_o=dw_o, ln1_g=dg1[0], ln1_b=db1[0], w_up=dw_up,
        conv_w=jnp.concatenate([dcw_a, dcw_g], axis=1), conv_b=jnp.concatenate([dcb_a, dcb_g], axis=1)[0],
        w_down=dw_down, ln2_g=dg2[0], ln2_b=db2[0])
    return loss[0, 0], grad_x, grads


N_CHIPS = 4
SHARDED = ("w_in", "w_uq", "w_o", "w_up", "conv_w", "w_down")
COL_SHARDED = ("w_in", "w_up", "conv_w")
SHARD_SHAPE = dict(w_in=(D_MODEL, IN_WIDTH // 4), w_uq=(Q_RANK // 4, HEADS, NOPE + ROPE), w_o=(D_MODEL // 4, D_MODEL),
                   w_up=(D_MODEL, 2 * D_FF // 4), conv_w=(3, 2 * D_FF // 4), w_down=(D_FF // 4, D_MODEL))
SMALL = ("g_cq", "g_ckv", "w_uk", "w_uv", "ln1_g", "ln1_b", "conv_b", "ln2_g", "ln2_b")
SMALL_SHAPE = dict(g_cq=(Q_RANK,), g_ckv=(KV_RANK,), w_uk=(KV_RANK, HEADS, NOPE), w_uv=(KV_RANK, HEADS, HEAD_DIM),
                   ln1_g=(D_MODEL,), ln1_b=(D_MODEL,), conv_b=(2 * D_FF,), ln2_g=(D_MODEL,), ln2_b=(D_MODEL,))
ROW_TILE = 512


def _size(shape):
    return math.prod(shape)


def _padded_rows(n_elems, mult):
    return -(-n_elems // (LANES * mult)) * mult


SHARD_ROWS = {n: _padded_rows(_size(SHARD_SHAPE[n]), SUBLANES) for n in SHARDED}
R_SHARD = -(-sum(SHARD_ROWS.values()) // (2 * ROW_TILE)) * (2 * ROW_TILE)
R_HALF = R_SHARD // 2
R_SMALL = -(-sum(_size(SMALL_SHAPE[n]) for n in SMALL) // (LANES * LANES)) * LANES
GATHERED = ("w_in", "w_uq", "w_o", "w_up", "w_down")
R_GATHER = sum(SHARD_ROWS[n] for n in GATHERED)


def _rows(a, rows=None):
    flat = a.reshape(-1)
    rows = -(-flat.shape[0] // LANES) if rows is None else rows
    return jnp.pad(flat, (0, rows * LANES - flat.shape[0])).reshape(rows, LANES)


def _pack_shard(t):
    parts = [_rows(t[n].astype(F32), SHARD_ROWS[n]) for n in SHARDED]
    used = sum(SHARD_ROWS.values())
    return jnp.concatenate(parts + [jnp.zeros((R_SHARD - used, LANES), F32)], axis=0)


def _unpack_shard(buf):
    out, r = {}, 0
    for n in SHARDED:
        out[n] = buf[r:r + SHARD_ROWS[n]].reshape(-1)[:_size(SHARD_SHAPE[n])].reshape(SHARD_SHAPE[n])
        r += SHARD_ROWS[n]
    return out


def _chip_blocks(name, full):
    shp = SHARD_SHAPE[name]
    if name in COL_SHARDED:
        a = full.reshape(shp[0], N_CHIPS, shp[1]).transpose(1, 0, 2)
    else:
        a = full.reshape((N_CHIPS,) + shp)
    a = a.reshape(N_CHIPS, -1)
    rows = SHARD_ROWS[name]
    return jnp.pad(a, ((0, 0), (0, rows * LANES - a.shape[1]))).reshape(N_CHIPS, rows, LANES)


def _from_chip_blocks(name, blocks):
    shp = SHARD_SHAPE[name]
    a = blocks.reshape(N_CHIPS, -1)[:, :_size(shp)].reshape((N_CHIPS,) + shp)
    if name in COL_SHARDED:
        return a.transpose(1, 0, 2).reshape(shp[0], N_CHIPS * shp[1])
    return a.reshape((N_CHIPS * shp[0],) + shp[1:])


def _pack_grads(g):
    parts = [_chip_blocks(n, g[n]) for n in SHARDED]
    used = sum(SHARD_ROWS.values())
    return jnp.concatenate(parts + [jnp.zeros((N_CHIPS, R_SHARD - used, LANES), F32)], axis=1)


def _pack_small(t):
    flat = jnp.concatenate([t[n].astype(F32).reshape(-1) for n in SMALL])
    return _rows(flat, R_SMALL)


def _unpack_small(buf):
    flat, out, r = buf.reshape(-1), {}, 0
    for n in SMALL:
        out[n] = flat[r:r + _size(SMALL_SHAPE[n])].reshape(SMALL_SHAPE[n])
        r += _size(SMALL_SHAPE[n])
    return out


ANY = pl.BlockSpec(memory_space=pl.ANY)
COMM_PARAMS = pltpu.CompilerParams(has_side_effects=True)


def _coords():
    return lax.axis_index("x"), lax.axis_index("y"), lax.axis_index("c")


def _other_chips(x, y):
    return [(1 - x, y), (x, 1 - y), (1 - x, 1 - y)]


def _remote(src, dst, send_sems, recv_sems, k, to):
    return pltpu.make_async_remote_copy(src_ref=src, dst_ref=dst, send_sem=send_sems.at[k], recv_sem=recv_sems.at[k],
                                        device_id=to, device_id_type=MESH)


def _gather_weights(wp, cwp):
    def body(wp_ref, cw_ref, wout_ref, cwout_ref, send_sems, recv_sems, loc_sems):
        x, y, c = _coords()
        me = 2 * x + y
        sib = (x, y, 1 - c)
        chips = _other_chips(x, y)
        loc = [pltpu.make_async_copy(wp_ref, wout_ref.at[me], loc_sems.at[0]),
               pltpu.make_async_copy(cw_ref, cwout_ref.at[me], loc_sems.at[1])]
        for cp in loc:
            cp.start()
        sends = [_remote(wp_ref.at[c], wout_ref.at[me, c], send_sems, recv_sems, j, (px, py, c))
                 for j, (px, py) in enumerate(chips)]
        sends += [_remote(cw_ref, cwout_ref.at[me], send_sems, recv_sems, 3 + j, (px, py, c))
                  for j, (px, py) in enumerate(chips)]
        for cp in sends:
            cp.start()
        for j, (px, py) in enumerate(chips):
            k = 2 * px + py
            _remote(wp_ref.at[c], wout_ref.at[k, c], send_sems, recv_sems, j, (px, py, c)).wait_recv()
            fwd = _remote(wout_ref.at[k, c], wout_ref.at[k, c], send_sems, recv_sems, 6 + j, sib)
            fwd.start()
            sends.append(fwd)
        for j, (px, py) in enumerate(chips):
            k = 2 * px + py
            _remote(cw_ref, cwout_ref.at[k], send_sems, recv_sems, 3 + j, (px, py, c)).wait_recv()
            _remote(wout_ref.at[k, 1 - c], wout_ref.at[k, 1 - c], send_sems, recv_sems, 6 + j, sib).wait_recv()
        for cp in sends:
            cp.wait_send()
        for cp in loc:
            cp.wait()

    return pl.pallas_call(
        body, name="gather_weights",
        out_shape=(jax.ShapeDtypeStruct((N_CHIPS,) + wp.shape, wp.dtype), jax.ShapeDtypeStruct((N_CHIPS,) + cwp.shape, cwp.dtype)),
        in_specs=[ANY, ANY], out_specs=(ANY, ANY),
        scratch_shapes=[pltpu.SemaphoreType.DMA((9,)), pltpu.SemaphoreType.DMA((9,)), pltpu.SemaphoreType.DMA((2,))],
        compiler_params=COMM_PARAMS,
    )(wp, cwp)


def _exchange_sibling_halves(gs, gr):
    def body(gs_ref, gr_ref, os_ref, or_ref, send_sems, recv_sems):
        x, y, c = _coords()
        sib = (x, y, 1 - c)
        cps = [_remote(gs_ref.at[k, 1 - c], os_ref.at[k], send_sems, recv_sems, k, sib) for k in range(N_CHIPS)]
        cps.append(_remote(gr_ref, or_ref, send_sems, recv_sems, N_CHIPS, sib))
        for cp in cps:
            cp.start()
        for cp in cps:
            cp.wait_recv()
        for cp in cps:
            cp.wait_send()

    return pl.pallas_call(
        body, name="exchange_sibling_halves",
        out_shape=(jax.ShapeDtypeStruct((N_CHIPS, R_HALF, LANES), F32), jax.ShapeDtypeStruct(gr.shape, F32)),
        in_specs=[ANY, ANY], out_specs=(ANY, ANY),
        scratch_shapes=[pltpu.SemaphoreType.DMA((N_CHIPS + 1,)), pltpu.SemaphoreType.DMA((N_CHIPS + 1,))],
        compiler_params=COMM_PARAMS,
    )(gs, gr)


def _exchange_chips(ps, pr):
    def body(ps_ref, pr_ref, ss_ref, sr_ref, send_sems, recv_sems, loc_sems):
        x, y, c = _coords()
        me = 2 * x + y
        chips = _other_chips(x, y)
        loc = [pltpu.make_async_copy(ps_ref.at[me], ss_ref.at[me], loc_sems.at[0]),
               pltpu.make_async_copy(pr_ref, sr_ref.at[me], loc_sems.at[1])]
        for cp in loc:
            cp.start()
        sends = []
        for j, (px, py) in enumerate(chips):
            sends.append(_remote(ps_ref.at[2 * px + py], ss_ref.at[me], send_sems, recv_sems, 2 * j, (px, py, c)))
            sends.append(_remote(pr_ref, sr_ref.at[me], send_sems, recv_sems, 2 * j + 1, (px, py, c)))
        for cp in sends:
            cp.start()
        for j, (px, py) in enumerate(chips):
            k = 2 * px + py
            _remote(ps_ref.at[me], ss_ref.at[k], send_sems, recv_sems, 2 * j, (px, py, c)).wait_recv()
            _remote(pr_ref, sr_ref.at[k], send_sems, recv_sems, 2 * j + 1, (px, py, c)).wait_recv()
        for cp in sends:
            cp.wait_send()
        for cp in loc:
            cp.wait()

    return pl.pallas_call(
        body, name="exchange_chips",
        out_shape=(jax.ShapeDtypeStruct(ps.shape, ps.dtype), jax.ShapeDtypeStruct((N_CHIPS,) + pr.shape, F32)),
        in_specs=[ANY, ANY], out_specs=(ANY, ANY),
        scratch_shapes=[pltpu.SemaphoreType.DMA((6,)), pltpu.SemaphoreType.DMA((6,)), pltpu.SemaphoreType.DMA((2,))],
        compiler_params=COMM_PARAMS,
    )(ps, pr)


def _exchange_sibling_result(gh):
    def body(gh_ref, out_ref, send_sems, recv_sems, loc_sem):
        x, y, c = _coords()
        sib = (x, y, 1 - c)
        loc = pltpu.make_async_copy(gh_ref, out_ref.at[c], loc_sem.at[0])
        loc.start()
        cp = _remote(gh_ref, out_ref.at[c], send_sems, recv_sems, 0, sib)
        cp.start()
        _remote(gh_ref, out_ref.at[1 - c], send_sems, recv_sems, 0, sib).wait_recv()
        cp.wait_send()
        loc.wait()

    return pl.pallas_call(
        body, name="exchange_sibling_result",
        out_shape=jax.ShapeDtypeStruct((2,) + gh.shape, F32),
        in_specs=[ANY], out_specs=ANY,
        scratch_shapes=[pltpu.SemaphoreType.DMA((1,)), pltpu.SemaphoreType.DMA((1,)), pltpu.SemaphoreType.DMA((1,))],
        compiler_params=COMM_PARAMS,
    )(gh)


def _add_own_half(gs, recv, c_arr):
    def body(c_ref, a_ref, b_ref, o_ref):
        o_ref[0] = (a_ref[0, 0] + b_ref[0]).astype(o_ref.dtype)

    return pl.pallas_call(
        body, name="add_own_half",
        out_shape=jax.ShapeDtypeStruct(recv.shape, GRAD_WIRE_DTYPE),
        grid_spec=pltpu.PrefetchScalarGridSpec(
            num_scalar_prefetch=1, grid=(N_CHIPS, R_HALF // ROW_TILE),
            in_specs=[pl.BlockSpec((1, 1, ROW_TILE, LANES), lambda k, i, c_ref: (k, c_ref[0], i, 0)),
                      pl.BlockSpec((1, ROW_TILE, LANES), lambda k, i, c_ref: (k, i, 0))],
            out_specs=pl.BlockSpec((1, ROW_TILE, LANES), lambda k, i, c_ref: (k, i, 0))),
        compiler_params=_params("parallel", "parallel"),
    )(c_arr, gs, recv)


def _add2(a, b, *, name):
    def body(a_ref, b_ref, o_ref):
        o_ref[...] = a_ref[...] + b_ref[...]

    return pl.pallas_call(body, name=name, out_shape=jax.ShapeDtypeStruct(a.shape, F32))(a, b)


def _sum_slots(slots, *, tr, name):
    r = slots.shape[1]

    def body(s_ref, o_ref):
        f = lambda k: s_ref[k].astype(F32)
        o_ref[...] = ((f(0) + f(1)) + f(2)) + f(3)

    return pl.pallas_call(
        body, name=name,
        out_shape=jax.ShapeDtypeStruct((r, LANES), F32),
        grid=(r // tr,),
        in_specs=[pl.BlockSpec((N_CHIPS, tr, LANES), lambda i: (0, i, 0))],
        out_specs=pl.BlockSpec((tr, LANES), lambda i: (i, 0)),
        compiler_params=_params("parallel"),
    )(slots)


def _adamw(w, g, m, v, *, tr, name):
    r = w.shape[0]

    def body(w_ref, g_ref, m_ref, v_ref, d_ref, nm_ref, nv_ref):
        g_ = g_ref[...]
        m_ = ADAM_B1 * m_ref[...] + (1.0 - ADAM_B1) * g_
        v_ = ADAM_B2 * v_ref[...] + (1.0 - ADAM_B2) * (g_ * g_)
        m_hat = m_ / (1.0 - ADAM_B1 ** ADAM_STEP)
        v_hat = v_ / (1.0 - ADAM_B2 ** ADAM_STEP)
        d_ref[...] = -ADAM_LR * (m_hat / (jnp.sqrt(v_hat) + ADAM_EPS) + ADAM_WD * w_ref[...])
        nm_ref[...] = m_
        nv_ref[...] = v_

    spec = pl.BlockSpec((tr, LANES), lambda i: (i, 0))
    out = jax.ShapeDtypeStruct((r, LANES), F32)
    return pl.pallas_call(
        body, name=name, out_shape=(out, out, out), grid=(r // tr,),
        in_specs=[spec] * 4, out_specs=(spec,) * 3,
        compiler_params=_params("parallel"),
    )(w, g, m, v)


WEIGHTS = ("w_in", "g_cq", "g_ckv", "w_uq", "w_uk", "w_uv", "w_o", "ln1_g", "ln1_b", "w_up", "conv_w", "conv_b",
           "w_down", "ln2_g", "ln2_b")


def kernel(x, w_in, g_cq, g_ckv, w_uq, w_uk, w_uv, w_o, ln1_g, ln1_b, w_up, conv_w, conv_b, w_down, ln2_g, ln2_b, loss_target, m_w_in, m_g_cq, m_g_ckv, m_w_uq, m_w_uk, m_w_uv, m_w_o, m_ln1_g, m_ln1_b, m_w_up, m_conv_w, m_conv_b, m_w_down, m_ln2_g, m_ln2_b, v_w_in, v_g_cq, v_g_ckv, v_w_uq, v_w_uk, v_w_uv, v_w_o, v_ln1_g, v_ln1_b, v_w_up, v_conv_w, v_conv_b, v_w_down, v_ln2_g, v_ln2_b):
    wts = dict(zip(WEIGHTS, (w_in, g_cq, g_ckv, w_uq, w_uk, w_uv, w_o, ln1_g, ln1_b, w_up, conv_w, conv_b, w_down, ln2_g, ln2_b)))
    mom = dict(zip(WEIGHTS, (m_w_in, m_g_cq, m_g_ckv, m_w_uq, m_w_uk, m_w_uv, m_w_o, m_ln1_g, m_ln1_b, m_w_up, m_conv_w, m_conv_b, m_w_down, m_ln2_g, m_ln2_b)))
    var = dict(zip(WEIGHTS, (v_w_in, v_g_cq, v_g_ckv, v_w_uq, v_w_uk, v_w_uv, v_w_o, v_ln1_g, v_ln1_b, v_w_up, v_conv_w, v_conv_b, v_w_down, v_ln2_g, v_ln2_b)))

    wp = jnp.concatenate([_rows(_mx(wts[n]), SHARD_ROWS[n]) for n in GATHERED], axis=0)
    cwp = _rows(conv_w, SHARD_ROWS["conv_w"])
    wfull, cwfull = _gather_weights(wp.reshape(2, R_GATHER // 2, LANES), cwp)
    wfull = wfull.reshape(N_CHIPS, R_GATHER, LANES)
    full, r = {}, 0
    for n in GATHERED:
        full[n] = _from_chip_blocks(n, wfull[:, r:r + SHARD_ROWS[n]])
        r += SHARD_ROWS[n]
    conv_w_full = _from_chip_blocks("conv_w", cwfull)
    w = _prep_weights(full["w_in"], full["w_uq"], w_uk, w_uv, full["w_o"], full["w_up"], full["w_down"])

    loss, grad_x, g = _local_step(x[0], loss_target[0], w, g_cq, g_ckv, ln1_g, ln1_b, conv_w_full, conv_b, ln2_g, ln2_b)
    loss = lax.psum(loss, ("x", "y", "c"))

    gs = _pack_grads(g).reshape(N_CHIPS, 2, R_HALF, LANES)
    gr = _pack_small(g)
    c_arr = lax.axis_index("c").astype(jnp.int32).reshape(1)
    recv_s, recv_r = _exchange_sibling_halves(gs, gr)
    ps = _add_own_half(gs, recv_s, c_arr)
    pr = _add2(gr, recv_r, name="add_small")
    slots_s, slots_r = _exchange_chips(ps, pr)
    g_half = _sum_slots(slots_s, tr=ROW_TILE, name="sum_chips")
    g_small = _sum_slots(slots_r, tr=R_SMALL, name="sum_chips_small")
    g_shard = _exchange_sibling_result(g_half).reshape(R_SHARD, LANES)

    d_s, m_s, v_s = _adamw(_pack_shard(wts), g_shard, _pack_shard(mom), _pack_shard(var), tr=ROW_TILE, name="adamw_shard")
    d_r, m_r, v_r = _adamw(_pack_small(wts), g_small, _pack_small(mom), _pack_small(var), tr=R_SMALL, name="adamw_small")
    outs = []
    for shard_buf, small_buf in ((g_shard, g_small), (d_s, d_r), (m_s, m_r), (v_s, v_r)):
        t = {**_unpack_shard(shard_buf), **_unpack_small(small_buf)}
        outs.extend(t[n] for n in WEIGHTS)
    return (loss, grad_x[None], *outs)
```

```python
import functools
import math

import jax
import jax.numpy as jnp
from jax import lax
from jax.experimental import pallas as pl
from jax.experimental.pallas import tpu as pltpu

F32 = jnp.float32
MXU_DTYPE = jnp.bfloat16
GRAD_WIRE_DTYPE = jnp.bfloat16
NEG = -1e30

D_MODEL = 1024
HEADS = 8
HEAD_DIM = 64
Q_RANK = 256
KV_RANK = 128
NOPE = 64
ROPE = 32
QK_PAD = 128
IN_WIDTH = 1952
IN_EXT = 2048
D_FF = 2816
DIL_PAIRS = ((128, 1), (512, 4), (2048, 16))
DIL_BLOCK = 128
ROPE_THETA = 10000.0
DN_ALPHA = 2.0 ** 0.25
LN_EPS = 1e-5
RMS_EPS = 1e-6
MLA_SCALE = 1.0 / math.sqrt(NOPE + ROPE)
DIL_SCALE = 1.0 / math.sqrt(HEAD_DIM)

ADAM_LR = 0.001
ADAM_B1 = 0.9
ADAM_B2 = 0.999
ADAM_EPS = 1e-08
ADAM_WD = 0.01
ADAM_STEP = 10

LANES = 128
SUBLANES = 8
VMEM_LIMIT_BYTES = 56 * 1024 * 1024

MESH = pl.DeviceIdType.MESH


def _params(*sem):
    return pltpu.CompilerParams(dimension_semantics=sem, vmem_limit_bytes=VMEM_LIMIT_BYTES)


def _dot(a, b):
    return jnp.dot(a, b, preferred_element_type=F32)


def _dot_nt(a, b):
    return lax.dot_general(a, b, (((1,), (1,)), ((), ())), preferred_element_type=F32)


def _dot_tn(a, b):
    return lax.dot_general(a, b, (((0,), (0,)), ((), ())), preferred_element_type=F32)


def _mx(a):
    return a.astype(MXU_DTYPE)


def _mm_nn(a, b, *, name, tm, tn, tk, out_dtype=F32, add=None, add_scale=1.0):
    m, kdim = a.shape
    n = b.shape[1]
    nk = kdim // tk

    def body(*refs):
        if add is None:
            a_ref, b_ref, o_ref, acc = refs
        else:
            a_ref, b_ref, c_ref, o_ref, acc = refs
        k = pl.program_id(2)

        @pl.when(k == 0)
        def _():
            acc[...] = jnp.zeros_like(acc)

        acc[...] += _dot(_mx(a_ref[...]), _mx(b_ref[...]))

        @pl.when(k == nk - 1)
        def _():
            r = acc[...]
            if add is not None:
                r = r + add_scale * c_ref[...]
            o_ref[...] = r.astype(out_dtype)

    in_specs = [pl.BlockSpec((tm, tk), lambda i, j, k: (i, k)),
                pl.BlockSpec((tk, tn), lambda i, j, k: (k, j))]
    args = [a, b]
    if add is not None:
        in_specs.append(pl.BlockSpec((tm, tn), lambda i, j, k: (i, j)))
        args.append(add)
    return pl.pallas_call(
        body, name=name,
        out_shape=jax.ShapeDtypeStruct((m, n), out_dtype),
        grid=(m // tm, n // tn, nk),
        in_specs=in_specs,
        out_specs=pl.BlockSpec((tm, tn), lambda i, j, k: (i, j)),
        scratch_shapes=[pltpu.VMEM((tm, tn), F32)],
        compiler_params=_params("parallel", "parallel", "arbitrary"),
    )(*args)


def _mm_tn(a, b, *, name, tm, tn, ts, out_dtype=F32):
    s, m = a.shape
    n = b.shape[1]
    ns = s // ts

    def body(a_ref, b_ref, o_ref, acc):
        k = pl.program_id(2)

        @pl.when(k == 0)
        def _():
            acc[...] = jnp.zeros_like(acc)

        acc[...] += _dot_tn(_mx(a_ref[...]), _mx(b_ref[...]))

        @pl.when(k == ns - 1)
        def _():
            o_ref[...] = acc[...].astype(out_dtype)

    return pl.pallas_call(
        body, name=name,
        out_shape=jax.ShapeDtypeStruct((m, n), out_dtype),
        grid=(m // tm, n // tn, ns),
        in_specs=[pl.BlockSpec((ts, tm), lambda i, j, k: (k, i)),
                  pl.BlockSpec((ts, tn), lambda i, j, k: (k, j))],
        out_specs=pl.BlockSpec((tm, tn), lambda i, j, k: (i, j)),
        scratch_shapes=[pltpu.VMEM((tm, tn), F32)],
        compiler_params=_params("parallel", "parallel", "arbitrary"),
    )(a, b)


def _rope_tables(s):
    half = ROPE // 2
    freqs = ROPE_THETA ** (-jnp.arange(half, dtype=F32) / half)
    ang = jnp.arange(s).astype(F32)[:, None] * freqs[None, :]
    cos, sin = jnp.cos(ang), jnp.sin(ang)
    z = lambda w: jnp.zeros((s, w), F32)
    c = jnp.concatenate([jnp.ones((s, NOPE), F32), cos, cos, z(32)], axis=1)
    s1 = jnp.concatenate([z(NOPE + half), sin, z(32)], axis=1)
    s2 = jnp.concatenate([z(NOPE), -sin, z(half + 32)], axis=1)
    mask = jnp.concatenate([z(NOPE), jnp.ones((s, ROPE), F32), z(32)], axis=1)
    return c, s1, s2, mask


def _rope(x, c, s1, s2):
    return x * c + pltpu.roll(x, 16, 1) * s1 + pltpu.roll(x, LANES - 16, 1) * s2


def _unrope(dy, c, s1, s2):
    return dy * c + pltpu.roll(dy * s1, LANES - 16, 1) + pltpu.roll(dy * s2, 16, 1)


def _rms(x):
    r = lax.rsqrt(jnp.mean(x * x, axis=-1, keepdims=True) + RMS_EPS)
    return x * r, r


def _mla_prep_fwd(h, g_cq, g_ckv, wq, wk, wv, wv_t, tabs, *, tm):
    s = h.shape[0]
    c_t, s1_t, s2_t, _ = tabs

    def body(h_ref, gq_ref, gkv_ref, wq_ref, wk_ref, wv_ref, wvt_ref, c_ref, s1_ref, s2_ref,
             q_ref, k_ref, v_ref, vt_ref):
        cq = h_ref[:, 0:Q_RANK]
        ckv = h_ref[:, Q_RANK:Q_RANK + KV_RANK]
        kr = h_ref[:, Q_RANK + KV_RANK:Q_RANK + KV_RANK + QK_PAD]
        c, s1, s2 = c_ref[...], s1_ref[...], s2_ref[...]
        cqn = _mx(_rms(cq)[0] * gq_ref[...])
        ckvn = _mx(_rms(ckv)[0] * gkv_ref[...])
        kr_rot = _rope(kr, c, s1, s2)
        for hd in range(HEADS):
            q_ref[hd] = _rope(_dot(cqn, wq_ref[hd]), c, s1, s2).astype(q_ref.dtype)
            k_ref[hd] = (_dot(ckvn, wk_ref[hd]) + kr_rot).astype(k_ref.dtype)
            v_ref[hd] = _dot(ckvn, wv_ref[hd]).astype(v_ref.dtype)
            vt_ref[hd] = _dot_nt(wvt_ref[hd], ckvn).astype(vt_ref.dtype)

    full = lambda shp: pl.BlockSpec(shp, lambda i: (0,) * len(shp))
    row = lambda w: pl.BlockSpec((tm, w), lambda i: (i, 0))
    return pl.pallas_call(
        body, name="mla_prep_fwd",
        out_shape=(jax.ShapeDtypeStruct((HEADS, s, QK_PAD), MXU_DTYPE),
                   jax.ShapeDtypeStruct((HEADS, s, QK_PAD), MXU_DTYPE),
                   jax.ShapeDtypeStruct((HEADS, s, HEAD_DIM), MXU_DTYPE),
                   jax.ShapeDtypeStruct((HEADS, HEAD_DIM, s), MXU_DTYPE)),
        grid=(s // tm,),
        in_specs=[row(4 * LANES), full((1, Q_RANK)), full((1, KV_RANK)),
                  full((HEADS, Q_RANK, QK_PAD)), full((HEADS, KV_RANK, QK_PAD)), full((HEADS, KV_RANK, HEAD_DIM)),
                  full((HEADS, HEAD_DIM, KV_RANK)), row(LANES), row(LANES), row(LANES)],
        out_specs=(pl.BlockSpec((HEADS, tm, QK_PAD), lambda i: (0, i, 0)),
                   pl.BlockSpec((HEADS, tm, QK_PAD), lambda i: (0, i, 0)),
                   pl.BlockSpec((HEADS, tm, HEAD_DIM), lambda i: (0, i, 0)),
                   pl.BlockSpec((HEADS, HEAD_DIM, tm), lambda i: (0, 0, i))),
        compiler_params=_params("parallel"),
    )(h, g_cq, g_ckv, wq, wk, wv, wv_t, c_t, s1_t, s2_t)


def _mla_prep_bwd(h, dq, dk, dv, g_cq, g_ckv, wq_t, wk_t, wv_t, tabs, *, tm):
    s = h.shape[0]
    c_t, s1_t, s2_t, mask_t = tabs

    def body(h_ref, dq_ref, dk_ref, dv_ref, gq_ref, gkv_ref, wqt_ref, wkt_ref, wvt_ref,
             c_ref, s1_ref, s2_ref, mask_ref, dh_ref, dwq_ref, dwk_ref, dwv_ref, dgq_ref, dgkv_ref):
        i = pl.program_id(0)

        @pl.when(i == 0)
        def _():
            dwq_ref[...] = jnp.zeros_like(dwq_ref)
            dwk_ref[...] = jnp.zeros_like(dwk_ref)
            dwv_ref[...] = jnp.zeros_like(dwv_ref)
            dgq_ref[...] = jnp.zeros_like(dgq_ref)
            dgkv_ref[...] = jnp.zeros_like(dgkv_ref)

        cq = h_ref[:, 0:Q_RANK]
        ckv = h_ref[:, Q_RANK:Q_RANK + KV_RANK]
        c, s1, s2 = c_ref[...], s1_ref[...], s2_ref[...]
        cqh, rq = _rms(cq)
        ckvh, rkv = _rms(ckv)
        gq, gkv = gq_ref[...], gkv_ref[...]
        cqn = _mx(cqh * gq)
        ckvn = _mx(ckvh * gkv)
        dcqn = jnp.zeros((tm, Q_RANK), F32)
        dckvn = jnp.zeros((tm, KV_RANK), F32)
        dkr = jnp.zeros((tm, QK_PAD), F32)
        for hd in range(HEADS):
            dqh = _mx(_unrope(dq_ref[hd], c, s1, s2))
            dcqn = dcqn + _dot(dqh, wqt_ref[hd])
            dwq_ref[hd] += _dot_tn(cqn, dqh)
            dkh = dk_ref[hd]
            dkr = dkr + dkh
            dkh = _mx(dkh)
            dckvn = dckvn + _dot(dkh, wkt_ref[hd])
            dwk_ref[hd] += _dot_tn(ckvn, dkh)
            dvh = _mx(dv_ref[hd])
            dckvn = dckvn + _dot(dvh, wvt_ref[hd])
            dwv_ref[hd] += _dot_tn(ckvn, dvh)
        dgq_ref[...] += jnp.sum(dcqn * cqh, axis=0, keepdims=True)
        dgkv_ref[...] += jnp.sum(dckvn * ckvh, axis=0, keepdims=True)
        gd = dcqn * gq
        dh_ref[:, 0:Q_RANK] = rq * (gd - cqh * jnp.mean(gd * cqh, axis=-1, keepdims=True))
        gd = dckvn * gkv
        dh_ref[:, Q_RANK:Q_RANK + KV_RANK] = rkv * (gd - ckvh * jnp.mean(gd * ckvh, axis=-1, keepdims=True))
        dh_ref[:, Q_RANK + KV_RANK:Q_RANK + KV_RANK + QK_PAD] = _unrope(dkr, c, s1, s2) * mask_ref[...]

    full = lambda shp: pl.BlockSpec(shp, lambda i: (0,) * len(shp))
    row = lambda w: pl.BlockSpec((tm, w), lambda i: (i, 0))
    hrow = lambda w: pl.BlockSpec((HEADS, tm, w), lambda i: (0, i, 0))
    return pl.pallas_call(
        body, name="mla_prep_bwd",
        out_shape=(jax.ShapeDtypeStruct((s, 4 * LANES), F32),
                   jax.ShapeDtypeStruct((HEADS, Q_RANK, QK_PAD), F32),
                   jax.ShapeDtypeStruct((HEADS, KV_RANK, QK_PAD), F32),
                   jax.ShapeDtypeStruct((HEADS, KV_RANK, HEAD_DIM), F32),
                   jax.ShapeDtypeStruct((1, Q_RANK), F32),
                   jax.ShapeDtypeStruct((1, KV_RANK), F32)),
        grid=(s // tm,),
        in_specs=[row(4 * LANES), hrow(QK_PAD), hrow(QK_PAD), hrow(HEAD_DIM),
                  full((1, Q_RANK)), full((1, KV_RANK)),
                  full((HEADS, QK_PAD, Q_RANK)), full((HEADS, QK_PAD, KV_RANK)), full((HEADS, HEAD_DIM, KV_RANK)),
                  row(LANES), row(LANES), row(LANES), row(LANES)],
        out_specs=(row(4 * LANES), full((HEADS, Q_RANK, QK_PAD)), full((HEADS, KV_RANK, QK_PAD)),
                   full((HEADS, KV_RANK, HEAD_DIM)), full((1, Q_RANK)), full((1, KV_RANK))),
        compiler_params=_params("arbitrary"),
    )(h, dq, dk, dv, g_cq, g_ckv, wq_t, wk_t, wv_t, c_t, s1_t, s2_t, mask_t)


def _bdot(a, b, ca, cb):
    return lax.dot_general(a, b, (((ca,), (cb,)), ((0,), (0,))), preferred_element_type=F32)


def _causal_mask_t(t):
    kk = lax.broadcasted_iota(jnp.int32, (t, t), 0)
    qq = lax.broadcasted_iota(jnp.int32, (t, t), 1)
    return (qq >= kk)[None]


def _mla_attn_fwd(q, k, v_t, *, t, g):
    hds, s, _ = q.shape
    n = s // t

    def body(q_ref, k_ref, vt_ref, o_ref, lse_ref, m_sc, l_sc, acc_sc):
        qi, ki = pl.program_id(1), pl.program_id(2)

        @pl.when(ki == 0)
        def _():
            m_sc[...] = jnp.full_like(m_sc, NEG)
            l_sc[...] = jnp.zeros_like(l_sc)
            acc_sc[...] = jnp.zeros_like(acc_sc)

        def step(masked):
            sc = _bdot(k_ref[...], q_ref[...], 2, 2) * MLA_SCALE
            if masked:
                sc = jnp.where(_causal_mask_t(t), sc, NEG)
            m_prev = m_sc[...]
            m_new = jnp.maximum(m_prev, jnp.max(sc, axis=1, keepdims=True))
            p = jnp.exp(sc - m_new)
            a = jnp.exp(m_prev - m_new)
            l_sc[...] = a * l_sc[...] + jnp.sum(p, axis=1, keepdims=True)
            acc_sc[...] = a * acc_sc[...] + _bdot(vt_ref[...], _mx(p), 2, 1)
            m_sc[...] = m_new

        @pl.when(ki < qi)
        def _():
            step(False)

        @pl.when(ki == qi)
        def _():
            step(True)
            o_ref[...] = acc_sc[...] / l_sc[...]
            lse_ref[...] = m_sc[...] + jnp.log(l_sc[...])

    qspec = pl.BlockSpec((g, t, QK_PAD), lambda h, i, j: (h, i, 0))
    kspec = pl.BlockSpec((g, t, QK_PAD), lambda h, i, j: (h, jnp.minimum(i, j), 0))
    vspec = pl.BlockSpec((g, HEAD_DIM, t), lambda h, i, j: (h, 0, jnp.minimum(i, j)))
    return pl.pallas_call(
        body, name="mla_attn_fwd",
        out_shape=(jax.ShapeDtypeStruct((hds, HEAD_DIM, s), F32), jax.ShapeDtypeStruct((hds, 1, s), F32)),
        grid=(hds // g, n, n),
        in_specs=[qspec, kspec, vspec],
        out_specs=(pl.BlockSpec((g, HEAD_DIM, t), lambda h, i, j: (h, 0, i)),
                   pl.BlockSpec((g, 1, t), lambda h, i, j: (h, 0, i))),
        scratch_shapes=[pltpu.VMEM((g, 1, t), F32), pltpu.VMEM((g, 1, t), F32), pltpu.VMEM((g, HEAD_DIM, t), F32)],
        compiler_params=_params("parallel", "parallel", "arbitrary"),
    )(q, k, v_t)


def _head_rowdot(a, b, *, tm):
    s, width = a.shape
    nh = width // HEAD_DIM

    def body(a_ref, b_ref, o_ref):
        prod = a_ref[...] * b_ref[...]
        for hd in range(nh):
            o_ref[:, hd:hd + 1] = jnp.sum(prod[:, hd * HEAD_DIM:(hd + 1) * HEAD_DIM], axis=-1, keepdims=True)

    return pl.pallas_call(
        body, name="head_rowdot",
        out_shape=jax.ShapeDtypeStruct((s, nh), F32),
        grid=(s // tm,),
        in_specs=[pl.BlockSpec((tm, width), lambda i: (i, 0))] * 2,
        out_specs=pl.BlockSpec((tm, nh), lambda i: (i, 0)),
        compiler_params=_params("parallel"),
    )(a, b)


def _mla_attn_bwd(q, k, v, do, lse, dd, *, t, g):
    hds, s, _ = q.shape
    n = s // t

    def body(q_ref, k_ref, v_ref, do_ref, lse_ref, dd_ref, dq_ref, dk_ref, dv_ref, dq_sc, dk_sc, dv_sc):
        ki, qi = pl.program_id(1), pl.program_id(2)

        @pl.when(jnp.logical_and(ki == 0, qi == 0))
        def _():
            dq_sc[...] = jnp.zeros_like(dq_sc)

        @pl.when(qi == 0)
        def _():
            dk_sc[...] = jnp.zeros_like(dk_sc)
            dv_sc[...] = jnp.zeros_like(dv_sc)

        def step(masked):
            qb, kb, dob = q_ref[...], k_ref[...], do_ref[...]
            sc = _bdot(kb, qb, 2, 2) * MLA_SCALE
            if masked:
                sc = jnp.where(_causal_mask_t(t), sc, NEG)
            p = jnp.exp(sc - lse_ref[...])
            dv_sc[...] += _bdot(_mx(p), dob, 2, 1)
            dp = _bdot(v_ref[...], dob, 2, 2)
            ds = _mx(p * (dp - dd_ref[...]) * MLA_SCALE)
            dk_sc[...] += _bdot(ds, qb, 2, 1)
            dq_sc[qi] += _bdot(ds, kb, 1, 1)

        @pl.when(qi == ki)
        def _():
            step(True)

        @pl.when(qi > ki)
        def _():
            step(False)

        @pl.when(qi == n - 1)
        def _():
            dk_ref[...] = dk_sc[...]
            dv_ref[...] = dv_sc[...]

        @pl.when(jnp.logical_and(ki == n - 1, qi == n - 1))
        def _():
            for j in range(n):
                dq_ref[:, j * t:(j + 1) * t, :] = dq_sc[j]

    qs = lambda w: pl.BlockSpec((g, t, w), lambda h, j, i: (h, jnp.maximum(i, j), 0))
    ks = lambda w: pl.BlockSpec((g, t, w), lambda h, j, i: (h, j, 0))
    rowq = pl.BlockSpec((g, 1, t), lambda h, j, i: (h, 0, jnp.maximum(i, j)))
    return pl.pallas_call(
        body, name="mla_attn_bwd",
        out_shape=(jax.ShapeDtypeStruct((hds, s, QK_PAD), F32), jax.ShapeDtypeStruct((hds, s, QK_PAD), F32),
                   jax.ShapeDtypeStruct((hds, s, HEAD_DIM), F32)),
        grid=(hds // g, n, n),
        in_specs=[qs(QK_PAD), ks(QK_PAD), ks(HEAD_DIM), qs(HEAD_DIM), rowq, rowq],
        out_specs=(pl.BlockSpec((g, s, QK_PAD), lambda h, j, i: (h, 0, 0)), ks(QK_PAD), ks(HEAD_DIM)),
        scratch_shapes=[pltpu.VMEM((n, g, t, QK_PAD), F32), pltpu.VMEM((g, t, QK_PAD), F32), pltpu.VMEM((g, t, HEAD_DIM), F32)],
        compiler_params=_params("parallel", "arbitrary", "arbitrary"),
    )(q, k, v, do, lse, dd)


def _perm(a, dil):
    if dil == 1:
        return a
    hds, s, e = a.shape
    return a.reshape(hds, s // dil, dil, e).transpose(0, 2, 1, 3).reshape(hds, s, e)


def _unperm(a, dil):
    if dil == 1:
        return a
    hds, s, e = a.shape
    return a.reshape(hds, dil, s // dil, e).transpose(0, 2, 1, 3).reshape(hds, s, e)


def _dil_bias(dil):
    slopes = 2.0 ** (-8.0 * jnp.arange(1, HEADS + 1, dtype=F32) / HEADS)
    iq = jnp.arange(DIL_BLOCK)[:, None]
    ik = jnp.arange(DIL_BLOCK)[None, :]
    off_c = iq - ik
    off_p = iq - ik + DIL_BLOCK
    b_c = -slopes[:, None, None] * (off_c * dil).astype(F32)[None]
    b_p = -slopes[:, None, None] * (off_p * dil).astype(F32)[None]
    b_c = jnp.where((off_c >= 0)[None], b_c, NEG)
    b_p = jnp.where((off_p <= DIL_BLOCK)[None], b_p, NEG)
    return b_c, b_p


def _dil_fwd(q, k, v, dil, *, name):
    hds, s, e = q.shape
    blk = DIL_BLOCK
    nblk = s // blk
    nb = nblk // dil
    b_c, b_p = _dil_bias(dil)

    def body(q_ref, kc_ref, kp_ref, vc_ref, vp_ref, bc_ref, bp_ref, o_ref, lse_ref):
        b = pl.program_id(0)
        first = (b % nb) == 0
        qb = q_ref[...]
        s_c = _bdot(qb, kc_ref[...], 2, 2) * DIL_SCALE + bc_ref[...]
        s_p = jnp.where(first, NEG, _bdot(qb, kp_ref[...], 2, 2) * DIL_SCALE + bp_ref[...])
        m = jnp.maximum(jnp.max(s_c, axis=-1, keepdims=True), jnp.max(s_p, axis=-1, keepdims=True))
        p_c = jnp.exp(s_c - m)
        p_p = jnp.exp(s_p - m)
        l = jnp.sum(p_c, axis=-1, keepdims=True) + jnp.sum(p_p, axis=-1, keepdims=True)
        o_ref[...] = (_bdot(_mx(p_c), vc_ref[...], 2, 1) + _bdot(_mx(p_p), vp_ref[...], 2, 1)) / l
        lse_ref[...] = m + jnp.log(l)

    cur = lambda w: pl.BlockSpec((hds, blk, w), lambda b: (0, b, 0))
    prev = lambda w: pl.BlockSpec((hds, blk, w), lambda b: (0, jnp.maximum(b - 1, 0), 0))
    bias = pl.BlockSpec((hds, blk, blk), lambda b: (0, 0, 0))
    return pl.pallas_call(
        body, name=name,
        out_shape=(jax.ShapeDtypeStruct((hds, s, e), F32), jax.ShapeDtypeStruct((hds, s, 1), F32)),
        grid=(nblk,),
        in_specs=[cur(e), cur(e), prev(e), cur(e), prev(e), bias, bias],
        out_specs=(cur(e), cur(1)),
        compiler_params=_params("parallel"),
    )(q, k, k, v, v, b_c, b_p)


def _dil_combine(os_, lses, *, ts):
    hds, s, e = os_[0].shape

    def body(o0, o1, o2, l0, l1, l2, o_ref, l_ref):
        a0, a1, a2 = l0[0], l1[0], l2[0]
        m = jnp.maximum(jnp.maximum(a0, a1), a2)
        e0, e1, e2 = jnp.exp(a0 - m), jnp.exp(a1 - m), jnp.exp(a2 - m)
        tot = e0 + e1 + e2
        o_ref[0] = ((e0 / tot) * o0[0] + (e1 / tot) * o1[0]) + (e2 / tot) * o2[0]
        l_ref[0] = m + jnp.log(tot)

    spec = lambda w: pl.BlockSpec((1, ts, w), lambda h, i: (h, i, 0))
    return pl.pallas_call(
        body, name="dil_combine",
        out_shape=(jax.ShapeDtypeStruct((hds, s, e), F32), jax.ShapeDtypeStruct((hds, s, 1), F32)),
        grid=(hds, s // ts),
        in_specs=[spec(e)] * 3 + [spec(1)] * 3,
        out_specs=(spec(e), spec(1)),
        compiler_params=_params("parallel", "parallel"),
    )(*os_, *lses)


def _dil_bwd(q, k, v, do, lj, dd, dil, *, name):
    hds, s, e = q.shape
    blk = DIL_BLOCK
    nblk = s // blk
    nb = nblk // dil
    b_c, b_p = _dil_bias(dil)

    def body(q_ref, qn_ref, kc_ref, kp_ref, vc_ref, vp_ref, do_ref, don_ref, l_ref, ln_ref, d_ref, dn_ref,
             bc_ref, bp_ref, dq_ref, dk_ref, dv_ref):
        b = pl.program_id(0)
        first = (b % nb) == 0
        nxt = jnp.logical_and(b + 1 < nblk, ((b + 1) % nb) != 0)
        qb, kc, kp, vc, vp = q_ref[...], kc_ref[...], kp_ref[...], vc_ref[...], vp_ref[...]
        dob = _mx(do_ref[...])
        bc, bp = bc_ref[...], bp_ref[...]
        p_c = jnp.exp(_bdot(qb, kc, 2, 2) * DIL_SCALE + bc - l_ref[...])
        p_p = jnp.where(first, 0.0, jnp.exp(_bdot(qb, kp, 2, 2) * DIL_SCALE + bp - l_ref[...]))
        ds_c = _mx(p_c * (_bdot(dob, vc, 2, 2) - d_ref[...]) * DIL_SCALE)
        ds_p = _mx(p_p * (_bdot(dob, vp, 2, 2) - d_ref[...]) * DIL_SCALE)
        dq_ref[...] = _bdot(ds_c, kc, 2, 1) + _bdot(ds_p, kp, 2, 1)
        qn = qn_ref[...]
        donb = _mx(don_ref[...])
        p_n = jnp.where(nxt, jnp.exp(_bdot(qn, kc, 2, 2) * DIL_SCALE + bp - ln_ref[...]), 0.0)
        ds_n = _mx(p_n * (_bdot(donb, vc, 2, 2) - dn_ref[...]) * DIL_SCALE)
        dk_ref[...] = _bdot(ds_c, qb, 1, 1) + _bdot(ds_n, qn, 1, 1)
        dv_ref[...] = _bdot(_mx(p_c), dob, 1, 1) + _bdot(_mx(p_n), donb, 1, 1)

    cur = lambda w: pl.BlockSpec((hds, blk, w), lambda b: (0, b, 0))
    prev = lambda w: pl.BlockSpec((hds, blk, w), lambda b: (0, jnp.maximum(b - 1, 0), 0))
    nxt_ = lambda w: pl.BlockSpec((hds, blk, w), lambda b: (0, jnp.minimum(b + 1, nblk - 1), 0))
    bias = pl.BlockSpec((hds, blk, blk), lambda b: (0, 0, 0))
    out = jax.ShapeDtypeStruct((hds, s, e), F32)
    return pl.pallas_call(
        body, name=name,
        out_shape=(out, out, out),
        grid=(nblk,),
        in_specs=[cur(e), nxt_(e), cur(e), prev(e), cur(e), prev(e), cur(e), nxt_(e),
                  cur(1), nxt_(1), cur(1), nxt_(1), bias, bias],
        out_specs=(cur(e), cur(e), cur(e)),
        compiler_params=_params("parallel"),
    )(q, q, k, k, v, v, do, do, lj, lj, dd, dd, b_c, b_p)


def _add3(a, b, c, *, ts, name):
    hds, s, e = a.shape

    def body(a_ref, b_ref, c_ref, o_ref):
        o_ref[...] = (a_ref[...] + b_ref[...]) + c_ref[...]

    spec = pl.BlockSpec((1, ts, e), lambda h, i: (h, i, 0))
    return pl.pallas_call(
        body, name=name,
        out_shape=jax.ShapeDtypeStruct((hds, s, e), F32),
        grid=(hds, s // ts),
        in_specs=[spec] * 3, out_specs=spec,
        compiler_params=_params("parallel", "parallel"),
    )(a, b, c)


def _ln_fwd(z, g, b):
    mu = jnp.mean(z, axis=-1, keepdims=True)
    zc = z - mu
    var = jnp.mean(zc * zc, axis=-1, keepdims=True)
    rstd = lax.rsqrt(var + LN_EPS)
    xhat = zc * rstd
    return xhat * g + b, xhat, rstd


def _ln_bwd(dy, xhat, rstd, g):
    dxh = dy * g
    return rstd * (dxh - jnp.mean(dxh, axis=-1, keepdims=True) - xhat * jnp.mean(dxh * xhat, axis=-1, keepdims=True))


def _out_ln1(attn, w_o, x, g, b, *, tm):
    s = x.shape[0]

    def body(a_ref, w_ref, x_ref, g_ref, b_ref, x1_ref, xh_ref, r_ref):
        z = DN_ALPHA * x_ref[...] + _dot(a_ref[...], w_ref[...])
        y, xhat, rstd = _ln_fwd(z, g_ref[...], b_ref[...])
        x1_ref[...] = y
        xh_ref[...] = xhat
        r_ref[...] = rstd

    row = lambda w: pl.BlockSpec((tm, w), lambda i: (i, 0))
    full = lambda shp: pl.BlockSpec(shp, lambda i: (0,) * len(shp))
    act = jax.ShapeDtypeStruct((s, D_MODEL), F32)
    return pl.pallas_call(
        body, name="out_ln1",
        out_shape=(act, act, jax.ShapeDtypeStruct((s, 1), F32)),
        grid=(s // tm,),
        in_specs=[row(D_MODEL), full((D_MODEL, D_MODEL)), row(D_MODEL), full((1, D_MODEL)), full((1, D_MODEL))],
        out_specs=(row(D_MODEL), row(D_MODEL), row(1)),
        compiler_params=_params("parallel"),
    )(attn, w_o, x, g, b)


def _down_ln2_loss(act, w_down, x1, g, b, target, *, tm):
    s = x1.shape[0]

    def body(a_ref, w_ref, x1_ref, g_ref, b_ref, t_ref, dz_ref, loss_ref, dg_ref, db_ref):
        i = pl.program_id(0)

        @pl.when(i == 0)
        def _():
            loss_ref[...] = jnp.zeros_like(loss_ref)
            dg_ref[...] = jnp.zeros_like(dg_ref)
            db_ref[...] = jnp.zeros_like(db_ref)

        gam = g_ref[...]
        z = DN_ALPHA * x1_ref[...] + _dot(a_ref[...], w_ref[...])
        y, xhat, rstd = _ln_fwd(z, gam, b_ref[...])
        err = y - t_ref[...]
        loss_ref[...] += 0.5 * jnp.sum(jnp.mean(err * err, axis=-1, keepdims=True))
        dy = err * (1.0 / D_MODEL)
        dg_ref[...] += jnp.sum(dy * xhat, axis=0, keepdims=True)
        db_ref[...] += jnp.sum(dy, axis=0, keepdims=True)
        dz_ref[...] = _ln_bwd(dy, xhat, rstd, gam)

    row = lambda w: pl.BlockSpec((tm, w), lambda i: (i, 0))
    full = lambda shp: pl.BlockSpec(shp, lambda i: (0,) * len(shp))
    vec = jax.ShapeDtypeStruct((1, D_MODEL), F32)
    return pl.pallas_call(
        body, name="down_ln2_loss",
        out_shape=(jax.ShapeDtypeStruct((s, D_MODEL), F32), jax.ShapeDtypeStruct((1, LANES), F32), vec, vec),
        grid=(s // tm,),
        in_specs=[row(D_FF), full((D_FF, D_MODEL)), row(D_MODEL), full((1, D_MODEL)), full((1, D_MODEL)), row(D_MODEL)],
        out_specs=(row(D_MODEL), full((1, LANES)), full((1, D_MODEL)), full((1, D_MODEL))),
        compiler_params=_params("arbitrary"),
    )(act, w_down, x1, g, b, target)


def _up_bwd_ln1(du_a, du_g, w_up_t, dz2, xhat1, rstd1, g, *, tm):
    s = dz2.shape[0]

    def body(dua_ref, dug_ref, wa_ref, wg_ref, dz2_ref, xh_ref, r_ref, g_ref, dz1_ref, dg_ref, db_ref):
        i = pl.program_id(0)

        @pl.when(i == 0)
        def _():
            dg_ref[...] = jnp.zeros_like(dg_ref)
            db_ref[...] = jnp.zeros_like(db_ref)

        dx1 = DN_ALPHA * dz2_ref[...] + (_dot(dua_ref[...], wa_ref[...]) + _dot(dug_ref[...], wg_ref[...]))
        xhat = xh_ref[...]
        dg_ref[...] += jnp.sum(dx1 * xhat, axis=0, keepdims=True)
        db_ref[...] += jnp.sum(dx1, axis=0, keepdims=True)
        dz1_ref[...] = _ln_bwd(dx1, xhat, r_ref[...], g_ref[...])

    row = lambda w: pl.BlockSpec((tm, w), lambda i: (i, 0))
    full = lambda shp: pl.BlockSpec(shp, lambda i: (0,) * len(shp))
    vec = jax.ShapeDtypeStruct((1, D_MODEL), F32)
    return pl.pallas_call(
        body, name="up_bwd_ln1",
        out_shape=(jax.ShapeDtypeStruct((s, D_MODEL), F32), vec, vec),
        grid=(s // tm,),
        in_specs=[row(D_FF), row(D_FF),
                  pl.BlockSpec((D_FF, D_MODEL), lambda i: (0, 0)), pl.BlockSpec((D_FF, D_MODEL), lambda i: (1, 0)),
                  row(D_MODEL), row(D_MODEL), row(1), full((1, D_MODEL))],
        out_specs=(row(D_MODEL), full((1, D_MODEL)), full((1, D_MODEL))),
        compiler_params=_params("arbitrary"),
    )(du_a, du_g, w_up_t, w_up_t, dz2, xhat1, rstd1, g)


GELU_C = math.sqrt(2.0 / math.pi)


def _gelu(x):
    cdf = 0.5 * (1.0 + jnp.tanh(GELU_C * (x + 0.044715 * (x * x * x))))
    return x * cdf


def _gelu_grad(x):
    t = jnp.tanh(GELU_C * (x + 0.044715 * (x * x * x)))
    return 0.5 * (1.0 + t) + 0.5 * x * (1.0 - t * t) * (GELU_C * (1.0 + 3.0 * 0.044715 * (x * x)))


def _shift_down(u, halo):
    t = u.shape[0]
    row = lax.broadcasted_iota(jnp.int32, u.shape, 0)
    h7, h6 = halo[7:8, :], halo[6:7, :]
    s1 = jnp.where(row == 0, h7, pltpu.roll(u, 1, 0))
    s2 = jnp.where(row == 0, h6, jnp.where(row == 1, h7, pltpu.roll(u, 2, 0)))
    return s1, s2


def _shift_up(d, nxt):
    t = d.shape[0]
    row = lax.broadcasted_iota(jnp.int32, d.shape, 0)
    n0, n1 = nxt[0:1, :], nxt[1:2, :]
    s1 = jnp.where(row == t - 1, n0, pltpu.roll(d, t - 1, 0))
    s2 = jnp.where(row == t - 1, n1, jnp.where(row == t - 2, n0, pltpu.roll(d, t - 2, 0)))
    return s1, s2


def _conv(u, s1, s2, w, b):
    return ((b + w[0:1, :] * s2) + w[1:2, :] * s1) + w[2:3, :] * u


def _gate_fwd(u, conv_w, conv_b, *, tm, tn):
    s = u.shape[0]
    nj = D_FF // tn
    hb = tm // SUBLANES

    def body(ua_ref, ug_ref, ha_ref, hg_ref, wa_ref, wg_ref, ba_ref, bg_ref, o_ref):
        keep = pl.program_id(0) > 0
        ua, ug = ua_ref[...], ug_ref[...]
        ha = jnp.where(keep, ha_ref[...], 0.0)
        hg = jnp.where(keep, hg_ref[...], 0.0)
        a = _conv(ua, *_shift_down(ua, ha), wa_ref[...], ba_ref[...])
        g = _conv(ug, *_shift_down(ug, hg), wg_ref[...], bg_ref[...])
        o_ref[...] = (_gelu(g) * a).astype(o_ref.dtype)

    main = lambda off: pl.BlockSpec((tm, tn), lambda i, j: (i, j + off))
    halo = lambda off: pl.BlockSpec((SUBLANES, tn), lambda i, j: (jnp.maximum(i * hb - 1, 0), j + off))
    wspec = lambda r, off: pl.BlockSpec((r, tn), lambda i, j: (0, j + off))
    return pl.pallas_call(
        body, name="gate_fwd",
        out_shape=jax.ShapeDtypeStruct((s, D_FF), MXU_DTYPE),
        grid=(s // tm, nj),
        in_specs=[main(0), main(nj), halo(0), halo(nj), wspec(3, 0), wspec(3, nj), wspec(1, 0), wspec(1, nj)],
        out_specs=pl.BlockSpec((tm, tn), lambda i, j: (i, j)),
        compiler_params=_params("parallel", "parallel"),
    )(u, u, u, u, conv_w, conv_w, conv_b, conv_b)


def _gate_bwd(u, dact, conv_w, conv_b, *, tm, tn):
    s = u.shape[0]
    nj = D_FF // tn
    ni = s // tm
    hb = tm // SUBLANES

    def body(ua_ref, ug_ref, ha_ref, hg_ref, na_ref, ng_ref, d_ref, dn_ref, wa_ref, wg_ref, ba_ref, bg_ref,
             dua_ref, dug_ref, dwa_ref, dwg_ref, dba_ref, dbg_ref):
        i = pl.program_id(1)

        @pl.when(i == 0)
        def _():
            for r in (dwa_ref, dwg_ref, dba_ref, dbg_ref):
                r[...] = jnp.zeros_like(r)

        wa, wg, ba, bg = wa_ref[...], wg_ref[...], ba_ref[...], bg_ref[...]
        ua, ug = ua_ref[...], ug_ref[...]
        ha = jnp.where(i > 0, ha_ref[...], 0.0)
        hg = jnp.where(i > 0, hg_ref[...], 0.0)
        sa1, sa2 = _shift_down(ua, ha)
        sg1, sg2 = _shift_down(ug, hg)
        a = _conv(ua, sa1, sa2, wa, ba)
        g = _conv(ug, sg1, sg2, wg, bg)
        d = d_ref[...]
        dya = d * _gelu(g)
        dyg = d * a * _gelu_grad(g)
        na, ng = na_ref[...], ng_ref[...]
        a_n = _conv(na, *_shift_down(na, ua[tm - SUBLANES:, :]), wa, ba)
        g_n = _conv(ng, *_shift_down(ng, ug[tm - SUBLANES:, :]), wg, bg)
        dn = jnp.where(i < ni - 1, dn_ref[...], 0.0)
        dya_n = dn * _gelu(g_n)
        dyg_n = dn * a_n * _gelu_grad(g_n)
        da1, da2 = _shift_up(dya, dya_n)
        dg1, dg2 = _shift_up(dyg, dyg_n)
        dua_ref[...] = (wa[2:3, :] * dya + wa[1:2, :] * da1 + wa[0:1, :] * da2).astype(dua_ref.dtype)
        dug_ref[...] = (wg[2:3, :] * dyg + wg[1:2, :] * dg1 + wg[0:1, :] * dg2).astype(dug_ref.dtype)
        ssum = lambda v: jnp.sum(v, axis=0, keepdims=True)
        dwa_ref[...] += jnp.concatenate([ssum(dya * sa2), ssum(dya * sa1), ssum(dya * ua)], axis=0)
        dwg_ref[...] += jnp.concatenate([ssum(dyg * sg2), ssum(dyg * sg1), ssum(dyg * ug)], axis=0)
        dba_ref[...] += ssum(dya)
        dbg_ref[...] += ssum(dyg)

    main = lambda off: pl.BlockSpec((tm, tn), lambda j, i: (i, j + off))
    halo = lambda off: pl.BlockSpec((SUBLANES, tn), lambda j, i: (jnp.maximum(i * hb - 1, 0), j + off))
    nxt = lambda off: pl.BlockSpec((SUBLANES, tn), lambda j, i: (jnp.minimum((i + 1) * hb, s // SUBLANES - 1), j + off))
    wspec = lambda r, off: pl.BlockSpec((r, tn), lambda j, i: (0, j + off))
    return pl.pallas_call(
        body, name="gate_bwd",
        out_shape=(jax.ShapeDtypeStruct((s, D_FF), MXU_DTYPE), jax.ShapeDtypeStruct((s, D_FF), MXU_DTYPE),
                   jax.ShapeDtypeStruct((3, D_FF), F32), jax.ShapeDtypeStruct((3, D_FF), F32),
                   jax.ShapeDtypeStruct((1, D_FF), F32), jax.ShapeDtypeStruct((1, D_FF), F32)),
        grid=(nj, ni),
        in_specs=[main(0), main(nj), halo(0), halo(nj), nxt(0), nxt(nj), main(0), nxt(0),
                  wspec(3, 0), wspec(3, nj), wspec(1, 0), wspec(1, nj)],
        out_specs=(main(0), main(0), wspec(3, 0), wspec(3, 0), wspec(1, 0), wspec(1, 0)),
        compiler_params=_params("parallel", "arbitrary"),
    )(u, u, u, u, u, u, dact, dact, conv_w, conv_w, conv_b, conv_b)


def _prep_weights(w_in, w_uq, w_uk, w_uv, w_o, w_up, w_down):
    c = lambda a: a.astype(MXU_DTYPE)
    w_in = c(w_in)
    z = lambda w: jnp.zeros((D_MODEL, w), MXU_DTYPE)
    r0 = Q_RANK + KV_RANK
    w_in_ext = jnp.concatenate([w_in[:, :r0], z(NOPE), w_in[:, r0:r0 + ROPE], z(32), w_in[:, r0 + ROPE:]], axis=1)
    wq = jnp.pad(c(w_uq).transpose(1, 0, 2), ((0, 0), (0, 0), (0, QK_PAD - NOPE - ROPE)))
    wk = jnp.pad(c(w_uk).transpose(1, 0, 2), ((0, 0), (0, 0), (0, QK_PAD - NOPE)))
    wv = c(w_uv).transpose(1, 0, 2)
    t3 = lambda a: a.transpose(0, 2, 1)
    w_o, w_up, w_down = c(w_o), c(w_up), c(w_down)
    return dict(w_in=w_in_ext, w_in_t=w_in_ext.T, wq=wq, wq_t=t3(wq), wk=wk, wk_t=t3(wk), wv=wv, wv_t=t3(wv),
                w_o=w_o, w_o_t=w_o.T, w_up=w_up, w_up_t=w_up.T, w_down=w_down, w_down_t=w_down.T)


def _local_step(x, target, w, g_cq, g_ckv, ln1_g, ln1_b, conv_w, conv_b, ln2_g, ln2_b):
    s = x.shape[0]
    tabs = _rope_tables(s)
    r2 = lambda a: a.reshape(1, -1)
    heads = lambda a: a.reshape(s, HEADS, HEAD_DIM).transpose(1, 0, 2)
    unheads = lambda a: a.transpose(1, 0, 2).reshape(s, HEADS * HEAD_DIM)
    cb = r2(conv_b)
    dils = [d for _, d in DIL_PAIRS]

    h = _mm_nn(x, w["w_in"], name="in_proj", tm=512, tn=1024, tk=D_MODEL)
    q, k, v, v_t = _mla_prep_fwd(h, r2(g_cq), r2(g_ckv), w["wq"], w["wk"], w["wv"], w["wv_t"], tabs, tm=256)
    o_mla_t, lse_mla = _mla_attn_fwd(q, k, v_t, t=512, g=HEADS)
    qd, kd, vd = (_mx(heads(h[:, 512 * (i + 1):512 * (i + 2)])) for i in range(3))
    qp = [_perm(qd, d) for d in dils]
    kp = [_perm(kd, d) for d in dils]
    vp = [_perm(vd, d) for d in dils]
    o_bs, lse_bs = [], []
    for i, d in enumerate(dils):
        o_b, l_b = _dil_fwd(qp[i], kp[i], vp[i], d, name=f"dil_fwd_{d}")
        o_bs.append(_unperm(o_b, d))
        lse_bs.append(_unperm(l_b, d))
    o_dil, lj = _dil_combine(o_bs, lse_bs, ts=512)
    attn_f = jnp.concatenate([o_mla_t.transpose(2, 0, 1).reshape(s, HEADS * HEAD_DIM), unheads(o_dil)], axis=1)
    attn = _mx(attn_f)
    x1, xhat1, rstd1 = _out_ln1(attn, w["w_o"], x, r2(ln1_g), r2(ln1_b), tm=256)
    u = _mm_nn(x1, w["w_up"], name="up_proj", tm=512, tn=1408, tk=D_MODEL)
    act = _gate_fwd(u, conv_w, cb, tm=256, tn=1408)
    dz2, loss, dg2, db2 = _down_ln2_loss(act, w["w_down"], x1, r2(ln2_g), r2(ln2_b), target, tm=256)

    dact = _mm_nn(dz2, w["w_down_t"], name="down_bwd", tm=512, tn=1408, tk=D_MODEL)
    dw_down = _mm_tn(act, dz2, name="dw_down", tm=1408, tn=D_MODEL, ts=512)
    du_a, du_g, dcw_a, dcw_g, dcb_a, dcb_g = _gate_bwd(u, dact, conv_w, cb, tm=256, tn=1408)
    dz1, dg1, db1 = _up_bwd_ln1(du_a, du_g, w["w_up_t"], dz2, xhat1, rstd1, r2(ln1_g), tm=256)
    dw_up = jnp.concatenate([_mm_tn(x1, du_a, name="dw_up_a", tm=D_MODEL, tn=1408, ts=512),
                             _mm_tn(x1, du_g, name="dw_up_g", tm=D_MODEL, tn=1408, ts=512)], axis=1)
    dattn = _mm_nn(dz1, w["w_o_t"], name="o_bwd", tm=512, tn=D_MODEL, tk=D_MODEL)
    dw_o = _mm_tn(attn, dz1, name="dw_o", tm=D_MODEL, tn=D_MODEL, ts=512)
    do_mla, do_dil = _mx(heads(dattn[:, :512])), heads(dattn[:, 512:])
    dd_all = _head_rowdot(dattn, attn_f, tm=256).T
    dd_mla, dd_dil = dd_all[:HEADS].reshape(HEADS, 1, s), dd_all[HEADS:].reshape(HEADS, s, 1)
    dq, dk, dv = _mla_attn_bwd(q, k, v, do_mla, lse_mla, dd_mla, t=512, g=4)
    parts = []
    for i, d in enumerate(dils):
        g3 = _dil_bwd(qp[i], kp[i], vp[i], _perm(do_dil, d), _perm(lj, d), _perm(dd_dil, d), d, name=f"dil_bwd_{d}")
        parts.append([_unperm(g, d) for g in g3])
    dqd, dkd, dvd = (_add3(parts[0][j], parts[1][j], parts[2][j], ts=512, name=f"dil_sum_{j}") for j in range(3))
    dh_mla, dwq, dwk, dwv, dgq, dgkv = _mla_prep_bwd(h, dq, dk, dv, r2(g_cq), r2(g_ckv),
                                                     w["wq_t"], w["wk_t"], w["wv_t"], tabs, tm=256)
    dh = _mx(jnp.concatenate([dh_mla, unheads(dqd), unheads(dkd), unheads(dvd)], axis=1))
    grad_x = _mm_nn(dh, w["w_in_t"], name="in_bwd", tm=512, tn=D_MODEL, tk=D_MODEL, add=dz1, add_scale=DN_ALPHA)
    dw_ext = _mm_tn(x, dh, name="dw_in", tm=D_MODEL, tn=1024, ts=512)
    r0 = Q_RANK + KV_RANK
    grads = dict(
        w_in=jnp.concatenate([dw_ext[:, :r0], dw_ext[:, r0 + NOPE:r0 + NOPE + ROPE], dw_ext[:, 512:]], axis=1),
        g_cq=dgq[0], g_ckv=dgkv[0],
        w_uq=dwq[:, :, :NOPE + ROPE].transpose(1, 0, 2),
        w_uk=dwk[:, :, :NOPE].transpose(1, 0, 2),
        w_uv=dwv.transpose(1, 0, 2),
        w_o=dw_o, ln1_g=dg1[0], ln1_b=db1[0], w_up=dw_up,
        conv_w=jnp.concatenate([dcw_a, dcw_g], axis=1), conv_b=jnp.concatenate([dcb_a, dcb_g], axis=1)[0],
        w_down=dw_down, ln2_g=dg2[0], ln2_b=db2[0])
    return loss[0, 0], grad_x, grads


N_CHIPS = 4
SHARDED = ("w_in", "w_uq", "w_o", "w_up", "conv_w", "w_down")
COL_SHARDED = ("w_in", "w_up", "conv_w")
SHARD_SHAPE = dict(w_in=(D_MODEL, IN_WIDTH // 4), w_uq=(Q_RANK // 4, HEADS, NOPE + ROPE), w_o=(D_MODEL // 4, D_MODEL),
                   w_up=(D_MODEL, 2 * D_FF // 4), conv_w=(3, 2 * D_FF // 4), w_down=(D_FF // 4, D_MODEL))
SMALL = ("g_cq", "g_ckv", "w_uk", "w_uv", "ln1_g", "ln1_b", "conv_b", "ln2_g", "ln2_b")
SMALL_SHAPE = dict(g_cq=(Q_RANK,), g_ckv=(KV_RANK,), w_uk=(KV_RANK, HEADS, NOPE), w_uv=(KV_RANK, HEADS, HEAD_DIM),
                   ln1_g=(D_MODEL,), ln1_b=(D_MODEL,), conv_b=(2 * D_FF,), ln2_g=(D_MODEL,), ln2_b=(D_MODEL,))
ROW_TILE = 512


def _size(shape):
    return math.prod(shape)


def _padded_rows(n_elems, mult):
    return -(-n_elems // (LANES * mult)) * mult


SHARD_ROWS = {n: _padded_rows(_size(SHARD_SHAPE[n]), SUBLANES) for n in SHARDED}
R_SHARD = -(-sum(SHARD_ROWS.values()) // (2 * ROW_TILE)) * (2 * ROW_TILE)
R_HALF = R_SHARD // 2
HALF_TILE = R_HALF // 2
SHARD_TILE = R_SHARD // 8
R_SMALL = -(-sum(_size(SMALL_SHAPE[n]) for n in SMALL) // (LANES * LANES)) * LANES
GATHERED = ("w_in", "w_uq", "w_o", "w_up", "w_down")
R_GATHER = sum(SHARD_ROWS[n] for n in GATHERED)


def _rows(a, rows=None):
    flat = a.reshape(-1)
    rows = -(-flat.shape[0] // LANES) if rows is None else rows
    return jnp.pad(flat, (0, rows * LANES - flat.shape[0])).reshape(rows, LANES)


def _pack_shard(t):
    parts = [_rows(t[n].astype(F32), SHARD_ROWS[n]) for n in SHARDED]
    used = sum(SHARD_ROWS.values())
    return jnp.concatenate(parts + [jnp.zeros((R_SHARD - used, LANES), F32)], axis=0)


def _unpack_shard(buf):
    out, r = {}, 0
    for n in SHARDED:
        out[n] = buf[r:r + SHARD_ROWS[n]].reshape(-1)[:_size(SHARD_SHAPE[n])].reshape(SHARD_SHAPE[n])
        r += SHARD_ROWS[n]
    return out


def _chip_blocks(name, full):
    shp = SHARD_SHAPE[name]
    if name in COL_SHARDED:
        a = full.reshape(shp[0], N_CHIPS, shp[1]).transpose(1, 0, 2)
    else:
        a = full.reshape((N_CHIPS,) + shp)
    a = a.reshape(N_CHIPS, -1)
    rows = SHARD_ROWS[name]
    return jnp.pad(a, ((0, 0), (0, rows * LANES - a.shape[1]))).reshape(N_CHIPS, rows, LANES)


def _from_chip_blocks(name, blocks):
    shp = SHARD_SHAPE[name]
    a = blocks.reshape(N_CHIPS, -1)[:, :_size(shp)].reshape((N_CHIPS,) + shp)
    if name in COL_SHARDED:
        return a.transpose(1, 0, 2).reshape(shp[0], N_CHIPS * shp[1])
    return a.reshape((N_CHIPS * shp[0],) + shp[1:])


def _pack_grads(g):
    parts = [_chip_blocks(n, g[n]) for n in SHARDED]
    used = sum(SHARD_ROWS.values())
    return jnp.concatenate(parts + [jnp.zeros((N_CHIPS, R_SHARD - used, LANES), F32)], axis=1)


def _pack_small(t):
    flat = jnp.concatenate([t[n].astype(F32).reshape(-1) for n in SMALL])
    return _rows(flat, R_SMALL)


def _unpack_small(buf):
    flat, out, r = buf.reshape(-1), {}, 0
    for n in SMALL:
        out[n] = flat[r:r + _size(SMALL_SHAPE[n])].reshape(SMALL_SHAPE[n])
        r += _size(SMALL_SHAPE[n])
    return out


ANY = pl.BlockSpec(memory_space=pl.ANY)
COMM_PARAMS = pltpu.CompilerParams(has_side_effects=True)


def _coords():
    return lax.axis_index("x"), lax.axis_index("y"), lax.axis_index("c")


def _other_chips(x, y):
    return [(1 - x, y), (x, 1 - y), (1 - x, 1 - y)]


def _remote(src, dst, send_sems, recv_sems, k, to):
    return pltpu.make_async_remote_copy(src_ref=src, dst_ref=dst, send_sem=send_sems.at[k], recv_sem=recv_sems.at[k],
                                        device_id=to, device_id_type=MESH)


def _gather_weights(wp, cwp):
    def body(wp_ref, cw_ref, wout_ref, cwout_ref, send_sems, recv_sems):
        x, y, c = _coords()
        me = 2 * x + y
        sib = (x, y, 1 - c)
        chips = _other_chips(x, y)
        sends = [_remote(wp_ref.at[c], wout_ref.at[me, c], send_sems, recv_sems, j, (px, py, c))
                 for j, (px, py) in enumerate(chips)]
        sends += [_remote(cw_ref, cwout_ref.at[me], send_sems, recv_sems, 3 + j, (px, py, c))
                  for j, (px, py) in enumerate(chips)]
        for cp in sends:
            cp.start()
        for j, (px, py) in enumerate(chips):
            k = 2 * px + py
            _remote(wp_ref.at[c], wout_ref.at[k, c], send_sems, recv_sems, j, (px, py, c)).wait_recv()
            fwd = _remote(wout_ref.at[k, c], wout_ref.at[k, c], send_sems, recv_sems, 6 + j, sib)
            fwd.start()
            sends.append(fwd)
        for j, (px, py) in enumerate(chips):
            k = 2 * px + py
            _remote(cw_ref, cwout_ref.at[k], send_sems, recv_sems, 3 + j, (px, py, c)).wait_recv()
            _remote(wout_ref.at[k, 1 - c], wout_ref.at[k, 1 - c], send_sems, recv_sems, 6 + j, sib).wait_recv()
        for cp in sends:
            cp.wait_send()

    return pl.pallas_call(
        body, name="gather_weights",
        out_shape=(jax.ShapeDtypeStruct((N_CHIPS,) + wp.shape, wp.dtype), jax.ShapeDtypeStruct((N_CHIPS,) + cwp.shape, cwp.dtype)),
        in_specs=[ANY, ANY], out_specs=(ANY, ANY),
        scratch_shapes=[pltpu.SemaphoreType.DMA((9,)), pltpu.SemaphoreType.DMA((9,))],
        compiler_params=COMM_PARAMS,
    )(wp, cwp)


def _exchange_sibling_halves(gs, gr):
    def body(gs_ref, gr_ref, os_ref, or_ref, send_sems, recv_sems):
        x, y, c = _coords()
        sib = (x, y, 1 - c)
        cps = [_remote(gs_ref.at[k, 1 - c], os_ref.at[k], send_sems, recv_sems, k, sib) for k in range(N_CHIPS)]
        cps.append(_remote(gr_ref, or_ref, send_sems, recv_sems, N_CHIPS, sib))
        for cp in cps:
            cp.start()
        for cp in cps:
            cp.wait_recv()
        for cp in cps:
            cp.wait_send()

    return pl.pallas_call(
        body, name="exchange_sibling_halves",
        out_shape=(jax.ShapeDtypeStruct((N_CHIPS, R_HALF, LANES), F32), jax.ShapeDtypeStruct(gr.shape, F32)),
        in_specs=[ANY, ANY], out_specs=(ANY, ANY),
        scratch_shapes=[pltpu.SemaphoreType.DMA((N_CHIPS + 1,)), pltpu.SemaphoreType.DMA((N_CHIPS + 1,))],
        compiler_params=COMM_PARAMS,
    )(gs, gr)


def _exchange_chips(ps, pr):
    def body(ps_ref, pr_ref, ss_ref, sr_ref, send_sems, recv_sems):
        x, y, c = _coords()
        me = 2 * x + y
        chips = _other_chips(x, y)
        sends = []
        for j, (px, py) in enumerate(chips):
            sends.append(_remote(ps_ref.at[2 * px + py], ss_ref.at[me], send_sems, recv_sems, 2 * j, (px, py, c)))
            sends.append(_remote(pr_ref, sr_ref.at[me], send_sems, recv_sems, 2 * j + 1, (px, py, c)))
        for cp in sends:
            cp.start()
        for j, (px, py) in enumerate(chips):
            k = 2 * px + py
            _remote(ps_ref.at[me], ss_ref.at[k], send_sems, recv_sems, 2 * j, (px, py, c)).wait_recv()
            _remote(pr_ref, sr_ref.at[k], send_sems, recv_sems, 2 * j + 1, (px, py, c)).wait_recv()
        for cp in sends:
            cp.wait_send()

    return pl.pallas_call(
        body, name="exchange_chips",
        out_shape=(jax.ShapeDtypeStruct(ps.shape, ps.dtype), jax.ShapeDtypeStruct((N_CHIPS,) + pr.shape, F32)),
        in_specs=[ANY, ANY], out_specs=(ANY, ANY),
        scratch_shapes=[pltpu.SemaphoreType.DMA((6,)), pltpu.SemaphoreType.DMA((6,))],
        compiler_params=COMM_PARAMS,
    )(ps, pr)


def _exchange_sibling_result(gh):
    def body(gh_ref, out_ref, send_sems, recv_sems):
        x, y, c = _coords()
        cp = _remote(gh_ref, out_ref, send_sems, recv_sems, 0, (x, y, 1 - c))
        cp.start()
        cp.wait_recv()
        cp.wait_send()

    return pl.pallas_call(
        body, name="exchange_sibling_result",
        out_shape=jax.ShapeDtypeStruct(gh.shape, F32),
        in_specs=[ANY], out_specs=ANY,
        scratch_shapes=[pltpu.SemaphoreType.DMA((1,)), pltpu.SemaphoreType.DMA((1,))],
        compiler_params=COMM_PARAMS,
    )(gh)


def _add_own_half(gs, recv, c_arr):
    def body(c_ref, a_ref, b_ref, o_ref):
        o_ref[0] = (a_ref[0, 0] + b_ref[0]).astype(o_ref.dtype)

    return pl.pallas_call(
        body, name="add_own_half",
        out_shape=jax.ShapeDtypeStruct(recv.shape, GRAD_WIRE_DTYPE),
        grid_spec=pltpu.PrefetchScalarGridSpec(
            num_scalar_prefetch=1, grid=(N_CHIPS, R_HALF // HALF_TILE),
            in_specs=[pl.BlockSpec((1, 1, HALF_TILE, LANES), lambda k, i, c_ref: (k, c_ref[0], i, 0)),
                      pl.BlockSpec((1, HALF_TILE, LANES), lambda k, i, c_ref: (k, i, 0))],
            out_specs=pl.BlockSpec((1, HALF_TILE, LANES), lambda k, i, c_ref: (k, i, 0))),
        compiler_params=_params("parallel", "parallel"),
    )(c_arr, gs, recv)


def _add2(a, b, *, name):
    def body(a_ref, b_ref, o_ref):
        o_ref[...] = a_ref[...] + b_ref[...]

    return pl.pallas_call(body, name=name, out_shape=jax.ShapeDtypeStruct(a.shape, F32))(a, b)


def _sum_slots(slots, *, tr, name):
    r = slots.shape[1]

    def body(s_ref, o_ref):
        f = lambda k: s_ref[k].astype(F32)
        o_ref[...] = ((f(0) + f(1)) + f(2)) + f(3)

    return pl.pallas_call(
        body, name=name,
        out_shape=jax.ShapeDtypeStruct((r, LANES), F32),
        grid=(r // tr,),
        in_specs=[pl.BlockSpec((N_CHIPS, tr, LANES), lambda i: (0, i, 0))],
        out_specs=pl.BlockSpec((tr, LANES), lambda i: (i, 0)),
        compiler_params=_params("parallel"),
    )(slots)


def _adamw(w, g, m, v, *, tr, name):
    r = w.shape[0]

    def body(w_ref, g_ref, m_ref, v_ref, d_ref, nm_ref, nv_ref):
        g_ = g_ref[...]
        m_ = ADAM_B1 * m_ref[...] + (1.0 - ADAM_B1) * g_
        v_ = ADAM_B2 * v_ref[...] + (1.0 - ADAM_B2) * (g_ * g_)
        m_hat = m_ / (1.0 - ADAM_B1 ** ADAM_STEP)
        v_hat = v_ / (1.0 - ADAM_B2 ** ADAM_STEP)
        d_ref[...] = -ADAM_LR * (m_hat / (jnp.sqrt(v_hat) + ADAM_EPS) + ADAM_WD * w_ref[...])
        nm_ref[...] = m_
        nv_ref[...] = v_

    spec = pl.BlockSpec((tr, LANES), lambda i: (i, 0))
    out = jax.ShapeDtypeStruct((r, LANES), F32)
    return pl.pallas_call(
        body, name=name, out_shape=(out, out, out), grid=(r // tr,),
        in_specs=[spec] * 4, out_specs=(spec,) * 3,
        compiler_params=_params("parallel"),
    )(w, g, m, v)


WEIGHTS = ("w_in", "g_cq", "g_ckv", "w_uq", "w_uk", "w_uv", "w_o", "ln1_g", "ln1_b", "w_up", "conv_w", "conv_b",
           "w_down", "ln2_g", "ln2_b")


def kernel(x, w_in, g_cq, g_ckv, w_uq, w_uk, w_uv, w_o, ln1_g, ln1_b, w_up, conv_w, conv_b, w_down, ln2_g, ln2_b, loss_target, m_w_in, m_g_cq, m_g_ckv, m_w_uq, m_w_uk, m_w_uv, m_w_o, m_ln1_g, m_ln1_b, m_w_up, m_conv_w, m_conv_b, m_w_down, m_ln2_g, m_ln2_b, v_w_in, v_g_cq, v_g_ckv, v_w_uq, v_w_uk, v_w_uv, v_w_o, v_ln1_g, v_ln1_b, v_w_up, v_conv_w, v_conv_b, v_w_down, v_ln2_g, v_ln2_b):
    wts = dict(zip(WEIGHTS, (w_in, g_cq, g_ckv, w_uq, w_uk, w_uv, w_o, ln1_g, ln1_b, w_up, conv_w, conv_b, w_down, ln2_g, ln2_b)))
    mom = dict(zip(WEIGHTS, (m_w_in, m_g_cq, m_g_ckv, m_w_uq, m_w_uk, m_w_uv, m_w_o, m_ln1_g, m_ln1_b, m_w_up, m_conv_w, m_conv_b, m_w_down, m_ln2_g, m_ln2_b)))
    var = dict(zip(WEIGHTS, (v_w_in, v_g_cq, v_g_ckv, v_w_uq, v_w_uk, v_w_uv, v_w_o, v_ln1_g, v_ln1_b, v_w_up, v_conv_w, v_conv_b, v_w_down, v_ln2_g, v_ln2_b)))

    wp = jnp.concatenate([_rows(_mx(wts[n]), SHARD_ROWS[n]) for n in GATHERED], axis=0)
    cwp = _rows(conv_w, SHARD_ROWS["conv_w"])
    me = 2 * lax.axis_index("x") + lax.axis_index("y")
    my_c = lax.axis_index("c")
    own = lambda slots, mine: lax.dynamic_update_index_in_dim(slots, mine, me, 0)
    wp = wp.reshape(2, R_GATHER // 2, LANES)
    wfull, cwfull = _gather_weights(wp, cwp)
    wfull, cwfull = own(wfull, wp).reshape(N_CHIPS, R_GATHER, LANES), own(cwfull, cwp)
    full, r = {}, 0
    for n in GATHERED:
        full[n] = _from_chip_blocks(n, wfull[:, r:r + SHARD_ROWS[n]])
        r += SHARD_ROWS[n]
    conv_w_full = _from_chip_blocks("conv_w", cwfull)
    w = _prep_weights(full["w_in"], full["w_uq"], w_uk, w_uv, full["w_o"], full["w_up"], full["w_down"])

    loss, grad_x, g = _local_step(x[0], loss_target[0], w, g_cq, g_ckv, ln1_g, ln1_b, conv_w_full, conv_b, ln2_g, ln2_b)
    loss = lax.psum(loss, ("x", "y", "c"))

    gs = _pack_grads(g).reshape(N_CHIPS, 2, R_HALF, LANES)
    gr = _pack_small(g)
    c_arr = my_c.astype(jnp.int32).reshape(1)
    recv_s, recv_r = _exchange_sibling_halves(gs, gr)
    ps = _add_own_half(gs, recv_s, c_arr)
    pr = _add2(gr, recv_r, name="add_small")
    slots_s, slots_r = _exchange_chips(ps, pr)
    slots_s = own(slots_s, lax.dynamic_index_in_dim(ps, me, 0, keepdims=False))
    slots_r = own(slots_r, pr)
    g_half = _sum_slots(slots_s, tr=HALF_TILE, name="sum_chips")
    g_small = _sum_slots(slots_r, tr=R_SMALL, name="sum_chips_small")
    g_other = _exchange_sibling_result(g_half)
    g_shard = jnp.where(my_c == 0, jnp.concatenate([g_half, g_other]), jnp.concatenate([g_other, g_half]))

    d_s, m_s, v_s = _adamw(_pack_shard(wts), g_shard, _pack_shard(mom), _pack_shard(var), tr=SHARD_TILE, name="adamw_shard")
    d_r, m_r, v_r = _adamw(_pack_small(wts), g_small, _pack_small(mom), _pack_small(var), tr=R_SMALL, name="adamw_small")
    outs = []
    for shard_buf, small_buf in ((g_shard, g_small), (d_s, d_r), (m_s, m_r), (v_s, v_r)):
        t = {**_unpack_shard(shard_buf), **_unpack_small(small_buf)}
        outs.extend(t[n] for n in WEIGHTS)
    return (loss, grad_x[None], *outs)
```

```python
import functools
import math

import jax
import jax.numpy as jnp
from jax import lax
from jax.experimental import pallas as pl
from jax.experimental.pallas import tpu as pltpu

F32 = jnp.float32
MXU_DTYPE = jnp.bfloat16
GRAD_WIRE_DTYPE = jnp.bfloat16
NEG = -1e30

D_MODEL = 1024
HEADS = 8
HEAD_DIM = 64
Q_RANK = 256
KV_RANK = 128
NOPE = 64
ROPE = 32
QK_PAD = 128
IN_WIDTH = 1952
IN_EXT = 2048
D_FF = 2816
DIL_PAIRS = ((128, 1), (512, 4), (2048, 16))
DIL_BLOCK = 128
ROPE_THETA = 10000.0
DN_ALPHA = 2.0 ** 0.25
LN_EPS = 1e-5
RMS_EPS = 1e-6
MLA_SCALE = 1.0 / math.sqrt(NOPE + ROPE)
DIL_SCALE = 1.0 / math.sqrt(HEAD_DIM)

ADAM_LR = 0.001
ADAM_B1 = 0.9
ADAM_B2 = 0.999
ADAM_EPS = 1e-08
ADAM_WD = 0.01
ADAM_STEP = 10

LANES = 128
SUBLANES = 8
VMEM_LIMIT_BYTES = 56 * 1024 * 1024

MESH = pl.DeviceIdType.MESH


def _params(*sem):
    return pltpu.CompilerParams(dimension_semantics=sem, vmem_limit_bytes=VMEM_LIMIT_BYTES)


def _dot(a, b):
    return jnp.dot(a, b, preferred_element_type=F32)


def _dot_nt(a, b):
    return lax.dot_general(a, b, (((1,), (1,)), ((), ())), preferred_element_type=F32)


def _dot_tn(a, b):
    return lax.dot_general(a, b, (((0,), (0,)), ((), ())), preferred_element_type=F32)


def _mx(a):
    return a.astype(MXU_DTYPE)


def _mm_nn(a, b, *, name, tm, tn, tk, out_dtype=F32, add=None, add_scale=1.0):
    m, kdim = a.shape
    n = b.shape[1]
    nk = kdim // tk

    def body(*refs):
        if add is None:
            a_ref, b_ref, o_ref, acc = refs
        else:
            a_ref, b_ref, c_ref, o_ref, acc = refs
        k = pl.program_id(2)

        @pl.when(k == 0)
        def _():
            acc[...] = jnp.zeros_like(acc)

        acc[...] += _dot(_mx(a_ref[...]), _mx(b_ref[...]))

        @pl.when(k == nk - 1)
        def _():
            r = acc[...]
            if add is not None:
                r = r + add_scale * c_ref[...]
            o_ref[...] = r.astype(out_dtype)

    in_specs = [pl.BlockSpec((tm, tk), lambda i, j, k: (i, k)),
                pl.BlockSpec((tk, tn), lambda i, j, k: (k, j))]
    args = [a, b]
    if add is not None:
        in_specs.append(pl.BlockSpec((tm, tn), lambda i, j, k: (i, j)))
        args.append(add)
    return pl.pallas_call(
        body, name=name,
        out_shape=jax.ShapeDtypeStruct((m, n), out_dtype),
        grid=(m // tm, n // tn, nk),
        in_specs=in_specs,
        out_specs=pl.BlockSpec((tm, tn), lambda i, j, k: (i, j)),
        scratch_shapes=[pltpu.VMEM((tm, tn), F32)],
        compiler_params=_params("parallel", "parallel", "arbitrary"),
    )(*args)


def _mm_tn(a, b, *, name, tm, tn, ts, out_dtype=F32):
    s, m = a.shape
    n = b.shape[1]
    ns = s // ts

    def body(a_ref, b_ref, o_ref, acc):
        k = pl.program_id(2)

        @pl.when(k == 0)
        def _():
            acc[...] = jnp.zeros_like(acc)

        acc[...] += _dot_tn(_mx(a_ref[...]), _mx(b_ref[...]))

        @pl.when(k == ns - 1)
        def _():
            o_ref[...] = acc[...].astype(out_dtype)

    return pl.pallas_call(
        body, name=name,
        out_shape=jax.ShapeDtypeStruct((m, n), out_dtype),
        grid=(m // tm, n // tn, ns),
        in_specs=[pl.BlockSpec((ts, tm), lambda i, j, k: (k, i)),
                  pl.BlockSpec((ts, tn), lambda i, j, k: (k, j))],
        out_specs=pl.BlockSpec((tm, tn), lambda i, j, k: (i, j)),
        scratch_shapes=[pltpu.VMEM((tm, tn), F32)],
        compiler_params=_params("parallel", "parallel", "arbitrary"),
    )(a, b)


def _rope_tables(s):
    half = ROPE // 2
    freqs = ROPE_THETA ** (-jnp.arange(half, dtype=F32) / half)
    ang = jnp.arange(s).astype(F32)[:, None] * freqs[None, :]
    cos, sin = jnp.cos(ang), jnp.sin(ang)
    z = lambda w: jnp.zeros((s, w), F32)
    c = jnp.concatenate([jnp.ones((s, NOPE), F32), cos, cos, z(32)], axis=1)
    s1 = jnp.concatenate([z(NOPE + half), sin, z(32)], axis=1)
    s2 = jnp.concatenate([z(NOPE), -sin, z(half + 32)], axis=1)
    mask = jnp.concatenate([z(NOPE), jnp.ones((s, ROPE), F32), z(32)], axis=1)
    return c, s1, s2, mask


def _rope(x, c, s1, s2):
    return x * c + pltpu.roll(x, 16, 1) * s1 + pltpu.roll(x, LANES - 16, 1) * s2


def _unrope(dy, c, s1, s2):
    return dy * c + pltpu.roll(dy * s1, LANES - 16, 1) + pltpu.roll(dy * s2, 16, 1)


def _rms(x):
    r = lax.rsqrt(jnp.mean(x * x, axis=-1, keepdims=True) + RMS_EPS)
    return x * r, r


def _mla_prep_fwd(h, g_cq, g_ckv, wq, wk, wv, wv_t, tabs, *, tm):
    s = h.shape[0]
    c_t, s1_t, s2_t, _ = tabs

    def body(h_ref, gq_ref, gkv_ref, wq_ref, wk_ref, wv_ref, wvt_ref, c_ref, s1_ref, s2_ref,
             q_ref, k_ref, v_ref, vt_ref):
        cq = h_ref[:, 0:Q_RANK]
        ckv = h_ref[:, Q_RANK:Q_RANK + KV_RANK]
        kr = h_ref[:, Q_RANK + KV_RANK:Q_RANK + KV_RANK + QK_PAD]
        c, s1, s2 = c_ref[...], s1_ref[...], s2_ref[...]
        cqn = _mx(_rms(cq)[0] * gq_ref[...])
        ckvn = _mx(_rms(ckv)[0] * gkv_ref[...])
        kr_rot = _rope(kr, c, s1, s2)
        for hd in range(HEADS):
            q_ref[hd] = _rope(_dot(cqn, wq_ref[hd]), c, s1, s2).astype(q_ref.dtype)
            k_ref[hd] = (_dot(ckvn, wk_ref[hd]) + kr_rot).astype(k_ref.dtype)
            v_ref[hd] = _dot(ckvn, wv_ref[hd]).astype(v_ref.dtype)
            vt_ref[hd] = _dot_nt(wvt_ref[hd], ckvn).astype(vt_ref.dtype)

    full = lambda shp: pl.BlockSpec(shp, lambda i: (0,) * len(shp))
    row = lambda w: pl.BlockSpec((tm, w), lambda i: (i, 0))
    return pl.pallas_call(
        body, name="mla_prep_fwd",
        out_shape=(jax.ShapeDtypeStruct((HEADS, s, QK_PAD), MXU_DTYPE),
                   jax.ShapeDtypeStruct((HEADS, s, QK_PAD), MXU_DTYPE),
                   jax.ShapeDtypeStruct((HEADS, s, HEAD_DIM), MXU_DTYPE),
                   jax.ShapeDtypeStruct((HEADS, HEAD_DIM, s), MXU_DTYPE)),
        grid=(s // tm,),
        in_specs=[row(4 * LANES), full((1, Q_RANK)), full((1, KV_RANK)),
                  full((HEADS, Q_RANK, QK_PAD)), full((HEADS, KV_RANK, QK_PAD)), full((HEADS, KV_RANK, HEAD_DIM)),
                  full((HEADS, HEAD_DIM, KV_RANK)), row(LANES), row(LANES), row(LANES)],
        out_specs=(pl.BlockSpec((HEADS, tm, QK_PAD), lambda i: (0, i, 0)),
                   pl.BlockSpec((HEADS, tm, QK_PAD), lambda i: (0, i, 0)),
                   pl.BlockSpec((HEADS, tm, HEAD_DIM), lambda i: (0, i, 0)),
                   pl.BlockSpec((HEADS, HEAD_DIM, tm), lambda i: (0, 0, i))),
        compiler_params=_params("parallel"),
    )(h, g_cq, g_ckv, wq, wk, wv, wv_t, c_t, s1_t, s2_t)


def _mla_prep_bwd(h, dq, dk, dv, g_cq, g_ckv, wq_t, wk_t, wv_t, tabs, *, tm):
    s = h.shape[0]
    c_t, s1_t, s2_t, mask_t = tabs

    def body(h_ref, dq_ref, dk_ref, dv_ref, gq_ref, gkv_ref, wqt_ref, wkt_ref, wvt_ref,
             c_ref, s1_ref, s2_ref, mask_ref, dh_ref, dwq_ref, dwk_ref, dwv_ref, dgq_ref, dgkv_ref):
        i = pl.program_id(0)

        @pl.when(i == 0)
        def _():
            dwq_ref[...] = jnp.zeros_like(dwq_ref)
            dwk_ref[...] = jnp.zeros_like(dwk_ref)
            dwv_ref[...] = jnp.zeros_like(dwv_ref)
            dgq_ref[...] = jnp.zeros_like(dgq_ref)
            dgkv_ref[...] = jnp.zeros_like(dgkv_ref)

        cq = h_ref[:, 0:Q_RANK]
        ckv = h_ref[:, Q_RANK:Q_RANK + KV_RANK]
        c, s1, s2 = c_ref[...], s1_ref[...], s2_ref[...]
        cqh, rq = _rms(cq)
        ckvh, rkv = _rms(ckv)
        gq, gkv = gq_ref[...], gkv_ref[...]
        cqn = _mx(cqh * gq)
        ckvn = _mx(ckvh * gkv)
        dcqn = jnp.zeros((tm, Q_RANK), F32)
        dckvn = jnp.zeros((tm, KV_RANK), F32)
        dkr = jnp.zeros((tm, QK_PAD), F32)
        for hd in range(HEADS):
            dqh = _mx(_unrope(dq_ref[hd], c, s1, s2))
            dcqn = dcqn + _dot(dqh, wqt_ref[hd])
            dwq_ref[hd] += _dot_tn(cqn, dqh)
            dkh = dk_ref[hd]
            dkr = dkr + dkh
            dkh = _mx(dkh)
            dckvn = dckvn + _dot(dkh, wkt_ref[hd])
            dwk_ref[hd] += _dot_tn(ckvn, dkh)
            dvh = _mx(dv_ref[hd])
            dckvn = dckvn + _dot(dvh, wvt_ref[hd])
            dwv_ref[hd] += _dot_tn(ckvn, dvh)
        dgq_ref[...] += jnp.sum(dcqn * cqh, axis=0, keepdims=True)
        dgkv_ref[...] += jnp.sum(dckvn * ckvh, axis=0, keepdims=True)
        gd = dcqn * gq
        dh_ref[:, 0:Q_RANK] = rq * (gd - cqh * jnp.mean(gd * cqh, axis=-1, keepdims=True))
        gd = dckvn * gkv
        dh_ref[:, Q_RANK:Q_RANK + KV_RANK] = rkv * (gd - ckvh * jnp.mean(gd * ckvh, axis=-1, keepdims=True))
        dh_ref[:, Q_RANK + KV_RANK:Q_RANK + KV_RANK + QK_PAD] = _unrope(dkr, c, s1, s2) * mask_ref[...]

    full = lambda shp: pl.BlockSpec(shp, lambda i: (0,) * len(shp))
    row = lambda w: pl.BlockSpec((tm, w), lambda i: (i, 0))
    hrow = lambda w: pl.BlockSpec((HEADS, tm, w), lambda i: (0, i, 0))
    return pl.pallas_call(
        body, name="mla_prep_bwd",
        out_shape=(jax.ShapeDtypeStruct((s, 4 * LANES), F32),
                   jax.ShapeDtypeStruct((HEADS, Q_RANK, QK_PAD), F32),
                   jax.ShapeDtypeStruct((HEADS, KV_RANK, QK_PAD), F32),
                   jax.ShapeDtypeStruct((HEADS, KV_RANK, HEAD_DIM), F32),
                   jax.ShapeDtypeStruct((1, Q_RANK), F32),
                   jax.ShapeDtypeStruct((1, KV_RANK), F32)),
        grid=(s // tm,),
        in_specs=[row(4 * LANES), hrow(QK_PAD), hrow(QK_PAD), hrow(HEAD_DIM),
                  full((1, Q_RANK)), full((1, KV_RANK)),
                  full((HEADS, QK_PAD, Q_RANK)), full((HEADS, QK_PAD, KV_RANK)), full((HEADS, HEAD_DIM, KV_RANK)),
                  row(LANES), row(LANES), row(LANES), row(LANES)],
        out_specs=(row(4 * LANES), full((HEADS, Q_RANK, QK_PAD)), full((HEADS, KV_RANK, QK_PAD)),
                   full((HEADS, KV_RANK, HEAD_DIM)), full((1, Q_RANK)), full((1, KV_RANK))),
        compiler_params=_params("arbitrary"),
    )(h, dq, dk, dv, g_cq, g_ckv, wq_t, wk_t, wv_t, c_t, s1_t, s2_t, mask_t)


def _bdot(a, b, ca, cb):
    return lax.dot_general(a, b, (((ca,), (cb,)), ((0,), (0,))), preferred_element_type=F32)


def _causal_mask_t(t):
    kk = lax.broadcasted_iota(jnp.int32, (t, t), 0)
    qq = lax.broadcasted_iota(jnp.int32, (t, t), 1)
    return (qq >= kk)[None]


def _mla_attn_fwd(q, k, v_t, *, t, g):
    hds, s, _ = q.shape
    n = s // t

    def body(q_ref, k_ref, vt_ref, o_ref, lse_ref, m_sc, l_sc, acc_sc):
        qi, ki = pl.program_id(1), pl.program_id(2)

        @pl.when(ki == 0)
        def _():
            m_sc[...] = jnp.full_like(m_sc, NEG)
            l_sc[...] = jnp.zeros_like(l_sc)
            acc_sc[...] = jnp.zeros_like(acc_sc)

        def step(masked):
            sc = _bdot(k_ref[...], q_ref[...], 2, 2) * MLA_SCALE
            if masked:
                sc = jnp.where(_causal_mask_t(t), sc, NEG)
            m_prev = m_sc[...]
            m_new = jnp.maximum(m_prev, jnp.max(sc, axis=1, keepdims=True))
            p = jnp.exp(sc - m_new)
            a = jnp.exp(m_prev - m_new)
            l_sc[...] = a * l_sc[...] + jnp.sum(p, axis=1, keepdims=True)
            acc_sc[...] = a * acc_sc[...] + _bdot(vt_ref[...], _mx(p), 2, 1)
            m_sc[...] = m_new

        @pl.when(ki < qi)
        def _():
            step(False)

        @pl.when(ki == qi)
        def _():
            step(True)
            o_ref[...] = acc_sc[...] / l_sc[...]
            lse_ref[...] = m_sc[...] + jnp.log(l_sc[...])

    qspec = pl.BlockSpec((g, t, QK_PAD), lambda h, i, j: (h, i, 0))
    kspec = pl.BlockSpec((g, t, QK_PAD), lambda h, i, j: (h, jnp.minimum(i, j), 0))
    vspec = pl.BlockSpec((g, HEAD_DIM, t), lambda h, i, j: (h, 0, jnp.minimum(i, j)))
    return pl.pallas_call(
        body, name="mla_attn_fwd",
        out_shape=(jax.ShapeDtypeStruct((hds, HEAD_DIM, s), F32), jax.ShapeDtypeStruct((hds, 1, s), F32)),
        grid=(hds // g, n, n),
        in_specs=[qspec, kspec, vspec],
        out_specs=(pl.BlockSpec((g, HEAD_DIM, t), lambda h, i, j: (h, 0, i)),
                   pl.BlockSpec((g, 1, t), lambda h, i, j: (h, 0, i))),
        scratch_shapes=[pltpu.VMEM((g, 1, t), F32), pltpu.VMEM((g, 1, t), F32), pltpu.VMEM((g, HEAD_DIM, t), F32)],
        compiler_params=_params("parallel", "parallel", "arbitrary"),
    )(q, k, v_t)


def _head_rowdot(a, b, *, tm):
    s, width = a.shape
    nh = width // HEAD_DIM

    def body(a_ref, b_ref, o_ref):
        prod = a_ref[...] * b_ref[...]
        for hd in range(nh):
            o_ref[:, hd:hd + 1] = jnp.sum(prod[:, hd * HEAD_DIM:(hd + 1) * HEAD_DIM], axis=-1, keepdims=True)

    return pl.pallas_call(
        body, name="head_rowdot",
        out_shape=jax.ShapeDtypeStruct((s, nh), F32),
        grid=(s // tm,),
        in_specs=[pl.BlockSpec((tm, width), lambda i: (i, 0))] * 2,
        out_specs=pl.BlockSpec((tm, nh), lambda i: (i, 0)),
        compiler_params=_params("parallel"),
    )(a, b)


def _mla_attn_bwd(q, k, v, do, lse, dd, *, t, g):
    hds, s, _ = q.shape
    n = s // t

    def body(q_ref, k_ref, v_ref, do_ref, lse_ref, dd_ref, dq_ref, dk_ref, dv_ref, dq_sc, dk_sc, dv_sc):
        ki, qi = pl.program_id(1), pl.program_id(2)

        @pl.when(jnp.logical_and(ki == 0, qi == 0))
        def _():
            dq_sc[...] = jnp.zeros_like(dq_sc)

        @pl.when(qi == 0)
        def _():
            dk_sc[...] = jnp.zeros_like(dk_sc)
            dv_sc[...] = jnp.zeros_like(dv_sc)

        def step(masked):
            qb, kb, dob = q_ref[...], k_ref[...], do_ref[...]
            sc = _bdot(kb, qb, 2, 2) * MLA_SCALE
            if masked:
                sc = jnp.where(_causal_mask_t(t), sc, NEG)
            p = jnp.exp(sc - lse_ref[...])
            dv_sc[...] += _bdot(_mx(p), dob, 2, 1)
            dp = _bdot(v_ref[...], dob, 2, 2)
            ds = _mx(p * (dp - dd_ref[...]) * MLA_SCALE)
            dk_sc[...] += _bdot(ds, qb, 2, 1)
            dq_sc[qi] += _bdot(ds, kb, 1, 1)

        @pl.when(qi == ki)
        def _():
            step(True)

        @pl.when(qi > ki)
        def _():
            step(False)

        @pl.when(qi == n - 1)
        def _():
            dk_ref[...] = dk_sc[...]
            dv_ref[...] = dv_sc[...]

        @pl.when(jnp.logical_and(ki == n - 1, qi == n - 1))
        def _():
            for j in range(n):
                dq_ref[:, j * t:(j + 1) * t, :] = dq_sc[j]

    qs = lambda w: pl.BlockSpec((g, t, w), lambda h, j, i: (h, jnp.maximum(i, j), 0))
    ks = lambda w: pl.BlockSpec((g, t, w), lambda h, j, i: (h, j, 0))
    rowq = pl.BlockSpec((g, 1, t), lambda h, j, i: (h, 0, jnp.maximum(i, j)))
    return pl.pallas_call(
        body, name="mla_attn_bwd",
        out_shape=(jax.ShapeDtypeStruct((hds, s, QK_PAD), F32), jax.ShapeDtypeStruct((hds, s, QK_PAD), F32),
                   jax.ShapeDtypeStruct((hds, s, HEAD_DIM), F32)),
        grid=(hds // g, n, n),
        in_specs=[qs(QK_PAD), ks(QK_PAD), ks(HEAD_DIM), qs(HEAD_DIM), rowq, rowq],
        out_specs=(pl.BlockSpec((g, s, QK_PAD), lambda h, j, i: (h, 0, 0)), ks(QK_PAD), ks(HEAD_DIM)),
        scratch_shapes=[pltpu.VMEM((n, g, t, QK_PAD), F32), pltpu.VMEM((g, t, QK_PAD), F32), pltpu.VMEM((g, t, HEAD_DIM), F32)],
        compiler_params=_params("parallel", "arbitrary", "arbitrary"),
    )(q, k, v, do, lse, dd)


def _perm(a, dil):
    if dil == 1:
        return a
    hds, s, e = a.shape
    return a.reshape(hds, s // dil, dil, e).transpose(0, 2, 1, 3).reshape(hds, s, e)


def _unperm(a, dil):
    if dil == 1:
        return a
    hds, s, e = a.shape
    return a.reshape(hds, dil, s // dil, e).transpose(0, 2, 1, 3).reshape(hds, s, e)


def _perm_row(a, dil):
    if dil == 1:
        return a
    hds, _, s = a.shape
    return a.reshape(hds, s // dil, dil).transpose(0, 2, 1).reshape(hds, 1, s)


def _unperm_row(a, dil):
    if dil == 1:
        return a
    hds, _, s = a.shape
    return a.reshape(hds, dil, s // dil).transpose(0, 2, 1).reshape(hds, 1, s)


def _dil_bias(dil):
    slopes = 2.0 ** (-8.0 * jnp.arange(1, HEADS + 1, dtype=F32) / HEADS)
    ik = jnp.arange(DIL_BLOCK)[:, None]
    iq = jnp.arange(DIL_BLOCK)[None, :]
    off_c = iq - ik
    off_p = iq - ik + DIL_BLOCK
    b_c = -slopes[:, None, None] * (off_c * dil).astype(F32)[None]
    b_p = -slopes[:, None, None] * (off_p * dil).astype(F32)[None]
    b_c = jnp.where((off_c >= 0)[None], b_c, NEG)
    b_p = jnp.where((off_p <= DIL_BLOCK)[None], b_p, NEG)
    return b_c, b_p


def _dil_fwd(q, k, v, dil, *, name):
    hds, s, e = q.shape
    blk = DIL_BLOCK
    nblk = s // blk
    nb = nblk // dil
    b_c, b_p = _dil_bias(dil)

    def body(q_ref, kc_ref, kp_ref, vc_ref, vp_ref, bc_ref, bp_ref, o_ref, lse_ref):
        b = pl.program_id(0)
        first = (b % nb) == 0
        qb = q_ref[...]
        s_c = _bdot(kc_ref[...], qb, 2, 2) * DIL_SCALE + bc_ref[...]
        s_p = jnp.where(first, NEG, _bdot(kp_ref[...], qb, 2, 2) * DIL_SCALE + bp_ref[...])
        m = jnp.maximum(jnp.max(s_c, axis=1, keepdims=True), jnp.max(s_p, axis=1, keepdims=True))
        p_c = jnp.exp(s_c - m)
        p_p = jnp.exp(s_p - m)
        l = jnp.sum(p_c, axis=1, keepdims=True) + jnp.sum(p_p, axis=1, keepdims=True)
        o = _bdot(_mx(p_c), vc_ref[...], 1, 1) + _bdot(_mx(p_p), vp_ref[...], 1, 1)
        o_ref[...] = o / jnp.swapaxes(l, 1, 2)
        lse_ref[...] = m + jnp.log(l)

    cur = lambda w: pl.BlockSpec((hds, blk, w), lambda b: (0, b, 0))
    prev = lambda w: pl.BlockSpec((hds, blk, w), lambda b: (0, jnp.maximum(b - 1, 0), 0))
    bias = pl.BlockSpec((hds, blk, blk), lambda b: (0, 0, 0))
    return pl.pallas_call(
        body, name=name,
        out_shape=(jax.ShapeDtypeStruct((hds, s, e), F32), jax.ShapeDtypeStruct((hds, 1, s), F32)),
        grid=(nblk,),
        in_specs=[cur(e), cur(e), prev(e), cur(e), prev(e), bias, bias],
        out_specs=(cur(e), pl.BlockSpec((hds, 1, blk), lambda b: (0, 0, b))),
        compiler_params=_params("parallel"),
    )(q, k, k, v, v, b_c, b_p)


def _dil_combine(os_, lses, *, ts):
    hds, s, e = os_[0].shape

    def body(o0, o1, o2, l0, l1, l2, o_ref, l_ref):
        a0, a1, a2 = l0[...], l1[...], l2[...]
        m = jnp.maximum(jnp.maximum(a0, a1), a2)
        e0, e1, e2 = jnp.exp(a0 - m), jnp.exp(a1 - m), jnp.exp(a2 - m)
        tot = e0 + e1 + e2
        col = lambda w: jnp.swapaxes(w, 1, 2)
        o_ref[...] = (col(e0 / tot) * o0[...] + col(e1 / tot) * o1[...]) + col(e2 / tot) * o2[...]
        l_ref[...] = m + jnp.log(tot)

    spec = pl.BlockSpec((hds, ts, e), lambda i: (0, i, 0))
    rspec = pl.BlockSpec((hds, 1, ts), lambda i: (0, 0, i))
    return pl.pallas_call(
        body, name="dil_combine",
        out_shape=(jax.ShapeDtypeStruct((hds, s, e), F32), jax.ShapeDtypeStruct((hds, 1, s), F32)),
        grid=(s // ts,),
        in_specs=[spec] * 3 + [rspec] * 3,
        out_specs=(spec, rspec),
        compiler_params=_params("parallel"),
    )(*os_, *lses)


def _dil_bwd(q, k, v, do, lj, dd, dil, *, name):
    hds, s, e = q.shape
    blk = DIL_BLOCK
    nblk = s // blk
    nb = nblk // dil
    b_c, b_p = _dil_bias(dil)

    def body(q_ref, qn_ref, kc_ref, kp_ref, vc_ref, vp_ref, do_ref, don_ref, l_ref, ln_ref, d_ref, dn_ref,
             bc_ref, bp_ref, dq_ref, dk_ref, dv_ref):
        b = pl.program_id(0)
        first = (b % nb) == 0
        nxt = jnp.logical_and(b + 1 < nblk, ((b + 1) % nb) != 0)
        qb, kc, kp, vc, vp = q_ref[...], kc_ref[...], kp_ref[...], vc_ref[...], vp_ref[...]
        dob = _mx(do_ref[...])
        bc, bp = bc_ref[...], bp_ref[...]
        p_c = jnp.exp(_bdot(kc, qb, 2, 2) * DIL_SCALE + bc - l_ref[...])
        p_p = jnp.where(first, 0.0, jnp.exp(_bdot(kp, qb, 2, 2) * DIL_SCALE + bp - l_ref[...]))
        ds_c = _mx(p_c * (_bdot(vc, dob, 2, 2) - d_ref[...]) * DIL_SCALE)
        ds_p = _mx(p_p * (_bdot(vp, dob, 2, 2) - d_ref[...]) * DIL_SCALE)
        dq_ref[...] = _bdot(ds_c, kc, 1, 1) + _bdot(ds_p, kp, 1, 1)
        qn = qn_ref[...]
        donb = _mx(don_ref[...])
        p_n = jnp.where(nxt, jnp.exp(_bdot(kc, qn, 2, 2) * DIL_SCALE + bp - ln_ref[...]), 0.0)
        ds_n = _mx(p_n * (_bdot(vc, donb, 2, 2) - dn_ref[...]) * DIL_SCALE)
        dk_ref[...] = _bdot(ds_c, qb, 2, 1) + _bdot(ds_n, qn, 2, 1)
        dv_ref[...] = _bdot(_mx(p_c), dob, 2, 1) + _bdot(_mx(p_n), donb, 2, 1)

    cur = lambda w: pl.BlockSpec((hds, blk, w), lambda b: (0, b, 0))
    prev = lambda w: pl.BlockSpec((hds, blk, w), lambda b: (0, jnp.maximum(b - 1, 0), 0))
    nxt_ = lambda w: pl.BlockSpec((hds, blk, w), lambda b: (0, jnp.minimum(b + 1, nblk - 1), 0))
    rcur = pl.BlockSpec((hds, 1, blk), lambda b: (0, 0, b))
    rnxt = pl.BlockSpec((hds, 1, blk), lambda b: (0, 0, jnp.minimum(b + 1, nblk - 1)))
    bias = pl.BlockSpec((hds, blk, blk), lambda b: (0, 0, 0))
    out = jax.ShapeDtypeStruct((hds, s, e), F32)
    return pl.pallas_call(
        body, name=name,
        out_shape=(out, out, out),
        grid=(nblk,),
        in_specs=[cur(e), nxt_(e), cur(e), prev(e), cur(e), prev(e), cur(e), nxt_(e),
                  rcur, rnxt, rcur, rnxt, bias, bias],
        out_specs=(cur(e), cur(e), cur(e)),
        compiler_params=_params("parallel"),
    )(q, q, k, k, v, v, do, do, lj, lj, dd, dd, b_c, b_p)


def _add3(a, b, c, *, ts, name):
    hds, s, e = a.shape

    def body(a_ref, b_ref, c_ref, o_ref):
        o_ref[...] = (a_ref[...] + b_ref[...]) + c_ref[...]

    spec = pl.BlockSpec((hds, ts, e), lambda i: (0, i, 0))
    return pl.pallas_call(
        body, name=name,
        out_shape=jax.ShapeDtypeStruct((hds, s, e), F32),
        grid=(s // ts,),
        in_specs=[spec] * 3, out_specs=spec,
        compiler_params=_params("parallel"),
    )(a, b, c)


def _ln_fwd(z, g, b):
    mu = jnp.mean(z, axis=-1, keepdims=True)
    zc = z - mu
    var = jnp.mean(zc * zc, axis=-1, keepdims=True)
    rstd = lax.rsqrt(var + LN_EPS)
    xhat = zc * rstd
    return xhat * g + b, xhat, rstd


def _ln_bwd(dy, xhat, rstd, g):
    dxh = dy * g
    return rstd * (dxh - jnp.mean(dxh, axis=-1, keepdims=True) - xhat * jnp.mean(dxh * xhat, axis=-1, keepdims=True))


def _out_ln1(attn, w_o, x, g, b, *, tm):
    s = x.shape[0]

    def body(a_ref, w_ref, x_ref, g_ref, b_ref, x1_ref, xh_ref, r_ref):
        z = DN_ALPHA * x_ref[...] + _dot(a_ref[...], w_ref[...])
        y, xhat, rstd = _ln_fwd(z, g_ref[...], b_ref[...])
        x1_ref[...] = y
        xh_ref[...] = xhat
        r_ref[...] = rstd

    row = lambda w: pl.BlockSpec((tm, w), lambda i: (i, 0))
    full = lambda shp: pl.BlockSpec(shp, lambda i: (0,) * len(shp))
    act = jax.ShapeDtypeStruct((s, D_MODEL), F32)
    return pl.pallas_call(
        body, name="out_ln1",
        out_shape=(act, act, jax.ShapeDtypeStruct((s, 1), F32)),
        grid=(s // tm,),
        in_specs=[row(D_MODEL), full((D_MODEL, D_MODEL)), row(D_MODEL), full((1, D_MODEL)), full((1, D_MODEL))],
        out_specs=(row(D_MODEL), row(D_MODEL), row(1)),
        compiler_params=_params("parallel"),
    )(attn, w_o, x, g, b)


def _down_ln2_loss(act, w_down, x1, g, b, target, *, tm):
    s = x1.shape[0]

    def body(a_ref, w_ref, x1_ref, g_ref, b_ref, t_ref, dz_ref, loss_ref, dg_ref, db_ref):
        i = pl.program_id(0)

        @pl.when(i == 0)
        def _():
            loss_ref[...] = jnp.zeros_like(loss_ref)
            dg_ref[...] = jnp.zeros_like(dg_ref)
            db_ref[...] = jnp.zeros_like(db_ref)

        gam = g_ref[...]
        z = DN_ALPHA * x1_ref[...] + _dot(a_ref[...], w_ref[...])
        y, xhat, rstd = _ln_fwd(z, gam, b_ref[...])
        err = y - t_ref[...]
        loss_ref[...] += 0.5 * jnp.sum(jnp.mean(err * err, axis=-1, keepdims=True))
        dy = err * (1.0 / D_MODEL)
        dg_ref[...] += jnp.sum(dy * xhat, axis=0, keepdims=True)
        db_ref[...] += jnp.sum(dy, axis=0, keepdims=True)
        dz_ref[...] = _ln_bwd(dy, xhat, rstd, gam)

    row = lambda w: pl.BlockSpec((tm, w), lambda i: (i, 0))
    full = lambda shp: pl.BlockSpec(shp, lambda i: (0,) * len(shp))
    vec = jax.ShapeDtypeStruct((1, D_MODEL), F32)
    return pl.pallas_call(
        body, name="down_ln2_loss",
        out_shape=(jax.ShapeDtypeStruct((s, D_MODEL), F32), jax.ShapeDtypeStruct((1, LANES), F32), vec, vec),
        grid=(s // tm,),
        in_specs=[row(D_FF), full((D_FF, D_MODEL)), row(D_MODEL), full((1, D_MODEL)), full((1, D_MODEL)), row(D_MODEL)],
        out_specs=(row(D_MODEL), full((1, LANES)), full((1, D_MODEL)), full((1, D_MODEL))),
        compiler_params=_params("arbitrary"),
    )(act, w_down, x1, g, b, target)


def _up_bwd_ln1(du_a, du_g, w_up_t, dz2, xhat1, rstd1, g, *, tm):
    s = dz2.shape[0]

    def body(dua_ref, dug_ref, wa_ref, wg_ref, dz2_ref, xh_ref, r_ref, g_ref, dz1_ref, dg_ref, db_ref):
        i = pl.program_id(0)

        @pl.when(i == 0)
        def _():
            dg_ref[...] = jnp.zeros_like(dg_ref)
            db_ref[...] = jnp.zeros_like(db_ref)

        dx1 = DN_ALPHA * dz2_ref[...] + (_dot(dua_ref[...], wa_ref[...]) + _dot(dug_ref[...], wg_ref[...]))
        xhat = xh_ref[...]
        dg_ref[...] += jnp.sum(dx1 * xhat, axis=0, keepdims=True)
        db_ref[...] += jnp.sum(dx1, axis=0, keepdims=True)
        dz1_ref[...] = _ln_bwd(dx1, xhat, r_ref[...], g_ref[...])

    row = lambda w: pl.BlockSpec((tm, w), lambda i: (i, 0))
    full = lambda shp: pl.BlockSpec(shp, lambda i: (0,) * len(shp))
    vec = jax.ShapeDtypeStruct((1, D_MODEL), F32)
    return pl.pallas_call(
        body, name="up_bwd_ln1",
        out_shape=(jax.ShapeDtypeStruct((s, D_MODEL), F32), vec, vec),
        grid=(s // tm,),
        in_specs=[row(D_FF), row(D_FF),
                  pl.BlockSpec((D_FF, D_MODEL), lambda i: (0, 0)), pl.BlockSpec((D_FF, D_MODEL), lambda i: (1, 0)),
                  row(D_MODEL), row(D_MODEL), row(1), full((1, D_MODEL))],
        out_specs=(row(D_MODEL), full((1, D_MODEL)), full((1, D_MODEL))),
        compiler_params=_params("arbitrary"),
    )(du_a, du_g, w_up_t, w_up_t, dz2, xhat1, rstd1, g)


GELU_C = math.sqrt(2.0 / math.pi)


def _gelu(x):
    cdf = 0.5 * (1.0 + jnp.tanh(GELU_C * (x + 0.044715 * (x * x * x))))
    return x * cdf


def _gelu_grad(x):
    t = jnp.tanh(GELU_C * (x + 0.044715 * (x * x * x)))
    return 0.5 * (1.0 + t) + 0.5 * x * (1.0 - t * t) * (GELU_C * (1.0 + 3.0 * 0.044715 * (x * x)))


def _shift_down(u, halo):
    t = u.shape[0]
    row = lax.broadcasted_iota(jnp.int32, u.shape, 0)
    h7, h6 = halo[7:8, :], halo[6:7, :]
    s1 = jnp.where(row == 0, h7, pltpu.roll(u, 1, 0))
    s2 = jnp.where(row == 0, h6, jnp.where(row == 1, h7, pltpu.roll(u, 2, 0)))
    return s1, s2


def _shift_up(d, nxt):
    t = d.shape[0]
    row = lax.broadcasted_iota(jnp.int32, d.shape, 0)
    n0, n1 = nxt[0:1, :], nxt[1:2, :]
    s1 = jnp.where(row == t - 1, n0, pltpu.roll(d, t - 1, 0))
    s2 = jnp.where(row == t - 1, n1, jnp.where(row == t - 2, n0, pltpu.roll(d, t - 2, 0)))
    return s1, s2


def _conv(u, s1, s2, w, b):
    return ((b + w[0:1, :] * s2) + w[1:2, :] * s1) + w[2:3, :] * u


def _gate_fwd(u, conv_w, conv_b, *, tm, tn):
    s = u.shape[0]
    nj = D_FF // tn
    hb = tm // SUBLANES

    def body(ua_ref, ug_ref, ha_ref, hg_ref, wa_ref, wg_ref, ba_ref, bg_ref, o_ref):
        keep = pl.program_id(0) > 0
        ua, ug = ua_ref[...], ug_ref[...]
        ha = jnp.where(keep, ha_ref[...], 0.0)
        hg = jnp.where(keep, hg_ref[...], 0.0)
        a = _conv(ua, *_shift_down(ua, ha), wa_ref[...], ba_ref[...])
        g = _conv(ug, *_shift_down(ug, hg), wg_ref[...], bg_ref[...])
        o_ref[...] = (_gelu(g) * a).astype(o_ref.dtype)

    main = lambda off: pl.BlockSpec((tm, tn), lambda i, j: (i, j + off))
    halo = lambda off: pl.BlockSpec((SUBLANES, tn), lambda i, j: (jnp.maximum(i * hb - 1, 0), j + off))
    wspec = lambda r, off: pl.BlockSpec((r, tn), lambda i, j: (0, j + off))
    return pl.pallas_call(
        body, name="gate_fwd",
        out_shape=jax.ShapeDtypeStruct((s, D_FF), MXU_DTYPE),
        grid=(s // tm, nj),
        in_specs=[main(0), main(nj), halo(0), halo(nj), wspec(3, 0), wspec(3, nj), wspec(1, 0), wspec(1, nj)],
        out_specs=pl.BlockSpec((tm, tn), lambda i, j: (i, j)),
        compiler_params=_params("parallel", "parallel"),
    )(u, u, u, u, conv_w, conv_w, conv_b, conv_b)


def _gate_bwd(u, dact, conv_w, conv_b, *, tm, tn):
    s = u.shape[0]
    nj = D_FF // tn
    ni = s // tm
    hb = tm // SUBLANES

    def body(ua_ref, ug_ref, ha_ref, hg_ref, na_ref, ng_ref, d_ref, dn_ref, wa_ref, wg_ref, ba_ref, bg_ref,
             dua_ref, dug_ref, dwa_ref, dwg_ref, dba_ref, dbg_ref):
        i = pl.program_id(1)

        @pl.when(i == 0)
        def _():
            for r in (dwa_ref, dwg_ref, dba_ref, dbg_ref):
                r[...] = jnp.zeros_like(r)

        wa, wg, ba, bg = wa_ref[...], wg_ref[...], ba_ref[...], bg_ref[...]
        ua, ug = ua_ref[...], ug_ref[...]
        ha = jnp.where(i > 0, ha_ref[...], 0.0)
        hg = jnp.where(i > 0, hg_ref[...], 0.0)
        sa1, sa2 = _shift_down(ua, ha)
        sg1, sg2 = _shift_down(ug, hg)
        a = _conv(ua, sa1, sa2, wa, ba)
        g = _conv(ug, sg1, sg2, wg, bg)
        d = d_ref[...]
        dya = d * _gelu(g)
        dyg = d * a * _gelu_grad(g)
        na, ng = na_ref[...], ng_ref[...]
        a_n = _conv(na, *_shift_down(na, ua[tm - SUBLANES:, :]), wa, ba)
        g_n = _conv(ng, *_shift_down(ng, ug[tm - SUBLANES:, :]), wg, bg)
        dn = jnp.where(i < ni - 1, dn_ref[...], 0.0)
        dya_n = dn * _gelu(g_n)
        dyg_n = dn * a_n * _gelu_grad(g_n)
        da1, da2 = _shift_up(dya, dya_n)
        dg1, dg2 = _shift_up(dyg, dyg_n)
        dua_ref[...] = (wa[2:3, :] * dya + wa[1:2, :] * da1 + wa[0:1, :] * da2).astype(dua_ref.dtype)
        dug_ref[...] = (wg[2:3, :] * dyg + wg[1:2, :] * dg1 + wg[0:1, :] * dg2).astype(dug_ref.dtype)
        ssum = lambda v: jnp.sum(v, axis=0, keepdims=True)
        dwa_ref[...] += jnp.concatenate([ssum(dya * sa2), ssum(dya * sa1), ssum(dya * ua)], axis=0)
        dwg_ref[...] += jnp.concatenate([ssum(dyg * sg2), ssum(dyg * sg1), ssum(dyg * ug)], axis=0)
        dba_ref[...] += ssum(dya)
        dbg_ref[...] += ssum(dyg)

    main = lambda off: pl.BlockSpec((tm, tn), lambda j, i: (i, j + off))
    halo = lambda off: pl.BlockSpec((SUBLANES, tn), lambda j, i: (jnp.maximum(i * hb - 1, 0), j + off))
    nxt = lambda off: pl.BlockSpec((SUBLANES, tn), lambda j, i: (jnp.minimum((i + 1) * hb, s // SUBLANES - 1), j + off))
    wspec = lambda r, off: pl.BlockSpec((r, tn), lambda j, i: (0, j + off))
    return pl.pallas_call(
        body, name="gate_bwd",
        out_shape=(jax.ShapeDtypeStruct((s, D_FF), MXU_DTYPE), jax.ShapeDtypeStruct((s, D_FF), MXU_DTYPE),
                   jax.ShapeDtypeStruct((3, D_FF), F32), jax.ShapeDtypeStruct((3, D_FF), F32),
                   jax.ShapeDtypeStruct((1, D_FF), F32), jax.ShapeDtypeStruct((1, D_FF), F32)),
        grid=(nj, ni),
        in_specs=[main(0), main(nj), halo(0), halo(nj), nxt(0), nxt(nj), main(0), nxt(0),
                  wspec(3, 0), wspec(3, nj), wspec(1, 0), wspec(1, nj)],
        out_specs=(main(0), main(0), wspec(3, 0), wspec(3, 0), wspec(1, 0), wspec(1, 0)),
        compiler_params=_params("parallel", "arbitrary"),
    )(u, u, u, u, u, u, dact, dact, conv_w, conv_w, conv_b, conv_b)


def _prep_weights(w_in, w_uq, w_uk, w_uv, w_o, w_up, w_down):
    c = lambda a: a.astype(MXU_DTYPE)
    w_in = c(w_in)
    z = lambda w: jnp.zeros((D_MODEL, w), MXU_DTYPE)
    r0 = Q_RANK + KV_RANK
    w_in_ext = jnp.concatenate([w_in[:, :r0], z(NOPE), w_in[:, r0:r0 + ROPE], z(32), w_in[:, r0 + ROPE:]], axis=1)
    wq = jnp.pad(c(w_uq).transpose(1, 0, 2), ((0, 0), (0, 0), (0, QK_PAD - NOPE - ROPE)))
    wk = jnp.pad(c(w_uk).transpose(1, 0, 2), ((0, 0), (0, 0), (0, QK_PAD - NOPE)))
    wv = c(w_uv).transpose(1, 0, 2)
    t3 = lambda a: a.transpose(0, 2, 1)
    w_o, w_up, w_down = c(w_o), c(w_up), c(w_down)
    return dict(w_in=w_in_ext, w_in_t=w_in_ext.T, wq=wq, wq_t=t3(wq), wk=wk, wk_t=t3(wk), wv=wv, wv_t=t3(wv),
                w_o=w_o, w_o_t=w_o.T, w_up=w_up, w_up_t=w_up.T, w_down=w_down, w_down_t=w_down.T)


def _local_step(x, target, w, g_cq, g_ckv, ln1_g, ln1_b, conv_w, conv_b, ln2_g, ln2_b):
    s = x.shape[0]
    tabs = _rope_tables(s)
    r2 = lambda a: a.reshape(1, -1)
    heads = lambda a: a.reshape(s, HEADS, HEAD_DIM).transpose(1, 0, 2)
    unheads = lambda a: a.transpose(1, 0, 2).reshape(s, HEADS * HEAD_DIM)
    cb = r2(conv_b)
    dils = [d for _, d in DIL_PAIRS]

    h = _mm_nn(x, w["w_in"], name="in_proj", tm=512, tn=1024, tk=D_MODEL)
    q, k, v, v_t = _mla_prep_fwd(h, r2(g_cq), r2(g_ckv), w["wq"], w["wk"], w["wv"], w["wv_t"], tabs, tm=256)
    o_mla_t, lse_mla = _mla_attn_fwd(q, k, v_t, t=512, g=HEADS)
    qd, kd, vd = (_mx(heads(h[:, 512 * (i + 1):512 * (i + 2)])) for i in range(3))
    qp = [_perm(qd, d) for d in dils]
    kp = [_perm(kd, d) for d in dils]
    vp = [_perm(vd, d) for d in dils]
    o_bs, lse_bs = [], []
    for i, d in enumerate(dils):
        o_b, l_b = _dil_fwd(qp[i], kp[i], vp[i], d, name=f"dil_fwd_{d}")
        o_bs.append(_unperm(o_b, d))
        lse_bs.append(_unperm_row(l_b, d))
    o_dil, lj = _dil_combine(o_bs, lse_bs, ts=512)
    attn_f = jnp.concatenate([o_mla_t.transpose(2, 0, 1).reshape(s, HEADS * HEAD_DIM), unheads(o_dil)], axis=1)
    attn = _mx(attn_f)
    x1, xhat1, rstd1 = _out_ln1(attn, w["w_o"], x, r2(ln1_g), r2(ln1_b), tm=256)
    u = _mm_nn(x1, w["w_up"], name="up_proj", tm=512, tn=1408, tk=D_MODEL)
    act = _gate_fwd(u, conv_w, cb, tm=256, tn=1408)
    dz2, loss, dg2, db2 = _down_ln2_loss(act, w["w_down"], x1, r2(ln2_g), r2(ln2_b), target, tm=256)

    dact = _mm_nn(dz2, w["w_down_t"], name="down_bwd", tm=512, tn=1408, tk=D_MODEL)
    dw_down = _mm_tn(act, dz2, name="dw_down", tm=1408, tn=D_MODEL, ts=512)
    du_a, du_g, dcw_a, dcw_g, dcb_a, dcb_g = _gate_bwd(u, dact, conv_w, cb, tm=256, tn=1408)
    dz1, dg1, db1 = _up_bwd_ln1(du_a, du_g, w["w_up_t"], dz2, xhat1, rstd1, r2(ln1_g), tm=256)
    dw_up = jnp.concatenate([_mm_tn(x1, du_a, name="dw_up_a", tm=D_MODEL, tn=1408, ts=512),
                             _mm_tn(x1, du_g, name="dw_up_g", tm=D_MODEL, tn=1408, ts=512)], axis=1)
    dattn = _mm_nn(dz1, w["w_o_t"], name="o_bwd", tm=512, tn=D_MODEL, tk=D_MODEL)
    dw_o = _mm_tn(attn, dz1, name="dw_o", tm=D_MODEL, tn=D_MODEL, ts=512)
    do_mla, do_dil = _mx(heads(dattn[:, :512])), heads(dattn[:, 512:])
    dd_all = _head_rowdot(dattn, attn_f, tm=256).T
    dd_mla, dd_dil = dd_all[:HEADS].reshape(HEADS, 1, s), dd_all[HEADS:].reshape(HEADS, 1, s)
    dq, dk, dv = _mla_attn_bwd(q, k, v, do_mla, lse_mla, dd_mla, t=512, g=4)
    parts = []
    for i, d in enumerate(dils):
        g3 = _dil_bwd(qp[i], kp[i], vp[i], _perm(do_dil, d), _perm_row(lj, d), _perm_row(dd_dil, d), d, name=f"dil_bwd_{d}")
        parts.append([_unperm(g, d) for g in g3])
    dqd, dkd, dvd = (_add3(parts[0][j], parts[1][j], parts[2][j], ts=512, name=f"dil_sum_{j}") for j in range(3))
    dh_mla, dwq, dwk, dwv, dgq, dgkv = _mla_prep_bwd(h, dq, dk, dv, r2(g_cq), r2(g_ckv),
                                                     w["wq_t"], w["wk_t"], w["wv_t"], tabs, tm=256)
    dh = _mx(jnp.concatenate([dh_mla, unheads(dqd), unheads(dkd), unheads(dvd)], axis=1))
    grad_x = _mm_nn(dh, w["w_in_t"], name="in_bwd", tm=512, tn=D_MODEL, tk=D_MODEL, add=dz1, add_scale=DN_ALPHA)
    dw_ext = _mm_tn(x, dh, name="dw_in", tm=D_MODEL, tn=1024, ts=512)
    r0 = Q_RANK + KV_RANK
    grads = dict(
        w_in=jnp.concatenate([dw_ext[:, :r0], dw_ext[:, r0 + NOPE:r0 + NOPE + ROPE], dw_ext[:, 512:]], axis=1),
        g_cq=dgq[0], g_ckv=dgkv[0],
        w_uq=dwq[:, :, :NOPE + ROPE].transpose(1, 0, 2),
        w_uk=dwk[:, :, :NOPE].transpose(1, 0, 2),
        w_uv=dwv.transpose(1, 0, 2),
        w_o=dw_o, ln1_g=dg1[0], ln1_b=db1[0], w_up=dw_up,
        conv_w=jnp.concatenate([dcw_a, dcw_g], axis=1), conv_b=jnp.concatenate([dcb_a, dcb_g], axis=1)[0],
        w_down=dw_down, ln2_g=dg2[0], ln2_b=db2[0])
    return loss[0, 0], grad_x, grads


N_CHIPS = 4
SHARDED = ("w_in", "w_uq", "w_o", "w_up", "conv_w", "w_down")
COL_SHARDED = ("w_in", "w_up", "conv_w")
SHARD_SHAPE = dict(w_in=(D_MODEL, IN_WIDTH // 4), w_uq=(Q_RANK // 4, HEADS, NOPE + ROPE), w_o=(D_MODEL // 4, D_MODEL),
                   w_up=(D_MODEL, 2 * D_FF // 4), conv_w=(3, 2 * D_FF // 4), w_down=(D_FF // 4, D_MODEL))
SMALL = ("g_cq", "g_ckv", "w_uk", "w_uv", "ln1_g", "ln1_b", "conv_b", "ln2_g", "ln2_b")
SMALL_SHAPE = dict(g_cq=(Q_RANK,), g_ckv=(KV_RANK,), w_uk=(KV_RANK, HEADS, NOPE), w_uv=(KV_RANK, HEADS, HEAD_DIM),
                   ln1_g=(D_MODEL,), ln1_b=(D_MODEL,), conv_b=(2 * D_FF,), ln2_g=(D_MODEL,), ln2_b=(D_MODEL,))
BIG = ("w_in", "w_uq", "w_o", "w_up", "w_down")
BIG_2D = dict(w_in=(D_MODEL, IN_WIDTH // 4), w_uq=(Q_RANK // 4, HEADS * (NOPE + ROPE)), w_o=(D_MODEL // 4, D_MODEL),
              w_up=(D_MODEL, 2 * D_FF // 4), w_down=(D_FF // 4, D_MODEL))
SMALL_G = SMALL + ("conv_w",)
SMALL_G_SHAPE = {**SMALL_SHAPE, "conv_w": (3, 2 * D_FF)}
SMALL_U_SHAPE = {**SMALL_SHAPE, "conv_w": (3, 2 * D_FF // 4)}


def _size(shape):
    return math.prod(shape)


def _padded_rows(n_elems, mult):
    return -(-n_elems // (LANES * mult)) * mult


SHARD_ROWS = {n: _padded_rows(_size(SHARD_SHAPE[n]), SUBLANES) for n in SHARDED}
R_SMALL = -(-sum(_size(SMALL_G_SHAPE[n]) for n in SMALL_G) // (LANES * LANES)) * LANES
GATHERED = ("w_in", "w_uq", "w_o", "w_up", "w_down")
R_GATHER = sum(SHARD_ROWS[n] for n in GATHERED)


def _rows(a, rows=None):
    flat = a.reshape(-1)
    rows = -(-flat.shape[0] // LANES) if rows is None else rows
    return jnp.pad(flat, (0, rows * LANES - flat.shape[0])).reshape(rows, LANES)


def _blocked(name, g):
    r, c = BIG_2D[name]
    a = g.reshape(r, N_CHIPS, c).transpose(1, 0, 2) if name in COL_SHARDED else g.reshape(N_CHIPS, r, c)
    return a.reshape(N_CHIPS, 2, r // 2, c)


def _pack_flat(t, names):
    return _rows(jnp.concatenate([t[n].astype(F32).reshape(-1) for n in names]), R_SMALL)


def _unpack_flat(buf, names, shapes):
    flat, out, r = buf.reshape(-1), {}, 0
    for n in names:
        out[n] = flat[r:r + _size(shapes[n])].reshape(shapes[n])
        r += _size(shapes[n])
    return out


def _from_chip_blocks(name, blocks):
    shp = SHARD_SHAPE[name]
    a = blocks.reshape(N_CHIPS, -1)[:, :_size(shp)].reshape((N_CHIPS,) + shp)
    if name in COL_SHARDED:
        return a.transpose(1, 0, 2).reshape(shp[0], N_CHIPS * shp[1])
    return a.reshape((N_CHIPS * shp[0],) + shp[1:])


ANY = pl.BlockSpec(memory_space=pl.ANY)
COMM_PARAMS = pltpu.CompilerParams(has_side_effects=True)


def _coords():
    return lax.axis_index("x"), lax.axis_index("y"), lax.axis_index("c")


def _other_chips(x, y):
    return [(1 - x, y), (x, 1 - y), (1 - x, 1 - y)]


def _remote(src, dst, send_sems, recv_sems, k, to):
    return pltpu.make_async_remote_copy(src_ref=src, dst_ref=dst, send_sem=send_sems.at[k], recv_sem=recv_sems.at[k],
                                        device_id=to, device_id_type=MESH)


def _gather_weights(wp, cwp):
    def body(wp_ref, cw_ref, wout_ref, cwout_ref, send_sems, recv_sems):
        x, y, c = _coords()
        me = 2 * x + y
        sib = (x, y, 1 - c)
        chips = _other_chips(x, y)
        sends = [_remote(wp_ref.at[c], wout_ref.at[me, c], send_sems, recv_sems, j, (px, py, c))
                 for j, (px, py) in enumerate(chips)]
        sends += [_remote(cw_ref, cwout_ref.at[me], send_sems, recv_sems, 3 + j, (px, py, c))
                  for j, (px, py) in enumerate(chips)]
        for cp in sends:
            cp.start()
        for j, (px, py) in enumerate(chips):
            k = 2 * px + py
            _remote(wp_ref.at[c], wout_ref.at[k, c], send_sems, recv_sems, j, (px, py, c)).wait_recv()
            fwd = _remote(wout_ref.at[k, c], wout_ref.at[k, c], send_sems, recv_sems, 6 + j, sib)
            fwd.start()
            sends.append(fwd)
        for j, (px, py) in enumerate(chips):
            k = 2 * px + py
            _remote(cw_ref, cwout_ref.at[k], send_sems, recv_sems, 3 + j, (px, py, c)).wait_recv()
            _remote(wout_ref.at[k, 1 - c], wout_ref.at[k, 1 - c], send_sems, recv_sems, 6 + j, sib).wait_recv()
        for cp in sends:
            cp.wait_send()

    return pl.pallas_call(
        body, name="gather_weights",
        out_shape=(jax.ShapeDtypeStruct((N_CHIPS,) + wp.shape, wp.dtype), jax.ShapeDtypeStruct((N_CHIPS,) + cwp.shape, cwp.dtype)),
        in_specs=[ANY, ANY], out_specs=(ANY, ANY),
        scratch_shapes=[pltpu.SemaphoreType.DMA((9,)), pltpu.SemaphoreType.DMA((9,))],
        compiler_params=COMM_PARAMS,
    )(wp, cwp)


def _exchange_sibling_halves(gs, gr):
    n = len(gs)

    def body(*refs):
        gs_refs, gr_ref, os_refs, or_ref = refs[:n], refs[n], refs[n + 1:2 * n + 1], refs[2 * n + 1]
        send_sems, recv_sems = refs[2 * n + 2:]
        x, y, c = _coords()
        sib = (x, y, 1 - c)
        cps = [_remote(gs_refs[t].at[k, 1 - c], os_refs[t].at[k], send_sems, recv_sems, t * N_CHIPS + k, sib)
               for t in range(n) for k in range(N_CHIPS)]
        cps.append(_remote(gr_ref, or_ref, send_sems, recv_sems, n * N_CHIPS, sib))
        for cp in cps:
            cp.start()
        for cp in cps:
            cp.wait_recv()
        for cp in cps:
            cp.wait_send()

    n_sem = n * N_CHIPS + 1
    return pl.pallas_call(
        body, name="exchange_sibling_halves",
        out_shape=tuple(jax.ShapeDtypeStruct((N_CHIPS,) + a.shape[2:], F32) for a in gs) + (jax.ShapeDtypeStruct(gr.shape, F32),),
        in_specs=[ANY] * (n + 1), out_specs=(ANY,) * (n + 1),
        scratch_shapes=[pltpu.SemaphoreType.DMA((n_sem,)), pltpu.SemaphoreType.DMA((n_sem,))],
        compiler_params=COMM_PARAMS,
    )(*gs, gr)


def _exchange_chips(ps, pr):
    n = len(ps)

    def body(*refs):
        ps_refs, pr_ref, ss_refs, sr_ref = refs[:n], refs[n], refs[n + 1:2 * n + 1], refs[2 * n + 1]
        send_sems, recv_sems = refs[2 * n + 2:]
        x, y, c = _coords()
        me = 2 * x + y
        chips = _other_chips(x, y)
        sends = []
        for j, (px, py) in enumerate(chips):
            to = (px, py, c)
            for t in range(n):
                sends.append(_remote(ps_refs[t].at[2 * px + py], ss_refs[t].at[me], send_sems, recv_sems, j * (n + 1) + t, to))
            sends.append(_remote(pr_ref, sr_ref.at[me], send_sems, recv_sems, j * (n + 1) + n, to))
        for cp in sends:
            cp.start()
        for j, (px, py) in enumerate(chips):
            k, to = 2 * px + py, (px, py, c)
            for t in range(n):
                _remote(ps_refs[t].at[me], ss_refs[t].at[k], send_sems, recv_sems, j * (n + 1) + t, to).wait_recv()
            _remote(pr_ref, sr_ref.at[k], send_sems, recv_sems, j * (n + 1) + n, to).wait_recv()
        for cp in sends:
            cp.wait_send()

    n_sem = 3 * (n + 1)
    return pl.pallas_call(
        body, name="exchange_chips",
        out_shape=tuple(jax.ShapeDtypeStruct(a.shape, a.dtype) for a in ps) + (jax.ShapeDtypeStruct((N_CHIPS,) + pr.shape, F32),),
        in_specs=[ANY] * (n + 1), out_specs=(ANY,) * (n + 1),
        scratch_shapes=[pltpu.SemaphoreType.DMA((n_sem,)), pltpu.SemaphoreType.DMA((n_sem,))],
        compiler_params=COMM_PARAMS,
    )(*ps, pr)


def _exchange_sibling_result(gh):
    n = len(gh)

    def body(*refs):
        gh_refs, out_refs, (send_sems, recv_sems) = refs[:n], refs[n:2 * n], refs[2 * n:]
        x, y, c = _coords()
        cps = [_remote(gh_refs[t], out_refs[t], send_sems, recv_sems, t, (x, y, 1 - c)) for t in range(n)]
        for cp in cps:
            cp.start()
        for cp in cps:
            cp.wait_recv()
        for cp in cps:
            cp.wait_send()

    return pl.pallas_call(
        body, name="exchange_sibling_result",
        out_shape=tuple(jax.ShapeDtypeStruct(a.shape, F32) for a in gh),
        in_specs=[ANY] * n, out_specs=(ANY,) * n,
        scratch_shapes=[pltpu.SemaphoreType.DMA((n,)), pltpu.SemaphoreType.DMA((n,))],
        compiler_params=COMM_PARAMS,
    )(*gh)


def _add_own_half(gs, recv, c_arr, *, name):
    _, rows, cols = recv.shape

    def body(c_ref, a_ref, b_ref, o_ref):
        o_ref[0] = (a_ref[0, 0] + b_ref[0]).astype(o_ref.dtype)

    return pl.pallas_call(
        body, name=name,
        out_shape=jax.ShapeDtypeStruct(recv.shape, GRAD_WIRE_DTYPE),
        grid_spec=pltpu.PrefetchScalarGridSpec(
            num_scalar_prefetch=1, grid=(N_CHIPS,),
            in_specs=[pl.BlockSpec((1, 1, rows, cols), lambda k, c_ref: (k, c_ref[0], 0, 0)),
                      pl.BlockSpec((1, rows, cols), lambda k, c_ref: (k, 0, 0))],
            out_specs=pl.BlockSpec((1, rows, cols), lambda k, c_ref: (k, 0, 0))),
        compiler_params=_params("parallel"),
    )(c_arr, gs, recv)


def _add2(a, b, *, name):
    def body(a_ref, b_ref, o_ref):
        o_ref[...] = a_ref[...] + b_ref[...]

    return pl.pallas_call(body, name=name, out_shape=jax.ShapeDtypeStruct(a.shape, F32))(a, b)


def _sum_slots(slots, *, tr, name):
    _, r, c = slots.shape

    def body(s_ref, o_ref):
        f = lambda k: s_ref[k].astype(F32)
        o_ref[...] = ((f(0) + f(1)) + f(2)) + f(3)

    return pl.pallas_call(
        body, name=name,
        out_shape=jax.ShapeDtypeStruct((r, c), F32),
        grid=(r // tr,),
        in_specs=[pl.BlockSpec((N_CHIPS, tr, c), lambda i: (0, i, 0))],
        out_specs=pl.BlockSpec((tr, c), lambda i: (i, 0)),
        compiler_params=_params("parallel"),
    )(slots)


def _adamw(w, g, m, v, *, tr, name):
    r, cols = w.shape

    def body(w_ref, g_ref, m_ref, v_ref, d_ref, nm_ref, nv_ref):
        g_ = g_ref[...]
        m_ = ADAM_B1 * m_ref[...] + (1.0 - ADAM_B1) * g_
        v_ = ADAM_B2 * v_ref[...] + (1.0 - ADAM_B2) * (g_ * g_)
        m_hat = m_ / (1.0 - ADAM_B1 ** ADAM_STEP)
        v_hat = v_ / (1.0 - ADAM_B2 ** ADAM_STEP)
        d_ref[...] = -ADAM_LR * (m_hat / (jnp.sqrt(v_hat) + ADAM_EPS) + ADAM_WD * w_ref[...])
        nm_ref[...] = m_
        nv_ref[...] = v_

    spec = pl.BlockSpec((tr, cols), lambda i: (i, 0))
    out = jax.ShapeDtypeStruct((r, cols), F32)
    return pl.pallas_call(
        body, name=name, out_shape=(out, out, out), grid=(r // tr,),
        in_specs=[spec] * 4, out_specs=(spec,) * 3,
        compiler_params=_params("parallel"),
    )(w, g, m, v)


WEIGHTS = ("w_in", "g_cq", "g_ckv", "w_uq", "w_uk", "w_uv", "w_o", "ln1_g", "ln1_b", "w_up", "conv_w", "conv_b",
           "w_down", "ln2_g", "ln2_b")


def kernel(x, w_in, g_cq, g_ckv, w_uq, w_uk, w_uv, w_o, ln1_g, ln1_b, w_up, conv_w, conv_b, w_down, ln2_g, ln2_b, loss_target, m_w_in, m_g_cq, m_g_ckv, m_w_uq, m_w_uk, m_w_uv, m_w_o, m_ln1_g, m_ln1_b, m_w_up, m_conv_w, m_conv_b, m_w_down, m_ln2_g, m_ln2_b, v_w_in, v_g_cq, v_g_ckv, v_w_uq, v_w_uk, v_w_uv, v_w_o, v_ln1_g, v_ln1_b, v_w_up, v_conv_w, v_conv_b, v_w_down, v_ln2_g, v_ln2_b):
    wts = dict(zip(WEIGHTS, (w_in, g_cq, g_ckv, w_uq, w_uk, w_uv, w_o, ln1_g, ln1_b, w_up, conv_w, conv_b, w_down, ln2_g, ln2_b)))
    mom = dict(zip(WEIGHTS, (m_w_in, m_g_cq, m_g_ckv, m_w_uq, m_w_uk, m_w_uv, m_w_o, m_ln1_g, m_ln1_b, m_w_up, m_conv_w, m_conv_b, m_w_down, m_ln2_g, m_ln2_b)))
    var = dict(zip(WEIGHTS, (v_w_in, v_g_cq, v_g_ckv, v_w_uq, v_w_uk, v_w_uv, v_w_o, v_ln1_g, v_ln1_b, v_w_up, v_conv_w, v_conv_b, v_w_down, v_ln2_g, v_ln2_b)))

    wp = jnp.concatenate([_rows(_mx(wts[n]), SHARD_ROWS[n]) for n in GATHERED], axis=0)
    cwp = _rows(conv_w, SHARD_ROWS["conv_w"])
    me = 2 * lax.axis_index("x") + lax.axis_index("y")
    my_c = lax.axis_index("c")
    own = lambda slots, mine: lax.dynamic_update_index_in_dim(slots, mine, me, 0)
    wp = wp.reshape(2, R_GATHER // 2, LANES)
    wfull, cwfull = _gather_weights(wp, cwp)
    wfull, cwfull = own(wfull, wp).reshape(N_CHIPS, R_GATHER, LANES), own(cwfull, cwp)
    full, r = {}, 0
    for n in GATHERED:
        full[n] = _from_chip_blocks(n, wfull[:, r:r + SHARD_ROWS[n]])
        r += SHARD_ROWS[n]
    conv_w_full = _from_chip_blocks("conv_w", cwfull)
    w = _prep_weights(full["w_in"], full["w_uq"], w_uk, w_uv, full["w_o"], full["w_up"], full["w_down"])

    loss, grad_x, g = _local_step(x[0], loss_target[0], w, g_cq, g_ckv, ln1_g, ln1_b, conv_w_full, conv_b, ln2_g, ln2_b)
    loss = lax.psum(loss, ("x", "y", "c"))

    gb = [_blocked(n, g[n]) for n in BIG]
    gr = _pack_flat(g, SMALL_G)
    c_arr = my_c.astype(jnp.int32).reshape(1)
    *recv, recv_r = _exchange_sibling_halves(gb, gr)
    ps = [_add_own_half(gb[i], recv[i], c_arr, name=f"add_half_{n}") for i, n in enumerate(BIG)]
    pr = _add2(gr, recv_r, name="add_small")
    *slots, slots_r = _exchange_chips(ps, pr)
    slots = [own(slots[i], lax.dynamic_index_in_dim(ps[i], me, 0, keepdims=False)) for i in range(len(BIG))]
    slots_r = own(slots_r, pr)
    g_half = [_sum_slots(slots[i], tr=slots[i].shape[1] // 2, name=f"sum_chips_{n}") for i, n in enumerate(BIG)]
    g_small = _unpack_flat(_sum_slots(slots_r, tr=R_SMALL, name="sum_chips_small"), SMALL_G, SMALL_G_SHAPE)
    g_other = _exchange_sibling_result(g_half)
    grads = {n: jnp.where(my_c == 0, jnp.concatenate([g_half[i], g_other[i]]), jnp.concatenate([g_other[i], g_half[i]]))
             for i, n in enumerate(BIG)}
    g_small["conv_w"] = lax.dynamic_slice_in_dim(g_small["conv_w"], me * SHARD_SHAPE["conv_w"][1], SHARD_SHAPE["conv_w"][1], 1)
    grads.update(g_small)

    res = {}
    for n in BIG:
        as2d = lambda a: a.reshape(BIG_2D[n])
        d, m, v = _adamw(as2d(wts[n]), grads[n], as2d(mom[n]), as2d(var[n]), tr=BIG_2D[n][0] // 4, name=f"adamw_{n}")
        res[n] = [a.reshape(SHARD_SHAPE[n]) for a in (grads[n], d, m, v)]
    flat = lambda t: _pack_flat(t, SMALL_G)
    dmv = _adamw(flat(wts), flat(g_small), flat(mom), flat(var), tr=R_SMALL, name="adamw_small")
    dmv = [_unpack_flat(a, SMALL_G, SMALL_U_SHAPE) for a in dmv]
    for n in SMALL_G:
        res[n] = [g_small[n]] + [t[n] for t in dmv]
    outs = [res[n][j] for j in range(4) for n in WEIGHTS]
    return (loss, grad_x[None], *outs)
```

```python
import functools
import math

import jax
import jax.numpy as jnp
from jax import lax
from jax.experimental import pallas as pl
from jax.experimental.pallas import tpu as pltpu

F32 = jnp.float32
MXU_DTYPE = jnp.bfloat16
GRAD_WIRE_DTYPE = jnp.bfloat16
NEG = -1e30

D_MODEL = 1024
HEADS = 8
HEAD_DIM = 64
Q_RANK = 256
KV_RANK = 128
NOPE = 64
ROPE = 32
QK_PAD = 128
IN_WIDTH = 1952
IN_EXT = 2048
D_FF = 2816
DIL_PAIRS = ((128, 1), (512, 4), (2048, 16))
DIL_BLOCK = 128
ROPE_THETA = 10000.0
DN_ALPHA = 2.0 ** 0.25
LN_EPS = 1e-5
RMS_EPS = 1e-6
MLA_SCALE = 1.0 / math.sqrt(NOPE + ROPE)
DIL_SCALE = 1.0 / math.sqrt(HEAD_DIM)

ADAM_LR = 0.001
ADAM_B1 = 0.9
ADAM_B2 = 0.999
ADAM_EPS = 1e-08
ADAM_WD = 0.01
ADAM_STEP = 10

LANES = 128
SUBLANES = 8
VMEM_LIMIT_BYTES = 56 * 1024 * 1024

MESH = pl.DeviceIdType.MESH


def _params(*sem):
    return pltpu.CompilerParams(dimension_semantics=sem, vmem_limit_bytes=VMEM_LIMIT_BYTES)


def _dot(a, b):
    return jnp.dot(a, b, preferred_element_type=F32)


def _dot_nt(a, b):
    return lax.dot_general(a, b, (((1,), (1,)), ((), ())), preferred_element_type=F32)


def _dot_tn(a, b):
    return lax.dot_general(a, b, (((0,), (0,)), ((), ())), preferred_element_type=F32)


def _mx(a):
    return a.astype(MXU_DTYPE)


def _mm_nn(a, b, *, name, tm, tn, tk, out_dtype=F32, add=None, add_scale=1.0):
    m, kdim = a.shape
    n = b.shape[1]
    nk = kdim // tk

    def body(*refs):
        if add is None:
            a_ref, b_ref, o_ref, acc = refs
        else:
            a_ref, b_ref, c_ref, o_ref, acc = refs
        k = pl.program_id(2)

        @pl.when(k == 0)
        def _():
            acc[...] = jnp.zeros_like(acc)

        acc[...] += _dot(_mx(a_ref[...]), _mx(b_ref[...]))

        @pl.when(k == nk - 1)
        def _():
            r = acc[...]
            if add is not None:
                r = r + add_scale * c_ref[...]
            o_ref[...] = r.astype(out_dtype)

    in_specs = [pl.BlockSpec((tm, tk), lambda i, j, k: (i, k)),
                pl.BlockSpec((tk, tn), lambda i, j, k: (k, j))]
    args = [a, b]
    if add is not None:
        in_specs.append(pl.BlockSpec((tm, tn), lambda i, j, k: (i, j)))
        args.append(add)
    return pl.pallas_call(
        body, name=name,
        out_shape=jax.ShapeDtypeStruct((m, n), out_dtype),
        grid=(m // tm, n // tn, nk),
        in_specs=in_specs,
        out_specs=pl.BlockSpec((tm, tn), lambda i, j, k: (i, j)),
        scratch_shapes=[pltpu.VMEM((tm, tn), F32)],
        compiler_params=_params("parallel", "parallel", "arbitrary"),
    )(*args)


def _mm_tn(a, b, *, name, tm, tn, ts, out_dtype=F32):
    s, m = a.shape
    n = b.shape[1]
    ns = s // ts

    def body(a_ref, b_ref, o_ref, acc):
        k = pl.program_id(2)

        @pl.when(k == 0)
        def _():
            acc[...] = jnp.zeros_like(acc)

        acc[...] += _dot_tn(_mx(a_ref[...]), _mx(b_ref[...]))

        @pl.when(k == ns - 1)
        def _():
            o_ref[...] = acc[...].astype(out_dtype)

    return pl.pallas_call(
        body, name=name,
        out_shape=jax.ShapeDtypeStruct((m, n), out_dtype),
        grid=(m // tm, n // tn, ns),
        in_specs=[pl.BlockSpec((ts, tm), lambda i, j, k: (k, i)),
                  pl.BlockSpec((ts, tn), lambda i, j, k: (k, j))],
        out_specs=pl.BlockSpec((tm, tn), lambda i, j, k: (i, j)),
        scratch_shapes=[pltpu.VMEM((tm, tn), F32)],
        compiler_params=_params("parallel", "parallel", "arbitrary"),
    )(a, b)


def _rope_tables(s):
    half = ROPE // 2
    freqs = ROPE_THETA ** (-jnp.arange(half, dtype=F32) / half)
    ang = jnp.arange(s).astype(F32)[:, None] * freqs[None, :]
    cos, sin = jnp.cos(ang), jnp.sin(ang)
    z = lambda w: jnp.zeros((s, w), F32)
    c = jnp.concatenate([jnp.ones((s, NOPE), F32), cos, cos, z(32)], axis=1)
    s1 = jnp.concatenate([z(NOPE + half), sin, z(32)], axis=1)
    s2 = jnp.concatenate([z(NOPE), -sin, z(half + 32)], axis=1)
    mask = jnp.concatenate([z(NOPE), jnp.ones((s, ROPE), F32), z(32)], axis=1)
    return c, s1, s2, mask


def _rope(x, c, s1, s2):
    return x * c + pltpu.roll(x, 16, 1) * s1 + pltpu.roll(x, LANES - 16, 1) * s2


def _unrope(dy, c, s1, s2):
    return dy * c + pltpu.roll(dy * s1, LANES - 16, 1) + pltpu.roll(dy * s2, 16, 1)


def _rms(x):
    r = lax.rsqrt(jnp.mean(x * x, axis=-1, keepdims=True) + RMS_EPS)
    return x * r, r


def _mla_prep_fwd(h, g_cq, g_ckv, wq, wk, wv, wv_t, tabs, *, tm):
    s = h.shape[0]
    c_t, s1_t, s2_t, _ = tabs

    def body(h_ref, gq_ref, gkv_ref, wq_ref, wk_ref, wv_ref, wvt_ref, c_ref, s1_ref, s2_ref,
             q_ref, k_ref, v_ref, vt_ref):
        cq = h_ref[:, 0:Q_RANK]
        ckv = h_ref[:, Q_RANK:Q_RANK + KV_RANK]
        kr = h_ref[:, Q_RANK + KV_RANK:Q_RANK + KV_RANK + QK_PAD]
        c, s1, s2 = c_ref[...], s1_ref[...], s2_ref[...]
        cqn = _mx(_rms(cq)[0] * gq_ref[...])
        ckvn = _mx(_rms(ckv)[0] * gkv_ref[...])
        kr_rot = _rope(kr, c, s1, s2)
        for hd in range(HEADS):
            q_ref[hd] = _rope(_dot(cqn, wq_ref[hd]), c, s1, s2).astype(q_ref.dtype)
            k_ref[hd] = (_dot(ckvn, wk_ref[hd]) + kr_rot).astype(k_ref.dtype)
            v_ref[hd] = _dot(ckvn, wv_ref[hd]).astype(v_ref.dtype)
            vt_ref[hd] = _dot_nt(wvt_ref[hd], ckvn).astype(vt_ref.dtype)

    full = lambda shp: pl.BlockSpec(shp, lambda i: (0,) * len(shp))
    row = lambda w: pl.BlockSpec((tm, w), lambda i: (i, 0))
    return pl.pallas_call(
        body, name="mla_prep_fwd",
        out_shape=(jax.ShapeDtypeStruct((HEADS, s, QK_PAD), MXU_DTYPE),
                   jax.ShapeDtypeStruct((HEADS, s, QK_PAD), MXU_DTYPE),
                   jax.ShapeDtypeStruct((HEADS, s, HEAD_DIM), MXU_DTYPE),
                   jax.ShapeDtypeStruct((HEADS, HEAD_DIM, s), MXU_DTYPE)),
        grid=(s // tm,),
        in_specs=[row(4 * LANES), full((1, Q_RANK)), full((1, KV_RANK)),
                  full((HEADS, Q_RANK, QK_PAD)), full((HEADS, KV_RANK, QK_PAD)), full((HEADS, KV_RANK, HEAD_DIM)),
                  full((HEADS, HEAD_DIM, KV_RANK)), row(LANES), row(LANES), row(LANES)],
        out_specs=(pl.BlockSpec((HEADS, tm, QK_PAD), lambda i: (0, i, 0)),
                   pl.BlockSpec((HEADS, tm, QK_PAD), lambda i: (0, i, 0)),
                   pl.BlockSpec((HEADS, tm, HEAD_DIM), lambda i: (0, i, 0)),
                   pl.BlockSpec((HEADS, HEAD_DIM, tm), lambda i: (0, 0, i))),
        compiler_params=_params("parallel"),
    )(h, g_cq, g_ckv, wq, wk, wv, wv_t, c_t, s1_t, s2_t)


def _mla_prep_bwd(h, dq, dk, dv, g_cq, g_ckv, wq_t, wk_t, wv_t, tabs, *, tm):
    s = h.shape[0]
    c_t, s1_t, s2_t, mask_t = tabs

    def body(h_ref, dq_ref, dk_ref, dv_ref, gq_ref, gkv_ref, wqt_ref, wkt_ref, wvt_ref,
             c_ref, s1_ref, s2_ref, mask_ref, dh_ref, dwq_ref, dwk_ref, dwv_ref, dgq_ref, dgkv_ref):
        i = pl.program_id(0)

        @pl.when(i == 0)
        def _():
            dwq_ref[...] = jnp.zeros_like(dwq_ref)
            dwk_ref[...] = jnp.zeros_like(dwk_ref)
            dwv_ref[...] = jnp.zeros_like(dwv_ref)
            dgq_ref[...] = jnp.zeros_like(dgq_ref)
            dgkv_ref[...] = jnp.zeros_like(dgkv_ref)

        cq = h_ref[:, 0:Q_RANK]
        ckv = h_ref[:, Q_RANK:Q_RANK + KV_RANK]
        c, s1, s2 = c_ref[...], s1_ref[...], s2_ref[...]
        cqh, rq = _rms(cq)
        ckvh, rkv = _rms(ckv)
        gq, gkv = gq_ref[...], gkv_ref[...]
        cqn = _mx(cqh * gq)
        ckvn = _mx(ckvh * gkv)
        dcqn = jnp.zeros((tm, Q_RANK), F32)
        dckvn = jnp.zeros((tm, KV_RANK), F32)
        dkr = jnp.zeros((tm, QK_PAD), F32)
        for hd in range(HEADS):
            dqh = _mx(_unrope(dq_ref[hd], c, s1, s2))
            dcqn = dcqn + _dot(dqh, wqt_ref[hd])
            dwq_ref[hd] += _dot_tn(cqn, dqh)
            dkh = dk_ref[hd]
            dkr = dkr + dkh
            dkh = _mx(dkh)
            dckvn = dckvn + _dot(dkh, wkt_ref[hd])
            dwk_ref[hd] += _dot_tn(ckvn, dkh)
            dvh = _mx(dv_ref[hd])
            dckvn = dckvn + _dot(dvh, wvt_ref[hd])
            dwv_ref[hd] += _dot_tn(ckvn, dvh)
        dgq_ref[...] += jnp.sum(dcqn * cqh, axis=0, keepdims=True)
        dgkv_ref[...] += jnp.sum(dckvn * ckvh, axis=0, keepdims=True)
        gd = dcqn * gq
        dh_ref[:, 0:Q_RANK] = rq * (gd - cqh * jnp.mean(gd * cqh, axis=-1, keepdims=True))
        gd = dckvn * gkv
        dh_ref[:, Q_RANK:Q_RANK + KV_RANK] = rkv * (gd - ckvh * jnp.mean(gd * ckvh, axis=-1, keepdims=True))
        dh_ref[:, Q_RANK + KV_RANK:Q_RANK + KV_RANK + QK_PAD] = _unrope(dkr, c, s1, s2) * mask_ref[...]

    full = lambda shp: pl.BlockSpec(shp, lambda i: (0,) * len(shp))
    row = lambda w: pl.BlockSpec((tm, w), lambda i: (i, 0))
    hrow = lambda w: pl.BlockSpec((HEADS, tm, w), lambda i: (0, i, 0))
    return pl.pallas_call(
        body, name="mla_prep_bwd",
        out_shape=(jax.ShapeDtypeStruct((s, 4 * LANES), F32),
                   jax.ShapeDtypeStruct((HEADS, Q_RANK, QK_PAD), F32),
                   jax.ShapeDtypeStruct((HEADS, KV_RANK, QK_PAD), F32),
                   jax.ShapeDtypeStruct((HEADS, KV_RANK, HEAD_DIM), F32),
                   jax.ShapeDtypeStruct((1, Q_RANK), F32),
                   jax.ShapeDtypeStruct((1, KV_RANK), F32)),
        grid=(s // tm,),
        in_specs=[row(4 * LANES), hrow(QK_PAD), hrow(QK_PAD), hrow(HEAD_DIM),
                  full((1, Q_RANK)), full((1, KV_RANK)),
                  full((HEADS, QK_PAD, Q_RANK)), full((HEADS, QK_PAD, KV_RANK)), full((HEADS, HEAD_DIM, KV_RANK)),
                  row(LANES), row(LANES), row(LANES), row(LANES)],
        out_specs=(row(4 * LANES), full((HEADS, Q_RANK, QK_PAD)), full((HEADS, KV_RANK, QK_PAD)),
                   full((HEADS, KV_RANK, HEAD_DIM)), full((1, Q_RANK)), full((1, KV_RANK))),
        compiler_params=_params("arbitrary"),
    )(h, dq, dk, dv, g_cq, g_ckv, wq_t, wk_t, wv_t, c_t, s1_t, s2_t, mask_t)


def _bdot(a, b, ca, cb):
    return lax.dot_general(a, b, (((ca,), (cb,)), ((0,), (0,))), preferred_element_type=F32)


def _causal_mask_t(t):
    kk = lax.broadcasted_iota(jnp.int32, (t, t), 0)
    qq = lax.broadcasted_iota(jnp.int32, (t, t), 1)
    return (qq >= kk)[None]


def _mla_attn_fwd(q, k, v_t, *, t, g, late=None):
    hds, s, _ = q.shape
    n = s // t
    n_groups = hds // g

    def body(*refs):
        if late is None:
            q_ref, k_ref, vt_ref, o_ref, lse_ref, m_sc, l_sc, acc_sc = refs
        else:
            q_ref, k_ref, vt_ref, wp_ref, o_ref, lse_ref, wout_ref, m_sc, l_sc, acc_sc, send_sems, recv_sems = refs
        hg, qi, ki = pl.program_id(0), pl.program_id(1), pl.program_id(2)
        if late is not None:
            tail = jnp.logical_and(hg == n_groups - 1, qi == n - 1)
            _gather_in_steps(wp_ref, wout_ref, send_sems, recv_sems,
                             first=jnp.logical_and(hg == 0, jnp.logical_and(qi == 0, ki == 0)),
                             mid=jnp.logical_and(tail, ki == 0), last=jnp.logical_and(tail, ki == n - 1))

        @pl.when(ki == 0)
        def _():
            m_sc[...] = jnp.full_like(m_sc, NEG)
            l_sc[...] = jnp.zeros_like(l_sc)
            acc_sc[...] = jnp.zeros_like(acc_sc)

        def step(masked):
            sc = _bdot(k_ref[...], q_ref[...], 2, 2) * MLA_SCALE
            if masked:
                sc = jnp.where(_causal_mask_t(t), sc, NEG)
            m_prev = m_sc[...]
            m_new = jnp.maximum(m_prev, jnp.max(sc, axis=1, keepdims=True))
            p = jnp.exp(sc - m_new)
            a = jnp.exp(m_prev - m_new)
            l_sc[...] = a * l_sc[...] + jnp.sum(p, axis=1, keepdims=True)
            acc_sc[...] = a * acc_sc[...] + _bdot(vt_ref[...], _mx(p), 2, 1)
            m_sc[...] = m_new

        @pl.when(ki < qi)
        def _():
            step(False)

        @pl.when(ki == qi)
        def _():
            step(True)
            o_ref[...] = acc_sc[...] / l_sc[...]
            lse_ref[...] = m_sc[...] + jnp.log(l_sc[...])

    qspec = pl.BlockSpec((g, t, QK_PAD), lambda h, i, j: (h, i, 0))
    kspec = pl.BlockSpec((g, t, QK_PAD), lambda h, i, j: (h, jnp.minimum(i, j), 0))
    vspec = pl.BlockSpec((g, HEAD_DIM, t), lambda h, i, j: (h, 0, jnp.minimum(i, j)))
    out_shape = [jax.ShapeDtypeStruct((hds, HEAD_DIM, s), F32), jax.ShapeDtypeStruct((hds, 1, s), F32)]
    in_specs = [qspec, kspec, vspec]
    out_specs = [pl.BlockSpec((g, HEAD_DIM, t), lambda h, i, j: (h, 0, i)), pl.BlockSpec((g, 1, t), lambda h, i, j: (h, 0, i))]
    scratch = [pltpu.VMEM((g, 1, t), F32), pltpu.VMEM((g, 1, t), F32), pltpu.VMEM((g, HEAD_DIM, t), F32)]
    args = [q, k, v_t]
    if late is not None:
        out_shape.append(jax.ShapeDtypeStruct((N_CHIPS,) + late.shape, late.dtype))
        in_specs.append(ANY)
        out_specs.append(ANY)
        scratch += [pltpu.SemaphoreType.DMA((6,)), pltpu.SemaphoreType.DMA((6,))]
        args.append(late)
    return pl.pallas_call(
        body, name="mla_attn_fwd",
        out_shape=tuple(out_shape), grid=(n_groups, n, n),
        in_specs=in_specs, out_specs=tuple(out_specs), scratch_shapes=scratch,
        compiler_params=pltpu.CompilerParams(dimension_semantics=("arbitrary",) * 3, vmem_limit_bytes=VMEM_LIMIT_BYTES,
                                             has_side_effects=late is not None),
    )(*args)


def _head_rowdot(a, b, *, tm):
    s, width = a.shape
    nh = width // HEAD_DIM

    def body(a_ref, b_ref, o_ref):
        prod = a_ref[...] * b_ref[...]
        for hd in range(nh):
            o_ref[:, hd:hd + 1] = jnp.sum(prod[:, hd * HEAD_DIM:(hd + 1) * HEAD_DIM], axis=-1, keepdims=True)

    return pl.pallas_call(
        body, name="head_rowdot",
        out_shape=jax.ShapeDtypeStruct((s, nh), F32),
        grid=(s // tm,),
        in_specs=[pl.BlockSpec((tm, width), lambda i: (i, 0))] * 2,
        out_specs=pl.BlockSpec((tm, nh), lambda i: (i, 0)),
        compiler_params=_params("parallel"),
    )(a, b)


def _mla_attn_bwd(q, k, v, do, lse, dd, *, t, g, early=()):
    hds, s, _ = q.shape
    n = s // t
    n_groups = hds // g
    ne = len(early)

    def body(*refs):
        q_ref, k_ref, v_ref, do_ref, lse_ref, dd_ref = refs[:6]
        ps_refs = refs[6:6 + ne]
        dq_ref, dk_ref, dv_ref = refs[6 + ne:9 + ne]
        ss_refs = refs[9 + ne:9 + 2 * ne]
        dq_sc, dk_sc, dv_sc = refs[9 + 2 * ne:12 + 2 * ne]
        hg, ki, qi = pl.program_id(0), pl.program_id(1), pl.program_id(2)
        if ne:
            send_sems, recv_sems = refs[12 + 2 * ne:]
            _exchange_in_steps(ps_refs, ss_refs, send_sems, recv_sems,
                               first=jnp.logical_and(hg == 0, jnp.logical_and(ki == 0, qi == 0)),
                               last=jnp.logical_and(hg == n_groups - 1, jnp.logical_and(ki == n - 1, qi == n - 1)))

        @pl.when(jnp.logical_and(ki == 0, qi == 0))
        def _():
            dq_sc[...] = jnp.zeros_like(dq_sc)

        @pl.when(qi == 0)
        def _():
            dk_sc[...] = jnp.zeros_like(dk_sc)
            dv_sc[...] = jnp.zeros_like(dv_sc)

        def step(masked):
            qb, kb, dob = q_ref[...], k_ref[...], do_ref[...]
            sc = _bdot(kb, qb, 2, 2) * MLA_SCALE
            if masked:
                sc = jnp.where(_causal_mask_t(t), sc, NEG)
            p = jnp.exp(sc - lse_ref[...])
            dv_sc[...] += _bdot(_mx(p), dob, 2, 1)
            dp = _bdot(v_ref[...], dob, 2, 2)
            ds = _mx(p * (dp - dd_ref[...]) * MLA_SCALE)
            dk_sc[...] += _bdot(ds, qb, 2, 1)
            dq_sc[qi] += _bdot(ds, kb, 1, 1)

        @pl.when(qi == ki)
        def _():
            step(True)

        @pl.when(qi > ki)
        def _():
            step(False)

        @pl.when(qi == n - 1)
        def _():
            dk_ref[...] = dk_sc[...]
            dv_ref[...] = dv_sc[...]

        @pl.when(jnp.logical_and(ki == n - 1, qi == n - 1))
        def _():
            for j in range(n):
                dq_ref[:, j * t:(j + 1) * t, :] = dq_sc[j]

    qs = lambda w: pl.BlockSpec((g, t, w), lambda h, j, i: (h, jnp.maximum(i, j), 0))
    ks = lambda w: pl.BlockSpec((g, t, w), lambda h, j, i: (h, j, 0))
    rowq = pl.BlockSpec((g, 1, t), lambda h, j, i: (h, 0, jnp.maximum(i, j)))
    scratch = [pltpu.VMEM((n, g, t, QK_PAD), F32), pltpu.VMEM((g, t, QK_PAD), F32), pltpu.VMEM((g, t, HEAD_DIM), F32)]
    if ne:
        scratch += [pltpu.SemaphoreType.DMA((3 * ne,)), pltpu.SemaphoreType.DMA((3 * ne,))]
    return pl.pallas_call(
        body, name="mla_attn_bwd",
        out_shape=(jax.ShapeDtypeStruct((hds, s, QK_PAD), F32), jax.ShapeDtypeStruct((hds, s, QK_PAD), F32),
                   jax.ShapeDtypeStruct((hds, s, HEAD_DIM), F32)) + tuple(jax.ShapeDtypeStruct(a.shape, a.dtype) for a in early),
        grid=(n_groups, n, n),
        in_specs=[qs(QK_PAD), ks(QK_PAD), ks(HEAD_DIM), qs(HEAD_DIM), rowq, rowq] + [ANY] * ne,
        out_specs=(pl.BlockSpec((g, s, QK_PAD), lambda h, j, i: (h, 0, 0)), ks(QK_PAD), ks(HEAD_DIM)) + (ANY,) * ne,
        scratch_shapes=scratch,
        compiler_params=pltpu.CompilerParams(dimension_semantics=("arbitrary",) * 3, vmem_limit_bytes=VMEM_LIMIT_BYTES,
                                             has_side_effects=ne > 0),
    )(q, k, v, do, lse, dd, *early)


def _perm(a, dil):
    if dil == 1:
        return a
    hds, s, e = a.shape
    return a.reshape(hds, s // dil, dil, e).transpose(0, 2, 1, 3).reshape(hds, s, e)


def _unperm(a, dil):
    if dil == 1:
        return a
    hds, s, e = a.shape
    return a.reshape(hds, dil, s // dil, e).transpose(0, 2, 1, 3).reshape(hds, s, e)


def _perm_row(a, dil):
    if dil == 1:
        return a
    hds, _, s = a.shape
    return a.reshape(hds, s // dil, dil).transpose(0, 2, 1).reshape(hds, 1, s)


def _unperm_row(a, dil):
    if dil == 1:
        return a
    hds, _, s = a.shape
    return a.reshape(hds, dil, s // dil).transpose(0, 2, 1).reshape(hds, 1, s)


def _dil_bias(dil):
    slopes = 2.0 ** (-8.0 * jnp.arange(1, HEADS + 1, dtype=F32) / HEADS)
    ik = jnp.arange(DIL_BLOCK)[:, None]
    iq = jnp.arange(DIL_BLOCK)[None, :]
    off_c = iq - ik
    off_p = iq - ik + DIL_BLOCK
    b_c = -slopes[:, None, None] * (off_c * dil).astype(F32)[None]
    b_p = -slopes[:, None, None] * (off_p * dil).astype(F32)[None]
    b_c = jnp.where((off_c >= 0)[None], b_c, NEG)
    b_p = jnp.where((off_p <= DIL_BLOCK)[None], b_p, NEG)
    return b_c, b_p


def _dil_fwd(q, k, v, dil, *, name):
    hds, s, e = q.shape
    blk = DIL_BLOCK
    nblk = s // blk
    nb = nblk // dil
    b_c, b_p = _dil_bias(dil)

    def body(q_ref, kc_ref, kp_ref, vc_ref, vp_ref, bc_ref, bp_ref, o_ref, lse_ref):
        b = pl.program_id(0)
        first = (b % nb) == 0
        qb = q_ref[...]
        s_c = _bdot(kc_ref[...], qb, 2, 2) * DIL_SCALE + bc_ref[...]
        s_p = jnp.where(first, NEG, _bdot(kp_ref[...], qb, 2, 2) * DIL_SCALE + bp_ref[...])
        m = jnp.maximum(jnp.max(s_c, axis=1, keepdims=True), jnp.max(s_p, axis=1, keepdims=True))
        p_c = jnp.exp(s_c - m)
        p_p = jnp.exp(s_p - m)
        l = jnp.sum(p_c, axis=1, keepdims=True) + jnp.sum(p_p, axis=1, keepdims=True)
        o = _bdot(_mx(p_c), vc_ref[...], 1, 1) + _bdot(_mx(p_p), vp_ref[...], 1, 1)
        o_ref[...] = o / jnp.swapaxes(l, 1, 2)
        lse_ref[...] = m + jnp.log(l)

    cur = lambda w: pl.BlockSpec((hds, blk, w), lambda b: (0, b, 0))
    prev = lambda w: pl.BlockSpec((hds, blk, w), lambda b: (0, jnp.maximum(b - 1, 0), 0))
    bias = pl.BlockSpec((hds, blk, blk), lambda b: (0, 0, 0))
    return pl.pallas_call(
        body, name=name,
        out_shape=(jax.ShapeDtypeStruct((hds, s, e), F32), jax.ShapeDtypeStruct((hds, 1, s), F32)),
        grid=(nblk,),
        in_specs=[cur(e), cur(e), prev(e), cur(e), prev(e), bias, bias],
        out_specs=(cur(e), pl.BlockSpec((hds, 1, blk), lambda b: (0, 0, b))),
        compiler_params=_params("parallel"),
    )(q, k, k, v, v, b_c, b_p)


def _dil_combine(os_, lses, *, ts):
    hds, s, e = os_[0].shape

    def body(o0, o1, o2, l0, l1, l2, o_ref, l_ref):
        a0, a1, a2 = l0[...], l1[...], l2[...]
        m = jnp.maximum(jnp.maximum(a0, a1), a2)
        e0, e1, e2 = jnp.exp(a0 - m), jnp.exp(a1 - m), jnp.exp(a2 - m)
        tot = e0 + e1 + e2
        col = lambda w: jnp.swapaxes(w, 1, 2)
        o_ref[...] = (col(e0 / tot) * o0[...] + col(e1 / tot) * o1[...]) + col(e2 / tot) * o2[...]
        l_ref[...] = m + jnp.log(tot)

    spec = pl.BlockSpec((hds, ts, e), lambda i: (0, i, 0))
    rspec = pl.BlockSpec((hds, 1, ts), lambda i: (0, 0, i))
    return pl.pallas_call(
        body, name="dil_combine",
        out_shape=(jax.ShapeDtypeStruct((hds, s, e), F32), jax.ShapeDtypeStruct((hds, 1, s), F32)),
        grid=(s // ts,),
        in_specs=[spec] * 3 + [rspec] * 3,
        out_specs=(spec, rspec),
        compiler_params=_params("parallel"),
    )(*os_, *lses)


def _dil_bwd(q, k, v, do, lj, dd, dil, *, name):
    hds, s, e = q.shape
    blk = DIL_BLOCK
    nblk = s // blk
    nb = nblk // dil
    b_c, b_p = _dil_bias(dil)

    def body(q_ref, qn_ref, kc_ref, kp_ref, vc_ref, vp_ref, do_ref, don_ref, l_ref, ln_ref, d_ref, dn_ref,
             bc_ref, bp_ref, dq_ref, dk_ref, dv_ref):
        b = pl.program_id(0)
        first = (b % nb) == 0
        nxt = jnp.logical_and(b + 1 < nblk, ((b + 1) % nb) != 0)
        qb, kc, kp, vc, vp = q_ref[...], kc_ref[...], kp_ref[...], vc_ref[...], vp_ref[...]
        dob = _mx(do_ref[...])
        bc, bp = bc_ref[...], bp_ref[...]
        p_c = jnp.exp(_bdot(kc, qb, 2, 2) * DIL_SCALE + bc - l_ref[...])
        p_p = jnp.where(first, 0.0, jnp.exp(_bdot(kp, qb, 2, 2) * DIL_SCALE + bp - l_ref[...]))
        ds_c = _mx(p_c * (_bdot(vc, dob, 2, 2) - d_ref[...]) * DIL_SCALE)
        ds_p = _mx(p_p * (_bdot(vp, dob, 2, 2) - d_ref[...]) * DIL_SCALE)
        dq_ref[...] = _bdot(ds_c, kc, 1, 1) + _bdot(ds_p, kp, 1, 1)
        qn = qn_ref[...]
        donb = _mx(don_ref[...])
        p_n = jnp.where(nxt, jnp.exp(_bdot(kc, qn, 2, 2) * DIL_SCALE + bp - ln_ref[...]), 0.0)
        ds_n = _mx(p_n * (_bdot(vc, donb, 2, 2) - dn_ref[...]) * DIL_SCALE)
        dk_ref[...] = _bdot(ds_c, qb, 2, 1) + _bdot(ds_n, qn, 2, 1)
        dv_ref[...] = _bdot(_mx(p_c), dob, 2, 1) + _bdot(_mx(p_n), donb, 2, 1)

    cur = lambda w: pl.BlockSpec((hds, blk, w), lambda b: (0, b, 0))
    prev = lambda w: pl.BlockSpec((hds, blk, w), lambda b: (0, jnp.maximum(b - 1, 0), 0))
    nxt_ = lambda w: pl.BlockSpec((hds, blk, w), lambda b: (0, jnp.minimum(b + 1, nblk - 1), 0))
    rcur = pl.BlockSpec((hds, 1, blk), lambda b: (0, 0, b))
    rnxt = pl.BlockSpec((hds, 1, blk), lambda b: (0, 0, jnp.minimum(b + 1, nblk - 1)))
    bias = pl.BlockSpec((hds, blk, blk), lambda b: (0, 0, 0))
    out = jax.ShapeDtypeStruct((hds, s, e), F32)
    return pl.pallas_call(
        body, name=name,
        out_shape=(out, out, out),
        grid=(nblk,),
        in_specs=[cur(e), nxt_(e), cur(e), prev(e), cur(e), prev(e), cur(e), nxt_(e),
                  rcur, rnxt, rcur, rnxt, bias, bias],
        out_specs=(cur(e), cur(e), cur(e)),
        compiler_params=_params("parallel"),
    )(q, q, k, k, v, v, do, do, lj, lj, dd, dd, b_c, b_p)


def _add3(a, b, c, *, ts, name):
    hds, s, e = a.shape

    def body(a_ref, b_ref, c_ref, o_ref):
        o_ref[...] = (a_ref[...] + b_ref[...]) + c_ref[...]

    spec = pl.BlockSpec((hds, ts, e), lambda i: (0, i, 0))
    return pl.pallas_call(
        body, name=name,
        out_shape=jax.ShapeDtypeStruct((hds, s, e), F32),
        grid=(s // ts,),
        in_specs=[spec] * 3, out_specs=spec,
        compiler_params=_params("parallel"),
    )(a, b, c)


def _ln_fwd(z, g, b):
    mu = jnp.mean(z, axis=-1, keepdims=True)
    zc = z - mu
    var = jnp.mean(zc * zc, axis=-1, keepdims=True)
    rstd = lax.rsqrt(var + LN_EPS)
    xhat = zc * rstd
    return xhat * g + b, xhat, rstd


def _ln_bwd(dy, xhat, rstd, g):
    dxh = dy * g
    return rstd * (dxh - jnp.mean(dxh, axis=-1, keepdims=True) - xhat * jnp.mean(dxh * xhat, axis=-1, keepdims=True))


def _out_ln1(attn, w_o, x, g, b, *, tm):
    s = x.shape[0]

    def body(a_ref, w_ref, x_ref, g_ref, b_ref, x1_ref, xh_ref, r_ref):
        z = DN_ALPHA * x_ref[...] + _dot(a_ref[...], w_ref[...])
        y, xhat, rstd = _ln_fwd(z, g_ref[...], b_ref[...])
        x1_ref[...] = y
        xh_ref[...] = xhat
        r_ref[...] = rstd

    row = lambda w: pl.BlockSpec((tm, w), lambda i: (i, 0))
    full = lambda shp: pl.BlockSpec(shp, lambda i: (0,) * len(shp))
    act = jax.ShapeDtypeStruct((s, D_MODEL), F32)
    return pl.pallas_call(
        body, name="out_ln1",
        out_shape=(act, act, jax.ShapeDtypeStruct((s, 1), F32)),
        grid=(s // tm,),
        in_specs=[row(D_MODEL), full((D_MODEL, D_MODEL)), row(D_MODEL), full((1, D_MODEL)), full((1, D_MODEL))],
        out_specs=(row(D_MODEL), row(D_MODEL), row(1)),
        compiler_params=_params("parallel"),
    )(attn, w_o, x, g, b)


def _down_ln2_loss(act, w_down, x1, g, b, target, *, tm):
    s = x1.shape[0]

    def body(a_ref, w_ref, x1_ref, g_ref, b_ref, t_ref, dz_ref, loss_ref, dg_ref, db_ref):
        i = pl.program_id(0)

        @pl.when(i == 0)
        def _():
            loss_ref[...] = jnp.zeros_like(loss_ref)
            dg_ref[...] = jnp.zeros_like(dg_ref)
            db_ref[...] = jnp.zeros_like(db_ref)

        gam = g_ref[...]
        z = DN_ALPHA * x1_ref[...] + _dot(a_ref[...], w_ref[...])
        y, xhat, rstd = _ln_fwd(z, gam, b_ref[...])
        err = y - t_ref[...]
        loss_ref[...] += 0.5 * jnp.sum(jnp.mean(err * err, axis=-1, keepdims=True))
        dy = err * (1.0 / D_MODEL)
        dg_ref[...] += jnp.sum(dy * xhat, axis=0, keepdims=True)
        db_ref[...] += jnp.sum(dy, axis=0, keepdims=True)
        dz_ref[...] = _ln_bwd(dy, xhat, rstd, gam)

    row = lambda w: pl.BlockSpec((tm, w), lambda i: (i, 0))
    full = lambda shp: pl.BlockSpec(shp, lambda i: (0,) * len(shp))
    vec = jax.ShapeDtypeStruct((1, D_MODEL), F32)
    return pl.pallas_call(
        body, name="down_ln2_loss",
        out_shape=(jax.ShapeDtypeStruct((s, D_MODEL), F32), jax.ShapeDtypeStruct((1, LANES), F32), vec, vec),
        grid=(s // tm,),
        in_specs=[row(D_FF), full((D_FF, D_MODEL)), row(D_MODEL), full((1, D_MODEL)), full((1, D_MODEL)), row(D_MODEL)],
        out_specs=(row(D_MODEL), full((1, LANES)), full((1, D_MODEL)), full((1, D_MODEL))),
        compiler_params=_params("arbitrary"),
    )(act, w_down, x1, g, b, target)


def _up_bwd_ln1(du_a, du_g, w_up_t, dz2, xhat1, rstd1, g, *, tm):
    s = dz2.shape[0]

    def body(dua_ref, dug_ref, wa_ref, wg_ref, dz2_ref, xh_ref, r_ref, g_ref, dz1_ref, dg_ref, db_ref):
        i = pl.program_id(0)

        @pl.when(i == 0)
        def _():
            dg_ref[...] = jnp.zeros_like(dg_ref)
            db_ref[...] = jnp.zeros_like(db_ref)

        dx1 = DN_ALPHA * dz2_ref[...] + (_dot(dua_ref[...], wa_ref[...]) + _dot(dug_ref[...], wg_ref[...]))
        xhat = xh_ref[...]
        dg_ref[...] += jnp.sum(dx1 * xhat, axis=0, keepdims=True)
        db_ref[...] += jnp.sum(dx1, axis=0, keepdims=True)
        dz1_ref[...] = _ln_bwd(dx1, xhat, r_ref[...], g_ref[...])

    row = lambda w: pl.BlockSpec((tm, w), lambda i: (i, 0))
    full = lambda shp: pl.BlockSpec(shp, lambda i: (0,) * len(shp))
    vec = jax.ShapeDtypeStruct((1, D_MODEL), F32)
    return pl.pallas_call(
        body, name="up_bwd_ln1",
        out_shape=(jax.ShapeDtypeStruct((s, D_MODEL), F32), vec, vec),
        grid=(s // tm,),
        in_specs=[row(D_FF), row(D_FF),
                  pl.BlockSpec((D_FF, D_MODEL), lambda i: (0, 0)), pl.BlockSpec((D_FF, D_MODEL), lambda i: (1, 0)),
                  row(D_MODEL), row(D_MODEL), row(1), full((1, D_MODEL))],
        out_specs=(row(D_MODEL), full((1, D_MODEL)), full((1, D_MODEL))),
        compiler_params=_params("arbitrary"),
    )(du_a, du_g, w_up_t, w_up_t, dz2, xhat1, rstd1, g)


GELU_C = math.sqrt(2.0 / math.pi)


def _gelu(x):
    cdf = 0.5 * (1.0 + jnp.tanh(GELU_C * (x + 0.044715 * (x * x * x))))
    return x * cdf


def _gelu_grad(x):
    t = jnp.tanh(GELU_C * (x + 0.044715 * (x * x * x)))
    return 0.5 * (1.0 + t) + 0.5 * x * (1.0 - t * t) * (GELU_C * (1.0 + 3.0 * 0.044715 * (x * x)))


def _shift_down(u, halo):
    t = u.shape[0]
    row = lax.broadcasted_iota(jnp.int32, u.shape, 0)
    h7, h6 = halo[7:8, :], halo[6:7, :]
    s1 = jnp.where(row == 0, h7, pltpu.roll(u, 1, 0))
    s2 = jnp.where(row == 0, h6, jnp.where(row == 1, h7, pltpu.roll(u, 2, 0)))
    return s1, s2


def _shift_up(d, nxt):
    t = d.shape[0]
    row = lax.broadcasted_iota(jnp.int32, d.shape, 0)
    n0, n1 = nxt[0:1, :], nxt[1:2, :]
    s1 = jnp.where(row == t - 1, n0, pltpu.roll(d, t - 1, 0))
    s2 = jnp.where(row == t - 1, n1, jnp.where(row == t - 2, n0, pltpu.roll(d, t - 2, 0)))
    return s1, s2


def _conv(u, s1, s2, w, b):
    return ((b + w[0:1, :] * s2) + w[1:2, :] * s1) + w[2:3, :] * u


def _gate_fwd(u, conv_w, conv_b, *, tm, tn):
    s = u.shape[0]
    nj = D_FF // tn
    hb = tm // SUBLANES

    def body(ua_ref, ug_ref, ha_ref, hg_ref, wa_ref, wg_ref, ba_ref, bg_ref, o_ref):
        keep = pl.program_id(0) > 0
        ua, ug = ua_ref[...], ug_ref[...]
        ha = jnp.where(keep, ha_ref[...], 0.0)
        hg = jnp.where(keep, hg_ref[...], 0.0)
        a = _conv(ua, *_shift_down(ua, ha), wa_ref[...], ba_ref[...])
        g = _conv(ug, *_shift_down(ug, hg), wg_ref[...], bg_ref[...])
        o_ref[...] = (_gelu(g) * a).astype(o_ref.dtype)

    main = lambda off: pl.BlockSpec((tm, tn), lambda i, j: (i, j + off))
    halo = lambda off: pl.BlockSpec((SUBLANES, tn), lambda i, j: (jnp.maximum(i * hb - 1, 0), j + off))
    wspec = lambda r, off: pl.BlockSpec((r, tn), lambda i, j: (0, j + off))
    return pl.pallas_call(
        body, name="gate_fwd",
        out_shape=jax.ShapeDtypeStruct((s, D_FF), MXU_DTYPE),
        grid=(s // tm, nj),
        in_specs=[main(0), main(nj), halo(0), halo(nj), wspec(3, 0), wspec(3, nj), wspec(1, 0), wspec(1, nj)],
        out_specs=pl.BlockSpec((tm, tn), lambda i, j: (i, j)),
        compiler_params=_params("parallel", "parallel"),
    )(u, u, u, u, conv_w, conv_w, conv_b, conv_b)


def _gate_bwd(u, dact, conv_w, conv_b, *, tm, tn):
    s = u.shape[0]
    nj = D_FF // tn
    ni = s // tm
    hb = tm // SUBLANES

    def body(ua_ref, ug_ref, ha_ref, hg_ref, na_ref, ng_ref, d_ref, dn_ref, wa_ref, wg_ref, ba_ref, bg_ref,
             dua_ref, dug_ref, dwa_ref, dwg_ref, dba_ref, dbg_ref):
        i = pl.program_id(1)

        @pl.when(i == 0)
        def _():
            for r in (dwa_ref, dwg_ref, dba_ref, dbg_ref):
                r[...] = jnp.zeros_like(r)

        wa, wg, ba, bg = wa_ref[...], wg_ref[...], ba_ref[...], bg_ref[...]
        ua, ug = ua_ref[...], ug_ref[...]
        ha = jnp.where(i > 0, ha_ref[...], 0.0)
        hg = jnp.where(i > 0, hg_ref[...], 0.0)
        sa1, sa2 = _shift_down(ua, ha)
        sg1, sg2 = _shift_down(ug, hg)
        a = _conv(ua, sa1, sa2, wa, ba)
        g = _conv(ug, sg1, sg2, wg, bg)
        d = d_ref[...]
        dya = d * _gelu(g)
        dyg = d * a * _gelu_grad(g)
        na, ng = na_ref[...], ng_ref[...]
        a_n = _conv(na, *_shift_down(na, ua[tm - SUBLANES:, :]), wa, ba)
        g_n = _conv(ng, *_shift_down(ng, ug[tm - SUBLANES:, :]), wg, bg)
        dn = jnp.where(i < ni - 1, dn_ref[...], 0.0)
        dya_n = dn * _gelu(g_n)
        dyg_n = dn * a_n * _gelu_grad(g_n)
        da1, da2 = _shift_up(dya, dya_n)
        dg1, dg2 = _shift_up(dyg, dyg_n)
        dua_ref[...] = (wa[2:3, :] * dya + wa[1:2, :] * da1 + wa[0:1, :] * da2).astype(dua_ref.dtype)
        dug_ref[...] = (wg[2:3, :] * dyg + wg[1:2, :] * dg1 + wg[0:1, :] * dg2).astype(dug_ref.dtype)
        ssum = lambda v: jnp.sum(v, axis=0, keepdims=True)
        dwa_ref[...] += jnp.concatenate([ssum(dya * sa2), ssum(dya * sa1), ssum(dya * ua)], axis=0)
        dwg_ref[...] += jnp.concatenate([ssum(dyg * sg2), ssum(dyg * sg1), ssum(dyg * ug)], axis=0)
        dba_ref[...] += ssum(dya)
        dbg_ref[...] += ssum(dyg)

    main = lambda off: pl.BlockSpec((tm, tn), lambda j, i: (i, j + off))
    halo = lambda off: pl.BlockSpec((SUBLANES, tn), lambda j, i: (jnp.maximum(i * hb - 1, 0), j + off))
    nxt = lambda off: pl.BlockSpec((SUBLANES, tn), lambda j, i: (jnp.minimum((i + 1) * hb, s // SUBLANES - 1), j + off))
    wspec = lambda r, off: pl.BlockSpec((r, tn), lambda j, i: (0, j + off))
    return pl.pallas_call(
        body, name="gate_bwd",
        out_shape=(jax.ShapeDtypeStruct((s, D_FF), MXU_DTYPE), jax.ShapeDtypeStruct((s, D_FF), MXU_DTYPE),
                   jax.ShapeDtypeStruct((3, D_FF), F32), jax.ShapeDtypeStruct((3, D_FF), F32),
                   jax.ShapeDtypeStruct((1, D_FF), F32), jax.ShapeDtypeStruct((1, D_FF), F32)),
        grid=(nj, ni),
        in_specs=[main(0), main(nj), halo(0), halo(nj), nxt(0), nxt(nj), main(0), nxt(0),
                  wspec(3, 0), wspec(3, nj), wspec(1, 0), wspec(1, nj)],
        out_specs=(main(0), main(0), wspec(3, 0), wspec(3, 0), wspec(1, 0), wspec(1, 0)),
        compiler_params=_params("parallel", "arbitrary"),
    )(u, u, u, u, u, u, dact, dact, conv_w, conv_w, conv_b, conv_b)


def _prep_weights(w_in, w_uq, w_uk, w_uv, w_o, w_up, w_down):
    return {**_prep_weights_first(w_in, w_uq, w_uk, w_uv), **_prep_weights_late(w_o, w_up, w_down)}


def _prep_weights_late(w_o, w_up, w_down):
    w_o, w_up, w_down = _mx(w_o), _mx(w_up), _mx(w_down)
    return dict(w_o=w_o, w_o_t=w_o.T, w_up=w_up, w_up_t=w_up.T, w_down=w_down, w_down_t=w_down.T)


def _prep_weights_first(w_in, w_uq, w_uk, w_uv):
    c = lambda a: a.astype(MXU_DTYPE)
    w_in = c(w_in)
    z = lambda w: jnp.zeros((D_MODEL, w), MXU_DTYPE)
    r0 = Q_RANK + KV_RANK
    w_in_ext = jnp.concatenate([w_in[:, :r0], z(NOPE), w_in[:, r0:r0 + ROPE], z(32), w_in[:, r0 + ROPE:]], axis=1)
    wq = jnp.pad(c(w_uq).transpose(1, 0, 2), ((0, 0), (0, 0), (0, QK_PAD - NOPE - ROPE)))
    wk = jnp.pad(c(w_uk).transpose(1, 0, 2), ((0, 0), (0, 0), (0, QK_PAD - NOPE)))
    wv = c(w_uv).transpose(1, 0, 2)
    t3 = lambda a: a.transpose(0, 2, 1)
    return dict(w_in=w_in_ext, w_in_t=w_in_ext.T, wq=wq, wq_t=t3(wq), wk=wk, wk_t=t3(wk), wv=wv, wv_t=t3(wv))


def _local_step(x, target, w, g_cq, g_ckv, ln1_g, ln1_b, conv_w, conv_b, ln2_g, ln2_b, comm=None):
    s = x.shape[0]
    tabs = _rope_tables(s)
    r2 = lambda a: a.reshape(1, -1)
    heads = lambda a: a.reshape(s, HEADS, HEAD_DIM).transpose(1, 0, 2)
    unheads = lambda a: a.transpose(1, 0, 2).reshape(s, HEADS * HEAD_DIM)
    cb = r2(conv_b)
    dils = [d for _, d in DIL_PAIRS]

    h = _mm_nn(x, w["w_in"], name="in_proj", tm=512, tn=1024, tk=D_MODEL)
    q, k, v, v_t = _mla_prep_fwd(h, r2(g_cq), r2(g_ckv), w["wq"], w["wk"], w["wv"], w["wv_t"], tabs, tm=256)
    if comm is None:
        o_mla_t, lse_mla = _mla_attn_fwd(q, k, v_t, t=512, g=HEADS)
    else:
        o_mla_t, lse_mla, gathered = _mla_attn_fwd(q, k, v_t, t=512, g=HEADS, late=comm["late"])
        w = {**w, **comm["finish"](gathered)}
    qd, kd, vd = (_mx(heads(h[:, 512 * (i + 1):512 * (i + 2)])) for i in range(3))
    qp = [_perm(qd, d) for d in dils]
    kp = [_perm(kd, d) for d in dils]
    vp = [_perm(vd, d) for d in dils]
    o_bs, lse_bs = [], []
    for i, d in enumerate(dils):
        o_b, l_b = _dil_fwd(qp[i], kp[i], vp[i], d, name=f"dil_fwd_{d}")
        o_bs.append(_unperm(o_b, d))
        lse_bs.append(_unperm_row(l_b, d))
    o_dil, lj = _dil_combine(o_bs, lse_bs, ts=512)
    attn_f = jnp.concatenate([o_mla_t.transpose(2, 0, 1).reshape(s, HEADS * HEAD_DIM), unheads(o_dil)], axis=1)
    attn = _mx(attn_f)
    x1, xhat1, rstd1 = _out_ln1(attn, w["w_o"], x, r2(ln1_g), r2(ln1_b), tm=256)
    u = _mm_nn(x1, w["w_up"], name="up_proj", tm=512, tn=1408, tk=D_MODEL)
    act = _gate_fwd(u, conv_w, cb, tm=256, tn=1408)
    dz2, loss, dg2, db2 = _down_ln2_loss(act, w["w_down"], x1, r2(ln2_g), r2(ln2_b), target, tm=256)

    dact = _mm_nn(dz2, w["w_down_t"], name="down_bwd", tm=512, tn=1408, tk=D_MODEL)
    dw_down = _mm_tn(act, dz2, name="dw_down", tm=1408, tn=D_MODEL, ts=512)
    du_a, du_g, dcw_a, dcw_g, dcb_a, dcb_g = _gate_bwd(u, dact, conv_w, cb, tm=256, tn=1408)
    dz1, dg1, db1 = _up_bwd_ln1(du_a, du_g, w["w_up_t"], dz2, xhat1, rstd1, r2(ln1_g), tm=256)
    dw_up = jnp.concatenate([_mm_tn(x1, du_a, name="dw_up_a", tm=D_MODEL, tn=1408, ts=512),
                             _mm_tn(x1, du_g, name="dw_up_g", tm=D_MODEL, tn=1408, ts=512)], axis=1)
    dattn = _mm_nn(dz1, w["w_o_t"], name="o_bwd", tm=512, tn=D_MODEL, tk=D_MODEL)
    dw_o = _mm_tn(attn, dz1, name="dw_o", tm=D_MODEL, tn=D_MODEL, ts=512)
    do_mla, do_dil = _mx(heads(dattn[:, :512])), heads(dattn[:, 512:])
    dd_all = _head_rowdot(dattn, attn_f, tm=256).T
    dd_mla, dd_dil = dd_all[:HEADS].reshape(HEADS, 1, s), dd_all[HEADS:].reshape(HEADS, 1, s)
    early = () if comm is None else tuple(comm["halve"]([("w_up", dw_up), ("w_down", dw_down)]))
    dq, dk, dv, *early_slots = _mla_attn_bwd(q, k, v, do_mla, lse_mla, dd_mla, t=512, g=4, early=early)
    parts = []
    for i, d in enumerate(dils):
        g3 = _dil_bwd(qp[i], kp[i], vp[i], _perm(do_dil, d), _perm_row(lj, d), _perm_row(dd_dil, d), d, name=f"dil_bwd_{d}")
        parts.append([_unperm(g, d) for g in g3])
    dqd, dkd, dvd = (_add3(parts[0][j], parts[1][j], parts[2][j], ts=512, name=f"dil_sum_{j}") for j in range(3))
    dh_mla, dwq, dwk, dwv, dgq, dgkv = _mla_prep_bwd(h, dq, dk, dv, r2(g_cq), r2(g_ckv),
                                                     w["wq_t"], w["wk_t"], w["wv_t"], tabs, tm=256)
    dh = _mx(jnp.concatenate([dh_mla, unheads(dqd), unheads(dkd), unheads(dvd)], axis=1))
    grad_x = _mm_nn(dh, w["w_in_t"], name="in_bwd", tm=512, tn=D_MODEL, tk=D_MODEL, add=dz1, add_scale=DN_ALPHA)
    dw_ext = _mm_tn(x, dh, name="dw_in", tm=D_MODEL, tn=1024, ts=512)
    r0 = Q_RANK + KV_RANK
    grads = dict(
        w_in=jnp.concatenate([dw_ext[:, :r0], dw_ext[:, r0 + NOPE:r0 + NOPE + ROPE], dw_ext[:, 512:]], axis=1),
        g_cq=dgq[0], g_ckv=dgkv[0],
        w_uq=dwq[:, :, :NOPE + ROPE].transpose(1, 0, 2),
        w_uk=dwk[:, :, :NOPE].transpose(1, 0, 2),
        w_uv=dwv.transpose(1, 0, 2),
        w_o=dw_o, ln1_g=dg1[0], ln1_b=db1[0], w_up=dw_up,
        conv_w=jnp.concatenate([dcw_a, dcw_g], axis=1), conv_b=jnp.concatenate([dcb_a, dcb_g], axis=1)[0],
        w_down=dw_down, ln2_g=dg2[0], ln2_b=db2[0])
    if comm is not None:
        grads["early"] = (early, tuple(early_slots))
    return loss[0, 0], grad_x, grads


N_CHIPS = 4
SHARDED = ("w_in", "w_uq", "w_o", "w_up", "conv_w", "w_down")
COL_SHARDED = ("w_in", "w_up", "conv_w")
SHARD_SHAPE = dict(w_in=(D_MODEL, IN_WIDTH // 4), w_uq=(Q_RANK // 4, HEADS, NOPE + ROPE), w_o=(D_MODEL // 4, D_MODEL),
                   w_up=(D_MODEL, 2 * D_FF // 4), conv_w=(3, 2 * D_FF // 4), w_down=(D_FF // 4, D_MODEL))
SMALL = ("g_cq", "g_ckv", "w_uk", "w_uv", "ln1_g", "ln1_b", "conv_b", "ln2_g", "ln2_b")
SMALL_SHAPE = dict(g_cq=(Q_RANK,), g_ckv=(KV_RANK,), w_uk=(KV_RANK, HEADS, NOPE), w_uv=(KV_RANK, HEADS, HEAD_DIM),
                   ln1_g=(D_MODEL,), ln1_b=(D_MODEL,), conv_b=(2 * D_FF,), ln2_g=(D_MODEL,), ln2_b=(D_MODEL,))
BIG = ("w_in", "w_uq", "w_o", "w_up", "w_down")
BIG_2D = dict(w_in=(D_MODEL, IN_WIDTH // 4), w_uq=(Q_RANK // 4, HEADS * (NOPE + ROPE)), w_o=(D_MODEL // 4, D_MODEL),
              w_up=(D_MODEL, 2 * D_FF // 4), w_down=(D_FF // 4, D_MODEL))
SMALL_G = SMALL + ("conv_w",)
SMALL_G_SHAPE = {**SMALL_SHAPE, "conv_w": (3, 2 * D_FF)}
SMALL_U_SHAPE = {**SMALL_SHAPE, "conv_w": (3, 2 * D_FF // 4)}


def _size(shape):
    return math.prod(shape)


def _padded_rows(n_elems, mult):
    return -(-n_elems // (LANES * mult)) * mult


SHARD_ROWS = {n: _padded_rows(_size(SHARD_SHAPE[n]), SUBLANES) for n in SHARDED}
R_SMALL = -(-sum(_size(SMALL_G_SHAPE[n]) for n in SMALL_G) // (LANES * LANES)) * LANES
GATHER_FIRST = ("w_in", "w_uq")
GATHER_LATE = ("w_o", "w_up", "w_down")
REDUCED_EARLY = ("w_up", "w_down")
REDUCED_LAST = ("w_in", "w_uq", "w_o")


def _rows(a, rows=None):
    flat = a.reshape(-1)
    rows = -(-flat.shape[0] // LANES) if rows is None else rows
    return jnp.pad(flat, (0, rows * LANES - flat.shape[0])).reshape(rows, LANES)


def _blocked(name, g):
    r, c = BIG_2D[name]
    a = g.reshape(r, N_CHIPS, c).transpose(1, 0, 2) if name in COL_SHARDED else g.reshape(N_CHIPS, r, c)
    return a.reshape(N_CHIPS, 2, r // 2, c)


def _pack_flat(t, names):
    return _rows(jnp.concatenate([t[n].astype(F32).reshape(-1) for n in names]), R_SMALL)


def _unpack_flat(buf, names, shapes):
    flat, out, r = buf.reshape(-1), {}, 0
    for n in names:
        out[n] = flat[r:r + _size(shapes[n])].reshape(shapes[n])
        r += _size(shapes[n])
    return out


def _from_chip_blocks(name, blocks):
    shp = SHARD_SHAPE[name]
    a = blocks.reshape(N_CHIPS, -1)[:, :_size(shp)].reshape((N_CHIPS,) + shp)
    if name in COL_SHARDED:
        return a.transpose(1, 0, 2).reshape(shp[0], N_CHIPS * shp[1])
    return a.reshape((N_CHIPS * shp[0],) + shp[1:])


ANY = pl.BlockSpec(memory_space=pl.ANY)
COMM_PARAMS = pltpu.CompilerParams(has_side_effects=True)


def _coords():
    return lax.axis_index("x"), lax.axis_index("y"), lax.axis_index("c")


def _other_chips(x, y):
    return [(1 - x, y), (x, 1 - y), (1 - x, 1 - y)]


def _remote(src, dst, send_sems, recv_sems, k, to):
    return pltpu.make_async_remote_copy(src_ref=src, dst_ref=dst, send_sem=send_sems.at[k], recv_sem=recv_sems.at[k],
                                        device_id=to, device_id_type=MESH)


def _gather_in_steps(wp_ref, wout_ref, send_sems, recv_sems, *, first, mid, last):
    x, y, c = _coords()
    me = 2 * x + y
    sib = (x, y, 1 - c)
    chips = _other_chips(x, y)
    ici = [_remote(wp_ref.at[c], wout_ref.at[me, c], send_sems, recv_sems, j, (px, py, c)) for j, (px, py) in enumerate(chips)]
    fwd = [_remote(wout_ref.at[2 * px + py, c], wout_ref.at[2 * px + py, c], send_sems, recv_sems, 3 + j, sib)
           for j, (px, py) in enumerate(chips)]

    @pl.when(first)
    def _():
        for cp in ici:
            cp.start()

    @pl.when(mid)
    def _():
        for j, (px, py) in enumerate(chips):
            _remote(wp_ref.at[c], wout_ref.at[2 * px + py, c], send_sems, recv_sems, j, (px, py, c)).wait_recv()
            fwd[j].start()

    @pl.when(last)
    def _():
        for j, (px, py) in enumerate(chips):
            k = 2 * px + py
            _remote(wout_ref.at[k, 1 - c], wout_ref.at[k, 1 - c], send_sems, recv_sems, 3 + j, sib).wait_recv()
        for cp in ici + fwd:
            cp.wait_send()


def _exchange_in_steps(ps_refs, ss_refs, send_sems, recv_sems, *, first, last):
    x, y, c = _coords()
    me = 2 * x + y
    chips = _other_chips(x, y)
    n = len(ps_refs)
    sends = [_remote(ps_refs[t].at[2 * px + py], ss_refs[t].at[me], send_sems, recv_sems, j * n + t, (px, py, c))
             for j, (px, py) in enumerate(chips) for t in range(n)]

    @pl.when(first)
    def _():
        for cp in sends:
            cp.start()

    @pl.when(last)
    def _():
        for j, (px, py) in enumerate(chips):
            for t in range(n):
                _remote(ps_refs[t].at[me], ss_refs[t].at[2 * px + py], send_sems, recv_sems, j * n + t, (px, py, c)).wait_recv()
        for cp in sends:
            cp.wait_send()


def _gather_weights(wp, cwp):
    def body(wp_ref, cw_ref, wout_ref, cwout_ref, send_sems, recv_sems):
        x, y, c = _coords()
        me = 2 * x + y
        sib = (x, y, 1 - c)
        chips = _other_chips(x, y)
        sends = [_remote(wp_ref.at[c], wout_ref.at[me, c], send_sems, recv_sems, j, (px, py, c))
                 for j, (px, py) in enumerate(chips)]
        sends += [_remote(cw_ref, cwout_ref.at[me], send_sems, recv_sems, 3 + j, (px, py, c))
                  for j, (px, py) in enumerate(chips)]
        for cp in sends:
            cp.start()
        for j, (px, py) in enumerate(chips):
            k = 2 * px + py
            _remote(wp_ref.at[c], wout_ref.at[k, c], send_sems, recv_sems, j, (px, py, c)).wait_recv()
            fwd = _remote(wout_ref.at[k, c], wout_ref.at[k, c], send_sems, recv_sems, 6 + j, sib)
            fwd.start()
            sends.append(fwd)
        for j, (px, py) in enumerate(chips):
            k = 2 * px + py
            _remote(cw_ref, cwout_ref.at[k], send_sems, recv_sems, 3 + j, (px, py, c)).wait_recv()
            _remote(wout_ref.at[k, 1 - c], wout_ref.at[k, 1 - c], send_sems, recv_sems, 6 + j, sib).wait_recv()
        for cp in sends:
            cp.wait_send()

    return pl.pallas_call(
        body, name="gather_weights",
        out_shape=(jax.ShapeDtypeStruct((N_CHIPS,) + wp.shape, wp.dtype), jax.ShapeDtypeStruct((N_CHIPS,) + cwp.shape, cwp.dtype)),
        in_specs=[ANY, ANY], out_specs=(ANY, ANY),
        scratch_shapes=[pltpu.SemaphoreType.DMA((9,)), pltpu.SemaphoreType.DMA((9,))],
        compiler_params=COMM_PARAMS,
    )(wp, cwp)


def _exchange_sibling_halves(gs, whole, *, name):
    n, nw = len(gs), len(whole)

    def body(*refs):
        gs_refs, wh_refs = refs[:n], refs[n:n + nw]
        os_refs, ow_refs = refs[n + nw:2 * n + nw], refs[2 * n + nw:2 * (n + nw)]
        send_sems, recv_sems = refs[2 * (n + nw):]
        x, y, c = _coords()
        sib = (x, y, 1 - c)
        cps = [_remote(gs_refs[t].at[k, 1 - c], os_refs[t].at[k], send_sems, recv_sems, t * N_CHIPS + k, sib)
               for t in range(n) for k in range(N_CHIPS)]
        cps += [_remote(wh_refs[t], ow_refs[t], send_sems, recv_sems, n * N_CHIPS + t, sib) for t in range(nw)]
        for cp in cps:
            cp.start()
        for cp in cps:
            cp.wait_recv()
        for cp in cps:
            cp.wait_send()

    n_sem = n * N_CHIPS + nw
    return pl.pallas_call(
        body, name=name,
        out_shape=tuple(jax.ShapeDtypeStruct((N_CHIPS,) + a.shape[2:], F32) for a in gs)
        + tuple(jax.ShapeDtypeStruct(a.shape, F32) for a in whole),
        in_specs=[ANY] * (n + nw), out_specs=(ANY,) * (n + nw),
        scratch_shapes=[pltpu.SemaphoreType.DMA((n_sem,)), pltpu.SemaphoreType.DMA((n_sem,))],
        compiler_params=COMM_PARAMS,
    )(*gs, *whole)


def _exchange_chips(ps, pr):
    n = len(ps)

    def body(*refs):
        ps_refs, pr_ref, ss_refs, sr_ref = refs[:n], refs[n], refs[n + 1:2 * n + 1], refs[2 * n + 1]
        send_sems, recv_sems = refs[2 * n + 2:]
        x, y, c = _coords()
        me = 2 * x + y
        chips = _other_chips(x, y)
        sends = []
        for j, (px, py) in enumerate(chips):
            to = (px, py, c)
            for t in range(n):
                sends.append(_remote(ps_refs[t].at[2 * px + py], ss_refs[t].at[me], send_sems, recv_sems, j * (n + 1) + t, to))
            sends.append(_remote(pr_ref, sr_ref.at[me], send_sems, recv_sems, j * (n + 1) + n, to))
        for cp in sends:
            cp.start()
        for j, (px, py) in enumerate(chips):
            k, to = 2 * px + py, (px, py, c)
            for t in range(n):
                _remote(ps_refs[t].at[me], ss_refs[t].at[k], send_sems, recv_sems, j * (n + 1) + t, to).wait_recv()
            _remote(pr_ref, sr_ref.at[k], send_sems, recv_sems, j * (n + 1) + n, to).wait_recv()
        for cp in sends:
            cp.wait_send()

    n_sem = 3 * (n + 1)
    return pl.pallas_call(
        body, name="exchange_chips",
        out_shape=tuple(jax.ShapeDtypeStruct(a.shape, a.dtype) for a in ps) + (jax.ShapeDtypeStruct((N_CHIPS,) + pr.shape, F32),),
        in_specs=[ANY] * (n + 1), out_specs=(ANY,) * (n + 1),
        scratch_shapes=[pltpu.SemaphoreType.DMA((n_sem,)), pltpu.SemaphoreType.DMA((n_sem,))],
        compiler_params=COMM_PARAMS,
    )(*ps, pr)


def _exchange_sibling_result(gh):
    n = len(gh)

    def body(*refs):
        gh_refs, out_refs, (send_sems, recv_sems) = refs[:n], refs[n:2 * n], refs[2 * n:]
        x, y, c = _coords()
        cps = [_remote(gh_refs[t], out_refs[t], send_sems, recv_sems, t, (x, y, 1 - c)) for t in range(n)]
        for cp in cps:
            cp.start()
        for cp in cps:
            cp.wait_recv()
        for cp in cps:
            cp.wait_send()

    return pl.pallas_call(
        body, name="exchange_sibling_result",
        out_shape=tuple(jax.ShapeDtypeStruct(a.shape, F32) for a in gh),
        in_specs=[ANY] * n, out_specs=(ANY,) * n,
        scratch_shapes=[pltpu.SemaphoreType.DMA((n,)), pltpu.SemaphoreType.DMA((n,))],
        compiler_params=COMM_PARAMS,
    )(*gh)


def _add_own_half(gs, recv, c_arr, *, name):
    _, rows, cols = recv.shape

    def body(c_ref, a_ref, b_ref, o_ref):
        o_ref[0] = (a_ref[0, 0] + b_ref[0]).astype(o_ref.dtype)

    return pl.pallas_call(
        body, name=name,
        out_shape=jax.ShapeDtypeStruct(recv.shape, GRAD_WIRE_DTYPE),
        grid_spec=pltpu.PrefetchScalarGridSpec(
            num_scalar_prefetch=1, grid=(N_CHIPS,),
            in_specs=[pl.BlockSpec((1, 1, rows, cols), lambda k, c_ref: (k, c_ref[0], 0, 0)),
                      pl.BlockSpec((1, rows, cols), lambda k, c_ref: (k, 0, 0))],
            out_specs=pl.BlockSpec((1, rows, cols), lambda k, c_ref: (k, 0, 0))),
        compiler_params=_params("parallel"),
    )(c_arr, gs, recv)


def _add2(a, b, *, name):
    def body(a_ref, b_ref, o_ref):
        o_ref[...] = a_ref[...] + b_ref[...]

    return pl.pallas_call(body, name=name, out_shape=jax.ShapeDtypeStruct(a.shape, F32))(a, b)


def _sum_slots(slots, *, tr, name):
    _, r, c = slots.shape

    def body(s_ref, o_ref):
        f = lambda k: s_ref[k].astype(F32)
        o_ref[...] = ((f(0) + f(1)) + f(2)) + f(3)

    return pl.pallas_call(
        body, name=name,
        out_shape=jax.ShapeDtypeStruct((r, c), F32),
        grid=(r // tr,),
        in_specs=[pl.BlockSpec((N_CHIPS, tr, c), lambda i: (0, i, 0))],
        out_specs=pl.BlockSpec((tr, c), lambda i: (i, 0)),
        compiler_params=_params("parallel"),
    )(slots)


def _adamw(w, g, m, v, *, tr, name):
    r, cols = w.shape

    def body(w_ref, g_ref, m_ref, v_ref, d_ref, nm_ref, nv_ref):
        g_ = g_ref[...]
        m_ = ADAM_B1 * m_ref[...] + (1.0 - ADAM_B1) * g_
        v_ = ADAM_B2 * v_ref[...] + (1.0 - ADAM_B2) * (g_ * g_)
        m_hat = m_ / (1.0 - ADAM_B1 ** ADAM_STEP)
        v_hat = v_ / (1.0 - ADAM_B2 ** ADAM_STEP)
        d_ref[...] = -ADAM_LR * (m_hat / (jnp.sqrt(v_hat) + ADAM_EPS) + ADAM_WD * w_ref[...])
        nm_ref[...] = m_
        nv_ref[...] = v_

    spec = pl.BlockSpec((tr, cols), lambda i: (i, 0))
    out = jax.ShapeDtypeStruct((r, cols), F32)
    return pl.pallas_call(
        body, name=name, out_shape=(out, out, out), grid=(r // tr,),
        in_specs=[spec] * 4, out_specs=(spec,) * 3,
        compiler_params=_params("parallel"),
    )(w, g, m, v)


WEIGHTS = ("w_in", "g_cq", "g_ckv", "w_uq", "w_uk", "w_uv", "w_o", "ln1_g", "ln1_b", "w_up", "conv_w", "conv_b",
           "w_down", "ln2_g", "ln2_b")


def kernel(x, w_in, g_cq, g_ckv, w_uq, w_uk, w_uv, w_o, ln1_g, ln1_b, w_up, conv_w, conv_b, w_down, ln2_g, ln2_b, loss_target, m_w_in, m_g_cq, m_g_ckv, m_w_uq, m_w_uk, m_w_uv, m_w_o, m_ln1_g, m_ln1_b, m_w_up, m_conv_w, m_conv_b, m_w_down, m_ln2_g, m_ln2_b, v_w_in, v_g_cq, v_g_ckv, v_w_uq, v_w_uk, v_w_uv, v_w_o, v_ln1_g, v_ln1_b, v_w_up, v_conv_w, v_conv_b, v_w_down, v_ln2_g, v_ln2_b):
    wts = dict(zip(WEIGHTS, (w_in, g_cq, g_ckv, w_uq, w_uk, w_uv, w_o, ln1_g, ln1_b, w_up, conv_w, conv_b, w_down, ln2_g, ln2_b)))
    mom = dict(zip(WEIGHTS, (m_w_in, m_g_cq, m_g_ckv, m_w_uq, m_w_uk, m_w_uv, m_w_o, m_ln1_g, m_ln1_b, m_w_up, m_conv_w, m_conv_b, m_w_down, m_ln2_g, m_ln2_b)))
    var = dict(zip(WEIGHTS, (v_w_in, v_g_cq, v_g_ckv, v_w_uq, v_w_uk, v_w_uv, v_w_o, v_ln1_g, v_ln1_b, v_w_up, v_conv_w, v_conv_b, v_w_down, v_ln2_g, v_ln2_b)))

    me = 2 * lax.axis_index("x") + lax.axis_index("y")
    my_c = lax.axis_index("c")
    c_arr = my_c.astype(jnp.int32).reshape(1)
    own = lambda slots, mine: lax.dynamic_update_index_in_dim(slots, mine, me, 0)

    def pack(names):
        return jnp.concatenate([_rows(_mx(wts[n]), SHARD_ROWS[n]) for n in names], axis=0).reshape(2, -1, LANES)

    def unpack(names, gathered, mine):
        buf, full, r = own(gathered, mine).reshape(N_CHIPS, -1, LANES), {}, 0
        for n in names:
            full[n] = _from_chip_blocks(n, buf[:, r:r + SHARD_ROWS[n]])
            r += SHARD_ROWS[n]
        return full

    wp_first, wp_late = pack(GATHER_FIRST), pack(GATHER_LATE)
    cwp = _rows(conv_w, SHARD_ROWS["conv_w"])
    gathered, cwfull = _gather_weights(wp_first, cwp)
    full = unpack(GATHER_FIRST, gathered, wp_first)
    conv_w_full = _from_chip_blocks("conv_w", own(cwfull, cwp))
    w = _prep_weights_first(full["w_in"], full["w_uq"], w_uk, w_uv)

    def finish(gathered_late):
        late = unpack(GATHER_LATE, gathered_late, wp_late)
        return _prep_weights_late(late["w_o"], late["w_up"], late["w_down"])

    def halve(named, whole=(), tag="early"):
        gb = [_blocked(n, a) for n, a in named]
        recv = _exchange_sibling_halves(gb, list(whole), name=f"exchange_sibling_halves_{tag}")
        ps = [_add_own_half(gb[i], recv[i], c_arr, name=f"add_half_{n}") for i, (n, _) in enumerate(named)]
        return ps + [_add2(a, recv[len(gb) + i], name=f"add_whole_{tag}_{i}") for i, a in enumerate(whole)]

    comm = dict(late=wp_late, finish=finish, halve=halve)
    loss, grad_x, g = _local_step(x[0], loss_target[0], w, g_cq, g_ckv, ln1_g, ln1_b, conv_w_full, conv_b, ln2_g, ln2_b, comm=comm)
    loss = lax.psum(loss, ("x", "y", "c"))

    ps_early, slots_early = g.pop("early")
    *ps_rest, pr = halve([(n, g[n]) for n in REDUCED_LAST], whole=[_pack_flat(g, SMALL_G)], tag="last")
    *slots_rest, slots_r = _exchange_chips(ps_rest, pr)
    ps = {**dict(zip(REDUCED_LAST, ps_rest)), **dict(zip(REDUCED_EARLY, ps_early))}
    slots = {**dict(zip(REDUCED_LAST, slots_rest)), **dict(zip(REDUCED_EARLY, slots_early))}
    slots = [own(slots[n], lax.dynamic_index_in_dim(ps[n], me, 0, keepdims=False)) for n in BIG]
    slots_r = own(slots_r, pr)
    g_half = [_sum_slots(slots[i], tr=slots[i].shape[1] // 2, name=f"sum_chips_{n}") for i, n in enumerate(BIG)]
    g_small = _unpack_flat(_sum_slots(slots_r, tr=R_SMALL, name="sum_chips_small"), SMALL_G, SMALL_G_SHAPE)
    g_other = _exchange_sibling_result(g_half)
    grads = {n: jnp.where(my_c == 0, jnp.concatenate([g_half[i], g_other[i]]), jnp.concatenate([g_other[i], g_half[i]]))
             for i, n in enumerate(BIG)}
    g_small["conv_w"] = lax.dynamic_slice_in_dim(g_small["conv_w"], me * SHARD_SHAPE["conv_w"][1], SHARD_SHAPE["conv_w"][1], 1)
    grads.update(g_small)

    res = {}
    for n in BIG:
        as2d = lambda a: a.reshape(BIG_2D[n])
        d, m, v = _adamw(as2d(wts[n]), grads[n], as2d(mom[n]), as2d(var[n]), tr=BIG_2D[n][0] // 4, name=f"adamw_{n}")
        res[n] = [a.reshape(SHARD_SHAPE[n]) for a in (grads[n], d, m, v)]
    flat = lambda t: _pack_flat(t, SMALL_G)
    dmv = _adamw(flat(wts), flat(g_small), flat(mom), flat(var), tr=R_SMALL, name="adamw_small")
    dmv = [_unpack_flat(a, SMALL_G, SMALL_U_SHAPE) for a in dmv]
    for n in SMALL_G:
        res[n] = [g_small[n]] + [t[n] for t in dmv]
    outs = [res[n][j] for j in range(4) for n in WEIGHTS]
    return (loss, grad_x[None], *outs)
```

```python
import functools
import math

import jax
import jax.numpy as jnp
from jax import lax
from jax.experimental import pallas as pl
from jax.experimental.pallas import tpu as pltpu

F32 = jnp.float32
MXU_DTYPE = jnp.bfloat16
GRAD_WIRE_DTYPE = jnp.bfloat16
NEG = -1e30

D_MODEL = 1024
HEADS = 8
HEAD_DIM = 64
Q_RANK = 256
KV_RANK = 128
NOPE = 64
ROPE = 32
QK_PAD = 128
IN_WIDTH = 1952
IN_EXT = 2048
D_FF = 2816
DIL_PAIRS = ((128, 1), (512, 4), (2048, 16))
DIL_BLOCK = 128
ROPE_THETA = 10000.0
DN_ALPHA = 2.0 ** 0.25
LN_EPS = 1e-5
RMS_EPS = 1e-6
MLA_SCALE = 1.0 / math.sqrt(NOPE + ROPE)
DIL_SCALE = 1.0 / math.sqrt(HEAD_DIM)

ADAM_LR = 0.001
ADAM_B1 = 0.9
ADAM_B2 = 0.999
ADAM_EPS = 1e-08
ADAM_WD = 0.01
ADAM_STEP = 10

LANES = 128
SUBLANES = 8
VMEM_LIMIT_BYTES = 56 * 1024 * 1024

MESH = pl.DeviceIdType.MESH


def _params(*sem):
    return pltpu.CompilerParams(dimension_semantics=sem, vmem_limit_bytes=VMEM_LIMIT_BYTES)


def _dot(a, b):
    return jnp.dot(a, b, preferred_element_type=F32)


def _dot_nt(a, b):
    return lax.dot_general(a, b, (((1,), (1,)), ((), ())), preferred_element_type=F32)


def _dot_tn(a, b):
    return lax.dot_general(a, b, (((0,), (0,)), ((), ())), preferred_element_type=F32)


def _mx(a):
    return a.astype(MXU_DTYPE)


def _mm_nn(a, b, *, name, tm, tn, tk, out_dtype=F32, add=None, add_scale=1.0):
    m, kdim = a.shape
    n = b.shape[1]
    nk = kdim // tk

    def body(*refs):
        if add is None:
            a_ref, b_ref, o_ref, acc = refs
        else:
            a_ref, b_ref, c_ref, o_ref, acc = refs
        k = pl.program_id(2)

        @pl.when(k == 0)
        def _():
            acc[...] = jnp.zeros_like(acc)

        acc[...] += _dot(_mx(a_ref[...]), _mx(b_ref[...]))

        @pl.when(k == nk - 1)
        def _():
            r = acc[...]
            if add is not None:
                r = r + add_scale * c_ref[...]
            o_ref[...] = r.astype(out_dtype)

    in_specs = [pl.BlockSpec((tm, tk), lambda i, j, k: (i, k)),
                pl.BlockSpec((tk, tn), lambda i, j, k: (k, j))]
    args = [a, b]
    if add is not None:
        in_specs.append(pl.BlockSpec((tm, tn), lambda i, j, k: (i, j)))
        args.append(add)
    return pl.pallas_call(
        body, name=name,
        out_shape=jax.ShapeDtypeStruct((m, n), out_dtype),
        grid=(m // tm, n // tn, nk),
        in_specs=in_specs,
        out_specs=pl.BlockSpec((tm, tn), lambda i, j, k: (i, j)),
        scratch_shapes=[pltpu.VMEM((tm, tn), F32)],
        compiler_params=_params("parallel", "parallel", "arbitrary"),
    )(*args)


def _mm_tn(a, b, *, name, tm, tn, ts, out_dtype=F32):
    s, m = a.shape
    n = b.shape[1]
    ns = s // ts

    def body(a_ref, b_ref, o_ref, acc):
        k = pl.program_id(2)

        @pl.when(k == 0)
        def _():
            acc[...] = jnp.zeros_like(acc)

        acc[...] += _dot_tn(_mx(a_ref[...]), _mx(b_ref[...]))

        @pl.when(k == ns - 1)
        def _():
            o_ref[...] = acc[...].astype(out_dtype)

    return pl.pallas_call(
        body, name=name,
        out_shape=jax.ShapeDtypeStruct((m, n), out_dtype),
        grid=(m // tm, n // tn, ns),
        in_specs=[pl.BlockSpec((ts, tm), lambda i, j, k: (k, i)),
                  pl.BlockSpec((ts, tn), lambda i, j, k: (k, j))],
        out_specs=pl.BlockSpec((tm, tn), lambda i, j, k: (i, j)),
        scratch_shapes=[pltpu.VMEM((tm, tn), F32)],
        compiler_params=_params("parallel", "parallel", "arbitrary"),
    )(a, b)


def _in_proj(x, w_in_ext, *, tm):
    s = x.shape[0]
    mla_w = 4 * LANES
    dil_w = HEADS * HEAD_DIM

    def body(x_ref, w_ref, h_ref, q_ref, k_ref, v_ref):
        xb = _mx(x_ref[...])
        h_ref[...] = _dot(xb, w_ref[:, 0:mla_w])
        for j, o_ref in enumerate((q_ref, k_ref, v_ref)):
            part = _dot(xb, w_ref[:, mla_w + j * dil_w:mla_w + (j + 1) * dil_w])
            for hd in range(HEADS):
                o_ref[hd] = part[:, hd * HEAD_DIM:(hd + 1) * HEAD_DIM].astype(o_ref.dtype)

    hm = jax.ShapeDtypeStruct((HEADS, s, HEAD_DIM), MXU_DTYPE)
    hspec = pl.BlockSpec((HEADS, tm, HEAD_DIM), lambda i: (0, i, 0))
    return pl.pallas_call(
        body, name="in_proj",
        out_shape=(jax.ShapeDtypeStruct((s, mla_w), F32), hm, hm, hm),
        grid=(s // tm,),
        in_specs=[pl.BlockSpec((tm, D_MODEL), lambda i: (i, 0)), pl.BlockSpec((D_MODEL, IN_EXT), lambda i: (0, 0))],
        out_specs=(pl.BlockSpec((tm, mla_w), lambda i: (i, 0)), hspec, hspec, hspec),
        compiler_params=_params("parallel"),
    )(x, w_in_ext)


def _attn_bwd_heads(dz1, w_o_t, a_mla, a_dil, *, tm):
    s = dz1.shape[0]
    half = HEADS * HEAD_DIM

    def body(dz_ref, w_ref, am_ref, ad_ref, dom_ref, dod_ref, dd_ref):
        dzb = _mx(dz_ref[...])
        for j, (a_ref, o_ref) in enumerate(((am_ref, dom_ref), (ad_ref, dod_ref))):
            da = _dot(dzb, w_ref[:, j * half:(j + 1) * half])
            prod = da * a_ref[...]
            for hd in range(HEADS):
                sl = slice(hd * HEAD_DIM, (hd + 1) * HEAD_DIM)
                o_ref[hd] = da[:, sl].astype(o_ref.dtype)
                dd_ref[:, j * HEADS + hd:j * HEADS + hd + 1] = jnp.sum(prod[:, sl], axis=-1, keepdims=True)

    hspec = pl.BlockSpec((HEADS, tm, HEAD_DIM), lambda i: (0, i, 0))
    row = lambda w: pl.BlockSpec((tm, w), lambda i: (i, 0))
    return pl.pallas_call(
        body, name="attn_bwd_heads",
        out_shape=(jax.ShapeDtypeStruct((HEADS, s, HEAD_DIM), MXU_DTYPE), jax.ShapeDtypeStruct((HEADS, s, HEAD_DIM), F32),
                   jax.ShapeDtypeStruct((s, 2 * HEADS), F32)),
        grid=(s // tm,),
        in_specs=[row(D_MODEL), pl.BlockSpec((D_MODEL, D_MODEL), lambda i: (0, 0)), row(half), row(half)],
        out_specs=(hspec, hspec, row(2 * HEADS)),
        compiler_params=_params("parallel"),
    )(dz1, w_o_t, a_mla, a_dil)


def _dil_merge(parts, *, ts):
    hds, s, e = parts[0][0].shape

    def body(*refs):
        o_ref = refs[9]
        for j in range(3):
            tot = (refs[j][...] + refs[3 + j][...]) + refs[6 + j][...]
            for hd in range(hds):
                col = j * hds * e + hd * e
                o_ref[:, col:col + e] = tot[hd].astype(o_ref.dtype)

    spec = pl.BlockSpec((hds, ts, e), lambda i: (0, i, 0))
    return pl.pallas_call(
        body, name="dil_merge",
        out_shape=jax.ShapeDtypeStruct((s, 3 * hds * e), MXU_DTYPE),
        grid=(s // ts,),
        in_specs=[spec] * 9,
        out_specs=pl.BlockSpec((ts, 3 * hds * e), lambda i: (i, 0)),
        compiler_params=_params("parallel"),
    )(*[parts[b][j] for b in range(3) for j in range(3)])


def _rope_tables(s):
    half = ROPE // 2
    freqs = ROPE_THETA ** (-jnp.arange(half, dtype=F32) / half)
    ang = jnp.arange(s).astype(F32)[:, None] * freqs[None, :]
    cos, sin = jnp.cos(ang), jnp.sin(ang)
    z = lambda w: jnp.zeros((s, w), F32)
    c = jnp.concatenate([jnp.ones((s, NOPE), F32), cos, cos, z(32)], axis=1)
    s1 = jnp.concatenate([z(NOPE + half), sin, z(32)], axis=1)
    s2 = jnp.concatenate([z(NOPE), -sin, z(half + 32)], axis=1)
    mask = jnp.concatenate([z(NOPE), jnp.ones((s, ROPE), F32), z(32)], axis=1)
    return c, s1, s2, mask


def _rope(x, c, s1, s2):
    return x * c + pltpu.roll(x, 16, 1) * s1 + pltpu.roll(x, LANES - 16, 1) * s2


def _unrope(dy, c, s1, s2):
    return dy * c + pltpu.roll(dy * s1, LANES - 16, 1) + pltpu.roll(dy * s2, 16, 1)


def _rms(x):
    r = lax.rsqrt(jnp.mean(x * x, axis=-1, keepdims=True) + RMS_EPS)
    return x * r, r


def _mla_prep_fwd(h, g_cq, g_ckv, wq, wk, wv, wv_t, tabs, *, tm):
    s = h.shape[0]
    c_t, s1_t, s2_t, _ = tabs

    def body(h_ref, gq_ref, gkv_ref, wq_ref, wk_ref, wv_ref, wvt_ref, c_ref, s1_ref, s2_ref,
             q_ref, k_ref, v_ref, vt_ref):
        cq = h_ref[:, 0:Q_RANK]
        ckv = h_ref[:, Q_RANK:Q_RANK + KV_RANK]
        kr = h_ref[:, Q_RANK + KV_RANK:Q_RANK + KV_RANK + QK_PAD]
        c, s1, s2 = c_ref[...], s1_ref[...], s2_ref[...]
        cqn = _mx(_rms(cq)[0] * gq_ref[...])
        ckvn = _mx(_rms(ckv)[0] * gkv_ref[...])
        kr_rot = _rope(kr, c, s1, s2)
        for hd in range(HEADS):
            q_ref[hd] = _rope(_dot(cqn, wq_ref[hd]), c, s1, s2).astype(q_ref.dtype)
            k_ref[hd] = (_dot(ckvn, wk_ref[hd]) + kr_rot).astype(k_ref.dtype)
            v_ref[hd] = _dot(ckvn, wv_ref[hd]).astype(v_ref.dtype)
            vt_ref[hd] = _dot_nt(wvt_ref[hd], ckvn).astype(vt_ref.dtype)

    full = lambda shp: pl.BlockSpec(shp, lambda i: (0,) * len(shp))
    row = lambda w: pl.BlockSpec((tm, w), lambda i: (i, 0))
    return pl.pallas_call(
        body, name="mla_prep_fwd",
        out_shape=(jax.ShapeDtypeStruct((HEADS, s, QK_PAD), MXU_DTYPE),
                   jax.ShapeDtypeStruct((HEADS, s, QK_PAD), MXU_DTYPE),
                   jax.ShapeDtypeStruct((HEADS, s, HEAD_DIM), MXU_DTYPE),
                   jax.ShapeDtypeStruct((HEADS, HEAD_DIM, s), MXU_DTYPE)),
        grid=(s // tm,),
        in_specs=[row(4 * LANES), full((1, Q_RANK)), full((1, KV_RANK)),
                  full((HEADS, Q_RANK, QK_PAD)), full((HEADS, KV_RANK, QK_PAD)), full((HEADS, KV_RANK, HEAD_DIM)),
                  full((HEADS, HEAD_DIM, KV_RANK)), row(LANES), row(LANES), row(LANES)],
        out_specs=(pl.BlockSpec((HEADS, tm, QK_PAD), lambda i: (0, i, 0)),
                   pl.BlockSpec((HEADS, tm, QK_PAD), lambda i: (0, i, 0)),
                   pl.BlockSpec((HEADS, tm, HEAD_DIM), lambda i: (0, i, 0)),
                   pl.BlockSpec((HEADS, HEAD_DIM, tm), lambda i: (0, 0, i))),
        compiler_params=_params("parallel"),
    )(h, g_cq, g_ckv, wq, wk, wv, wv_t, c_t, s1_t, s2_t)


def _mla_prep_bwd(h, dq, dk, dv, g_cq, g_ckv, wq_t, wk_t, wv_t, tabs, *, tm):
    s = h.shape[0]
    c_t, s1_t, s2_t, mask_t = tabs

    def body(h_ref, dq_ref, dk_ref, dv_ref, gq_ref, gkv_ref, wqt_ref, wkt_ref, wvt_ref,
             c_ref, s1_ref, s2_ref, mask_ref, dh_ref, dwq_ref, dwk_ref, dwv_ref, dgq_ref, dgkv_ref):
        i = pl.program_id(0)

        @pl.when(i == 0)
        def _():
            dwq_ref[...] = jnp.zeros_like(dwq_ref)
            dwk_ref[...] = jnp.zeros_like(dwk_ref)
            dwv_ref[...] = jnp.zeros_like(dwv_ref)
            dgq_ref[...] = jnp.zeros_like(dgq_ref)
            dgkv_ref[...] = jnp.zeros_like(dgkv_ref)

        cq = h_ref[:, 0:Q_RANK]
        ckv = h_ref[:, Q_RANK:Q_RANK + KV_RANK]
        c, s1, s2 = c_ref[...], s1_ref[...], s2_ref[...]
        cqh, rq = _rms(cq)
        ckvh, rkv = _rms(ckv)
        gq, gkv = gq_ref[...], gkv_ref[...]
        cqn = _mx(cqh * gq)
        ckvn = _mx(ckvh * gkv)
        dcqn = jnp.zeros((tm, Q_RANK), F32)
        dckvn = jnp.zeros((tm, KV_RANK), F32)
        dkr = jnp.zeros((tm, QK_PAD), F32)
        for hd in range(HEADS):
            dqh = _mx(_unrope(dq_ref[hd], c, s1, s2))
            dcqn = dcqn + _dot(dqh, wqt_ref[hd])
            dwq_ref[hd] += _dot_tn(cqn, dqh)
            dkh = dk_ref[hd]
            dkr = dkr + dkh
            dkh = _mx(dkh)
            dckvn = dckvn + _dot(dkh, wkt_ref[hd])
            dwk_ref[hd] += _dot_tn(ckvn, dkh)
            dvh = _mx(dv_ref[hd])
            dckvn = dckvn + _dot(dvh, wvt_ref[hd])
            dwv_ref[hd] += _dot_tn(ckvn, dvh)
        dgq_ref[...] += jnp.sum(dcqn * cqh, axis=0, keepdims=True)
        dgkv_ref[...] += jnp.sum(dckvn * ckvh, axis=0, keepdims=True)
        gd = dcqn * gq
        dh_ref[:, 0:Q_RANK] = rq * (gd - cqh * jnp.mean(gd * cqh, axis=-1, keepdims=True))
        gd = dckvn * gkv
        dh_ref[:, Q_RANK:Q_RANK + KV_RANK] = rkv * (gd - ckvh * jnp.mean(gd * ckvh, axis=-1, keepdims=True))
        dh_ref[:, Q_RANK + KV_RANK:Q_RANK + KV_RANK + QK_PAD] = _unrope(dkr, c, s1, s2) * mask_ref[...]

    full = lambda shp: pl.BlockSpec(shp, lambda i: (0,) * len(shp))
    row = lambda w: pl.BlockSpec((tm, w), lambda i: (i, 0))
    hrow = lambda w: pl.BlockSpec((HEADS, tm, w), lambda i: (0, i, 0))
    return pl.pallas_call(
        body, name="mla_prep_bwd",
        out_shape=(jax.ShapeDtypeStruct((s, 4 * LANES), F32),
                   jax.ShapeDtypeStruct((HEADS, Q_RANK, QK_PAD), F32),
                   jax.ShapeDtypeStruct((HEADS, KV_RANK, QK_PAD), F32),
                   jax.ShapeDtypeStruct((HEADS, KV_RANK, HEAD_DIM), F32),
                   jax.ShapeDtypeStruct((1, Q_RANK), F32),
                   jax.ShapeDtypeStruct((1, KV_RANK), F32)),
        grid=(s // tm,),
        in_specs=[row(4 * LANES), hrow(QK_PAD), hrow(QK_PAD), hrow(HEAD_DIM),
                  full((1, Q_RANK)), full((1, KV_RANK)),
                  full((HEADS, QK_PAD, Q_RANK)), full((HEADS, QK_PAD, KV_RANK)), full((HEADS, HEAD_DIM, KV_RANK)),
                  row(LANES), row(LANES), row(LANES), row(LANES)],
        out_specs=(row(4 * LANES), full((HEADS, Q_RANK, QK_PAD)), full((HEADS, KV_RANK, QK_PAD)),
                   full((HEADS, KV_RANK, HEAD_DIM)), full((1, Q_RANK)), full((1, KV_RANK))),
        compiler_params=_params("arbitrary"),
    )(h, dq, dk, dv, g_cq, g_ckv, wq_t, wk_t, wv_t, c_t, s1_t, s2_t, mask_t)


def _bdot(a, b, ca, cb):
    return lax.dot_general(a, b, (((ca,), (cb,)), ((0,), (0,))), preferred_element_type=F32)


def _causal_mask_t(t):
    kk = lax.broadcasted_iota(jnp.int32, (t, t), 0)
    qq = lax.broadcasted_iota(jnp.int32, (t, t), 1)
    return (qq >= kk)[None]


def _mla_attn_fwd(q, k, v_t, *, t, g, late=None):
    hds, s, _ = q.shape
    n = s // t
    n_groups = hds // g

    def body(*refs):
        if late is None:
            q_ref, k_ref, vt_ref, o_ref, lse_ref, m_sc, l_sc, acc_sc = refs
        else:
            q_ref, k_ref, vt_ref, wp_ref, o_ref, lse_ref, wout_ref, m_sc, l_sc, acc_sc, send_sems, recv_sems = refs
        hg, qi, ki = pl.program_id(0), pl.program_id(1), pl.program_id(2)
        if late is not None:
            tail = jnp.logical_and(hg == n_groups - 1, qi == n - 1)
            _gather_in_steps(wp_ref, wout_ref, send_sems, recv_sems,
                             first=jnp.logical_and(hg == 0, jnp.logical_and(qi == 0, ki == 0)),
                             mid=jnp.logical_and(tail, ki == 0), last=jnp.logical_and(tail, ki == n - 1))

        @pl.when(ki == 0)
        def _():
            m_sc[...] = jnp.full_like(m_sc, NEG)
            l_sc[...] = jnp.zeros_like(l_sc)
            acc_sc[...] = jnp.zeros_like(acc_sc)

        def step(masked):
            sc = _bdot(k_ref[...], q_ref[...], 2, 2) * MLA_SCALE
            if masked:
                sc = jnp.where(_causal_mask_t(t), sc, NEG)
            m_prev = m_sc[...]
            m_new = jnp.maximum(m_prev, jnp.max(sc, axis=1, keepdims=True))
            p = jnp.exp(sc - m_new)
            a = jnp.exp(m_prev - m_new)
            l_sc[...] = a * l_sc[...] + jnp.sum(p, axis=1, keepdims=True)
            acc_sc[...] = a * acc_sc[...] + _bdot(vt_ref[...], _mx(p), 2, 1)
            m_sc[...] = m_new

        @pl.when(ki < qi)
        def _():
            step(False)

        @pl.when(ki == qi)
        def _():
            step(True)
            o_ref[...] = acc_sc[...] / l_sc[...]
            lse_ref[...] = m_sc[...] + jnp.log(l_sc[...])

    qspec = pl.BlockSpec((g, t, QK_PAD), lambda h, i, j: (h, i, 0))
    kspec = pl.BlockSpec((g, t, QK_PAD), lambda h, i, j: (h, jnp.minimum(i, j), 0))
    vspec = pl.BlockSpec((g, HEAD_DIM, t), lambda h, i, j: (h, 0, jnp.minimum(i, j)))
    out_shape = [jax.ShapeDtypeStruct((hds, HEAD_DIM, s), F32), jax.ShapeDtypeStruct((hds, 1, s), F32)]
    in_specs = [qspec, kspec, vspec]
    out_specs = [pl.BlockSpec((g, HEAD_DIM, t), lambda h, i, j: (h, 0, i)), pl.BlockSpec((g, 1, t), lambda h, i, j: (h, 0, i))]
    scratch = [pltpu.VMEM((g, 1, t), F32), pltpu.VMEM((g, 1, t), F32), pltpu.VMEM((g, HEAD_DIM, t), F32)]
    args = [q, k, v_t]
    if late is not None:
        out_shape.append(jax.ShapeDtypeStruct((N_CHIPS,) + late.shape, late.dtype))
        in_specs.append(ANY)
        out_specs.append(ANY)
        scratch += [pltpu.SemaphoreType.DMA((6,)), pltpu.SemaphoreType.DMA((6,))]
        args.append(late)
    return pl.pallas_call(
        body, name="mla_attn_fwd",
        out_shape=tuple(out_shape), grid=(n_groups, n, n),
        in_specs=in_specs, out_specs=tuple(out_specs), scratch_shapes=scratch,
        compiler_params=pltpu.CompilerParams(dimension_semantics=("arbitrary",) * 3, vmem_limit_bytes=VMEM_LIMIT_BYTES,
                                             has_side_effects=late is not None),
    )(*args)


def _head_rowdot(a, b, *, tm):
    s, width = a.shape
    nh = width // HEAD_DIM

    def body(a_ref, b_ref, o_ref):
        prod = a_ref[...] * b_ref[...]
        for hd in range(nh):
            o_ref[:, hd:hd + 1] = jnp.sum(prod[:, hd * HEAD_DIM:(hd + 1) * HEAD_DIM], axis=-1, keepdims=True)

    return pl.pallas_call(
        body, name="head_rowdot",
        out_shape=jax.ShapeDtypeStruct((s, nh), F32),
        grid=(s // tm,),
        in_specs=[pl.BlockSpec((tm, width), lambda i: (i, 0))] * 2,
        out_specs=pl.BlockSpec((tm, nh), lambda i: (i, 0)),
        compiler_params=_params("parallel"),
    )(a, b)


def _mla_attn_bwd(q, k, v, do, lse, dd, *, t, g, early=()):
    hds, s, _ = q.shape
    n = s // t
    n_groups = hds // g
    ne = len(early)

    def body(*refs):
        q_ref, k_ref, v_ref, do_ref, lse_ref, dd_ref = refs[:6]
        ps_refs = refs[6:6 + ne]
        dq_ref, dk_ref, dv_ref = refs[6 + ne:9 + ne]
        ss_refs = refs[9 + ne:9 + 2 * ne]
        dq_sc, dk_sc, dv_sc = refs[9 + 2 * ne:12 + 2 * ne]
        hg, ki, qi = pl.program_id(0), pl.program_id(1), pl.program_id(2)
        if ne:
            send_sems, recv_sems = refs[12 + 2 * ne:]
            _exchange_in_steps(ps_refs, ss_refs, send_sems, recv_sems,
                               first=jnp.logical_and(hg == 0, jnp.logical_and(ki == 0, qi == 0)),
                               last=jnp.logical_and(hg == n_groups - 1, jnp.logical_and(ki == n - 1, qi == n - 1)))

        @pl.when(jnp.logical_and(ki == 0, qi == 0))
        def _():
            dq_sc[...] = jnp.zeros_like(dq_sc)

        @pl.when(qi == 0)
        def _():
            dk_sc[...] = jnp.zeros_like(dk_sc)
            dv_sc[...] = jnp.zeros_like(dv_sc)

        def step(masked):
            qb, kb, dob = q_ref[...], k_ref[...], do_ref[...]
            sc = _bdot(kb, qb, 2, 2) * MLA_SCALE
            if masked:
                sc = jnp.where(_causal_mask_t(t), sc, NEG)
            p = jnp.exp(sc - lse_ref[...])
            dv_sc[...] += _bdot(_mx(p), dob, 2, 1)
            dp = _bdot(v_ref[...], dob, 2, 2)
            ds = _mx(p * (dp - dd_ref[...]) * MLA_SCALE)
            dk_sc[...] += _bdot(ds, qb, 2, 1)
            dq_sc[qi] += _bdot(ds, kb, 1, 1)

        @pl.when(qi == ki)
        def _():
            step(True)

        @pl.when(qi > ki)
        def _():
            step(False)

        @pl.when(qi == n - 1)
        def _():
            dk_ref[...] = dk_sc[...]
            dv_ref[...] = dv_sc[...]

        @pl.when(jnp.logical_and(ki == n - 1, qi == n - 1))
        def _():
            for j in range(n):
                dq_ref[:, j * t:(j + 1) * t, :] = dq_sc[j]

    qs = lambda w: pl.BlockSpec((g, t, w), lambda h, j, i: (h, jnp.maximum(i, j), 0))
    ks = lambda w: pl.BlockSpec((g, t, w), lambda h, j, i: (h, j, 0))
    rowq = pl.BlockSpec((g, 1, t), lambda h, j, i: (h, 0, jnp.maximum(i, j)))
    scratch = [pltpu.VMEM((n, g, t, QK_PAD), F32), pltpu.VMEM((g, t, QK_PAD), F32), pltpu.VMEM((g, t, HEAD_DIM), F32)]
    if ne:
        scratch += [pltpu.SemaphoreType.DMA((3 * ne,)), pltpu.SemaphoreType.DMA((3 * ne,))]
    return pl.pallas_call(
        body, name="mla_attn_bwd",
        out_shape=(jax.ShapeDtypeStruct((hds, s, QK_PAD), F32), jax.ShapeDtypeStruct((hds, s, QK_PAD), F32),
                   jax.ShapeDtypeStruct((hds, s, HEAD_DIM), F32)) + tuple(jax.ShapeDtypeStruct(a.shape, a.dtype) for a in early),
        grid=(n_groups, n, n),
        in_specs=[qs(QK_PAD), ks(QK_PAD), ks(HEAD_DIM), qs(HEAD_DIM), rowq, rowq] + [ANY] * ne,
        out_specs=(pl.BlockSpec((g, s, QK_PAD), lambda h, j, i: (h, 0, 0)), ks(QK_PAD), ks(HEAD_DIM)) + (ANY,) * ne,
        scratch_shapes=scratch,
        compiler_params=pltpu.CompilerParams(dimension_semantics=("arbitrary",) * 3, vmem_limit_bytes=VMEM_LIMIT_BYTES,
                                             has_side_effects=ne > 0),
    )(q, k, v, do, lse, dd, *early)


def _perm(a, dil):
    if dil == 1:
        return a
    hds, s, e = a.shape
    return a.reshape(hds, s // dil, dil, e).transpose(0, 2, 1, 3).reshape(hds, s, e)


def _unperm(a, dil):
    if dil == 1:
        return a
    hds, s, e = a.shape
    return a.reshape(hds, dil, s // dil, e).transpose(0, 2, 1, 3).reshape(hds, s, e)


def _perm_row(a, dil):
    if dil == 1:
        return a
    hds, _, s = a.shape
    return a.reshape(hds, s // dil, dil).transpose(0, 2, 1).reshape(hds, 1, s)


def _unperm_row(a, dil):
    if dil == 1:
        return a
    hds, _, s = a.shape
    return a.reshape(hds, dil, s // dil).transpose(0, 2, 1).reshape(hds, 1, s)


def _dil_bias(dil):
    slopes = 2.0 ** (-8.0 * jnp.arange(1, HEADS + 1, dtype=F32) / HEADS)
    ik = jnp.arange(DIL_BLOCK)[:, None]
    iq = jnp.arange(DIL_BLOCK)[None, :]
    off_c = iq - ik
    off_p = iq - ik + DIL_BLOCK
    b_c = -slopes[:, None, None] * (off_c * dil).astype(F32)[None]
    b_p = -slopes[:, None, None] * (off_p * dil).astype(F32)[None]
    b_c = jnp.where((off_c >= 0)[None], b_c, NEG)
    b_p = jnp.where((off_p <= DIL_BLOCK)[None], b_p, NEG)
    return b_c, b_p


def _dil_fwd(q, k, v, dil, *, name):
    hds, s, e = q.shape
    blk = DIL_BLOCK
    nblk = s // blk
    nb = nblk // dil
    b_c, b_p = _dil_bias(dil)

    def body(q_ref, kc_ref, kp_ref, vc_ref, vp_ref, bc_ref, bp_ref, o_ref, lse_ref):
        b = pl.program_id(0)
        first = (b % nb) == 0
        qb = q_ref[...]
        s_c = _bdot(kc_ref[...], qb, 2, 2) * DIL_SCALE + bc_ref[...]
        s_p = jnp.where(first, NEG, _bdot(kp_ref[...], qb, 2, 2) * DIL_SCALE + bp_ref[...])
        m = jnp.maximum(jnp.max(s_c, axis=1, keepdims=True), jnp.max(s_p, axis=1, keepdims=True))
        p_c = jnp.exp(s_c - m)
        p_p = jnp.exp(s_p - m)
        l = jnp.sum(p_c, axis=1, keepdims=True) + jnp.sum(p_p, axis=1, keepdims=True)
        o = _bdot(_mx(p_c), vc_ref[...], 1, 1) + _bdot(_mx(p_p), vp_ref[...], 1, 1)
        o_ref[...] = o / jnp.swapaxes(l, 1, 2)
        lse_ref[...] = m + jnp.log(l)

    cur = lambda w: pl.BlockSpec((hds, blk, w), lambda b: (0, b, 0))
    prev = lambda w: pl.BlockSpec((hds, blk, w), lambda b: (0, jnp.maximum(b - 1, 0), 0))
    bias = pl.BlockSpec((hds, blk, blk), lambda b: (0, 0, 0))
    return pl.pallas_call(
        body, name=name,
        out_shape=(jax.ShapeDtypeStruct((hds, s, e), F32), jax.ShapeDtypeStruct((hds, 1, s), F32)),
        grid=(nblk,),
        in_specs=[cur(e), cur(e), prev(e), cur(e), prev(e), bias, bias],
        out_specs=(cur(e), pl.BlockSpec((hds, 1, blk), lambda b: (0, 0, b))),
        compiler_params=_params("parallel"),
    )(q, k, k, v, v, b_c, b_p)


def _dil_combine(os_, lses, *, ts):
    hds, s, e = os_[0].shape

    def body(o0, o1, o2, l0, l1, l2, o_ref, l_ref):
        a0, a1, a2 = l0[...], l1[...], l2[...]
        m = jnp.maximum(jnp.maximum(a0, a1), a2)
        e0, e1, e2 = jnp.exp(a0 - m), jnp.exp(a1 - m), jnp.exp(a2 - m)
        tot = e0 + e1 + e2
        col = lambda w: jnp.swapaxes(w, 1, 2)
        res = (col(e0 / tot) * o0[...] + col(e1 / tot) * o1[...]) + col(e2 / tot) * o2[...]
        for hd in range(hds):
            o_ref[:, hd * e:(hd + 1) * e] = res[hd]
        l_ref[...] = m + jnp.log(tot)

    spec = pl.BlockSpec((hds, ts, e), lambda i: (0, i, 0))
    rspec = pl.BlockSpec((hds, 1, ts), lambda i: (0, 0, i))
    return pl.pallas_call(
        body, name="dil_combine",
        out_shape=(jax.ShapeDtypeStruct((s, hds * e), F32), jax.ShapeDtypeStruct((hds, 1, s), F32)),
        grid=(s // ts,),
        in_specs=[spec] * 3 + [rspec] * 3,
        out_specs=(pl.BlockSpec((ts, hds * e), lambda i: (i, 0)), rspec),
        compiler_params=_params("parallel"),
    )(*os_, *lses)


def _dil_bwd(q, k, v, do, lj, dd, dil, *, name):
    hds, s, e = q.shape
    blk = DIL_BLOCK
    nblk = s // blk
    nb = nblk // dil
    b_c, b_p = _dil_bias(dil)

    def body(q_ref, qn_ref, kc_ref, kp_ref, vc_ref, vp_ref, do_ref, don_ref, l_ref, ln_ref, d_ref, dn_ref,
             bc_ref, bp_ref, dq_ref, dk_ref, dv_ref):
        b = pl.program_id(0)
        first = (b % nb) == 0
        nxt = jnp.logical_and(b + 1 < nblk, ((b + 1) % nb) != 0)
        qb, kc, kp, vc, vp = q_ref[...], kc_ref[...], kp_ref[...], vc_ref[...], vp_ref[...]
        dob = _mx(do_ref[...])
        bc, bp = bc_ref[...], bp_ref[...]
        p_c = jnp.exp(_bdot(kc, qb, 2, 2) * DIL_SCALE + bc - l_ref[...])
        p_p = jnp.where(first, 0.0, jnp.exp(_bdot(kp, qb, 2, 2) * DIL_SCALE + bp - l_ref[...]))
        ds_c = _mx(p_c * (_bdot(vc, dob, 2, 2) - d_ref[...]) * DIL_SCALE)
        ds_p = _mx(p_p * (_bdot(vp, dob, 2, 2) - d_ref[...]) * DIL_SCALE)
        dq_ref[...] = _bdot(ds_c, kc, 1, 1) + _bdot(ds_p, kp, 1, 1)
        qn = qn_ref[...]
        donb = _mx(don_ref[...])
        p_n = jnp.where(nxt, jnp.exp(_bdot(kc, qn, 2, 2) * DIL_SCALE + bp - ln_ref[...]), 0.0)
        ds_n = _mx(p_n * (_bdot(vc, donb, 2, 2) - dn_ref[...]) * DIL_SCALE)
        dk_ref[...] = _bdot(ds_c, qb, 2, 1) + _bdot(ds_n, qn, 2, 1)
        dv_ref[...] = _bdot(_mx(p_c), dob, 2, 1) + _bdot(_mx(p_n), donb, 2, 1)

    cur = lambda w: pl.BlockSpec((hds, blk, w), lambda b: (0, b, 0))
    prev = lambda w: pl.BlockSpec((hds, blk, w), lambda b: (0, jnp.maximum(b - 1, 0), 0))
    nxt_ = lambda w: pl.BlockSpec((hds, blk, w), lambda b: (0, jnp.minimum(b + 1, nblk - 1), 0))
    rcur = pl.BlockSpec((hds, 1, blk), lambda b: (0, 0, b))
    rnxt = pl.BlockSpec((hds, 1, blk), lambda b: (0, 0, jnp.minimum(b + 1, nblk - 1)))
    bias = pl.BlockSpec((hds, blk, blk), lambda b: (0, 0, 0))
    out = jax.ShapeDtypeStruct((hds, s, e), F32)
    return pl.pallas_call(
        body, name=name,
        out_shape=(out, out, out),
        grid=(nblk,),
        in_specs=[cur(e), nxt_(e), cur(e), prev(e), cur(e), prev(e), cur(e), nxt_(e),
                  rcur, rnxt, rcur, rnxt, bias, bias],
        out_specs=(cur(e), cur(e), cur(e)),
        compiler_params=_params("parallel"),
    )(q, q, k, k, v, v, do, do, lj, lj, dd, dd, b_c, b_p)


def _add3(a, b, c, *, ts, name):
    hds, s, e = a.shape

    def body(a_ref, b_ref, c_ref, o_ref):
        o_ref[...] = (a_ref[...] + b_ref[...]) + c_ref[...]

    spec = pl.BlockSpec((hds, ts, e), lambda i: (0, i, 0))
    return pl.pallas_call(
        body, name=name,
        out_shape=jax.ShapeDtypeStruct((hds, s, e), F32),
        grid=(s // ts,),
        in_specs=[spec] * 3, out_specs=spec,
        compiler_params=_params("parallel"),
    )(a, b, c)


def _ln_fwd(z, g, b):
    mu = jnp.mean(z, axis=-1, keepdims=True)
    zc = z - mu
    var = jnp.mean(zc * zc, axis=-1, keepdims=True)
    rstd = lax.rsqrt(var + LN_EPS)
    xhat = zc * rstd
    return xhat * g + b, xhat, rstd


def _ln_bwd(dy, xhat, rstd, g):
    dxh = dy * g
    return rstd * (dxh - jnp.mean(dxh, axis=-1, keepdims=True) - xhat * jnp.mean(dxh * xhat, axis=-1, keepdims=True))


def _out_ln1(a_mla, a_dil, w_o, x, g, b, *, tm):
    s = x.shape[0]
    half = HEADS * HEAD_DIM

    def body(am_ref, ad_ref, w_ref, x_ref, g_ref, b_ref, x1_ref, xh_ref, r_ref):
        mix = _dot(_mx(am_ref[...]), w_ref[0:half, :]) + _dot(_mx(ad_ref[...]), w_ref[half:2 * half, :])
        z = DN_ALPHA * x_ref[...] + mix
        y, xhat, rstd = _ln_fwd(z, g_ref[...], b_ref[...])
        x1_ref[...] = y
        xh_ref[...] = xhat
        r_ref[...] = rstd

    row = lambda w: pl.BlockSpec((tm, w), lambda i: (i, 0))
    full = lambda shp: pl.BlockSpec(shp, lambda i: (0,) * len(shp))
    act = jax.ShapeDtypeStruct((s, D_MODEL), F32)
    return pl.pallas_call(
        body, name="out_ln1",
        out_shape=(act, act, jax.ShapeDtypeStruct((s, 1), F32)),
        grid=(s // tm,),
        in_specs=[row(half), row(half), full((D_MODEL, D_MODEL)), row(D_MODEL), full((1, D_MODEL)), full((1, D_MODEL))],
        out_specs=(row(D_MODEL), row(D_MODEL), row(1)),
        compiler_params=_params("parallel"),
    )(a_mla, a_dil, w_o, x, g, b)


def _down_ln2_loss(act, w_down, x1, g, b, target, *, tm):
    s = x1.shape[0]

    def body(a_ref, w_ref, x1_ref, g_ref, b_ref, t_ref, dz_ref, loss_ref, dg_ref, db_ref):
        i = pl.program_id(0)

        @pl.when(i == 0)
        def _():
            loss_ref[...] = jnp.zeros_like(loss_ref)
            dg_ref[...] = jnp.zeros_like(dg_ref)
            db_ref[...] = jnp.zeros_like(db_ref)

        gam = g_ref[...]
        z = DN_ALPHA * x1_ref[...] + _dot(a_ref[...], w_ref[...])
        y, xhat, rstd = _ln_fwd(z, gam, b_ref[...])
        err = y - t_ref[...]
        loss_ref[...] += 0.5 * jnp.sum(jnp.mean(err * err, axis=-1, keepdims=True))
        dy = err * (1.0 / D_MODEL)
        dg_ref[...] += jnp.sum(dy * xhat, axis=0, keepdims=True)
        db_ref[...] += jnp.sum(dy, axis=0, keepdims=True)
        dz_ref[...] = _ln_bwd(dy, xhat, rstd, gam)

    row = lambda w: pl.BlockSpec((tm, w), lambda i: (i, 0))
    full = lambda shp: pl.BlockSpec(shp, lambda i: (0,) * len(shp))
    vec = jax.ShapeDtypeStruct((1, D_MODEL), F32)
    return pl.pallas_call(
        body, name="down_ln2_loss",
        out_shape=(jax.ShapeDtypeStruct((s, D_MODEL), F32), jax.ShapeDtypeStruct((1, LANES), F32), vec, vec),
        grid=(s // tm,),
        in_specs=[row(D_FF), full((D_FF, D_MODEL)), row(D_MODEL), full((1, D_MODEL)), full((1, D_MODEL)), row(D_MODEL)],
        out_specs=(row(D_MODEL), full((1, LANES)), full((1, D_MODEL)), full((1, D_MODEL))),
        compiler_params=_params("arbitrary"),
    )(act, w_down, x1, g, b, target)


def _up_bwd_ln1(du_a, du_g, w_up_t, dz2, xhat1, rstd1, g, *, tm):
    s = dz2.shape[0]

    def body(dua_ref, dug_ref, wa_ref, wg_ref, dz2_ref, xh_ref, r_ref, g_ref, dz1_ref, dg_ref, db_ref):
        i = pl.program_id(0)

        @pl.when(i == 0)
        def _():
            dg_ref[...] = jnp.zeros_like(dg_ref)
            db_ref[...] = jnp.zeros_like(db_ref)

        dx1 = DN_ALPHA * dz2_ref[...] + (_dot(dua_ref[...], wa_ref[...]) + _dot(dug_ref[...], wg_ref[...]))
        xhat = xh_ref[...]
        dg_ref[...] += jnp.sum(dx1 * xhat, axis=0, keepdims=True)
        db_ref[...] += jnp.sum(dx1, axis=0, keepdims=True)
        dz1_ref[...] = _ln_bwd(dx1, xhat, r_ref[...], g_ref[...])

    row = lambda w: pl.BlockSpec((tm, w), lambda i: (i, 0))
    full = lambda shp: pl.BlockSpec(shp, lambda i: (0,) * len(shp))
    vec = jax.ShapeDtypeStruct((1, D_MODEL), F32)
    return pl.pallas_call(
        body, name="up_bwd_ln1",
        out_shape=(jax.ShapeDtypeStruct((s, D_MODEL), F32), vec, vec),
        grid=(s // tm,),
        in_specs=[row(D_FF), row(D_FF),
                  pl.BlockSpec((D_FF, D_MODEL), lambda i: (0, 0)), pl.BlockSpec((D_FF, D_MODEL), lambda i: (1, 0)),
                  row(D_MODEL), row(D_MODEL), row(1), full((1, D_MODEL))],
        out_specs=(row(D_MODEL), full((1, D_MODEL)), full((1, D_MODEL))),
        compiler_params=_params("arbitrary"),
    )(du_a, du_g, w_up_t, w_up_t, dz2, xhat1, rstd1, g)


GELU_C = math.sqrt(2.0 / math.pi)


def _gelu(x):
    cdf = 0.5 * (1.0 + jnp.tanh(GELU_C * (x + 0.044715 * (x * x * x))))
    return x * cdf


def _gelu_grad(x):
    t = jnp.tanh(GELU_C * (x + 0.044715 * (x * x * x)))
    return 0.5 * (1.0 + t) + 0.5 * x * (1.0 - t * t) * (GELU_C * (1.0 + 3.0 * 0.044715 * (x * x)))


def _shift_down(u, halo):
    t = u.shape[0]
    row = lax.broadcasted_iota(jnp.int32, u.shape, 0)
    h7, h6 = halo[7:8, :], halo[6:7, :]
    s1 = jnp.where(row == 0, h7, pltpu.roll(u, 1, 0))
    s2 = jnp.where(row == 0, h6, jnp.where(row == 1, h7, pltpu.roll(u, 2, 0)))
    return s1, s2


def _shift_up(d, nxt):
    t = d.shape[0]
    row = lax.broadcasted_iota(jnp.int32, d.shape, 0)
    n0, n1 = nxt[0:1, :], nxt[1:2, :]
    s1 = jnp.where(row == t - 1, n0, pltpu.roll(d, t - 1, 0))
    s2 = jnp.where(row == t - 1, n1, jnp.where(row == t - 2, n0, pltpu.roll(d, t - 2, 0)))
    return s1, s2


def _conv(u, s1, s2, w, b):
    return ((b + w[0:1, :] * s2) + w[1:2, :] * s1) + w[2:3, :] * u


def _gate_fwd(u, conv_w, conv_b, *, tm, tn):
    s = u.shape[0]
    nj = D_FF // tn
    hb = tm // SUBLANES

    def body(ua_ref, ug_ref, ha_ref, hg_ref, wa_ref, wg_ref, ba_ref, bg_ref, o_ref):
        keep = pl.program_id(0) > 0
        ua, ug = ua_ref[...], ug_ref[...]
        ha = jnp.where(keep, ha_ref[...], 0.0)
        hg = jnp.where(keep, hg_ref[...], 0.0)
        a = _conv(ua, *_shift_down(ua, ha), wa_ref[...], ba_ref[...])
        g = _conv(ug, *_shift_down(ug, hg), wg_ref[...], bg_ref[...])
        o_ref[...] = (_gelu(g) * a).astype(o_ref.dtype)

    main = lambda off: pl.BlockSpec((tm, tn), lambda i, j: (i, j + off))
    halo = lambda off: pl.BlockSpec((SUBLANES, tn), lambda i, j: (jnp.maximum(i * hb - 1, 0), j + off))
    wspec = lambda r, off: pl.BlockSpec((r, tn), lambda i, j: (0, j + off))
    return pl.pallas_call(
        body, name="gate_fwd",
        out_shape=jax.ShapeDtypeStruct((s, D_FF), MXU_DTYPE),
        grid=(s // tm, nj),
        in_specs=[main(0), main(nj), halo(0), halo(nj), wspec(3, 0), wspec(3, nj), wspec(1, 0), wspec(1, nj)],
        out_specs=pl.BlockSpec((tm, tn), lambda i, j: (i, j)),
        compiler_params=_params("parallel", "parallel"),
    )(u, u, u, u, conv_w, conv_w, conv_b, conv_b)


def _gate_bwd(u, dact, conv_w, conv_b, *, tm, tn):
    s = u.shape[0]
    nj = D_FF // tn
    ni = s // tm
    hb = tm // SUBLANES

    def body(ua_ref, ug_ref, ha_ref, hg_ref, na_ref, ng_ref, d_ref, dn_ref, wa_ref, wg_ref, ba_ref, bg_ref,
             dua_ref, dug_ref, dwa_ref, dwg_ref, dba_ref, dbg_ref):
        i = pl.program_id(1)

        @pl.when(i == 0)
        def _():
            for r in (dwa_ref, dwg_ref, dba_ref, dbg_ref):
                r[...] = jnp.zeros_like(r)

        wa, wg, ba, bg = wa_ref[...], wg_ref[...], ba_ref[...], bg_ref[...]
        ua, ug = ua_ref[...], ug_ref[...]
        ha = jnp.where(i > 0, ha_ref[...], 0.0)
        hg = jnp.where(i > 0, hg_ref[...], 0.0)
        sa1, sa2 = _shift_down(ua, ha)
        sg1, sg2 = _shift_down(ug, hg)
        a = _conv(ua, sa1, sa2, wa, ba)
        g = _conv(ug, sg1, sg2, wg, bg)
        d = d_ref[...]
        dya = d * _gelu(g)
        dyg = d * a * _gelu_grad(g)
        na, ng = na_ref[...], ng_ref[...]
        a_n = _conv(na, *_shift_down(na, ua[tm - SUBLANES:, :]), wa, ba)
        g_n = _conv(ng, *_shift_down(ng, ug[tm - SUBLANES:, :]), wg, bg)
        dn = jnp.where(i < ni - 1, dn_ref[...], 0.0)
        dya_n = dn * _gelu(g_n)
        dyg_n = dn * a_n * _gelu_grad(g_n)
        da1, da2 = _shift_up(dya, dya_n)
        dg1, dg2 = _shift_up(dyg, dyg_n)
        dua_ref[...] = (wa[2:3, :] * dya + wa[1:2, :] * da1 + wa[0:1, :] * da2).astype(dua_ref.dtype)
        dug_ref[...] = (wg[2:3, :] * dyg + wg[1:2, :] * dg1 + wg[0:1, :] * dg2).astype(dug_ref.dtype)
        ssum = lambda v: jnp.sum(v, axis=0, keepdims=True)
        dwa_ref[...] += jnp.concatenate([ssum(dya * sa2), ssum(dya * sa1), ssum(dya * ua)], axis=0)
        dwg_ref[...] += jnp.concatenate([ssum(dyg * sg2), ssum(dyg * sg1), ssum(dyg * ug)], axis=0)
        dba_ref[...] += ssum(dya)
        dbg_ref[...] += ssum(dyg)

    main = lambda off: pl.BlockSpec((tm, tn), lambda j, i: (i, j + off))
    halo = lambda off: pl.BlockSpec((SUBLANES, tn), lambda j, i: (jnp.maximum(i * hb - 1, 0), j + off))
    nxt = lambda off: pl.BlockSpec((SUBLANES, tn), lambda j, i: (jnp.minimum((i + 1) * hb, s // SUBLANES - 1), j + off))
    wspec = lambda r, off: pl.BlockSpec((r, tn), lambda j, i: (0, j + off))
    return pl.pallas_call(
        body, name="gate_bwd",
        out_shape=(jax.ShapeDtypeStruct((s, D_FF), MXU_DTYPE), jax.ShapeDtypeStruct((s, D_FF), MXU_DTYPE),
                   jax.ShapeDtypeStruct((3, D_FF), F32), jax.ShapeDtypeStruct((3, D_FF), F32),
                   jax.ShapeDtypeStruct((1, D_FF), F32), jax.ShapeDtypeStruct((1, D_FF), F32)),
        grid=(nj, ni),
        in_specs=[main(0), main(nj), halo(0), halo(nj), nxt(0), nxt(nj), main(0), nxt(0),
                  wspec(3, 0), wspec(3, nj), wspec(1, 0), wspec(1, nj)],
        out_specs=(main(0), main(0), wspec(3, 0), wspec(3, 0), wspec(1, 0), wspec(1, 0)),
        compiler_params=_params("parallel", "arbitrary"),
    )(u, u, u, u, u, u, dact, dact, conv_w, conv_w, conv_b, conv_b)


def _prep_weights(w_in, w_uq, w_uk, w_uv, w_o, w_up, w_down):
    return {**_prep_weights_first(w_in, w_uq, w_uk, w_uv), **_prep_weights_late(w_o, w_up, w_down)}


def _prep_weights_late(w_o, w_up, w_down):
    w_o, w_up, w_down = _mx(w_o), _mx(w_up), _mx(w_down)
    return dict(w_o=w_o, w_o_t=w_o.T, w_up=w_up, w_up_t=w_up.T, w_down=w_down, w_down_t=w_down.T)


def _prep_weights_first(w_in, w_uq, w_uk, w_uv):
    c = lambda a: a.astype(MXU_DTYPE)
    w_in = c(w_in)
    z = lambda w: jnp.zeros((D_MODEL, w), MXU_DTYPE)
    r0 = Q_RANK + KV_RANK
    w_in_ext = jnp.concatenate([w_in[:, :r0], z(NOPE), w_in[:, r0:r0 + ROPE], z(32), w_in[:, r0 + ROPE:]], axis=1)
    wq = jnp.pad(c(w_uq).transpose(1, 0, 2), ((0, 0), (0, 0), (0, QK_PAD - NOPE - ROPE)))
    wk = jnp.pad(c(w_uk).transpose(1, 0, 2), ((0, 0), (0, 0), (0, QK_PAD - NOPE)))
    wv = c(w_uv).transpose(1, 0, 2)
    t3 = lambda a: a.transpose(0, 2, 1)
    return dict(w_in=w_in_ext, w_in_t=w_in_ext.T, wq=wq, wq_t=t3(wq), wk=wk, wk_t=t3(wk), wv=wv, wv_t=t3(wv))


def _local_step(x, target, w, g_cq, g_ckv, ln1_g, ln1_b, conv_w, conv_b, ln2_g, ln2_b, comm=None):
    s = x.shape[0]
    tabs = _rope_tables(s)
    r2 = lambda a: a.reshape(1, -1)
    heads = lambda a: a.reshape(s, HEADS, HEAD_DIM).transpose(1, 0, 2)
    unheads = lambda a: a.transpose(1, 0, 2).reshape(s, HEADS * HEAD_DIM)
    cb = r2(conv_b)
    dils = [d for _, d in DIL_PAIRS]

    h, qd, kd, vd = _in_proj(x, w["w_in"], tm=256)
    q, k, v, v_t = _mla_prep_fwd(h, r2(g_cq), r2(g_ckv), w["wq"], w["wk"], w["wv"], w["wv_t"], tabs, tm=256)
    if comm is None:
        o_mla_t, lse_mla = _mla_attn_fwd(q, k, v_t, t=512, g=HEADS)
    else:
        o_mla_t, lse_mla, gathered = _mla_attn_fwd(q, k, v_t, t=512, g=HEADS, late=comm["late"])
        w = {**w, **comm["finish"](gathered)}
    qp = [_perm(qd, d) for d in dils]
    kp = [_perm(kd, d) for d in dils]
    vp = [_perm(vd, d) for d in dils]
    o_bs, lse_bs = [], []
    for i, d in enumerate(dils):
        o_b, l_b = _dil_fwd(qp[i], kp[i], vp[i], d, name=f"dil_fwd_{d}")
        o_bs.append(_unperm(o_b, d))
        lse_bs.append(_unperm_row(l_b, d))
    o_dil, lj = _dil_combine(o_bs, lse_bs, ts=512)
    o_mla = o_mla_t.transpose(2, 0, 1).reshape(s, HEADS * HEAD_DIM)
    x1, xhat1, rstd1 = _out_ln1(o_mla, o_dil, w["w_o"], x, r2(ln1_g), r2(ln1_b), tm=256)
    u = _mm_nn(x1, w["w_up"], name="up_proj", tm=512, tn=1408, tk=D_MODEL)
    act = _gate_fwd(u, conv_w, cb, tm=256, tn=1408)
    dz2, loss, dg2, db2 = _down_ln2_loss(act, w["w_down"], x1, r2(ln2_g), r2(ln2_b), target, tm=256)

    dact = _mm_nn(dz2, w["w_down_t"], name="down_bwd", tm=512, tn=1408, tk=D_MODEL)
    dw_down = _mm_tn(act, dz2, name="dw_down", tm=1408, tn=D_MODEL, ts=512)
    du_a, du_g, dcw_a, dcw_g, dcb_a, dcb_g = _gate_bwd(u, dact, conv_w, cb, tm=256, tn=1408)
    dz1, dg1, db1 = _up_bwd_ln1(du_a, du_g, w["w_up_t"], dz2, xhat1, rstd1, r2(ln1_g), tm=256)
    dw_up = jnp.concatenate([_mm_tn(x1, du_a, name="dw_up_a", tm=D_MODEL, tn=1408, ts=512),
                             _mm_tn(x1, du_g, name="dw_up_g", tm=D_MODEL, tn=1408, ts=512)], axis=1)
    do_mla, do_dil, dd_all = _attn_bwd_heads(dz1, w["w_o_t"], o_mla, o_dil, tm=256)
    dw_o = jnp.concatenate([_mm_tn(o_mla, dz1, name="dw_o_mla", tm=512, tn=D_MODEL, ts=512),
                            _mm_tn(o_dil, dz1, name="dw_o_dil", tm=512, tn=D_MODEL, ts=512)], axis=0)
    dd_all = dd_all.T
    dd_mla, dd_dil = dd_all[:HEADS].reshape(HEADS, 1, s), dd_all[HEADS:].reshape(HEADS, 1, s)
    early = () if comm is None else tuple(comm["halve"]([("w_up", dw_up), ("w_down", dw_down)]))
    dq, dk, dv, *early_slots = _mla_attn_bwd(q, k, v, do_mla, lse_mla, dd_mla, t=512, g=4, early=early)
    parts = []
    for i, d in enumerate(dils):
        g3 = _dil_bwd(qp[i], kp[i], vp[i], _perm(do_dil, d), _perm_row(lj, d), _perm_row(dd_dil, d), d, name=f"dil_bwd_{d}")
        parts.append([_unperm(g, d) for g in g3])
    dh_dil = _dil_merge(parts, ts=512)
    dh_mla, dwq, dwk, dwv, dgq, dgkv = _mla_prep_bwd(h, dq, dk, dv, r2(g_cq), r2(g_ckv),
                                                     w["wq_t"], w["wk_t"], w["wv_t"], tabs, tm=256)
    mla_w = 4 * LANES
    w_in_t = w["w_in_t"]
    grad_x = _mm_nn(dh_mla, w_in_t[:mla_w], name="in_bwd_mla", tm=512, tn=D_MODEL, tk=mla_w, add=dz1, add_scale=DN_ALPHA)
    grad_x = _mm_nn(dh_dil, w_in_t[mla_w:], name="in_bwd_dil", tm=512, tn=D_MODEL, tk=512, add=grad_x)
    dw_mla = _mm_tn(x, dh_mla, name="dw_in_mla", tm=D_MODEL, tn=mla_w, ts=512)
    dw_dil = _mm_tn(x, dh_dil, name="dw_in_dil", tm=D_MODEL, tn=512, ts=512)
    r0 = Q_RANK + KV_RANK
    grads = dict(
        w_in=jnp.concatenate([dw_mla[:, :r0], dw_mla[:, r0 + NOPE:r0 + NOPE + ROPE], dw_dil], axis=1),
        g_cq=dgq[0], g_ckv=dgkv[0],
        w_uq=dwq[:, :, :NOPE + ROPE].transpose(1, 0, 2),
        w_uk=dwk[:, :, :NOPE].transpose(1, 0, 2),
        w_uv=dwv.transpose(1, 0, 2),
        w_o=dw_o, ln1_g=dg1[0], ln1_b=db1[0], w_up=dw_up,
        conv_w=jnp.concatenate([dcw_a, dcw_g], axis=1), conv_b=jnp.concatenate([dcb_a, dcb_g], axis=1)[0],
        w_down=dw_down, ln2_g=dg2[0], ln2_b=db2[0])
    if comm is not None:
        grads["early"] = (early, tuple(early_slots))
    return loss[0, 0], grad_x, grads


N_CHIPS = 4
SHARDED = ("w_in", "w_uq", "w_o", "w_up", "conv_w", "w_down")
COL_SHARDED = ("w_in", "w_up", "conv_w")
SHARD_SHAPE = dict(w_in=(D_MODEL, IN_WIDTH // 4), w_uq=(Q_RANK // 4, HEADS, NOPE + ROPE), w_o=(D_MODEL // 4, D_MODEL),
                   w_up=(D_MODEL, 2 * D_FF // 4), conv_w=(3, 2 * D_FF // 4), w_down=(D_FF // 4, D_MODEL))
SMALL = ("g_cq", "g_ckv", "w_uk", "w_uv", "ln1_g", "ln1_b", "conv_b", "ln2_g", "ln2_b")
SMALL_SHAPE = dict(g_cq=(Q_RANK,), g_ckv=(KV_RANK,), w_uk=(KV_RANK, HEADS, NOPE), w_uv=(KV_RANK, HEADS, HEAD_DIM),
                   ln1_g=(D_MODEL,), ln1_b=(D_MODEL,), conv_b=(2 * D_FF,), ln2_g=(D_MODEL,), ln2_b=(D_MODEL,))
BIG = ("w_in", "w_uq", "w_o", "w_up", "w_down")
BIG_2D = dict(w_in=(D_MODEL, IN_WIDTH // 4), w_uq=(Q_RANK // 4, HEADS * (NOPE + ROPE)), w_o=(D_MODEL // 4, D_MODEL),
              w_up=(D_MODEL, 2 * D_FF // 4), w_down=(D_FF // 4, D_MODEL))
SMALL_G = SMALL + ("conv_w",)
SMALL_G_SHAPE = {**SMALL_SHAPE, "conv_w": (3, 2 * D_FF)}
SMALL_U_SHAPE = {**SMALL_SHAPE, "conv_w": (3, 2 * D_FF // 4)}


def _size(shape):
    return math.prod(shape)


def _padded_rows(n_elems, mult):
    return -(-n_elems // (LANES * mult)) * mult


SHARD_ROWS = {n: _padded_rows(_size(SHARD_SHAPE[n]), SUBLANES) for n in SHARDED}
R_SMALL = -(-sum(_size(SMALL_G_SHAPE[n]) for n in SMALL_G) // (LANES * LANES)) * LANES
GATHER_FIRST = ("w_in", "w_uq")
GATHER_LATE = ("w_o", "w_up", "w_down")
REDUCED_EARLY = ("w_up", "w_down")
REDUCED_LAST = ("w_in", "w_uq", "w_o")


def _rows(a, rows=None):
    flat = a.reshape(-1)
    rows = -(-flat.shape[0] // LANES) if rows is None else rows
    return jnp.pad(flat, (0, rows * LANES - flat.shape[0])).reshape(rows, LANES)


def _blocked(name, g):
    r, c = BIG_2D[name]
    a = g.reshape(r, N_CHIPS, c).transpose(1, 0, 2) if name in COL_SHARDED else g.reshape(N_CHIPS, r, c)
    return a.reshape(N_CHIPS, 2, r // 2, c)


def _pack_flat(t, names):
    return _rows(jnp.concatenate([t[n].astype(F32).reshape(-1) for n in names]), R_SMALL)


def _unpack_flat(buf, names, shapes):
    flat, out, r = buf.reshape(-1), {}, 0
    for n in names:
        out[n] = flat[r:r + _size(shapes[n])].reshape(shapes[n])
        r += _size(shapes[n])
    return out


def _from_chip_blocks(name, blocks):
    shp = SHARD_SHAPE[name]
    a = blocks.reshape(N_CHIPS, -1)[:, :_size(shp)].reshape((N_CHIPS,) + shp)
    if name in COL_SHARDED:
        return a.transpose(1, 0, 2).reshape(shp[0], N_CHIPS * shp[1])
    return a.reshape((N_CHIPS * shp[0],) + shp[1:])


ANY = pl.BlockSpec(memory_space=pl.ANY)
COMM_PARAMS = pltpu.CompilerParams(has_side_effects=True)


def _coords():
    return lax.axis_index("x"), lax.axis_index("y"), lax.axis_index("c")


def _other_chips(x, y):
    return [(1 - x, y), (x, 1 - y), (1 - x, 1 - y)]


def _remote(src, dst, send_sems, recv_sems, k, to):
    return pltpu.make_async_remote_copy(src_ref=src, dst_ref=dst, send_sem=send_sems.at[k], recv_sem=recv_sems.at[k],
                                        device_id=to, device_id_type=MESH)


def _gather_in_steps(wp_ref, wout_ref, send_sems, recv_sems, *, first, mid, last):
    x, y, c = _coords()
    me = 2 * x + y
    sib = (x, y, 1 - c)
    chips = _other_chips(x, y)
    ici = [_remote(wp_ref.at[c], wout_ref.at[me, c], send_sems, recv_sems, j, (px, py, c)) for j, (px, py) in enumerate(chips)]
    fwd = [_remote(wout_ref.at[2 * px + py, c], wout_ref.at[2 * px + py, c], send_sems, recv_sems, 3 + j, sib)
           for j, (px, py) in enumerate(chips)]

    @pl.when(first)
    def _():
        for cp in ici:
            cp.start()

    @pl.when(mid)
    def _():
        for j, (px, py) in enumerate(chips):
            _remote(wp_ref.at[c], wout_ref.at[2 * px + py, c], send_sems, recv_sems, j, (px, py, c)).wait_recv()
            fwd[j].start()

    @pl.when(last)
    def _():
        for j, (px, py) in enumerate(chips):
            k = 2 * px + py
            _remote(wout_ref.at[k, 1 - c], wout_ref.at[k, 1 - c], send_sems, recv_sems, 3 + j, sib).wait_recv()
        for cp in ici + fwd:
            cp.wait_send()


def _exchange_in_steps(ps_refs, ss_refs, send_sems, recv_sems, *, first, last):
    x, y, c = _coords()
    me = 2 * x + y
    chips = _other_chips(x, y)
    n = len(ps_refs)
    sends = [_remote(ps_refs[t].at[2 * px + py], ss_refs[t].at[me], send_sems, recv_sems, j * n + t, (px, py, c))
             for j, (px, py) in enumerate(chips) for t in range(n)]

    @pl.when(first)
    def _():
        for cp in sends:
            cp.start()

    @pl.when(last)
    def _():
        for j, (px, py) in enumerate(chips):
            for t in range(n):
                _remote(ps_refs[t].at[me], ss_refs[t].at[2 * px + py], send_sems, recv_sems, j * n + t, (px, py, c)).wait_recv()
        for cp in sends:
            cp.wait_send()


def _gather_weights(wp, cwp):
    def body(wp_ref, cw_ref, wout_ref, cwout_ref, send_sems, recv_sems):
        x, y, c = _coords()
        me = 2 * x + y
        sib = (x, y, 1 - c)
        chips = _other_chips(x, y)
        sends = [_remote(wp_ref.at[c], wout_ref.at[me, c], send_sems, recv_sems, j, (px, py, c))
                 for j, (px, py) in enumerate(chips)]
        sends += [_remote(cw_ref, cwout_ref.at[me], send_sems, recv_sems, 3 + j, (px, py, c))
                  for j, (px, py) in enumerate(chips)]
        for cp in sends:
            cp.start()
        for j, (px, py) in enumerate(chips):
            k = 2 * px + py
            _remote(wp_ref.at[c], wout_ref.at[k, c], send_sems, recv_sems, j, (px, py, c)).wait_recv()
            fwd = _remote(wout_ref.at[k, c], wout_ref.at[k, c], send_sems, recv_sems, 6 + j, sib)
            fwd.start()
            sends.append(fwd)
        for j, (px, py) in enumerate(chips):
            k = 2 * px + py
            _remote(cw_ref, cwout_ref.at[k], send_sems, recv_sems, 3 + j, (px, py, c)).wait_recv()
            _remote(wout_ref.at[k, 1 - c], wout_ref.at[k, 1 - c], send_sems, recv_sems, 6 + j, sib).wait_recv()
        for cp in sends:
            cp.wait_send()

    return pl.pallas_call(
        body, name="gather_weights",
        out_shape=(jax.ShapeDtypeStruct((N_CHIPS,) + wp.shape, wp.dtype), jax.ShapeDtypeStruct((N_CHIPS,) + cwp.shape, cwp.dtype)),
        in_specs=[ANY, ANY], out_specs=(ANY, ANY),
        scratch_shapes=[pltpu.SemaphoreType.DMA((9,)), pltpu.SemaphoreType.DMA((9,))],
        compiler_params=COMM_PARAMS,
    )(wp, cwp)


def _exchange_sibling_halves(gs, whole, *, name):
    n, nw = len(gs), len(whole)

    def body(*refs):
        gs_refs, wh_refs = refs[:n], refs[n:n + nw]
        os_refs, ow_refs = refs[n + nw:2 * n + nw], refs[2 * n + nw:2 * (n + nw)]
        send_sems, recv_sems = refs[2 * (n + nw):]
        x, y, c = _coords()
        sib = (x, y, 1 - c)
        cps = [_remote(gs_refs[t].at[k, 1 - c], os_refs[t].at[k], send_sems, recv_sems, t * N_CHIPS + k, sib)
               for t in range(n) for k in range(N_CHIPS)]
        cps += [_remote(wh_refs[t], ow_refs[t], send_sems, recv_sems, n * N_CHIPS + t, sib) for t in range(nw)]
        for cp in cps:
            cp.start()
        for cp in cps:
            cp.wait_recv()
        for cp in cps:
            cp.wait_send()

    n_sem = n * N_CHIPS + nw
    return pl.pallas_call(
        body, name=name,
        out_shape=tuple(jax.ShapeDtypeStruct((N_CHIPS,) + a.shape[2:], F32) for a in gs)
        + tuple(jax.ShapeDtypeStruct(a.shape, F32) for a in whole),
        in_specs=[ANY] * (n + nw), out_specs=(ANY,) * (n + nw),
        scratch_shapes=[pltpu.SemaphoreType.DMA((n_sem,)), pltpu.SemaphoreType.DMA((n_sem,))],
        compiler_params=COMM_PARAMS,
    )(*gs, *whole)


def _exchange_chips(ps, pr):
    n = len(ps)

    def body(*refs):
        ps_refs, pr_ref, ss_refs, sr_ref = refs[:n], refs[n], refs[n + 1:2 * n + 1], refs[2 * n + 1]
        send_sems, recv_sems = refs[2 * n + 2:]
        x, y, c = _coords()
        me = 2 * x + y
        chips = _other_chips(x, y)
        sends = []
        for j, (px, py) in enumerate(chips):
            to = (px, py, c)
            for t in range(n):
                sends.append(_remote(ps_refs[t].at[2 * px + py], ss_refs[t].at[me], send_sems, recv_sems, j * (n + 1) + t, to))
            sends.append(_remote(pr_ref, sr_ref.at[me], send_sems, recv_sems, j * (n + 1) + n, to))
        for cp in sends:
            cp.start()
        for j, (px, py) in enumerate(chips):
            k, to = 2 * px + py, (px, py, c)
            for t in range(n):
                _remote(ps_refs[t].at[me], ss_refs[t].at[k], send_sems, recv_sems, j * (n + 1) + t, to).wait_recv()
            _remote(pr_ref, sr_ref.at[k], send_sems, recv_sems, j * (n + 1) + n, to).wait_recv()
        for cp in sends:
            cp.wait_send()

    n_sem = 3 * (n + 1)
    return pl.pallas_call(
        body, name="exchange_chips",
        out_shape=tuple(jax.ShapeDtypeStruct(a.shape, a.dtype) for a in ps) + (jax.ShapeDtypeStruct((N_CHIPS,) + pr.shape, F32),),
        in_specs=[ANY] * (n + 1), out_specs=(ANY,) * (n + 1),
        scratch_shapes=[pltpu.SemaphoreType.DMA((n_sem,)), pltpu.SemaphoreType.DMA((n_sem,))],
        compiler_params=COMM_PARAMS,
    )(*ps, pr)


def _exchange_sibling_result(gh):
    n = len(gh)

    def body(*refs):
        gh_refs, out_refs, (send_sems, recv_sems) = refs[:n], refs[n:2 * n], refs[2 * n:]
        x, y, c = _coords()
        cps = [_remote(gh_refs[t], out_refs[t], send_sems, recv_sems, t, (x, y, 1 - c)) for t in range(n)]
        for cp in cps:
            cp.start()
        for cp in cps:
            cp.wait_recv()
        for cp in cps:
            cp.wait_send()

    return pl.pallas_call(
        body, name="exchange_sibling_result",
        out_shape=tuple(jax.ShapeDtypeStruct(a.shape, F32) for a in gh),
        in_specs=[ANY] * n, out_specs=(ANY,) * n,
        scratch_shapes=[pltpu.SemaphoreType.DMA((n,)), pltpu.SemaphoreType.DMA((n,))],
        compiler_params=COMM_PARAMS,
    )(*gh)


def _add_own_half(gs, recv, c_arr, *, name):
    _, rows, cols = recv.shape

    def body(c_ref, a_ref, b_ref, o_ref):
        o_ref[0] = (a_ref[0, 0] + b_ref[0]).astype(o_ref.dtype)

    return pl.pallas_call(
        body, name=name,
        out_shape=jax.ShapeDtypeStruct(recv.shape, GRAD_WIRE_DTYPE),
        grid_spec=pltpu.PrefetchScalarGridSpec(
            num_scalar_prefetch=1, grid=(N_CHIPS,),
            in_specs=[pl.BlockSpec((1, 1, rows, cols), lambda k, c_ref: (k, c_ref[0], 0, 0)),
                      pl.BlockSpec((1, rows, cols), lambda k, c_ref: (k, 0, 0))],
            out_specs=pl.BlockSpec((1, rows, cols), lambda k, c_ref: (k, 0, 0))),
        compiler_params=_params("parallel"),
    )(c_arr, gs, recv)


def _add2(a, b, *, name):
    def body(a_ref, b_ref, o_ref):
        o_ref[...] = a_ref[...] + b_ref[...]

    return pl.pallas_call(body, name=name, out_shape=jax.ShapeDtypeStruct(a.shape, F32))(a, b)


def _sum_slots(slots, *, tr, name):
    _, r, c = slots.shape

    def body(s_ref, o_ref):
        f = lambda k: s_ref[k].astype(F32)
        o_ref[...] = ((f(0) + f(1)) + f(2)) + f(3)

    return pl.pallas_call(
        body, name=name,
        out_shape=jax.ShapeDtypeStruct((r, c), F32),
        grid=(r // tr,),
        in_specs=[pl.BlockSpec((N_CHIPS, tr, c), lambda i: (0, i, 0))],
        out_specs=pl.BlockSpec((tr, c), lambda i: (i, 0)),
        compiler_params=_params("parallel"),
    )(slots)


def _adamw(w, g, m, v, *, tr, name):
    r, cols = w.shape

    def body(w_ref, g_ref, m_ref, v_ref, d_ref, nm_ref, nv_ref):
        g_ = g_ref[...]
        m_ = ADAM_B1 * m_ref[...] + (1.0 - ADAM_B1) * g_
        v_ = ADAM_B2 * v_ref[...] + (1.0 - ADAM_B2) * (g_ * g_)
        m_hat = m_ / (1.0 - ADAM_B1 ** ADAM_STEP)
        v_hat = v_ / (1.0 - ADAM_B2 ** ADAM_STEP)
        d_ref[...] = -ADAM_LR * (m_hat / (jnp.sqrt(v_hat) + ADAM_EPS) + ADAM_WD * w_ref[...])
        nm_ref[...] = m_
        nv_ref[...] = v_

    spec = pl.BlockSpec((tr, cols), lambda i: (i, 0))
    out = jax.ShapeDtypeStruct((r, cols), F32)
    return pl.pallas_call(
        body, name=name, out_shape=(out, out, out), grid=(r // tr,),
        in_specs=[spec] * 4, out_specs=(spec,) * 3,
        compiler_params=_params("parallel"),
    )(w, g, m, v)


WEIGHTS = ("w_in", "g_cq", "g_ckv", "w_uq", "w_uk", "w_uv", "w_o", "ln1_g", "ln1_b", "w_up", "conv_w", "conv_b",
           "w_down", "ln2_g", "ln2_b")


def kernel(x, w_in, g_cq, g_ckv, w_uq, w_uk, w_uv, w_o, ln1_g, ln1_b, w_up, conv_w, conv_b, w_down, ln2_g, ln2_b, loss_target, m_w_in, m_g_cq, m_g_ckv, m_w_uq, m_w_uk, m_w_uv, m_w_o, m_ln1_g, m_ln1_b, m_w_up, m_conv_w, m_conv_b, m_w_down, m_ln2_g, m_ln2_b, v_w_in, v_g_cq, v_g_ckv, v_w_uq, v_w_uk, v_w_uv, v_w_o, v_ln1_g, v_ln1_b, v_w_up, v_conv_w, v_conv_b, v_w_down, v_ln2_g, v_ln2_b):
    wts = dict(zip(WEIGHTS, (w_in, g_cq, g_ckv, w_uq, w_uk, w_uv, w_o, ln1_g, ln1_b, w_up, conv_w, conv_b, w_down, ln2_g, ln2_b)))
    mom = dict(zip(WEIGHTS, (m_w_in, m_g_cq, m_g_ckv, m_w_uq, m_w_uk, m_w_uv, m_w_o, m_ln1_g, m_ln1_b, m_w_up, m_conv_w, m_conv_b, m_w_down, m_ln2_g, m_ln2_b)))
    var = dict(zip(WEIGHTS, (v_w_in, v_g_cq, v_g_ckv, v_w_uq, v_w_uk, v_w_uv, v_w_o, v_ln1_g, v_ln1_b, v_w_up, v_conv_w, v_conv_b, v_w_down, v_ln2_g, v_ln2_b)))

    me = 2 * lax.axis_index("x") + lax.axis_index("y")
    my_c = lax.axis_index("c")
    c_arr = my_c.astype(jnp.int32).reshape(1)
    own = lambda slots, mine: lax.dynamic_update_index_in_dim(slots, mine, me, 0)

    def pack(names):
        return jnp.concatenate([_rows(_mx(wts[n]), SHARD_ROWS[n]) for n in names], axis=0).reshape(2, -1, LANES)

    def unpack(names, gathered, mine):
        buf, full, r = own(gathered, mine).reshape(N_CHIPS, -1, LANES), {}, 0
        for n in names:
            full[n] = _from_chip_blocks(n, buf[:, r:r + SHARD_ROWS[n]])
            r += SHARD_ROWS[n]
        return full

    wp_first, wp_late = pack(GATHER_FIRST), pack(GATHER_LATE)
    cwp = _rows(conv_w, SHARD_ROWS["conv_w"])
    gathered, cwfull = _gather_weights(wp_first, cwp)
    full = unpack(GATHER_FIRST, gathered, wp_first)
    conv_w_full = _from_chip_blocks("conv_w", own(cwfull, cwp))
    w = _prep_weights_first(full["w_in"], full["w_uq"], w_uk, w_uv)

    def finish(gathered_late):
        late = unpack(GATHER_LATE, gathered_late, wp_late)
        return _prep_weights_late(late["w_o"], late["w_up"], late["w_down"])

    def halve(named, whole=(), tag="early"):
        gb = [_blocked(n, a) for n, a in named]
        recv = _exchange_sibling_halves(gb, list(whole), name=f"exchange_sibling_halves_{tag}")
        ps = [_add_own_half(gb[i], recv[i], c_arr, name=f"add_half_{n}") for i, (n, _) in enumerate(named)]
        return ps + [_add2(a, recv[len(gb) + i], name=f"add_whole_{tag}_{i}") for i, a in enumerate(whole)]

    comm = dict(late=wp_late, finish=finish, halve=halve)
    loss, grad_x, g = _local_step(x[0], loss_target[0], w, g_cq, g_ckv, ln1_g, ln1_b, conv_w_full, conv_b, ln2_g, ln2_b, comm=comm)
    loss = lax.psum(loss, ("x", "y", "c"))

    ps_early, slots_early = g.pop("early")
    *ps_rest, pr = halve([(n, g[n]) for n in REDUCED_LAST], whole=[_pack_flat(g, SMALL_G)], tag="last")
    *slots_rest, slots_r = _exchange_chips(ps_rest, pr)
    ps = {**dict(zip(REDUCED_LAST, ps_rest)), **dict(zip(REDUCED_EARLY, ps_early))}
    slots = {**dict(zip(REDUCED_LAST, slots_rest)), **dict(zip(REDUCED_EARLY, slots_early))}
    slots = [own(slots[n], lax.dynamic_index_in_dim(ps[n], me, 0, keepdims=False)) for n in BIG]
    slots_r = own(slots_r, pr)
    g_half = [_sum_slots(slots[i], tr=slots[i].shape[1] // 2, name=f"sum_chips_{n}") for i, n in enumerate(BIG)]
    g_small = _unpack_flat(_sum_slots(slots_r, tr=R_SMALL, name="sum_chips_small"), SMALL_G, SMALL_G_SHAPE)
    g_other = _exchange_sibling_result(g_half)
    grads = {n: jnp.where(my_c == 0, jnp.concatenate([g_half[i], g_other[i]]), jnp.concatenate([g_other[i], g_half[i]]))
             for i, n in enumerate(BIG)}
    g_small["conv_w"] = lax.dynamic_slice_in_dim(g_small["conv_w"], me * SHARD_SHAPE["conv_w"][1], SHARD_SHAPE["conv_w"][1], 1)
    grads.update(g_small)

    res = {}
    for n in BIG:
        as2d = lambda a: a.reshape(BIG_2D[n])
        d, m, v = _adamw(as2d(wts[n]), grads[n], as2d(mom[n]), as2d(var[n]), tr=BIG_2D[n][0] // 4, name=f"adamw_{n}")
        res[n] = [a.reshape(SHARD_SHAPE[n]) for a in (grads[n], d, m, v)]
    flat = lambda t: _pack_flat(t, SMALL_G)
    dmv = _adamw(flat(wts), flat(g_small), flat(mom), flat(var), tr=R_SMALL, name="adamw_small")
    dmv = [_unpack_flat(a, SMALL_G, SMALL_U_SHAPE) for a in dmv]
    for n in SMALL_G:
        res[n] = [g_small[n]] + [t[n] for t in dmv]
    outs = [res[n][j] for j in range(4) for n in WEIGHTS]
    return (loss, grad_x[None], *outs)
```

```python
import functools
import math

import jax
import jax.numpy as jnp
from jax import lax
from jax.experimental import pallas as pl
from jax.experimental.pallas import tpu as pltpu

F32 = jnp.float32
MXU_DTYPE = jnp.bfloat16
GRAD_WIRE_DTYPE = jnp.bfloat16
NEG = -1e30

D_MODEL = 1024
HEADS = 8
HEAD_DIM = 64
Q_RANK = 256
KV_RANK = 128
NOPE = 64
ROPE = 32
QK_PAD = 128
IN_WIDTH = 1952
IN_EXT = 2048
D_FF = 2816
DIL_PAIRS = ((128, 1), (512, 4), (2048, 16))
DIL_BLOCK = 128
ROPE_THETA = 10000.0
DN_ALPHA = 2.0 ** 0.25
LN_EPS = 1e-5
RMS_EPS = 1e-6
MLA_SCALE = 1.0 / math.sqrt(NOPE + ROPE)
DIL_SCALE = 1.0 / math.sqrt(HEAD_DIM)

ADAM_LR = 0.001
ADAM_B1 = 0.9
ADAM_B2 = 0.999
ADAM_EPS = 1e-08
ADAM_WD = 0.01
ADAM_STEP = 10

LANES = 128
SUBLANES = 8
VMEM_LIMIT_BYTES = 56 * 1024 * 1024

MESH = pl.DeviceIdType.MESH


def _params(*sem):
    return pltpu.CompilerParams(dimension_semantics=sem, vmem_limit_bytes=VMEM_LIMIT_BYTES)


def _dot(a, b):
    return jnp.dot(a, b, preferred_element_type=F32)


def _dot_nt(a, b):
    return lax.dot_general(a, b, (((1,), (1,)), ((), ())), preferred_element_type=F32)


def _dot_tn(a, b):
    return lax.dot_general(a, b, (((0,), (0,)), ((), ())), preferred_element_type=F32)


def _mx(a):
    return a.astype(MXU_DTYPE)


def _mm_nn(a, b, *, name, tm, tn, tk, out_dtype=F32, add=None, add_scale=1.0):
    m, kdim = a.shape
    n = b.shape[1]
    nk = kdim // tk

    def body(*refs):
        if add is None:
            a_ref, b_ref, o_ref, acc = refs
        else:
            a_ref, b_ref, c_ref, o_ref, acc = refs
        k = pl.program_id(2)

        @pl.when(k == 0)
        def _():
            acc[...] = jnp.zeros_like(acc)

        acc[...] += _dot(_mx(a_ref[...]), _mx(b_ref[...]))

        @pl.when(k == nk - 1)
        def _():
            r = acc[...]
            if add is not None:
                r = r + add_scale * c_ref[...]
            o_ref[...] = r.astype(out_dtype)

    in_specs = [pl.BlockSpec((tm, tk), lambda i, j, k: (i, k)),
                pl.BlockSpec((tk, tn), lambda i, j, k: (k, j))]
    args = [a, b]
    if add is not None:
        in_specs.append(pl.BlockSpec((tm, tn), lambda i, j, k: (i, j)))
        args.append(add)
    return pl.pallas_call(
        body, name=name,
        out_shape=jax.ShapeDtypeStruct((m, n), out_dtype),
        grid=(m // tm, n // tn, nk),
        in_specs=in_specs,
        out_specs=pl.BlockSpec((tm, tn), lambda i, j, k: (i, j)),
        scratch_shapes=[pltpu.VMEM((tm, tn), F32)],
        compiler_params=_params("parallel", "parallel", "arbitrary"),
    )(*args)


def _mm_tn(a, b, *, name, tm, tn, ts, out_dtype=F32):
    s, m = a.shape
    n = b.shape[1]
    ns = s // ts

    def body(a_ref, b_ref, o_ref, acc):
        k = pl.program_id(2)

        @pl.when(k == 0)
        def _():
            acc[...] = jnp.zeros_like(acc)

        acc[...] += _dot_tn(_mx(a_ref[...]), _mx(b_ref[...]))

        @pl.when(k == ns - 1)
        def _():
            o_ref[...] = acc[...].astype(out_dtype)

    return pl.pallas_call(
        body, name=name,
        out_shape=jax.ShapeDtypeStruct((m, n), out_dtype),
        grid=(m // tm, n // tn, ns),
        in_specs=[pl.BlockSpec((ts, tm), lambda i, j, k: (k, i)),
                  pl.BlockSpec((ts, tn), lambda i, j, k: (k, j))],
        out_specs=pl.BlockSpec((tm, tn), lambda i, j, k: (i, j)),
        scratch_shapes=[pltpu.VMEM((tm, tn), F32)],
        compiler_params=_params("parallel", "parallel", "arbitrary"),
    )(a, b)


def _in_proj(x, w_in_ext, *, tm):
    s = x.shape[0]
    mla_w = 4 * LANES
    dil_w = HEADS * HEAD_DIM
    dils = [d for _, d in DIL_PAIRS]

    def body(x_ref, w_ref, h_ref, *rest):
        outs, sc = rest[:-1], rest[-1]
        xb = _mx(x_ref[...])
        h_ref[...] = _dot(xb, w_ref[:, 0:mla_w])
        for j in range(3):
            part = _dot(xb, w_ref[:, mla_w + j * dil_w:mla_w + (j + 1) * dil_w])
            for hd in range(HEADS):
                sc[hd] = part[:, hd * HEAD_DIM:(hd + 1) * HEAD_DIM]
            for b, d in enumerate(dils):
                _store_residue_major(outs[3 * j + b], sc, d, tm)

    shapes, specs = _residue_major_outs(s, tm, dils, MXU_DTYPE)
    res = pl.pallas_call(
        body, name="in_proj",
        out_shape=(jax.ShapeDtypeStruct((s, mla_w), F32),) + shapes * 3,
        grid=(s // tm,),
        in_specs=[pl.BlockSpec((tm, D_MODEL), lambda i: (i, 0)), pl.BlockSpec((D_MODEL, IN_EXT), lambda i: (0, 0))],
        out_specs=(pl.BlockSpec((tm, mla_w), lambda i: (i, 0)),) + specs * 3,
        scratch_shapes=[pltpu.VMEM((HEADS, tm, HEAD_DIM), F32)],
        compiler_params=_params("parallel"),
    )(x, w_in_ext)
    hm = lambda a: a.reshape(HEADS, s, HEAD_DIM)
    return res[0], [hm(a) for a in res[1:4]], [hm(a) for a in res[4:7]], [hm(a) for a in res[7:10]]


def _residue_major_outs(s, tm, dils, dtype):
    shapes, specs = [], []
    for d in dils:
        if d == 1:
            shapes.append(jax.ShapeDtypeStruct((HEADS, s, HEAD_DIM), dtype))
            specs.append(pl.BlockSpec((HEADS, tm, HEAD_DIM), lambda i: (0, i, 0)))
        else:
            shapes.append(jax.ShapeDtypeStruct((HEADS, d, s // d, HEAD_DIM), dtype))
            specs.append(pl.BlockSpec((HEADS, d, tm // d, HEAD_DIM), lambda i: (0, 0, i, 0)))
    return tuple(shapes), tuple(specs)


def _store_residue_major(o_ref, src_ref, d, tm):
    if d == 1:
        o_ref[...] = src_ref[...].astype(o_ref.dtype)
    else:
        for r in range(d):
            o_ref[:, r] = src_ref[:, pl.ds(r, tm // d, stride=d), :].astype(o_ref.dtype)


def _load_token_order(dst_ref, src_ref, d, tm, accumulate=False):
    if d == 1:
        dst_ref[...] = dst_ref[...] + src_ref[...] if accumulate else src_ref[...]
    else:
        for r in range(d):
            rows = pl.ds(r, tm // d, stride=d)
            dst_ref[:, rows, :] = dst_ref[:, rows, :] + src_ref[:, r] if accumulate else src_ref[:, r]


def _attn_bwd_heads(dz1, w_o_t, a_mla, a_dil, *, tm):
    s = dz1.shape[0]
    half = HEADS * HEAD_DIM
    dils = [d for _, d in DIL_PAIRS]

    def body(dz_ref, w_ref, am_ref, ad_ref, dom_ref, dd_ref, *dod_refs):
        dzb = _mx(dz_ref[...])
        for j, (a_ref, o_ref) in enumerate(((am_ref, dom_ref), (ad_ref, dod_refs[0]))):
            da = _dot(dzb, w_ref[:, j * half:(j + 1) * half])
            prod = da * a_ref[...]
            for hd in range(HEADS):
                sl = slice(hd * HEAD_DIM, (hd + 1) * HEAD_DIM)
                o_ref[hd] = da[:, sl].astype(o_ref.dtype)
                dd_ref[:, j * HEADS + hd:j * HEADS + hd + 1] = jnp.sum(prod[:, sl], axis=-1, keepdims=True)
        for b, d in enumerate(dils[1:]):
            _store_residue_major(dod_refs[1 + b], dod_refs[0], d, tm)

    hspec = pl.BlockSpec((HEADS, tm, HEAD_DIM), lambda i: (0, i, 0))
    row = lambda w: pl.BlockSpec((tm, w), lambda i: (i, 0))
    shapes, specs = _residue_major_outs(s, tm, dils, F32)
    do_mla, dd, *do_dil = pl.pallas_call(
        body, name="attn_bwd_heads",
        out_shape=(jax.ShapeDtypeStruct((HEADS, s, HEAD_DIM), MXU_DTYPE), jax.ShapeDtypeStruct((s, 2 * HEADS), F32)) + shapes,
        grid=(s // tm,),
        in_specs=[row(D_MODEL), pl.BlockSpec((D_MODEL, D_MODEL), lambda i: (0, 0)), row(half), row(half)],
        out_specs=(hspec, row(2 * HEADS)) + specs,
        compiler_params=_params("parallel"),
    )(dz1, w_o_t, a_mla, a_dil)
    return do_mla, [a.reshape(HEADS, s, HEAD_DIM) for a in do_dil], dd


def _dil_merge(parts, *, ts):
    hds, s, e = parts[0][0].shape
    dils = [d for _, d in DIL_PAIRS]

    def body(*refs):
        o_ref, sc = refs[9], refs[10]
        for j in range(3):
            for b, d in enumerate(dils):
                _load_token_order(sc, refs[3 * b + j], d, ts, accumulate=b > 0)
            tot = sc[...]
            for hd in range(hds):
                col = j * hds * e + hd * e
                o_ref[:, col:col + e] = tot[hd].astype(o_ref.dtype)

    _, specs = _residue_major_outs(s, ts, dils, F32)
    view = lambda a, d: a if d == 1 else a.reshape(hds, d, s // d, e)
    return pl.pallas_call(
        body, name="dil_merge",
        out_shape=jax.ShapeDtypeStruct((s, 3 * hds * e), MXU_DTYPE),
        grid=(s // ts,),
        in_specs=[specs[b] for b in range(3) for _ in range(3)],
        out_specs=pl.BlockSpec((ts, 3 * hds * e), lambda i: (i, 0)),
        scratch_shapes=[pltpu.VMEM((hds, ts, e), F32)],
        compiler_params=_params("parallel"),
    )(*[view(parts[b][j], dils[b]) for b in range(3) for j in range(3)])


def _rope_tables(s):
    half = ROPE // 2
    freqs = ROPE_THETA ** (-jnp.arange(half, dtype=F32) / half)
    ang = jnp.arange(s).astype(F32)[:, None] * freqs[None, :]
    cos, sin = jnp.cos(ang), jnp.sin(ang)
    z = lambda w: jnp.zeros((s, w), F32)
    c = jnp.concatenate([jnp.ones((s, NOPE), F32), cos, cos, z(32)], axis=1)
    s1 = jnp.concatenate([z(NOPE + half), sin, z(32)], axis=1)
    s2 = jnp.concatenate([z(NOPE), -sin, z(half + 32)], axis=1)
    mask = jnp.concatenate([z(NOPE), jnp.ones((s, ROPE), F32), z(32)], axis=1)
    return c, s1, s2, mask


def _rope(x, c, s1, s2):
    return x * c + pltpu.roll(x, 16, 1) * s1 + pltpu.roll(x, LANES - 16, 1) * s2


def _unrope(dy, c, s1, s2):
    return dy * c + pltpu.roll(dy * s1, LANES - 16, 1) + pltpu.roll(dy * s2, 16, 1)


def _rms(x):
    r = lax.rsqrt(jnp.mean(x * x, axis=-1, keepdims=True) + RMS_EPS)
    return x * r, r


def _mla_prep_fwd(h, g_cq, g_ckv, wq, wk, wv, wv_t, tabs, *, tm):
    s = h.shape[0]
    c_t, s1_t, s2_t, _ = tabs

    def body(h_ref, gq_ref, gkv_ref, wq_ref, wk_ref, wv_ref, wvt_ref, c_ref, s1_ref, s2_ref,
             q_ref, k_ref, v_ref, vt_ref):
        cq = h_ref[:, 0:Q_RANK]
        ckv = h_ref[:, Q_RANK:Q_RANK + KV_RANK]
        kr = h_ref[:, Q_RANK + KV_RANK:Q_RANK + KV_RANK + QK_PAD]
        c, s1, s2 = c_ref[...], s1_ref[...], s2_ref[...]
        cqn = _mx(_rms(cq)[0] * gq_ref[...])
        ckvn = _mx(_rms(ckv)[0] * gkv_ref[...])
        kr_rot = _rope(kr, c, s1, s2)
        for hd in range(HEADS):
            q_ref[hd] = _rope(_dot(cqn, wq_ref[hd]), c, s1, s2).astype(q_ref.dtype)
            k_ref[hd] = (_dot(ckvn, wk_ref[hd]) + kr_rot).astype(k_ref.dtype)
            v_ref[hd] = _dot(ckvn, wv_ref[hd]).astype(v_ref.dtype)
            vt_ref[hd] = _dot_nt(wvt_ref[hd], ckvn).astype(vt_ref.dtype)

    full = lambda shp: pl.BlockSpec(shp, lambda i: (0,) * len(shp))
    row = lambda w: pl.BlockSpec((tm, w), lambda i: (i, 0))
    return pl.pallas_call(
        body, name="mla_prep_fwd",
        out_shape=(jax.ShapeDtypeStruct((HEADS, s, QK_PAD), MXU_DTYPE),
                   jax.ShapeDtypeStruct((HEADS, s, QK_PAD), MXU_DTYPE),
                   jax.ShapeDtypeStruct((HEADS, s, HEAD_DIM), MXU_DTYPE),
                   jax.ShapeDtypeStruct((HEADS, HEAD_DIM, s), MXU_DTYPE)),
        grid=(s // tm,),
        in_specs=[row(4 * LANES), full((1, Q_RANK)), full((1, KV_RANK)),
                  full((HEADS, Q_RANK, QK_PAD)), full((HEADS, KV_RANK, QK_PAD)), full((HEADS, KV_RANK, HEAD_DIM)),
                  full((HEADS, HEAD_DIM, KV_RANK)), row(LANES), row(LANES), row(LANES)],
        out_specs=(pl.BlockSpec((HEADS, tm, QK_PAD), lambda i: (0, i, 0)),
                   pl.BlockSpec((HEADS, tm, QK_PAD), lambda i: (0, i, 0)),
                   pl.BlockSpec((HEADS, tm, HEAD_DIM), lambda i: (0, i, 0)),
                   pl.BlockSpec((HEADS, HEAD_DIM, tm), lambda i: (0, 0, i))),
        compiler_params=_params("parallel"),
    )(h, g_cq, g_ckv, wq, wk, wv, wv_t, c_t, s1_t, s2_t)


def _mla_prep_bwd(h, dq, dk, dv, g_cq, g_ckv, wq_t, wk_t, wv_t, tabs, *, tm):
    s = h.shape[0]
    c_t, s1_t, s2_t, mask_t = tabs

    def body(h_ref, dq_ref, dk_ref, dv_ref, gq_ref, gkv_ref, wqt_ref, wkt_ref, wvt_ref,
             c_ref, s1_ref, s2_ref, mask_ref, dh_ref, dwq_ref, dwk_ref, dwv_ref, dgq_ref, dgkv_ref):
        i = pl.program_id(0)

        @pl.when(i == 0)
        def _():
            dwq_ref[...] = jnp.zeros_like(dwq_ref)
            dwk_ref[...] = jnp.zeros_like(dwk_ref)
            dwv_ref[...] = jnp.zeros_like(dwv_ref)
            dgq_ref[...] = jnp.zeros_like(dgq_ref)
            dgkv_ref[...] = jnp.zeros_like(dgkv_ref)

        cq = h_ref[:, 0:Q_RANK]
        ckv = h_ref[:, Q_RANK:Q_RANK + KV_RANK]
        c, s1, s2 = c_ref[...], s1_ref[...], s2_ref[...]
        cqh, rq = _rms(cq)
        ckvh, rkv = _rms(ckv)
        gq, gkv = gq_ref[...], gkv_ref[...]
        cqn = _mx(cqh * gq)
        ckvn = _mx(ckvh * gkv)
        dcqn = jnp.zeros((tm, Q_RANK), F32)
        dckvn = jnp.zeros((tm, KV_RANK), F32)
        dkr = jnp.zeros((tm, QK_PAD), F32)
        for hd in range(HEADS):
            dqh = _mx(_unrope(dq_ref[hd], c, s1, s2))
            dcqn = dcqn + _dot(dqh, wqt_ref[hd])
            dwq_ref[hd] += _dot_tn(cqn, dqh)
            dkh = dk_ref[hd]
            dkr = dkr + dkh
            dkh = _mx(dkh)
            dckvn = dckvn + _dot(dkh, wkt_ref[hd])
            dwk_ref[hd] += _dot_tn(ckvn, dkh)
            dvh = _mx(dv_ref[hd])
            dckvn = dckvn + _dot(dvh, wvt_ref[hd])
            dwv_ref[hd] += _dot_tn(ckvn, dvh)
        dgq_ref[...] += jnp.sum(dcqn * cqh, axis=0, keepdims=True)
        dgkv_ref[...] += jnp.sum(dckvn * ckvh, axis=0, keepdims=True)
        gd = dcqn * gq
        dh_ref[:, 0:Q_RANK] = rq * (gd - cqh * jnp.mean(gd * cqh, axis=-1, keepdims=True))
        gd = dckvn * gkv
        dh_ref[:, Q_RANK:Q_RANK + KV_RANK] = rkv * (gd - ckvh * jnp.mean(gd * ckvh, axis=-1, keepdims=True))
        dh_ref[:, Q_RANK + KV_RANK:Q_RANK + KV_RANK + QK_PAD] = _unrope(dkr, c, s1, s2) * mask_ref[...]

    full = lambda shp: pl.BlockSpec(shp, lambda i: (0,) * len(shp))
    row = lambda w: pl.BlockSpec((tm, w), lambda i: (i, 0))
    hrow = lambda w: pl.BlockSpec((HEADS, tm, w), lambda i: (0, i, 0))
    return pl.pallas_call(
        body, name="mla_prep_bwd",
        out_shape=(jax.ShapeDtypeStruct((s, 4 * LANES), F32),
                   jax.ShapeDtypeStruct((HEADS, Q_RANK, QK_PAD), F32),
                   jax.ShapeDtypeStruct((HEADS, KV_RANK, QK_PAD), F32),
                   jax.ShapeDtypeStruct((HEADS, KV_RANK, HEAD_DIM), F32),
                   jax.ShapeDtypeStruct((1, Q_RANK), F32),
                   jax.ShapeDtypeStruct((1, KV_RANK), F32)),
        grid=(s // tm,),
        in_specs=[row(4 * LANES), hrow(QK_PAD), hrow(QK_PAD), hrow(HEAD_DIM),
                  full((1, Q_RANK)), full((1, KV_RANK)),
                  full((HEADS, QK_PAD, Q_RANK)), full((HEADS, QK_PAD, KV_RANK)), full((HEADS, HEAD_DIM, KV_RANK)),
                  row(LANES), row(LANES), row(LANES), row(LANES)],
        out_specs=(row(4 * LANES), full((HEADS, Q_RANK, QK_PAD)), full((HEADS, KV_RANK, QK_PAD)),
                   full((HEADS, KV_RANK, HEAD_DIM)), full((1, Q_RANK)), full((1, KV_RANK))),
        compiler_params=_params("arbitrary"),
    )(h, dq, dk, dv, g_cq, g_ckv, wq_t, wk_t, wv_t, c_t, s1_t, s2_t, mask_t)


def _bdot(a, b, ca, cb):
    return lax.dot_general(a, b, (((ca,), (cb,)), ((0,), (0,))), preferred_element_type=F32)


def _causal_mask_t(t):
    kk = lax.broadcasted_iota(jnp.int32, (t, t), 0)
    qq = lax.broadcasted_iota(jnp.int32, (t, t), 1)
    return (qq >= kk)[None]


def _mla_attn_fwd(q, k, v_t, *, t, g, late=None):
    hds, s, _ = q.shape
    n = s // t
    n_groups = hds // g

    def body(*refs):
        if late is None:
            q_ref, k_ref, vt_ref, o_ref, lse_ref, m_sc, l_sc, acc_sc = refs
        else:
            q_ref, k_ref, vt_ref, wp_ref, o_ref, lse_ref, wout_ref, m_sc, l_sc, acc_sc, send_sems, recv_sems = refs
        hg, qi, ki = pl.program_id(0), pl.program_id(1), pl.program_id(2)
        if late is not None:
            tail = jnp.logical_and(hg == n_groups - 1, qi == n - 1)
            _gather_in_steps(wp_ref, wout_ref, send_sems, recv_sems,
                             first=jnp.logical_and(hg == 0, jnp.logical_and(qi == 0, ki == 0)),
                             mid=jnp.logical_and(tail, ki == 0), last=jnp.logical_and(tail, ki == n - 1))

        @pl.when(ki == 0)
        def _():
            m_sc[...] = jnp.full_like(m_sc, NEG)
            l_sc[...] = jnp.zeros_like(l_sc)
            acc_sc[...] = jnp.zeros_like(acc_sc)

        def step(masked):
            sc = _bdot(k_ref[...], q_ref[...], 2, 2) * MLA_SCALE
            if masked:
                sc = jnp.where(_causal_mask_t(t), sc, NEG)
            m_prev = m_sc[...]
            m_new = jnp.maximum(m_prev, jnp.max(sc, axis=1, keepdims=True))
            p = jnp.exp(sc - m_new)
            a = jnp.exp(m_prev - m_new)
            l_sc[...] = a * l_sc[...] + jnp.sum(p, axis=1, keepdims=True)
            acc_sc[...] = a * acc_sc[...] + _bdot(vt_ref[...], _mx(p), 2, 1)
            m_sc[...] = m_new

        @pl.when(ki < qi)
        def _():
            step(False)

        @pl.when(ki == qi)
        def _():
            step(True)
            o_ref[...] = acc_sc[...] / l_sc[...]
            lse_ref[...] = m_sc[...] + jnp.log(l_sc[...])

    qspec = pl.BlockSpec((g, t, QK_PAD), lambda h, i, j: (h, i, 0))
    kspec = pl.BlockSpec((g, t, QK_PAD), lambda h, i, j: (h, jnp.minimum(i, j), 0))
    vspec = pl.BlockSpec((g, HEAD_DIM, t), lambda h, i, j: (h, 0, jnp.minimum(i, j)))
    out_shape = [jax.ShapeDtypeStruct((hds, HEAD_DIM, s), F32), jax.ShapeDtypeStruct((hds, 1, s), F32)]
    in_specs = [qspec, kspec, vspec]
    out_specs = [pl.BlockSpec((g, HEAD_DIM, t), lambda h, i, j: (h, 0, i)), pl.BlockSpec((g, 1, t), lambda h, i, j: (h, 0, i))]
    scratch = [pltpu.VMEM((g, 1, t), F32), pltpu.VMEM((g, 1, t), F32), pltpu.VMEM((g, HEAD_DIM, t), F32)]
    args = [q, k, v_t]
    if late is not None:
        out_shape.append(jax.ShapeDtypeStruct((N_CHIPS,) + late.shape, late.dtype))
        in_specs.append(ANY)
        out_specs.append(ANY)
        scratch += [pltpu.SemaphoreType.DMA((6,)), pltpu.SemaphoreType.DMA((6,))]
        args.append(late)
    return pl.pallas_call(
        body, name="mla_attn_fwd",
        out_shape=tuple(out_shape), grid=(n_groups, n, n),
        in_specs=in_specs, out_specs=tuple(out_specs), scratch_shapes=scratch,
        compiler_params=pltpu.CompilerParams(dimension_semantics=("arbitrary",) * 3, vmem_limit_bytes=VMEM_LIMIT_BYTES,
                                             has_side_effects=late is not None),
    )(*args)


def _head_rowdot(a, b, *, tm):
    s, width = a.shape
    nh = width // HEAD_DIM

    def body(a_ref, b_ref, o_ref):
        prod = a_ref[...] * b_ref[...]
        for hd in range(nh):
            o_ref[:, hd:hd + 1] = jnp.sum(prod[:, hd * HEAD_DIM:(hd + 1) * HEAD_DIM], axis=-1, keepdims=True)

    return pl.pallas_call(
        body, name="head_rowdot",
        out_shape=jax.ShapeDtypeStruct((s, nh), F32),
        grid=(s // tm,),
        in_specs=[pl.BlockSpec((tm, width), lambda i: (i, 0))] * 2,
        out_specs=pl.BlockSpec((tm, nh), lambda i: (i, 0)),
        compiler_params=_params("parallel"),
    )(a, b)


def _mla_attn_bwd(q, k, v, do, lse, dd, *, t, g, early=()):
    hds, s, _ = q.shape
    n = s // t
    n_groups = hds // g
    ne = len(early)

    def body(*refs):
        q_ref, k_ref, v_ref, do_ref, lse_ref, dd_ref = refs[:6]
        ps_refs = refs[6:6 + ne]
        dq_ref, dk_ref, dv_ref = refs[6 + ne:9 + ne]
        ss_refs = refs[9 + ne:9 + 2 * ne]
        dq_sc, dk_sc, dv_sc = refs[9 + 2 * ne:12 + 2 * ne]
        hg, ki, qi = pl.program_id(0), pl.program_id(1), pl.program_id(2)
        if ne:
            send_sems, recv_sems = refs[12 + 2 * ne:]
            _exchange_in_steps(ps_refs, ss_refs, send_sems, recv_sems,
                               first=jnp.logical_and(hg == 0, jnp.logical_and(ki == 0, qi == 0)),
                               last=jnp.logical_and(hg == n_groups - 1, jnp.logical_and(ki == n - 1, qi == n - 1)))

        @pl.when(jnp.logical_and(ki == 0, qi == 0))
        def _():
            dq_sc[...] = jnp.zeros_like(dq_sc)

        @pl.when(qi == 0)
        def _():
            dk_sc[...] = jnp.zeros_like(dk_sc)
            dv_sc[...] = jnp.zeros_like(dv_sc)

        def step(masked):
            qb, kb, dob = q_ref[...], k_ref[...], do_ref[...]
            sc = _bdot(kb, qb, 2, 2) * MLA_SCALE
            if masked:
                sc = jnp.where(_causal_mask_t(t), sc, NEG)
            p = jnp.exp(sc - lse_ref[...])
            dv_sc[...] += _bdot(_mx(p), dob, 2, 1)
            dp = _bdot(v_ref[...], dob, 2, 2)
            ds = _mx(p * (dp - dd_ref[...]) * MLA_SCALE)
            dk_sc[...] += _bdot(ds, qb, 2, 1)
            dq_sc[qi] += _bdot(ds, kb, 1, 1)

        @pl.when(qi == ki)
        def _():
            step(True)

        @pl.when(qi > ki)
        def _():
            step(False)

        @pl.when(qi == n - 1)
        def _():
            dk_ref[...] = dk_sc[...]
            dv_ref[...] = dv_sc[...]

        @pl.when(jnp.logical_and(ki == n - 1, qi == n - 1))
        def _():
            for j in range(n):
                dq_ref[:, j * t:(j + 1) * t, :] = dq_sc[j]

    qs = lambda w: pl.BlockSpec((g, t, w), lambda h, j, i: (h, jnp.maximum(i, j), 0))
    ks = lambda w: pl.BlockSpec((g, t, w), lambda h, j, i: (h, j, 0))
    rowq = pl.BlockSpec((g, 1, t), lambda h, j, i: (h, 0, jnp.maximum(i, j)))
    scratch = [pltpu.VMEM((n, g, t, QK_PAD), F32), pltpu.VMEM((g, t, QK_PAD), F32), pltpu.VMEM((g, t, HEAD_DIM), F32)]
    if ne:
        scratch += [pltpu.SemaphoreType.DMA((3 * ne,)), pltpu.SemaphoreType.DMA((3 * ne,))]
    return pl.pallas_call(
        body, name="mla_attn_bwd",
        out_shape=(jax.ShapeDtypeStruct((hds, s, QK_PAD), F32), jax.ShapeDtypeStruct((hds, s, QK_PAD), F32),
                   jax.ShapeDtypeStruct((hds, s, HEAD_DIM), F32)) + tuple(jax.ShapeDtypeStruct(a.shape, a.dtype) for a in early),
        grid=(n_groups, n, n),
        in_specs=[qs(QK_PAD), ks(QK_PAD), ks(HEAD_DIM), qs(HEAD_DIM), rowq, rowq] + [ANY] * ne,
        out_specs=(pl.BlockSpec((g, s, QK_PAD), lambda h, j, i: (h, 0, 0)), ks(QK_PAD), ks(HEAD_DIM)) + (ANY,) * ne,
        scratch_shapes=scratch,
        compiler_params=pltpu.CompilerParams(dimension_semantics=("arbitrary",) * 3, vmem_limit_bytes=VMEM_LIMIT_BYTES,
                                             has_side_effects=ne > 0),
    )(q, k, v, do, lse, dd, *early)


def _perm(a, dil):
    if dil == 1:
        return a
    hds, s, e = a.shape
    return a.reshape(hds, s // dil, dil, e).transpose(0, 2, 1, 3).reshape(hds, s, e)


def _unperm(a, dil):
    if dil == 1:
        return a
    hds, s, e = a.shape
    return a.reshape(hds, dil, s // dil, e).transpose(0, 2, 1, 3).reshape(hds, s, e)


def _perm_row(a, dil):
    if dil == 1:
        return a
    hds, _, s = a.shape
    return a.reshape(hds, s // dil, dil).transpose(0, 2, 1).reshape(hds, 1, s)


def _unperm_row(a, dil):
    if dil == 1:
        return a
    hds, _, s = a.shape
    return a.reshape(hds, dil, s // dil).transpose(0, 2, 1).reshape(hds, 1, s)


def _dil_bias(dil):
    slopes = 2.0 ** (-8.0 * jnp.arange(1, HEADS + 1, dtype=F32) / HEADS)
    ik = jnp.arange(DIL_BLOCK)[:, None]
    iq = jnp.arange(DIL_BLOCK)[None, :]
    off_c = iq - ik
    off_p = iq - ik + DIL_BLOCK
    b_c = -slopes[:, None, None] * (off_c * dil).astype(F32)[None]
    b_p = -slopes[:, None, None] * (off_p * dil).astype(F32)[None]
    b_c = jnp.where((off_c >= 0)[None], b_c, NEG)
    b_p = jnp.where((off_p <= DIL_BLOCK)[None], b_p, NEG)
    return b_c, b_p


def _dil_fwd(q, k, v, dil, *, name):
    hds, s, e = q.shape
    blk = DIL_BLOCK
    nblk = s // blk
    nb = nblk // dil
    b_c, b_p = _dil_bias(dil)

    def body(q_ref, kc_ref, kp_ref, vc_ref, vp_ref, bc_ref, bp_ref, o_ref, lse_ref):
        b = pl.program_id(0)
        first = (b % nb) == 0
        qb = q_ref[...]
        s_c = _bdot(kc_ref[...], qb, 2, 2) * DIL_SCALE + bc_ref[...]
        s_p = jnp.where(first, NEG, _bdot(kp_ref[...], qb, 2, 2) * DIL_SCALE + bp_ref[...])
        m = jnp.maximum(jnp.max(s_c, axis=1, keepdims=True), jnp.max(s_p, axis=1, keepdims=True))
        p_c = jnp.exp(s_c - m)
        p_p = jnp.exp(s_p - m)
        l = jnp.sum(p_c, axis=1, keepdims=True) + jnp.sum(p_p, axis=1, keepdims=True)
        o = _bdot(_mx(p_c), vc_ref[...], 1, 1) + _bdot(_mx(p_p), vp_ref[...], 1, 1)
        o_ref[...] = o / jnp.swapaxes(l, 1, 2)
        lse_ref[...] = m + jnp.log(l)

    cur = lambda w: pl.BlockSpec((hds, blk, w), lambda b: (0, b, 0))
    prev = lambda w: pl.BlockSpec((hds, blk, w), lambda b: (0, jnp.maximum(b - 1, 0), 0))
    bias = pl.BlockSpec((hds, blk, blk), lambda b: (0, 0, 0))
    return pl.pallas_call(
        body, name=name,
        out_shape=(jax.ShapeDtypeStruct((hds, s, e), F32), jax.ShapeDtypeStruct((hds, 1, s), F32)),
        grid=(nblk,),
        in_specs=[cur(e), cur(e), prev(e), cur(e), prev(e), bias, bias],
        out_specs=(cur(e), pl.BlockSpec((hds, 1, blk), lambda b: (0, 0, b))),
        compiler_params=_params("parallel"),
    )(q, k, k, v, v, b_c, b_p)


def _dil_combine(os_, lses, *, ts):
    hds, s, e = os_[0].shape
    dils = [d for _, d in DIL_PAIRS]

    def body(o0, o1, o2, l0, l1, l2, o_ref, l_ref, sc1, sc2):
        _load_token_order(sc1, o1, dils[1], ts)
        _load_token_order(sc2, o2, dils[2], ts)
        a0, a1, a2 = l0[...], l1[...], l2[...]
        m = jnp.maximum(jnp.maximum(a0, a1), a2)
        e0, e1, e2 = jnp.exp(a0 - m), jnp.exp(a1 - m), jnp.exp(a2 - m)
        tot = e0 + e1 + e2
        col = lambda w: jnp.swapaxes(w, 1, 2)
        res = (col(e0 / tot) * o0[...] + col(e1 / tot) * sc1[...]) + col(e2 / tot) * sc2[...]
        for hd in range(hds):
            o_ref[:, hd * e:(hd + 1) * e] = res[hd]
        l_ref[...] = m + jnp.log(tot)

    _, specs = _residue_major_outs(s, ts, dils, F32)
    view = lambda a, d: a if d == 1 else a.reshape(hds, d, s // d, e)
    rspec = pl.BlockSpec((hds, 1, ts), lambda i: (0, 0, i))
    return pl.pallas_call(
        body, name="dil_combine",
        out_shape=(jax.ShapeDtypeStruct((s, hds * e), F32), jax.ShapeDtypeStruct((hds, 1, s), F32)),
        grid=(s // ts,),
        in_specs=list(specs) + [rspec] * 3,
        out_specs=(pl.BlockSpec((ts, hds * e), lambda i: (i, 0)), rspec),
        scratch_shapes=[pltpu.VMEM((hds, ts, e), F32), pltpu.VMEM((hds, ts, e), F32)],
        compiler_params=_params("parallel"),
    )(*[view(a, d) for a, d in zip(os_, dils)], *lses)


def _dil_bwd(q, k, v, do, lj, dd, dil, *, name):
    hds, s, e = q.shape
    blk = DIL_BLOCK
    nblk = s // blk
    nb = nblk // dil
    b_c, b_p = _dil_bias(dil)

    def body(q_ref, qn_ref, kc_ref, kp_ref, vc_ref, vp_ref, do_ref, don_ref, l_ref, ln_ref, d_ref, dn_ref,
             bc_ref, bp_ref, dq_ref, dk_ref, dv_ref):
        b = pl.program_id(0)
        first = (b % nb) == 0
        nxt = jnp.logical_and(b + 1 < nblk, ((b + 1) % nb) != 0)
        qb, kc, kp, vc, vp = q_ref[...], kc_ref[...], kp_ref[...], vc_ref[...], vp_ref[...]
        dob = _mx(do_ref[...])
        bc, bp = bc_ref[...], bp_ref[...]
        p_c = jnp.exp(_bdot(kc, qb, 2, 2) * DIL_SCALE + bc - l_ref[...])
        p_p = jnp.where(first, 0.0, jnp.exp(_bdot(kp, qb, 2, 2) * DIL_SCALE + bp - l_ref[...]))
        ds_c = _mx(p_c * (_bdot(vc, dob, 2, 2) - d_ref[...]) * DIL_SCALE)
        ds_p = _mx(p_p * (_bdot(vp, dob, 2, 2) - d_ref[...]) * DIL_SCALE)
        dq_ref[...] = _bdot(ds_c, kc, 1, 1) + _bdot(ds_p, kp, 1, 1)
        qn = qn_ref[...]
        donb = _mx(don_ref[...])
        p_n = jnp.where(nxt, jnp.exp(_bdot(kc, qn, 2, 2) * DIL_SCALE + bp - ln_ref[...]), 0.0)
        ds_n = _mx(p_n * (_bdot(vc, donb, 2, 2) - dn_ref[...]) * DIL_SCALE)
        dk_ref[...] = _bdot(ds_c, qb, 2, 1) + _bdot(ds_n, qn, 2, 1)
        dv_ref[...] = _bdot(_mx(p_c), dob, 2, 1) + _bdot(_mx(p_n), donb, 2, 1)

    cur = lambda w: pl.BlockSpec((hds, blk, w), lambda b: (0, b, 0))
    prev = lambda w: pl.BlockSpec((hds, blk, w), lambda b: (0, jnp.maximum(b - 1, 0), 0))
    nxt_ = lambda w: pl.BlockSpec((hds, blk, w), lambda b: (0, jnp.minimum(b + 1, nblk - 1), 0))
    rcur = pl.BlockSpec((hds, 1, blk), lambda b: (0, 0, b))
    rnxt = pl.BlockSpec((hds, 1, blk), lambda b: (0, 0, jnp.minimum(b + 1, nblk - 1)))
    bias = pl.BlockSpec((hds, blk, blk), lambda b: (0, 0, 0))
    out = jax.ShapeDtypeStruct((hds, s, e), F32)
    return pl.pallas_call(
        body, name=name,
        out_shape=(out, out, out),
        grid=(nblk,),
        in_specs=[cur(e), nxt_(e), cur(e), prev(e), cur(e), prev(e), cur(e), nxt_(e),
                  rcur, rnxt, rcur, rnxt, bias, bias],
        out_specs=(cur(e), cur(e), cur(e)),
        compiler_params=_params("parallel"),
    )(q, q, k, k, v, v, do, do, lj, lj, dd, dd, b_c, b_p)


def _add3(a, b, c, *, ts, name):
    hds, s, e = a.shape

    def body(a_ref, b_ref, c_ref, o_ref):
        o_ref[...] = (a_ref[...] + b_ref[...]) + c_ref[...]

    spec = pl.BlockSpec((hds, ts, e), lambda i: (0, i, 0))
    return pl.pallas_call(
        body, name=name,
        out_shape=jax.ShapeDtypeStruct((hds, s, e), F32),
        grid=(s // ts,),
        in_specs=[spec] * 3, out_specs=spec,
        compiler_params=_params("parallel"),
    )(a, b, c)


def _ln_fwd(z, g, b):
    mu = jnp.mean(z, axis=-1, keepdims=True)
    zc = z - mu
    var = jnp.mean(zc * zc, axis=-1, keepdims=True)
    rstd = lax.rsqrt(var + LN_EPS)
    xhat = zc * rstd
    return xhat * g + b, xhat, rstd


def _ln_bwd(dy, xhat, rstd, g):
    dxh = dy * g
    return rstd * (dxh - jnp.mean(dxh, axis=-1, keepdims=True) - xhat * jnp.mean(dxh * xhat, axis=-1, keepdims=True))


def _out_ln1(a_mla, a_dil, w_o, x, g, b, *, tm):
    s = x.shape[0]
    half = HEADS * HEAD_DIM

    def body(am_ref, ad_ref, w_ref, x_ref, g_ref, b_ref, x1_ref, xh_ref, r_ref):
        mix = _dot(_mx(am_ref[...]), w_ref[0:half, :]) + _dot(_mx(ad_ref[...]), w_ref[half:2 * half, :])
        z = DN_ALPHA * x_ref[...] + mix
        y, xhat, rstd = _ln_fwd(z, g_ref[...], b_ref[...])
        x1_ref[...] = y
        xh_ref[...] = xhat
        r_ref[...] = rstd

    row = lambda w: pl.BlockSpec((tm, w), lambda i: (i, 0))
    full = lambda shp: pl.BlockSpec(shp, lambda i: (0,) * len(shp))
    act = jax.ShapeDtypeStruct((s, D_MODEL), F32)
    return pl.pallas_call(
        body, name="out_ln1",
        out_shape=(act, act, jax.ShapeDtypeStruct((s, 1), F32)),
        grid=(s // tm,),
        in_specs=[row(half), row(half), full((D_MODEL, D_MODEL)), row(D_MODEL), full((1, D_MODEL)), full((1, D_MODEL))],
        out_specs=(row(D_MODEL), row(D_MODEL), row(1)),
        compiler_params=_params("parallel"),
    )(a_mla, a_dil, w_o, x, g, b)


def _down_ln2_loss(act, w_down, x1, g, b, target, *, tm):
    s = x1.shape[0]

    def body(a_ref, w_ref, x1_ref, g_ref, b_ref, t_ref, dz_ref, loss_ref, dg_ref, db_ref):
        i = pl.program_id(0)

        @pl.when(i == 0)
        def _():
            loss_ref[...] = jnp.zeros_like(loss_ref)
            dg_ref[...] = jnp.zeros_like(dg_ref)
            db_ref[...] = jnp.zeros_like(db_ref)

        gam = g_ref[...]
        z = DN_ALPHA * x1_ref[...] + _dot(a_ref[...], w_ref[...])
        y, xhat, rstd = _ln_fwd(z, gam, b_ref[...])
        err = y - t_ref[...]
        loss_ref[...] += 0.5 * jnp.sum(jnp.mean(err * err, axis=-1, keepdims=True))
        dy = err * (1.0 / D_MODEL)
        dg_ref[...] += jnp.sum(dy * xhat, axis=0, keepdims=True)
        db_ref[...] += jnp.sum(dy, axis=0, keepdims=True)
        dz_ref[...] = _ln_bwd(dy, xhat, rstd, gam)

    row = lambda w: pl.BlockSpec((tm, w), lambda i: (i, 0))
    full = lambda shp: pl.BlockSpec(shp, lambda i: (0,) * len(shp))
    vec = jax.ShapeDtypeStruct((1, D_MODEL), F32)
    return pl.pallas_call(
        body, name="down_ln2_loss",
        out_shape=(jax.ShapeDtypeStruct((s, D_MODEL), F32), jax.ShapeDtypeStruct((1, LANES), F32), vec, vec),
        grid=(s // tm,),
        in_specs=[row(D_FF), full((D_FF, D_MODEL)), row(D_MODEL), full((1, D_MODEL)), full((1, D_MODEL)), row(D_MODEL)],
        out_specs=(row(D_MODEL), full((1, LANES)), full((1, D_MODEL)), full((1, D_MODEL))),
        compiler_params=_params("arbitrary"),
    )(act, w_down, x1, g, b, target)


def _up_bwd_ln1(du_a, du_g, w_up_t, dz2, xhat1, rstd1, g, *, tm):
    s = dz2.shape[0]

    def body(dua_ref, dug_ref, wa_ref, wg_ref, dz2_ref, xh_ref, r_ref, g_ref, dz1_ref, dg_ref, db_ref):
        i = pl.program_id(0)

        @pl.when(i == 0)
        def _():
            dg_ref[...] = jnp.zeros_like(dg_ref)
            db_ref[...] = jnp.zeros_like(db_ref)

        dx1 = DN_ALPHA * dz2_ref[...] + (_dot(dua_ref[...], wa_ref[...]) + _dot(dug_ref[...], wg_ref[...]))
        xhat = xh_ref[...]
        dg_ref[...] += jnp.sum(dx1 * xhat, axis=0, keepdims=True)
        db_ref[...] += jnp.sum(dx1, axis=0, keepdims=True)
        dz1_ref[...] = _ln_bwd(dx1, xhat, r_ref[...], g_ref[...])

    row = lambda w: pl.BlockSpec((tm, w), lambda i: (i, 0))
    full = lambda shp: pl.BlockSpec(shp, lambda i: (0,) * len(shp))
    vec = jax.ShapeDtypeStruct((1, D_MODEL), F32)
    return pl.pallas_call(
        body, name="up_bwd_ln1",
        out_shape=(jax.ShapeDtypeStruct((s, D_MODEL), F32), vec, vec),
        grid=(s // tm,),
        in_specs=[row(D_FF), row(D_FF),
                  pl.BlockSpec((D_FF, D_MODEL), lambda i: (0, 0)), pl.BlockSpec((D_FF, D_MODEL), lambda i: (1, 0)),
                  row(D_MODEL), row(D_MODEL), row(1), full((1, D_MODEL))],
        out_specs=(row(D_MODEL), full((1, D_MODEL)), full((1, D_MODEL))),
        compiler_params=_params("arbitrary"),
    )(du_a, du_g, w_up_t, w_up_t, dz2, xhat1, rstd1, g)


GELU_C = math.sqrt(2.0 / math.pi)


def _gelu(x):
    cdf = 0.5 * (1.0 + jnp.tanh(GELU_C * (x + 0.044715 * (x * x * x))))
    return x * cdf


def _gelu_grad(x):
    t = jnp.tanh(GELU_C * (x + 0.044715 * (x * x * x)))
    return 0.5 * (1.0 + t) + 0.5 * x * (1.0 - t * t) * (GELU_C * (1.0 + 3.0 * 0.044715 * (x * x)))


def _shift_down(u, halo):
    t = u.shape[0]
    row = lax.broadcasted_iota(jnp.int32, u.shape, 0)
    h7, h6 = halo[7:8, :], halo[6:7, :]
    s1 = jnp.where(row == 0, h7, pltpu.roll(u, 1, 0))
    s2 = jnp.where(row == 0, h6, jnp.where(row == 1, h7, pltpu.roll(u, 2, 0)))
    return s1, s2


def _shift_up(d, nxt):
    t = d.shape[0]
    row = lax.broadcasted_iota(jnp.int32, d.shape, 0)
    n0, n1 = nxt[0:1, :], nxt[1:2, :]
    s1 = jnp.where(row == t - 1, n0, pltpu.roll(d, t - 1, 0))
    s2 = jnp.where(row == t - 1, n1, jnp.where(row == t - 2, n0, pltpu.roll(d, t - 2, 0)))
    return s1, s2


def _conv(u, s1, s2, w, b):
    return ((b + w[0:1, :] * s2) + w[1:2, :] * s1) + w[2:3, :] * u


def _gate_fwd(u, conv_w, conv_b, *, tm, tn):
    s = u.shape[0]
    nj = D_FF // tn
    hb = tm // SUBLANES

    def body(ua_ref, ug_ref, ha_ref, hg_ref, wa_ref, wg_ref, ba_ref, bg_ref, o_ref):
        keep = pl.program_id(0) > 0
        ua, ug = ua_ref[...], ug_ref[...]
        ha = jnp.where(keep, ha_ref[...], 0.0)
        hg = jnp.where(keep, hg_ref[...], 0.0)
        a = _conv(ua, *_shift_down(ua, ha), wa_ref[...], ba_ref[...])
        g = _conv(ug, *_shift_down(ug, hg), wg_ref[...], bg_ref[...])
        o_ref[...] = (_gelu(g) * a).astype(o_ref.dtype)

    main = lambda off: pl.BlockSpec((tm, tn), lambda i, j: (i, j + off))
    halo = lambda off: pl.BlockSpec((SUBLANES, tn), lambda i, j: (jnp.maximum(i * hb - 1, 0), j + off))
    wspec = lambda r, off: pl.BlockSpec((r, tn), lambda i, j: (0, j + off))
    return pl.pallas_call(
        body, name="gate_fwd",
        out_shape=jax.ShapeDtypeStruct((s, D_FF), MXU_DTYPE),
        grid=(s // tm, nj),
        in_specs=[main(0), main(nj), halo(0), halo(nj), wspec(3, 0), wspec(3, nj), wspec(1, 0), wspec(1, nj)],
        out_specs=pl.BlockSpec((tm, tn), lambda i, j: (i, j)),
        compiler_params=_params("parallel", "parallel"),
    )(u, u, u, u, conv_w, conv_w, conv_b, conv_b)


def _gate_bwd(u, dact, conv_w, conv_b, *, tm, tn):
    s = u.shape[0]
    nj = D_FF // tn
    ni = s // tm
    hb = tm // SUBLANES

    def body(ua_ref, ug_ref, ha_ref, hg_ref, na_ref, ng_ref, d_ref, dn_ref, wa_ref, wg_ref, ba_ref, bg_ref,
             dua_ref, dug_ref, dwa_ref, dwg_ref, dba_ref, dbg_ref):
        i = pl.program_id(1)

        @pl.when(i == 0)
        def _():
            for r in (dwa_ref, dwg_ref, dba_ref, dbg_ref):
                r[...] = jnp.zeros_like(r)

        wa, wg, ba, bg = wa_ref[...], wg_ref[...], ba_ref[...], bg_ref[...]
        ua, ug = ua_ref[...], ug_ref[...]
        ha = jnp.where(i > 0, ha_ref[...], 0.0)
        hg = jnp.where(i > 0, hg_ref[...], 0.0)
        sa1, sa2 = _shift_down(ua, ha)
        sg1, sg2 = _shift_down(ug, hg)
        a = _conv(ua, sa1, sa2, wa, ba)
        g = _conv(ug, sg1, sg2, wg, bg)
        d = d_ref[...]
        dya = d * _gelu(g)
        dyg = d * a * _gelu_grad(g)
        na, ng = na_ref[...], ng_ref[...]
        a_n = _conv(na, *_shift_down(na, ua[tm - SUBLANES:, :]), wa, ba)
        g_n = _conv(ng, *_shift_down(ng, ug[tm - SUBLANES:, :]), wg, bg)
        dn = jnp.where(i < ni - 1, dn_ref[...], 0.0)
        dya_n = dn * _gelu(g_n)
        dyg_n = dn * a_n * _gelu_grad(g_n)
        da1, da2 = _shift_up(dya, dya_n)
        dg1, dg2 = _shift_up(dyg, dyg_n)
        dua_ref[...] = (wa[2:3, :] * dya + wa[1:2, :] * da1 + wa[0:1, :] * da2).astype(dua_ref.dtype)
        dug_ref[...] = (wg[2:3, :] * dyg + wg[1:2, :] * dg1 + wg[0:1, :] * dg2).astype(dug_ref.dtype)
        ssum = lambda v: jnp.sum(v, axis=0, keepdims=True)
        dwa_ref[...] += jnp.concatenate([ssum(dya * sa2), ssum(dya * sa1), ssum(dya * ua)], axis=0)
        dwg_ref[...] += jnp.concatenate([ssum(dyg * sg2), ssum(dyg * sg1), ssum(dyg * ug)], axis=0)
        dba_ref[...] += ssum(dya)
        dbg_ref[...] += ssum(dyg)

    main = lambda off: pl.BlockSpec((tm, tn), lambda j, i: (i, j + off))
    halo = lambda off: pl.BlockSpec((SUBLANES, tn), lambda j, i: (jnp.maximum(i * hb - 1, 0), j + off))
    nxt = lambda off: pl.BlockSpec((SUBLANES, tn), lambda j, i: (jnp.minimum((i + 1) * hb, s // SUBLANES - 1), j + off))
    wspec = lambda r, off: pl.BlockSpec((r, tn), lambda j, i: (0, j + off))
    return pl.pallas_call(
        body, name="gate_bwd",
        out_shape=(jax.ShapeDtypeStruct((s, D_FF), MXU_DTYPE), jax.ShapeDtypeStruct((s, D_FF), MXU_DTYPE),
                   jax.ShapeDtypeStruct((3, D_FF), F32), jax.ShapeDtypeStruct((3, D_FF), F32),
                   jax.ShapeDtypeStruct((1, D_FF), F32), jax.ShapeDtypeStruct((1, D_FF), F32)),
        grid=(nj, ni),
        in_specs=[main(0), main(nj), halo(0), halo(nj), nxt(0), nxt(nj), main(0), nxt(0),
                  wspec(3, 0), wspec(3, nj), wspec(1, 0), wspec(1, nj)],
        out_specs=(main(0), main(0), wspec(3, 0), wspec(3, 0), wspec(1, 0), wspec(1, 0)),
        compiler_params=_params("parallel", "arbitrary"),
    )(u, u, u, u, u, u, dact, dact, conv_w, conv_w, conv_b, conv_b)


def _prep_weights(w_in, w_uq, w_uk, w_uv, w_o, w_up, w_down):
    return {**_prep_weights_first(w_in, w_uq, w_uk, w_uv), **_prep_weights_late(w_o, w_up, w_down)}


def _prep_weights_late(w_o, w_up, w_down):
    w_o, w_up, w_down = _mx(w_o), _mx(w_up), _mx(w_down)
    return dict(w_o=w_o, w_o_t=w_o.T, w_up=w_up, w_up_t=w_up.T, w_down=w_down, w_down_t=w_down.T)


def _prep_weights_first(w_in, w_uq, w_uk, w_uv):
    c = lambda a: a.astype(MXU_DTYPE)
    w_in = c(w_in)
    z = lambda w: jnp.zeros((D_MODEL, w), MXU_DTYPE)
    r0 = Q_RANK + KV_RANK
    w_in_ext = jnp.concatenate([w_in[:, :r0], z(NOPE), w_in[:, r0:r0 + ROPE], z(32), w_in[:, r0 + ROPE:]], axis=1)
    wq = jnp.pad(c(w_uq).transpose(1, 0, 2), ((0, 0), (0, 0), (0, QK_PAD - NOPE - ROPE)))
    wk = jnp.pad(c(w_uk).transpose(1, 0, 2), ((0, 0), (0, 0), (0, QK_PAD - NOPE)))
    wv = c(w_uv).transpose(1, 0, 2)
    t3 = lambda a: a.transpose(0, 2, 1)
    return dict(w_in=w_in_ext, w_in_t=w_in_ext.T, wq=wq, wq_t=t3(wq), wk=wk, wk_t=t3(wk), wv=wv, wv_t=t3(wv))


def _local_step(x, target, w, g_cq, g_ckv, ln1_g, ln1_b, conv_w, conv_b, ln2_g, ln2_b, comm=None):
    s = x.shape[0]
    tabs = _rope_tables(s)
    r2 = lambda a: a.reshape(1, -1)
    heads = lambda a: a.reshape(s, HEADS, HEAD_DIM).transpose(1, 0, 2)
    unheads = lambda a: a.transpose(1, 0, 2).reshape(s, HEADS * HEAD_DIM)
    cb = r2(conv_b)
    dils = [d for _, d in DIL_PAIRS]

    h, qp, kp, vp = _in_proj(x, w["w_in"], tm=256)
    q, k, v, v_t = _mla_prep_fwd(h, r2(g_cq), r2(g_ckv), w["wq"], w["wk"], w["wv"], w["wv_t"], tabs, tm=256)
    if comm is None:
        o_mla_t, lse_mla = _mla_attn_fwd(q, k, v_t, t=512, g=HEADS)
    else:
        o_mla_t, lse_mla, gathered = _mla_attn_fwd(q, k, v_t, t=512, g=HEADS, late=comm["late"])
        w = {**w, **comm["finish"](gathered)}
    o_bs, lse_bs = [], []
    for i, d in enumerate(dils):
        o_b, l_b = _dil_fwd(qp[i], kp[i], vp[i], d, name=f"dil_fwd_{d}")
        o_bs.append(o_b)
        lse_bs.append(_unperm_row(l_b, d))
    o_dil, lj = _dil_combine(o_bs, lse_bs, ts=512)
    o_mla = o_mla_t.transpose(2, 0, 1).reshape(s, HEADS * HEAD_DIM)
    x1, xhat1, rstd1 = _out_ln1(o_mla, o_dil, w["w_o"], x, r2(ln1_g), r2(ln1_b), tm=256)
    u = _mm_nn(x1, w["w_up"], name="up_proj", tm=512, tn=1408, tk=D_MODEL)
    act = _gate_fwd(u, conv_w, cb, tm=256, tn=1408)
    dz2, loss, dg2, db2 = _down_ln2_loss(act, w["w_down"], x1, r2(ln2_g), r2(ln2_b), target, tm=256)

    dact = _mm_nn(dz2, w["w_down_t"], name="down_bwd", tm=512, tn=1408, tk=D_MODEL)
    dw_down = _mm_tn(act, dz2, name="dw_down", tm=1408, tn=D_MODEL, ts=512)
    du_a, du_g, dcw_a, dcw_g, dcb_a, dcb_g = _gate_bwd(u, dact, conv_w, cb, tm=256, tn=1408)
    dz1, dg1, db1 = _up_bwd_ln1(du_a, du_g, w["w_up_t"], dz2, xhat1, rstd1, r2(ln1_g), tm=256)
    dw_up = jnp.concatenate([_mm_tn(x1, du_a, name="dw_up_a", tm=D_MODEL, tn=1408, ts=512),
                             _mm_tn(x1, du_g, name="dw_up_g", tm=D_MODEL, tn=1408, ts=512)], axis=1)
    do_mla, do_dil, dd_all = _attn_bwd_heads(dz1, w["w_o_t"], o_mla, o_dil, tm=256)
    dw_o = jnp.concatenate([_mm_tn(o_mla, dz1, name="dw_o_mla", tm=512, tn=D_MODEL, ts=512),
                            _mm_tn(o_dil, dz1, name="dw_o_dil", tm=512, tn=D_MODEL, ts=512)], axis=0)
    dd_all = dd_all.T
    dd_mla, dd_dil = dd_all[:HEADS].reshape(HEADS, 1, s), dd_all[HEADS:].reshape(HEADS, 1, s)
    early = () if comm is None else tuple(comm["halve"]([("w_up", dw_up), ("w_down", dw_down)]))
    dq, dk, dv, *early_slots = _mla_attn_bwd(q, k, v, do_mla, lse_mla, dd_mla, t=512, g=4, early=early)
    parts = []
    for i, d in enumerate(dils):
        parts.append(_dil_bwd(qp[i], kp[i], vp[i], do_dil[i], _perm_row(lj, d), _perm_row(dd_dil, d), d, name=f"dil_bwd_{d}"))
    dh_dil = _dil_merge(parts, ts=512)
    dh_mla, dwq, dwk, dwv, dgq, dgkv = _mla_prep_bwd(h, dq, dk, dv, r2(g_cq), r2(g_ckv),
                                                     w["wq_t"], w["wk_t"], w["wv_t"], tabs, tm=256)
    mla_w = 4 * LANES
    w_in_t = w["w_in_t"]
    grad_x = _mm_nn(dh_mla, w_in_t[:mla_w], name="in_bwd_mla", tm=512, tn=D_MODEL, tk=mla_w, add=dz1, add_scale=DN_ALPHA)
    grad_x = _mm_nn(dh_dil, w_in_t[mla_w:], name="in_bwd_dil", tm=512, tn=D_MODEL, tk=512, add=grad_x)
    dw_mla = _mm_tn(x, dh_mla, name="dw_in_mla", tm=D_MODEL, tn=mla_w, ts=512)
    dw_dil = _mm_tn(x, dh_dil, name="dw_in_dil", tm=D_MODEL, tn=512, ts=512)
    r0 = Q_RANK + KV_RANK
    grads = dict(
        w_in=jnp.concatenate([dw_mla[:, :r0], dw_mla[:, r0 + NOPE:r0 + NOPE + ROPE], dw_dil], axis=1),
        g_cq=dgq[0], g_ckv=dgkv[0],
        w_uq=dwq[:, :, :NOPE + ROPE].transpose(1, 0, 2),
        w_uk=dwk[:, :, :NOPE].transpose(1, 0, 2),
        w_uv=dwv.transpose(1, 0, 2),
        w_o=dw_o, ln1_g=dg1[0], ln1_b=db1[0], w_up=dw_up,
        conv_w=jnp.concatenate([dcw_a, dcw_g], axis=1), conv_b=jnp.concatenate([dcb_a, dcb_g], axis=1)[0],
        w_down=dw_down, ln2_g=dg2[0], ln2_b=db2[0])
    if comm is not None:
        grads["early"] = (early, tuple(early_slots))
    return loss[0, 0], grad_x, grads


N_CHIPS = 4
SHARDED = ("w_in", "w_uq", "w_o", "w_up", "conv_w", "w_down")
COL_SHARDED = ("w_in", "w_up", "conv_w")
SHARD_SHAPE = dict(w_in=(D_MODEL, IN_WIDTH // 4), w_uq=(Q_RANK // 4, HEADS, NOPE + ROPE), w_o=(D_MODEL // 4, D_MODEL),
                   w_up=(D_MODEL, 2 * D_FF // 4), conv_w=(3, 2 * D_FF // 4), w_down=(D_FF // 4, D_MODEL))
SMALL = ("g_cq", "g_ckv", "w_uk", "w_uv", "ln1_g", "ln1_b", "conv_b", "ln2_g", "ln2_b")
SMALL_SHAPE = dict(g_cq=(Q_RANK,), g_ckv=(KV_RANK,), w_uk=(KV_RANK, HEADS, NOPE), w_uv=(KV_RANK, HEADS, HEAD_DIM),
                   ln1_g=(D_MODEL,), ln1_b=(D_MODEL,), conv_b=(2 * D_FF,), ln2_g=(D_MODEL,), ln2_b=(D_MODEL,))
BIG = ("w_in", "w_uq", "w_o", "w_up", "w_down")
BIG_2D = dict(w_in=(D_MODEL, IN_WIDTH // 4), w_uq=(Q_RANK // 4, HEADS * (NOPE + ROPE)), w_o=(D_MODEL // 4, D_MODEL),
              w_up=(D_MODEL, 2 * D_FF // 4), w_down=(D_FF // 4, D_MODEL))
SMALL_G = SMALL + ("conv_w",)
SMALL_G_SHAPE = {**SMALL_SHAPE, "conv_w": (3, 2 * D_FF)}
SMALL_U_SHAPE = {**SMALL_SHAPE, "conv_w": (3, 2 * D_FF // 4)}


def _size(shape):
    return math.prod(shape)


def _padded_rows(n_elems, mult):
    return -(-n_elems // (LANES * mult)) * mult


SHARD_ROWS = {n: _padded_rows(_size(SHARD_SHAPE[n]), SUBLANES) for n in SHARDED}
R_SMALL = -(-sum(_size(SMALL_G_SHAPE[n]) for n in SMALL_G) // (LANES * LANES)) * LANES
GATHER_FIRST = ("w_in", "w_uq")
GATHER_LATE = ("w_o", "w_up", "w_down")
REDUCED_EARLY = ("w_up", "w_down")
REDUCED_LAST = ("w_in", "w_uq", "w_o")


def _rows(a, rows=None):
    flat = a.reshape(-1)
    rows = -(-flat.shape[0] // LANES) if rows is None else rows
    return jnp.pad(flat, (0, rows * LANES - flat.shape[0])).reshape(rows, LANES)


def _blocked(name, g):
    r, c = BIG_2D[name]
    a = g.reshape(r, N_CHIPS, c).transpose(1, 0, 2) if name in COL_SHARDED else g.reshape(N_CHIPS, r, c)
    return a.reshape(N_CHIPS, 2, r // 2, c)


def _pack_flat(t, names):
    return _rows(jnp.concatenate([t[n].astype(F32).reshape(-1) for n in names]), R_SMALL)


def _unpack_flat(buf, names, shapes):
    flat, out, r = buf.reshape(-1), {}, 0
    for n in names:
        out[n] = flat[r:r + _size(shapes[n])].reshape(shapes[n])
        r += _size(shapes[n])
    return out


def _from_chip_blocks(name, blocks):
    shp = SHARD_SHAPE[name]
    a = blocks.reshape(N_CHIPS, -1)[:, :_size(shp)].reshape((N_CHIPS,) + shp)
    if name in COL_SHARDED:
        return a.transpose(1, 0, 2).reshape(shp[0], N_CHIPS * shp[1])
    return a.reshape((N_CHIPS * shp[0],) + shp[1:])


ANY = pl.BlockSpec(memory_space=pl.ANY)
COMM_PARAMS = pltpu.CompilerParams(has_side_effects=True)


def _coords():
    return lax.axis_index("x"), lax.axis_index("y"), lax.axis_index("c")


def _other_chips(x, y):
    return [(1 - x, y), (x, 1 - y), (1 - x, 1 - y)]


def _remote(src, dst, send_sems, recv_sems, k, to):
    return pltpu.make_async_remote_copy(src_ref=src, dst_ref=dst, send_sem=send_sems.at[k], recv_sem=recv_sems.at[k],
                                        device_id=to, device_id_type=MESH)


def _gather_in_steps(wp_ref, wout_ref, send_sems, recv_sems, *, first, mid, last):
    x, y, c = _coords()
    me = 2 * x + y
    sib = (x, y, 1 - c)
    chips = _other_chips(x, y)
    ici = [_remote(wp_ref.at[c], wout_ref.at[me, c], send_sems, recv_sems, j, (px, py, c)) for j, (px, py) in enumerate(chips)]
    fwd = [_remote(wout_ref.at[2 * px + py, c], wout_ref.at[2 * px + py, c], send_sems, recv_sems, 3 + j, sib)
           for j, (px, py) in enumerate(chips)]

    @pl.when(first)
    def _():
        for cp in ici:
            cp.start()

    @pl.when(mid)
    def _():
        for j, (px, py) in enumerate(chips):
            _remote(wp_ref.at[c], wout_ref.at[2 * px + py, c], send_sems, recv_sems, j, (px, py, c)).wait_recv()
            fwd[j].start()

    @pl.when(last)
    def _():
        for j, (px, py) in enumerate(chips):
            k = 2 * px + py
            _remote(wout_ref.at[k, 1 - c], wout_ref.at[k, 1 - c], send_sems, recv_sems, 3 + j, sib).wait_recv()
        for cp in ici + fwd:
            cp.wait_send()


def _exchange_in_steps(ps_refs, ss_refs, send_sems, recv_sems, *, first, last):
    x, y, c = _coords()
    me = 2 * x + y
    chips = _other_chips(x, y)
    n = len(ps_refs)
    sends = [_remote(ps_refs[t].at[2 * px + py], ss_refs[t].at[me], send_sems, recv_sems, j * n + t, (px, py, c))
             for j, (px, py) in enumerate(chips) for t in range(n)]

    @pl.when(first)
    def _():
        for cp in sends:
            cp.start()

    @pl.when(last)
    def _():
        for j, (px, py) in enumerate(chips):
            for t in range(n):
                _remote(ps_refs[t].at[me], ss_refs[t].at[2 * px + py], send_sems, recv_sems, j * n + t, (px, py, c)).wait_recv()
        for cp in sends:
            cp.wait_send()


def _gather_weights(wp, cwp):
    def body(wp_ref, cw_ref, wout_ref, cwout_ref, send_sems, recv_sems):
        x, y, c = _coords()
        me = 2 * x + y
        sib = (x, y, 1 - c)
        chips = _other_chips(x, y)
        sends = [_remote(wp_ref.at[c], wout_ref.at[me, c], send_sems, recv_sems, j, (px, py, c))
                 for j, (px, py) in enumerate(chips)]
        sends += [_remote(cw_ref, cwout_ref.at[me], send_sems, recv_sems, 3 + j, (px, py, c))
                  for j, (px, py) in enumerate(chips)]
        for cp in sends:
            cp.start()
        for j, (px, py) in enumerate(chips):
            k = 2 * px + py
            _remote(wp_ref.at[c], wout_ref.at[k, c], send_sems, recv_sems, j, (px, py, c)).wait_recv()
            fwd = _remote(wout_ref.at[k, c], wout_ref.at[k, c], send_sems, recv_sems, 6 + j, sib)
            fwd.start()
            sends.append(fwd)
        for j, (px, py) in enumerate(chips):
            k = 2 * px + py
            _remote(cw_ref, cwout_ref.at[k], send_sems, recv_sems, 3 + j, (px, py, c)).wait_recv()
            _remote(wout_ref.at[k, 1 - c], wout_ref.at[k, 1 - c], send_sems, recv_sems, 6 + j, sib).wait_recv()
        for cp in sends:
            cp.wait_send()

    return pl.pallas_call(
        body, name="gather_weights",
        out_shape=(jax.ShapeDtypeStruct((N_CHIPS,) + wp.shape, wp.dtype), jax.ShapeDtypeStruct((N_CHIPS,) + cwp.shape, cwp.dtype)),
        in_specs=[ANY, ANY], out_specs=(ANY, ANY),
        scratch_shapes=[pltpu.SemaphoreType.DMA((9,)), pltpu.SemaphoreType.DMA((9,))],
        compiler_params=COMM_PARAMS,
    )(wp, cwp)


def _exchange_sibling_halves(gs, whole, *, name):
    n, nw = len(gs), len(whole)

    def body(*refs):
        gs_refs, wh_refs = refs[:n], refs[n:n + nw]
        os_refs, ow_refs = refs[n + nw:2 * n + nw], refs[2 * n + nw:2 * (n + nw)]
        send_sems, recv_sems = refs[2 * (n + nw):]
        x, y, c = _coords()
        sib = (x, y, 1 - c)
        cps = [_remote(gs_refs[t].at[k, 1 - c], os_refs[t].at[k], send_sems, recv_sems, t * N_CHIPS + k, sib)
               for t in range(n) for k in range(N_CHIPS)]
        cps += [_remote(wh_refs[t], ow_refs[t], send_sems, recv_sems, n * N_CHIPS + t, sib) for t in range(nw)]
        for cp in cps:
            cp.start()
        for cp in cps:
            cp.wait_recv()
        for cp in cps:
            cp.wait_send()

    n_sem = n * N_CHIPS + nw
    return pl.pallas_call(
        body, name=name,
        out_shape=tuple(jax.ShapeDtypeStruct((N_CHIPS,) + a.shape[2:], F32) for a in gs)
        + tuple(jax.ShapeDtypeStruct(a.shape, F32) for a in whole),
        in_specs=[ANY] * (n + nw), out_specs=(ANY,) * (n + nw),
        scratch_shapes=[pltpu.SemaphoreType.DMA((n_sem,)), pltpu.SemaphoreType.DMA((n_sem,))],
        compiler_params=COMM_PARAMS,
    )(*gs, *whole)


def _exchange_chips(ps, pr):
    n = len(ps)

    def body(*refs):
        ps_refs, pr_ref, ss_refs, sr_ref = refs[:n], refs[n], refs[n + 1:2 * n + 1], refs[2 * n + 1]
        send_sems, recv_sems = refs[2 * n + 2:]
        x, y, c = _coords()
        me = 2 * x + y
        chips = _other_chips(x, y)
        sends = []
        for j, (px, py) in enumerate(chips):
            to = (px, py, c)
            for t in range(n):
                sends.append(_remote(ps_refs[t].at[2 * px + py], ss_refs[t].at[me], send_sems, recv_sems, j * (n + 1) + t, to))
            sends.append(_remote(pr_ref, sr_ref.at[me], send_sems, recv_sems, j * (n + 1) + n, to))
        for cp in sends:
            cp.start()
        for j, (px, py) in enumerate(chips):
            k, to = 2 * px + py, (px, py, c)
            for t in range(n):
                _remote(ps_refs[t].at[me], ss_refs[t].at[k], send_sems, recv_sems, j * (n + 1) + t, to).wait_recv()
            _remote(pr_ref, sr_ref.at[k], send_sems, recv_sems, j * (n + 1) + n, to).wait_recv()
        for cp in sends:
            cp.wait_send()

    n_sem = 3 * (n + 1)
    return pl.pallas_call(
        body, name="exchange_chips",
        out_shape=tuple(jax.ShapeDtypeStruct(a.shape, a.dtype) for a in ps) + (jax.ShapeDtypeStruct((N_CHIPS,) + pr.shape, F32),),
        in_specs=[ANY] * (n + 1), out_specs=(ANY,) * (n + 1),
        scratch_shapes=[pltpu.SemaphoreType.DMA((n_sem,)), pltpu.SemaphoreType.DMA((n_sem,))],
        compiler_params=COMM_PARAMS,
    )(*ps, pr)


def _exchange_sibling_result(gh):
    n = len(gh)

    def body(*refs):
        gh_refs, out_refs, (send_sems, recv_sems) = refs[:n], refs[n:2 * n], refs[2 * n:]
        x, y, c = _coords()
        cps = [_remote(gh_refs[t], out_refs[t], send_sems, recv_sems, t, (x, y, 1 - c)) for t in range(n)]
        for cp in cps:
            cp.start()
        for cp in cps:
            cp.wait_recv()
        for cp in cps:
            cp.wait_send()

    return pl.pallas_call(
        body, name="exchange_sibling_result",
        out_shape=tuple(jax.ShapeDtypeStruct(a.shape, F32) for a in gh),
        in_specs=[ANY] * n, out_specs=(ANY,) * n,
        scratch_shapes=[pltpu.SemaphoreType.DMA((n,)), pltpu.SemaphoreType.DMA((n,))],
        compiler_params=COMM_PARAMS,
    )(*gh)


def _add_own_half(gs, recv, c_arr, *, name):
    _, rows, cols = recv.shape

    def body(c_ref, a_ref, b_ref, o_ref):
        o_ref[0] = (a_ref[0, 0] + b_ref[0]).astype(o_ref.dtype)

    return pl.pallas_call(
        body, name=name,
        out_shape=jax.ShapeDtypeStruct(recv.shape, GRAD_WIRE_DTYPE),
        grid_spec=pltpu.PrefetchScalarGridSpec(
            num_scalar_prefetch=1, grid=(N_CHIPS,),
            in_specs=[pl.BlockSpec((1, 1, rows, cols), lambda k, c_ref: (k, c_ref[0], 0, 0)),
                      pl.BlockSpec((1, rows, cols), lambda k, c_ref: (k, 0, 0))],
            out_specs=pl.BlockSpec((1, rows, cols), lambda k, c_ref: (k, 0, 0))),
        compiler_params=_params("parallel"),
    )(c_arr, gs, recv)


def _add2(a, b, *, name):
    def body(a_ref, b_ref, o_ref):
        o_ref[...] = a_ref[...] + b_ref[...]

    return pl.pallas_call(body, name=name, out_shape=jax.ShapeDtypeStruct(a.shape, F32))(a, b)


def _sum_slots(slots, *, tr, name):
    _, r, c = slots.shape

    def body(s_ref, o_ref):
        f = lambda k: s_ref[k].astype(F32)
        o_ref[...] = ((f(0) + f(1)) + f(2)) + f(3)

    return pl.pallas_call(
        body, name=name,
        out_shape=jax.ShapeDtypeStruct((r, c), F32),
        grid=(r // tr,),
        in_specs=[pl.BlockSpec((N_CHIPS, tr, c), lambda i: (0, i, 0))],
        out_specs=pl.BlockSpec((tr, c), lambda i: (i, 0)),
        compiler_params=_params("parallel"),
    )(slots)


def _adamw(w, g, m, v, *, tr, name):
    r, cols = w.shape

    def body(w_ref, g_ref, m_ref, v_ref, d_ref, nm_ref, nv_ref):
        g_ = g_ref[...]
        m_ = ADAM_B1 * m_ref[...] + (1.0 - ADAM_B1) * g_
        v_ = ADAM_B2 * v_ref[...] + (1.0 - ADAM_B2) * (g_ * g_)
        m_hat = m_ / (1.0 - ADAM_B1 ** ADAM_STEP)
        v_hat = v_ / (1.0 - ADAM_B2 ** ADAM_STEP)
        d_ref[...] = -ADAM_LR * (m_hat / (jnp.sqrt(v_hat) + ADAM_EPS) + ADAM_WD * w_ref[...])
        nm_ref[...] = m_
        nv_ref[...] = v_

    spec = pl.BlockSpec((tr, cols), lambda i: (i, 0))
    out = jax.ShapeDtypeStruct((r, cols), F32)
    return pl.pallas_call(
        body, name=name, out_shape=(out, out, out), grid=(r // tr,),
        in_specs=[spec] * 4, out_specs=(spec,) * 3,
        compiler_params=_params("parallel"),
    )(w, g, m, v)


WEIGHTS = ("w_in", "g_cq", "g_ckv", "w_uq", "w_uk", "w_uv", "w_o", "ln1_g", "ln1_b", "w_up", "conv_w", "conv_b",
           "w_down", "ln2_g", "ln2_b")


def kernel(x, w_in, g_cq, g_ckv, w_uq, w_uk, w_uv, w_o, ln1_g, ln1_b, w_up, conv_w, conv_b, w_down, ln2_g, ln2_b, loss_target, m_w_in, m_g_cq, m_g_ckv, m_w_uq, m_w_uk, m_w_uv, m_w_o, m_ln1_g, m_ln1_b, m_w_up, m_conv_w, m_conv_b, m_w_down, m_ln2_g, m_ln2_b, v_w_in, v_g_cq, v_g_ckv, v_w_uq, v_w_uk, v_w_uv, v_w_o, v_ln1_g, v_ln1_b, v_w_up, v_conv_w, v_conv_b, v_w_down, v_ln2_g, v_ln2_b):
    wts = dict(zip(WEIGHTS, (w_in, g_cq, g_ckv, w_uq, w_uk, w_uv, w_o, ln1_g, ln1_b, w_up, conv_w, conv_b, w_down, ln2_g, ln2_b)))
    mom = dict(zip(WEIGHTS, (m_w_in, m_g_cq, m_g_ckv, m_w_uq, m_w_uk, m_w_uv, m_w_o, m_ln1_g, m_ln1_b, m_w_up, m_conv_w, m_conv_b, m_w_down, m_ln2_g, m_ln2_b)))
    var = dict(zip(WEIGHTS, (v_w_in, v_g_cq, v_g_ckv, v_w_uq, v_w_uk, v_w_uv, v_w_o, v_ln1_g, v_ln1_b, v_w_up, v_conv_w, v_conv_b, v_w_down, v_ln2_g, v_ln2_b)))

    me = 2 * lax.axis_index("x") + lax.axis_index("y")
    my_c = lax.axis_index("c")
    c_arr = my_c.astype(jnp.int32).reshape(1)
    own = lambda slots, mine: lax.dynamic_update_index_in_dim(slots, mine, me, 0)

    def pack(names):
        return jnp.concatenate([_rows(_mx(wts[n]), SHARD_ROWS[n]) for n in names], axis=0).reshape(2, -1, LANES)

    def unpack(names, gathered, mine):
        buf, full, r = own(gathered, mine).reshape(N_CHIPS, -1, LANES), {}, 0
        for n in names:
            full[n] = _from_chip_blocks(n, buf[:, r:r + SHARD_ROWS[n]])
            r += SHARD_ROWS[n]
        return full

    wp_first, wp_late = pack(GATHER_FIRST), pack(GATHER_LATE)
    cwp = _rows(conv_w, SHARD_ROWS["conv_w"])
    gathered, cwfull = _gather_weights(wp_first, cwp)
    full = unpack(GATHER_FIRST, gathered, wp_first)
    conv_w_full = _from_chip_blocks("conv_w", own(cwfull, cwp))
    w = _prep_weights_first(full["w_in"], full["w_uq"], w_uk, w_uv)

    def finish(gathered_late):
        late = unpack(GATHER_LATE, gathered_late, wp_late)
        return _prep_weights_late(late["w_o"], late["w_up"], late["w_down"])

    def halve(named, whole=(), tag="early"):
        gb = [_blocked(n, a) for n, a in named]
        recv = _exchange_sibling_halves(gb, list(whole), name=f"exchange_sibling_halves_{tag}")
        ps = [_add_own_half(gb[i], recv[i], c_arr, name=f"add_half_{n}") for i, (n, _) in enumerate(named)]
        return ps + [_add2(a, recv[len(gb) + i], name=f"add_whole_{tag}_{i}") for i, a in enumerate(whole)]

    comm = dict(late=wp_late, finish=finish, halve=halve)
    loss, grad_x, g = _local_step(x[0], loss_target[0], w, g_cq, g_ckv, ln1_g, ln1_b, conv_w_full, conv_b, ln2_g, ln2_b, comm=comm)
    loss = lax.psum(loss, ("x", "y", "c"))

    ps_early, slots_early = g.pop("early")
    *ps_rest, pr = halve([(n, g[n]) for n in REDUCED_LAST], whole=[_pack_flat(g, SMALL_G)], tag="last")
    *slots_rest, slots_r = _exchange_chips(ps_rest, pr)
    ps = {**dict(zip(REDUCED_LAST, ps_rest)), **dict(zip(REDUCED_EARLY, ps_early))}
    slots = {**dict(zip(REDUCED_LAST, slots_rest)), **dict(zip(REDUCED_EARLY, slots_early))}
    slots = [own(slots[n], lax.dynamic_index_in_dim(ps[n], me, 0, keepdims=False)) for n in BIG]
    slots_r = own(slots_r, pr)
    g_half = [_sum_slots(slots[i], tr=slots[i].shape[1] // 2, name=f"sum_chips_{n}") for i, n in enumerate(BIG)]
    g_small = _unpack_flat(_sum_slots(slots_r, tr=R_SMALL, name="sum_chips_small"), SMALL_G, SMALL_G_SHAPE)
    g_other = _exchange_sibling_result(g_half)
    grads = {n: jnp.where(my_c == 0, jnp.concatenate([g_half[i], g_other[i]]), jnp.concatenate([g_other[i], g_half[i]]))
             for i, n in enumerate(BIG)}
    g_small["conv_w"] = lax.dynamic_slice_in_dim(g_small["conv_w"], me * SHARD_SHAPE["conv_w"][1], SHARD_SHAPE["conv_w"][1], 1)
    grads.update(g_small)

    res = {}
    for n in BIG:
        as2d = lambda a: a.reshape(BIG_2D[n])
        d, m, v = _adamw(as2d(wts[n]), grads[n], as2d(mom[n]), as2d(var[n]), tr=BIG_2D[n][0] // 4, name=f"adamw_{n}")
        res[n] = [a.reshape(SHARD_SHAPE[n]) for a in (grads[n], d, m, v)]
    flat = lambda t: _pack_flat(t, SMALL_G)
    dmv = _adamw(flat(wts), flat(g_small), flat(mom), flat(var), tr=R_SMALL, name="adamw_small")
    dmv = [_unpack_flat(a, SMALL_G, SMALL_U_SHAPE) for a in dmv]
    for n in SMALL_G:
        res[n] = [g_small[n]] + [t[n] for t in dmv]
    outs = [res[n][j] for j in range(4) for n in WEIGHTS]
    return (loss, grad_x[None], *outs)
```

```python
import functools
import math

import jax
import jax.numpy as jnp
from jax import lax
from jax.experimental import pallas as pl
from jax.experimental.pallas import tpu as pltpu

F32 = jnp.float32
MXU_DTYPE = jnp.bfloat16
GRAD_WIRE_DTYPE = jnp.bfloat16
NEG = -1e30

D_MODEL = 1024
HEADS = 8
HEAD_DIM = 64
Q_RANK = 256
KV_RANK = 128
NOPE = 64
ROPE = 32
QK_PAD = 128
IN_WIDTH = 1952
IN_EXT = 2048
D_FF = 2816
DIL_PAIRS = ((128, 1), (512, 4), (2048, 16))
DIL_BLOCK = 128
ROPE_THETA = 10000.0
DN_ALPHA = 2.0 ** 0.25
LN_EPS = 1e-5
RMS_EPS = 1e-6
MLA_SCALE = 1.0 / math.sqrt(NOPE + ROPE)
DIL_SCALE = 1.0 / math.sqrt(HEAD_DIM)

ADAM_LR = 0.001
ADAM_B1 = 0.9
ADAM_B2 = 0.999
ADAM_EPS = 1e-08
ADAM_WD = 0.01
ADAM_STEP = 10

LANES = 128
SUBLANES = 8
VMEM_LIMIT_BYTES = 56 * 1024 * 1024

MESH = pl.DeviceIdType.MESH


def _params(*sem):
    return pltpu.CompilerParams(dimension_semantics=sem, vmem_limit_bytes=VMEM_LIMIT_BYTES)


def _dot(a, b):
    return jnp.dot(a, b, preferred_element_type=F32)


def _dot_nt(a, b):
    return lax.dot_general(a, b, (((1,), (1,)), ((), ())), preferred_element_type=F32)


def _dot_tn(a, b):
    return lax.dot_general(a, b, (((0,), (0,)), ((), ())), preferred_element_type=F32)


def _mx(a):
    return a.astype(MXU_DTYPE)


def _mm_nn(a, b, *, name, tm, tn, tk, out_dtype=F32, add=None, add_scale=1.0):
    m, kdim = a.shape
    blocked = b.ndim == 3
    n = b.shape[0] * b.shape[2] if blocked else b.shape[1]
    nk = kdim // tk

    def body(*refs):
        if add is None:
            a_ref, b_ref, o_ref, acc = refs
        else:
            a_ref, b_ref, c_ref, o_ref, acc = refs
        k = pl.program_id(2)

        @pl.when(k == 0)
        def _():
            acc[...] = jnp.zeros_like(acc)

        acc[...] += _dot(_mx(a_ref[...]), _mx(b_ref[...]))

        @pl.when(k == nk - 1)
        def _():
            r = acc[...]
            if add is not None:
                r = r + add_scale * c_ref[...]
            o_ref[...] = r.astype(out_dtype)

    b_spec = (pl.BlockSpec((None, tk, tn), lambda i, j, k: (j, k, 0)) if blocked
              else pl.BlockSpec((tk, tn), lambda i, j, k: (k, j)))
    in_specs = [pl.BlockSpec((tm, tk), lambda i, j, k: (i, k)), b_spec]
    args = [a, b]
    if add is not None:
        in_specs.append(pl.BlockSpec((tm, tn), lambda i, j, k: (i, j)))
        args.append(add)
    return pl.pallas_call(
        body, name=name,
        out_shape=jax.ShapeDtypeStruct((m, n), out_dtype),
        grid=(m // tm, n // tn, nk),
        in_specs=in_specs,
        out_specs=pl.BlockSpec((tm, tn), lambda i, j, k: (i, j)),
        scratch_shapes=[pltpu.VMEM((tm, tn), F32)],
        compiler_params=_params("parallel", "parallel", "arbitrary"),
    )(*args)


def _mm_tn(a, b, *, name, tm, tn, ts, out_dtype=F32):
    s, m = a.shape
    n = b.shape[1]
    ns = s // ts

    def body(a_ref, b_ref, o_ref, acc):
        k = pl.program_id(2)

        @pl.when(k == 0)
        def _():
            acc[...] = jnp.zeros_like(acc)

        acc[...] += _dot_tn(_mx(a_ref[...]), _mx(b_ref[...]))

        @pl.when(k == ns - 1)
        def _():
            o_ref[...] = acc[...].astype(out_dtype)

    return pl.pallas_call(
        body, name=name,
        out_shape=jax.ShapeDtypeStruct((m, n), out_dtype),
        grid=(m // tm, n // tn, ns),
        in_specs=[pl.BlockSpec((ts, tm), lambda i, j, k: (k, i)),
                  pl.BlockSpec((ts, tn), lambda i, j, k: (k, j))],
        out_specs=pl.BlockSpec((tm, tn), lambda i, j, k: (i, j)),
        scratch_shapes=[pltpu.VMEM((tm, tn), F32)],
        compiler_params=_params("parallel", "parallel", "arbitrary"),
    )(a, b)


def _in_proj(x, w_in_ext, *, tm):
    s = x.shape[0]
    mla_w = 4 * LANES
    dil_w = HEADS * HEAD_DIM
    dils = [d for _, d in DIL_PAIRS]

    def body(x_ref, w_ref, h_ref, *rest):
        outs, sc = rest[:-1], rest[-1]
        xb = _mx(x_ref[...])
        h_ref[...] = _dot(xb, w_ref[:, 0:mla_w])
        for j in range(3):
            part = _dot(xb, w_ref[:, mla_w + j * dil_w:mla_w + (j + 1) * dil_w])
            for hd in range(HEADS):
                sc[hd] = part[:, hd * HEAD_DIM:(hd + 1) * HEAD_DIM]
            for b, d in enumerate(dils):
                _store_residue_major(outs[3 * j + b], sc, d, tm)

    shapes, specs = _residue_major_outs(s, tm, dils, MXU_DTYPE)
    res = pl.pallas_call(
        body, name="in_proj",
        out_shape=(jax.ShapeDtypeStruct((s, mla_w), F32),) + shapes * 3,
        grid=(s // tm,),
        in_specs=[pl.BlockSpec((tm, D_MODEL), lambda i: (i, 0)), pl.BlockSpec((D_MODEL, IN_EXT), lambda i: (0, 0))],
        out_specs=(pl.BlockSpec((tm, mla_w), lambda i: (i, 0)),) + specs * 3,
        scratch_shapes=[pltpu.VMEM((HEADS, tm, HEAD_DIM), F32)],
        compiler_params=_params("parallel"),
    )(x, w_in_ext)
    hm = lambda a: a.reshape(HEADS, s, HEAD_DIM)
    return res[0], [hm(a) for a in res[1:4]], [hm(a) for a in res[4:7]], [hm(a) for a in res[7:10]]


def _residue_major_outs(s, tm, dils, dtype):
    shapes, specs = [], []
    for d in dils:
        if d == 1:
            shapes.append(jax.ShapeDtypeStruct((HEADS, s, HEAD_DIM), dtype))
            specs.append(pl.BlockSpec((HEADS, tm, HEAD_DIM), lambda i: (0, i, 0)))
        else:
            shapes.append(jax.ShapeDtypeStruct((HEADS, d, s // d, HEAD_DIM), dtype))
            specs.append(pl.BlockSpec((HEADS, d, tm // d, HEAD_DIM), lambda i: (0, 0, i, 0)))
    return tuple(shapes), tuple(specs)


def _store_residue_major(o_ref, src_ref, d, tm):
    if d == 1:
        o_ref[...] = src_ref[...].astype(o_ref.dtype)
    else:
        for r in range(d):
            o_ref[:, r] = src_ref[:, pl.ds(r, tm // d, stride=d), :].astype(o_ref.dtype)


def _load_token_order(dst_ref, src_ref, d, tm, accumulate=False):
    if d == 1:
        dst_ref[...] = dst_ref[...] + src_ref[...] if accumulate else src_ref[...]
    else:
        for r in range(d):
            rows = pl.ds(r, tm // d, stride=d)
            dst_ref[:, rows, :] = dst_ref[:, rows, :] + src_ref[:, r] if accumulate else src_ref[:, r]


def _attn_bwd_heads(dz1, w_o_t, a_mla, a_dil, *, tm):
    s = dz1.shape[0]
    half = HEADS * HEAD_DIM
    dils = [d for _, d in DIL_PAIRS]

    def body(dz_ref, w_ref, am_ref, ad_ref, dom_ref, dd_ref, *dod_refs):
        dzb = _mx(dz_ref[...])
        for j, (a_ref, o_ref) in enumerate(((am_ref, dom_ref), (ad_ref, dod_refs[0]))):
            da = _dot(dzb, w_ref[:, j * half:(j + 1) * half])
            prod = da * a_ref[...]
            for hd in range(HEADS):
                sl = slice(hd * HEAD_DIM, (hd + 1) * HEAD_DIM)
                o_ref[hd] = da[:, sl].astype(o_ref.dtype)
                dd_ref[:, j * HEADS + hd:j * HEADS + hd + 1] = jnp.sum(prod[:, sl], axis=-1, keepdims=True)
        for b, d in enumerate(dils[1:]):
            _store_residue_major(dod_refs[1 + b], dod_refs[0], d, tm)

    hspec = pl.BlockSpec((HEADS, tm, HEAD_DIM), lambda i: (0, i, 0))
    row = lambda w: pl.BlockSpec((tm, w), lambda i: (i, 0))
    shapes, specs = _residue_major_outs(s, tm, dils, F32)
    do_mla, dd, *do_dil = pl.pallas_call(
        body, name="attn_bwd_heads",
        out_shape=(jax.ShapeDtypeStruct((HEADS, s, HEAD_DIM), MXU_DTYPE), jax.ShapeDtypeStruct((s, 2 * HEADS), F32)) + shapes,
        grid=(s // tm,),
        in_specs=[row(D_MODEL), pl.BlockSpec((D_MODEL, D_MODEL), lambda i: (0, 0)), row(half), row(half)],
        out_specs=(hspec, row(2 * HEADS)) + specs,
        compiler_params=_params("parallel"),
    )(dz1, w_o_t, a_mla, a_dil)
    return do_mla, [a.reshape(HEADS, s, HEAD_DIM) for a in do_dil], dd


def _dil_merge(parts, *, ts):
    hds, s, e = parts[0][0].shape
    dils = [d for _, d in DIL_PAIRS]

    def body(*refs):
        o_ref, sc = refs[9], refs[10]
        for j in range(3):
            for b, d in enumerate(dils):
                _load_token_order(sc, refs[3 * b + j], d, ts, accumulate=b > 0)
            tot = sc[...]
            for hd in range(hds):
                col = j * hds * e + hd * e
                o_ref[:, col:col + e] = tot[hd].astype(o_ref.dtype)

    _, specs = _residue_major_outs(s, ts, dils, F32)
    view = lambda a, d: a if d == 1 else a.reshape(hds, d, s // d, e)
    return pl.pallas_call(
        body, name="dil_merge",
        out_shape=jax.ShapeDtypeStruct((s, 3 * hds * e), MXU_DTYPE),
        grid=(s // ts,),
        in_specs=[specs[b] for b in range(3) for _ in range(3)],
        out_specs=pl.BlockSpec((ts, 3 * hds * e), lambda i: (i, 0)),
        scratch_shapes=[pltpu.VMEM((hds, ts, e), F32)],
        compiler_params=_params("parallel"),
    )(*[view(parts[b][j], dils[b]) for b in range(3) for j in range(3)])


def _rope_tables(s):
    half = ROPE // 2
    freqs = ROPE_THETA ** (-jnp.arange(half, dtype=F32) / half)
    ang = jnp.arange(s).astype(F32)[:, None] * freqs[None, :]
    cos, sin = jnp.cos(ang), jnp.sin(ang)
    z = lambda w: jnp.zeros((s, w), F32)
    c = jnp.concatenate([jnp.ones((s, NOPE), F32), cos, cos, z(32)], axis=1)
    s1 = jnp.concatenate([z(NOPE + half), sin, z(32)], axis=1)
    s2 = jnp.concatenate([z(NOPE), -sin, z(half + 32)], axis=1)
    mask = jnp.concatenate([z(NOPE), jnp.ones((s, ROPE), F32), z(32)], axis=1)
    return c, s1, s2, mask


def _rope(x, c, s1, s2):
    return x * c + pltpu.roll(x, 16, 1) * s1 + pltpu.roll(x, LANES - 16, 1) * s2


def _unrope(dy, c, s1, s2):
    return dy * c + pltpu.roll(dy * s1, LANES - 16, 1) + pltpu.roll(dy * s2, 16, 1)


def _rms(x):
    r = lax.rsqrt(jnp.mean(x * x, axis=-1, keepdims=True) + RMS_EPS)
    return x * r, r


def _mla_prep_fwd(h, g_cq, g_ckv, wq, wk, wv, wv_t, tabs, *, tm):
    s = h.shape[0]
    c_t, s1_t, s2_t, _ = tabs

    def body(h_ref, gq_ref, gkv_ref, wq_ref, wk_ref, wv_ref, wvt_ref, c_ref, s1_ref, s2_ref,
             q_ref, k_ref, v_ref, vt_ref):
        cq = h_ref[:, 0:Q_RANK]
        ckv = h_ref[:, Q_RANK:Q_RANK + KV_RANK]
        kr = h_ref[:, Q_RANK + KV_RANK:Q_RANK + KV_RANK + QK_PAD]
        c, s1, s2 = c_ref[...], s1_ref[...], s2_ref[...]
        cqn = _mx(_rms(cq)[0] * gq_ref[...])
        ckvn = _mx(_rms(ckv)[0] * gkv_ref[...])
        kr_rot = _rope(kr, c, s1, s2)
        for hd in range(HEADS):
            q_ref[hd] = _rope(_dot(cqn, wq_ref[hd]), c, s1, s2).astype(q_ref.dtype)
            k_ref[hd] = (_dot(ckvn, wk_ref[hd]) + kr_rot).astype(k_ref.dtype)
            v_ref[hd] = _dot(ckvn, wv_ref[hd]).astype(v_ref.dtype)
            vt_ref[hd] = _dot_nt(wvt_ref[hd], ckvn).astype(vt_ref.dtype)

    full = lambda shp: pl.BlockSpec(shp, lambda i: (0,) * len(shp))
    row = lambda w: pl.BlockSpec((tm, w), lambda i: (i, 0))
    return pl.pallas_call(
        body, name="mla_prep_fwd",
        out_shape=(jax.ShapeDtypeStruct((HEADS, s, QK_PAD), MXU_DTYPE),
                   jax.ShapeDtypeStruct((HEADS, s, QK_PAD), MXU_DTYPE),
                   jax.ShapeDtypeStruct((HEADS, s, HEAD_DIM), MXU_DTYPE),
                   jax.ShapeDtypeStruct((HEADS, HEAD_DIM, s), MXU_DTYPE)),
        grid=(s // tm,),
        in_specs=[row(4 * LANES), full((1, Q_RANK)), full((1, KV_RANK)),
                  full((HEADS, Q_RANK, QK_PAD)), full((HEADS, KV_RANK, QK_PAD)), full((HEADS, KV_RANK, HEAD_DIM)),
                  full((HEADS, HEAD_DIM, KV_RANK)), row(LANES), row(LANES), row(LANES)],
        out_specs=(pl.BlockSpec((HEADS, tm, QK_PAD), lambda i: (0, i, 0)),
                   pl.BlockSpec((HEADS, tm, QK_PAD), lambda i: (0, i, 0)),
                   pl.BlockSpec((HEADS, tm, HEAD_DIM), lambda i: (0, i, 0)),
                   pl.BlockSpec((HEADS, HEAD_DIM, tm), lambda i: (0, 0, i))),
        compiler_params=_params("parallel"),
    )(h, g_cq, g_ckv, wq, wk, wv, wv_t, c_t, s1_t, s2_t)


def _mla_prep_bwd(h, dq, dk, dv, g_cq, g_ckv, wq_t, wk_t, wv_t, tabs, *, tm):
    s = h.shape[0]
    c_t, s1_t, s2_t, mask_t = tabs

    def body(h_ref, dq_ref, dk_ref, dv_ref, gq_ref, gkv_ref, wqt_ref, wkt_ref, wvt_ref,
             c_ref, s1_ref, s2_ref, mask_ref, dh_ref, dwq_ref, dwk_ref, dwv_ref, dgq_ref, dgkv_ref):
        i = pl.program_id(0)

        @pl.when(i == 0)
        def _():
            dwq_ref[...] = jnp.zeros_like(dwq_ref)
            dwk_ref[...] = jnp.zeros_like(dwk_ref)
            dwv_ref[...] = jnp.zeros_like(dwv_ref)
            dgq_ref[...] = jnp.zeros_like(dgq_ref)
            dgkv_ref[...] = jnp.zeros_like(dgkv_ref)

        cq = h_ref[:, 0:Q_RANK]
        ckv = h_ref[:, Q_RANK:Q_RANK + KV_RANK]
        c, s1, s2 = c_ref[...], s1_ref[...], s2_ref[...]
        cqh, rq = _rms(cq)
        ckvh, rkv = _rms(ckv)
        gq, gkv = gq_ref[...], gkv_ref[...]
        cqn = _mx(cqh * gq)
        ckvn = _mx(ckvh * gkv)
        dcqn = jnp.zeros((tm, Q_RANK), F32)
        dckvn = jnp.zeros((tm, KV_RANK), F32)
        dkr = jnp.zeros((tm, QK_PAD), F32)
        for hd in range(HEADS):
            dqh = _mx(_unrope(dq_ref[hd], c, s1, s2))
            dcqn = dcqn + _dot(dqh, wqt_ref[hd])
            dwq_ref[hd] += _dot_tn(cqn, dqh)
            dkh = dk_ref[hd]
            dkr = dkr + dkh
            dkh = _mx(dkh)
            dckvn = dckvn + _dot(dkh, wkt_ref[hd])
            dwk_ref[hd] += _dot_tn(ckvn, dkh)
            dvh = _mx(dv_ref[hd])
            dckvn = dckvn + _dot(dvh, wvt_ref[hd])
            dwv_ref[hd] += _dot_tn(ckvn, dvh)
        dgq_ref[...] += jnp.sum(dcqn * cqh, axis=0, keepdims=True)
        dgkv_ref[...] += jnp.sum(dckvn * ckvh, axis=0, keepdims=True)
        gd = dcqn * gq
        dh_ref[:, 0:Q_RANK] = rq * (gd - cqh * jnp.mean(gd * cqh, axis=-1, keepdims=True))
        gd = dckvn * gkv
        dh_ref[:, Q_RANK:Q_RANK + KV_RANK] = rkv * (gd - ckvh * jnp.mean(gd * ckvh, axis=-1, keepdims=True))
        dh_ref[:, Q_RANK + KV_RANK:Q_RANK + KV_RANK + QK_PAD] = _unrope(dkr, c, s1, s2) * mask_ref[...]

    full = lambda shp: pl.BlockSpec(shp, lambda i: (0,) * len(shp))
    row = lambda w: pl.BlockSpec((tm, w), lambda i: (i, 0))
    hrow = lambda w: pl.BlockSpec((HEADS, tm, w), lambda i: (0, i, 0))
    return pl.pallas_call(
        body, name="mla_prep_bwd",
        out_shape=(jax.ShapeDtypeStruct((s, 4 * LANES), F32),
                   jax.ShapeDtypeStruct((HEADS, Q_RANK, QK_PAD), F32),
                   jax.ShapeDtypeStruct((HEADS, KV_RANK, QK_PAD), F32),
                   jax.ShapeDtypeStruct((HEADS, KV_RANK, HEAD_DIM), F32),
                   jax.ShapeDtypeStruct((1, Q_RANK), F32),
                   jax.ShapeDtypeStruct((1, KV_RANK), F32)),
        grid=(s // tm,),
        in_specs=[row(4 * LANES), hrow(QK_PAD), hrow(QK_PAD), hrow(HEAD_DIM),
                  full((1, Q_RANK)), full((1, KV_RANK)),
                  full((HEADS, QK_PAD, Q_RANK)), full((HEADS, QK_PAD, KV_RANK)), full((HEADS, HEAD_DIM, KV_RANK)),
                  row(LANES), row(LANES), row(LANES), row(LANES)],
        out_specs=(row(4 * LANES), full((HEADS, Q_RANK, QK_PAD)), full((HEADS, KV_RANK, QK_PAD)),
                   full((HEADS, KV_RANK, HEAD_DIM)), full((1, Q_RANK)), full((1, KV_RANK))),
        compiler_params=_params("arbitrary"),
    )(h, dq, dk, dv, g_cq, g_ckv, wq_t, wk_t, wv_t, c_t, s1_t, s2_t, mask_t)


def _bdot(a, b, ca, cb):
    return lax.dot_general(a, b, (((ca,), (cb,)), ((0,), (0,))), preferred_element_type=F32)


def _causal_mask_t(t):
    kk = lax.broadcasted_iota(jnp.int32, (t, t), 0)
    qq = lax.broadcasted_iota(jnp.int32, (t, t), 1)
    return (qq >= kk)[None]


def _mla_attn_fwd(q, k, v_t, *, t, g, late=None):
    hds, s, _ = q.shape
    n = s // t
    n_groups = hds // g

    nl = 0 if late is None else len(late)

    def body(*refs):
        q_ref, k_ref, vt_ref = refs[:3]
        wp_refs = refs[3:3 + nl]
        o_ref, lse_ref = refs[3 + nl:5 + nl]
        wout_refs = refs[5 + nl:5 + 2 * nl]
        m_sc, l_sc, acc_sc = refs[5 + 2 * nl:8 + 2 * nl]
        hg, qi, ki = pl.program_id(0), pl.program_id(1), pl.program_id(2)
        if nl:
            send_sems, recv_sems = refs[8 + 2 * nl:]
            tail = jnp.logical_and(hg == n_groups - 1, qi == n - 1)
            _gather_in_steps(wp_refs, wout_refs, send_sems, recv_sems,
                             first=jnp.logical_and(hg == 0, jnp.logical_and(qi == 0, ki == 0)),
                             mid=jnp.logical_and(tail, ki == 0), last=jnp.logical_and(tail, ki == n - 1))

        @pl.when(ki == 0)
        def _():
            m_sc[...] = jnp.full_like(m_sc, NEG)
            l_sc[...] = jnp.zeros_like(l_sc)
            acc_sc[...] = jnp.zeros_like(acc_sc)

        def step(masked):
            sc = _bdot(k_ref[...], q_ref[...], 2, 2) * MLA_SCALE
            if masked:
                sc = jnp.where(_causal_mask_t(t), sc, NEG)
            m_prev = m_sc[...]
            m_new = jnp.maximum(m_prev, jnp.max(sc, axis=1, keepdims=True))
            p = jnp.exp(sc - m_new)
            a = jnp.exp(m_prev - m_new)
            l_sc[...] = a * l_sc[...] + jnp.sum(p, axis=1, keepdims=True)
            acc_sc[...] = a * acc_sc[...] + _bdot(vt_ref[...], _mx(p), 2, 1)
            m_sc[...] = m_new

        @pl.when(ki < qi)
        def _():
            step(False)

        @pl.when(ki == qi)
        def _():
            step(True)
            o_ref[...] = acc_sc[...] / l_sc[...]
            lse_ref[...] = m_sc[...] + jnp.log(l_sc[...])

    qspec = pl.BlockSpec((g, t, QK_PAD), lambda h, i, j: (h, i, 0))
    kspec = pl.BlockSpec((g, t, QK_PAD), lambda h, i, j: (h, jnp.minimum(i, j), 0))
    vspec = pl.BlockSpec((g, HEAD_DIM, t), lambda h, i, j: (h, 0, jnp.minimum(i, j)))
    out_shape = [jax.ShapeDtypeStruct((hds, HEAD_DIM, s), F32), jax.ShapeDtypeStruct((hds, 1, s), F32)]
    in_specs = [qspec, kspec, vspec]
    out_specs = [pl.BlockSpec((g, HEAD_DIM, t), lambda h, i, j: (h, 0, i)), pl.BlockSpec((g, 1, t), lambda h, i, j: (h, 0, i))]
    scratch = [pltpu.VMEM((g, 1, t), F32), pltpu.VMEM((g, 1, t), F32), pltpu.VMEM((g, HEAD_DIM, t), F32)]
    args = [q, k, v_t]
    if nl:
        out_shape += [jax.ShapeDtypeStruct((N_CHIPS,) + a.shape, a.dtype) for a in late]
        in_specs += [ANY] * nl
        out_specs += [ANY] * nl
        scratch += [pltpu.SemaphoreType.DMA((6 * nl,)), pltpu.SemaphoreType.DMA((6 * nl,))]
        args += list(late)
    return pl.pallas_call(
        body, name="mla_attn_fwd",
        out_shape=tuple(out_shape), grid=(n_groups, n, n),
        in_specs=in_specs, out_specs=tuple(out_specs), scratch_shapes=scratch,
        compiler_params=pltpu.CompilerParams(dimension_semantics=("arbitrary",) * 3, vmem_limit_bytes=VMEM_LIMIT_BYTES,
                                             has_side_effects=nl > 0),
    )(*args)


def _head_rowdot(a, b, *, tm):
    s, width = a.shape
    nh = width // HEAD_DIM

    def body(a_ref, b_ref, o_ref):
        prod = a_ref[...] * b_ref[...]
        for hd in range(nh):
            o_ref[:, hd:hd + 1] = jnp.sum(prod[:, hd * HEAD_DIM:(hd + 1) * HEAD_DIM], axis=-1, keepdims=True)

    return pl.pallas_call(
        body, name="head_rowdot",
        out_shape=jax.ShapeDtypeStruct((s, nh), F32),
        grid=(s // tm,),
        in_specs=[pl.BlockSpec((tm, width), lambda i: (i, 0))] * 2,
        out_specs=pl.BlockSpec((tm, nh), lambda i: (i, 0)),
        compiler_params=_params("parallel"),
    )(a, b)


def _mla_attn_bwd(q, k, v, do, lse, dd, *, t, g, early=()):
    hds, s, _ = q.shape
    n = s // t
    n_groups = hds // g
    ne = len(early)

    def body(*refs):
        q_ref, k_ref, v_ref, do_ref, lse_ref, dd_ref = refs[:6]
        ps_refs = refs[6:6 + ne]
        dq_ref, dk_ref, dv_ref = refs[6 + ne:9 + ne]
        ss_refs = refs[9 + ne:9 + 2 * ne]
        dq_sc, dk_sc, dv_sc = refs[9 + 2 * ne:12 + 2 * ne]
        hg, ki, qi = pl.program_id(0), pl.program_id(1), pl.program_id(2)
        if ne:
            send_sems, recv_sems = refs[12 + 2 * ne:]
            _exchange_in_steps(ps_refs, ss_refs, send_sems, recv_sems,
                               first=jnp.logical_and(hg == 0, jnp.logical_and(ki == 0, qi == 0)),
                               last=jnp.logical_and(hg == n_groups - 1, jnp.logical_and(ki == n - 1, qi == n - 1)))

        @pl.when(jnp.logical_and(ki == 0, qi == 0))
        def _():
            dq_sc[...] = jnp.zeros_like(dq_sc)

        @pl.when(qi == 0)
        def _():
            dk_sc[...] = jnp.zeros_like(dk_sc)
            dv_sc[...] = jnp.zeros_like(dv_sc)

        def step(masked):
            qb, kb, dob = q_ref[...], k_ref[...], do_ref[...]
            sc = _bdot(kb, qb, 2, 2) * MLA_SCALE
            if masked:
                sc = jnp.where(_causal_mask_t(t), sc, NEG)
            p = jnp.exp(sc - lse_ref[...])
            dv_sc[...] += _bdot(_mx(p), dob, 2, 1)
            dp = _bdot(v_ref[...], dob, 2, 2)
            ds = _mx(p * (dp - dd_ref[...]) * MLA_SCALE)
            dk_sc[...] += _bdot(ds, qb, 2, 1)
            dq_sc[qi] += _bdot(ds, kb, 1, 1)

        @pl.when(qi == ki)
        def _():
            step(True)

        @pl.when(qi > ki)
        def _():
            step(False)

        @pl.when(qi == n - 1)
        def _():
            dk_ref[...] = dk_sc[...]
            dv_ref[...] = dv_sc[...]

        @pl.when(jnp.logical_and(ki == n - 1, qi == n - 1))
        def _():
            for j in range(n):
                dq_ref[:, j * t:(j + 1) * t, :] = dq_sc[j]

    qs = lambda w: pl.BlockSpec((g, t, w), lambda h, j, i: (h, jnp.maximum(i, j), 0))
    ks = lambda w: pl.BlockSpec((g, t, w), lambda h, j, i: (h, j, 0))
    rowq = pl.BlockSpec((g, 1, t), lambda h, j, i: (h, 0, jnp.maximum(i, j)))
    scratch = [pltpu.VMEM((n, g, t, QK_PAD), F32), pltpu.VMEM((g, t, QK_PAD), F32), pltpu.VMEM((g, t, HEAD_DIM), F32)]
    if ne:
        scratch += [pltpu.SemaphoreType.DMA((3 * ne,)), pltpu.SemaphoreType.DMA((3 * ne,))]
    return pl.pallas_call(
        body, name="mla_attn_bwd",
        out_shape=(jax.ShapeDtypeStruct((hds, s, QK_PAD), F32), jax.ShapeDtypeStruct((hds, s, QK_PAD), F32),
                   jax.ShapeDtypeStruct((hds, s, HEAD_DIM), F32)) + tuple(jax.ShapeDtypeStruct(a.shape, a.dtype) for a in early),
        grid=(n_groups, n, n),
        in_specs=[qs(QK_PAD), ks(QK_PAD), ks(HEAD_DIM), qs(HEAD_DIM), rowq, rowq] + [ANY] * ne,
        out_specs=(pl.BlockSpec((g, s, QK_PAD), lambda h, j, i: (h, 0, 0)), ks(QK_PAD), ks(HEAD_DIM)) + (ANY,) * ne,
        scratch_shapes=scratch,
        compiler_params=pltpu.CompilerParams(dimension_semantics=("arbitrary",) * 3, vmem_limit_bytes=VMEM_LIMIT_BYTES,
                                             has_side_effects=ne > 0),
    )(q, k, v, do, lse, dd, *early)


def _perm(a, dil):
    if dil == 1:
        return a
    hds, s, e = a.shape
    return a.reshape(hds, s // dil, dil, e).transpose(0, 2, 1, 3).reshape(hds, s, e)


def _unperm(a, dil):
    if dil == 1:
        return a
    hds, s, e = a.shape
    return a.reshape(hds, dil, s // dil, e).transpose(0, 2, 1, 3).reshape(hds, s, e)


def _perm_row(a, dil):
    if dil == 1:
        return a
    hds, _, s = a.shape
    return a.reshape(hds, s // dil, dil).transpose(0, 2, 1).reshape(hds, 1, s)


def _unperm_row(a, dil):
    if dil == 1:
        return a
    hds, _, s = a.shape
    return a.reshape(hds, dil, s // dil).transpose(0, 2, 1).reshape(hds, 1, s)


def _dil_bias(dil):
    slopes = 2.0 ** (-8.0 * jnp.arange(1, HEADS + 1, dtype=F32) / HEADS)
    ik = jnp.arange(DIL_BLOCK)[:, None]
    iq = jnp.arange(DIL_BLOCK)[None, :]
    off_c = iq - ik
    off_p = iq - ik + DIL_BLOCK
    b_c = -slopes[:, None, None] * (off_c * dil).astype(F32)[None]
    b_p = -slopes[:, None, None] * (off_p * dil).astype(F32)[None]
    b_c = jnp.where((off_c >= 0)[None], b_c, NEG)
    b_p = jnp.where((off_p <= DIL_BLOCK)[None], b_p, NEG)
    return b_c, b_p


def _dil_fwd(q, k, v, dil, *, name):
    hds, s, e = q.shape
    blk = DIL_BLOCK
    nblk = s // blk
    nb = nblk // dil
    b_c, b_p = _dil_bias(dil)

    def body(q_ref, kc_ref, kp_ref, vc_ref, vp_ref, bc_ref, bp_ref, o_ref, lse_ref):
        b = pl.program_id(0)
        first = (b % nb) == 0
        qb = q_ref[...]
        s_c = _bdot(kc_ref[...], qb, 2, 2) * DIL_SCALE + bc_ref[...]
        s_p = jnp.where(first, NEG, _bdot(kp_ref[...], qb, 2, 2) * DIL_SCALE + bp_ref[...])
        m = jnp.maximum(jnp.max(s_c, axis=1, keepdims=True), jnp.max(s_p, axis=1, keepdims=True))
        p_c = jnp.exp(s_c - m)
        p_p = jnp.exp(s_p - m)
        l = jnp.sum(p_c, axis=1, keepdims=True) + jnp.sum(p_p, axis=1, keepdims=True)
        o = _bdot(_mx(p_c), vc_ref[...], 1, 1) + _bdot(_mx(p_p), vp_ref[...], 1, 1)
        o_ref[...] = o / jnp.swapaxes(l, 1, 2)
        lse_ref[...] = m + jnp.log(l)

    cur = lambda w: pl.BlockSpec((hds, blk, w), lambda b: (0, b, 0))
    prev = lambda w: pl.BlockSpec((hds, blk, w), lambda b: (0, jnp.maximum(b - 1, 0), 0))
    bias = pl.BlockSpec((hds, blk, blk), lambda b: (0, 0, 0))
    return pl.pallas_call(
        body, name=name,
        out_shape=(jax.ShapeDtypeStruct((hds, s, e), F32), jax.ShapeDtypeStruct((hds, 1, s), F32)),
        grid=(nblk,),
        in_specs=[cur(e), cur(e), prev(e), cur(e), prev(e), bias, bias],
        out_specs=(cur(e), pl.BlockSpec((hds, 1, blk), lambda b: (0, 0, b))),
        compiler_params=_params("parallel"),
    )(q, k, k, v, v, b_c, b_p)


def _dil_combine(os_, lses, *, ts):
    hds, s, e = os_[0].shape
    dils = [d for _, d in DIL_PAIRS]

    def body(o0, o1, o2, l0, l1, l2, o_ref, l_ref, sc1, sc2):
        _load_token_order(sc1, o1, dils[1], ts)
        _load_token_order(sc2, o2, dils[2], ts)
        a0, a1, a2 = l0[...], l1[...], l2[...]
        m = jnp.maximum(jnp.maximum(a0, a1), a2)
        e0, e1, e2 = jnp.exp(a0 - m), jnp.exp(a1 - m), jnp.exp(a2 - m)
        tot = e0 + e1 + e2
        col = lambda w: jnp.swapaxes(w, 1, 2)
        res = (col(e0 / tot) * o0[...] + col(e1 / tot) * sc1[...]) + col(e2 / tot) * sc2[...]
        for hd in range(hds):
            o_ref[:, hd * e:(hd + 1) * e] = res[hd]
        l_ref[...] = m + jnp.log(tot)

    _, specs = _residue_major_outs(s, ts, dils, F32)
    view = lambda a, d: a if d == 1 else a.reshape(hds, d, s // d, e)
    rspec = pl.BlockSpec((hds, 1, ts), lambda i: (0, 0, i))
    return pl.pallas_call(
        body, name="dil_combine",
        out_shape=(jax.ShapeDtypeStruct((s, hds * e), F32), jax.ShapeDtypeStruct((hds, 1, s), F32)),
        grid=(s // ts,),
        in_specs=list(specs) + [rspec] * 3,
        out_specs=(pl.BlockSpec((ts, hds * e), lambda i: (i, 0)), rspec),
        scratch_shapes=[pltpu.VMEM((hds, ts, e), F32), pltpu.VMEM((hds, ts, e), F32)],
        compiler_params=_params("parallel"),
    )(*[view(a, d) for a, d in zip(os_, dils)], *lses)


def _dil_bwd(q, k, v, do, lj, dd, dil, *, name):
    hds, s, e = q.shape
    blk = DIL_BLOCK
    nblk = s // blk
    nb = nblk // dil
    b_c, b_p = _dil_bias(dil)

    def body(q_ref, qn_ref, kc_ref, kp_ref, vc_ref, vp_ref, do_ref, don_ref, l_ref, ln_ref, d_ref, dn_ref,
             bc_ref, bp_ref, dq_ref, dk_ref, dv_ref):
        b = pl.program_id(0)
        first = (b % nb) == 0
        nxt = jnp.logical_and(b + 1 < nblk, ((b + 1) % nb) != 0)
        qb, kc, kp, vc, vp = q_ref[...], kc_ref[...], kp_ref[...], vc_ref[...], vp_ref[...]
        dob = _mx(do_ref[...])
        bc, bp = bc_ref[...], bp_ref[...]
        p_c = jnp.exp(_bdot(kc, qb, 2, 2) * DIL_SCALE + bc - l_ref[...])
        p_p = jnp.where(first, 0.0, jnp.exp(_bdot(kp, qb, 2, 2) * DIL_SCALE + bp - l_ref[...]))
        ds_c = _mx(p_c * (_bdot(vc, dob, 2, 2) - d_ref[...]) * DIL_SCALE)
        ds_p = _mx(p_p * (_bdot(vp, dob, 2, 2) - d_ref[...]) * DIL_SCALE)
        dq_ref[...] = _bdot(ds_c, kc, 1, 1) + _bdot(ds_p, kp, 1, 1)
        qn = qn_ref[...]
        donb = _mx(don_ref[...])
        p_n = jnp.where(nxt, jnp.exp(_bdot(kc, qn, 2, 2) * DIL_SCALE + bp - ln_ref[...]), 0.0)
        ds_n = _mx(p_n * (_bdot(vc, donb, 2, 2) - dn_ref[...]) * DIL_SCALE)
        dk_ref[...] = _bdot(ds_c, qb, 2, 1) + _bdot(ds_n, qn, 2, 1)
        dv_ref[...] = _bdot(_mx(p_c), dob, 2, 1) + _bdot(_mx(p_n), donb, 2, 1)

    cur = lambda w: pl.BlockSpec((hds, blk, w), lambda b: (0, b, 0))
    prev = lambda w: pl.BlockSpec((hds, blk, w), lambda b: (0, jnp.maximum(b - 1, 0), 0))
    nxt_ = lambda w: pl.BlockSpec((hds, blk, w), lambda b: (0, jnp.minimum(b + 1, nblk - 1), 0))
    rcur = pl.BlockSpec((hds, 1, blk), lambda b: (0, 0, b))
    rnxt = pl.BlockSpec((hds, 1, blk), lambda b: (0, 0, jnp.minimum(b + 1, nblk - 1)))
    bias = pl.BlockSpec((hds, blk, blk), lambda b: (0, 0, 0))
    out = jax.ShapeDtypeStruct((hds, s, e), F32)
    return pl.pallas_call(
        body, name=name,
        out_shape=(out, out, out),
        grid=(nblk,),
        in_specs=[cur(e), nxt_(e), cur(e), prev(e), cur(e), prev(e), cur(e), nxt_(e),
                  rcur, rnxt, rcur, rnxt, bias, bias],
        out_specs=(cur(e), cur(e), cur(e)),
        compiler_params=_params("parallel"),
    )(q, q, k, k, v, v, do, do, lj, lj, dd, dd, b_c, b_p)


def _add3(a, b, c, *, ts, name):
    hds, s, e = a.shape

    def body(a_ref, b_ref, c_ref, o_ref):
        o_ref[...] = (a_ref[...] + b_ref[...]) + c_ref[...]

    spec = pl.BlockSpec((hds, ts, e), lambda i: (0, i, 0))
    return pl.pallas_call(
        body, name=name,
        out_shape=jax.ShapeDtypeStruct((hds, s, e), F32),
        grid=(s // ts,),
        in_specs=[spec] * 3, out_specs=spec,
        compiler_params=_params("parallel"),
    )(a, b, c)


def _ln_fwd(z, g, b):
    mu = jnp.mean(z, axis=-1, keepdims=True)
    zc = z - mu
    var = jnp.mean(zc * zc, axis=-1, keepdims=True)
    rstd = lax.rsqrt(var + LN_EPS)
    xhat = zc * rstd
    return xhat * g + b, xhat, rstd


def _ln_bwd(dy, xhat, rstd, g):
    dxh = dy * g
    return rstd * (dxh - jnp.mean(dxh, axis=-1, keepdims=True) - xhat * jnp.mean(dxh * xhat, axis=-1, keepdims=True))


def _out_ln1(a_mla, a_dil, w_o, x, g, b, *, tm):
    s = x.shape[0]
    half = HEADS * HEAD_DIM

    def body(am_ref, ad_ref, w_ref, x_ref, g_ref, b_ref, x1_ref, xh_ref, r_ref):
        mix = _dot(_mx(am_ref[...]), w_ref[0:half, :]) + _dot(_mx(ad_ref[...]), w_ref[half:2 * half, :])
        z = DN_ALPHA * x_ref[...] + mix
        y, xhat, rstd = _ln_fwd(z, g_ref[...], b_ref[...])
        x1_ref[...] = y
        xh_ref[...] = xhat
        r_ref[...] = rstd

    row = lambda w: pl.BlockSpec((tm, w), lambda i: (i, 0))
    full = lambda shp: pl.BlockSpec(shp, lambda i: (0,) * len(shp))
    act = jax.ShapeDtypeStruct((s, D_MODEL), F32)
    return pl.pallas_call(
        body, name="out_ln1",
        out_shape=(act, act, jax.ShapeDtypeStruct((s, 1), F32)),
        grid=(s // tm,),
        in_specs=[row(half), row(half), full((D_MODEL, D_MODEL)), row(D_MODEL), full((1, D_MODEL)), full((1, D_MODEL))],
        out_specs=(row(D_MODEL), row(D_MODEL), row(1)),
        compiler_params=_params("parallel"),
    )(a_mla, a_dil, w_o, x, g, b)


def _down_ln2_loss(act, w_down, x1, g, b, target, *, tm):
    s = x1.shape[0]

    def body(a_ref, w_ref, x1_ref, g_ref, b_ref, t_ref, dz_ref, loss_ref, dg_ref, db_ref):
        i = pl.program_id(0)

        @pl.when(i == 0)
        def _():
            loss_ref[...] = jnp.zeros_like(loss_ref)
            dg_ref[...] = jnp.zeros_like(dg_ref)
            db_ref[...] = jnp.zeros_like(db_ref)

        gam = g_ref[...]
        z = DN_ALPHA * x1_ref[...] + _dot(a_ref[...], w_ref[...])
        y, xhat, rstd = _ln_fwd(z, gam, b_ref[...])
        err = y - t_ref[...]
        loss_ref[...] += 0.5 * jnp.sum(jnp.mean(err * err, axis=-1, keepdims=True))
        dy = err * (1.0 / D_MODEL)
        dg_ref[...] += jnp.sum(dy * xhat, axis=0, keepdims=True)
        db_ref[...] += jnp.sum(dy, axis=0, keepdims=True)
        dz_ref[...] = _ln_bwd(dy, xhat, rstd, gam)

    row = lambda w: pl.BlockSpec((tm, w), lambda i: (i, 0))
    full = lambda shp: pl.BlockSpec(shp, lambda i: (0,) * len(shp))
    vec = jax.ShapeDtypeStruct((1, D_MODEL), F32)
    return pl.pallas_call(
        body, name="down_ln2_loss",
        out_shape=(jax.ShapeDtypeStruct((s, D_MODEL), F32), jax.ShapeDtypeStruct((1, LANES), F32), vec, vec),
        grid=(s // tm,),
        in_specs=[row(D_FF), full((D_FF, D_MODEL)), row(D_MODEL), full((1, D_MODEL)), full((1, D_MODEL)), row(D_MODEL)],
        out_specs=(row(D_MODEL), full((1, LANES)), full((1, D_MODEL)), full((1, D_MODEL))),
        compiler_params=_params("arbitrary"),
    )(act, w_down, x1, g, b, target)


def _up_bwd_ln1(du_a, du_g, w_up_t, dz2, xhat1, rstd1, g, *, tm):
    s = dz2.shape[0]

    def body(dua_ref, dug_ref, wa_ref, wg_ref, dz2_ref, xh_ref, r_ref, g_ref, dz1_ref, dg_ref, db_ref):
        i = pl.program_id(0)

        @pl.when(i == 0)
        def _():
            dg_ref[...] = jnp.zeros_like(dg_ref)
            db_ref[...] = jnp.zeros_like(db_ref)

        dx1 = DN_ALPHA * dz2_ref[...] + (_dot(dua_ref[...], wa_ref[...]) + _dot(dug_ref[...], wg_ref[...]))
        xhat = xh_ref[...]
        dg_ref[...] += jnp.sum(dx1 * xhat, axis=0, keepdims=True)
        db_ref[...] += jnp.sum(dx1, axis=0, keepdims=True)
        dz1_ref[...] = _ln_bwd(dx1, xhat, r_ref[...], g_ref[...])

    row = lambda w: pl.BlockSpec((tm, w), lambda i: (i, 0))
    full = lambda shp: pl.BlockSpec(shp, lambda i: (0,) * len(shp))
    vec = jax.ShapeDtypeStruct((1, D_MODEL), F32)
    return pl.pallas_call(
        body, name="up_bwd_ln1",
        out_shape=(jax.ShapeDtypeStruct((s, D_MODEL), F32), vec, vec),
        grid=(s // tm,),
        in_specs=[row(D_FF), row(D_FF),
                  pl.BlockSpec((D_FF, D_MODEL), lambda i: (0, 0)), pl.BlockSpec((D_FF, D_MODEL), lambda i: (1, 0)),
                  row(D_MODEL), row(D_MODEL), row(1), full((1, D_MODEL))],
        out_specs=(row(D_MODEL), full((1, D_MODEL)), full((1, D_MODEL))),
        compiler_params=_params("arbitrary"),
    )(du_a, du_g, w_up_t, w_up_t, dz2, xhat1, rstd1, g)


GELU_C = math.sqrt(2.0 / math.pi)


def _gelu(x):
    cdf = 0.5 * (1.0 + jnp.tanh(GELU_C * (x + 0.044715 * (x * x * x))))
    return x * cdf


def _gelu_grad(x):
    t = jnp.tanh(GELU_C * (x + 0.044715 * (x * x * x)))
    return 0.5 * (1.0 + t) + 0.5 * x * (1.0 - t * t) * (GELU_C * (1.0 + 3.0 * 0.044715 * (x * x)))


def _shift_down(u, halo):
    t = u.shape[0]
    row = lax.broadcasted_iota(jnp.int32, u.shape, 0)
    h7, h6 = halo[7:8, :], halo[6:7, :]
    s1 = jnp.where(row == 0, h7, pltpu.roll(u, 1, 0))
    s2 = jnp.where(row == 0, h6, jnp.where(row == 1, h7, pltpu.roll(u, 2, 0)))
    return s1, s2


def _shift_up(d, nxt):
    t = d.shape[0]
    row = lax.broadcasted_iota(jnp.int32, d.shape, 0)
    n0, n1 = nxt[0:1, :], nxt[1:2, :]
    s1 = jnp.where(row == t - 1, n0, pltpu.roll(d, t - 1, 0))
    s2 = jnp.where(row == t - 1, n1, jnp.where(row == t - 2, n0, pltpu.roll(d, t - 2, 0)))
    return s1, s2


def _conv(u, s1, s2, w, b):
    return ((b + w[0:1, :] * s2) + w[1:2, :] * s1) + w[2:3, :] * u


def _gate_fwd(u, conv_w, conv_b, *, tm, tn):
    s = u.shape[0]
    nj = D_FF // tn
    hb = tm // SUBLANES

    def body(ua_ref, ug_ref, ha_ref, hg_ref, wa_ref, wg_ref, ba_ref, bg_ref, o_ref, a_ref, ge_ref, gd_ref):
        keep = pl.program_id(0) > 0
        ua, ug = ua_ref[...], ug_ref[...]
        ha = jnp.where(keep, ha_ref[...], 0.0)
        hg = jnp.where(keep, hg_ref[...], 0.0)
        a = _conv(ua, *_shift_down(ua, ha), wa_ref[...], ba_ref[...])
        g = _conv(ug, *_shift_down(ug, hg), wg_ref[...], bg_ref[...])
        ge = _gelu(g)
        o_ref[...] = (ge * a).astype(o_ref.dtype)
        a_ref[...] = a
        ge_ref[...] = ge
        gd_ref[...] = _gelu_grad(g)

    main = lambda off: pl.BlockSpec((tm, tn), lambda i, j: (i, j + off))
    halo = lambda off: pl.BlockSpec((SUBLANES, tn), lambda i, j: (jnp.maximum(i * hb - 1, 0), j + off))
    wspec = lambda r, off: pl.BlockSpec((r, tn), lambda i, j: (0, j + off))
    keep_f32 = jax.ShapeDtypeStruct((s, D_FF), F32)
    return pl.pallas_call(
        body, name="gate_fwd",
        out_shape=(jax.ShapeDtypeStruct((s, D_FF), MXU_DTYPE), keep_f32, keep_f32, keep_f32),
        grid=(s // tm, nj),
        in_specs=[main(0), main(nj), halo(0), halo(nj), wspec(3, 0), wspec(3, nj), wspec(1, 0), wspec(1, nj)],
        out_specs=(main(0),) * 4,
        compiler_params=_params("parallel", "parallel"),
    )(u, u, u, u, conv_w, conv_w, conv_b, conv_b)


def _gate_bwd(u, dact, a, ge, gd, conv_w, *, tm, tn):
    s = u.shape[0]
    nj = D_FF // tn
    ni = s // tm
    hb = tm // SUBLANES

    def body(ua_ref, ug_ref, ha_ref, hg_ref, d_ref, dn_ref, a_ref, an_ref, ge_ref, gen_ref, gd_ref, gdn_ref,
             wa_ref, wg_ref, dua_ref, dug_ref, dwa_ref, dwg_ref, dba_ref, dbg_ref):
        i = pl.program_id(1)

        @pl.when(i == 0)
        def _():
            for r in (dwa_ref, dwg_ref, dba_ref, dbg_ref):
                r[...] = jnp.zeros_like(r)

        wa, wg = wa_ref[...], wg_ref[...]
        ua, ug = ua_ref[...], ug_ref[...]
        ha = jnp.where(i > 0, ha_ref[...], 0.0)
        hg = jnp.where(i > 0, hg_ref[...], 0.0)
        sa1, sa2 = _shift_down(ua, ha)
        sg1, sg2 = _shift_down(ug, hg)
        d = d_ref[...]
        dya = d * ge_ref[...]
        dyg = d * a_ref[...] * gd_ref[...]
        dn = jnp.where(i < ni - 1, dn_ref[...], 0.0)
        dya_n = dn * gen_ref[...]
        dyg_n = dn * an_ref[...] * gdn_ref[...]
        da1, da2 = _shift_up(dya, dya_n)
        dg1, dg2 = _shift_up(dyg, dyg_n)
        dua_ref[...] = (wa[2:3, :] * dya + wa[1:2, :] * da1 + wa[0:1, :] * da2).astype(dua_ref.dtype)
        dug_ref[...] = (wg[2:3, :] * dyg + wg[1:2, :] * dg1 + wg[0:1, :] * dg2).astype(dug_ref.dtype)
        ssum = lambda v: jnp.sum(v, axis=0, keepdims=True)
        dwa_ref[...] += jnp.concatenate([ssum(dya * sa2), ssum(dya * sa1), ssum(dya * ua)], axis=0)
        dwg_ref[...] += jnp.concatenate([ssum(dyg * sg2), ssum(dyg * sg1), ssum(dyg * ug)], axis=0)
        dba_ref[...] += ssum(dya)
        dbg_ref[...] += ssum(dyg)

    main = lambda off: pl.BlockSpec((tm, tn), lambda j, i: (i, j + off))
    halo = lambda off: pl.BlockSpec((SUBLANES, tn), lambda j, i: (jnp.maximum(i * hb - 1, 0), j + off))
    nxt = lambda off: pl.BlockSpec((SUBLANES, tn), lambda j, i: (jnp.minimum((i + 1) * hb, s // SUBLANES - 1), j + off))
    wspec = lambda r, off: pl.BlockSpec((r, tn), lambda j, i: (0, j + off))
    return pl.pallas_call(
        body, name="gate_bwd",
        out_shape=(jax.ShapeDtypeStruct((s, D_FF), MXU_DTYPE), jax.ShapeDtypeStruct((s, D_FF), MXU_DTYPE),
                   jax.ShapeDtypeStruct((3, D_FF), F32), jax.ShapeDtypeStruct((3, D_FF), F32),
                   jax.ShapeDtypeStruct((1, D_FF), F32), jax.ShapeDtypeStruct((1, D_FF), F32)),
        grid=(nj, ni),
        in_specs=[main(0), main(nj), halo(0), halo(nj)] + [main(0), nxt(0)] * 4 + [wspec(3, 0), wspec(3, nj)],
        out_specs=(main(0), main(0), wspec(3, 0), wspec(3, 0), wspec(1, 0), wspec(1, 0)),
        compiler_params=_params("parallel", "arbitrary"),
    )(u, u, u, u, dact, dact, a, a, ge, ge, gd, gd, conv_w, conv_w)


def _prep_weights(w_in, w_uq, w_uk, w_uv, w_o, w_up, w_down):
    return {**_prep_weights_first(w_in, w_uq, w_uk, w_uv), **_prep_weights_late(w_o, w_up, w_down)}


def _prep_weights_late(w_o, w_up, w_down):
    w_o, w_up, w_down = _mx(w_o), _mx(w_up), _mx(w_down)
    w_up_t = w_up.T if w_up.ndim == 2 else w_up.transpose(0, 2, 1).reshape(2 * D_FF, D_MODEL)
    return dict(w_o=w_o, w_o_t=w_o.T, w_up=w_up, w_up_t=w_up_t, w_down=w_down, w_down_t=w_down.T)


def _prep_weights_first(w_in, w_uq, w_uk, w_uv):
    c = lambda a: a.astype(MXU_DTYPE)
    w_in = c(w_in)
    z = lambda w: jnp.zeros((D_MODEL, w), MXU_DTYPE)
    r0 = Q_RANK + KV_RANK
    w_in_ext = jnp.concatenate([w_in[:, :r0], z(NOPE), w_in[:, r0:r0 + ROPE], z(32), w_in[:, r0 + ROPE:]], axis=1)
    wq = jnp.pad(c(w_uq).transpose(1, 0, 2), ((0, 0), (0, 0), (0, QK_PAD - NOPE - ROPE)))
    wk = jnp.pad(c(w_uk).transpose(1, 0, 2), ((0, 0), (0, 0), (0, QK_PAD - NOPE)))
    wv = c(w_uv).transpose(1, 0, 2)
    t3 = lambda a: a.transpose(0, 2, 1)
    return dict(w_in=w_in_ext, w_in_t=w_in_ext.T, wq=wq, wq_t=t3(wq), wk=wk, wk_t=t3(wk), wv=wv, wv_t=t3(wv))


def _local_step(x, target, w, g_cq, g_ckv, ln1_g, ln1_b, conv_w, conv_b, ln2_g, ln2_b, comm=None):
    s = x.shape[0]
    tabs = _rope_tables(s)
    r2 = lambda a: a.reshape(1, -1)
    heads = lambda a: a.reshape(s, HEADS, HEAD_DIM).transpose(1, 0, 2)
    unheads = lambda a: a.transpose(1, 0, 2).reshape(s, HEADS * HEAD_DIM)
    cb = r2(conv_b)
    dils = [d for _, d in DIL_PAIRS]

    h, qp, kp, vp = _in_proj(x, w["w_in"], tm=256)
    q, k, v, v_t = _mla_prep_fwd(h, r2(g_cq), r2(g_ckv), w["wq"], w["wk"], w["wv"], w["wv_t"], tabs, tm=256)
    if comm is None:
        o_mla_t, lse_mla = _mla_attn_fwd(q, k, v_t, t=512, g=HEADS)
    else:
        o_mla_t, lse_mla, *gathered = _mla_attn_fwd(q, k, v_t, t=512, g=HEADS, late=comm["late"])
        w = {**w, **comm["finish"](gathered)}
    o_bs, lse_bs = [], []
    for i, d in enumerate(dils):
        o_b, l_b = _dil_fwd(qp[i], kp[i], vp[i], d, name=f"dil_fwd_{d}")
        o_bs.append(o_b)
        lse_bs.append(_unperm_row(l_b, d))
    o_dil, lj = _dil_combine(o_bs, lse_bs, ts=512)
    o_mla = o_mla_t.transpose(2, 0, 1).reshape(s, HEADS * HEAD_DIM)
    x1, xhat1, rstd1 = _out_ln1(o_mla, o_dil, w["w_o"], x, r2(ln1_g), r2(ln1_b), tm=256)
    u = _mm_nn(x1, w["w_up"], name="up_proj", tm=512, tn=1408, tk=D_MODEL)
    act, conv_a, gelu_g, gelu_dg = _gate_fwd(u, conv_w, cb, tm=256, tn=1408)
    dz2, loss, dg2, db2 = _down_ln2_loss(act, w["w_down"], x1, r2(ln2_g), r2(ln2_b), target, tm=256)

    dact = _mm_nn(dz2, w["w_down_t"], name="down_bwd", tm=512, tn=1408, tk=D_MODEL)
    dw_down = _mm_tn(act, dz2, name="dw_down", tm=1408, tn=D_MODEL, ts=512)
    du_a, du_g, dcw_a, dcw_g, dcb_a, dcb_g = _gate_bwd(u, dact, conv_a, gelu_g, gelu_dg, conv_w, tm=256, tn=1408)
    dz1, dg1, db1 = _up_bwd_ln1(du_a, du_g, w["w_up_t"], dz2, xhat1, rstd1, r2(ln1_g), tm=256)
    dw_up = jnp.concatenate([_mm_tn(x1, du_a, name="dw_up_a", tm=D_MODEL, tn=1408, ts=512),
                             _mm_tn(x1, du_g, name="dw_up_g", tm=D_MODEL, tn=1408, ts=512)], axis=1)
    do_mla, do_dil, dd_all = _attn_bwd_heads(dz1, w["w_o_t"], o_mla, o_dil, tm=256)
    dw_o = jnp.concatenate([_mm_tn(o_mla, dz1, name="dw_o_mla", tm=512, tn=D_MODEL, ts=512),
                            _mm_tn(o_dil, dz1, name="dw_o_dil", tm=512, tn=D_MODEL, ts=512)], axis=0)
    dd_all = dd_all.T
    dd_mla, dd_dil = dd_all[:HEADS].reshape(HEADS, 1, s), dd_all[HEADS:].reshape(HEADS, 1, s)
    early = () if comm is None else tuple(comm["halve"]([("w_up", dw_up), ("w_down", dw_down)]))
    dq, dk, dv, *early_slots = _mla_attn_bwd(q, k, v, do_mla, lse_mla, dd_mla, t=512, g=4, early=early)
    parts = []
    for i, d in enumerate(dils):
        parts.append(_dil_bwd(qp[i], kp[i], vp[i], do_dil[i], _perm_row(lj, d), _perm_row(dd_dil, d), d, name=f"dil_bwd_{d}"))
    dh_dil = _dil_merge(parts, ts=512)
    dh_mla, dwq, dwk, dwv, dgq, dgkv = _mla_prep_bwd(h, dq, dk, dv, r2(g_cq), r2(g_ckv),
                                                     w["wq_t"], w["wk_t"], w["wv_t"], tabs, tm=256)
    mla_w = 4 * LANES
    w_in_t = w["w_in_t"]
    grad_x = _mm_nn(dh_mla, w_in_t[:mla_w], name="in_bwd_mla", tm=512, tn=D_MODEL, tk=mla_w, add=dz1, add_scale=DN_ALPHA)
    grad_x = _mm_nn(dh_dil, w_in_t[mla_w:], name="in_bwd_dil", tm=512, tn=D_MODEL, tk=512, add=grad_x)
    dw_mla = _mm_tn(x, dh_mla, name="dw_in_mla", tm=D_MODEL, tn=mla_w, ts=512)
    dw_dil = _mm_tn(x, dh_dil, name="dw_in_dil", tm=D_MODEL, tn=512, ts=512)
    r0 = Q_RANK + KV_RANK
    grads = dict(
        w_in=jnp.concatenate([dw_mla[:, :r0], dw_mla[:, r0 + NOPE:r0 + NOPE + ROPE], dw_dil], axis=1),
        g_cq=dgq[0], g_ckv=dgkv[0],
        w_uq=dwq[:, :, :NOPE + ROPE].transpose(1, 0, 2),
        w_uk=dwk[:, :, :NOPE].transpose(1, 0, 2),
        w_uv=dwv.transpose(1, 0, 2),
        w_o=dw_o, ln1_g=dg1[0], ln1_b=db1[0], w_up=dw_up,
        conv_w=jnp.concatenate([dcw_a, dcw_g], axis=1), conv_b=jnp.concatenate([dcb_a, dcb_g], axis=1)[0],
        w_down=dw_down, ln2_g=dg2[0], ln2_b=db2[0])
    if comm is not None:
        grads["early"] = (early, tuple(early_slots))
    return loss[0, 0], grad_x, grads


N_CHIPS = 4
SHARDED = ("w_in", "w_uq", "w_o", "w_up", "conv_w", "w_down")
COL_SHARDED = ("w_in", "w_up", "conv_w")
SHARD_SHAPE = dict(w_in=(D_MODEL, IN_WIDTH // 4), w_uq=(Q_RANK // 4, HEADS, NOPE + ROPE), w_o=(D_MODEL // 4, D_MODEL),
                   w_up=(D_MODEL, 2 * D_FF // 4), conv_w=(3, 2 * D_FF // 4), w_down=(D_FF // 4, D_MODEL))
SMALL = ("g_cq", "g_ckv", "w_uk", "w_uv", "ln1_g", "ln1_b", "conv_b", "ln2_g", "ln2_b")
SMALL_SHAPE = dict(g_cq=(Q_RANK,), g_ckv=(KV_RANK,), w_uk=(KV_RANK, HEADS, NOPE), w_uv=(KV_RANK, HEADS, HEAD_DIM),
                   ln1_g=(D_MODEL,), ln1_b=(D_MODEL,), conv_b=(2 * D_FF,), ln2_g=(D_MODEL,), ln2_b=(D_MODEL,))
BIG = ("w_in", "w_uq", "w_o", "w_up", "w_down")
BIG_2D = dict(w_in=(D_MODEL, IN_WIDTH // 4), w_uq=(Q_RANK // 4, HEADS * (NOPE + ROPE)), w_o=(D_MODEL // 4, D_MODEL),
              w_up=(D_MODEL, 2 * D_FF // 4), w_down=(D_FF // 4, D_MODEL))
SMALL_G = SMALL + ("conv_w",)
SMALL_G_SHAPE = {**SMALL_SHAPE, "conv_w": (3, 2 * D_FF)}
SMALL_U_SHAPE = {**SMALL_SHAPE, "conv_w": (3, 2 * D_FF // 4)}


def _size(shape):
    return math.prod(shape)


def _padded_rows(n_elems, mult):
    return -(-n_elems // (LANES * mult)) * mult


SHARD_ROWS = {n: _padded_rows(_size(SHARD_SHAPE[n]), SUBLANES) for n in SHARDED}
R_SMALL = -(-sum(_size(SMALL_G_SHAPE[n]) for n in SMALL_G) // (LANES * LANES)) * LANES
GATHER_FIRST = ("w_in", "w_uq")
GATHER_LATE = ("w_o", "w_up", "w_down")
REDUCED_EARLY = ("w_up", "w_down")
REDUCED_LAST = ("w_in", "w_uq", "w_o")


def _rows(a, rows=None):
    flat = a.reshape(-1)
    rows = -(-flat.shape[0] // LANES) if rows is None else rows
    return jnp.pad(flat, (0, rows * LANES - flat.shape[0])).reshape(rows, LANES)


def _blocked(name, g):
    r, c = BIG_2D[name]
    a = g.reshape(r, N_CHIPS, c).transpose(1, 0, 2) if name in COL_SHARDED else g.reshape(N_CHIPS, r, c)
    return a.reshape(N_CHIPS, 2, r // 2, c)


def _pack_flat(t, names):
    return _rows(jnp.concatenate([t[n].astype(F32).reshape(-1) for n in names]), R_SMALL)


def _unpack_flat(buf, names, shapes):
    flat, out, r = buf.reshape(-1), {}, 0
    for n in names:
        out[n] = flat[r:r + _size(shapes[n])].reshape(shapes[n])
        r += _size(shapes[n])
    return out


def _from_chip_blocks(name, blocks):
    shp = SHARD_SHAPE[name]
    a = blocks.reshape(N_CHIPS, -1)[:, :_size(shp)].reshape((N_CHIPS,) + shp)
    if name in COL_SHARDED:
        return a.transpose(1, 0, 2).reshape(shp[0], N_CHIPS * shp[1])
    return a.reshape((N_CHIPS * shp[0],) + shp[1:])


ANY = pl.BlockSpec(memory_space=pl.ANY)
COMM_PARAMS = pltpu.CompilerParams(has_side_effects=True)


def _coords():
    return lax.axis_index("x"), lax.axis_index("y"), lax.axis_index("c")


def _other_chips(x, y):
    return [(1 - x, y), (x, 1 - y), (1 - x, 1 - y)]


def _remote(src, dst, send_sems, recv_sems, k, to):
    return pltpu.make_async_remote_copy(src_ref=src, dst_ref=dst, send_sem=send_sems.at[k], recv_sem=recv_sems.at[k],
                                        device_id=to, device_id_type=MESH)


def _gather_in_steps(wp_refs, wout_refs, send_sems, recv_sems, *, first, mid, last):
    x, y, c = _coords()
    me = 2 * x + y
    sib = (x, y, 1 - c)
    chips = _other_chips(x, y)
    n = len(wp_refs)
    pairs = [(j, t, px, py) for j, (px, py) in enumerate(chips) for t in range(n)]
    ici = [_remote(wp_refs[t].at[c], wout_refs[t].at[me, c], send_sems, recv_sems, j * n + t, (px, py, c))
           for j, t, px, py in pairs]
    fwd = [_remote(wout_refs[t].at[2 * px + py, c], wout_refs[t].at[2 * px + py, c], send_sems, recv_sems, (3 + j) * n + t, sib)
           for j, t, px, py in pairs]

    @pl.when(first)
    def _():
        for cp in ici:
            cp.start()

    @pl.when(mid)
    def _():
        for i, (j, t, px, py) in enumerate(pairs):
            _remote(wp_refs[t].at[c], wout_refs[t].at[2 * px + py, c], send_sems, recv_sems, j * n + t, (px, py, c)).wait_recv()
            fwd[i].start()

    @pl.when(last)
    def _():
        for j, t, px, py in pairs:
            k = 2 * px + py
            _remote(wout_refs[t].at[k, 1 - c], wout_refs[t].at[k, 1 - c], send_sems, recv_sems, (3 + j) * n + t, sib).wait_recv()
        for cp in ici + fwd:
            cp.wait_send()


def _exchange_in_steps(ps_refs, ss_refs, send_sems, recv_sems, *, first, last):
    x, y, c = _coords()
    me = 2 * x + y
    chips = _other_chips(x, y)
    n = len(ps_refs)
    sends = [_remote(ps_refs[t].at[2 * px + py], ss_refs[t].at[me], send_sems, recv_sems, j * n + t, (px, py, c))
             for j, (px, py) in enumerate(chips) for t in range(n)]

    @pl.when(first)
    def _():
        for cp in sends:
            cp.start()

    @pl.when(last)
    def _():
        for j, (px, py) in enumerate(chips):
            for t in range(n):
                _remote(ps_refs[t].at[me], ss_refs[t].at[2 * px + py], send_sems, recv_sems, j * n + t, (px, py, c)).wait_recv()
        for cp in sends:
            cp.wait_send()


def _gather_weights(wp, cwp):
    def body(wp_ref, cw_ref, wout_ref, cwout_ref, send_sems, recv_sems):
        x, y, c = _coords()
        me = 2 * x + y
        sib = (x, y, 1 - c)
        chips = _other_chips(x, y)
        sends = [_remote(wp_ref.at[c], wout_ref.at[me, c], send_sems, recv_sems, j, (px, py, c))
                 for j, (px, py) in enumerate(chips)]
        sends += [_remote(cw_ref, cwout_ref.at[me], send_sems, recv_sems, 3 + j, (px, py, c))
                  for j, (px, py) in enumerate(chips)]
        for cp in sends:
            cp.start()
        for j, (px, py) in enumerate(chips):
            k = 2 * px + py
            _remote(wp_ref.at[c], wout_ref.at[k, c], send_sems, recv_sems, j, (px, py, c)).wait_recv()
            fwd = _remote(wout_ref.at[k, c], wout_ref.at[k, c], send_sems, recv_sems, 6 + j, sib)
            fwd.start()
            sends.append(fwd)
        for j, (px, py) in enumerate(chips):
            k = 2 * px + py
            _remote(cw_ref, cwout_ref.at[k], send_sems, recv_sems, 3 + j, (px, py, c)).wait_recv()
            _remote(wout_ref.at[k, 1 - c], wout_ref.at[k, 1 - c], send_sems, recv_sems, 6 + j, sib).wait_recv()
        for cp in sends:
            cp.wait_send()

    return pl.pallas_call(
        body, name="gather_weights",
        out_shape=(jax.ShapeDtypeStruct((N_CHIPS,) + wp.shape, wp.dtype), jax.ShapeDtypeStruct((N_CHIPS,) + cwp.shape, cwp.dtype)),
        in_specs=[ANY, ANY], out_specs=(ANY, ANY),
        scratch_shapes=[pltpu.SemaphoreType.DMA((9,)), pltpu.SemaphoreType.DMA((9,))],
        compiler_params=COMM_PARAMS,
    )(wp, cwp)


def _exchange_sibling_halves(gs, whole, *, name):
    n, nw = len(gs), len(whole)

    def body(*refs):
        gs_refs, wh_refs = refs[:n], refs[n:n + nw]
        os_refs, ow_refs = refs[n + nw:2 * n + nw], refs[2 * n + nw:2 * (n + nw)]
        send_sems, recv_sems = refs[2 * (n + nw):]
        x, y, c = _coords()
        sib = (x, y, 1 - c)
        cps = [_remote(gs_refs[t].at[k, 1 - c], os_refs[t].at[k], send_sems, recv_sems, t * N_CHIPS + k, sib)
               for t in range(n) for k in range(N_CHIPS)]
        cps += [_remote(wh_refs[t], ow_refs[t], send_sems, recv_sems, n * N_CHIPS + t, sib) for t in range(nw)]
        for cp in cps:
            cp.start()
        for cp in cps:
            cp.wait_recv()
        for cp in cps:
            cp.wait_send()

    n_sem = n * N_CHIPS + nw
    return pl.pallas_call(
        body, name=name,
        out_shape=tuple(jax.ShapeDtypeStruct((N_CHIPS,) + a.shape[2:], F32) for a in gs)
        + tuple(jax.ShapeDtypeStruct(a.shape, F32) for a in whole),
        in_specs=[ANY] * (n + nw), out_specs=(ANY,) * (n + nw),
        scratch_shapes=[pltpu.SemaphoreType.DMA((n_sem,)), pltpu.SemaphoreType.DMA((n_sem,))],
        compiler_params=COMM_PARAMS,
    )(*gs, *whole)


def _exchange_chips(ps, pr):
    n = len(ps)

    def body(*refs):
        ps_refs, pr_ref, ss_refs, sr_ref = refs[:n], refs[n], refs[n + 1:2 * n + 1], refs[2 * n + 1]
        send_sems, recv_sems = refs[2 * n + 2:]
        x, y, c = _coords()
        me = 2 * x + y
        chips = _other_chips(x, y)
        sends = []
        for j, (px, py) in enumerate(chips):
            to = (px, py, c)
            for t in range(n):
                sends.append(_remote(ps_refs[t].at[2 * px + py], ss_refs[t].at[me], send_sems, recv_sems, j * (n + 1) + t, to))
            sends.append(_remote(pr_ref, sr_ref.at[me], send_sems, recv_sems, j * (n + 1) + n, to))
        for cp in sends:
            cp.start()
        for j, (px, py) in enumerate(chips):
            k, to = 2 * px + py, (px, py, c)
            for t in range(n):
                _remote(ps_refs[t].at[me], ss_refs[t].at[k], send_sems, recv_sems, j * (n + 1) + t, to).wait_recv()
            _remote(pr_ref, sr_ref.at[k], send_sems, recv_sems, j * (n + 1) + n, to).wait_recv()
        for cp in sends:
            cp.wait_send()

    n_sem = 3 * (n + 1)
    return pl.pallas_call(
        body, name="exchange_chips",
        out_shape=tuple(jax.ShapeDtypeStruct(a.shape, a.dtype) for a in ps) + (jax.ShapeDtypeStruct((N_CHIPS,) + pr.shape, F32),),
        in_specs=[ANY] * (n + 1), out_specs=(ANY,) * (n + 1),
        scratch_shapes=[pltpu.SemaphoreType.DMA((n_sem,)), pltpu.SemaphoreType.DMA((n_sem,))],
        compiler_params=COMM_PARAMS,
    )(*ps, pr)


def _exchange_sibling_result(gh):
    n = len(gh)

    def body(*refs):
        gh_refs, out_refs, (send_sems, recv_sems) = refs[:n], refs[n:2 * n], refs[2 * n:]
        x, y, c = _coords()
        cps = [_remote(gh_refs[t], out_refs[t], send_sems, recv_sems, t, (x, y, 1 - c)) for t in range(n)]
        for cp in cps:
            cp.start()
        for cp in cps:
            cp.wait_recv()
        for cp in cps:
            cp.wait_send()

    return pl.pallas_call(
        body, name="exchange_sibling_result",
        out_shape=tuple(jax.ShapeDtypeStruct(a.shape, F32) for a in gh),
        in_specs=[ANY] * n, out_specs=(ANY,) * n,
        scratch_shapes=[pltpu.SemaphoreType.DMA((n,)), pltpu.SemaphoreType.DMA((n,))],
        compiler_params=COMM_PARAMS,
    )(*gh)


def _add_own_half(gs, recv, c_arr, *, name):
    _, rows, cols = recv.shape

    def body(c_ref, a_ref, b_ref, o_ref):
        o_ref[0] = (a_ref[0, 0] + b_ref[0]).astype(o_ref.dtype)

    return pl.pallas_call(
        body, name=name,
        out_shape=jax.ShapeDtypeStruct(recv.shape, GRAD_WIRE_DTYPE),
        grid_spec=pltpu.PrefetchScalarGridSpec(
            num_scalar_prefetch=1, grid=(N_CHIPS,),
            in_specs=[pl.BlockSpec((1, 1, rows, cols), lambda k, c_ref: (k, c_ref[0], 0, 0)),
                      pl.BlockSpec((1, rows, cols), lambda k, c_ref: (k, 0, 0))],
            out_specs=pl.BlockSpec((1, rows, cols), lambda k, c_ref: (k, 0, 0))),
        compiler_params=_params("parallel"),
    )(c_arr, gs, recv)


def _add2(a, b, *, name):
    def body(a_ref, b_ref, o_ref):
        o_ref[...] = a_ref[...] + b_ref[...]

    return pl.pallas_call(body, name=name, out_shape=jax.ShapeDtypeStruct(a.shape, F32))(a, b)


def _sum_slots(slots, *, tr, name):
    _, r, c = slots.shape

    def body(s_ref, o_ref):
        f = lambda k: s_ref[k].astype(F32)
        o_ref[...] = ((f(0) + f(1)) + f(2)) + f(3)

    return pl.pallas_call(
        body, name=name,
        out_shape=jax.ShapeDtypeStruct((r, c), F32),
        grid=(r // tr,),
        in_specs=[pl.BlockSpec((N_CHIPS, tr, c), lambda i: (0, i, 0))],
        out_specs=pl.BlockSpec((tr, c), lambda i: (i, 0)),
        compiler_params=_params("parallel"),
    )(slots)


def _adamw(w, g, m, v, *, tr, name):
    r, cols = w.shape

    def body(w_ref, g_ref, m_ref, v_ref, d_ref, nm_ref, nv_ref):
        g_ = g_ref[...]
        m_ = ADAM_B1 * m_ref[...] + (1.0 - ADAM_B1) * g_
        v_ = ADAM_B2 * v_ref[...] + (1.0 - ADAM_B2) * (g_ * g_)
        m_hat = m_ / (1.0 - ADAM_B1 ** ADAM_STEP)
        v_hat = v_ / (1.0 - ADAM_B2 ** ADAM_STEP)
        d_ref[...] = -ADAM_LR * (m_hat / (jnp.sqrt(v_hat) + ADAM_EPS) + ADAM_WD * w_ref[...])
        nm_ref[...] = m_
        nv_ref[...] = v_

    spec = pl.BlockSpec((tr, cols), lambda i: (i, 0))
    out = jax.ShapeDtypeStruct((r, cols), F32)
    return pl.pallas_call(
        body, name=name, out_shape=(out, out, out), grid=(r // tr,),
        in_specs=[spec] * 4, out_specs=(spec,) * 3,
        compiler_params=_params("parallel"),
    )(w, g, m, v)


WEIGHTS = ("w_in", "g_cq", "g_ckv", "w_uq", "w_uk", "w_uv", "w_o", "ln1_g", "ln1_b", "w_up", "conv_w", "conv_b",
           "w_down", "ln2_g", "ln2_b")


def kernel(x, w_in, g_cq, g_ckv, w_uq, w_uk, w_uv, w_o, ln1_g, ln1_b, w_up, conv_w, conv_b, w_down, ln2_g, ln2_b, loss_target, m_w_in, m_g_cq, m_g_ckv, m_w_uq, m_w_uk, m_w_uv, m_w_o, m_ln1_g, m_ln1_b, m_w_up, m_conv_w, m_conv_b, m_w_down, m_ln2_g, m_ln2_b, v_w_in, v_g_cq, v_g_ckv, v_w_uq, v_w_uk, v_w_uv, v_w_o, v_ln1_g, v_ln1_b, v_w_up, v_conv_w, v_conv_b, v_w_down, v_ln2_g, v_ln2_b):
    wts = dict(zip(WEIGHTS, (w_in, g_cq, g_ckv, w_uq, w_uk, w_uv, w_o, ln1_g, ln1_b, w_up, conv_w, conv_b, w_down, ln2_g, ln2_b)))
    mom = dict(zip(WEIGHTS, (m_w_in, m_g_cq, m_g_ckv, m_w_uq, m_w_uk, m_w_uv, m_w_o, m_ln1_g, m_ln1_b, m_w_up, m_conv_w, m_conv_b, m_w_down, m_ln2_g, m_ln2_b)))
    var = dict(zip(WEIGHTS, (v_w_in, v_g_cq, v_g_ckv, v_w_uq, v_w_uk, v_w_uv, v_w_o, v_ln1_g, v_ln1_b, v_w_up, v_conv_w, v_conv_b, v_w_down, v_ln2_g, v_ln2_b)))

    me = 2 * lax.axis_index("x") + lax.axis_index("y")
    my_c = lax.axis_index("c")
    c_arr = my_c.astype(jnp.int32).reshape(1)
    own = lambda slots, mine: lax.dynamic_update_index_in_dim(slots, mine, me, 0)

    def pack(names):
        return jnp.concatenate([_rows(_mx(wts[n]), SHARD_ROWS[n]) for n in names], axis=0).reshape(2, -1, LANES)

    def unpack(names, gathered, mine):
        buf, full, r = own(gathered, mine).reshape(N_CHIPS, -1, LANES), {}, 0
        for n in names:
            full[n] = _from_chip_blocks(n, buf[:, r:r + SHARD_ROWS[n]])
            r += SHARD_ROWS[n]
        return full

    wp_first = pack(GATHER_FIRST)
    cwp = _rows(conv_w, SHARD_ROWS["conv_w"])
    gathered, cwfull = _gather_weights(wp_first, cwp)
    full = unpack(GATHER_FIRST, gathered, wp_first)
    conv_w_full = _from_chip_blocks("conv_w", own(cwfull, cwp))
    w = _prep_weights_first(full["w_in"], full["w_uq"], w_uk, w_uv)
    late_halves = [_mx(wts[n]).reshape(2, BIG_2D[n][0] // 2, BIG_2D[n][1]) for n in GATHER_LATE]

    def finish(gathered_late):
        w_o_b, w_up_b, w_down_b = (own(a, mine).reshape((N_CHIPS,) + BIG_2D[n])
                                   for a, mine, n in zip(gathered_late, late_halves, GATHER_LATE))
        return _prep_weights_late(w_o_b.reshape(D_MODEL, D_MODEL), w_up_b, w_down_b.reshape(D_FF, D_MODEL))

    def halve(named, whole=(), tag="early"):
        gb = [_blocked(n, a) for n, a in named]
        recv = _exchange_sibling_halves(gb, list(whole), name=f"exchange_sibling_halves_{tag}")
        ps = [_add_own_half(gb[i], recv[i], c_arr, name=f"add_half_{n}") for i, (n, _) in enumerate(named)]
        return ps + [_add2(a, recv[len(gb) + i], name=f"add_whole_{tag}_{i}") for i, a in enumerate(whole)]

    comm = dict(late=late_halves, finish=finish, halve=halve)
    loss, grad_x, g = _local_step(x[0], loss_target[0], w, g_cq, g_ckv, ln1_g, ln1_b, conv_w_full, conv_b, ln2_g, ln2_b, comm=comm)

    ps_early, slots_early = g.pop("early")
    g["loss"] = loss.reshape(1)
    *ps_rest, pr = halve([(n, g[n]) for n in REDUCED_LAST], whole=[_pack_flat(g, SMALL_G + ("loss",))], tag="last")
    *slots_rest, slots_r = _exchange_chips(ps_rest, pr)
    ps = {**dict(zip(REDUCED_LAST, ps_rest)), **dict(zip(REDUCED_EARLY, ps_early))}
    slots = {**dict(zip(REDUCED_LAST, slots_rest)), **dict(zip(REDUCED_EARLY, slots_early))}
    slots = [own(slots[n], lax.dynamic_index_in_dim(ps[n], me, 0, keepdims=False)) for n in BIG]
    slots_r = own(slots_r, pr)
    g_half = [_sum_slots(slots[i], tr=slots[i].shape[1] // 2, name=f"sum_chips_{n}") for i, n in enumerate(BIG)]
    g_small = _unpack_flat(_sum_slots(slots_r, tr=R_SMALL, name="sum_chips_small"), SMALL_G + ("loss",),
                           {**SMALL_G_SHAPE, "loss": (1,)})
    loss = g_small.pop("loss")[0]
    g_other = _exchange_sibling_result(g_half)
    grads = {n: jnp.where(my_c == 0, jnp.concatenate([g_half[i], g_other[i]]), jnp.concatenate([g_other[i], g_half[i]]))
             for i, n in enumerate(BIG)}
    g_small["conv_w"] = lax.dynamic_slice_in_dim(g_small["conv_w"], me * SHARD_SHAPE["conv_w"][1], SHARD_SHAPE["conv_w"][1], 1)
    grads.update(g_small)

    res = {}
    for n in BIG:
        as2d = lambda a: a.reshape(BIG_2D[n])
        d, m, v = _adamw(as2d(wts[n]), grads[n], as2d(mom[n]), as2d(var[n]), tr=BIG_2D[n][0] // 4, name=f"adamw_{n}")
        res[n] = [a.reshape(SHARD_SHAPE[n]) for a in (grads[n], d, m, v)]
    flat = lambda t: _pack_flat(t, SMALL_G)
    dmv = _adamw(flat(wts), flat(g_small), flat(mom), flat(var), tr=R_SMALL, name="adamw_small")
    dmv = [_unpack_flat(a, SMALL_G, SMALL_U_SHAPE) for a in dmv]
    for n in SMALL_G:
        res[n] = [g_small[n]] + [t[n] for t in dmv]
    outs = [res[n][j] for j in range(4) for n in WEIGHTS]
    return (loss, grad_x[None], *outs)
```

```python
import functools
import math

import jax
import jax.numpy as jnp
from jax import lax
from jax.experimental import pallas as pl
from jax.experimental.pallas import tpu as pltpu

F32 = jnp.float32
MXU_DTYPE = jnp.bfloat16
GRAD_WIRE_DTYPE = jnp.bfloat16
NEG = -1e30

D_MODEL = 1024
HEADS = 8
HEAD_DIM = 64
Q_RANK = 256
KV_RANK = 128
NOPE = 64
ROPE = 32
QK_PAD = 128
IN_WIDTH = 1952
IN_EXT = 2048
D_FF = 2816
DIL_PAIRS = ((128, 1), (512, 4), (2048, 16))
DIL_BLOCK = 128
ROPE_THETA = 10000.0
DN_ALPHA = 2.0 ** 0.25
LN_EPS = 1e-5
RMS_EPS = 1e-6
MLA_SCALE = 1.0 / math.sqrt(NOPE + ROPE)
DIL_SCALE = 1.0 / math.sqrt(HEAD_DIM)

ADAM_LR = 0.001
ADAM_B1 = 0.9
ADAM_B2 = 0.999
ADAM_EPS = 1e-08
ADAM_WD = 0.01
ADAM_STEP = 10

LANES = 128
SUBLANES = 8
VMEM_LIMIT_BYTES = 56 * 1024 * 1024

MESH = pl.DeviceIdType.MESH


def _params(*sem):
    return pltpu.CompilerParams(dimension_semantics=sem, vmem_limit_bytes=VMEM_LIMIT_BYTES)


def _dot(a, b):
    return jnp.dot(a, b, preferred_element_type=F32)


def _dot_nt(a, b):
    return lax.dot_general(a, b, (((1,), (1,)), ((), ())), preferred_element_type=F32)


def _dot_tn(a, b):
    return lax.dot_general(a, b, (((0,), (0,)), ((), ())), preferred_element_type=F32)


def _mx(a):
    return a.astype(MXU_DTYPE)


def _mm_nn(a, b, *, name, tm, tn, tk, out_dtype=F32, add=None, add_scale=1.0):
    m, kdim = a.shape
    blocked = b.ndim == 3
    n = b.shape[0] * b.shape[2] if blocked else b.shape[1]
    nk = kdim // tk

    def body(*refs):
        if add is None:
            a_ref, b_ref, o_ref, acc = refs
        else:
            a_ref, b_ref, c_ref, o_ref, acc = refs
        k = pl.program_id(2)

        @pl.when(k == 0)
        def _():
            acc[...] = jnp.zeros_like(acc)

        acc[...] += _dot(_mx(a_ref[...]), _mx(b_ref[...]))

        @pl.when(k == nk - 1)
        def _():
            r = acc[...]
            if add is not None:
                r = r + add_scale * c_ref[...]
            o_ref[...] = r.astype(out_dtype)

    b_spec = (pl.BlockSpec((None, tk, tn), lambda i, j, k: (j, k, 0)) if blocked
              else pl.BlockSpec((tk, tn), lambda i, j, k: (k, j)))
    in_specs = [pl.BlockSpec((tm, tk), lambda i, j, k: (i, k)), b_spec]
    args = [a, b]
    if add is not None:
        in_specs.append(pl.BlockSpec((tm, tn), lambda i, j, k: (i, j)))
        args.append(add)
    return pl.pallas_call(
        body, name=name,
        out_shape=jax.ShapeDtypeStruct((m, n), out_dtype),
        grid=(m // tm, n // tn, nk),
        in_specs=in_specs,
        out_specs=pl.BlockSpec((tm, tn), lambda i, j, k: (i, j)),
        scratch_shapes=[pltpu.VMEM((tm, tn), F32)],
        compiler_params=_params("parallel", "parallel", "arbitrary"),
    )(*args)


def _mm_tn(a, b, *, name, tm, tn, ts, out_dtype=F32):
    s, m = a.shape
    n = b.shape[1]
    ns = s // ts

    def body(a_ref, b_ref, o_ref, acc):
        k = pl.program_id(2)

        @pl.when(k == 0)
        def _():
            acc[...] = jnp.zeros_like(acc)

        acc[...] += _dot_tn(_mx(a_ref[...]), _mx(b_ref[...]))

        @pl.when(k == ns - 1)
        def _():
            o_ref[...] = acc[...].astype(out_dtype)

    return pl.pallas_call(
        body, name=name,
        out_shape=jax.ShapeDtypeStruct((m, n), out_dtype),
        grid=(m // tm, n // tn, ns),
        in_specs=[pl.BlockSpec((ts, tm), lambda i, j, k: (k, i)),
                  pl.BlockSpec((ts, tn), lambda i, j, k: (k, j))],
        out_specs=pl.BlockSpec((tm, tn), lambda i, j, k: (i, j)),
        scratch_shapes=[pltpu.VMEM((tm, tn), F32)],
        compiler_params=_params("parallel", "parallel", "arbitrary"),
    )(a, b)


def _in_proj(x, w_in_ext, *, tm):
    s = x.shape[0]
    mla_w = 4 * LANES
    dil_w = HEADS * HEAD_DIM
    dils = [d for _, d in DIL_PAIRS]

    def body(x_ref, w_ref, h_ref, *rest):
        outs, sc = rest[:-1], rest[-1]
        xb = _mx(x_ref[...])
        h_ref[...] = _dot(xb, w_ref[:, 0:mla_w])
        for j in range(3):
            part = _dot(xb, w_ref[:, mla_w + j * dil_w:mla_w + (j + 1) * dil_w])
            for hd in range(HEADS):
                sc[hd] = part[:, hd * HEAD_DIM:(hd + 1) * HEAD_DIM]
            for b, d in enumerate(dils):
                _store_residue_major(outs[3 * j + b], sc, d, tm)

    shapes, specs = _residue_major_outs(s, tm, dils, MXU_DTYPE)
    res = pl.pallas_call(
        body, name="in_proj",
        out_shape=(jax.ShapeDtypeStruct((s, mla_w), F32),) + shapes * 3,
        grid=(s // tm,),
        in_specs=[pl.BlockSpec((tm, D_MODEL), lambda i: (i, 0)), pl.BlockSpec((D_MODEL, IN_EXT), lambda i: (0, 0))],
        out_specs=(pl.BlockSpec((tm, mla_w), lambda i: (i, 0)),) + specs * 3,
        scratch_shapes=[pltpu.VMEM((HEADS, tm, HEAD_DIM), F32)],
        compiler_params=_params("parallel"),
    )(x, w_in_ext)
    hm = lambda a: a.reshape(HEADS, s, HEAD_DIM)
    return res[0], [hm(a) for a in res[1:4]], [hm(a) for a in res[4:7]], [hm(a) for a in res[7:10]]


def _residue_major_outs(s, tm, dils, dtype):
    shapes, specs = [], []
    for d in dils:
        if d == 1:
            shapes.append(jax.ShapeDtypeStruct((HEADS, s, HEAD_DIM), dtype))
            specs.append(pl.BlockSpec((HEADS, tm, HEAD_DIM), lambda i: (0, i, 0)))
        else:
            shapes.append(jax.ShapeDtypeStruct((HEADS, d, s // d, HEAD_DIM), dtype))
            specs.append(pl.BlockSpec((HEADS, d, tm // d, HEAD_DIM), lambda i: (0, 0, i, 0)))
    return tuple(shapes), tuple(specs)


def _store_residue_major(o_ref, src_ref, d, tm):
    if d == 1:
        o_ref[...] = src_ref[...].astype(o_ref.dtype)
    else:
        for r in range(d):
            o_ref[:, r] = src_ref[:, pl.ds(r, tm // d, stride=d), :].astype(o_ref.dtype)


def _load_token_order(dst_ref, src_ref, d, tm, accumulate=False):
    if d == 1:
        dst_ref[...] = dst_ref[...] + src_ref[...] if accumulate else src_ref[...]
    else:
        for r in range(d):
            rows = pl.ds(r, tm // d, stride=d)
            dst_ref[:, rows, :] = dst_ref[:, rows, :] + src_ref[:, r] if accumulate else src_ref[:, r]


def _attn_bwd_heads(dz1, w_o_t, a_mla, a_dil, *, tm):
    s = dz1.shape[0]
    half = HEADS * HEAD_DIM
    dils = [d for _, d in DIL_PAIRS]

    def body(dz_ref, w_ref, am_ref, ad_ref, dom_ref, dd_ref, *dod_refs):
        dzb = _mx(dz_ref[...])
        for j, (a_ref, o_ref) in enumerate(((am_ref, dom_ref), (ad_ref, dod_refs[0]))):
            da = _dot(dzb, w_ref[:, j * half:(j + 1) * half])
            prod = da * a_ref[...]
            for hd in range(HEADS):
                sl = slice(hd * HEAD_DIM, (hd + 1) * HEAD_DIM)
                o_ref[hd] = da[:, sl].astype(o_ref.dtype)
                dd_ref[:, j * HEADS + hd:j * HEADS + hd + 1] = jnp.sum(prod[:, sl], axis=-1, keepdims=True)
        for b, d in enumerate(dils[1:]):
            _store_residue_major(dod_refs[1 + b], dod_refs[0], d, tm)

    hspec = pl.BlockSpec((HEADS, tm, HEAD_DIM), lambda i: (0, i, 0))
    row = lambda w: pl.BlockSpec((tm, w), lambda i: (i, 0))
    shapes, specs = _residue_major_outs(s, tm, dils, F32)
    do_mla, dd, *do_dil = pl.pallas_call(
        body, name="attn_bwd_heads",
        out_shape=(jax.ShapeDtypeStruct((HEADS, s, HEAD_DIM), MXU_DTYPE), jax.ShapeDtypeStruct((s, 2 * HEADS), F32)) + shapes,
        grid=(s // tm,),
        in_specs=[row(D_MODEL), pl.BlockSpec((D_MODEL, D_MODEL), lambda i: (0, 0)), row(half), row(half)],
        out_specs=(hspec, row(2 * HEADS)) + specs,
        compiler_params=_params("parallel"),
    )(dz1, w_o_t, a_mla, a_dil)
    return do_mla, [a.reshape(HEADS, s, HEAD_DIM) for a in do_dil], dd


def _dil_merge(parts, *, ts):
    hds, s, e = parts[0][0].shape
    dils = [d for _, d in DIL_PAIRS]

    def body(*refs):
        o_ref, sc = refs[9], refs[10]
        for j in range(3):
            for b, d in enumerate(dils):
                _load_token_order(sc, refs[3 * b + j], d, ts, accumulate=b > 0)
            tot = sc[...]
            for hd in range(hds):
                col = j * hds * e + hd * e
                o_ref[:, col:col + e] = tot[hd].astype(o_ref.dtype)

    _, specs = _residue_major_outs(s, ts, dils, F32)
    view = lambda a, d: a if d == 1 else a.reshape(hds, d, s // d, e)
    return pl.pallas_call(
        body, name="dil_merge",
        out_shape=jax.ShapeDtypeStruct((s, 3 * hds * e), MXU_DTYPE),
        grid=(s // ts,),
        in_specs=[specs[b] for b in range(3) for _ in range(3)],
        out_specs=pl.BlockSpec((ts, 3 * hds * e), lambda i: (i, 0)),
        scratch_shapes=[pltpu.VMEM((hds, ts, e), F32)],
        compiler_params=_params("parallel"),
    )(*[view(parts[b][j], dils[b]) for b in range(3) for j in range(3)])


def _rope_tables(s):
    half = ROPE // 2
    freqs = ROPE_THETA ** (-jnp.arange(half, dtype=F32) / half)
    ang = jnp.arange(s).astype(F32)[:, None] * freqs[None, :]
    cos, sin = jnp.cos(ang), jnp.sin(ang)
    z = lambda w: jnp.zeros((s, w), F32)
    c = jnp.concatenate([jnp.ones((s, NOPE), F32), cos, cos, z(32)], axis=1)
    s1 = jnp.concatenate([z(NOPE + half), sin, z(32)], axis=1)
    s2 = jnp.concatenate([z(NOPE), -sin, z(half + 32)], axis=1)
    mask = jnp.concatenate([z(NOPE), jnp.ones((s, ROPE), F32), z(32)], axis=1)
    return c, s1, s2, mask


def _rope(x, c, s1, s2):
    return x * c + pltpu.roll(x, 16, 1) * s1 + pltpu.roll(x, LANES - 16, 1) * s2


def _unrope(dy, c, s1, s2):
    return dy * c + pltpu.roll(dy * s1, LANES - 16, 1) + pltpu.roll(dy * s2, 16, 1)


def _rms(x):
    r = lax.rsqrt(jnp.mean(x * x, axis=-1, keepdims=True) + RMS_EPS)
    return x * r, r


def _mla_prep_fwd(h, g_cq, g_ckv, wq, wk, wv, wv_t, tabs, *, tm):
    s = h.shape[0]
    c_t, s1_t, s2_t, _ = tabs

    def body(h_ref, gq_ref, gkv_ref, wq_ref, wk_ref, wv_ref, wvt_ref, c_ref, s1_ref, s2_ref,
             q_ref, k_ref, v_ref, vt_ref):
        cq = h_ref[:, 0:Q_RANK]
        ckv = h_ref[:, Q_RANK:Q_RANK + KV_RANK]
        kr = h_ref[:, Q_RANK + KV_RANK:Q_RANK + KV_RANK + QK_PAD]
        c, s1, s2 = c_ref[...], s1_ref[...], s2_ref[...]
        cqn = _mx(_rms(cq)[0] * gq_ref[...])
        ckvn = _mx(_rms(ckv)[0] * gkv_ref[...])
        kr_rot = _rope(kr, c, s1, s2)
        for hd in range(HEADS):
            q_ref[hd] = _rope(_dot(cqn, wq_ref[hd]), c, s1, s2).astype(q_ref.dtype)
            k_ref[hd] = (_dot(ckvn, wk_ref[hd]) + kr_rot).astype(k_ref.dtype)
            v_ref[hd] = _dot(ckvn, wv_ref[hd]).astype(v_ref.dtype)
            vt_ref[hd] = _dot_nt(wvt_ref[hd], ckvn).astype(vt_ref.dtype)

    full = lambda shp: pl.BlockSpec(shp, lambda i: (0,) * len(shp))
    row = lambda w: pl.BlockSpec((tm, w), lambda i: (i, 0))
    return pl.pallas_call(
        body, name="mla_prep_fwd",
        out_shape=(jax.ShapeDtypeStruct((HEADS, s, QK_PAD), MXU_DTYPE),
                   jax.ShapeDtypeStruct((HEADS, s, QK_PAD), MXU_DTYPE),
                   jax.ShapeDtypeStruct((HEADS, s, HEAD_DIM), MXU_DTYPE),
                   jax.ShapeDtypeStruct((HEADS, HEAD_DIM, s), MXU_DTYPE)),
        grid=(s // tm,),
        in_specs=[row(4 * LANES), full((1, Q_RANK)), full((1, KV_RANK)),
                  full((HEADS, Q_RANK, QK_PAD)), full((HEADS, KV_RANK, QK_PAD)), full((HEADS, KV_RANK, HEAD_DIM)),
                  full((HEADS, HEAD_DIM, KV_RANK)), row(LANES), row(LANES), row(LANES)],
        out_specs=(pl.BlockSpec((HEADS, tm, QK_PAD), lambda i: (0, i, 0)),
                   pl.BlockSpec((HEADS, tm, QK_PAD), lambda i: (0, i, 0)),
                   pl.BlockSpec((HEADS, tm, HEAD_DIM), lambda i: (0, i, 0)),
                   pl.BlockSpec((HEADS, HEAD_DIM, tm), lambda i: (0, 0, i))),
        compiler_params=_params("parallel"),
    )(h, g_cq, g_ckv, wq, wk, wv, wv_t, c_t, s1_t, s2_t)


def _mla_prep_bwd(h, dq, dk, dv, g_cq, g_ckv, wq_t, wk_t, wv_t, tabs, *, tm):
    s = h.shape[0]
    c_t, s1_t, s2_t, mask_t = tabs

    def body(h_ref, dq_ref, dk_ref, dv_ref, gq_ref, gkv_ref, wqt_ref, wkt_ref, wvt_ref,
             c_ref, s1_ref, s2_ref, mask_ref, dh_ref, dwq_ref, dwk_ref, dwv_ref, dgq_ref, dgkv_ref):
        i = pl.program_id(0)

        @pl.when(i == 0)
        def _():
            dwq_ref[...] = jnp.zeros_like(dwq_ref)
            dwk_ref[...] = jnp.zeros_like(dwk_ref)
            dwv_ref[...] = jnp.zeros_like(dwv_ref)
            dgq_ref[...] = jnp.zeros_like(dgq_ref)
            dgkv_ref[...] = jnp.zeros_like(dgkv_ref)

        cq = h_ref[:, 0:Q_RANK]
        ckv = h_ref[:, Q_RANK:Q_RANK + KV_RANK]
        c, s1, s2 = c_ref[...], s1_ref[...], s2_ref[...]
        cqh, rq = _rms(cq)
        ckvh, rkv = _rms(ckv)
        gq, gkv = gq_ref[...], gkv_ref[...]
        cqn = _mx(cqh * gq)
        ckvn = _mx(ckvh * gkv)
        dcqn = jnp.zeros((tm, Q_RANK), F32)
        dckvn = jnp.zeros((tm, KV_RANK), F32)
        dkr = jnp.zeros((tm, QK_PAD), F32)
        for hd in range(HEADS):
            dqh = _mx(_unrope(dq_ref[hd], c, s1, s2))
            dcqn = dcqn + _dot(dqh, wqt_ref[hd])
            dwq_ref[hd] += _dot_tn(cqn, dqh)
            dkh = dk_ref[hd]
            dkr = dkr + dkh
            dkh = _mx(dkh)
            dckvn = dckvn + _dot(dkh, wkt_ref[hd])
            dwk_ref[hd] += _dot_tn(ckvn, dkh)
            dvh = _mx(dv_ref[hd])
            dckvn = dckvn + _dot(dvh, wvt_ref[hd])
            dwv_ref[hd] += _dot_tn(ckvn, dvh)
        dgq_ref[...] += jnp.sum(dcqn * cqh, axis=0, keepdims=True)
        dgkv_ref[...] += jnp.sum(dckvn * ckvh, axis=0, keepdims=True)
        gd = dcqn * gq
        dh_ref[:, 0:Q_RANK] = rq * (gd - cqh * jnp.mean(gd * cqh, axis=-1, keepdims=True))
        gd = dckvn * gkv
        dh_ref[:, Q_RANK:Q_RANK + KV_RANK] = rkv * (gd - ckvh * jnp.mean(gd * ckvh, axis=-1, keepdims=True))
        dh_ref[:, Q_RANK + KV_RANK:Q_RANK + KV_RANK + QK_PAD] = _unrope(dkr, c, s1, s2) * mask_ref[...]

    full = lambda shp: pl.BlockSpec(shp, lambda i: (0,) * len(shp))
    row = lambda w: pl.BlockSpec((tm, w), lambda i: (i, 0))
    hrow = lambda w: pl.BlockSpec((HEADS, tm, w), lambda i: (0, i, 0))
    return pl.pallas_call(
        body, name="mla_prep_bwd",
        out_shape=(jax.ShapeDtypeStruct((s, 4 * LANES), F32),
                   jax.ShapeDtypeStruct((HEADS, Q_RANK, QK_PAD), F32),
                   jax.ShapeDtypeStruct((HEADS, KV_RANK, QK_PAD), F32),
                   jax.ShapeDtypeStruct((HEADS, KV_RANK, HEAD_DIM), F32),
                   jax.ShapeDtypeStruct((1, Q_RANK), F32),
                   jax.ShapeDtypeStruct((1, KV_RANK), F32)),
        grid=(s // tm,),
        in_specs=[row(4 * LANES), hrow(QK_PAD), hrow(QK_PAD), hrow(HEAD_DIM),
                  full((1, Q_RANK)), full((1, KV_RANK)),
                  full((HEADS, QK_PAD, Q_RANK)), full((HEADS, QK_PAD, KV_RANK)), full((HEADS, HEAD_DIM, KV_RANK)),
                  row(LANES), row(LANES), row(LANES), row(LANES)],
        out_specs=(row(4 * LANES), full((HEADS, Q_RANK, QK_PAD)), full((HEADS, KV_RANK, QK_PAD)),
                   full((HEADS, KV_RANK, HEAD_DIM)), full((1, Q_RANK)), full((1, KV_RANK))),
        compiler_params=_params("arbitrary"),
    )(h, dq, dk, dv, g_cq, g_ckv, wq_t, wk_t, wv_t, c_t, s1_t, s2_t, mask_t)


def _bdot(a, b, ca, cb):
    return lax.dot_general(a, b, (((ca,), (cb,)), ((0,), (0,))), preferred_element_type=F32)


def _causal_mask_t(t):
    kk = lax.broadcasted_iota(jnp.int32, (t, t), 0)
    qq = lax.broadcasted_iota(jnp.int32, (t, t), 1)
    return (qq >= kk)[None]


def _mla_attn_fwd(q, k, v_t, *, t, g, late=None):
    hds, s, _ = q.shape
    n = s // t
    n_groups = hds // g

    nl = 0 if late is None else len(late)

    def body(*refs):
        q_ref, k_ref, vt_ref = refs[:3]
        wp_refs = refs[3:3 + nl]
        o_ref, lse_ref = refs[3 + nl:5 + nl]
        wout_refs = refs[5 + nl:5 + 2 * nl]
        m_sc, l_sc, acc_sc = refs[5 + 2 * nl:8 + 2 * nl]
        hg, qi, ki = pl.program_id(0), pl.program_id(1), pl.program_id(2)
        if nl:
            send_sems, recv_sems = refs[8 + 2 * nl:]
            tail = jnp.logical_and(hg == n_groups - 1, qi == n - 1)
            _gather_in_steps(wp_refs, wout_refs, send_sems, recv_sems,
                             first=jnp.logical_and(hg == 0, jnp.logical_and(qi == 0, ki == 0)),
                             mid=jnp.logical_and(tail, ki == 0), last=jnp.logical_and(tail, ki == n - 1))

        @pl.when(ki == 0)
        def _():
            m_sc[...] = jnp.full_like(m_sc, NEG)
            l_sc[...] = jnp.zeros_like(l_sc)
            acc_sc[...] = jnp.zeros_like(acc_sc)

        def step(masked):
            sc = _bdot(k_ref[...], q_ref[...], 2, 2) * MLA_SCALE
            if masked:
                sc = jnp.where(_causal_mask_t(t), sc, NEG)
            m_prev = m_sc[...]
            m_new = jnp.maximum(m_prev, jnp.max(sc, axis=1, keepdims=True))
            p = jnp.exp(sc - m_new)
            a = jnp.exp(m_prev - m_new)
            l_sc[...] = a * l_sc[...] + jnp.sum(p, axis=1, keepdims=True)
            acc_sc[...] = a * acc_sc[...] + _bdot(vt_ref[...], _mx(p), 2, 1)
            m_sc[...] = m_new

        @pl.when(ki < qi)
        def _():
            step(False)

        @pl.when(ki == qi)
        def _():
            step(True)
            o_ref[...] = acc_sc[...] / l_sc[...]
            lse_ref[...] = m_sc[...] + jnp.log(l_sc[...])

    qspec = pl.BlockSpec((g, t, QK_PAD), lambda h, i, j: (h, i, 0))
    kspec = pl.BlockSpec((g, t, QK_PAD), lambda h, i, j: (h, jnp.minimum(i, j), 0))
    vspec = pl.BlockSpec((g, HEAD_DIM, t), lambda h, i, j: (h, 0, jnp.minimum(i, j)))
    out_shape = [jax.ShapeDtypeStruct((hds, HEAD_DIM, s), F32), jax.ShapeDtypeStruct((hds, 1, s), F32)]
    in_specs = [qspec, kspec, vspec]
    out_specs = [pl.BlockSpec((g, HEAD_DIM, t), lambda h, i, j: (h, 0, i)), pl.BlockSpec((g, 1, t), lambda h, i, j: (h, 0, i))]
    scratch = [pltpu.VMEM((g, 1, t), F32), pltpu.VMEM((g, 1, t), F32), pltpu.VMEM((g, HEAD_DIM, t), F32)]
    args = [q, k, v_t]
    if nl:
        out_shape += [jax.ShapeDtypeStruct((N_CHIPS,) + a.shape, a.dtype) for a in late]
        in_specs += [ANY] * nl
        out_specs += [ANY] * nl
        scratch += [pltpu.SemaphoreType.DMA((6 * nl,)), pltpu.SemaphoreType.DMA((6 * nl,))]
        args += list(late)
    return pl.pallas_call(
        body, name="mla_attn_fwd",
        out_shape=tuple(out_shape), grid=(n_groups, n, n),
        in_specs=in_specs, out_specs=tuple(out_specs), scratch_shapes=scratch,
        compiler_params=pltpu.CompilerParams(dimension_semantics=("arbitrary",) * 3, vmem_limit_bytes=VMEM_LIMIT_BYTES,
                                             has_side_effects=nl > 0),
    )(*args)


def _head_rowdot(a, b, *, tm):
    s, width = a.shape
    nh = width // HEAD_DIM

    def body(a_ref, b_ref, o_ref):
        prod = a_ref[...] * b_ref[...]
        for hd in range(nh):
            o_ref[:, hd:hd + 1] = jnp.sum(prod[:, hd * HEAD_DIM:(hd + 1) * HEAD_DIM], axis=-1, keepdims=True)

    return pl.pallas_call(
        body, name="head_rowdot",
        out_shape=jax.ShapeDtypeStruct((s, nh), F32),
        grid=(s // tm,),
        in_specs=[pl.BlockSpec((tm, width), lambda i: (i, 0))] * 2,
        out_specs=pl.BlockSpec((tm, nh), lambda i: (i, 0)),
        compiler_params=_params("parallel"),
    )(a, b)


def _mla_attn_bwd(q, k, v, do, lse, dd, *, t, g, early=()):
    hds, s, _ = q.shape
    n = s // t
    n_groups = hds // g
    ne = len(early)

    def body(*refs):
        q_ref, k_ref, v_ref, do_ref, lse_ref, dd_ref = refs[:6]
        ps_refs = refs[6:6 + ne]
        dq_ref, dk_ref, dv_ref = refs[6 + ne:9 + ne]
        ss_refs = refs[9 + ne:9 + 2 * ne]
        dq_sc, dk_sc, dv_sc = refs[9 + 2 * ne:12 + 2 * ne]
        hg, ki, qi = pl.program_id(0), pl.program_id(1), pl.program_id(2)
        if ne:
            send_sems, recv_sems = refs[12 + 2 * ne:]
            _exchange_in_steps(ps_refs, ss_refs, send_sems, recv_sems,
                               first=jnp.logical_and(hg == 0, jnp.logical_and(ki == 0, qi == 0)),
                               last=jnp.logical_and(hg == n_groups - 1, jnp.logical_and(ki == n - 1, qi == n - 1)))

        @pl.when(jnp.logical_and(ki == 0, qi == 0))
        def _():
            dq_sc[...] = jnp.zeros_like(dq_sc)

        @pl.when(qi == 0)
        def _():
            dk_sc[...] = jnp.zeros_like(dk_sc)
            dv_sc[...] = jnp.zeros_like(dv_sc)

        def step(masked):
            qb, kb, dob = q_ref[...], k_ref[...], do_ref[...]
            sc = _bdot(kb, qb, 2, 2) * MLA_SCALE
            if masked:
                sc = jnp.where(_causal_mask_t(t), sc, NEG)
            p = jnp.exp(sc - lse_ref[...])
            dv_sc[...] += _bdot(_mx(p), dob, 2, 1)
            dp = _bdot(v_ref[...], dob, 2, 2)
            ds = _mx(p * (dp - dd_ref[...]) * MLA_SCALE)
            dk_sc[...] += _bdot(ds, qb, 2, 1)
            dq_sc[qi] += _bdot(ds, kb, 1, 1)

        @pl.when(qi == ki)
        def _():
            step(True)

        @pl.when(qi > ki)
        def _():
            step(False)

        @pl.when(qi == n - 1)
        def _():
            dk_ref[...] = dk_sc[...]
            dv_ref[...] = dv_sc[...]

        @pl.when(jnp.logical_and(ki == n - 1, qi == n - 1))
        def _():
            for j in range(n):
                dq_ref[:, j * t:(j + 1) * t, :] = dq_sc[j]

    qs = lambda w: pl.BlockSpec((g, t, w), lambda h, j, i: (h, jnp.maximum(i, j), 0))
    ks = lambda w: pl.BlockSpec((g, t, w), lambda h, j, i: (h, j, 0))
    rowq = pl.BlockSpec((g, 1, t), lambda h, j, i: (h, 0, jnp.maximum(i, j)))
    scratch = [pltpu.VMEM((n, g, t, QK_PAD), F32), pltpu.VMEM((g, t, QK_PAD), F32), pltpu.VMEM((g, t, HEAD_DIM), F32)]
    if ne:
        scratch += [pltpu.SemaphoreType.DMA((3 * ne,)), pltpu.SemaphoreType.DMA((3 * ne,))]
    return pl.pallas_call(
        body, name="mla_attn_bwd",
        out_shape=(jax.ShapeDtypeStruct((hds, s, QK_PAD), F32), jax.ShapeDtypeStruct((hds, s, QK_PAD), F32),
                   jax.ShapeDtypeStruct((hds, s, HEAD_DIM), F32)) + tuple(jax.ShapeDtypeStruct(a.shape, a.dtype) for a in early),
        grid=(n_groups, n, n),
        in_specs=[qs(QK_PAD), ks(QK_PAD), ks(HEAD_DIM), qs(HEAD_DIM), rowq, rowq] + [ANY] * ne,
        out_specs=(pl.BlockSpec((g, s, QK_PAD), lambda h, j, i: (h, 0, 0)), ks(QK_PAD), ks(HEAD_DIM)) + (ANY,) * ne,
        scratch_shapes=scratch,
        compiler_params=pltpu.CompilerParams(dimension_semantics=("arbitrary",) * 3, vmem_limit_bytes=VMEM_LIMIT_BYTES,
                                             has_side_effects=ne > 0),
    )(q, k, v, do, lse, dd, *early)


def _perm(a, dil):
    if dil == 1:
        return a
    hds, s, e = a.shape
    return a.reshape(hds, s // dil, dil, e).transpose(0, 2, 1, 3).reshape(hds, s, e)


def _unperm(a, dil):
    if dil == 1:
        return a
    hds, s, e = a.shape
    return a.reshape(hds, dil, s // dil, e).transpose(0, 2, 1, 3).reshape(hds, s, e)


def _perm_row(a, dil):
    if dil == 1:
        return a
    hds, _, s = a.shape
    return a.reshape(hds, s // dil, dil).transpose(0, 2, 1).reshape(hds, 1, s)


def _unperm_row(a, dil):
    if dil == 1:
        return a
    hds, _, s = a.shape
    return a.reshape(hds, dil, s // dil).transpose(0, 2, 1).reshape(hds, 1, s)


def _dil_bias(dil):
    slopes = 2.0 ** (-8.0 * jnp.arange(1, HEADS + 1, dtype=F32) / HEADS)
    ik = jnp.arange(DIL_BLOCK)[:, None]
    iq = jnp.arange(DIL_BLOCK)[None, :]
    off_c = iq - ik
    off_p = iq - ik + DIL_BLOCK
    b_c = -slopes[:, None, None] * (off_c * dil).astype(F32)[None]
    b_p = -slopes[:, None, None] * (off_p * dil).astype(F32)[None]
    b_c = jnp.where((off_c >= 0)[None], b_c, NEG)
    b_p = jnp.where((off_p <= DIL_BLOCK)[None], b_p, NEG)
    return b_c, b_p


def _dil_fwd(q, k, v, dil, *, name):
    hds, s, e = q.shape
    blk = DIL_BLOCK
    nblk = s // blk
    nb = nblk // dil
    b_c, b_p = _dil_bias(dil)

    def body(q_ref, kc_ref, kp_ref, vc_ref, vp_ref, bc_ref, bp_ref, o_ref, lse_ref):
        b = pl.program_id(0)
        first = (b % nb) == 0
        qb = q_ref[...]
        s_c = _bdot(kc_ref[...], qb, 2, 2) * DIL_SCALE + bc_ref[...]
        s_p = jnp.where(first, NEG, _bdot(kp_ref[...], qb, 2, 2) * DIL_SCALE + bp_ref[...])
        m = jnp.maximum(jnp.max(s_c, axis=1, keepdims=True), jnp.max(s_p, axis=1, keepdims=True))
        p_c = jnp.exp(s_c - m)
        p_p = jnp.exp(s_p - m)
        l = jnp.sum(p_c, axis=1, keepdims=True) + jnp.sum(p_p, axis=1, keepdims=True)
        o = _bdot(_mx(p_c), vc_ref[...], 1, 1) + _bdot(_mx(p_p), vp_ref[...], 1, 1)
        o_ref[...] = o / jnp.swapaxes(l, 1, 2)
        lse_ref[...] = m + jnp.log(l)

    cur = lambda w: pl.BlockSpec((hds, blk, w), lambda b: (0, b, 0))
    prev = lambda w: pl.BlockSpec((hds, blk, w), lambda b: (0, jnp.maximum(b - 1, 0), 0))
    bias = pl.BlockSpec((hds, blk, blk), lambda b: (0, 0, 0))
    return pl.pallas_call(
        body, name=name,
        out_shape=(jax.ShapeDtypeStruct((hds, s, e), F32), jax.ShapeDtypeStruct((hds, 1, s), F32)),
        grid=(nblk,),
        in_specs=[cur(e), cur(e), prev(e), cur(e), prev(e), bias, bias],
        out_specs=(cur(e), pl.BlockSpec((hds, 1, blk), lambda b: (0, 0, b))),
        compiler_params=_params("parallel"),
    )(q, k, k, v, v, b_c, b_p)


def _dil_combine(os_, lses, *, ts):
    hds, s, e = os_[0].shape
    dils = [d for _, d in DIL_PAIRS]

    def body(o0, o1, o2, l0, l1, l2, o_ref, l_ref, sc1, sc2):
        _load_token_order(sc1, o1, dils[1], ts)
        _load_token_order(sc2, o2, dils[2], ts)
        a0, a1, a2 = l0[...], l1[...], l2[...]
        m = jnp.maximum(jnp.maximum(a0, a1), a2)
        e0, e1, e2 = jnp.exp(a0 - m), jnp.exp(a1 - m), jnp.exp(a2 - m)
        tot = e0 + e1 + e2
        col = lambda w: jnp.swapaxes(w, 1, 2)
        res = (col(e0 / tot) * o0[...] + col(e1 / tot) * sc1[...]) + col(e2 / tot) * sc2[...]
        for hd in range(hds):
            o_ref[:, hd * e:(hd + 1) * e] = res[hd]
        l_ref[...] = m + jnp.log(tot)

    _, specs = _residue_major_outs(s, ts, dils, F32)
    view = lambda a, d: a if d == 1 else a.reshape(hds, d, s // d, e)
    rspec = pl.BlockSpec((hds, 1, ts), lambda i: (0, 0, i))
    return pl.pallas_call(
        body, name="dil_combine",
        out_shape=(jax.ShapeDtypeStruct((s, hds * e), F32), jax.ShapeDtypeStruct((hds, 1, s), F32)),
        grid=(s // ts,),
        in_specs=list(specs) + [rspec] * 3,
        out_specs=(pl.BlockSpec((ts, hds * e), lambda i: (i, 0)), rspec),
        scratch_shapes=[pltpu.VMEM((hds, ts, e), F32), pltpu.VMEM((hds, ts, e), F32)],
        compiler_params=_params("parallel"),
    )(*[view(a, d) for a, d in zip(os_, dils)], *lses)


def _dil_bwd(q, k, v, do, lj, dd, dil, *, name):
    hds, s, e = q.shape
    blk = DIL_BLOCK
    nblk = s // blk
    nb = nblk // dil
    b_c, b_p = _dil_bias(dil)

    def body(q_ref, qn_ref, kc_ref, kp_ref, vc_ref, vp_ref, do_ref, don_ref, l_ref, ln_ref, d_ref, dn_ref,
             bc_ref, bp_ref, dq_ref, dk_ref, dv_ref):
        b = pl.program_id(0)
        first = (b % nb) == 0
        nxt = jnp.logical_and(b + 1 < nblk, ((b + 1) % nb) != 0)
        qb, kc, kp, vc, vp = q_ref[...], kc_ref[...], kp_ref[...], vc_ref[...], vp_ref[...]
        dob = _mx(do_ref[...])
        bc, bp = bc_ref[...], bp_ref[...]
        p_c = jnp.exp(_bdot(kc, qb, 2, 2) * DIL_SCALE + bc - l_ref[...])
        p_p = jnp.where(first, 0.0, jnp.exp(_bdot(kp, qb, 2, 2) * DIL_SCALE + bp - l_ref[...]))
        ds_c = _mx(p_c * (_bdot(vc, dob, 2, 2) - d_ref[...]) * DIL_SCALE)
        ds_p = _mx(p_p * (_bdot(vp, dob, 2, 2) - d_ref[...]) * DIL_SCALE)
        dq_ref[...] = _bdot(ds_c, kc, 1, 1) + _bdot(ds_p, kp, 1, 1)
        qn = qn_ref[...]
        donb = _mx(don_ref[...])
        p_n = jnp.where(nxt, jnp.exp(_bdot(kc, qn, 2, 2) * DIL_SCALE + bp - ln_ref[...]), 0.0)
        ds_n = _mx(p_n * (_bdot(vc, donb, 2, 2) - dn_ref[...]) * DIL_SCALE)
        dk_ref[...] = _bdot(ds_c, qb, 2, 1) + _bdot(ds_n, qn, 2, 1)
        dv_ref[...] = _bdot(_mx(p_c), dob, 2, 1) + _bdot(_mx(p_n), donb, 2, 1)

    cur = lambda w: pl.BlockSpec((hds, blk, w), lambda b: (0, b, 0))
    prev = lambda w: pl.BlockSpec((hds, blk, w), lambda b: (0, jnp.maximum(b - 1, 0), 0))
    nxt_ = lambda w: pl.BlockSpec((hds, blk, w), lambda b: (0, jnp.minimum(b + 1, nblk - 1), 0))
    rcur = pl.BlockSpec((hds, 1, blk), lambda b: (0, 0, b))
    rnxt = pl.BlockSpec((hds, 1, blk), lambda b: (0, 0, jnp.minimum(b + 1, nblk - 1)))
    bias = pl.BlockSpec((hds, blk, blk), lambda b: (0, 0, 0))
    out = jax.ShapeDtypeStruct((hds, s, e), F32)
    return pl.pallas_call(
        body, name=name,
        out_shape=(out, out, out),
        grid=(nblk,),
        in_specs=[cur(e), nxt_(e), cur(e), prev(e), cur(e), prev(e), cur(e), nxt_(e),
                  rcur, rnxt, rcur, rnxt, bias, bias],
        out_specs=(cur(e), cur(e), cur(e)),
        compiler_params=_params("parallel"),
    )(q, q, k, k, v, v, do, do, lj, lj, dd, dd, b_c, b_p)


def _add3(a, b, c, *, ts, name):
    hds, s, e = a.shape

    def body(a_ref, b_ref, c_ref, o_ref):
        o_ref[...] = (a_ref[...] + b_ref[...]) + c_ref[...]

    spec = pl.BlockSpec((hds, ts, e), lambda i: (0, i, 0))
    return pl.pallas_call(
        body, name=name,
        out_shape=jax.ShapeDtypeStruct((hds, s, e), F32),
        grid=(s // ts,),
        in_specs=[spec] * 3, out_specs=spec,
        compiler_params=_params("parallel"),
    )(a, b, c)


def _ln_fwd(z, g, b):
    mu = jnp.mean(z, axis=-1, keepdims=True)
    zc = z - mu
    var = jnp.mean(zc * zc, axis=-1, keepdims=True)
    rstd = lax.rsqrt(var + LN_EPS)
    xhat = zc * rstd
    return xhat * g + b, xhat, rstd


def _ln_bwd(dy, xhat, rstd, g):
    dxh = dy * g
    return rstd * (dxh - jnp.mean(dxh, axis=-1, keepdims=True) - xhat * jnp.mean(dxh * xhat, axis=-1, keepdims=True))


def _out_ln1(a_mla, a_dil, w_o, x, g, b, *, tm):
    s = x.shape[0]
    half = HEADS * HEAD_DIM

    def body(am_ref, ad_ref, w_ref, x_ref, g_ref, b_ref, x1_ref, xh_ref, r_ref):
        mix = _dot(_mx(am_ref[...]), w_ref[0:half, :]) + _dot(_mx(ad_ref[...]), w_ref[half:2 * half, :])
        z = DN_ALPHA * x_ref[...] + mix
        y, xhat, rstd = _ln_fwd(z, g_ref[...], b_ref[...])
        x1_ref[...] = y
        xh_ref[...] = xhat
        r_ref[...] = rstd

    row = lambda w: pl.BlockSpec((tm, w), lambda i: (i, 0))
    full = lambda shp: pl.BlockSpec(shp, lambda i: (0,) * len(shp))
    act = jax.ShapeDtypeStruct((s, D_MODEL), F32)
    return pl.pallas_call(
        body, name="out_ln1",
        out_shape=(act, act, jax.ShapeDtypeStruct((s, 1), F32)),
        grid=(s // tm,),
        in_specs=[row(half), row(half), full((D_MODEL, D_MODEL)), row(D_MODEL), full((1, D_MODEL)), full((1, D_MODEL))],
        out_specs=(row(D_MODEL), row(D_MODEL), row(1)),
        compiler_params=_params("parallel"),
    )(a_mla, a_dil, w_o, x, g, b)


def _down_ln2_loss(act, w_down, x1, g, b, target, *, tm):
    s = x1.shape[0]

    def body(a_ref, w_ref, x1_ref, g_ref, b_ref, t_ref, dz_ref, loss_ref, dg_ref, db_ref):
        i = pl.program_id(0)

        @pl.when(i == 0)
        def _():
            loss_ref[...] = jnp.zeros_like(loss_ref)
            dg_ref[...] = jnp.zeros_like(dg_ref)
            db_ref[...] = jnp.zeros_like(db_ref)

        gam = g_ref[...]
        z = DN_ALPHA * x1_ref[...] + _dot(a_ref[...], w_ref[...])
        y, xhat, rstd = _ln_fwd(z, gam, b_ref[...])
        err = y - t_ref[...]
        loss_ref[...] += 0.5 * jnp.sum(jnp.mean(err * err, axis=-1, keepdims=True))
        dy = err * (1.0 / D_MODEL)
        dg_ref[...] += jnp.sum(dy * xhat, axis=0, keepdims=True)
        db_ref[...] += jnp.sum(dy, axis=0, keepdims=True)
        dz_ref[...] = _ln_bwd(dy, xhat, rstd, gam)

    row = lambda w: pl.BlockSpec((tm, w), lambda i: (i, 0))
    full = lambda shp: pl.BlockSpec(shp, lambda i: (0,) * len(shp))
    vec = jax.ShapeDtypeStruct((1, D_MODEL), F32)
    return pl.pallas_call(
        body, name="down_ln2_loss",
        out_shape=(jax.ShapeDtypeStruct((s, D_MODEL), F32), jax.ShapeDtypeStruct((1, LANES), F32), vec, vec),
        grid=(s // tm,),
        in_specs=[row(D_FF), full((D_FF, D_MODEL)), row(D_MODEL), full((1, D_MODEL)), full((1, D_MODEL)), row(D_MODEL)],
        out_specs=(row(D_MODEL), full((1, LANES)), full((1, D_MODEL)), full((1, D_MODEL))),
        compiler_params=_params("arbitrary"),
    )(act, w_down, x1, g, b, target)


def _up_bwd_ln1(du_a, du_g, w_up_t, dz2, xhat1, rstd1, g, *, tm):
    s = dz2.shape[0]

    def body(dua_ref, dug_ref, wa_ref, wg_ref, dz2_ref, xh_ref, r_ref, g_ref, dz1_ref, dg_ref, db_ref):
        i = pl.program_id(0)

        @pl.when(i == 0)
        def _():
            dg_ref[...] = jnp.zeros_like(dg_ref)
            db_ref[...] = jnp.zeros_like(db_ref)

        dx1 = DN_ALPHA * dz2_ref[...] + (_dot(dua_ref[...], wa_ref[...]) + _dot(dug_ref[...], wg_ref[...]))
        xhat = xh_ref[...]
        dg_ref[...] += jnp.sum(dx1 * xhat, axis=0, keepdims=True)
        db_ref[...] += jnp.sum(dx1, axis=0, keepdims=True)
        dz1_ref[...] = _ln_bwd(dx1, xhat, r_ref[...], g_ref[...])

    row = lambda w: pl.BlockSpec((tm, w), lambda i: (i, 0))
    full = lambda shp: pl.BlockSpec(shp, lambda i: (0,) * len(shp))
    vec = jax.ShapeDtypeStruct((1, D_MODEL), F32)
    return pl.pallas_call(
        body, name="up_bwd_ln1",
        out_shape=(jax.ShapeDtypeStruct((s, D_MODEL), F32), vec, vec),
        grid=(s // tm,),
        in_specs=[row(D_FF), row(D_FF),
                  pl.BlockSpec((D_FF, D_MODEL), lambda i: (0, 0)), pl.BlockSpec((D_FF, D_MODEL), lambda i: (1, 0)),
                  row(D_MODEL), row(D_MODEL), row(1), full((1, D_MODEL))],
        out_specs=(row(D_MODEL), full((1, D_MODEL)), full((1, D_MODEL))),
        compiler_params=_params("arbitrary"),
    )(du_a, du_g, w_up_t, w_up_t, dz2, xhat1, rstd1, g)


GELU_C = math.sqrt(2.0 / math.pi)


def _gelu(x):
    cdf = 0.5 * (1.0 + jnp.tanh(GELU_C * (x + 0.044715 * (x * x * x))))
    return x * cdf


def _gelu_grad(x):
    t = jnp.tanh(GELU_C * (x + 0.044715 * (x * x * x)))
    return 0.5 * (1.0 + t) + 0.5 * x * (1.0 - t * t) * (GELU_C * (1.0 + 3.0 * 0.044715 * (x * x)))


def _shift_down(u, halo):
    r1, r2 = pltpu.roll(u, 1, 0), pltpu.roll(u, 2, 0)
    row = lax.broadcasted_iota(jnp.int32, (SUBLANES, u.shape[1]), 0)
    h7, h6 = halo[7:8, :], halo[6:7, :]
    head1 = jnp.where(row == 0, h7, r1[:SUBLANES])
    head2 = jnp.where(row == 0, h6, jnp.where(row == 1, h7, r2[:SUBLANES]))
    return (jnp.concatenate([head1, r1[SUBLANES:]], axis=0), jnp.concatenate([head2, r2[SUBLANES:]], axis=0))


def _shift_up(d, nxt):
    t = d.shape[0]
    r1, r2 = pltpu.roll(d, t - 1, 0), pltpu.roll(d, t - 2, 0)
    row = lax.broadcasted_iota(jnp.int32, (SUBLANES, d.shape[1]), 0)
    n0, n1 = nxt[0:1, :], nxt[1:2, :]
    last = t - SUBLANES
    tail1 = jnp.where(row == SUBLANES - 1, n0, r1[last:])
    tail2 = jnp.where(row == SUBLANES - 1, n1, jnp.where(row == SUBLANES - 2, n0, r2[last:]))
    return (jnp.concatenate([r1[:last], tail1], axis=0), jnp.concatenate([r2[:last], tail2], axis=0))


def _conv(u, s1, s2, w, b):
    return ((b + w[0:1, :] * s2) + w[1:2, :] * s1) + w[2:3, :] * u


def _up_gate_fwd(x1, w_up, conv_w, conv_b, *, tm, tn):
    s = x1.shape[0]
    nj = D_FF // tn
    hb = tm // SUBLANES

    def body(x_ref, xh_ref, wua_ref, wug_ref, wa_ref, wg_ref, ba_ref, bg_ref,
             ua_ref, ug_ref, o_ref, a_ref, ge_ref, gd_ref):
        keep = pl.program_id(1) > 0
        xb, xh = _mx(x_ref[...]), _mx(xh_ref[...])
        wua, wug = wua_ref[...], wug_ref[...]
        ua, ug = _dot(xb, wua), _dot(xb, wug)
        ha = jnp.where(keep, _dot(xh, wua), 0.0)
        hg = jnp.where(keep, _dot(xh, wug), 0.0)
        ua_ref[...] = ua
        ug_ref[...] = ug
        a = _conv(ua, *_shift_down(ua, ha), wa_ref[...], ba_ref[...])
        g = _conv(ug, *_shift_down(ug, hg), wg_ref[...], bg_ref[...])
        ge = _gelu(g)
        o_ref[...] = (ge * a).astype(o_ref.dtype)
        a_ref[...] = a
        ge_ref[...] = ge
        gd_ref[...] = _gelu_grad(g)

    main = lambda off: pl.BlockSpec((tm, tn), lambda j, i: (i, j + off))
    wspec = lambda r, off: pl.BlockSpec((r, tn), lambda j, i: (0, j + off))
    if w_up.ndim == 3:
        wu = lambda off: pl.BlockSpec((None, D_MODEL, tn), lambda j, i: (j + off, 0, 0))
    else:
        wu = lambda off: pl.BlockSpec((D_MODEL, tn), lambda j, i: (0, j + off))
    keep_f32 = jax.ShapeDtypeStruct((s, D_FF), F32)
    return pl.pallas_call(
        body, name="up_gate_fwd",
        out_shape=(keep_f32, keep_f32, jax.ShapeDtypeStruct((s, D_FF), MXU_DTYPE), keep_f32, keep_f32, keep_f32),
        grid=(nj, s // tm),
        in_specs=[pl.BlockSpec((tm, D_MODEL), lambda j, i: (i, 0)),
                  pl.BlockSpec((SUBLANES, D_MODEL), lambda j, i: (jnp.maximum(i * hb - 1, 0), 0)),
                  wu(0), wu(nj), wspec(3, 0), wspec(3, nj), wspec(1, 0), wspec(1, nj)],
        out_specs=(main(0),) * 6,
        compiler_params=_params("parallel", "parallel"),
    )(x1, x1, w_up, w_up, conv_w, conv_w, conv_b, conv_b)


def _gate_bwd(u_a, u_g, dz2, w_down_t, a, ge, gd, conv_w, *, tm, tn):
    s = u_a.shape[0]
    nj = D_FF // tn
    ni = s // tm
    hb = tm // SUBLANES

    def body(ua_ref, ug_ref, ha_ref, hg_ref, dz_ref, dzn_ref, wd_ref, a_ref, an_ref, ge_ref, gen_ref, gd_ref, gdn_ref,
             wa_ref, wg_ref, dua_ref, dug_ref, dwa_ref, dwg_ref, dba_ref, dbg_ref):
        i = pl.program_id(1)

        @pl.when(i == 0)
        def _():
            for r in (dwa_ref, dwg_ref, dba_ref, dbg_ref):
                r[...] = jnp.zeros_like(r)

        wa, wg = wa_ref[...], wg_ref[...]
        ua, ug = ua_ref[...], ug_ref[...]
        ha = jnp.where(i > 0, ha_ref[...], 0.0)
        hg = jnp.where(i > 0, hg_ref[...], 0.0)
        sa1, sa2 = _shift_down(ua, ha)
        sg1, sg2 = _shift_down(ug, hg)
        wd = wd_ref[...]
        d = _dot(_mx(dz_ref[...]), wd)
        dya = d * ge_ref[...]
        dyg = d * a_ref[...] * gd_ref[...]
        dn = jnp.where(i < ni - 1, _dot(_mx(dzn_ref[...]), wd), 0.0)
        dya_n = dn * gen_ref[...]
        dyg_n = dn * an_ref[...] * gdn_ref[...]
        da1, da2 = _shift_up(dya, dya_n)
        dg1, dg2 = _shift_up(dyg, dyg_n)
        dua_ref[...] = (wa[2:3, :] * dya + wa[1:2, :] * da1 + wa[0:1, :] * da2).astype(dua_ref.dtype)
        dug_ref[...] = (wg[2:3, :] * dyg + wg[1:2, :] * dg1 + wg[0:1, :] * dg2).astype(dug_ref.dtype)
        ssum = lambda v: jnp.sum(v, axis=0, keepdims=True)
        dwa_ref[...] += jnp.concatenate([ssum(dya * sa2), ssum(dya * sa1), ssum(dya * ua)], axis=0)
        dwg_ref[...] += jnp.concatenate([ssum(dyg * sg2), ssum(dyg * sg1), ssum(dyg * ug)], axis=0)
        dba_ref[...] += ssum(dya)
        dbg_ref[...] += ssum(dyg)

    main = pl.BlockSpec((tm, tn), lambda j, i: (i, j))
    halo = pl.BlockSpec((SUBLANES, tn), lambda j, i: (jnp.maximum(i * hb - 1, 0), j))
    next_row = lambda j, i: jnp.minimum((i + 1) * hb, s // SUBLANES - 1)
    nxt = pl.BlockSpec((SUBLANES, tn), lambda j, i: (next_row(j, i), j))
    wspec = lambda r, off: pl.BlockSpec((r, tn), lambda j, i: (0, j + off))
    return pl.pallas_call(
        body, name="gate_bwd",
        out_shape=(jax.ShapeDtypeStruct((s, D_FF), MXU_DTYPE), jax.ShapeDtypeStruct((s, D_FF), MXU_DTYPE),
                   jax.ShapeDtypeStruct((3, D_FF), F32), jax.ShapeDtypeStruct((3, D_FF), F32),
                   jax.ShapeDtypeStruct((1, D_FF), F32), jax.ShapeDtypeStruct((1, D_FF), F32)),
        grid=(nj, ni),
        in_specs=[main, main, halo, halo,
                  pl.BlockSpec((tm, D_MODEL), lambda j, i: (i, 0)),
                  pl.BlockSpec((SUBLANES, D_MODEL), lambda j, i: (next_row(j, i), 0)),
                  pl.BlockSpec((D_MODEL, tn), lambda j, i: (0, j))]
        + [main, nxt] * 3 + [wspec(3, 0), wspec(3, nj)],
        out_specs=(main, main, wspec(3, 0), wspec(3, 0), wspec(1, 0), wspec(1, 0)),
        compiler_params=_params("parallel", "arbitrary"),
    )(u_a, u_g, u_a, u_g, dz2, dz2, w_down_t, a, a, ge, ge, gd, gd, conv_w, conv_w)


def _prep_weights(w_in, w_uq, w_uk, w_uv, w_o, w_up, w_down):
    return {**_prep_weights_first(w_in, w_uq, w_uk, w_uv), **_prep_weights_late(w_o, w_up, w_down)}


def _prep_weights_late(w_o, w_up, w_down):
    w_o, w_up, w_down = _mx(w_o), _mx(w_up), _mx(w_down)
    w_up_t = w_up.T if w_up.ndim == 2 else w_up.transpose(0, 2, 1).reshape(2 * D_FF, D_MODEL)
    return dict(w_o=w_o, w_o_t=w_o.T, w_up=w_up, w_up_t=w_up_t, w_down=w_down, w_down_t=w_down.T)


def _prep_weights_first(w_in, w_uq, w_uk, w_uv):
    c = lambda a: a.astype(MXU_DTYPE)
    w_in = c(w_in)
    z = lambda w: jnp.zeros((D_MODEL, w), MXU_DTYPE)
    r0 = Q_RANK + KV_RANK
    w_in_ext = jnp.concatenate([w_in[:, :r0], z(NOPE), w_in[:, r0:r0 + ROPE], z(32), w_in[:, r0 + ROPE:]], axis=1)
    wq = jnp.pad(c(w_uq).transpose(1, 0, 2), ((0, 0), (0, 0), (0, QK_PAD - NOPE - ROPE)))
    wk = jnp.pad(c(w_uk).transpose(1, 0, 2), ((0, 0), (0, 0), (0, QK_PAD - NOPE)))
    wv = c(w_uv).transpose(1, 0, 2)
    t3 = lambda a: a.transpose(0, 2, 1)
    return dict(w_in=w_in_ext, w_in_t=w_in_ext.T, wq=wq, wq_t=t3(wq), wk=wk, wk_t=t3(wk), wv=wv, wv_t=t3(wv))


def _local_step(x, target, w, g_cq, g_ckv, ln1_g, ln1_b, conv_w, conv_b, ln2_g, ln2_b, comm=None):
    s = x.shape[0]
    tabs = _rope_tables(s)
    r2 = lambda a: a.reshape(1, -1)
    heads = lambda a: a.reshape(s, HEADS, HEAD_DIM).transpose(1, 0, 2)
    unheads = lambda a: a.transpose(1, 0, 2).reshape(s, HEADS * HEAD_DIM)
    cb = r2(conv_b)
    dils = [d for _, d in DIL_PAIRS]

    h, qp, kp, vp = _in_proj(x, w["w_in"], tm=256)
    q, k, v, v_t = _mla_prep_fwd(h, r2(g_cq), r2(g_ckv), w["wq"], w["wk"], w["wv"], w["wv_t"], tabs, tm=256)
    if comm is None:
        o_mla_t, lse_mla = _mla_attn_fwd(q, k, v_t, t=512, g=HEADS)
    else:
        o_mla_t, lse_mla, *gathered = _mla_attn_fwd(q, k, v_t, t=512, g=HEADS, late=comm["late"])
        w = {**w, **comm["finish"](gathered)}
    o_bs, lse_bs = [], []
    for i, d in enumerate(dils):
        o_b, l_b = _dil_fwd(qp[i], kp[i], vp[i], d, name=f"dil_fwd_{d}")
        o_bs.append(o_b)
        lse_bs.append(_unperm_row(l_b, d))
    o_dil, lj = _dil_combine(o_bs, lse_bs, ts=512)
    o_mla = o_mla_t.transpose(2, 0, 1).reshape(s, HEADS * HEAD_DIM)
    x1, xhat1, rstd1 = _out_ln1(o_mla, o_dil, w["w_o"], x, r2(ln1_g), r2(ln1_b), tm=256)
    u_a, u_g, act, conv_a, gelu_g, gelu_dg = _up_gate_fwd(x1, w["w_up"], conv_w, cb, tm=256, tn=1408)
    dz2, loss, dg2, db2 = _down_ln2_loss(act, w["w_down"], x1, r2(ln2_g), r2(ln2_b), target, tm=256)

    dw_down = _mm_tn(act, dz2, name="dw_down", tm=1408, tn=D_MODEL, ts=512)
    du_a, du_g, dcw_a, dcw_g, dcb_a, dcb_g = _gate_bwd(u_a, u_g, dz2, w["w_down_t"], conv_a, gelu_g, gelu_dg, conv_w,
                                                       tm=256, tn=1408)
    dz1, dg1, db1 = _up_bwd_ln1(du_a, du_g, w["w_up_t"], dz2, xhat1, rstd1, r2(ln1_g), tm=256)
    dw_up = jnp.concatenate([_mm_tn(x1, du_a, name="dw_up_a", tm=D_MODEL, tn=1408, ts=512),
                             _mm_tn(x1, du_g, name="dw_up_g", tm=D_MODEL, tn=1408, ts=512)], axis=1)
    do_mla, do_dil, dd_all = _attn_bwd_heads(dz1, w["w_o_t"], o_mla, o_dil, tm=256)
    dw_o = jnp.concatenate([_mm_tn(o_mla, dz1, name="dw_o_mla", tm=512, tn=D_MODEL, ts=512),
                            _mm_tn(o_dil, dz1, name="dw_o_dil", tm=512, tn=D_MODEL, ts=512)], axis=0)
    dd_all = dd_all.T
    dd_mla, dd_dil = dd_all[:HEADS].reshape(HEADS, 1, s), dd_all[HEADS:].reshape(HEADS, 1, s)
    early = () if comm is None else tuple(comm["halve"]([("w_up", dw_up), ("w_down", dw_down)]))
    dq, dk, dv, *early_slots = _mla_attn_bwd(q, k, v, do_mla, lse_mla, dd_mla, t=512, g=4, early=early)
    parts = []
    for i, d in enumerate(dils):
        parts.append(_dil_bwd(qp[i], kp[i], vp[i], do_dil[i], _perm_row(lj, d), _perm_row(dd_dil, d), d, name=f"dil_bwd_{d}"))
    dh_dil = _dil_merge(parts, ts=512)
    dh_mla, dwq, dwk, dwv, dgq, dgkv = _mla_prep_bwd(h, dq, dk, dv, r2(g_cq), r2(g_ckv),
                                                     w["wq_t"], w["wk_t"], w["wv_t"], tabs, tm=256)
    mla_w = 4 * LANES
    w_in_t = w["w_in_t"]
    grad_x = _mm_nn(dh_mla, w_in_t[:mla_w], name="in_bwd_mla", tm=512, tn=D_MODEL, tk=mla_w, add=dz1, add_scale=DN_ALPHA)
    grad_x = _mm_nn(dh_dil, w_in_t[mla_w:], name="in_bwd_dil", tm=512, tn=D_MODEL, tk=512, add=grad_x)
    dw_mla = _mm_tn(x, dh_mla, name="dw_in_mla", tm=D_MODEL, tn=mla_w, ts=512)
    dw_dil = _mm_tn(x, dh_dil, name="dw_in_dil", tm=D_MODEL, tn=512, ts=512)
    r0 = Q_RANK + KV_RANK
    grads = dict(
        w_in=jnp.concatenate([dw_mla[:, :r0], dw_mla[:, r0 + NOPE:r0 + NOPE + ROPE], dw_dil], axis=1),
        g_cq=dgq[0], g_ckv=dgkv[0],
        w_uq=dwq[:, :, :NOPE + ROPE].transpose(1, 0, 2),
        w_uk=dwk[:, :, :NOPE].transpose(1, 0, 2),
        w_uv=dwv.transpose(1, 0, 2),
        w_o=dw_o, ln1_g=dg1[0], ln1_b=db1[0], w_up=dw_up,
        conv_w=jnp.concatenate([dcw_a, dcw_g], axis=1), conv_b=jnp.concatenate([dcb_a, dcb_g], axis=1)[0],
        w_down=dw_down, ln2_g=dg2[0], ln2_b=db2[0])
    if comm is not None:
        grads["early"] = (early, tuple(early_slots))
    return loss[0, 0], grad_x, grads


N_CHIPS = 4
SHARDED = ("w_in", "w_uq", "w_o", "w_up", "conv_w", "w_down")
COL_SHARDED = ("w_in", "w_up", "conv_w")
SHARD_SHAPE = dict(w_in=(D_MODEL, IN_WIDTH // 4), w_uq=(Q_RANK // 4, HEADS, NOPE + ROPE), w_o=(D_MODEL // 4, D_MODEL),
                   w_up=(D_MODEL, 2 * D_FF // 4), conv_w=(3, 2 * D_FF // 4), w_down=(D_FF // 4, D_MODEL))
SMALL = ("g_cq", "g_ckv", "w_uk", "w_uv", "ln1_g", "ln1_b", "conv_b", "ln2_g", "ln2_b")
SMALL_SHAPE = dict(g_cq=(Q_RANK,), g_ckv=(KV_RANK,), w_uk=(KV_RANK, HEADS, NOPE), w_uv=(KV_RANK, HEADS, HEAD_DIM),
                   ln1_g=(D_MODEL,), ln1_b=(D_MODEL,), conv_b=(2 * D_FF,), ln2_g=(D_MODEL,), ln2_b=(D_MODEL,))
BIG = ("w_in", "w_uq", "w_o", "w_up", "w_down")
BIG_2D = dict(w_in=(D_MODEL, IN_WIDTH // 4), w_uq=(Q_RANK // 4, HEADS * (NOPE + ROPE)), w_o=(D_MODEL // 4, D_MODEL),
              w_up=(D_MODEL, 2 * D_FF // 4), w_down=(D_FF // 4, D_MODEL))
SMALL_G = SMALL + ("conv_w",)
SMALL_G_SHAPE = {**SMALL_SHAPE, "conv_w": (3, 2 * D_FF)}
SMALL_U_SHAPE = {**SMALL_SHAPE, "conv_w": (3, 2 * D_FF // 4)}


def _size(shape):
    return math.prod(shape)


def _padded_rows(n_elems, mult):
    return -(-n_elems // (LANES * mult)) * mult


SHARD_ROWS = {n: _padded_rows(_size(SHARD_SHAPE[n]), SUBLANES) for n in SHARDED}
R_SMALL = -(-sum(_size(SMALL_G_SHAPE[n]) for n in SMALL_G) // (LANES * LANES)) * LANES
GATHER_FIRST = ("w_in", "w_uq")
GATHER_LATE = ("w_o", "w_up", "w_down")
REDUCED_EARLY = ("w_up", "w_down")
REDUCED_LAST = ("w_in", "w_uq", "w_o")


def _rows(a, rows=None):
    flat = a.reshape(-1)
    rows = -(-flat.shape[0] // LANES) if rows is None else rows
    return jnp.pad(flat, (0, rows * LANES - flat.shape[0])).reshape(rows, LANES)


def _blocked(name, g):
    r, c = BIG_2D[name]
    a = g.reshape(r, N_CHIPS, c).transpose(1, 0, 2) if name in COL_SHARDED else g.reshape(N_CHIPS, r, c)
    return a.reshape(N_CHIPS, 2, r // 2, c)


def _pack_flat(t, names):
    return _rows(jnp.concatenate([t[n].astype(F32).reshape(-1) for n in names]), R_SMALL)


def _unpack_flat(buf, names, shapes):
    flat, out, r = buf.reshape(-1), {}, 0
    for n in names:
        out[n] = flat[r:r + _size(shapes[n])].reshape(shapes[n])
        r += _size(shapes[n])
    return out


def _from_chip_blocks(name, blocks):
    shp = SHARD_SHAPE[name]
    a = blocks.reshape(N_CHIPS, -1)[:, :_size(shp)].reshape((N_CHIPS,) + shp)
    if name in COL_SHARDED:
        return a.transpose(1, 0, 2).reshape(shp[0], N_CHIPS * shp[1])
    return a.reshape((N_CHIPS * shp[0],) + shp[1:])


ANY = pl.BlockSpec(memory_space=pl.ANY)
COMM_PARAMS = pltpu.CompilerParams(has_side_effects=True)


def _coords():
    return lax.axis_index("x"), lax.axis_index("y"), lax.axis_index("c")


def _other_chips(x, y):
    return [(1 - x, y), (x, 1 - y), (1 - x, 1 - y)]


def _remote(src, dst, send_sems, recv_sems, k, to):
    return pltpu.make_async_remote_copy(src_ref=src, dst_ref=dst, send_sem=send_sems.at[k], recv_sem=recv_sems.at[k],
                                        device_id=to, device_id_type=MESH)


def _gather_in_steps(wp_refs, wout_refs, send_sems, recv_sems, *, first, mid, last):
    x, y, c = _coords()
    me = 2 * x + y
    sib = (x, y, 1 - c)
    chips = _other_chips(x, y)
    n = len(wp_refs)
    pairs = [(j, t, px, py) for j, (px, py) in enumerate(chips) for t in range(n)]
    ici = [_remote(wp_refs[t].at[c], wout_refs[t].at[me, c], send_sems, recv_sems, j * n + t, (px, py, c))
           for j, t, px, py in pairs]
    fwd = [_remote(wout_refs[t].at[2 * px + py, c], wout_refs[t].at[2 * px + py, c], send_sems, recv_sems, (3 + j) * n + t, sib)
           for j, t, px, py in pairs]

    @pl.when(first)
    def _():
        for cp in ici:
            cp.start()

    @pl.when(mid)
    def _():
        for i, (j, t, px, py) in enumerate(pairs):
            _remote(wp_refs[t].at[c], wout_refs[t].at[2 * px + py, c], send_sems, recv_sems, j * n + t, (px, py, c)).wait_recv()
            fwd[i].start()

    @pl.when(last)
    def _():
        for j, t, px, py in pairs:
            k = 2 * px + py
            _remote(wout_refs[t].at[k, 1 - c], wout_refs[t].at[k, 1 - c], send_sems, recv_sems, (3 + j) * n + t, sib).wait_recv()
        for cp in ici + fwd:
            cp.wait_send()


def _exchange_in_steps(ps_refs, ss_refs, send_sems, recv_sems, *, first, last):
    x, y, c = _coords()
    me = 2 * x + y
    chips = _other_chips(x, y)
    n = len(ps_refs)
    sends = [_remote(ps_refs[t].at[2 * px + py], ss_refs[t].at[me], send_sems, recv_sems, j * n + t, (px, py, c))
             for j, (px, py) in enumerate(chips) for t in range(n)]

    @pl.when(first)
    def _():
        for cp in sends:
            cp.start()

    @pl.when(last)
    def _():
        for j, (px, py) in enumerate(chips):
            for t in range(n):
                _remote(ps_refs[t].at[me], ss_refs[t].at[2 * px + py], send_sems, recv_sems, j * n + t, (px, py, c)).wait_recv()
        for cp in sends:
            cp.wait_send()


def _gather_weights(wp, cwp):
    def body(wp_ref, cw_ref, wout_ref, cwout_ref, send_sems, recv_sems):
        x, y, c = _coords()
        me = 2 * x + y
        sib = (x, y, 1 - c)
        chips = _other_chips(x, y)
        sends = [_remote(wp_ref.at[c], wout_ref.at[me, c], send_sems, recv_sems, j, (px, py, c))
                 for j, (px, py) in enumerate(chips)]
        sends += [_remote(cw_ref, cwout_ref.at[me], send_sems, recv_sems, 3 + j, (px, py, c))
                  for j, (px, py) in enumerate(chips)]
        for cp in sends:
            cp.start()
        for j, (px, py) in enumerate(chips):
            k = 2 * px + py
            _remote(wp_ref.at[c], wout_ref.at[k, c], send_sems, recv_sems, j, (px, py, c)).wait_recv()
            fwd = _remote(wout_ref.at[k, c], wout_ref.at[k, c], send_sems, recv_sems, 6 + j, sib)
            fwd.start()
            sends.append(fwd)
        for j, (px, py) in enumerate(chips):
            k = 2 * px + py
            _remote(cw_ref, cwout_ref.at[k], send_sems, recv_sems, 3 + j, (px, py, c)).wait_recv()
            _remote(wout_ref.at[k, 1 - c], wout_ref.at[k, 1 - c], send_sems, recv_sems, 6 + j, sib).wait_recv()
        for cp in sends:
            cp.wait_send()

    return pl.pallas_call(
        body, name="gather_weights",
        out_shape=(jax.ShapeDtypeStruct((N_CHIPS,) + wp.shape, wp.dtype), jax.ShapeDtypeStruct((N_CHIPS,) + cwp.shape, cwp.dtype)),
        in_specs=[ANY, ANY], out_specs=(ANY, ANY),
        scratch_shapes=[pltpu.SemaphoreType.DMA((9,)), pltpu.SemaphoreType.DMA((9,))],
        compiler_params=COMM_PARAMS,
    )(wp, cwp)


def _exchange_sibling_halves(gs, whole, *, name):
    n, nw = len(gs), len(whole)

    def body(*refs):
        gs_refs, wh_refs = refs[:n], refs[n:n + nw]
        os_refs, ow_refs = refs[n + nw:2 * n + nw], refs[2 * n + nw:2 * (n + nw)]
        send_sems, recv_sems = refs[2 * (n + nw):]
        x, y, c = _coords()
        sib = (x, y, 1 - c)
        cps = [_remote(gs_refs[t].at[k, 1 - c], os_refs[t].at[k], send_sems, recv_sems, t * N_CHIPS + k, sib)
               for t in range(n) for k in range(N_CHIPS)]
        cps += [_remote(wh_refs[t], ow_refs[t], send_sems, recv_sems, n * N_CHIPS + t, sib) for t in range(nw)]
        for cp in cps:
            cp.start()
        for cp in cps:
            cp.wait_recv()
        for cp in cps:
            cp.wait_send()

    n_sem = n * N_CHIPS + nw
    return pl.pallas_call(
        body, name=name,
        out_shape=tuple(jax.ShapeDtypeStruct((N_CHIPS,) + a.shape[2:], F32) for a in gs)
        + tuple(jax.ShapeDtypeStruct(a.shape, F32) for a in whole),
        in_specs=[ANY] * (n + nw), out_specs=(ANY,) * (n + nw),
        scratch_shapes=[pltpu.SemaphoreType.DMA((n_sem,)), pltpu.SemaphoreType.DMA((n_sem,))],
        compiler_params=COMM_PARAMS,
    )(*gs, *whole)


def _exchange_chips(ps, pr):
    n = len(ps)

    def body(*refs):
        ps_refs, pr_ref, ss_refs, sr_ref = refs[:n], refs[n], refs[n + 1:2 * n + 1], refs[2 * n + 1]
        send_sems, recv_sems = refs[2 * n + 2:]
        x, y, c = _coords()
        me = 2 * x + y
        chips = _other_chips(x, y)
        sends = []
        for j, (px, py) in enumerate(chips):
            to = (px, py, c)
            for t in range(n):
                sends.append(_remote(ps_refs[t].at[2 * px + py], ss_refs[t].at[me], send_sems, recv_sems, j * (n + 1) + t, to))
            sends.append(_remote(pr_ref, sr_ref.at[me], send_sems, recv_sems, j * (n + 1) + n, to))
        for cp in sends:
            cp.start()
        for j, (px, py) in enumerate(chips):
            k, to = 2 * px + py, (px, py, c)
            for t in range(n):
                _remote(ps_refs[t].at[me], ss_refs[t].at[k], send_sems, recv_sems, j * (n + 1) + t, to).wait_recv()
            _remote(pr_ref, sr_ref.at[k], send_sems, recv_sems, j * (n + 1) + n, to).wait_recv()
        for cp in sends:
            cp.wait_send()

    n_sem = 3 * (n + 1)
    return pl.pallas_call(
        body, name="exchange_chips",
        out_shape=tuple(jax.ShapeDtypeStruct(a.shape, a.dtype) for a in ps) + (jax.ShapeDtypeStruct((N_CHIPS,) + pr.shape, F32),),
        in_specs=[ANY] * (n + 1), out_specs=(ANY,) * (n + 1),
        scratch_shapes=[pltpu.SemaphoreType.DMA((n_sem,)), pltpu.SemaphoreType.DMA((n_sem,))],
        compiler_params=COMM_PARAMS,
    )(*ps, pr)


def _exchange_sibling_result(gh):
    n = len(gh)

    def body(*refs):
        gh_refs, out_refs, (send_sems, recv_sems) = refs[:n], refs[n:2 * n], refs[2 * n:]
        x, y, c = _coords()
        cps = [_remote(gh_refs[t], out_refs[t], send_sems, recv_sems, t, (x, y, 1 - c)) for t in range(n)]
        for cp in cps:
            cp.start()
        for cp in cps:
            cp.wait_recv()
        for cp in cps:
            cp.wait_send()

    return pl.pallas_call(
        body, name="exchange_sibling_result",
        out_shape=tuple(jax.ShapeDtypeStruct(a.shape, F32) for a in gh),
        in_specs=[ANY] * n, out_specs=(ANY,) * n,
        scratch_shapes=[pltpu.SemaphoreType.DMA((n,)), pltpu.SemaphoreType.DMA((n,))],
        compiler_params=COMM_PARAMS,
    )(*gh)


def _add_own_half(gs, recv, c_arr, *, name):
    _, rows, cols = recv.shape

    def body(c_ref, a_ref, b_ref, o_ref):
        o_ref[0] = (a_ref[0, 0] + b_ref[0]).astype(o_ref.dtype)

    return pl.pallas_call(
        body, name=name,
        out_shape=jax.ShapeDtypeStruct(recv.shape, GRAD_WIRE_DTYPE),
        grid_spec=pltpu.PrefetchScalarGridSpec(
            num_scalar_prefetch=1, grid=(N_CHIPS,),
            in_specs=[pl.BlockSpec((1, 1, rows, cols), lambda k, c_ref: (k, c_ref[0], 0, 0)),
                      pl.BlockSpec((1, rows, cols), lambda k, c_ref: (k, 0, 0))],
            out_specs=pl.BlockSpec((1, rows, cols), lambda k, c_ref: (k, 0, 0))),
        compiler_params=_params("parallel"),
    )(c_arr, gs, recv)


def _add2(a, b, *, name):
    def body(a_ref, b_ref, o_ref):
        o_ref[...] = a_ref[...] + b_ref[...]

    return pl.pallas_call(body, name=name, out_shape=jax.ShapeDtypeStruct(a.shape, F32))(a, b)


def _sum_slots(slots, *, tr, name):
    _, r, c = slots.shape

    def body(s_ref, o_ref):
        f = lambda k: s_ref[k].astype(F32)
        o_ref[...] = ((f(0) + f(1)) + f(2)) + f(3)

    return pl.pallas_call(
        body, name=name,
        out_shape=jax.ShapeDtypeStruct((r, c), F32),
        grid=(r // tr,),
        in_specs=[pl.BlockSpec((N_CHIPS, tr, c), lambda i: (0, i, 0))],
        out_specs=pl.BlockSpec((tr, c), lambda i: (i, 0)),
        compiler_params=_params("parallel"),
    )(slots)


def _adamw(w, g, m, v, *, tr, name):
    r, cols = w.shape

    def body(w_ref, g_ref, m_ref, v_ref, d_ref, nm_ref, nv_ref):
        g_ = g_ref[...]
        m_ = ADAM_B1 * m_ref[...] + (1.0 - ADAM_B1) * g_
        v_ = ADAM_B2 * v_ref[...] + (1.0 - ADAM_B2) * (g_ * g_)
        m_hat = m_ / (1.0 - ADAM_B1 ** ADAM_STEP)
        v_hat = v_ / (1.0 - ADAM_B2 ** ADAM_STEP)
        d_ref[...] = -ADAM_LR * (m_hat / (jnp.sqrt(v_hat) + ADAM_EPS) + ADAM_WD * w_ref[...])
        nm_ref[...] = m_
        nv_ref[...] = v_

    spec = pl.BlockSpec((tr, cols), lambda i: (i, 0))
    out = jax.ShapeDtypeStruct((r, cols), F32)
    return pl.pallas_call(
        body, name=name, out_shape=(out, out, out), grid=(r // tr,),
        in_specs=[spec] * 4, out_specs=(spec,) * 3,
        compiler_params=_params("parallel"),
    )(w, g, m, v)


WEIGHTS = ("w_in", "g_cq", "g_ckv", "w_uq", "w_uk", "w_uv", "w_o", "ln1_g", "ln1_b", "w_up", "conv_w", "conv_b",
           "w_down", "ln2_g", "ln2_b")


def kernel(x, w_in, g_cq, g_ckv, w_uq, w_uk, w_uv, w_o, ln1_g, ln1_b, w_up, conv_w, conv_b, w_down, ln2_g, ln2_b, loss_target, m_w_in, m_g_cq, m_g_ckv, m_w_uq, m_w_uk, m_w_uv, m_w_o, m_ln1_g, m_ln1_b, m_w_up, m_conv_w, m_conv_b, m_w_down, m_ln2_g, m_ln2_b, v_w_in, v_g_cq, v_g_ckv, v_w_uq, v_w_uk, v_w_uv, v_w_o, v_ln1_g, v_ln1_b, v_w_up, v_conv_w, v_conv_b, v_w_down, v_ln2_g, v_ln2_b):
    wts = dict(zip(WEIGHTS, (w_in, g_cq, g_ckv, w_uq, w_uk, w_uv, w_o, ln1_g, ln1_b, w_up, conv_w, conv_b, w_down, ln2_g, ln2_b)))
    mom = dict(zip(WEIGHTS, (m_w_in, m_g_cq, m_g_ckv, m_w_uq, m_w_uk, m_w_uv, m_w_o, m_ln1_g, m_ln1_b, m_w_up, m_conv_w, m_conv_b, m_w_down, m_ln2_g, m_ln2_b)))
    var = dict(zip(WEIGHTS, (v_w_in, v_g_cq, v_g_ckv, v_w_uq, v_w_uk, v_w_uv, v_w_o, v_ln1_g, v_ln1_b, v_w_up, v_conv_w, v_conv_b, v_w_down, v_ln2_g, v_ln2_b)))

    me = 2 * lax.axis_index("x") + lax.axis_index("y")
    my_c = lax.axis_index("c")
    c_arr = my_c.astype(jnp.int32).reshape(1)
    own = lambda slots, mine: lax.dynamic_update_index_in_dim(slots, mine, me, 0)

    def pack(names):
        return jnp.concatenate([_rows(_mx(wts[n]), SHARD_ROWS[n]) for n in names], axis=0).reshape(2, -1, LANES)

    def unpack(names, gathered, mine):
        buf, full, r = own(gathered, mine).reshape(N_CHIPS, -1, LANES), {}, 0
        for n in names:
            full[n] = _from_chip_blocks(n, buf[:, r:r + SHARD_ROWS[n]])
            r += SHARD_ROWS[n]
        return full

    wp_first = pack(GATHER_FIRST)
    cwp = _rows(conv_w, SHARD_ROWS["conv_w"])
    gathered, cwfull = _gather_weights(wp_first, cwp)
    full = unpack(GATHER_FIRST, gathered, wp_first)
    conv_w_full = _from_chip_blocks("conv_w", own(cwfull, cwp))
    w = _prep_weights_first(full["w_in"], full["w_uq"], w_uk, w_uv)
    late_halves = [_mx(wts[n]).reshape(2, BIG_2D[n][0] // 2, BIG_2D[n][1]) for n in GATHER_LATE]

    def finish(gathered_late):
        w_o_b, w_up_b, w_down_b = (own(a, mine).reshape((N_CHIPS,) + BIG_2D[n])
                                   for a, mine, n in zip(gathered_late, late_halves, GATHER_LATE))
        return _prep_weights_late(w_o_b.reshape(D_MODEL, D_MODEL), w_up_b, w_down_b.reshape(D_FF, D_MODEL))

    def halve(named, whole=(), tag="early"):
        gb = [_blocked(n, a) for n, a in named]
        recv = _exchange_sibling_halves(gb, list(whole), name=f"exchange_sibling_halves_{tag}")
        ps = [_add_own_half(gb[i], recv[i], c_arr, name=f"add_half_{n}") for i, (n, _) in enumerate(named)]
        return ps + [_add2(a, recv[len(gb) + i], name=f"add_whole_{tag}_{i}") for i, a in enumerate(whole)]

    comm = dict(late=late_halves, finish=finish, halve=halve)
    loss, grad_x, g = _local_step(x[0], loss_target[0], w, g_cq, g_ckv, ln1_g, ln1_b, conv_w_full, conv_b, ln2_g, ln2_b, comm=comm)

    ps_early, slots_early = g.pop("early")
    g["loss"] = loss.reshape(1)
    *ps_rest, pr = halve([(n, g[n]) for n in REDUCED_LAST], whole=[_pack_flat(g, SMALL_G + ("loss",))], tag="last")
    *slots_rest, slots_r = _exchange_chips(ps_rest, pr)
    ps = {**dict(zip(REDUCED_LAST, ps_rest)), **dict(zip(REDUCED_EARLY, ps_early))}
    slots = {**dict(zip(REDUCED_LAST, slots_rest)), **dict(zip(REDUCED_EARLY, slots_early))}
    slots = [own(slots[n], lax.dynamic_index_in_dim(ps[n], me, 0, keepdims=False)) for n in BIG]
    slots_r = own(slots_r, pr)
    g_half = [_sum_slots(slots[i], tr=slots[i].shape[1] // 2, name=f"sum_chips_{n}") for i, n in enumerate(BIG)]
    g_small = _unpack_flat(_sum_slots(slots_r, tr=R_SMALL, name="sum_chips_small"), SMALL_G + ("loss",),
                           {**SMALL_G_SHAPE, "loss": (1,)})
    loss = g_small.pop("loss")[0]
    g_other = _exchange_sibling_result(g_half)
    grads = {n: jnp.where(my_c == 0, jnp.concatenate([g_half[i], g_other[i]]), jnp.concatenate([g_other[i], g_half[i]]))
             for i, n in enumerate(BIG)}
    g_small["conv_w"] = lax.dynamic_slice_in_dim(g_small["conv_w"], me * SHARD_SHAPE["conv_w"][1], SHARD_SHAPE["conv_w"][1], 1)
    grads.update(g_small)

    res = {}
    for n in BIG:
        as2d = lambda a: a.reshape(BIG_2D[n])
        d, m, v = _adamw(as2d(wts[n]), grads[n], as2d(mom[n]), as2d(var[n]), tr=BIG_2D[n][0] // 4, name=f"adamw_{n}")
        res[n] = [a.reshape(SHARD_SHAPE[n]) for a in (grads[n], d, m, v)]
    flat = lambda t: _pack_flat(t, SMALL_G)
    dmv = _adamw(flat(wts), flat(g_small), flat(mom), flat(var), tr=R_SMALL, name="adamw_small")
    dmv = [_unpack_flat(a, SMALL_G, SMALL_U_SHAPE) for a in dmv]
    for n in SMALL_G:
        res[n] = [g_small[n]] + [t[n] for t in dmv]
    outs = [res[n][j] for j in range(4) for n in WEIGHTS]
    return (loss, grad_x[None], *outs)
```

```python
import functools
import math

import jax
import jax.numpy as jnp
from jax import lax
from jax.experimental import pallas as pl
from jax.experimental.pallas import tpu as pltpu

F32 = jnp.float32
MXU_DTYPE = jnp.bfloat16
GRAD_WIRE_DTYPE = jnp.bfloat16
NEG = -1e30

D_MODEL = 1024
HEADS = 8
HEAD_DIM = 64
Q_RANK = 256
KV_RANK = 128
NOPE = 64
ROPE = 32
QK_PAD = 128
IN_WIDTH = 1952
IN_EXT = 2048
D_FF = 2816
DIL_PAIRS = ((128, 1), (512, 4), (2048, 16))
DIL_BLOCK = 128
ROPE_THETA = 10000.0
DN_ALPHA = 2.0 ** 0.25
LN_EPS = 1e-5
RMS_EPS = 1e-6
MLA_SCALE = 1.0 / math.sqrt(NOPE + ROPE)
LOG2_E = math.log2(math.e)
DIL_SCALE = 1.0 / math.sqrt(HEAD_DIM)

ADAM_LR = 0.001
ADAM_B1 = 0.9
ADAM_B2 = 0.999
ADAM_EPS = 1e-08
ADAM_WD = 0.01
ADAM_STEP = 10

LANES = 128
SUBLANES = 8
VMEM_LIMIT_BYTES = 56 * 1024 * 1024

MESH = pl.DeviceIdType.MESH


def _params(*sem):
    return pltpu.CompilerParams(dimension_semantics=sem, vmem_limit_bytes=VMEM_LIMIT_BYTES)


def _dot(a, b):
    return jnp.dot(a, b, preferred_element_type=F32)


def _dot_nt(a, b):
    return lax.dot_general(a, b, (((1,), (1,)), ((), ())), preferred_element_type=F32)


def _dot_tn(a, b):
    return lax.dot_general(a, b, (((0,), (0,)), ((), ())), preferred_element_type=F32)


def _mx(a):
    return a.astype(MXU_DTYPE)


def _mm_nn(a, b, *, name, tm, tn, tk, out_dtype=F32, add=None, add_scale=1.0):
    m, kdim = a.shape
    blocked = b.ndim == 3
    n = b.shape[0] * b.shape[2] if blocked else b.shape[1]
    nk = kdim // tk

    def body(*refs):
        if add is None:
            a_ref, b_ref, o_ref, acc = refs
        else:
            a_ref, b_ref, c_ref, o_ref, acc = refs
        k = pl.program_id(2)

        @pl.when(k == 0)
        def _():
            acc[...] = jnp.zeros_like(acc)

        acc[...] += _dot(_mx(a_ref[...]), _mx(b_ref[...]))

        @pl.when(k == nk - 1)
        def _():
            r = acc[...]
            if add is not None:
                r = r + add_scale * c_ref[...]
            o_ref[...] = r.astype(out_dtype)

    b_spec = (pl.BlockSpec((None, tk, tn), lambda i, j, k: (j, k, 0)) if blocked
              else pl.BlockSpec((tk, tn), lambda i, j, k: (k, j)))
    in_specs = [pl.BlockSpec((tm, tk), lambda i, j, k: (i, k)), b_spec]
    args = [a, b]
    if add is not None:
        in_specs.append(pl.BlockSpec((tm, tn), lambda i, j, k: (i, j)))
        args.append(add)
    return pl.pallas_call(
        body, name=name,
        out_shape=jax.ShapeDtypeStruct((m, n), out_dtype),
        grid=(m // tm, n // tn, nk),
        in_specs=in_specs,
        out_specs=pl.BlockSpec((tm, tn), lambda i, j, k: (i, j)),
        scratch_shapes=[pltpu.VMEM((tm, tn), F32)],
        compiler_params=_params("parallel", "parallel", "arbitrary"),
    )(*args)


def _mm_tn(a, b, *, name, tm, tn, ts, out_dtype=F32):
    s, m = a.shape
    n = b.shape[1]
    ns = s // ts

    def body(a_ref, b_ref, o_ref, acc):
        k = pl.program_id(2)

        @pl.when(k == 0)
        def _():
            acc[...] = jnp.zeros_like(acc)

        acc[...] += _dot_tn(_mx(a_ref[...]), _mx(b_ref[...]))

        @pl.when(k == ns - 1)
        def _():
            o_ref[...] = acc[...].astype(out_dtype)

    return pl.pallas_call(
        body, name=name,
        out_shape=jax.ShapeDtypeStruct((m, n), out_dtype),
        grid=(m // tm, n // tn, ns),
        in_specs=[pl.BlockSpec((ts, tm), lambda i, j, k: (k, i)),
                  pl.BlockSpec((ts, tn), lambda i, j, k: (k, j))],
        out_specs=pl.BlockSpec((tm, tn), lambda i, j, k: (i, j)),
        scratch_shapes=[pltpu.VMEM((tm, tn), F32)],
        compiler_params=_params("parallel", "parallel", "arbitrary"),
    )(a, b)


def _in_proj(x, w_in_ext, *, tm):
    s = x.shape[0]
    mla_w = 4 * LANES
    dil_w = HEADS * HEAD_DIM
    dils = [d for _, d in DIL_PAIRS]

    def body(x_ref, w_ref, h_ref, *rest):
        outs, sc = rest[:-1], rest[-1]
        xb = _mx(x_ref[...])
        h_ref[...] = _dot(xb, w_ref[:, 0:mla_w])
        for j in range(3):
            part = _dot(xb, w_ref[:, mla_w + j * dil_w:mla_w + (j + 1) * dil_w])
            for hd in range(HEADS):
                sc[hd] = part[:, hd * HEAD_DIM:(hd + 1) * HEAD_DIM]
            for b, d in enumerate(dils):
                _store_residue_major(outs[3 * j + b], sc, d, tm)

    shapes, specs = _residue_major_outs(s, tm, dils, MXU_DTYPE)
    res = pl.pallas_call(
        body, name="in_proj",
        out_shape=(jax.ShapeDtypeStruct((s, mla_w), F32),) + shapes * 3,
        grid=(s // tm,),
        in_specs=[pl.BlockSpec((tm, D_MODEL), lambda i: (i, 0)), pl.BlockSpec((D_MODEL, IN_EXT), lambda i: (0, 0))],
        out_specs=(pl.BlockSpec((tm, mla_w), lambda i: (i, 0)),) + specs * 3,
        scratch_shapes=[pltpu.VMEM((HEADS, tm, HEAD_DIM), F32)],
        compiler_params=_params("parallel"),
    )(x, w_in_ext)
    hm = lambda a: a.reshape(HEADS, s, HEAD_DIM)
    return res[0], [hm(a) for a in res[1:4]], [hm(a) for a in res[4:7]], [hm(a) for a in res[7:10]]


def _residue_major_outs(s, tm, dils, dtype):
    shapes, specs = [], []
    for d in dils:
        if d == 1:
            shapes.append(jax.ShapeDtypeStruct((HEADS, s, HEAD_DIM), dtype))
            specs.append(pl.BlockSpec((HEADS, tm, HEAD_DIM), lambda i: (0, i, 0)))
        else:
            shapes.append(jax.ShapeDtypeStruct((HEADS, d, s // d, HEAD_DIM), dtype))
            specs.append(pl.BlockSpec((HEADS, d, tm // d, HEAD_DIM), lambda i: (0, 0, i, 0)))
    return tuple(shapes), tuple(specs)


def _store_residue_major(o_ref, src_ref, d, tm):
    if d == 1:
        o_ref[...] = src_ref[...].astype(o_ref.dtype)
    else:
        for r in range(d):
            o_ref[:, r] = src_ref[:, pl.ds(r, tm // d, stride=d), :].astype(o_ref.dtype)


def _load_token_order(dst_ref, src_ref, d, tm, accumulate=False):
    if d == 1:
        dst_ref[...] = dst_ref[...] + src_ref[...] if accumulate else src_ref[...]
    else:
        for r in range(d):
            rows = pl.ds(r, tm // d, stride=d)
            dst_ref[:, rows, :] = dst_ref[:, rows, :] + src_ref[:, r] if accumulate else src_ref[:, r]


def _attn_bwd_heads(dz1, w_o_t, a_mla, a_dil, *, tm):
    s = dz1.shape[0]
    half = HEADS * HEAD_DIM
    dils = [d for _, d in DIL_PAIRS]

    def body(dz_ref, w_ref, am_ref, ad_ref, dom_ref, dd_ref, *dod_refs):
        dzb = _mx(dz_ref[...])
        for j, (a_ref, o_ref) in enumerate(((am_ref, dom_ref), (ad_ref, dod_refs[0]))):
            da = _dot(dzb, w_ref[:, j * half:(j + 1) * half])
            prod = da * a_ref[...]
            for hd in range(HEADS):
                sl = slice(hd * HEAD_DIM, (hd + 1) * HEAD_DIM)
                o_ref[hd] = da[:, sl].astype(o_ref.dtype)
                dd_ref[:, j * HEADS + hd:j * HEADS + hd + 1] = jnp.sum(prod[:, sl], axis=-1, keepdims=True)
        for b, d in enumerate(dils[1:]):
            _store_residue_major(dod_refs[1 + b], dod_refs[0], d, tm)

    hspec = pl.BlockSpec((HEADS, tm, HEAD_DIM), lambda i: (0, i, 0))
    row = lambda w: pl.BlockSpec((tm, w), lambda i: (i, 0))
    shapes, specs = _residue_major_outs(s, tm, dils, F32)
    do_mla, dd, *do_dil = pl.pallas_call(
        body, name="attn_bwd_heads",
        out_shape=(jax.ShapeDtypeStruct((HEADS, s, HEAD_DIM), MXU_DTYPE), jax.ShapeDtypeStruct((s, 2 * HEADS), F32)) + shapes,
        grid=(s // tm,),
        in_specs=[row(D_MODEL), pl.BlockSpec((D_MODEL, D_MODEL), lambda i: (0, 0)), row(half), row(half)],
        out_specs=(hspec, row(2 * HEADS)) + specs,
        compiler_params=_params("parallel"),
    )(dz1, w_o_t, a_mla, a_dil)
    return do_mla, [a.reshape(HEADS, s, HEAD_DIM) for a in do_dil], dd


def _dil_merge(parts, *, ts):
    hds, s, e = parts[0][0].shape
    dils = [d for _, d in DIL_PAIRS]

    def body(*refs):
        o_ref, sc = refs[9], refs[10]
        for j in range(3):
            for b, d in enumerate(dils):
                _load_token_order(sc, refs[3 * b + j], d, ts, accumulate=b > 0)
            tot = sc[...]
            for hd in range(hds):
                col = j * hds * e + hd * e
                o_ref[:, col:col + e] = tot[hd].astype(o_ref.dtype)

    _, specs = _residue_major_outs(s, ts, dils, F32)
    view = lambda a, d: a if d == 1 else a.reshape(hds, d, s // d, e)
    return pl.pallas_call(
        body, name="dil_merge",
        out_shape=jax.ShapeDtypeStruct((s, 3 * hds * e), MXU_DTYPE),
        grid=(s // ts,),
        in_specs=[specs[b] for b in range(3) for _ in range(3)],
        out_specs=pl.BlockSpec((ts, 3 * hds * e), lambda i: (i, 0)),
        scratch_shapes=[pltpu.VMEM((hds, ts, e), F32)],
        compiler_params=_params("parallel"),
    )(*[view(parts[b][j], dils[b]) for b in range(3) for j in range(3)])


def _rope_tables(s):
    half = ROPE // 2
    freqs = ROPE_THETA ** (-jnp.arange(half, dtype=F32) / half)
    ang = jnp.arange(s).astype(F32)[:, None] * freqs[None, :]
    cos, sin = jnp.cos(ang), jnp.sin(ang)
    z = lambda w: jnp.zeros((s, w), F32)
    c = jnp.concatenate([jnp.ones((s, NOPE), F32), cos, cos, z(32)], axis=1)
    s1 = jnp.concatenate([z(NOPE + half), sin, z(32)], axis=1)
    s2 = jnp.concatenate([z(NOPE), -sin, z(half + 32)], axis=1)
    mask = jnp.concatenate([z(NOPE), jnp.ones((s, ROPE), F32), z(32)], axis=1)
    return c, s1, s2, mask


def _rope(x, c, s1, s2):
    return x * c + pltpu.roll(x, 16, 1) * s1 + pltpu.roll(x, LANES - 16, 1) * s2


def _unrope(dy, c, s1, s2):
    return dy * c + pltpu.roll(dy * s1, LANES - 16, 1) + pltpu.roll(dy * s2, 16, 1)


def _rms(x):
    r = lax.rsqrt(jnp.mean(x * x, axis=-1, keepdims=True) + RMS_EPS)
    return x * r, r


def _mla_prep_fwd(h, g_cq, g_ckv, wq, wk, wv, wv_t, tabs, *, tm):
    s = h.shape[0]
    c_t, s1_t, s2_t, _ = tabs

    def body(h_ref, gq_ref, gkv_ref, wq_ref, wk_ref, wv_ref, wvt_ref, c_ref, s1_ref, s2_ref,
             q_ref, k_ref, v_ref, vt_ref):
        cq = h_ref[:, 0:Q_RANK]
        ckv = h_ref[:, Q_RANK:Q_RANK + KV_RANK]
        kr = h_ref[:, Q_RANK + KV_RANK:Q_RANK + KV_RANK + QK_PAD]
        c, s1, s2 = c_ref[...], s1_ref[...], s2_ref[...]
        cqn = _mx(_rms(cq)[0] * gq_ref[...])
        ckvn = _mx(_rms(ckv)[0] * gkv_ref[...])
        kr_rot = _rope(kr, c, s1, s2)
        for hd in range(HEADS):
            q_ref[hd] = _rope(_dot(cqn, wq_ref[hd]), c, s1, s2).astype(q_ref.dtype)
            k_ref[hd] = (_dot(ckvn, wk_ref[hd]) + kr_rot).astype(k_ref.dtype)
            v_ref[hd] = _dot(ckvn, wv_ref[hd]).astype(v_ref.dtype)
            vt_ref[hd] = _dot_nt(wvt_ref[hd], ckvn).astype(vt_ref.dtype)

    full = lambda shp: pl.BlockSpec(shp, lambda i: (0,) * len(shp))
    row = lambda w: pl.BlockSpec((tm, w), lambda i: (i, 0))
    return pl.pallas_call(
        body, name="mla_prep_fwd",
        out_shape=(jax.ShapeDtypeStruct((HEADS, s, QK_PAD), MXU_DTYPE),
                   jax.ShapeDtypeStruct((HEADS, s, QK_PAD), MXU_DTYPE),
                   jax.ShapeDtypeStruct((HEADS, s, HEAD_DIM), MXU_DTYPE),
                   jax.ShapeDtypeStruct((HEADS, HEAD_DIM, s), MXU_DTYPE)),
        grid=(s // tm,),
        in_specs=[row(4 * LANES), full((1, Q_RANK)), full((1, KV_RANK)),
                  full((HEADS, Q_RANK, QK_PAD)), full((HEADS, KV_RANK, QK_PAD)), full((HEADS, KV_RANK, HEAD_DIM)),
                  full((HEADS, HEAD_DIM, KV_RANK)), row(LANES), row(LANES), row(LANES)],
        out_specs=(pl.BlockSpec((HEADS, tm, QK_PAD), lambda i: (0, i, 0)),
                   pl.BlockSpec((HEADS, tm, QK_PAD), lambda i: (0, i, 0)),
                   pl.BlockSpec((HEADS, tm, HEAD_DIM), lambda i: (0, i, 0)),
                   pl.BlockSpec((HEADS, HEAD_DIM, tm), lambda i: (0, 0, i))),
        compiler_params=_params("parallel"),
    )(h, g_cq, g_ckv, wq, wk, wv, wv_t, c_t, s1_t, s2_t)


def _mla_prep_bwd(h, dq, dk, dv, g_cq, g_ckv, wq_t, wk_t, wv_t, tabs, *, tm):
    s = h.shape[0]
    c_t, s1_t, s2_t, mask_t = tabs

    def body(h_ref, dq_ref, dk_ref, dv_ref, gq_ref, gkv_ref, wqt_ref, wkt_ref, wvt_ref,
             c_ref, s1_ref, s2_ref, mask_ref, dh_ref, dwq_ref, dwk_ref, dwv_ref, dgq_ref, dgkv_ref):
        i = pl.program_id(0)

        @pl.when(i == 0)
        def _():
            dwq_ref[...] = jnp.zeros_like(dwq_ref)
            dwk_ref[...] = jnp.zeros_like(dwk_ref)
            dwv_ref[...] = jnp.zeros_like(dwv_ref)
            dgq_ref[...] = jnp.zeros_like(dgq_ref)
            dgkv_ref[...] = jnp.zeros_like(dgkv_ref)

        cq = h_ref[:, 0:Q_RANK]
        ckv = h_ref[:, Q_RANK:Q_RANK + KV_RANK]
        c, s1, s2 = c_ref[...], s1_ref[...], s2_ref[...]
        cqh, rq = _rms(cq)
        ckvh, rkv = _rms(ckv)
        gq, gkv = gq_ref[...], gkv_ref[...]
        cqn = _mx(cqh * gq)
        ckvn = _mx(ckvh * gkv)
        dcqn = jnp.zeros((tm, Q_RANK), F32)
        dckvn = jnp.zeros((tm, KV_RANK), F32)
        dkr = jnp.zeros((tm, QK_PAD), F32)
        for hd in range(HEADS):
            dqh = _mx(_unrope(dq_ref[hd], c, s1, s2))
            dcqn = dcqn + _dot(dqh, wqt_ref[hd])
            dwq_ref[hd] += _dot_tn(cqn, dqh)
            dkh = dk_ref[hd]
            dkr = dkr + dkh
            dkh = _mx(dkh)
            dckvn = dckvn + _dot(dkh, wkt_ref[hd])
            dwk_ref[hd] += _dot_tn(ckvn, dkh)
            dvh = _mx(dv_ref[hd])
            dckvn = dckvn + _dot(dvh, wvt_ref[hd])
            dwv_ref[hd] += _dot_tn(ckvn, dvh)
        dgq_ref[...] += jnp.sum(dcqn * cqh, axis=0, keepdims=True)
        dgkv_ref[...] += jnp.sum(dckvn * ckvh, axis=0, keepdims=True)
        gd = dcqn * gq
        dh_ref[:, 0:Q_RANK] = rq * (gd - cqh * jnp.mean(gd * cqh, axis=-1, keepdims=True))
        gd = dckvn * gkv
        dh_ref[:, Q_RANK:Q_RANK + KV_RANK] = rkv * (gd - ckvh * jnp.mean(gd * ckvh, axis=-1, keepdims=True))
        dh_ref[:, Q_RANK + KV_RANK:Q_RANK + KV_RANK + QK_PAD] = _unrope(dkr, c, s1, s2) * mask_ref[...]

    full = lambda shp: pl.BlockSpec(shp, lambda i: (0,) * len(shp))
    row = lambda w: pl.BlockSpec((tm, w), lambda i: (i, 0))
    hrow = lambda w: pl.BlockSpec((HEADS, tm, w), lambda i: (0, i, 0))
    return pl.pallas_call(
        body, name="mla_prep_bwd",
        out_shape=(jax.ShapeDtypeStruct((s, 4 * LANES), F32),
                   jax.ShapeDtypeStruct((HEADS, Q_RANK, QK_PAD), F32),
                   jax.ShapeDtypeStruct((HEADS, KV_RANK, QK_PAD), F32),
                   jax.ShapeDtypeStruct((HEADS, KV_RANK, HEAD_DIM), F32),
                   jax.ShapeDtypeStruct((1, Q_RANK), F32),
                   jax.ShapeDtypeStruct((1, KV_RANK), F32)),
        grid=(s // tm,),
        in_specs=[row(4 * LANES), hrow(QK_PAD), hrow(QK_PAD), hrow(HEAD_DIM),
                  full((1, Q_RANK)), full((1, KV_RANK)),
                  full((HEADS, QK_PAD, Q_RANK)), full((HEADS, QK_PAD, KV_RANK)), full((HEADS, HEAD_DIM, KV_RANK)),
                  row(LANES), row(LANES), row(LANES), row(LANES)],
        out_specs=(row(4 * LANES), full((HEADS, Q_RANK, QK_PAD)), full((HEADS, KV_RANK, QK_PAD)),
                   full((HEADS, KV_RANK, HEAD_DIM)), full((1, Q_RANK)), full((1, KV_RANK))),
        compiler_params=_params("arbitrary"),
    )(h, dq, dk, dv, g_cq, g_ckv, wq_t, wk_t, wv_t, c_t, s1_t, s2_t, mask_t)


def _bdot(a, b, ca, cb):
    return lax.dot_general(a, b, (((ca,), (cb,)), ((0,), (0,))), preferred_element_type=F32)


def _causal_mask_t(t):
    kk = lax.broadcasted_iota(jnp.int32, (t, t), 0)
    qq = lax.broadcasted_iota(jnp.int32, (t, t), 1)
    return (qq >= kk)[None]


def _mla_attn_fwd(q, k, v_t, *, t, g, late=None):
    hds, s, _ = q.shape
    n = s // t
    n_groups = hds // g

    nl = 0 if late is None else len(late)

    def body(*refs):
        q_ref, k_ref, vt_ref = refs[:3]
        wp_refs = refs[3:3 + nl]
        o_ref, lse_ref = refs[3 + nl:5 + nl]
        wout_refs = refs[5 + nl:5 + 2 * nl]
        m_sc, l_sc, acc_sc = refs[5 + 2 * nl:8 + 2 * nl]
        hg, qi, ki = pl.program_id(0), pl.program_id(1), pl.program_id(2)
        if nl:
            send_sems, recv_sems = refs[8 + 2 * nl:]
            tail = jnp.logical_and(hg == n_groups - 1, qi == n - 1)
            _gather_in_steps(wp_refs, wout_refs, send_sems, recv_sems,
                             first=jnp.logical_and(hg == 0, jnp.logical_and(qi == 0, ki == 0)),
                             mid=jnp.logical_and(tail, ki == 0), last=jnp.logical_and(tail, ki == n - 1))

        @pl.when(ki == 0)
        def _():
            m_sc[...] = jnp.full_like(m_sc, NEG)
            l_sc[...] = jnp.zeros_like(l_sc)
            acc_sc[...] = jnp.zeros_like(acc_sc)

        def step(masked):
            sc = _bdot(k_ref[...], q_ref[...], 2, 2)
            if masked:
                sc = jnp.where(_causal_mask_t(t), sc, NEG)
            m_prev = m_sc[...]
            m_new = jnp.maximum(m_prev, jnp.max(sc, axis=1, keepdims=True))
            p = jnp.exp2((sc - m_new) * (MLA_SCALE * LOG2_E))
            a = jnp.exp2((m_prev - m_new) * (MLA_SCALE * LOG2_E))
            l_sc[...] = a * l_sc[...] + jnp.sum(p, axis=1, keepdims=True)
            acc_sc[...] = a * acc_sc[...] + _bdot(vt_ref[...], _mx(p), 2, 1)
            m_sc[...] = m_new

        @pl.when(ki < qi)
        def _():
            step(False)

        @pl.when(ki == qi)
        def _():
            step(True)
            o_ref[...] = acc_sc[...] / l_sc[...]
            lse_ref[...] = m_sc[...] * MLA_SCALE + jnp.log(l_sc[...])

    qspec = pl.BlockSpec((g, t, QK_PAD), lambda h, i, j: (h, i, 0))
    kspec = pl.BlockSpec((g, t, QK_PAD), lambda h, i, j: (h, jnp.minimum(i, j), 0))
    vspec = pl.BlockSpec((g, HEAD_DIM, t), lambda h, i, j: (h, 0, jnp.minimum(i, j)))
    out_shape = [jax.ShapeDtypeStruct((hds, HEAD_DIM, s), F32), jax.ShapeDtypeStruct((hds, 1, s), F32)]
    in_specs = [qspec, kspec, vspec]
    out_specs = [pl.BlockSpec((g, HEAD_DIM, t), lambda h, i, j: (h, 0, i)), pl.BlockSpec((g, 1, t), lambda h, i, j: (h, 0, i))]
    scratch = [pltpu.VMEM((g, 1, t), F32), pltpu.VMEM((g, 1, t), F32), pltpu.VMEM((g, HEAD_DIM, t), F32)]
    args = [q, k, v_t]
    if nl:
        out_shape += [jax.ShapeDtypeStruct((N_CHIPS,) + a.shape, a.dtype) for a in late]
        in_specs += [ANY] * nl
        out_specs += [ANY] * nl
        scratch += [pltpu.SemaphoreType.DMA((6 * nl,)), pltpu.SemaphoreType.DMA((6 * nl,))]
        args += list(late)
    return pl.pallas_call(
        body, name="mla_attn_fwd",
        out_shape=tuple(out_shape), grid=(n_groups, n, n),
        in_specs=in_specs, out_specs=tuple(out_specs), scratch_shapes=scratch,
        compiler_params=pltpu.CompilerParams(dimension_semantics=("arbitrary",) * 3, vmem_limit_bytes=VMEM_LIMIT_BYTES,
                                             has_side_effects=nl > 0),
    )(*args)


def _head_rowdot(a, b, *, tm):
    s, width = a.shape
    nh = width // HEAD_DIM

    def body(a_ref, b_ref, o_ref):
        prod = a_ref[...] * b_ref[...]
        for hd in range(nh):
            o_ref[:, hd:hd + 1] = jnp.sum(prod[:, hd * HEAD_DIM:(hd + 1) * HEAD_DIM], axis=-1, keepdims=True)

    return pl.pallas_call(
        body, name="head_rowdot",
        out_shape=jax.ShapeDtypeStruct((s, nh), F32),
        grid=(s // tm,),
        in_specs=[pl.BlockSpec((tm, width), lambda i: (i, 0))] * 2,
        out_specs=pl.BlockSpec((tm, nh), lambda i: (i, 0)),
        compiler_params=_params("parallel"),
    )(a, b)


def _mla_attn_bwd(q, k, v, do, lse, dd, *, t, g, early=()):
    hds, s, _ = q.shape
    n = s // t
    n_groups = hds // g
    ne = len(early)

    def body(*refs):
        q_ref, k_ref, v_ref, do_ref, lse_ref, dd_ref = refs[:6]
        ps_refs = refs[6:6 + ne]
        dq_ref, dk_ref, dv_ref = refs[6 + ne:9 + ne]
        ss_refs = refs[9 + ne:9 + 2 * ne]
        dq_sc, dk_sc, dv_sc = refs[9 + 2 * ne:12 + 2 * ne]
        hg, ki, qi = pl.program_id(0), pl.program_id(1), pl.program_id(2)
        if ne:
            send_sems, recv_sems = refs[12 + 2 * ne:]
            _exchange_in_steps(ps_refs, ss_refs, send_sems, recv_sems,
                               first=jnp.logical_and(hg == 0, jnp.logical_and(ki == 0, qi == 0)),
                               last=jnp.logical_and(hg == n_groups - 1, jnp.logical_and(ki == n - 1, qi == n - 1)))

        @pl.when(jnp.logical_and(ki == 0, qi == 0))
        def _():
            dq_sc[...] = jnp.zeros_like(dq_sc)

        @pl.when(qi == 0)
        def _():
            dk_sc[...] = jnp.zeros_like(dk_sc)
            dv_sc[...] = jnp.zeros_like(dv_sc)

        def step(masked):
            qb, kb, dob = q_ref[...], k_ref[...], do_ref[...]
            sc = _bdot(kb, qb, 2, 2) * MLA_SCALE
            if masked:
                sc = jnp.where(_causal_mask_t(t), sc, NEG)
            p = jnp.exp(sc - lse_ref[...])
            dv_sc[...] += _bdot(_mx(p), dob, 2, 1)
            dp = _bdot(v_ref[...], dob, 2, 2)
            ds = _mx(p * (dp - dd_ref[...]) * MLA_SCALE)
            dk_sc[...] += _bdot(ds, qb, 2, 1)
            dq_sc[qi] += _bdot(ds, kb, 1, 1)

        @pl.when(qi == ki)
        def _():
            step(True)

        @pl.when(qi > ki)
        def _():
            step(False)

        @pl.when(qi == n - 1)
        def _():
            dk_ref[...] = dk_sc[...]
            dv_ref[...] = dv_sc[...]

        @pl.when(jnp.logical_and(ki == n - 1, qi == n - 1))
        def _():
            for j in range(n):
                dq_ref[:, j * t:(j + 1) * t, :] = dq_sc[j]

    qs = lambda w: pl.BlockSpec((g, t, w), lambda h, j, i: (h, jnp.maximum(i, j), 0))
    ks = lambda w: pl.BlockSpec((g, t, w), lambda h, j, i: (h, j, 0))
    rowq = pl.BlockSpec((g, 1, t), lambda h, j, i: (h, 0, jnp.maximum(i, j)))
    scratch = [pltpu.VMEM((n, g, t, QK_PAD), F32), pltpu.VMEM((g, t, QK_PAD), F32), pltpu.VMEM((g, t, HEAD_DIM), F32)]
    if ne:
        scratch += [pltpu.SemaphoreType.DMA((3 * ne,)), pltpu.SemaphoreType.DMA((3 * ne,))]
    return pl.pallas_call(
        body, name="mla_attn_bwd",
        out_shape=(jax.ShapeDtypeStruct((hds, s, QK_PAD), F32), jax.ShapeDtypeStruct((hds, s, QK_PAD), F32),
                   jax.ShapeDtypeStruct((hds, s, HEAD_DIM), F32)) + tuple(jax.ShapeDtypeStruct(a.shape, a.dtype) for a in early),
        grid=(n_groups, n, n),
        in_specs=[qs(QK_PAD), ks(QK_PAD), ks(HEAD_DIM), qs(HEAD_DIM), rowq, rowq] + [ANY] * ne,
        out_specs=(pl.BlockSpec((g, s, QK_PAD), lambda h, j, i: (h, 0, 0)), ks(QK_PAD), ks(HEAD_DIM)) + (ANY,) * ne,
        scratch_shapes=scratch,
        compiler_params=pltpu.CompilerParams(dimension_semantics=("arbitrary",) * 3, vmem_limit_bytes=VMEM_LIMIT_BYTES,
                                             has_side_effects=ne > 0),
    )(q, k, v, do, lse, dd, *early)


def _perm(a, dil):
    if dil == 1:
        return a
    hds, s, e = a.shape
    return a.reshape(hds, s // dil, dil, e).transpose(0, 2, 1, 3).reshape(hds, s, e)


def _unperm(a, dil):
    if dil == 1:
        return a
    hds, s, e = a.shape
    return a.reshape(hds, dil, s // dil, e).transpose(0, 2, 1, 3).reshape(hds, s, e)


def _perm_row(a, dil):
    if dil == 1:
        return a
    hds, _, s = a.shape
    return a.reshape(hds, s // dil, dil).transpose(0, 2, 1).reshape(hds, 1, s)


def _unperm_row(a, dil):
    if dil == 1:
        return a
    hds, _, s = a.shape
    return a.reshape(hds, dil, s // dil).transpose(0, 2, 1).reshape(hds, 1, s)


def _dil_bias(dil):
    slopes = 2.0 ** (-8.0 * jnp.arange(1, HEADS + 1, dtype=F32) / HEADS)
    ik = jnp.arange(DIL_BLOCK)[:, None]
    iq = jnp.arange(DIL_BLOCK)[None, :]
    off_c = iq - ik
    off_p = iq - ik + DIL_BLOCK
    b_c = -slopes[:, None, None] * (off_c * dil).astype(F32)[None]
    b_p = -slopes[:, None, None] * (off_p * dil).astype(F32)[None]
    b_c = jnp.where((off_c >= 0)[None], b_c, NEG)
    b_p = jnp.where((off_p <= DIL_BLOCK)[None], b_p, NEG)
    return b_c, b_p


def _dil_fwd(q, k, v, dil, *, name):
    hds, s, e = q.shape
    blk = DIL_BLOCK
    nblk = s // blk
    nb = nblk // dil
    pair = 2 if nb % 2 == 0 else 1
    b_c, b_p = _dil_bias(dil)

    def body(q_ref, k_ref, kp_ref, v_ref, vp_ref, bc_ref, bp_ref, o_ref, lse_ref):
        first = ((pair * pl.program_id(0)) % nb) == 0
        bc, bp = bc_ref[...], bp_ref[...]
        for j in range(pair):
            rows = slice(j * blk, (j + 1) * blk)
            qb = q_ref[:, rows, :]
            if j == 0:
                kp, vp = kp_ref[...], vp_ref[...]
            else:
                kp, vp = k_ref[:, (j - 1) * blk:j * blk, :], v_ref[:, (j - 1) * blk:j * blk, :]
            s_c = _bdot(k_ref[:, rows, :], qb, 2, 2) * DIL_SCALE + bc
            s_p = _bdot(kp, qb, 2, 2) * DIL_SCALE + bp
            if j == 0:
                s_p = jnp.where(first, NEG, s_p)
            m = jnp.maximum(jnp.max(s_c, axis=1, keepdims=True), jnp.max(s_p, axis=1, keepdims=True))
            p_c = jnp.exp(s_c - m)
            p_p = jnp.exp(s_p - m)
            l = jnp.sum(p_c, axis=1, keepdims=True) + jnp.sum(p_p, axis=1, keepdims=True)
            o = _bdot(_mx(p_c), v_ref[:, rows, :], 1, 1) + _bdot(_mx(p_p), vp, 1, 1)
            o_ref[:, rows, :] = o / jnp.swapaxes(l, 1, 2)
            lse_ref[:, :, rows] = m + jnp.log(l)

    cur = lambda w: pl.BlockSpec((hds, pair * blk, w), lambda b: (0, b, 0))
    prev = lambda w: pl.BlockSpec((hds, blk, w), lambda b: (0, jnp.maximum(pair * b - 1, 0), 0))
    bias = pl.BlockSpec((hds, blk, blk), lambda b: (0, 0, 0))
    return pl.pallas_call(
        body, name=name,
        out_shape=(jax.ShapeDtypeStruct((hds, s, e), F32), jax.ShapeDtypeStruct((hds, 1, s), F32)),
        grid=(nblk // pair,),
        in_specs=[cur(e), cur(e), prev(e), cur(e), prev(e), bias, bias],
        out_specs=(cur(e), pl.BlockSpec((hds, 1, pair * blk), lambda b: (0, 0, b))),
        compiler_params=_params("parallel"),
    )(q, k, k, v, v, b_c, b_p)


def _dil_combine(os_, lses, *, ts):
    hds, s, e = os_[0].shape
    dils = [d for _, d in DIL_PAIRS]

    def body(o0, o1, o2, l0, l1, l2, o_ref, l_ref, sc1, sc2):
        _load_token_order(sc1, o1, dils[1], ts)
        _load_token_order(sc2, o2, dils[2], ts)
        a0, a1, a2 = l0[...], l1[...], l2[...]
        m = jnp.maximum(jnp.maximum(a0, a1), a2)
        e0, e1, e2 = jnp.exp(a0 - m), jnp.exp(a1 - m), jnp.exp(a2 - m)
        tot = e0 + e1 + e2
        col = lambda w: jnp.swapaxes(w, 1, 2)
        res = (col(e0 / tot) * o0[...] + col(e1 / tot) * sc1[...]) + col(e2 / tot) * sc2[...]
        for hd in range(hds):
            o_ref[:, hd * e:(hd + 1) * e] = res[hd]
        l_ref[...] = m + jnp.log(tot)

    _, specs = _residue_major_outs(s, ts, dils, F32)
    view = lambda a, d: a if d == 1 else a.reshape(hds, d, s // d, e)
    rspec = pl.BlockSpec((hds, 1, ts), lambda i: (0, 0, i))
    return pl.pallas_call(
        body, name="dil_combine",
        out_shape=(jax.ShapeDtypeStruct((s, hds * e), F32), jax.ShapeDtypeStruct((hds, 1, s), F32)),
        grid=(s // ts,),
        in_specs=list(specs) + [rspec] * 3,
        out_specs=(pl.BlockSpec((ts, hds * e), lambda i: (i, 0)), rspec),
        scratch_shapes=[pltpu.VMEM((hds, ts, e), F32), pltpu.VMEM((hds, ts, e), F32)],
        compiler_params=_params("parallel"),
    )(*[view(a, d) for a, d in zip(os_, dils)], *lses)


def _dil_bwd(q, k, v, do, lj, dd, dil, *, name):
    hds, s, e = q.shape
    blk = DIL_BLOCK
    nblk = s // blk
    nb = nblk // dil
    pair = 2 if nb % 2 == 0 else 1
    b_c, b_p = _dil_bias(dil)

    def body(q_ref, qn_ref, k_ref, kp_ref, v_ref, vp_ref, do_ref, don_ref, l_ref, ln_ref, d_ref, dn_ref,
             bc_ref, bp_ref, dq_ref, dk_ref, dv_ref):
        b0 = pair * pl.program_id(0)
        first = (b0 % nb) == 0
        nxt = jnp.logical_and(b0 + pair < nblk, ((b0 + pair) % nb) != 0)
        bc, bp = bc_ref[...], bp_ref[...]
        for j in range(pair):
            rows = slice(j * blk, (j + 1) * blk)
            qb, kc, vc = q_ref[:, rows, :], k_ref[:, rows, :], v_ref[:, rows, :]
            dob, l, d = _mx(do_ref[:, rows, :]), l_ref[:, :, rows], d_ref[:, :, rows]
            if j == 0:
                kp, vp = kp_ref[...], vp_ref[...]
            else:
                kp, vp = k_ref[:, (j - 1) * blk:j * blk, :], v_ref[:, (j - 1) * blk:j * blk, :]
            p_c = jnp.exp(_bdot(kc, qb, 2, 2) * DIL_SCALE + bc - l)
            p_p = jnp.exp(_bdot(kp, qb, 2, 2) * DIL_SCALE + bp - l)
            if j == 0:
                p_p = jnp.where(first, 0.0, p_p)
            ds_c = _mx(p_c * (_bdot(vc, dob, 2, 2) - d) * DIL_SCALE)
            ds_p = _mx(p_p * (_bdot(vp, dob, 2, 2) - d) * DIL_SCALE)
            dq_ref[:, rows, :] = _bdot(ds_c, kc, 1, 1) + _bdot(ds_p, kp, 1, 1)
            if j < pair - 1:
                nrows = slice((j + 1) * blk, (j + 2) * blk)
                qn, donb, ln, dn = q_ref[:, nrows, :], _mx(do_ref[:, nrows, :]), l_ref[:, :, nrows], d_ref[:, :, nrows]
            else:
                qn, donb, ln, dn = qn_ref[...], _mx(don_ref[...]), ln_ref[...], dn_ref[...]
            p_n = jnp.exp(_bdot(kc, qn, 2, 2) * DIL_SCALE + bp - ln)
            if j == pair - 1:
                p_n = jnp.where(nxt, p_n, 0.0)
            ds_n = _mx(p_n * (_bdot(vc, donb, 2, 2) - dn) * DIL_SCALE)
            dk_ref[:, rows, :] = _bdot(ds_c, qb, 2, 1) + _bdot(ds_n, qn, 2, 1)
            dv_ref[:, rows, :] = _bdot(_mx(p_c), dob, 2, 1) + _bdot(_mx(p_n), donb, 2, 1)

    cur = lambda w: pl.BlockSpec((hds, pair * blk, w), lambda b: (0, b, 0))
    prev = lambda w: pl.BlockSpec((hds, blk, w), lambda b: (0, jnp.maximum(pair * b - 1, 0), 0))
    nxt_ = lambda w: pl.BlockSpec((hds, blk, w), lambda b: (0, jnp.minimum(pair * (b + 1), nblk - 1), 0))
    rcur = pl.BlockSpec((hds, 1, pair * blk), lambda b: (0, 0, b))
    rnxt = pl.BlockSpec((hds, 1, blk), lambda b: (0, 0, jnp.minimum(pair * (b + 1), nblk - 1)))
    bias = pl.BlockSpec((hds, blk, blk), lambda b: (0, 0, 0))
    out = jax.ShapeDtypeStruct((hds, s, e), F32)
    return pl.pallas_call(
        body, name=name,
        out_shape=(out, out, out),
        grid=(nblk // pair,),
        in_specs=[cur(e), nxt_(e), cur(e), prev(e), cur(e), prev(e), cur(e), nxt_(e),
                  rcur, rnxt, rcur, rnxt, bias, bias],
        out_specs=(cur(e), cur(e), cur(e)),
        compiler_params=_params("parallel"),
    )(q, q, k, k, v, v, do, do, lj, lj, dd, dd, b_c, b_p)


def _add3(a, b, c, *, ts, name):
    hds, s, e = a.shape

    def body(a_ref, b_ref, c_ref, o_ref):
        o_ref[...] = (a_ref[...] + b_ref[...]) + c_ref[...]

    spec = pl.BlockSpec((hds, ts, e), lambda i: (0, i, 0))
    return pl.pallas_call(
        body, name=name,
        out_shape=jax.ShapeDtypeStruct((hds, s, e), F32),
        grid=(s // ts,),
        in_specs=[spec] * 3, out_specs=spec,
        compiler_params=_params("parallel"),
    )(a, b, c)


def _ln_fwd(z, g, b):
    mu = jnp.mean(z, axis=-1, keepdims=True)
    zc = z - mu
    var = jnp.mean(zc * zc, axis=-1, keepdims=True)
    rstd = lax.rsqrt(var + LN_EPS)
    xhat = zc * rstd
    return xhat * g + b, xhat, rstd


def _ln_bwd(dy, xhat, rstd, g):
    dxh = dy * g
    return rstd * (dxh - jnp.mean(dxh, axis=-1, keepdims=True) - xhat * jnp.mean(dxh * xhat, axis=-1, keepdims=True))


def _out_ln1(a_mla, a_dil, w_o, x, g, b, *, tm):
    s = x.shape[0]
    half = HEADS * HEAD_DIM

    def body(am_ref, ad_ref, w_ref, x_ref, g_ref, b_ref, x1_ref, xh_ref, r_ref):
        mix = _dot(_mx(am_ref[...]), w_ref[0:half, :]) + _dot(_mx(ad_ref[...]), w_ref[half:2 * half, :])
        z = DN_ALPHA * x_ref[...] + mix
        y, xhat, rstd = _ln_fwd(z, g_ref[...], b_ref[...])
        x1_ref[...] = y
        xh_ref[...] = xhat
        r_ref[...] = rstd

    row = lambda w: pl.BlockSpec((tm, w), lambda i: (i, 0))
    full = lambda shp: pl.BlockSpec(shp, lambda i: (0,) * len(shp))
    act = jax.ShapeDtypeStruct((s, D_MODEL), F32)
    return pl.pallas_call(
        body, name="out_ln1",
        out_shape=(act, act, jax.ShapeDtypeStruct((s, 1), F32)),
        grid=(s // tm,),
        in_specs=[row(half), row(half), full((D_MODEL, D_MODEL)), row(D_MODEL), full((1, D_MODEL)), full((1, D_MODEL))],
        out_specs=(row(D_MODEL), row(D_MODEL), row(1)),
        compiler_params=_params("parallel"),
    )(a_mla, a_dil, w_o, x, g, b)


def _down_ln2_loss(act, w_down, x1, g, b, target, *, tm):
    s = x1.shape[0]

    def body(a_ref, w_ref, x1_ref, g_ref, b_ref, t_ref, dz_ref, loss_ref, dg_ref, db_ref):
        i = pl.program_id(0)

        @pl.when(i == 0)
        def _():
            loss_ref[...] = jnp.zeros_like(loss_ref)
            dg_ref[...] = jnp.zeros_like(dg_ref)
            db_ref[...] = jnp.zeros_like(db_ref)

        gam = g_ref[...]
        z = DN_ALPHA * x1_ref[...] + _dot(a_ref[...], w_ref[...])
        y, xhat, rstd = _ln_fwd(z, gam, b_ref[...])
        err = y - t_ref[...]
        loss_ref[...] += 0.5 * jnp.sum(jnp.mean(err * err, axis=-1, keepdims=True))
        dy = err * (1.0 / D_MODEL)
        dg_ref[...] += jnp.sum(dy * xhat, axis=0, keepdims=True)
        db_ref[...] += jnp.sum(dy, axis=0, keepdims=True)
        dz_ref[...] = _ln_bwd(dy, xhat, rstd, gam)

    row = lambda w: pl.BlockSpec((tm, w), lambda i: (i, 0))
    full = lambda shp: pl.BlockSpec(shp, lambda i: (0,) * len(shp))
    vec = jax.ShapeDtypeStruct((1, D_MODEL), F32)
    return pl.pallas_call(
        body, name="down_ln2_loss",
        out_shape=(jax.ShapeDtypeStruct((s, D_MODEL), F32), jax.ShapeDtypeStruct((1, LANES), F32), vec, vec),
        grid=(s // tm,),
        in_specs=[row(D_FF), full((D_FF, D_MODEL)), row(D_MODEL), full((1, D_MODEL)), full((1, D_MODEL)), row(D_MODEL)],
        out_specs=(row(D_MODEL), full((1, LANES)), full((1, D_MODEL)), full((1, D_MODEL))),
        compiler_params=_params("arbitrary"),
    )(act, w_down, x1, g, b, target)


def _up_bwd_ln1(du_a, du_g, w_up_t, dz2, xhat1, rstd1, g, *, tm):
    s = dz2.shape[0]

    def body(dua_ref, dug_ref, wa_ref, wg_ref, dz2_ref, xh_ref, r_ref, g_ref, dz1_ref, dg_ref, db_ref):
        i = pl.program_id(0)

        @pl.when(i == 0)
        def _():
            dg_ref[...] = jnp.zeros_like(dg_ref)
            db_ref[...] = jnp.zeros_like(db_ref)

        dx1 = DN_ALPHA * dz2_ref[...] + (_dot(dua_ref[...], wa_ref[...]) + _dot(dug_ref[...], wg_ref[...]))
        xhat = xh_ref[...]
        dg_ref[...] += jnp.sum(dx1 * xhat, axis=0, keepdims=True)
        db_ref[...] += jnp.sum(dx1, axis=0, keepdims=True)
        dz1_ref[...] = _ln_bwd(dx1, xhat, r_ref[...], g_ref[...])

    row = lambda w: pl.BlockSpec((tm, w), lambda i: (i, 0))
    full = lambda shp: pl.BlockSpec(shp, lambda i: (0,) * len(shp))
    vec = jax.ShapeDtypeStruct((1, D_MODEL), F32)
    return pl.pallas_call(
        body, name="up_bwd_ln1",
        out_shape=(jax.ShapeDtypeStruct((s, D_MODEL), F32), vec, vec),
        grid=(s // tm,),
        in_specs=[row(D_FF), row(D_FF),
                  pl.BlockSpec((D_FF, D_MODEL), lambda i: (0, 0)), pl.BlockSpec((D_FF, D_MODEL), lambda i: (1, 0)),
                  row(D_MODEL), row(D_MODEL), row(1), full((1, D_MODEL))],
        out_specs=(row(D_MODEL), full((1, D_MODEL)), full((1, D_MODEL))),
        compiler_params=_params("arbitrary"),
    )(du_a, du_g, w_up_t, w_up_t, dz2, xhat1, rstd1, g)


GELU_C = math.sqrt(2.0 / math.pi)


def _gelu(x):
    cdf = 0.5 * (1.0 + jnp.tanh(GELU_C * (x + 0.044715 * (x * x * x))))
    return x * cdf


def _gelu_grad(x):
    t = jnp.tanh(GELU_C * (x + 0.044715 * (x * x * x)))
    return 0.5 * (1.0 + t) + 0.5 * x * (1.0 - t * t) * (GELU_C * (1.0 + 3.0 * 0.044715 * (x * x)))


def _shift_down(u, halo):
    r1, r2 = pltpu.roll(u, 1, 0), pltpu.roll(u, 2, 0)
    row = lax.broadcasted_iota(jnp.int32, (SUBLANES, u.shape[1]), 0)
    h7, h6 = halo[7:8, :], halo[6:7, :]
    head1 = jnp.where(row == 0, h7, r1[:SUBLANES])
    head2 = jnp.where(row == 0, h6, jnp.where(row == 1, h7, r2[:SUBLANES]))
    return (jnp.concatenate([head1, r1[SUBLANES:]], axis=0), jnp.concatenate([head2, r2[SUBLANES:]], axis=0))


def _shift_up(d, nxt):
    t = d.shape[0]
    r1, r2 = pltpu.roll(d, t - 1, 0), pltpu.roll(d, t - 2, 0)
    row = lax.broadcasted_iota(jnp.int32, (SUBLANES, d.shape[1]), 0)
    n0, n1 = nxt[0:1, :], nxt[1:2, :]
    last = t - SUBLANES
    tail1 = jnp.where(row == SUBLANES - 1, n0, r1[last:])
    tail2 = jnp.where(row == SUBLANES - 1, n1, jnp.where(row == SUBLANES - 2, n0, r2[last:]))
    return (jnp.concatenate([r1[:last], tail1], axis=0), jnp.concatenate([r2[:last], tail2], axis=0))


def _conv(u, s1, s2, w, b):
    return ((b + w[0:1, :] * s2) + w[1:2, :] * s1) + w[2:3, :] * u


def _up_gate_fwd(x1, w_up, conv_w, conv_b, *, tm, tn):
    s = x1.shape[0]
    nj = D_FF // tn
    hb = tm // SUBLANES

    def body(x_ref, xh_ref, wua_ref, wug_ref, wa_ref, wg_ref, ba_ref, bg_ref,
             ua_ref, ug_ref, o_ref, a_ref, ge_ref, gd_ref):
        keep = pl.program_id(1) > 0
        xb, xh = _mx(x_ref[...]), _mx(xh_ref[...])
        wua, wug = wua_ref[...], wug_ref[...]
        ua, ug = _dot(xb, wua), _dot(xb, wug)
        ha = jnp.where(keep, _dot(xh, wua), 0.0)
        hg = jnp.where(keep, _dot(xh, wug), 0.0)
        ua_ref[...] = ua
        ug_ref[...] = ug
        a = _conv(ua, *_shift_down(ua, ha), wa_ref[...], ba_ref[...])
        g = _conv(ug, *_shift_down(ug, hg), wg_ref[...], bg_ref[...])
        ge = _gelu(g)
        o_ref[...] = (ge * a).astype(o_ref.dtype)
        a_ref[...] = a
        ge_ref[...] = ge
        gd_ref[...] = _gelu_grad(g)

    main = lambda off: pl.BlockSpec((tm, tn), lambda j, i: (i, j + off))
    wspec = lambda r, off: pl.BlockSpec((r, tn), lambda j, i: (0, j + off))
    if w_up.ndim == 3:
        wu = lambda off: pl.BlockSpec((None, D_MODEL, tn), lambda j, i: (j + off, 0, 0))
    else:
        wu = lambda off: pl.BlockSpec((D_MODEL, tn), lambda j, i: (0, j + off))
    keep_f32 = jax.ShapeDtypeStruct((s, D_FF), F32)
    return pl.pallas_call(
        body, name="up_gate_fwd",
        out_shape=(keep_f32, keep_f32, jax.ShapeDtypeStruct((s, D_FF), MXU_DTYPE), keep_f32, keep_f32, keep_f32),
        grid=(nj, s // tm),
        in_specs=[pl.BlockSpec((tm, D_MODEL), lambda j, i: (i, 0)),
                  pl.BlockSpec((SUBLANES, D_MODEL), lambda j, i: (jnp.maximum(i * hb - 1, 0), 0)),
                  wu(0), wu(nj), wspec(3, 0), wspec(3, nj), wspec(1, 0), wspec(1, nj)],
        out_specs=(main(0),) * 6,
        compiler_params=_params("parallel", "parallel"),
    )(x1, x1, w_up, w_up, conv_w, conv_w, conv_b, conv_b)


def _gate_bwd(u_a, u_g, dz2, w_down_t, a, ge, gd, conv_w, *, tm, tn):
    s = u_a.shape[0]
    nj = D_FF // tn
    ni = s // tm
    hb = tm // SUBLANES

    def body(ua_ref, ug_ref, ha_ref, hg_ref, dz_ref, dzn_ref, wd_ref, a_ref, an_ref, ge_ref, gen_ref, gd_ref, gdn_ref,
             wa_ref, wg_ref, dua_ref, dug_ref, dwa_ref, dwg_ref, dba_ref, dbg_ref):
        i = pl.program_id(1)

        @pl.when(i == 0)
        def _():
            for r in (dwa_ref, dwg_ref, dba_ref, dbg_ref):
                r[...] = jnp.zeros_like(r)

        wa, wg = wa_ref[...], wg_ref[...]
        ua, ug = ua_ref[...], ug_ref[...]
        ha = jnp.where(i > 0, ha_ref[...], 0.0)
        hg = jnp.where(i > 0, hg_ref[...], 0.0)
        sa1, sa2 = _shift_down(ua, ha)
        sg1, sg2 = _shift_down(ug, hg)
        wd = wd_ref[...]
        d = _dot(_mx(dz_ref[...]), wd)
        dya = d * ge_ref[...]
        dyg = d * a_ref[...] * gd_ref[...]
        dn = jnp.where(i < ni - 1, _dot(_mx(dzn_ref[...]), wd), 0.0)
        dya_n = dn * gen_ref[...]
        dyg_n = dn * an_ref[...] * gdn_ref[...]
        da1, da2 = _shift_up(dya, dya_n)
        dg1, dg2 = _shift_up(dyg, dyg_n)
        dua_ref[...] = (wa[2:3, :] * dya + wa[1:2, :] * da1 + wa[0:1, :] * da2).astype(dua_ref.dtype)
        dug_ref[...] = (wg[2:3, :] * dyg + wg[1:2, :] * dg1 + wg[0:1, :] * dg2).astype(dug_ref.dtype)
        ssum = lambda v: jnp.sum(v, axis=0, keepdims=True)
        dwa_ref[...] += jnp.concatenate([ssum(dya * sa2), ssum(dya * sa1), ssum(dya * ua)], axis=0)
        dwg_ref[...] += jnp.concatenate([ssum(dyg * sg2), ssum(dyg * sg1), ssum(dyg * ug)], axis=0)
        dba_ref[...] += ssum(dya)
        dbg_ref[...] += ssum(dyg)

    main = pl.BlockSpec((tm, tn), lambda j, i: (i, j))
    halo = pl.BlockSpec((SUBLANES, tn), lambda j, i: (jnp.maximum(i * hb - 1, 0), j))
    next_row = lambda j, i: jnp.minimum((i + 1) * hb, s // SUBLANES - 1)
    nxt = pl.BlockSpec((SUBLANES, tn), lambda j, i: (next_row(j, i), j))
    wspec = lambda r, off: pl.BlockSpec((r, tn), lambda j, i: (0, j + off))
    return pl.pallas_call(
        body, name="gate_bwd",
        out_shape=(jax.ShapeDtypeStruct((s, D_FF), MXU_DTYPE), jax.ShapeDtypeStruct((s, D_FF), MXU_DTYPE),
                   jax.ShapeDtypeStruct((3, D_FF), F32), jax.ShapeDtypeStruct((3, D_FF), F32),
                   jax.ShapeDtypeStruct((1, D_FF), F32), jax.ShapeDtypeStruct((1, D_FF), F32)),
        grid=(nj, ni),
        in_specs=[main, main, halo, halo,
                  pl.BlockSpec((tm, D_MODEL), lambda j, i: (i, 0)),
                  pl.BlockSpec((SUBLANES, D_MODEL), lambda j, i: (next_row(j, i), 0)),
                  pl.BlockSpec((D_MODEL, tn), lambda j, i: (0, j))]
        + [main, nxt] * 3 + [wspec(3, 0), wspec(3, nj)],
        out_specs=(main, main, wspec(3, 0), wspec(3, 0), wspec(1, 0), wspec(1, 0)),
        compiler_params=_params("parallel", "arbitrary"),
    )(u_a, u_g, u_a, u_g, dz2, dz2, w_down_t, a, a, ge, ge, gd, gd, conv_w, conv_w)


def _prep_weights(w_in, w_uq, w_uk, w_uv, w_o, w_up, w_down):
    return {**_prep_weights_first(w_in, w_uq, w_uk, w_uv), **_prep_weights_late(w_o, w_up, w_down)}


def _prep_weights_late(w_o, w_up, w_down):
    w_o, w_up, w_down = _mx(w_o), _mx(w_up), _mx(w_down)
    w_up_t = w_up.T if w_up.ndim == 2 else w_up.transpose(0, 2, 1).reshape(2 * D_FF, D_MODEL)
    return dict(w_o=w_o, w_o_t=w_o.T, w_up=w_up, w_up_t=w_up_t, w_down=w_down, w_down_t=w_down.T)


def _prep_weights_first(w_in, w_uq, w_uk, w_uv):
    c = lambda a: a.astype(MXU_DTYPE)
    w_in = c(w_in)
    z = lambda w: jnp.zeros((D_MODEL, w), MXU_DTYPE)
    r0 = Q_RANK + KV_RANK
    w_in_ext = jnp.concatenate([w_in[:, :r0], z(NOPE), w_in[:, r0:r0 + ROPE], z(32), w_in[:, r0 + ROPE:]], axis=1)
    wq = jnp.pad(c(w_uq).transpose(1, 0, 2), ((0, 0), (0, 0), (0, QK_PAD - NOPE - ROPE)))
    wk = jnp.pad(c(w_uk).transpose(1, 0, 2), ((0, 0), (0, 0), (0, QK_PAD - NOPE)))
    wv = c(w_uv).transpose(1, 0, 2)
    t3 = lambda a: a.transpose(0, 2, 1)
    return dict(w_in=w_in_ext, w_in_t=w_in_ext.T, wq=wq, wq_t=t3(wq), wk=wk, wk_t=t3(wk), wv=wv, wv_t=t3(wv))


def _local_step(x, target, w, g_cq, g_ckv, ln1_g, ln1_b, conv_w, conv_b, ln2_g, ln2_b, comm=None):
    s = x.shape[0]
    tabs = _rope_tables(s)
    r2 = lambda a: a.reshape(1, -1)
    heads = lambda a: a.reshape(s, HEADS, HEAD_DIM).transpose(1, 0, 2)
    unheads = lambda a: a.transpose(1, 0, 2).reshape(s, HEADS * HEAD_DIM)
    cb = r2(conv_b)
    dils = [d for _, d in DIL_PAIRS]

    h, qp, kp, vp = _in_proj(x, w["w_in"], tm=256)
    q, k, v, v_t = _mla_prep_fwd(h, r2(g_cq), r2(g_ckv), w["wq"], w["wk"], w["wv"], w["wv_t"], tabs, tm=256)
    if comm is None:
        o_mla_t, lse_mla = _mla_attn_fwd(q, k, v_t, t=512, g=HEADS)
    else:
        o_mla_t, lse_mla, *gathered = _mla_attn_fwd(q, k, v_t, t=512, g=HEADS, late=comm["late"])
        w = {**w, **comm["finish"](gathered)}
    o_bs, lse_bs = [], []
    for i, d in enumerate(dils):
        o_b, l_b = _dil_fwd(qp[i], kp[i], vp[i], d, name=f"dil_fwd_{d}")
        o_bs.append(o_b)
        lse_bs.append(_unperm_row(l_b, d))
    o_dil, lj = _dil_combine(o_bs, lse_bs, ts=512)
    o_mla = o_mla_t.transpose(2, 0, 1).reshape(s, HEADS * HEAD_DIM)
    x1, xhat1, rstd1 = _out_ln1(o_mla, o_dil, w["w_o"], x, r2(ln1_g), r2(ln1_b), tm=256)
    u_a, u_g, act, conv_a, gelu_g, gelu_dg = _up_gate_fwd(x1, w["w_up"], conv_w, cb, tm=256, tn=1408)
    dz2, loss, dg2, db2 = _down_ln2_loss(act, w["w_down"], x1, r2(ln2_g), r2(ln2_b), target, tm=256)

    dw_down = _mm_tn(act, dz2, name="dw_down", tm=1408, tn=D_MODEL, ts=512)
    du_a, du_g, dcw_a, dcw_g, dcb_a, dcb_g = _gate_bwd(u_a, u_g, dz2, w["w_down_t"], conv_a, gelu_g, gelu_dg, conv_w,
                                                       tm=256, tn=1408)
    dz1, dg1, db1 = _up_bwd_ln1(du_a, du_g, w["w_up_t"], dz2, xhat1, rstd1, r2(ln1_g), tm=256)
    dw_up = jnp.concatenate([_mm_tn(x1, du_a, name="dw_up_a", tm=D_MODEL, tn=1408, ts=512),
                             _mm_tn(x1, du_g, name="dw_up_g", tm=D_MODEL, tn=1408, ts=512)], axis=1)
    do_mla, do_dil, dd_all = _attn_bwd_heads(dz1, w["w_o_t"], o_mla, o_dil, tm=256)
    dw_o = jnp.concatenate([_mm_tn(o_mla, dz1, name="dw_o_mla", tm=512, tn=D_MODEL, ts=512),
                            _mm_tn(o_dil, dz1, name="dw_o_dil", tm=512, tn=D_MODEL, ts=512)], axis=0)
    dd_all = dd_all.T
    dd_mla, dd_dil = dd_all[:HEADS].reshape(HEADS, 1, s), dd_all[HEADS:].reshape(HEADS, 1, s)
    early = () if comm is None else tuple(comm["halve"]([("w_up", dw_up), ("w_down", dw_down)]))
    dq, dk, dv, *early_slots = _mla_attn_bwd(q, k, v, do_mla, lse_mla, dd_mla, t=512, g=4, early=early)
    parts = []
    for i, d in enumerate(dils):
        parts.append(_dil_bwd(qp[i], kp[i], vp[i], do_dil[i], _perm_row(lj, d), _perm_row(dd_dil, d), d, name=f"dil_bwd_{d}"))
    dh_dil = _dil_merge(parts, ts=512)
    dh_mla, dwq, dwk, dwv, dgq, dgkv = _mla_prep_bwd(h, dq, dk, dv, r2(g_cq), r2(g_ckv),
                                                     w["wq_t"], w["wk_t"], w["wv_t"], tabs, tm=256)
    mla_w = 4 * LANES
    w_in_t = w["w_in_t"]
    grad_x = _mm_nn(dh_mla, w_in_t[:mla_w], name="in_bwd_mla", tm=512, tn=D_MODEL, tk=mla_w, add=dz1, add_scale=DN_ALPHA)
    grad_x = _mm_nn(dh_dil, w_in_t[mla_w:], name="in_bwd_dil", tm=512, tn=D_MODEL, tk=512, add=grad_x)
    dw_mla = _mm_tn(x, dh_mla, name="dw_in_mla", tm=D_MODEL, tn=mla_w, ts=512)
    dw_dil = _mm_tn(x, dh_dil, name="dw_in_dil", tm=D_MODEL, tn=512, ts=512)
    r0 = Q_RANK + KV_RANK
    grads = dict(
        w_in=jnp.concatenate([dw_mla[:, :r0], dw_mla[:, r0 + NOPE:r0 + NOPE + ROPE], dw_dil], axis=1),
        g_cq=dgq[0], g_ckv=dgkv[0],
        w_uq=dwq[:, :, :NOPE + ROPE].transpose(1, 0, 2),
        w_uk=dwk[:, :, :NOPE].transpose(1, 0, 2),
        w_uv=dwv.transpose(1, 0, 2),
        w_o=dw_o, ln1_g=dg1[0], ln1_b=db1[0], w_up=dw_up,
        conv_w=jnp.concatenate([dcw_a, dcw_g], axis=1), conv_b=jnp.concatenate([dcb_a, dcb_g], axis=1)[0],
        w_down=dw_down, ln2_g=dg2[0], ln2_b=db2[0])
    if comm is not None:
        grads["early"] = (early, tuple(early_slots))
    return loss[0, 0], grad_x, grads


N_CHIPS = 4
SHARDED = ("w_in", "w_uq", "w_o", "w_up", "conv_w", "w_down")
COL_SHARDED = ("w_in", "w_up", "conv_w")
SHARD_SHAPE = dict(w_in=(D_MODEL, IN_WIDTH // 4), w_uq=(Q_RANK // 4, HEADS, NOPE + ROPE), w_o=(D_MODEL // 4, D_MODEL),
                   w_up=(D_MODEL, 2 * D_FF // 4), conv_w=(3, 2 * D_FF // 4), w_down=(D_FF // 4, D_MODEL))
SMALL = ("g_cq", "g_ckv", "w_uk", "w_uv", "ln1_g", "ln1_b", "conv_b", "ln2_g", "ln2_b")
SMALL_SHAPE = dict(g_cq=(Q_RANK,), g_ckv=(KV_RANK,), w_uk=(KV_RANK, HEADS, NOPE), w_uv=(KV_RANK, HEADS, HEAD_DIM),
                   ln1_g=(D_MODEL,), ln1_b=(D_MODEL,), conv_b=(2 * D_FF,), ln2_g=(D_MODEL,), ln2_b=(D_MODEL,))
BIG = ("w_in", "w_uq", "w_o", "w_up", "w_down")
BIG_2D = dict(w_in=(D_MODEL, IN_WIDTH // 4), w_uq=(Q_RANK // 4, HEADS * (NOPE + ROPE)), w_o=(D_MODEL // 4, D_MODEL),
              w_up=(D_MODEL, 2 * D_FF // 4), w_down=(D_FF // 4, D_MODEL))
SMALL_G = SMALL + ("conv_w",)
SMALL_G_SHAPE = {**SMALL_SHAPE, "conv_w": (3, 2 * D_FF)}
SMALL_U_SHAPE = {**SMALL_SHAPE, "conv_w": (3, 2 * D_FF // 4)}


def _size(shape):
    return math.prod(shape)


def _padded_rows(n_elems, mult):
    return -(-n_elems // (LANES * mult)) * mult


SHARD_ROWS = {n: _padded_rows(_size(SHARD_SHAPE[n]), SUBLANES) for n in SHARDED}
R_SMALL = -(-sum(_size(SMALL_G_SHAPE[n]) for n in SMALL_G) // (LANES * LANES)) * LANES
GATHER_FIRST = ("w_in", "w_uq")
GATHER_LATE = ("w_o", "w_up", "w_down")
REDUCED_EARLY = ("w_up", "w_down")
REDUCED_LAST = ("w_in", "w_uq", "w_o")


def _rows(a, rows=None):
    flat = a.reshape(-1)
    rows = -(-flat.shape[0] // LANES) if rows is None else rows
    return jnp.pad(flat, (0, rows * LANES - flat.shape[0])).reshape(rows, LANES)


def _blocked(name, g):
    r, c = BIG_2D[name]
    a = g.reshape(r, N_CHIPS, c).transpose(1, 0, 2) if name in COL_SHARDED else g.reshape(N_CHIPS, r, c)
    return a.reshape(N_CHIPS, 2, r // 2, c)


def _pack_flat(t, names):
    return _rows(jnp.concatenate([t[n].astype(F32).reshape(-1) for n in names]), R_SMALL)


def _unpack_flat(buf, names, shapes):
    flat, out, r = buf.reshape(-1), {}, 0
    for n in names:
        out[n] = flat[r:r + _size(shapes[n])].reshape(shapes[n])
        r += _size(shapes[n])
    return out


def _from_chip_blocks(name, blocks):
    shp = SHARD_SHAPE[name]
    a = blocks.reshape(N_CHIPS, -1)[:, :_size(shp)].reshape((N_CHIPS,) + shp)
    if name in COL_SHARDED:
        return a.transpose(1, 0, 2).reshape(shp[0], N_CHIPS * shp[1])
    return a.reshape((N_CHIPS * shp[0],) + shp[1:])


ANY = pl.BlockSpec(memory_space=pl.ANY)
COMM_PARAMS = pltpu.CompilerParams(has_side_effects=True)


def _coords():
    return lax.axis_index("x"), lax.axis_index("y"), lax.axis_index("c")


def _other_chips(x, y):
    return [(1 - x, y), (x, 1 - y), (1 - x, 1 - y)]


def _remote(src, dst, send_sems, recv_sems, k, to):
    return pltpu.make_async_remote_copy(src_ref=src, dst_ref=dst, send_sem=send_sems.at[k], recv_sem=recv_sems.at[k],
                                        device_id=to, device_id_type=MESH)


def _gather_in_steps(wp_refs, wout_refs, send_sems, recv_sems, *, first, mid, last):
    x, y, c = _coords()
    me = 2 * x + y
    sib = (x, y, 1 - c)
    chips = _other_chips(x, y)
    n = len(wp_refs)
    pairs = [(j, t, px, py) for j, (px, py) in enumerate(chips) for t in range(n)]
    ici = [_remote(wp_refs[t].at[c], wout_refs[t].at[me, c], send_sems, recv_sems, j * n + t, (px, py, c))
           for j, t, px, py in pairs]
    fwd = [_remote(wout_refs[t].at[2 * px + py, c], wout_refs[t].at[2 * px + py, c], send_sems, recv_sems, (3 + j) * n + t, sib)
           for j, t, px, py in pairs]

    @pl.when(first)
    def _():
        for cp in ici:
            cp.start()

    @pl.when(mid)
    def _():
        for i, (j, t, px, py) in enumerate(pairs):
            _remote(wp_refs[t].at[c], wout_refs[t].at[2 * px + py, c], send_sems, recv_sems, j * n + t, (px, py, c)).wait_recv()
            fwd[i].start()

    @pl.when(last)
    def _():
        for j, t, px, py in pairs:
            k = 2 * px + py
            _remote(wout_refs[t].at[k, 1 - c], wout_refs[t].at[k, 1 - c], send_sems, recv_sems, (3 + j) * n + t, sib).wait_recv()
        for cp in ici + fwd:
            cp.wait_send()


def _exchange_in_steps(ps_refs, ss_refs, send_sems, recv_sems, *, first, last):
    x, y, c = _coords()
    me = 2 * x + y
    chips = _other_chips(x, y)
    n = len(ps_refs)
    sends = [_remote(ps_refs[t].at[2 * px + py], ss_refs[t].at[me], send_sems, recv_sems, j * n + t, (px, py, c))
             for j, (px, py) in enumerate(chips) for t in range(n)]

    @pl.when(first)
    def _():
        for cp in sends:
            cp.start()

    @pl.when(last)
    def _():
        for j, (px, py) in enumerate(chips):
            for t in range(n):
                _remote(ps_refs[t].at[me], ss_refs[t].at[2 * px + py], send_sems, recv_sems, j * n + t, (px, py, c)).wait_recv()
        for cp in sends:
            cp.wait_send()


def _gather_weights(wp, cwp):
    def body(wp_ref, cw_ref, wout_ref, cwout_ref, send_sems, recv_sems):
        x, y, c = _coords()
        me = 2 * x + y
        sib = (x, y, 1 - c)
        chips = _other_chips(x, y)
        sends = [_remote(wp_ref.at[c], wout_ref.at[me, c], send_sems, recv_sems, j, (px, py, c))
                 for j, (px, py) in enumerate(chips)]
        sends += [_remote(cw_ref, cwout_ref.at[me], send_sems, recv_sems, 3 + j, (px, py, c))
                  for j, (px, py) in enumerate(chips)]
        for cp in sends:
            cp.start()
        for j, (px, py) in enumerate(chips):
            k = 2 * px + py
            _remote(wp_ref.at[c], wout_ref.at[k, c], send_sems, recv_sems, j, (px, py, c)).wait_recv()
            fwd = _remote(wout_ref.at[k, c], wout_ref.at[k, c], send_sems, recv_sems, 6 + j, sib)
            fwd.start()
            sends.append(fwd)
        for j, (px, py) in enumerate(chips):
            k = 2 * px + py
            _remote(cw_ref, cwout_ref.at[k], send_sems, recv_sems, 3 + j, (px, py, c)).wait_recv()
            _remote(wout_ref.at[k, 1 - c], wout_ref.at[k, 1 - c], send_sems, recv_sems, 6 + j, sib).wait_recv()
        for cp in sends:
            cp.wait_send()

    return pl.pallas_call(
        body, name="gather_weights",
        out_shape=(jax.ShapeDtypeStruct((N_CHIPS,) + wp.shape, wp.dtype), jax.ShapeDtypeStruct((N_CHIPS,) + cwp.shape, cwp.dtype)),
        in_specs=[ANY, ANY], out_specs=(ANY, ANY),
        scratch_shapes=[pltpu.SemaphoreType.DMA((9,)), pltpu.SemaphoreType.DMA((9,))],
        compiler_params=COMM_PARAMS,
    )(wp, cwp)


def _exchange_sibling_halves(gs, whole, *, name):
    n, nw = len(gs), len(whole)

    def body(*refs):
        gs_refs, wh_refs = refs[:n], refs[n:n + nw]
        os_refs, ow_refs = refs[n + nw:2 * n + nw], refs[2 * n + nw:2 * (n + nw)]
        send_sems, recv_sems = refs[2 * (n + nw):]
        x, y, c = _coords()
        sib = (x, y, 1 - c)
        cps = [_remote(gs_refs[t].at[k, 1 - c], os_refs[t].at[k], send_sems, recv_sems, t * N_CHIPS + k, sib)
               for t in range(n) for k in range(N_CHIPS)]
        cps += [_remote(wh_refs[t], ow_refs[t], send_sems, recv_sems, n * N_CHIPS + t, sib) for t in range(nw)]
        for cp in cps:
            cp.start()
        for cp in cps:
            cp.wait_recv()
        for cp in cps:
            cp.wait_send()

    n_sem = n * N_CHIPS + nw
    return pl.pallas_call(
        body, name=name,
        out_shape=tuple(jax.ShapeDtypeStruct((N_CHIPS,) + a.shape[2:], F32) for a in gs)
        + tuple(jax.ShapeDtypeStruct(a.shape, F32) for a in whole),
        in_specs=[ANY] * (n + nw), out_specs=(ANY,) * (n + nw),
        scratch_shapes=[pltpu.SemaphoreType.DMA((n_sem,)), pltpu.SemaphoreType.DMA((n_sem,))],
        compiler_params=COMM_PARAMS,
    )(*gs, *whole)


def _exchange_chips(ps, pr):
    n = len(ps)

    def body(*refs):
        ps_refs, pr_ref, ss_refs, sr_ref = refs[:n], refs[n], refs[n + 1:2 * n + 1], refs[2 * n + 1]
        send_sems, recv_sems = refs[2 * n + 2:]
        x, y, c = _coords()
        me = 2 * x + y
        chips = _other_chips(x, y)
        sends = []
        for j, (px, py) in enumerate(chips):
            to = (px, py, c)
            for t in range(n):
                sends.append(_remote(ps_refs[t].at[2 * px + py], ss_refs[t].at[me], send_sems, recv_sems, j * (n + 1) + t, to))
            sends.append(_remote(pr_ref, sr_ref.at[me], send_sems, recv_sems, j * (n + 1) + n, to))
        for cp in sends:
            cp.start()
        for j, (px, py) in enumerate(chips):
            k, to = 2 * px + py, (px, py, c)
            for t in range(n):
                _remote(ps_refs[t].at[me], ss_refs[t].at[k], send_sems, recv_sems, j * (n + 1) + t, to).wait_recv()
            _remote(pr_ref, sr_ref.at[k], send_sems, recv_sems, j * (n + 1) + n, to).wait_recv()
        for cp in sends:
            cp.wait_send()

    n_sem = 3 * (n + 1)
    return pl.pallas_call(
        body, name="exchange_chips",
        out_shape=tuple(jax.ShapeDtypeStruct(a.shape, a.dtype) for a in ps) + (jax.ShapeDtypeStruct((N_CHIPS,) + pr.shape, F32),),
        in_specs=[ANY] * (n + 1), out_specs=(ANY,) * (n + 1),
        scratch_shapes=[pltpu.SemaphoreType.DMA((n_sem,)), pltpu.SemaphoreType.DMA((n_sem,))],
        compiler_params=COMM_PARAMS,
    )(*ps, pr)


def _exchange_sibling_result(gh):
    n = len(gh)

    def body(*refs):
        gh_refs, out_refs, (send_sems, recv_sems) = refs[:n], refs[n:2 * n], refs[2 * n:]
        x, y, c = _coords()
        cps = [_remote(gh_refs[t], out_refs[t], send_sems, recv_sems, t, (x, y, 1 - c)) for t in range(n)]
        for cp in cps:
            cp.start()
        for cp in cps:
            cp.wait_recv()
        for cp in cps:
            cp.wait_send()

    return pl.pallas_call(
        body, name="exchange_sibling_result",
        out_shape=tuple(jax.ShapeDtypeStruct(a.shape, F32) for a in gh),
        in_specs=[ANY] * n, out_specs=(ANY,) * n,
        scratch_shapes=[pltpu.SemaphoreType.DMA((n,)), pltpu.SemaphoreType.DMA((n,))],
        compiler_params=COMM_PARAMS,
    )(*gh)


def _add_own_half(gs, recv, c_arr, *, name):
    _, rows, cols = recv.shape

    def body(c_ref, a_ref, b_ref, o_ref):
        o_ref[0] = (a_ref[0, 0] + b_ref[0]).astype(o_ref.dtype)

    return pl.pallas_call(
        body, name=name,
        out_shape=jax.ShapeDtypeStruct(recv.shape, GRAD_WIRE_DTYPE),
        grid_spec=pltpu.PrefetchScalarGridSpec(
            num_scalar_prefetch=1, grid=(N_CHIPS,),
            in_specs=[pl.BlockSpec((1, 1, rows, cols), lambda k, c_ref: (k, c_ref[0], 0, 0)),
                      pl.BlockSpec((1, rows, cols), lambda k, c_ref: (k, 0, 0))],
            out_specs=pl.BlockSpec((1, rows, cols), lambda k, c_ref: (k, 0, 0))),
        compiler_params=_params("parallel"),
    )(c_arr, gs, recv)


def _add2(a, b, *, name):
    def body(a_ref, b_ref, o_ref):
        o_ref[...] = a_ref[...] + b_ref[...]

    return pl.pallas_call(body, name=name, out_shape=jax.ShapeDtypeStruct(a.shape, F32))(a, b)


def _sum_slots(slots, *, tr, name):
    _, r, c = slots.shape

    def body(s_ref, o_ref):
        f = lambda k: s_ref[k].astype(F32)
        o_ref[...] = ((f(0) + f(1)) + f(2)) + f(3)

    return pl.pallas_call(
        body, name=name,
        out_shape=jax.ShapeDtypeStruct((r, c), F32),
        grid=(r // tr,),
        in_specs=[pl.BlockSpec((N_CHIPS, tr, c), lambda i: (0, i, 0))],
        out_specs=pl.BlockSpec((tr, c), lambda i: (i, 0)),
        compiler_params=_params("parallel"),
    )(slots)


def _adamw(w, g, m, v, *, tr, name):
    r, cols = w.shape

    def body(w_ref, g_ref, m_ref, v_ref, d_ref, nm_ref, nv_ref):
        g_ = g_ref[...]
        m_ = ADAM_B1 * m_ref[...] + (1.0 - ADAM_B1) * g_
        v_ = ADAM_B2 * v_ref[...] + (1.0 - ADAM_B2) * (g_ * g_)
        m_hat = m_ / (1.0 - ADAM_B1 ** ADAM_STEP)
        v_hat = v_ / (1.0 - ADAM_B2 ** ADAM_STEP)
        d_ref[...] = -ADAM_LR * (m_hat / (jnp.sqrt(v_hat) + ADAM_EPS) + ADAM_WD * w_ref[...])
        nm_ref[...] = m_
        nv_ref[...] = v_

    spec = pl.BlockSpec((tr, cols), lambda i: (i, 0))
    out = jax.ShapeDtypeStruct((r, cols), F32)
    return pl.pallas_call(
        body, name=name, out_shape=(out, out, out), grid=(r // tr,),
        in_specs=[spec] * 4, out_specs=(spec,) * 3,
        compiler_params=_params("parallel"),
    )(w, g, m, v)


WEIGHTS = ("w_in", "g_cq", "g_ckv", "w_uq", "w_uk", "w_uv", "w_o", "ln1_g", "ln1_b", "w_up", "conv_w", "conv_b",
           "w_down", "ln2_g", "ln2_b")


def kernel(x, w_in, g_cq, g_ckv, w_uq, w_uk, w_uv, w_o, ln1_g, ln1_b, w_up, conv_w, conv_b, w_down, ln2_g, ln2_b, loss_target, m_w_in, m_g_cq, m_g_ckv, m_w_uq, m_w_uk, m_w_uv, m_w_o, m_ln1_g, m_ln1_b, m_w_up, m_conv_w, m_conv_b, m_w_down, m_ln2_g, m_ln2_b, v_w_in, v_g_cq, v_g_ckv, v_w_uq, v_w_uk, v_w_uv, v_w_o, v_ln1_g, v_ln1_b, v_w_up, v_conv_w, v_conv_b, v_w_down, v_ln2_g, v_ln2_b):
    wts = dict(zip(WEIGHTS, (w_in, g_cq, g_ckv, w_uq, w_uk, w_uv, w_o, ln1_g, ln1_b, w_up, conv_w, conv_b, w_down, ln2_g, ln2_b)))
    mom = dict(zip(WEIGHTS, (m_w_in, m_g_cq, m_g_ckv, m_w_uq, m_w_uk, m_w_uv, m_w_o, m_ln1_g, m_ln1_b, m_w_up, m_conv_w, m_conv_b, m_w_down, m_ln2_g, m_ln2_b)))
    var = dict(zip(WEIGHTS, (v_w_in, v_g_cq, v_g_ckv, v_w_uq, v_w_uk, v_w_uv, v_w_o, v_ln1_g, v_ln1_b, v_w_up, v_conv_w, v_conv_b, v_w_down, v_ln2_g, v_ln2_b)))

    me = 2 * lax.axis_index("x") + lax.axis_index("y")
    my_c = lax.axis_index("c")
    c_arr = my_c.astype(jnp.int32).reshape(1)
    own = lambda slots, mine: lax.dynamic_update_index_in_dim(slots, mine, me, 0)

    def pack(names):
        return jnp.concatenate([_rows(_mx(wts[n]), SHARD_ROWS[n]) for n in names], axis=0).reshape(2, -1, LANES)

    def unpack(names, gathered, mine):
        buf, full, r = own(gathered, mine).reshape(N_CHIPS, -1, LANES), {}, 0
        for n in names:
            full[n] = _from_chip_blocks(n, buf[:, r:r + SHARD_ROWS[n]])
            r += SHARD_ROWS[n]
        return full

    wp_first = pack(GATHER_FIRST)
    cwp = _rows(conv_w, SHARD_ROWS["conv_w"])
    gathered, cwfull = _gather_weights(wp_first, cwp)
    full = unpack(GATHER_FIRST, gathered, wp_first)
    conv_w_full = _from_chip_blocks("conv_w", own(cwfull, cwp))
    w = _prep_weights_first(full["w_in"], full["w_uq"], w_uk, w_uv)
    late_halves = [_mx(wts[n]).reshape(2, BIG_2D[n][0] // 2, BIG_2D[n][1]) for n in GATHER_LATE]

    def finish(gathered_late):
        w_o_b, w_up_b, w_down_b = (own(a, mine).reshape((N_CHIPS,) + BIG_2D[n])
                                   for a, mine, n in zip(gathered_late, late_halves, GATHER_LATE))
        return _prep_weights_late(w_o_b.reshape(D_MODEL, D_MODEL), w_up_b, w_down_b.reshape(D_FF, D_MODEL))

    def halve(named, whole=(), tag="early"):
        gb = [_blocked(n, a) for n, a in named]
        recv = _exchange_sibling_halves(gb, list(whole), name=f"exchange_sibling_halves_{tag}")
        ps = [_add_own_half(gb[i], recv[i], c_arr, name=f"add_half_{n}") for i, (n, _) in enumerate(named)]
        return ps + [_add2(a, recv[len(gb) + i], name=f"add_whole_{tag}_{i}") for i, a in enumerate(whole)]

    comm = dict(late=late_halves, finish=finish, halve=halve)
    loss, grad_x, g = _local_step(x[0], loss_target[0], w, g_cq, g_ckv, ln1_g, ln1_b, conv_w_full, conv_b, ln2_g, ln2_b, comm=comm)

    ps_early, slots_early = g.pop("early")
    g["loss"] = loss.reshape(1)
    *ps_rest, pr = halve([(n, g[n]) for n in REDUCED_LAST], whole=[_pack_flat(g, SMALL_G + ("loss",))], tag="last")
    *slots_rest, slots_r = _exchange_chips(ps_rest, pr)
    ps = {**dict(zip(REDUCED_LAST, ps_rest)), **dict(zip(REDUCED_EARLY, ps_early))}
    slots = {**dict(zip(REDUCED_LAST, slots_rest)), **dict(zip(REDUCED_EARLY, slots_early))}
    slots = [own(slots[n], lax.dynamic_index_in_dim(ps[n], me, 0, keepdims=False)) for n in BIG]
    slots_r = own(slots_r, pr)
    g_half = [_sum_slots(slots[i], tr=slots[i].shape[1] // 2, name=f"sum_chips_{n}") for i, n in enumerate(BIG)]
    g_small = _unpack_flat(_sum_slots(slots_r, tr=R_SMALL, name="sum_chips_small"), SMALL_G + ("loss",),
                           {**SMALL_G_SHAPE, "loss": (1,)})
    loss = g_small.pop("loss")[0]
    g_other = _exchange_sibling_result(g_half)
    grads = {n: jnp.where(my_c == 0, jnp.concatenate([g_half[i], g_other[i]]), jnp.concatenate([g_other[i], g_half[i]]))
             for i, n in enumerate(BIG)}
    g_small["conv_w"] = lax.dynamic_slice_in_dim(g_small["conv_w"], me * SHARD_SHAPE["conv_w"][1], SHARD_SHAPE["conv_w"][1], 1)
    grads.update(g_small)

    res = {}
    for n in BIG:
        as2d = lambda a: a.reshape(BIG_2D[n])
        d, m, v = _adamw(as2d(wts[n]), grads[n], as2d(mom[n]), as2d(var[n]), tr=BIG_2D[n][0] // 4, name=f"adamw_{n}")
        res[n] = [a.reshape(SHARD_SHAPE[n]) for a in (grads[n], d, m, v)]
    flat = lambda t: _pack_flat(t, SMALL_G)
    dmv = _adamw(flat(wts), flat(g_small), flat(mom), flat(var), tr=R_SMALL, name="adamw_small")
    dmv = [_unpack_flat(a, SMALL_G, SMALL_U_SHAPE) for a in dmv]
    for n in SMALL_G:
        res[n] = [g_small[n]] + [t[n] for t in dmv]
    outs = [res[n][j] for j in range(4) for n in WEIGHTS]
    return (loss, grad_x[None], *outs)
```

```python
import math

import jax
import jax.numpy as jnp
from jax import lax
from jax.experimental import pallas as pl
from jax.experimental.pallas import tpu as pltpu

F32 = jnp.float32
MXU_DTYPE = jnp.bfloat16
GRAD_WIRE_DTYPE = jnp.bfloat16
NEG = -1e30

D_MODEL = 1024
HEADS = 8
HEAD_DIM = 64
Q_RANK = 256
KV_RANK = 128
NOPE = 64
ROPE = 32
QK_PAD = 128
IN_WIDTH = 1952
IN_EXT = 2048
D_FF = 2816
DIL_PAIRS = ((128, 1), (512, 4), (2048, 16))
DIL_BLOCK = 128
ROPE_THETA = 10000.0
DN_ALPHA = 2.0 ** 0.25
LN_EPS = 1e-5
RMS_EPS = 1e-6
MLA_SCALE = 1.0 / math.sqrt(NOPE + ROPE)
LOG2_E = math.log2(math.e)
DIL_SCALE = 1.0 / math.sqrt(HEAD_DIM)

ADAM_LR = 0.001
ADAM_B1 = 0.9
ADAM_B2 = 0.999
ADAM_EPS = 1e-08
ADAM_WD = 0.01
ADAM_STEP = 10

LANES = 128
SUBLANES = 8
VMEM_LIMIT_BYTES = 56 * 1024 * 1024

MESH = pl.DeviceIdType.MESH


def _params(*sem):
    return pltpu.CompilerParams(dimension_semantics=sem, vmem_limit_bytes=VMEM_LIMIT_BYTES)


def _dot(a, b):
    return jnp.dot(a, b, preferred_element_type=F32)


def _dot_nt(a, b):
    return lax.dot_general(a, b, (((1,), (1,)), ((), ())), preferred_element_type=F32)


def _dot_tn(a, b):
    return lax.dot_general(a, b, (((0,), (0,)), ((), ())), preferred_element_type=F32)


def _mx(a):
    return a.astype(MXU_DTYPE)


def _mm_nn(a, b, *, name, tm, tn, tk, out_dtype=F32, add=None, add_scale=1.0):
    m, kdim = a.shape
    blocked = b.ndim == 3
    n = b.shape[0] * b.shape[2] if blocked else b.shape[1]
    nk = kdim // tk

    def body(*refs):
        if add is None:
            a_ref, b_ref, o_ref, acc = refs
        else:
            a_ref, b_ref, c_ref, o_ref, acc = refs
        k = pl.program_id(2)

        @pl.when(k == 0)
        def _():
            acc[...] = jnp.zeros_like(acc)

        acc[...] += _dot(_mx(a_ref[...]), _mx(b_ref[...]))

        @pl.when(k == nk - 1)
        def _():
            r = acc[...]
            if add is not None:
                r = r + add_scale * c_ref[...]
            o_ref[...] = r.astype(out_dtype)

    b_spec = (pl.BlockSpec((None, tk, tn), lambda i, j, k: (j, k, 0)) if blocked
              else pl.BlockSpec((tk, tn), lambda i, j, k: (k, j)))
    in_specs = [pl.BlockSpec((tm, tk), lambda i, j, k: (i, k)), b_spec]
    args = [a, b]
    if add is not None:
        in_specs.append(pl.BlockSpec((tm, tn), lambda i, j, k: (i, j)))
        args.append(add)
    return pl.pallas_call(
        body, name=name,
        out_shape=jax.ShapeDtypeStruct((m, n), out_dtype),
        grid=(m // tm, n // tn, nk),
        in_specs=in_specs,
        out_specs=pl.BlockSpec((tm, tn), lambda i, j, k: (i, j)),
        scratch_shapes=[pltpu.VMEM((tm, tn), F32)],
        compiler_params=_params("parallel", "parallel", "arbitrary"),
    )(*args)


def _mm_tn(a, b, *, name, tm, tn, ts, out_dtype=F32):
    s, m = a.shape
    n = b.shape[1]
    ns = s // ts

    def body(a_ref, b_ref, o_ref, acc):
        k = pl.program_id(2)

        @pl.when(k == 0)
        def _():
            acc[...] = jnp.zeros_like(acc)

        acc[...] += _dot_tn(_mx(a_ref[...]), _mx(b_ref[...]))

        @pl.when(k == ns - 1)
        def _():
            o_ref[...] = acc[...].astype(out_dtype)

    return pl.pallas_call(
        body, name=name,
        out_shape=jax.ShapeDtypeStruct((m, n), out_dtype),
        grid=(m // tm, n // tn, ns),
        in_specs=[pl.BlockSpec((ts, tm), lambda i, j, k: (k, i)),
                  pl.BlockSpec((ts, tn), lambda i, j, k: (k, j))],
        out_specs=pl.BlockSpec((tm, tn), lambda i, j, k: (i, j)),
        scratch_shapes=[pltpu.VMEM((tm, tn), F32)],
        compiler_params=_params("parallel", "parallel", "arbitrary"),
    )(a, b)


def _in_proj(x, w_in_ext, *, tm):
    s = x.shape[0]
    mla_w = 4 * LANES
    dil_w = HEADS * HEAD_DIM
    dils = [d for _, d in DIL_PAIRS]

    def body(x_ref, w_ref, h_ref, *rest):
        outs, sc = rest[:-1], rest[-1]
        xb = _mx(x_ref[...])
        h_ref[...] = _dot(xb, w_ref[:, 0:mla_w])
        for j in range(3):
            part = _dot(xb, w_ref[:, mla_w + j * dil_w:mla_w + (j + 1) * dil_w])
            for hd in range(HEADS):
                sc[hd] = part[:, hd * HEAD_DIM:(hd + 1) * HEAD_DIM]
            for b, d in enumerate(dils):
                _store_residue_major(outs[3 * j + b], sc, d, tm)

    shapes, specs = _residue_major_outs(s, tm, dils, MXU_DTYPE)
    res = pl.pallas_call(
        body, name="in_proj",
        out_shape=(jax.ShapeDtypeStruct((s, mla_w), F32),) + shapes * 3,
        grid=(s // tm,),
        in_specs=[pl.BlockSpec((tm, D_MODEL), lambda i: (i, 0)), pl.BlockSpec((D_MODEL, IN_EXT), lambda i: (0, 0))],
        out_specs=(pl.BlockSpec((tm, mla_w), lambda i: (i, 0)),) + specs * 3,
        scratch_shapes=[pltpu.VMEM((HEADS, tm, HEAD_DIM), F32)],
        compiler_params=_params("parallel"),
    )(x, w_in_ext)
    hm = lambda a: a.reshape(HEADS, s, HEAD_DIM)
    return res[0], [hm(a) for a in res[1:4]], [hm(a) for a in res[4:7]], [hm(a) for a in res[7:10]]


def _residue_major_outs(s, tm, dils, dtype):
    shapes, specs = [], []
    for d in dils:
        if d == 1:
            shapes.append(jax.ShapeDtypeStruct((HEADS, s, HEAD_DIM), dtype))
            specs.append(pl.BlockSpec((HEADS, tm, HEAD_DIM), lambda i: (0, i, 0)))
        else:
            shapes.append(jax.ShapeDtypeStruct((HEADS, d, s // d, HEAD_DIM), dtype))
            specs.append(pl.BlockSpec((HEADS, d, tm // d, HEAD_DIM), lambda i: (0, 0, i, 0)))
    return tuple(shapes), tuple(specs)


def _store_residue_major(o_ref, src_ref, d, tm):
    if d == 1:
        o_ref[...] = src_ref[...].astype(o_ref.dtype)
    else:
        for r in range(d):
            o_ref[:, r] = src_ref[:, pl.ds(r, tm // d, stride=d), :].astype(o_ref.dtype)


def _load_token_order(dst_ref, src_ref, d, tm, accumulate=False):
    if d == 1:
        dst_ref[...] = dst_ref[...] + src_ref[...] if accumulate else src_ref[...]
    else:
        for r in range(d):
            rows = pl.ds(r, tm // d, stride=d)
            dst_ref[:, rows, :] = dst_ref[:, rows, :] + src_ref[:, r] if accumulate else src_ref[:, r]


def _attn_bwd_heads(dz1, w_o_t, a_mla, a_dil, *, tm, swap=()):
    s = dz1.shape[0]
    half = HEADS * HEAD_DIM
    dils = [d for _, d in DIL_PAIRS]
    nsw = len(swap)
    n_steps = s // tm

    def body(*refs):
        dz_ref, w_ref, am_ref, ad_ref = refs[:4]
        gs_refs = refs[4:4 + nsw]
        dom_ref, dd_ref = refs[4 + nsw:6 + nsw]
        dod_refs = refs[6 + nsw:6 + nsw + len(dils)]
        os_refs = refs[6 + nsw + len(dils):6 + 2 * nsw + len(dils)]
        if nsw:
            send_sems, recv_sems = refs[6 + 2 * nsw + len(dils):]
            i = pl.program_id(0)
            _swap_halves_in_steps(gs_refs, os_refs, send_sems, recv_sems, first=i == 0, last=i == n_steps - 1)
        dzb = _mx(dz_ref[...])
        for j, (a_ref, o_ref) in enumerate(((am_ref, dom_ref), (ad_ref, dod_refs[0]))):
            da = _dot(dzb, w_ref[:, j * half:(j + 1) * half])
            prod = da * a_ref[...]
            for hd in range(HEADS):
                sl = slice(hd * HEAD_DIM, (hd + 1) * HEAD_DIM)
                o_ref[hd] = da[:, sl].astype(o_ref.dtype)
                dd_ref[:, j * HEADS + hd:j * HEADS + hd + 1] = jnp.sum(prod[:, sl], axis=-1, keepdims=True)
        for b, d in enumerate(dils[1:]):
            _store_residue_major(dod_refs[1 + b], dod_refs[0], d, tm)

    hspec = pl.BlockSpec((HEADS, tm, HEAD_DIM), lambda i: (0, i, 0))
    row = lambda w: pl.BlockSpec((tm, w), lambda i: (i, 0))
    shapes, specs = _residue_major_outs(s, tm, dils, F32)
    n_sem = nsw * N_CHIPS
    do_mla, dd, *rest = pl.pallas_call(
        body, name="attn_bwd_heads",
        out_shape=(jax.ShapeDtypeStruct((HEADS, s, HEAD_DIM), MXU_DTYPE), jax.ShapeDtypeStruct((s, 2 * HEADS), F32)) + shapes
        + tuple(jax.ShapeDtypeStruct((N_CHIPS,) + a.shape[2:], F32) for a in swap),
        grid=(n_steps,),
        in_specs=[row(D_MODEL), pl.BlockSpec((D_MODEL, D_MODEL), lambda i: (0, 0)), row(half), row(half)] + [ANY] * nsw,
        out_specs=(hspec, row(2 * HEADS)) + specs + (ANY,) * nsw,
        scratch_shapes=[pltpu.SemaphoreType.DMA((n_sem,)), pltpu.SemaphoreType.DMA((n_sem,))] if nsw else [],
        compiler_params=pltpu.CompilerParams(dimension_semantics=("arbitrary",), vmem_limit_bytes=VMEM_LIMIT_BYTES,
                                             has_side_effects=nsw > 0),
    )(dz1, w_o_t, a_mla, a_dil, *swap)
    do_dil, received = rest[:len(dils)], rest[len(dils):]
    return do_mla, [a.reshape(HEADS, s, HEAD_DIM) for a in do_dil], dd, received


def _dil_merge(parts, *, ts):
    hds, s, e = parts[0][0].shape
    dils = [d for _, d in DIL_PAIRS]

    def body(*refs):
        o_ref, sc = refs[9], refs[10]
        for j in range(3):
            for b, d in enumerate(dils):
                _load_token_order(sc, refs[3 * b + j], d, ts, accumulate=b > 0)
            tot = sc[...]
            for hd in range(hds):
                col = j * hds * e + hd * e
                o_ref[:, col:col + e] = tot[hd].astype(o_ref.dtype)

    _, specs = _residue_major_outs(s, ts, dils, F32)
    view = lambda a, d: a if d == 1 else a.reshape(hds, d, s // d, e)
    return pl.pallas_call(
        body, name="dil_merge",
        out_shape=jax.ShapeDtypeStruct((s, 3 * hds * e), MXU_DTYPE),
        grid=(s // ts,),
        in_specs=[specs[b] for b in range(3) for _ in range(3)],
        out_specs=pl.BlockSpec((ts, 3 * hds * e), lambda i: (i, 0)),
        scratch_shapes=[pltpu.VMEM((hds, ts, e), F32)],
        compiler_params=_params("parallel"),
    )(*[view(parts[b][j], dils[b]) for b in range(3) for j in range(3)])


def _rope_tables(s):
    half = ROPE // 2
    freqs = ROPE_THETA ** (-jnp.arange(half, dtype=F32) / half)
    ang = jnp.arange(s).astype(F32)[:, None] * freqs[None, :]
    cos, sin = jnp.cos(ang), jnp.sin(ang)
    z = lambda w: jnp.zeros((s, w), F32)
    c = jnp.concatenate([jnp.ones((s, NOPE), F32), cos, cos, z(32)], axis=1)
    s1 = jnp.concatenate([z(NOPE + half), sin, z(32)], axis=1)
    s2 = jnp.concatenate([z(NOPE), -sin, z(half + 32)], axis=1)
    mask = jnp.concatenate([z(NOPE), jnp.ones((s, ROPE), F32), z(32)], axis=1)
    return c, s1, s2, mask


def _rope(x, c, s1, s2):
    return x * c + pltpu.roll(x, 16, 1) * s1 + pltpu.roll(x, LANES - 16, 1) * s2


def _unrope(dy, c, s1, s2):
    return dy * c + pltpu.roll(dy * s1, LANES - 16, 1) + pltpu.roll(dy * s2, 16, 1)


def _rms(x):
    r = lax.rsqrt(jnp.mean(x * x, axis=-1, keepdims=True) + RMS_EPS)
    return x * r, r


def _mla_prep_fwd(h, g_cq, g_ckv, wq, wk, wv, wv_t, tabs, *, tm):
    s = h.shape[0]
    c_t, s1_t, s2_t, _ = tabs

    def body(h_ref, gq_ref, gkv_ref, wq_ref, wk_ref, wv_ref, wvt_ref, c_ref, s1_ref, s2_ref,
             q_ref, k_ref, v_ref, vt_ref):
        cq = h_ref[:, 0:Q_RANK]
        ckv = h_ref[:, Q_RANK:Q_RANK + KV_RANK]
        kr = h_ref[:, Q_RANK + KV_RANK:Q_RANK + KV_RANK + QK_PAD]
        c, s1, s2 = c_ref[...], s1_ref[...], s2_ref[...]
        cqn = _mx(_rms(cq)[0] * gq_ref[...])
        ckvn = _mx(_rms(ckv)[0] * gkv_ref[...])
        kr_rot = _rope(kr, c, s1, s2)
        for hd in range(HEADS):
            q_ref[hd] = _rope(_dot(cqn, wq_ref[hd]), c, s1, s2).astype(q_ref.dtype)
            k_ref[hd] = (_dot(ckvn, wk_ref[hd]) + kr_rot).astype(k_ref.dtype)
            v_ref[hd] = _dot(ckvn, wv_ref[hd]).astype(v_ref.dtype)
            vt_ref[hd] = _dot_nt(wvt_ref[hd], ckvn).astype(vt_ref.dtype)

    full = lambda shp: pl.BlockSpec(shp, lambda i: (0,) * len(shp))
    row = lambda w: pl.BlockSpec((tm, w), lambda i: (i, 0))
    return pl.pallas_call(
        body, name="mla_prep_fwd",
        out_shape=(jax.ShapeDtypeStruct((HEADS, s, QK_PAD), MXU_DTYPE),
                   jax.ShapeDtypeStruct((HEADS, s, QK_PAD), MXU_DTYPE),
                   jax.ShapeDtypeStruct((HEADS, s, HEAD_DIM), MXU_DTYPE),
                   jax.ShapeDtypeStruct((HEADS, HEAD_DIM, s), MXU_DTYPE)),
        grid=(s // tm,),
        in_specs=[row(4 * LANES), full((1, Q_RANK)), full((1, KV_RANK)),
                  full((HEADS, Q_RANK, QK_PAD)), full((HEADS, KV_RANK, QK_PAD)), full((HEADS, KV_RANK, HEAD_DIM)),
                  full((HEADS, HEAD_DIM, KV_RANK)), row(LANES), row(LANES), row(LANES)],
        out_specs=(pl.BlockSpec((HEADS, tm, QK_PAD), lambda i: (0, i, 0)),
                   pl.BlockSpec((HEADS, tm, QK_PAD), lambda i: (0, i, 0)),
                   pl.BlockSpec((HEADS, tm, HEAD_DIM), lambda i: (0, i, 0)),
                   pl.BlockSpec((HEADS, HEAD_DIM, tm), lambda i: (0, 0, i))),
        compiler_params=_params("parallel"),
    )(h, g_cq, g_ckv, wq, wk, wv, wv_t, c_t, s1_t, s2_t)


def _mla_prep_bwd(h, dq, dk, dv, g_cq, g_ckv, wq_t, wk_t, wv_t, tabs, *, tm):
    s = h.shape[0]
    c_t, s1_t, s2_t, mask_t = tabs

    def body(h_ref, dq_ref, dk_ref, dv_ref, gq_ref, gkv_ref, wqt_ref, wkt_ref, wvt_ref,
             c_ref, s1_ref, s2_ref, mask_ref, dh_ref, dwq_ref, dwk_ref, dwv_ref, dgq_ref, dgkv_ref):
        i = pl.program_id(0)

        @pl.when(i == 0)
        def _():
            dwq_ref[...] = jnp.zeros_like(dwq_ref)
            dwk_ref[...] = jnp.zeros_like(dwk_ref)
            dwv_ref[...] = jnp.zeros_like(dwv_ref)
            dgq_ref[...] = jnp.zeros_like(dgq_ref)
            dgkv_ref[...] = jnp.zeros_like(dgkv_ref)

        cq = h_ref[:, 0:Q_RANK]
        ckv = h_ref[:, Q_RANK:Q_RANK + KV_RANK]
        c, s1, s2 = c_ref[...], s1_ref[...], s2_ref[...]
        cqh, rq = _rms(cq)
        ckvh, rkv = _rms(ckv)
        gq, gkv = gq_ref[...], gkv_ref[...]
        cqn = _mx(cqh * gq)
        ckvn = _mx(ckvh * gkv)
        dcqn = jnp.zeros((tm, Q_RANK), F32)
        dckvn = jnp.zeros((tm, KV_RANK), F32)
        dkr = jnp.zeros((tm, QK_PAD), F32)
        for hd in range(HEADS):
            dqh = _mx(_unrope(dq_ref[hd], c, s1, s2))
            dcqn = dcqn + _dot(dqh, wqt_ref[hd])
            dwq_ref[hd] += _dot_tn(cqn, dqh)
            dkh = dk_ref[hd]
            dkr = dkr + dkh
            dkh = _mx(dkh)
            dckvn = dckvn + _dot(dkh, wkt_ref[hd])
            dwk_ref[hd] += _dot_tn(ckvn, dkh)
            dvh = _mx(dv_ref[hd])
            dckvn = dckvn + _dot(dvh, wvt_ref[hd])
            dwv_ref[hd] += _dot_tn(ckvn, dvh)
        dgq_ref[...] += jnp.sum(dcqn * cqh, axis=0, keepdims=True)
        dgkv_ref[...] += jnp.sum(dckvn * ckvh, axis=0, keepdims=True)
        gd = dcqn * gq
        dh_ref[:, 0:Q_RANK] = rq * (gd - cqh * jnp.mean(gd * cqh, axis=-1, keepdims=True))
        gd = dckvn * gkv
        dh_ref[:, Q_RANK:Q_RANK + KV_RANK] = rkv * (gd - ckvh * jnp.mean(gd * ckvh, axis=-1, keepdims=True))
        dh_ref[:, Q_RANK + KV_RANK:Q_RANK + KV_RANK + QK_PAD] = _unrope(dkr, c, s1, s2) * mask_ref[...]

    full = lambda shp: pl.BlockSpec(shp, lambda i: (0,) * len(shp))
    row = lambda w: pl.BlockSpec((tm, w), lambda i: (i, 0))
    hrow = lambda w: pl.BlockSpec((HEADS, tm, w), lambda i: (0, i, 0))
    return pl.pallas_call(
        body, name="mla_prep_bwd",
        out_shape=(jax.ShapeDtypeStruct((s, 4 * LANES), F32),
                   jax.ShapeDtypeStruct((HEADS, Q_RANK, QK_PAD), F32),
                   jax.ShapeDtypeStruct((HEADS, KV_RANK, QK_PAD), F32),
                   jax.ShapeDtypeStruct((HEADS, KV_RANK, HEAD_DIM), F32),
                   jax.ShapeDtypeStruct((1, Q_RANK), F32),
                   jax.ShapeDtypeStruct((1, KV_RANK), F32)),
        grid=(s // tm,),
        in_specs=[row(4 * LANES), hrow(QK_PAD), hrow(QK_PAD), hrow(HEAD_DIM),
                  full((1, Q_RANK)), full((1, KV_RANK)),
                  full((HEADS, QK_PAD, Q_RANK)), full((HEADS, QK_PAD, KV_RANK)), full((HEADS, HEAD_DIM, KV_RANK)),
                  row(LANES), row(LANES), row(LANES), row(LANES)],
        out_specs=(row(4 * LANES), full((HEADS, Q_RANK, QK_PAD)), full((HEADS, KV_RANK, QK_PAD)),
                   full((HEADS, KV_RANK, HEAD_DIM)), full((1, Q_RANK)), full((1, KV_RANK))),
        compiler_params=_params("arbitrary"),
    )(h, dq, dk, dv, g_cq, g_ckv, wq_t, wk_t, wv_t, c_t, s1_t, s2_t, mask_t)


def _bdot(a, b, ca, cb):
    return lax.dot_general(a, b, (((ca,), (cb,)), ((0,), (0,))), preferred_element_type=F32)


def _causal_mask_t(t):
    kk = lax.broadcasted_iota(jnp.int32, (t, t), 0)
    qq = lax.broadcasted_iota(jnp.int32, (t, t), 1)
    return (qq >= kk)[None]


def _mla_attn_fwd(q, k, v_t, *, t, g, late=None):
    hds, s, _ = q.shape
    n = s // t
    n_groups = hds // g

    nl = 0 if late is None else len(late)

    def body(*refs):
        q_ref, k_ref, vt_ref = refs[:3]
        wp_refs = refs[3:3 + nl]
        o_ref, lse_ref = refs[3 + nl:5 + nl]
        wout_refs = refs[5 + nl:5 + 2 * nl]
        m_sc, l_sc, acc_sc = refs[5 + 2 * nl:8 + 2 * nl]
        hg, qi, ki = pl.program_id(0), pl.program_id(1), pl.program_id(2)
        if nl:
            send_sems, recv_sems = refs[8 + 2 * nl:]
            tail = jnp.logical_and(hg == n_groups - 1, qi == n - 1)
            _gather_in_steps(wp_refs, wout_refs, send_sems, recv_sems,
                             first=jnp.logical_and(hg == 0, jnp.logical_and(qi == 0, ki == 0)),
                             mid=jnp.logical_and(tail, ki == 0), last=jnp.logical_and(tail, ki == n - 1))

        @pl.when(ki == 0)
        def _():
            m_sc[...] = jnp.full_like(m_sc, NEG)
            l_sc[...] = jnp.zeros_like(l_sc)
            acc_sc[...] = jnp.zeros_like(acc_sc)

        def step(masked):
            sc = _bdot(k_ref[...], q_ref[...], 2, 2)
            if masked:
                sc = jnp.where(_causal_mask_t(t), sc, NEG)
            m_prev = m_sc[...]
            m_new = jnp.maximum(m_prev, jnp.max(sc, axis=1, keepdims=True))
            p = jnp.exp2((sc - m_new) * (MLA_SCALE * LOG2_E))
            a = jnp.exp2((m_prev - m_new) * (MLA_SCALE * LOG2_E))
            l_sc[...] = a * l_sc[...] + jnp.sum(p, axis=1, keepdims=True)
            acc_sc[...] = a * acc_sc[...] + _bdot(vt_ref[...], _mx(p), 2, 1)
            m_sc[...] = m_new

        @pl.when(ki < qi)
        def _():
            step(False)

        @pl.when(ki == qi)
        def _():
            step(True)
            o_ref[...] = acc_sc[...] / l_sc[...]
            lse_ref[...] = m_sc[...] * MLA_SCALE + jnp.log(l_sc[...])

    qspec = pl.BlockSpec((g, t, QK_PAD), lambda h, i, j: (h, i, 0))
    kspec = pl.BlockSpec((g, t, QK_PAD), lambda h, i, j: (h, jnp.minimum(i, j), 0))
    vspec = pl.BlockSpec((g, HEAD_DIM, t), lambda h, i, j: (h, 0, jnp.minimum(i, j)))
    out_shape = [jax.ShapeDtypeStruct((hds, HEAD_DIM, s), F32), jax.ShapeDtypeStruct((hds, 1, s), F32)]
    in_specs = [qspec, kspec, vspec]
    out_specs = [pl.BlockSpec((g, HEAD_DIM, t), lambda h, i, j: (h, 0, i)), pl.BlockSpec((g, 1, t), lambda h, i, j: (h, 0, i))]
    scratch = [pltpu.VMEM((g, 1, t), F32), pltpu.VMEM((g, 1, t), F32), pltpu.VMEM((g, HEAD_DIM, t), F32)]
    args = [q, k, v_t]
    if nl:
        out_shape += [jax.ShapeDtypeStruct((N_CHIPS,) + a.shape, a.dtype) for a in late]
        in_specs += [ANY] * nl
        out_specs += [ANY] * nl
        scratch += [pltpu.SemaphoreType.DMA((6 * nl,)), pltpu.SemaphoreType.DMA((6 * nl,))]
        args += list(late)
    return pl.pallas_call(
        body, name="mla_attn_fwd",
        out_shape=tuple(out_shape), grid=(n_groups, n, n),
        in_specs=in_specs, out_specs=tuple(out_specs), scratch_shapes=scratch,
        compiler_params=pltpu.CompilerParams(dimension_semantics=("arbitrary",) * 3, vmem_limit_bytes=VMEM_LIMIT_BYTES,
                                             has_side_effects=nl > 0),
    )(*args)


def _mla_attn_bwd(q, k, v, do, lse, dd, *, t, g, early=()):
    hds, s, _ = q.shape
    n = s // t
    n_groups = hds // g
    ne = len(early)

    def body(*refs):
        q_ref, k_ref, v_ref, do_ref, lse_ref, dd_ref = refs[:6]
        ps_refs = refs[6:6 + ne]
        dq_ref, dk_ref, dv_ref = refs[6 + ne:9 + ne]
        ss_refs = refs[9 + ne:9 + 2 * ne]
        dq_sc, dk_sc, dv_sc = refs[9 + 2 * ne:12 + 2 * ne]
        hg, ki, qi = pl.program_id(0), pl.program_id(1), pl.program_id(2)
        if ne:
            send_sems, recv_sems = refs[12 + 2 * ne:]
            _exchange_in_steps(ps_refs, ss_refs, send_sems, recv_sems,
                               first=jnp.logical_and(hg == 0, jnp.logical_and(ki == 0, qi == 0)),
                               last=jnp.logical_and(hg == n_groups - 1, jnp.logical_and(ki == n - 1, qi == n - 1)))

        @pl.when(jnp.logical_and(ki == 0, qi == 0))
        def _():
            dq_sc[...] = jnp.zeros_like(dq_sc)

        @pl.when(qi == 0)
        def _():
            dk_sc[...] = jnp.zeros_like(dk_sc)
            dv_sc[...] = jnp.zeros_like(dv_sc)

        def step(masked):
            qb, kb, dob = q_ref[...], k_ref[...], do_ref[...]
            sc = _bdot(kb, qb, 2, 2) * MLA_SCALE
            if masked:
                sc = jnp.where(_causal_mask_t(t), sc, NEG)
            p = jnp.exp(sc - lse_ref[...])
            dv_sc[...] += _bdot(_mx(p), dob, 2, 1)
            dp = _bdot(v_ref[...], dob, 2, 2)
            ds = _mx(p * (dp - dd_ref[...]) * MLA_SCALE)
            dk_sc[...] += _bdot(ds, qb, 2, 1)
            dq_sc[qi] += _bdot(ds, kb, 1, 1)

        @pl.when(qi == ki)
        def _():
            step(True)

        @pl.when(qi > ki)
        def _():
            step(False)

        @pl.when(qi == n - 1)
        def _():
            dk_ref[...] = dk_sc[...]
            dv_ref[...] = dv_sc[...]

        @pl.when(jnp.logical_and(ki == n - 1, qi == n - 1))
        def _():
            for j in range(n):
                dq_ref[:, j * t:(j + 1) * t, :] = dq_sc[j]

    qs = lambda w: pl.BlockSpec((g, t, w), lambda h, j, i: (h, jnp.maximum(i, j), 0))
    ks = lambda w: pl.BlockSpec((g, t, w), lambda h, j, i: (h, j, 0))
    rowq = pl.BlockSpec((g, 1, t), lambda h, j, i: (h, 0, jnp.maximum(i, j)))
    scratch = [pltpu.VMEM((n, g, t, QK_PAD), F32), pltpu.VMEM((g, t, QK_PAD), F32), pltpu.VMEM((g, t, HEAD_DIM), F32)]
    if ne:
        scratch += [pltpu.SemaphoreType.DMA((3 * ne,)), pltpu.SemaphoreType.DMA((3 * ne,))]
    return pl.pallas_call(
        body, name="mla_attn_bwd",
        out_shape=(jax.ShapeDtypeStruct((hds, s, QK_PAD), F32), jax.ShapeDtypeStruct((hds, s, QK_PAD), F32),
                   jax.ShapeDtypeStruct((hds, s, HEAD_DIM), F32)) + tuple(jax.ShapeDtypeStruct(a.shape, a.dtype) for a in early),
        grid=(n_groups, n, n),
        in_specs=[qs(QK_PAD), ks(QK_PAD), ks(HEAD_DIM), qs(HEAD_DIM), rowq, rowq] + [ANY] * ne,
        out_specs=(pl.BlockSpec((g, s, QK_PAD), lambda h, j, i: (h, 0, 0)), ks(QK_PAD), ks(HEAD_DIM)) + (ANY,) * ne,
        scratch_shapes=scratch,
        compiler_params=pltpu.CompilerParams(dimension_semantics=("arbitrary",) * 3, vmem_limit_bytes=VMEM_LIMIT_BYTES,
                                             has_side_effects=ne > 0),
    )(q, k, v, do, lse, dd, *early)


def _perm_row(a, dil):
    if dil == 1:
        return a
    hds, _, s = a.shape
    return a.reshape(hds, s // dil, dil).transpose(0, 2, 1).reshape(hds, 1, s)


def _unperm_row(a, dil):
    if dil == 1:
        return a
    hds, _, s = a.shape
    return a.reshape(hds, dil, s // dil).transpose(0, 2, 1).reshape(hds, 1, s)


def _dil_bias(dil):
    slopes = 2.0 ** (-8.0 * jnp.arange(1, HEADS + 1, dtype=F32) / HEADS)
    ik = jnp.arange(DIL_BLOCK)[:, None]
    iq = jnp.arange(DIL_BLOCK)[None, :]
    off_c = iq - ik
    off_p = iq - ik + DIL_BLOCK
    b_c = -slopes[:, None, None] * (off_c * dil).astype(F32)[None]
    b_p = -slopes[:, None, None] * (off_p * dil).astype(F32)[None]
    b_c = jnp.where((off_c >= 0)[None], b_c, NEG)
    b_p = jnp.where((off_p <= DIL_BLOCK)[None], b_p, NEG)
    return b_c, b_p


def _dil_fwd(q, k, v, dil, *, name):
    hds, s, e = q.shape
    blk = DIL_BLOCK
    nblk = s // blk
    nb = nblk // dil
    pair = 2 if nb % 2 == 0 else 1
    b_c, b_p = _dil_bias(dil)

    def body(q_ref, k_ref, kp_ref, v_ref, vp_ref, bc_ref, bp_ref, o_ref, lse_ref):
        first = ((pair * pl.program_id(0)) % nb) == 0
        bc, bp = bc_ref[...], bp_ref[...]
        for j in range(pair):
            rows = slice(j * blk, (j + 1) * blk)
            qb = q_ref[:, rows, :]
            if j == 0:
                kp, vp = kp_ref[...], vp_ref[...]
            else:
                kp, vp = k_ref[:, (j - 1) * blk:j * blk, :], v_ref[:, (j - 1) * blk:j * blk, :]
            s_c = _bdot(k_ref[:, rows, :], qb, 2, 2) * DIL_SCALE + bc
            s_p = _bdot(kp, qb, 2, 2) * DIL_SCALE + bp
            if j == 0:
                s_p = jnp.where(first, NEG, s_p)
            m = jnp.maximum(jnp.max(s_c, axis=1, keepdims=True), jnp.max(s_p, axis=1, keepdims=True))
            p_c = jnp.exp(s_c - m)
            p_p = jnp.exp(s_p - m)
            l = jnp.sum(p_c, axis=1, keepdims=True) + jnp.sum(p_p, axis=1, keepdims=True)
            o = _bdot(_mx(p_c), v_ref[:, rows, :], 1, 1) + _bdot(_mx(p_p), vp, 1, 1)
            o_ref[:, rows, :] = o / jnp.swapaxes(l, 1, 2)
            lse_ref[:, :, rows] = m + jnp.log(l)

    cur = lambda w: pl.BlockSpec((hds, pair * blk, w), lambda b: (0, b, 0))
    prev = lambda w: pl.BlockSpec((hds, blk, w), lambda b: (0, jnp.maximum(pair * b - 1, 0), 0))
    bias = pl.BlockSpec((hds, blk, blk), lambda b: (0, 0, 0))
    return pl.pallas_call(
        body, name=name,
        out_shape=(jax.ShapeDtypeStruct((hds, s, e), F32), jax.ShapeDtypeStruct((hds, 1, s), F32)),
        grid=(nblk // pair,),
        in_specs=[cur(e), cur(e), prev(e), cur(e), prev(e), bias, bias],
        out_specs=(cur(e), pl.BlockSpec((hds, 1, pair * blk), lambda b: (0, 0, b))),
        compiler_params=_params("parallel"),
    )(q, k, k, v, v, b_c, b_p)


def _dil_combine(os_, lses, *, ts):
    hds, s, e = os_[0].shape
    dils = [d for _, d in DIL_PAIRS]

    def body(o0, o1, o2, l0, l1, l2, o_ref, l_ref, sc1, sc2):
        _load_token_order(sc1, o1, dils[1], ts)
        _load_token_order(sc2, o2, dils[2], ts)
        a0, a1, a2 = l0[...], l1[...], l2[...]
        m = jnp.maximum(jnp.maximum(a0, a1), a2)
        e0, e1, e2 = jnp.exp(a0 - m), jnp.exp(a1 - m), jnp.exp(a2 - m)
        tot = e0 + e1 + e2
        col = lambda w: jnp.swapaxes(w, 1, 2)
        res = (col(e0 / tot) * o0[...] + col(e1 / tot) * sc1[...]) + col(e2 / tot) * sc2[...]
        for hd in range(hds):
            o_ref[:, hd * e:(hd + 1) * e] = res[hd]
        l_ref[...] = m + jnp.log(tot)

    _, specs = _residue_major_outs(s, ts, dils, F32)
    view = lambda a, d: a if d == 1 else a.reshape(hds, d, s // d, e)
    rspec = pl.BlockSpec((hds, 1, ts), lambda i: (0, 0, i))
    return pl.pallas_call(
        body, name="dil_combine",
        out_shape=(jax.ShapeDtypeStruct((s, hds * e), F32), jax.ShapeDtypeStruct((hds, 1, s), F32)),
        grid=(s // ts,),
        in_specs=list(specs) + [rspec] * 3,
        out_specs=(pl.BlockSpec((ts, hds * e), lambda i: (i, 0)), rspec),
        scratch_shapes=[pltpu.VMEM((hds, ts, e), F32), pltpu.VMEM((hds, ts, e), F32)],
        compiler_params=_params("parallel"),
    )(*[view(a, d) for a, d in zip(os_, dils)], *lses)


def _dil_bwd(q, k, v, do, lj, dd, dil, *, name):
    hds, s, e = q.shape
    blk = DIL_BLOCK
    nblk = s // blk
    nb = nblk // dil
    pair = 2 if nb % 2 == 0 else 1
    b_c, b_p = _dil_bias(dil)

    def body(q_ref, qn_ref, k_ref, kp_ref, v_ref, vp_ref, do_ref, don_ref, l_ref, ln_ref, d_ref, dn_ref,
             bc_ref, bp_ref, dq_ref, dk_ref, dv_ref):
        b0 = pair * pl.program_id(0)
        first = (b0 % nb) == 0
        nxt = jnp.logical_and(b0 + pair < nblk, ((b0 + pair) % nb) != 0)
        bc, bp = bc_ref[...], bp_ref[...]
        for j in range(pair):
            rows = slice(j * blk, (j + 1) * blk)
            qb, kc, vc = q_ref[:, rows, :], k_ref[:, rows, :], v_ref[:, rows, :]
            dob, l, d = _mx(do_ref[:, rows, :]), l_ref[:, :, rows], d_ref[:, :, rows]
            if j == 0:
                kp, vp = kp_ref[...], vp_ref[...]
            else:
                kp, vp = k_ref[:, (j - 1) * blk:j * blk, :], v_ref[:, (j - 1) * blk:j * blk, :]
            p_c = jnp.exp(_bdot(kc, qb, 2, 2) * DIL_SCALE + bc - l)
            p_p = jnp.exp(_bdot(kp, qb, 2, 2) * DIL_SCALE + bp - l)
            if j == 0:
                p_p = jnp.where(first, 0.0, p_p)
            ds_c = _mx(p_c * (_bdot(vc, dob, 2, 2) - d) * DIL_SCALE)
            ds_p = _mx(p_p * (_bdot(vp, dob, 2, 2) - d) * DIL_SCALE)
            dq_ref[:, rows, :] = _bdot(ds_c, kc, 1, 1) + _bdot(ds_p, kp, 1, 1)
            if j < pair - 1:
                nrows = slice((j + 1) * blk, (j + 2) * blk)
                qn, donb, ln, dn = q_ref[:, nrows, :], _mx(do_ref[:, nrows, :]), l_ref[:, :, nrows], d_ref[:, :, nrows]
            else:
                qn, donb, ln, dn = qn_ref[...], _mx(don_ref[...]), ln_ref[...], dn_ref[...]
            p_n = jnp.exp(_bdot(kc, qn, 2, 2) * DIL_SCALE + bp - ln)
            if j == pair - 1:
                p_n = jnp.where(nxt, p_n, 0.0)
            ds_n = _mx(p_n * (_bdot(vc, donb, 2, 2) - dn) * DIL_SCALE)
            dk_ref[:, rows, :] = _bdot(ds_c, qb, 2, 1) + _bdot(ds_n, qn, 2, 1)
            dv_ref[:, rows, :] = _bdot(_mx(p_c), dob, 2, 1) + _bdot(_mx(p_n), donb, 2, 1)

    cur = lambda w: pl.BlockSpec((hds, pair * blk, w), lambda b: (0, b, 0))
    prev = lambda w: pl.BlockSpec((hds, blk, w), lambda b: (0, jnp.maximum(pair * b - 1, 0), 0))
    nxt_ = lambda w: pl.BlockSpec((hds, blk, w), lambda b: (0, jnp.minimum(pair * (b + 1), nblk - 1), 0))
    rcur = pl.BlockSpec((hds, 1, pair * blk), lambda b: (0, 0, b))
    rnxt = pl.BlockSpec((hds, 1, blk), lambda b: (0, 0, jnp.minimum(pair * (b + 1), nblk - 1)))
    bias = pl.BlockSpec((hds, blk, blk), lambda b: (0, 0, 0))
    out = jax.ShapeDtypeStruct((hds, s, e), F32)
    return pl.pallas_call(
        body, name=name,
        out_shape=(out, out, out),
        grid=(nblk // pair,),
        in_specs=[cur(e), nxt_(e), cur(e), prev(e), cur(e), prev(e), cur(e), nxt_(e),
                  rcur, rnxt, rcur, rnxt, bias, bias],
        out_specs=(cur(e), cur(e), cur(e)),
        compiler_params=_params("parallel"),
    )(q, q, k, k, v, v, do, do, lj, lj, dd, dd, b_c, b_p)


def _ln_fwd(z, g, b):
    mu = jnp.mean(z, axis=-1, keepdims=True)
    zc = z - mu
    var = jnp.mean(zc * zc, axis=-1, keepdims=True)
    rstd = lax.rsqrt(var + LN_EPS)
    xhat = zc * rstd
    return xhat * g + b, xhat, rstd


def _ln_bwd(dy, xhat, rstd, g):
    dxh = dy * g
    return rstd * (dxh - jnp.mean(dxh, axis=-1, keepdims=True) - xhat * jnp.mean(dxh * xhat, axis=-1, keepdims=True))


def _out_ln1(a_mla, a_dil, w_o, x, g, b, *, tm):
    s = x.shape[0]
    half = HEADS * HEAD_DIM

    def body(am_ref, ad_ref, w_ref, x_ref, g_ref, b_ref, x1_ref, xh_ref, r_ref):
        mix = _dot(_mx(am_ref[...]), w_ref[0:half, :]) + _dot(_mx(ad_ref[...]), w_ref[half:2 * half, :])
        z = DN_ALPHA * x_ref[...] + mix
        y, xhat, rstd = _ln_fwd(z, g_ref[...], b_ref[...])
        x1_ref[...] = y
        xh_ref[...] = xhat
        r_ref[...] = rstd

    row = lambda w: pl.BlockSpec((tm, w), lambda i: (i, 0))
    full = lambda shp: pl.BlockSpec(shp, lambda i: (0,) * len(shp))
    act = jax.ShapeDtypeStruct((s, D_MODEL), F32)
    return pl.pallas_call(
        body, name="out_ln1",
        out_shape=(act, act, jax.ShapeDtypeStruct((s, 1), F32)),
        grid=(s // tm,),
        in_specs=[row(half), row(half), full((D_MODEL, D_MODEL)), row(D_MODEL), full((1, D_MODEL)), full((1, D_MODEL))],
        out_specs=(row(D_MODEL), row(D_MODEL), row(1)),
        compiler_params=_params("parallel"),
    )(a_mla, a_dil, w_o, x, g, b)


def _down_ln2_loss(act, w_down, x1, g, b, target, *, tm):
    s = x1.shape[0]

    def body(a_ref, w_ref, x1_ref, g_ref, b_ref, t_ref, dz_ref, loss_ref, dg_ref, db_ref):
        i = pl.program_id(0)

        @pl.when(i == 0)
        def _():
            loss_ref[...] = jnp.zeros_like(loss_ref)
            dg_ref[...] = jnp.zeros_like(dg_ref)
            db_ref[...] = jnp.zeros_like(db_ref)

        gam = g_ref[...]
        z = DN_ALPHA * x1_ref[...] + _dot(a_ref[...], w_ref[...])
        y, xhat, rstd = _ln_fwd(z, gam, b_ref[...])
        err = y - t_ref[...]
        loss_ref[...] += 0.5 * jnp.sum(jnp.mean(err * err, axis=-1, keepdims=True))
        dy = err * (1.0 / D_MODEL)
        dg_ref[...] += jnp.sum(dy * xhat, axis=0, keepdims=True)
        db_ref[...] += jnp.sum(dy, axis=0, keepdims=True)
        dz_ref[...] = _ln_bwd(dy, xhat, rstd, gam)

    row = lambda w: pl.BlockSpec((tm, w), lambda i: (i, 0))
    full = lambda shp: pl.BlockSpec(shp, lambda i: (0,) * len(shp))
    vec = jax.ShapeDtypeStruct((1, D_MODEL), F32)
    return pl.pallas_call(
        body, name="down_ln2_loss",
        out_shape=(jax.ShapeDtypeStruct((s, D_MODEL), F32), jax.ShapeDtypeStruct((1, LANES), F32), vec, vec),
        grid=(s // tm,),
        in_specs=[row(D_FF), full((D_FF, D_MODEL)), row(D_MODEL), full((1, D_MODEL)), full((1, D_MODEL)), row(D_MODEL)],
        out_specs=(row(D_MODEL), full((1, LANES)), full((1, D_MODEL)), full((1, D_MODEL))),
        compiler_params=_params("arbitrary"),
    )(act, w_down, x1, g, b, target)


def _up_bwd_ln1(du_a, du_g, w_up_t, dz2, xhat1, rstd1, g, *, tm):
    s = dz2.shape[0]

    def body(dua_ref, dug_ref, wa_ref, wg_ref, dz2_ref, xh_ref, r_ref, g_ref, dz1_ref, dg_ref, db_ref):
        i = pl.program_id(0)

        @pl.when(i == 0)
        def _():
            dg_ref[...] = jnp.zeros_like(dg_ref)
            db_ref[...] = jnp.zeros_like(db_ref)

        dx1 = DN_ALPHA * dz2_ref[...] + (_dot(dua_ref[...], wa_ref[...]) + _dot(dug_ref[...], wg_ref[...]))
        xhat = xh_ref[...]
        dg_ref[...] += jnp.sum(dx1 * xhat, axis=0, keepdims=True)
        db_ref[...] += jnp.sum(dx1, axis=0, keepdims=True)
        dz1_ref[...] = _ln_bwd(dx1, xhat, r_ref[...], g_ref[...])

    row = lambda w: pl.BlockSpec((tm, w), lambda i: (i, 0))
    full = lambda shp: pl.BlockSpec(shp, lambda i: (0,) * len(shp))
    vec = jax.ShapeDtypeStruct((1, D_MODEL), F32)
    return pl.pallas_call(
        body, name="up_bwd_ln1",
        out_shape=(jax.ShapeDtypeStruct((s, D_MODEL), F32), vec, vec),
        grid=(s // tm,),
        in_specs=[row(D_FF), row(D_FF),
                  pl.BlockSpec((D_FF, D_MODEL), lambda i: (0, 0)), pl.BlockSpec((D_FF, D_MODEL), lambda i: (1, 0)),
                  row(D_MODEL), row(D_MODEL), row(1), full((1, D_MODEL))],
        out_specs=(row(D_MODEL), full((1, D_MODEL)), full((1, D_MODEL))),
        compiler_params=_params("arbitrary"),
    )(du_a, du_g, w_up_t, w_up_t, dz2, xhat1, rstd1, g)


GELU_C = math.sqrt(2.0 / math.pi)


def _gelu(x):
    cdf = 0.5 * (1.0 + jnp.tanh(GELU_C * (x + 0.044715 * (x * x * x))))
    return x * cdf


def _gelu_grad(x):
    t = jnp.tanh(GELU_C * (x + 0.044715 * (x * x * x)))
    return 0.5 * (1.0 + t) + 0.5 * x * (1.0 - t * t) * (GELU_C * (1.0 + 3.0 * 0.044715 * (x * x)))


def _shift_down(u, halo):
    r1, r2 = pltpu.roll(u, 1, 0), pltpu.roll(u, 2, 0)
    row = lax.broadcasted_iota(jnp.int32, (SUBLANES, u.shape[1]), 0)
    h7, h6 = halo[7:8, :], halo[6:7, :]
    head1 = jnp.where(row == 0, h7, r1[:SUBLANES])
    head2 = jnp.where(row == 0, h6, jnp.where(row == 1, h7, r2[:SUBLANES]))
    return (jnp.concatenate([head1, r1[SUBLANES:]], axis=0), jnp.concatenate([head2, r2[SUBLANES:]], axis=0))


def _shift_up(d, nxt):
    t = d.shape[0]
    r1, r2 = pltpu.roll(d, t - 1, 0), pltpu.roll(d, t - 2, 0)
    row = lax.broadcasted_iota(jnp.int32, (SUBLANES, d.shape[1]), 0)
    n0, n1 = nxt[0:1, :], nxt[1:2, :]
    last = t - SUBLANES
    tail1 = jnp.where(row == SUBLANES - 1, n0, r1[last:])
    tail2 = jnp.where(row == SUBLANES - 1, n1, jnp.where(row == SUBLANES - 2, n0, r2[last:]))
    return (jnp.concatenate([r1[:last], tail1], axis=0), jnp.concatenate([r2[:last], tail2], axis=0))


def _conv(u, s1, s2, w, b):
    return ((b + w[0:1, :] * s2) + w[1:2, :] * s1) + w[2:3, :] * u


def _up_gate_fwd(x1, w_up, conv_w, conv_b, *, tm, tn):
    s = x1.shape[0]
    nj = D_FF // tn
    hb = tm // SUBLANES

    def body(x_ref, xh_ref, wua_ref, wug_ref, wa_ref, wg_ref, ba_ref, bg_ref,
             ua_ref, ug_ref, o_ref, a_ref, ge_ref, gd_ref):
        keep = pl.program_id(1) > 0
        xb, xh = _mx(x_ref[...]), _mx(xh_ref[...])
        wua, wug = wua_ref[...], wug_ref[...]
        ua, ug = _dot(xb, wua), _dot(xb, wug)
        ha = jnp.where(keep, _dot(xh, wua), 0.0)
        hg = jnp.where(keep, _dot(xh, wug), 0.0)
        ua_ref[...] = ua
        ug_ref[...] = ug
        a = _conv(ua, *_shift_down(ua, ha), wa_ref[...], ba_ref[...])
        g = _conv(ug, *_shift_down(ug, hg), wg_ref[...], bg_ref[...])
        ge = _gelu(g)
        o_ref[...] = (ge * a).astype(o_ref.dtype)
        a_ref[...] = a
        ge_ref[...] = ge
        gd_ref[...] = _gelu_grad(g)

    main = lambda off: pl.BlockSpec((tm, tn), lambda j, i: (i, j + off))
    wspec = lambda r, off: pl.BlockSpec((r, tn), lambda j, i: (0, j + off))
    if w_up.ndim == 3:
        wu = lambda off: pl.BlockSpec((None, D_MODEL, tn), lambda j, i: (j + off, 0, 0))
    else:
        wu = lambda off: pl.BlockSpec((D_MODEL, tn), lambda j, i: (0, j + off))
    keep_f32 = jax.ShapeDtypeStruct((s, D_FF), F32)
    return pl.pallas_call(
        body, name="up_gate_fwd",
        out_shape=(keep_f32, keep_f32, jax.ShapeDtypeStruct((s, D_FF), MXU_DTYPE), keep_f32, keep_f32, keep_f32),
        grid=(nj, s // tm),
        in_specs=[pl.BlockSpec((tm, D_MODEL), lambda j, i: (i, 0)),
                  pl.BlockSpec((SUBLANES, D_MODEL), lambda j, i: (jnp.maximum(i * hb - 1, 0), 0)),
                  wu(0), wu(nj), wspec(3, 0), wspec(3, nj), wspec(1, 0), wspec(1, nj)],
        out_specs=(main(0),) * 6,
        compiler_params=_params("parallel", "parallel"),
    )(x1, x1, w_up, w_up, conv_w, conv_w, conv_b, conv_b)


def _gate_bwd(u_a, u_g, dz2, w_down_t, a, ge, gd, conv_w, *, tm, tn):
    s = u_a.shape[0]
    nj = D_FF // tn
    ni = s // tm
    hb = tm // SUBLANES

    def body(ua_ref, ug_ref, ha_ref, hg_ref, dz_ref, dzn_ref, wd_ref, a_ref, an_ref, ge_ref, gen_ref, gd_ref, gdn_ref,
             wa_ref, wg_ref, dua_ref, dug_ref, dwa_ref, dwg_ref, dba_ref, dbg_ref):
        i = pl.program_id(1)

        @pl.when(i == 0)
        def _():
            for r in (dwa_ref, dwg_ref, dba_ref, dbg_ref):
                r[...] = jnp.zeros_like(r)

        wa, wg = wa_ref[...], wg_ref[...]
        ua, ug = ua_ref[...], ug_ref[...]
        ha = jnp.where(i > 0, ha_ref[...], 0.0)
        hg = jnp.where(i > 0, hg_ref[...], 0.0)
        sa1, sa2 = _shift_down(ua, ha)
        sg1, sg2 = _shift_down(ug, hg)
        wd = wd_ref[...]
        d = _dot(_mx(dz_ref[...]), wd)
        dya = d * ge_ref[...]
        dyg = d * a_ref[...] * gd_ref[...]
        dn = jnp.where(i < ni - 1, _dot(_mx(dzn_ref[...]), wd), 0.0)
        dya_n = dn * gen_ref[...]
        dyg_n = dn * an_ref[...] * gdn_ref[...]
        da1, da2 = _shift_up(dya, dya_n)
        dg1, dg2 = _shift_up(dyg, dyg_n)
        dua_ref[...] = (wa[2:3, :] * dya + wa[1:2, :] * da1 + wa[0:1, :] * da2).astype(dua_ref.dtype)
        dug_ref[...] = (wg[2:3, :] * dyg + wg[1:2, :] * dg1 + wg[0:1, :] * dg2).astype(dug_ref.dtype)
        ssum = lambda v: jnp.sum(v, axis=0, keepdims=True)
        dwa_ref[...] += jnp.concatenate([ssum(dya * sa2), ssum(dya * sa1), ssum(dya * ua)], axis=0)
        dwg_ref[...] += jnp.concatenate([ssum(dyg * sg2), ssum(dyg * sg1), ssum(dyg * ug)], axis=0)
        dba_ref[...] += ssum(dya)
        dbg_ref[...] += ssum(dyg)

    main = pl.BlockSpec((tm, tn), lambda j, i: (i, j))
    halo = pl.BlockSpec((SUBLANES, tn), lambda j, i: (jnp.maximum(i * hb - 1, 0), j))
    next_row = lambda j, i: jnp.minimum((i + 1) * hb, s // SUBLANES - 1)
    nxt = pl.BlockSpec((SUBLANES, tn), lambda j, i: (next_row(j, i), j))
    wspec = lambda r, off: pl.BlockSpec((r, tn), lambda j, i: (0, j + off))
    return pl.pallas_call(
        body, name="gate_bwd",
        out_shape=(jax.ShapeDtypeStruct((s, D_FF), MXU_DTYPE), jax.ShapeDtypeStruct((s, D_FF), MXU_DTYPE),
                   jax.ShapeDtypeStruct((3, D_FF), F32), jax.ShapeDtypeStruct((3, D_FF), F32),
                   jax.ShapeDtypeStruct((1, D_FF), F32), jax.ShapeDtypeStruct((1, D_FF), F32)),
        grid=(nj, ni),
        in_specs=[main, main, halo, halo,
                  pl.BlockSpec((tm, D_MODEL), lambda j, i: (i, 0)),
                  pl.BlockSpec((SUBLANES, D_MODEL), lambda j, i: (next_row(j, i), 0)),
                  pl.BlockSpec((D_MODEL, tn), lambda j, i: (0, j))]
        + [main, nxt] * 3 + [wspec(3, 0), wspec(3, nj)],
        out_specs=(main, main, wspec(3, 0), wspec(3, 0), wspec(1, 0), wspec(1, 0)),
        compiler_params=_params("parallel", "arbitrary"),
    )(u_a, u_g, u_a, u_g, dz2, dz2, w_down_t, a, a, ge, ge, gd, gd, conv_w, conv_w)


def _prep_weights(w_in, w_uq, w_uk, w_uv, w_o, w_up, w_down):
    return {**_prep_weights_first(w_in, w_uq, w_uk, w_uv), **_prep_weights_late(w_o, w_up, w_down)}


def _prep_weights_late(w_o, w_up, w_down):
    w_o, w_up, w_down = _mx(w_o), _mx(w_up), _mx(w_down)
    w_up_t = w_up.T if w_up.ndim == 2 else w_up.transpose(0, 2, 1).reshape(2 * D_FF, D_MODEL)
    return dict(w_o=w_o, w_o_t=w_o.T, w_up=w_up, w_up_t=w_up_t, w_down=w_down, w_down_t=w_down.T)


def _prep_weights_first(w_in, w_uq, w_uk, w_uv):
    c = lambda a: a.astype(MXU_DTYPE)
    w_in = c(w_in)
    z = lambda w: jnp.zeros((D_MODEL, w), MXU_DTYPE)
    r0 = Q_RANK + KV_RANK
    w_in_ext = jnp.concatenate([w_in[:, :r0], z(NOPE), w_in[:, r0:r0 + ROPE], z(32), w_in[:, r0 + ROPE:]], axis=1)
    wq = jnp.pad(c(w_uq).transpose(1, 0, 2), ((0, 0), (0, 0), (0, QK_PAD - NOPE - ROPE)))
    wk = jnp.pad(c(w_uk).transpose(1, 0, 2), ((0, 0), (0, 0), (0, QK_PAD - NOPE)))
    wv = c(w_uv).transpose(1, 0, 2)
    t3 = lambda a: a.transpose(0, 2, 1)
    return dict(w_in=w_in_ext, w_in_t=w_in_ext.T, wq=wq, wq_t=t3(wq), wk=wk, wk_t=t3(wk), wv=wv, wv_t=t3(wv))


def _local_step(x, target, w, g_cq, g_ckv, ln1_g, ln1_b, conv_w, conv_b, ln2_g, ln2_b, comm=None):
    s = x.shape[0]
    tabs = _rope_tables(s)
    r2 = lambda a: a.reshape(1, -1)
    cb = r2(conv_b)
    dils = [d for _, d in DIL_PAIRS]

    h, qp, kp, vp = _in_proj(x, w["w_in"], tm=256)
    q, k, v, v_t = _mla_prep_fwd(h, r2(g_cq), r2(g_ckv), w["wq"], w["wk"], w["wv"], w["wv_t"], tabs, tm=256)
    if comm is None:
        o_mla_t, lse_mla = _mla_attn_fwd(q, k, v_t, t=512, g=HEADS)
    else:
        o_mla_t, lse_mla, *gathered = _mla_attn_fwd(q, k, v_t, t=512, g=HEADS, late=comm["late"])
        w = {**w, **comm["finish"](gathered)}
    o_bs, lse_bs = [], []
    for i, d in enumerate(dils):
        o_b, l_b = _dil_fwd(qp[i], kp[i], vp[i], d, name=f"dil_fwd_{d}")
        o_bs.append(o_b)
        lse_bs.append(_unperm_row(l_b, d))
    o_dil, lj = _dil_combine(o_bs, lse_bs, ts=512)
    o_mla = o_mla_t.transpose(2, 0, 1).reshape(s, HEADS * HEAD_DIM)
    x1, xhat1, rstd1 = _out_ln1(o_mla, o_dil, w["w_o"], x, r2(ln1_g), r2(ln1_b), tm=256)
    u_a, u_g, act, conv_a, gelu_g, gelu_dg = _up_gate_fwd(x1, w["w_up"], conv_w, cb, tm=256, tn=1408)
    dz2, loss, dg2, db2 = _down_ln2_loss(act, w["w_down"], x1, r2(ln2_g), r2(ln2_b), target, tm=256)

    dw_down = _mm_tn(act, dz2, name="dw_down", tm=1408, tn=D_MODEL, ts=512)
    du_a, du_g, dcw_a, dcw_g, dcb_a, dcb_g = _gate_bwd(u_a, u_g, dz2, w["w_down_t"], conv_a, gelu_g, gelu_dg, conv_w,
                                                       tm=256, tn=1408)
    dz1, dg1, db1 = _up_bwd_ln1(du_a, du_g, w["w_up_t"], dz2, xhat1, rstd1, r2(ln1_g), tm=256)
    dw_up = jnp.concatenate([_mm_tn(x1, du_a, name="dw_up_a", tm=D_MODEL, tn=1408, ts=512),
                             _mm_tn(x1, du_g, name="dw_up_g", tm=D_MODEL, tn=1408, ts=512)], axis=1)
    named_early = [("w_up", dw_up), ("w_down", dw_down)]
    swap = () if comm is None else comm["blocked"](named_early)
    do_mla, do_dil, dd_all, received = _attn_bwd_heads(dz1, w["w_o_t"], o_mla, o_dil, tm=256, swap=swap)
    dw_o = jnp.concatenate([_mm_tn(o_mla, dz1, name="dw_o_mla", tm=512, tn=D_MODEL, ts=512),
                            _mm_tn(o_dil, dz1, name="dw_o_dil", tm=512, tn=D_MODEL, ts=512)], axis=0)
    dd_all = dd_all.T
    dd_mla, dd_dil = dd_all[:HEADS].reshape(HEADS, 1, s), dd_all[HEADS:].reshape(HEADS, 1, s)
    early = () if comm is None else tuple(comm["add_halves"](named_early, swap, received))
    dq, dk, dv, *early_slots = _mla_attn_bwd(q, k, v, do_mla, lse_mla, dd_mla, t=512, g=4, early=early)
    parts = []
    for i, d in enumerate(dils):
        parts.append(_dil_bwd(qp[i], kp[i], vp[i], do_dil[i], _perm_row(lj, d), _perm_row(dd_dil, d), d, name=f"dil_bwd_{d}"))
    dh_dil = _dil_merge(parts, ts=512)
    dh_mla, dwq, dwk, dwv, dgq, dgkv = _mla_prep_bwd(h, dq, dk, dv, r2(g_cq), r2(g_ckv),
                                                     w["wq_t"], w["wk_t"], w["wv_t"], tabs, tm=256)
    mla_w = 4 * LANES
    w_in_t = w["w_in_t"]
    grad_x = _mm_nn(dh_mla, w_in_t[:mla_w], name="in_bwd_mla", tm=512, tn=D_MODEL, tk=mla_w, add=dz1, add_scale=DN_ALPHA)
    grad_x = _mm_nn(dh_dil, w_in_t[mla_w:], name="in_bwd_dil", tm=512, tn=D_MODEL, tk=512, add=grad_x)
    dw_mla = _mm_tn(x, dh_mla, name="dw_in_mla", tm=D_MODEL, tn=mla_w, ts=512)
    dw_dil = _mm_tn(x, dh_dil, name="dw_in_dil", tm=D_MODEL, tn=512, ts=512)
    r0 = Q_RANK + KV_RANK
    grads = dict(
        w_in=jnp.concatenate([dw_mla[:, :r0], dw_mla[:, r0 + NOPE:r0 + NOPE + ROPE], dw_dil], axis=1),
        g_cq=dgq[0], g_ckv=dgkv[0],
        w_uq=dwq[:, :, :NOPE + ROPE].transpose(1, 0, 2),
        w_uk=dwk[:, :, :NOPE].transpose(1, 0, 2),
        w_uv=dwv.transpose(1, 0, 2),
        w_o=dw_o, ln1_g=dg1[0], ln1_b=db1[0], w_up=dw_up,
        conv_w=jnp.concatenate([dcw_a, dcw_g], axis=1), conv_b=jnp.concatenate([dcb_a, dcb_g], axis=1)[0],
        w_down=dw_down, ln2_g=dg2[0], ln2_b=db2[0])
    if comm is not None:
        grads["early"] = (early, tuple(early_slots))
    return loss[0, 0], grad_x, grads


N_CHIPS = 4
SHARDED = ("w_in", "w_uq", "w_o", "w_up", "conv_w", "w_down")
COL_SHARDED = ("w_in", "w_up", "conv_w")
SHARD_SHAPE = dict(w_in=(D_MODEL, IN_WIDTH // 4), w_uq=(Q_RANK // 4, HEADS, NOPE + ROPE), w_o=(D_MODEL // 4, D_MODEL),
                   w_up=(D_MODEL, 2 * D_FF // 4), conv_w=(3, 2 * D_FF // 4), w_down=(D_FF // 4, D_MODEL))
SMALL = ("g_cq", "g_ckv", "w_uk", "w_uv", "ln1_g", "ln1_b", "conv_b", "ln2_g", "ln2_b")
SMALL_SHAPE = dict(g_cq=(Q_RANK,), g_ckv=(KV_RANK,), w_uk=(KV_RANK, HEADS, NOPE), w_uv=(KV_RANK, HEADS, HEAD_DIM),
                   ln1_g=(D_MODEL,), ln1_b=(D_MODEL,), conv_b=(2 * D_FF,), ln2_g=(D_MODEL,), ln2_b=(D_MODEL,))
BIG = ("w_in", "w_uq", "w_o", "w_up", "w_down")
BIG_2D = dict(w_in=(D_MODEL, IN_WIDTH // 4), w_uq=(Q_RANK // 4, HEADS * (NOPE + ROPE)), w_o=(D_MODEL // 4, D_MODEL),
              w_up=(D_MODEL, 2 * D_FF // 4), w_down=(D_FF // 4, D_MODEL))
SMALL_G = SMALL + ("conv_w",)
SMALL_G_SHAPE = {**SMALL_SHAPE, "conv_w": (3, 2 * D_FF)}
SMALL_U_SHAPE = {**SMALL_SHAPE, "conv_w": (3, 2 * D_FF // 4)}


def _size(shape):
    return math.prod(shape)


def _padded_rows(n_elems, mult):
    return -(-n_elems // (LANES * mult)) * mult


SHARD_ROWS = {n: _padded_rows(_size(SHARD_SHAPE[n]), SUBLANES) for n in SHARDED}
R_SMALL = -(-sum(_size(SMALL_G_SHAPE[n]) for n in SMALL_G) // (LANES * LANES)) * LANES
GATHER_FIRST = ("w_in", "w_uq")
GATHER_LATE = ("w_o", "w_up", "w_down")
REDUCED_EARLY = ("w_up", "w_down")
REDUCED_LAST = ("w_in", "w_uq", "w_o")


def _rows(a, rows=None):
    flat = a.reshape(-1)
    rows = -(-flat.shape[0] // LANES) if rows is None else rows
    return jnp.pad(flat, (0, rows * LANES - flat.shape[0])).reshape(rows, LANES)


def _blocked(name, g):
    r, c = BIG_2D[name]
    a = g.reshape(r, N_CHIPS, c).transpose(1, 0, 2) if name in COL_SHARDED else g.reshape(N_CHIPS, r, c)
    return a.reshape(N_CHIPS, 2, r // 2, c)


def _pack_flat(t, names):
    return _rows(jnp.concatenate([t[n].astype(F32).reshape(-1) for n in names]), R_SMALL)


def _unpack_flat(buf, names, shapes):
    flat, out, r = buf.reshape(-1), {}, 0
    for n in names:
        out[n] = flat[r:r + _size(shapes[n])].reshape(shapes[n])
        r += _size(shapes[n])
    return out


def _from_chip_blocks(name, blocks):
    shp = SHARD_SHAPE[name]
    a = blocks.reshape(N_CHIPS, -1)[:, :_size(shp)].reshape((N_CHIPS,) + shp)
    if name in COL_SHARDED:
        return a.transpose(1, 0, 2).reshape(shp[0], N_CHIPS * shp[1])
    return a.reshape((N_CHIPS * shp[0],) + shp[1:])


ANY = pl.BlockSpec(memory_space=pl.ANY)
COMM_PARAMS = pltpu.CompilerParams(has_side_effects=True)


def _coords():
    return lax.axis_index("x"), lax.axis_index("y"), lax.axis_index("c")


def _other_chips(x, y):
    return [(1 - x, y), (x, 1 - y), (1 - x, 1 - y)]


def _remote(src, dst, send_sems, recv_sems, k, to):
    return pltpu.make_async_remote_copy(src_ref=src, dst_ref=dst, send_sem=send_sems.at[k], recv_sem=recv_sems.at[k],
                                        device_id=to, device_id_type=MESH)


def _gather_in_steps(wp_refs, wout_refs, send_sems, recv_sems, *, first, mid, last):
    x, y, c = _coords()
    me = 2 * x + y
    sib = (x, y, 1 - c)
    chips = _other_chips(x, y)
    n = len(wp_refs)
    pairs = [(j, t, px, py) for j, (px, py) in enumerate(chips) for t in range(n)]
    ici = [_remote(wp_refs[t].at[c], wout_refs[t].at[me, c], send_sems, recv_sems, j * n + t, (px, py, c))
           for j, t, px, py in pairs]
    fwd = [_remote(wout_refs[t].at[2 * px + py, c], wout_refs[t].at[2 * px + py, c], send_sems, recv_sems, (3 + j) * n + t, sib)
           for j, t, px, py in pairs]

    @pl.when(first)
    def _():
        for cp in ici:
            cp.start()

    @pl.when(mid)
    def _():
        for i, (j, t, px, py) in enumerate(pairs):
            _remote(wp_refs[t].at[c], wout_refs[t].at[2 * px + py, c], send_sems, recv_sems, j * n + t, (px, py, c)).wait_recv()
            fwd[i].start()

    @pl.when(last)
    def _():
        for j, t, px, py in pairs:
            k = 2 * px + py
            _remote(wout_refs[t].at[k, 1 - c], wout_refs[t].at[k, 1 - c], send_sems, recv_sems, (3 + j) * n + t, sib).wait_recv()
        for cp in ici + fwd:
            cp.wait_send()


def _swap_halves_in_steps(gs_refs, os_refs, send_sems, recv_sems, *, first, last):
    x, y, c = _coords()
    sib = (x, y, 1 - c)
    cps = [_remote(gs_refs[t].at[k, 1 - c], os_refs[t].at[k], send_sems, recv_sems, t * N_CHIPS + k, sib)
           for t in range(len(gs_refs)) for k in range(N_CHIPS)]

    @pl.when(first)
    def _():
        for cp in cps:
            cp.start()

    @pl.when(last)
    def _():
        for cp in cps:
            cp.wait_recv()
        for cp in cps:
            cp.wait_send()


def _exchange_in_steps(ps_refs, ss_refs, send_sems, recv_sems, *, first, last):
    x, y, c = _coords()
    me = 2 * x + y
    chips = _other_chips(x, y)
    n = len(ps_refs)
    sends = [_remote(ps_refs[t].at[2 * px + py], ss_refs[t].at[me], send_sems, recv_sems, j * n + t, (px, py, c))
             for j, (px, py) in enumerate(chips) for t in range(n)]

    @pl.when(first)
    def _():
        for cp in sends:
            cp.start()

    @pl.when(last)
    def _():
        for j, (px, py) in enumerate(chips):
            for t in range(n):
                _remote(ps_refs[t].at[me], ss_refs[t].at[2 * px + py], send_sems, recv_sems, j * n + t, (px, py, c)).wait_recv()
        for cp in sends:
            cp.wait_send()


def _gather_weights(wp, cwp):
    def body(wp_ref, cw_ref, wout_ref, cwout_ref, send_sems, recv_sems):
        x, y, c = _coords()
        me = 2 * x + y
        sib = (x, y, 1 - c)
        chips = _other_chips(x, y)
        sends = [_remote(wp_ref.at[c], wout_ref.at[me, c], send_sems, recv_sems, j, (px, py, c))
                 for j, (px, py) in enumerate(chips)]
        sends += [_remote(cw_ref, cwout_ref.at[me], send_sems, recv_sems, 3 + j, (px, py, c))
                  for j, (px, py) in enumerate(chips)]
        for cp in sends:
            cp.start()
        for j, (px, py) in enumerate(chips):
            k = 2 * px + py
            _remote(wp_ref.at[c], wout_ref.at[k, c], send_sems, recv_sems, j, (px, py, c)).wait_recv()
            fwd = _remote(wout_ref.at[k, c], wout_ref.at[k, c], send_sems, recv_sems, 6 + j, sib)
            fwd.start()
            sends.append(fwd)
        for j, (px, py) in enumerate(chips):
            k = 2 * px + py
            _remote(cw_ref, cwout_ref.at[k], send_sems, recv_sems, 3 + j, (px, py, c)).wait_recv()
            _remote(wout_ref.at[k, 1 - c], wout_ref.at[k, 1 - c], send_sems, recv_sems, 6 + j, sib).wait_recv()
        for cp in sends:
            cp.wait_send()

    return pl.pallas_call(
        body, name="gather_weights",
        out_shape=(jax.ShapeDtypeStruct((N_CHIPS,) + wp.shape, wp.dtype), jax.ShapeDtypeStruct((N_CHIPS,) + cwp.shape, cwp.dtype)),
        in_specs=[ANY, ANY], out_specs=(ANY, ANY),
        scratch_shapes=[pltpu.SemaphoreType.DMA((9,)), pltpu.SemaphoreType.DMA((9,))],
        compiler_params=COMM_PARAMS,
    )(wp, cwp)


def _exchange_sibling_halves(gs, whole, *, name):
    n, nw = len(gs), len(whole)

    def body(*refs):
        gs_refs, wh_refs = refs[:n], refs[n:n + nw]
        os_refs, ow_refs = refs[n + nw:2 * n + nw], refs[2 * n + nw:2 * (n + nw)]
        send_sems, recv_sems = refs[2 * (n + nw):]
        x, y, c = _coords()
        sib = (x, y, 1 - c)
        cps = [_remote(gs_refs[t].at[k, 1 - c], os_refs[t].at[k], send_sems, recv_sems, t * N_CHIPS + k, sib)
               for t in range(n) for k in range(N_CHIPS)]
        cps += [_remote(wh_refs[t], ow_refs[t], send_sems, recv_sems, n * N_CHIPS + t, sib) for t in range(nw)]
        for cp in cps:
            cp.start()
        for cp in cps:
            cp.wait_recv()
        for cp in cps:
            cp.wait_send()

    n_sem = n * N_CHIPS + nw
    return pl.pallas_call(
        body, name=name,
        out_shape=tuple(jax.ShapeDtypeStruct((N_CHIPS,) + a.shape[2:], F32) for a in gs)
        + tuple(jax.ShapeDtypeStruct(a.shape, F32) for a in whole),
        in_specs=[ANY] * (n + nw), out_specs=(ANY,) * (n + nw),
        scratch_shapes=[pltpu.SemaphoreType.DMA((n_sem,)), pltpu.SemaphoreType.DMA((n_sem,))],
        compiler_params=COMM_PARAMS,
    )(*gs, *whole)


def _exchange_chips(ps, pr):
    n = len(ps)

    def body(*refs):
        ps_refs, pr_ref, ss_refs, sr_ref = refs[:n], refs[n], refs[n + 1:2 * n + 1], refs[2 * n + 1]
        send_sems, recv_sems = refs[2 * n + 2:]
        x, y, c = _coords()
        me = 2 * x + y
        chips = _other_chips(x, y)
        sends = []
        for j, (px, py) in enumerate(chips):
            to = (px, py, c)
            for t in range(n):
                sends.append(_remote(ps_refs[t].at[2 * px + py], ss_refs[t].at[me], send_sems, recv_sems, j * (n + 1) + t, to))
            sends.append(_remote(pr_ref, sr_ref.at[me], send_sems, recv_sems, j * (n + 1) + n, to))
        for cp in sends:
            cp.start()
        for j, (px, py) in enumerate(chips):
            k, to = 2 * px + py, (px, py, c)
            for t in range(n):
                _remote(ps_refs[t].at[me], ss_refs[t].at[k], send_sems, recv_sems, j * (n + 1) + t, to).wait_recv()
            _remote(pr_ref, sr_ref.at[k], send_sems, recv_sems, j * (n + 1) + n, to).wait_recv()
        for cp in sends:
            cp.wait_send()

    n_sem = 3 * (n + 1)
    return pl.pallas_call(
        body, name="exchange_chips",
        out_shape=tuple(jax.ShapeDtypeStruct(a.shape, a.dtype) for a in ps) + (jax.ShapeDtypeStruct((N_CHIPS,) + pr.shape, F32),),
        in_specs=[ANY] * (n + 1), out_specs=(ANY,) * (n + 1),
        scratch_shapes=[pltpu.SemaphoreType.DMA((n_sem,)), pltpu.SemaphoreType.DMA((n_sem,))],
        compiler_params=COMM_PARAMS,
    )(*ps, pr)


def _exchange_sibling_result(gh):
    n = len(gh)

    def body(*refs):
        gh_refs, out_refs, (send_sems, recv_sems) = refs[:n], refs[n:2 * n], refs[2 * n:]
        x, y, c = _coords()
        cps = [_remote(gh_refs[t], out_refs[t], send_sems, recv_sems, t, (x, y, 1 - c)) for t in range(n)]
        for cp in cps:
            cp.start()
        for cp in cps:
            cp.wait_recv()
        for cp in cps:
            cp.wait_send()

    return pl.pallas_call(
        body, name="exchange_sibling_result",
        out_shape=tuple(jax.ShapeDtypeStruct(a.shape, F32) for a in gh),
        in_specs=[ANY] * n, out_specs=(ANY,) * n,
        scratch_shapes=[pltpu.SemaphoreType.DMA((n,)), pltpu.SemaphoreType.DMA((n,))],
        compiler_params=COMM_PARAMS,
    )(*gh)


def _add_own_half(gs, recv, c_arr, *, name):
    _, rows, cols = recv.shape

    def body(c_ref, a_ref, b_ref, o_ref):
        o_ref[0] = (a_ref[0, 0] + b_ref[0]).astype(o_ref.dtype)

    return pl.pallas_call(
        body, name=name,
        out_shape=jax.ShapeDtypeStruct(recv.shape, GRAD_WIRE_DTYPE),
        grid_spec=pltpu.PrefetchScalarGridSpec(
            num_scalar_prefetch=1, grid=(N_CHIPS,),
            in_specs=[pl.BlockSpec((1, 1, rows, cols), lambda k, c_ref: (k, c_ref[0], 0, 0)),
                      pl.BlockSpec((1, rows, cols), lambda k, c_ref: (k, 0, 0))],
            out_specs=pl.BlockSpec((1, rows, cols), lambda k, c_ref: (k, 0, 0))),
        compiler_params=_params("parallel"),
    )(c_arr, gs, recv)


def _add2(a, b, *, name):
    def body(a_ref, b_ref, o_ref):
        o_ref[...] = a_ref[...] + b_ref[...]

    return pl.pallas_call(body, name=name, out_shape=jax.ShapeDtypeStruct(a.shape, F32))(a, b)


def _sum_slots(slots, *, tr, name):
    _, r, c = slots.shape

    def body(s_ref, o_ref):
        f = lambda k: s_ref[k].astype(F32)
        o_ref[...] = ((f(0) + f(1)) + f(2)) + f(3)

    return pl.pallas_call(
        body, name=name,
        out_shape=jax.ShapeDtypeStruct((r, c), F32),
        grid=(r // tr,),
        in_specs=[pl.BlockSpec((N_CHIPS, tr, c), lambda i: (0, i, 0))],
        out_specs=pl.BlockSpec((tr, c), lambda i: (i, 0)),
        compiler_params=_params("parallel"),
    )(slots)


def _adamw(w, g, m, v, *, tr, name):
    r, cols = w.shape

    def body(w_ref, g_ref, m_ref, v_ref, d_ref, nm_ref, nv_ref):
        g_ = g_ref[...]
        m_ = ADAM_B1 * m_ref[...] + (1.0 - ADAM_B1) * g_
        v_ = ADAM_B2 * v_ref[...] + (1.0 - ADAM_B2) * (g_ * g_)
        m_hat = m_ / (1.0 - ADAM_B1 ** ADAM_STEP)
        v_hat = v_ / (1.0 - ADAM_B2 ** ADAM_STEP)
        d_ref[...] = -ADAM_LR * (m_hat / (jnp.sqrt(v_hat) + ADAM_EPS) + ADAM_WD * w_ref[...])
        nm_ref[...] = m_
        nv_ref[...] = v_

    spec = pl.BlockSpec((tr, cols), lambda i: (i, 0))
    out = jax.ShapeDtypeStruct((r, cols), F32)
    return pl.pallas_call(
        body, name=name, out_shape=(out, out, out), grid=(r // tr,),
        in_specs=[spec] * 4, out_specs=(spec,) * 3,
        compiler_params=_params("parallel"),
    )(w, g, m, v)


WEIGHTS = ("w_in", "g_cq", "g_ckv", "w_uq", "w_uk", "w_uv", "w_o", "ln1_g", "ln1_b", "w_up", "conv_w", "conv_b",
           "w_down", "ln2_g", "ln2_b")


def kernel(x, w_in, g_cq, g_ckv, w_uq, w_uk, w_uv, w_o, ln1_g, ln1_b, w_up, conv_w, conv_b, w_down, ln2_g, ln2_b, loss_target, m_w_in, m_g_cq, m_g_ckv, m_w_uq, m_w_uk, m_w_uv, m_w_o, m_ln1_g, m_ln1_b, m_w_up, m_conv_w, m_conv_b, m_w_down, m_ln2_g, m_ln2_b, v_w_in, v_g_cq, v_g_ckv, v_w_uq, v_w_uk, v_w_uv, v_w_o, v_ln1_g, v_ln1_b, v_w_up, v_conv_w, v_conv_b, v_w_down, v_ln2_g, v_ln2_b):
    wts = dict(zip(WEIGHTS, (w_in, g_cq, g_ckv, w_uq, w_uk, w_uv, w_o, ln1_g, ln1_b, w_up, conv_w, conv_b, w_down, ln2_g, ln2_b)))
    mom = dict(zip(WEIGHTS, (m_w_in, m_g_cq, m_g_ckv, m_w_uq, m_w_uk, m_w_uv, m_w_o, m_ln1_g, m_ln1_b, m_w_up, m_conv_w, m_conv_b, m_w_down, m_ln2_g, m_ln2_b)))
    var = dict(zip(WEIGHTS, (v_w_in, v_g_cq, v_g_ckv, v_w_uq, v_w_uk, v_w_uv, v_w_o, v_ln1_g, v_ln1_b, v_w_up, v_conv_w, v_conv_b, v_w_down, v_ln2_g, v_ln2_b)))

    me = 2 * lax.axis_index("x") + lax.axis_index("y")
    my_c = lax.axis_index("c")
    c_arr = my_c.astype(jnp.int32).reshape(1)
    own = lambda slots, mine: lax.dynamic_update_index_in_dim(slots, mine, me, 0)

    def pack(names):
        return jnp.concatenate([_rows(_mx(wts[n]), SHARD_ROWS[n]) for n in names], axis=0).reshape(2, -1, LANES)

    def unpack(names, gathered, mine):
        buf, full, r = own(gathered, mine).reshape(N_CHIPS, -1, LANES), {}, 0
        for n in names:
            full[n] = _from_chip_blocks(n, buf[:, r:r + SHARD_ROWS[n]])
            r += SHARD_ROWS[n]
        return full

    wp_first = pack(GATHER_FIRST)
    cwp = _rows(conv_w, SHARD_ROWS["conv_w"])
    gathered, cwfull = _gather_weights(wp_first, cwp)
    full = unpack(GATHER_FIRST, gathered, wp_first)
    conv_w_full = _from_chip_blocks("conv_w", own(cwfull, cwp))
    w = _prep_weights_first(full["w_in"], full["w_uq"], w_uk, w_uv)
    late_halves = [_mx(wts[n]).reshape(2, BIG_2D[n][0] // 2, BIG_2D[n][1]) for n in GATHER_LATE]

    def finish(gathered_late):
        w_o_b, w_up_b, w_down_b = (own(a, mine).reshape((N_CHIPS,) + BIG_2D[n])
                                   for a, mine, n in zip(gathered_late, late_halves, GATHER_LATE))
        return _prep_weights_late(w_o_b.reshape(D_MODEL, D_MODEL), w_up_b, w_down_b.reshape(D_FF, D_MODEL))

    def blocked(named):
        return [_blocked(n, a) for n, a in named]

    def add_halves(named, gb, recv):
        return [_add_own_half(gb[i], recv[i], c_arr, name=f"add_half_{n}") for i, (n, _) in enumerate(named)]

    def halve(named, whole):
        gb = blocked(named)
        recv = _exchange_sibling_halves(gb, list(whole), name="exchange_sibling_halves")
        return add_halves(named, gb, recv) + [_add2(a, recv[len(gb) + i], name=f"add_whole_{i}") for i, a in enumerate(whole)]

    comm = dict(late=late_halves, finish=finish, blocked=blocked, add_halves=add_halves)
    loss, grad_x, g = _local_step(x[0], loss_target[0], w, g_cq, g_ckv, ln1_g, ln1_b, conv_w_full, conv_b, ln2_g, ln2_b, comm=comm)

    ps_early, slots_early = g.pop("early")
    g["loss"] = loss.reshape(1)
    *ps_rest, pr = halve([(n, g[n]) for n in REDUCED_LAST], whole=[_pack_flat(g, SMALL_G + ("loss",))])
    *slots_rest, slots_r = _exchange_chips(ps_rest, pr)
    ps = {**dict(zip(REDUCED_LAST, ps_rest)), **dict(zip(REDUCED_EARLY, ps_early))}
    slots = {**dict(zip(REDUCED_LAST, slots_rest)), **dict(zip(REDUCED_EARLY, slots_early))}
    slots = [own(slots[n], lax.dynamic_index_in_dim(ps[n], me, 0, keepdims=False)) for n in BIG]
    slots_r = own(slots_r, pr)
    g_half = [_sum_slots(slots[i], tr=slots[i].shape[1] // 2, name=f"sum_chips_{n}") for i, n in enumerate(BIG)]
    g_small = _unpack_flat(_sum_slots(slots_r, tr=R_SMALL, name="sum_chips_small"), SMALL_G + ("loss",),
                           {**SMALL_G_SHAPE, "loss": (1,)})
    loss = g_small.pop("loss")[0]
    g_other = _exchange_sibling_result(g_half)
    grads = {n: jnp.where(my_c == 0, jnp.concatenate([g_half[i], g_other[i]]), jnp.concatenate([g_other[i], g_half[i]]))
             for i, n in enumerate(BIG)}
    g_small["conv_w"] = lax.dynamic_slice_in_dim(g_small["conv_w"], me * SHARD_SHAPE["conv_w"][1], SHARD_SHAPE["conv_w"][1], 1)
    grads.update(g_small)

    res = {}
    for n in BIG:
        as2d = lambda a: a.reshape(BIG_2D[n])
        d, m, v = _adamw(as2d(wts[n]), grads[n], as2d(mom[n]), as2d(var[n]), tr=BIG_2D[n][0] // 4, name=f"adamw_{n}")
        res[n] = [a.reshape(SHARD_SHAPE[n]) for a in (grads[n], d, m, v)]
    flat = lambda t: _pack_flat(t, SMALL_G)
    dmv = _adamw(flat(wts), flat(g_small), flat(mom), flat(var), tr=R_SMALL, name="adamw_small")
    dmv = [_unpack_flat(a, SMALL_G, SMALL_U_SHAPE) for a in dmv]
    for n in SMALL_G:
        res[n] = [g_small[n]] + [t[n] for t in dmv]
    outs = [res[n][j] for j in range(4) for n in WEIGHTS]
    return (loss, grad_x[None], *outs)
```

```python
import math

import jax
import jax.numpy as jnp
from jax import lax
from jax.experimental import pallas as pl
from jax.experimental.pallas import tpu as pltpu

F32 = jnp.float32
MXU_DTYPE = jnp.bfloat16
GRAD_WIRE_DTYPE = jnp.bfloat16
NEG = -1e30

D_MODEL = 1024
HEADS = 8
HEAD_DIM = 64
Q_RANK = 256
KV_RANK = 128
NOPE = 64
ROPE = 32
QK_PAD = 128
IN_WIDTH = 1952
IN_EXT = 2048
D_FF = 2816
DIL_PAIRS = ((128, 1), (512, 4), (2048, 16))
DIL_BLOCK = 128
ROPE_THETA = 10000.0
DN_ALPHA = 2.0 ** 0.25
LN_EPS = 1e-5
RMS_EPS = 1e-6
MLA_SCALE = 1.0 / math.sqrt(NOPE + ROPE)
LOG2_E = math.log2(math.e)
DIL_SCALE = 1.0 / math.sqrt(HEAD_DIM)

ADAM_LR = 0.001
ADAM_B1 = 0.9
ADAM_B2 = 0.999
ADAM_EPS = 1e-08
ADAM_WD = 0.01
ADAM_STEP = 10

LANES = 128
SUBLANES = 8
VMEM_LIMIT_BYTES = 56 * 1024 * 1024
DW_TOKENS = 1024

MESH = pl.DeviceIdType.MESH


def _params(*sem):
    return pltpu.CompilerParams(dimension_semantics=sem, vmem_limit_bytes=VMEM_LIMIT_BYTES)


def _dot(a, b):
    return jnp.dot(a, b, preferred_element_type=F32)


def _dot_nt(a, b):
    return lax.dot_general(a, b, (((1,), (1,)), ((), ())), preferred_element_type=F32)


def _dot_tn(a, b):
    return lax.dot_general(a, b, (((0,), (0,)), ((), ())), preferred_element_type=F32)


def _mx(a):
    return a.astype(MXU_DTYPE)


def _mm_nn(a, b, *, name, tm, tn, tk, out_dtype=F32, add=None, add_scale=1.0):
    m, kdim = a.shape
    blocked = b.ndim == 3
    n = b.shape[0] * b.shape[2] if blocked else b.shape[1]
    nk = kdim // tk

    def body(*refs):
        if add is None:
            a_ref, b_ref, o_ref, acc = refs
        else:
            a_ref, b_ref, c_ref, o_ref, acc = refs
        k = pl.program_id(2)

        @pl.when(k == 0)
        def _():
            acc[...] = jnp.zeros_like(acc)

        acc[...] += _dot(_mx(a_ref[...]), _mx(b_ref[...]))

        @pl.when(k == nk - 1)
        def _():
            r = acc[...]
            if add is not None:
                r = r + add_scale * c_ref[...]
            o_ref[...] = r.astype(out_dtype)

    b_spec = (pl.BlockSpec((None, tk, tn), lambda i, j, k: (j, k, 0)) if blocked
              else pl.BlockSpec((tk, tn), lambda i, j, k: (k, j)))
    in_specs = [pl.BlockSpec((tm, tk), lambda i, j, k: (i, k)), b_spec]
    args = [a, b]
    if add is not None:
        in_specs.append(pl.BlockSpec((tm, tn), lambda i, j, k: (i, j)))
        args.append(add)
    return pl.pallas_call(
        body, name=name,
        out_shape=jax.ShapeDtypeStruct((m, n), out_dtype),
        grid=(m // tm, n // tn, nk),
        in_specs=in_specs,
        out_specs=pl.BlockSpec((tm, tn), lambda i, j, k: (i, j)),
        scratch_shapes=[pltpu.VMEM((tm, tn), F32)],
        compiler_params=_params("parallel", "parallel", "arbitrary"),
    )(*args)


def _mm_tn(a, b, *, name, tm, tn, ts, out_dtype=F32):
    s, m = a.shape
    n = b.shape[1]
    ns = s // ts

    def body(a_ref, b_ref, o_ref, acc):
        k = pl.program_id(2)

        @pl.when(k == 0)
        def _():
            acc[...] = jnp.zeros_like(acc)

        acc[...] += _dot_tn(_mx(a_ref[...]), _mx(b_ref[...]))

        @pl.when(k == ns - 1)
        def _():
            o_ref[...] = acc[...].astype(out_dtype)

    return pl.pallas_call(
        body, name=name,
        out_shape=jax.ShapeDtypeStruct((m, n), out_dtype),
        grid=(m // tm, n // tn, ns),
        in_specs=[pl.BlockSpec((ts, tm), lambda i, j, k: (k, i)),
                  pl.BlockSpec((ts, tn), lambda i, j, k: (k, j))],
        out_specs=pl.BlockSpec((tm, tn), lambda i, j, k: (i, j)),
        scratch_shapes=[pltpu.VMEM((tm, tn), F32)],
        compiler_params=_params("parallel", "parallel", "arbitrary"),
    )(a, b)


def _in_proj(x, w_in_ext, *, tm):
    s = x.shape[0]
    mla_w = 4 * LANES
    dil_w = HEADS * HEAD_DIM
    dils = [d for _, d in DIL_PAIRS]

    def body(x_ref, w_ref, h_ref, *rest):
        outs, sc = rest[:-1], rest[-1]
        xb = _mx(x_ref[...])
        h_ref[...] = _dot(xb, w_ref[:, 0:mla_w])
        for j in range(3):
            part = _dot(xb, w_ref[:, mla_w + j * dil_w:mla_w + (j + 1) * dil_w])
            for hd in range(HEADS):
                sc[hd] = part[:, hd * HEAD_DIM:(hd + 1) * HEAD_DIM]
            for b, d in enumerate(dils):
                _store_residue_major(outs[3 * j + b], sc, d, tm)

    shapes, specs = _residue_major_outs(s, tm, dils, MXU_DTYPE)
    res = pl.pallas_call(
        body, name="in_proj",
        out_shape=(jax.ShapeDtypeStruct((s, mla_w), F32),) + shapes * 3,
        grid=(s // tm,),
        in_specs=[pl.BlockSpec((tm, D_MODEL), lambda i: (i, 0)), pl.BlockSpec((D_MODEL, IN_EXT), lambda i: (0, 0))],
        out_specs=(pl.BlockSpec((tm, mla_w), lambda i: (i, 0)),) + specs * 3,
        scratch_shapes=[pltpu.VMEM((HEADS, tm, HEAD_DIM), F32)],
        compiler_params=_params("parallel"),
    )(x, w_in_ext)
    hm = lambda a: a.reshape(HEADS, s, HEAD_DIM)
    return res[0], [hm(a) for a in res[1:4]], [hm(a) for a in res[4:7]], [hm(a) for a in res[7:10]]


def _residue_major_outs(s, tm, dils, dtype):
    shapes, specs = [], []
    for d in dils:
        if d == 1:
            shapes.append(jax.ShapeDtypeStruct((HEADS, s, HEAD_DIM), dtype))
            specs.append(pl.BlockSpec((HEADS, tm, HEAD_DIM), lambda i: (0, i, 0)))
        else:
            shapes.append(jax.ShapeDtypeStruct((HEADS, d, s // d, HEAD_DIM), dtype))
            specs.append(pl.BlockSpec((HEADS, d, tm // d, HEAD_DIM), lambda i: (0, 0, i, 0)))
    return tuple(shapes), tuple(specs)


def _store_residue_major(o_ref, src_ref, d, tm):
    if d == 1:
        o_ref[...] = src_ref[...].astype(o_ref.dtype)
    else:
        for r in range(d):
            o_ref[:, r] = src_ref[:, pl.ds(r, tm // d, stride=d), :].astype(o_ref.dtype)


def _load_token_order(dst_ref, src_ref, d, tm, accumulate=False):
    if d == 1:
        dst_ref[...] = dst_ref[...] + src_ref[...] if accumulate else src_ref[...]
    else:
        for r in range(d):
            rows = pl.ds(r, tm // d, stride=d)
            dst_ref[:, rows, :] = dst_ref[:, rows, :] + src_ref[:, r] if accumulate else src_ref[:, r]


def _attn_bwd_heads(dz1, w_o_t, a_mla, a_dil, *, tm, swap=()):
    s = dz1.shape[0]
    half = HEADS * HEAD_DIM
    dils = [d for _, d in DIL_PAIRS]
    nsw = len(swap)
    n_steps = s // tm

    def body(*refs):
        dz_ref, w_ref, am_ref, ad_ref = refs[:4]
        gs_refs = refs[4:4 + nsw]
        dom_ref, dd_ref = refs[4 + nsw:6 + nsw]
        dod_refs = refs[6 + nsw:6 + nsw + len(dils)]
        os_refs = refs[6 + nsw + len(dils):6 + 2 * nsw + len(dils)]
        if nsw:
            send_sems, recv_sems = refs[6 + 2 * nsw + len(dils):]
            i = pl.program_id(0)
            _swap_halves_in_steps(gs_refs, os_refs, send_sems, recv_sems, first=i == 0, last=i == n_steps - 1)
        dzb = _mx(dz_ref[...])
        for j, (a_ref, o_ref) in enumerate(((am_ref, dom_ref), (ad_ref, dod_refs[0]))):
            da = _dot(dzb, w_ref[:, j * half:(j + 1) * half])
            prod = da * a_ref[...]
            for hd in range(HEADS):
                sl = slice(hd * HEAD_DIM, (hd + 1) * HEAD_DIM)
                o_ref[hd] = da[:, sl].astype(o_ref.dtype)
                dd_ref[:, j * HEADS + hd:j * HEADS + hd + 1] = jnp.sum(prod[:, sl], axis=-1, keepdims=True)
        for b, d in enumerate(dils[1:]):
            _store_residue_major(dod_refs[1 + b], dod_refs[0], d, tm)

    hspec = pl.BlockSpec((HEADS, tm, HEAD_DIM), lambda i: (0, i, 0))
    row = lambda w: pl.BlockSpec((tm, w), lambda i: (i, 0))
    shapes, specs = _residue_major_outs(s, tm, dils, F32)
    n_sem = nsw * N_CHIPS
    do_mla, dd, *rest = pl.pallas_call(
        body, name="attn_bwd_heads",
        out_shape=(jax.ShapeDtypeStruct((HEADS, s, HEAD_DIM), MXU_DTYPE), jax.ShapeDtypeStruct((s, 2 * HEADS), F32)) + shapes
        + tuple(jax.ShapeDtypeStruct((N_CHIPS,) + a.shape[2:], F32) for a in swap),
        grid=(n_steps,),
        in_specs=[row(D_MODEL), pl.BlockSpec((D_MODEL, D_MODEL), lambda i: (0, 0)), row(half), row(half)] + [ANY] * nsw,
        out_specs=(hspec, row(2 * HEADS)) + specs + (ANY,) * nsw,
        scratch_shapes=[pltpu.SemaphoreType.DMA((n_sem,)), pltpu.SemaphoreType.DMA((n_sem,))] if nsw else [],
        compiler_params=pltpu.CompilerParams(dimension_semantics=("arbitrary",), vmem_limit_bytes=VMEM_LIMIT_BYTES,
                                             has_side_effects=nsw > 0),
    )(dz1, w_o_t, a_mla, a_dil, *swap)
    do_dil, received = rest[:len(dils)], rest[len(dils):]
    return do_mla, [a.reshape(HEADS, s, HEAD_DIM) for a in do_dil], dd, received


def _dil_merge(parts, *, ts):
    hds, s, e = parts[0][0].shape
    dils = [d for _, d in DIL_PAIRS]

    def body(*refs):
        o_ref, sc = refs[9], refs[10]
        for j in range(3):
            for b, d in enumerate(dils):
                _load_token_order(sc, refs[3 * b + j], d, ts, accumulate=b > 0)
            tot = sc[...]
            for hd in range(hds):
                col = j * hds * e + hd * e
                o_ref[:, col:col + e] = tot[hd].astype(o_ref.dtype)

    _, specs = _residue_major_outs(s, ts, dils, F32)
    view = lambda a, d: a if d == 1 else a.reshape(hds, d, s // d, e)
    return pl.pallas_call(
        body, name="dil_merge",
        out_shape=jax.ShapeDtypeStruct((s, 3 * hds * e), MXU_DTYPE),
        grid=(s // ts,),
        in_specs=[specs[b] for b in range(3) for _ in range(3)],
        out_specs=pl.BlockSpec((ts, 3 * hds * e), lambda i: (i, 0)),
        scratch_shapes=[pltpu.VMEM((hds, ts, e), F32)],
        compiler_params=_params("parallel"),
    )(*[view(parts[b][j], dils[b]) for b in range(3) for j in range(3)])


def _rope_tables(s):
    half = ROPE // 2
    freqs = ROPE_THETA ** (-jnp.arange(half, dtype=F32) / half)
    ang = jnp.arange(s).astype(F32)[:, None] * freqs[None, :]
    cos, sin = jnp.cos(ang), jnp.sin(ang)
    z = lambda w: jnp.zeros((s, w), F32)
    c = jnp.concatenate([jnp.ones((s, NOPE), F32), cos, cos, z(32)], axis=1)
    s1 = jnp.concatenate([z(NOPE + half), sin, z(32)], axis=1)
    s2 = jnp.concatenate([z(NOPE), -sin, z(half + 32)], axis=1)
    mask = jnp.concatenate([z(NOPE), jnp.ones((s, ROPE), F32), z(32)], axis=1)
    return c, s1, s2, mask


def _rope(x, c, s1, s2):
    return x * c + pltpu.roll(x, 16, 1) * s1 + pltpu.roll(x, LANES - 16, 1) * s2


def _unrope(dy, c, s1, s2):
    return dy * c + pltpu.roll(dy * s1, LANES - 16, 1) + pltpu.roll(dy * s2, 16, 1)


def _rms(x):
    r = lax.rsqrt(jnp.mean(x * x, axis=-1, keepdims=True) + RMS_EPS)
    return x * r, r


def _mla_prep_fwd(h, g_cq, g_ckv, wq, wk, wv, wv_t, tabs, *, tm):
    s = h.shape[0]
    c_t, s1_t, s2_t, _ = tabs

    def body(h_ref, gq_ref, gkv_ref, wq_ref, wk_ref, wv_ref, wvt_ref, c_ref, s1_ref, s2_ref,
             q_ref, k_ref, v_ref, vt_ref):
        cq = h_ref[:, 0:Q_RANK]
        ckv = h_ref[:, Q_RANK:Q_RANK + KV_RANK]
        kr = h_ref[:, Q_RANK + KV_RANK:Q_RANK + KV_RANK + QK_PAD]
        c, s1, s2 = c_ref[...], s1_ref[...], s2_ref[...]
        cqn = _mx(_rms(cq)[0] * gq_ref[...])
        ckvn = _mx(_rms(ckv)[0] * gkv_ref[...])
        kr_rot = _rope(kr, c, s1, s2)
        for hd in range(HEADS):
            q_ref[hd] = _rope(_dot(cqn, wq_ref[hd]), c, s1, s2).astype(q_ref.dtype)
            k_ref[hd] = (_dot(ckvn, wk_ref[hd]) + kr_rot).astype(k_ref.dtype)
            v_ref[hd] = _dot(ckvn, wv_ref[hd]).astype(v_ref.dtype)
            vt_ref[hd] = _dot_nt(wvt_ref[hd], ckvn).astype(vt_ref.dtype)

    full = lambda shp: pl.BlockSpec(shp, lambda i: (0,) * len(shp))
    row = lambda w: pl.BlockSpec((tm, w), lambda i: (i, 0))
    return pl.pallas_call(
        body, name="mla_prep_fwd",
        out_shape=(jax.ShapeDtypeStruct((HEADS, s, QK_PAD), MXU_DTYPE),
                   jax.ShapeDtypeStruct((HEADS, s, QK_PAD), MXU_DTYPE),
                   jax.ShapeDtypeStruct((HEADS, s, HEAD_DIM), MXU_DTYPE),
                   jax.ShapeDtypeStruct((HEADS, HEAD_DIM, s), MXU_DTYPE)),
        grid=(s // tm,),
        in_specs=[row(4 * LANES), full((1, Q_RANK)), full((1, KV_RANK)),
                  full((HEADS, Q_RANK, QK_PAD)), full((HEADS, KV_RANK, QK_PAD)), full((HEADS, KV_RANK, HEAD_DIM)),
                  full((HEADS, HEAD_DIM, KV_RANK)), row(LANES), row(LANES), row(LANES)],
        out_specs=(pl.BlockSpec((HEADS, tm, QK_PAD), lambda i: (0, i, 0)),
                   pl.BlockSpec((HEADS, tm, QK_PAD), lambda i: (0, i, 0)),
                   pl.BlockSpec((HEADS, tm, HEAD_DIM), lambda i: (0, i, 0)),
                   pl.BlockSpec((HEADS, HEAD_DIM, tm), lambda i: (0, 0, i))),
        compiler_params=_params("parallel"),
    )(h, g_cq, g_ckv, wq, wk, wv, wv_t, c_t, s1_t, s2_t)


def _mla_prep_bwd(h, dq, dk, dv, g_cq, g_ckv, wq_t, wk_t, wv_t, tabs, *, tm):
    s = h.shape[0]
    c_t, s1_t, s2_t, mask_t = tabs

    def body(h_ref, dq_ref, dk_ref, dv_ref, gq_ref, gkv_ref, wqt_ref, wkt_ref, wvt_ref,
             c_ref, s1_ref, s2_ref, mask_ref, dh_ref, dwq_ref, dwk_ref, dwv_ref, dgq_ref, dgkv_ref):
        i = pl.program_id(0)

        @pl.when(i == 0)
        def _():
            dwq_ref[...] = jnp.zeros_like(dwq_ref)
            dwk_ref[...] = jnp.zeros_like(dwk_ref)
            dwv_ref[...] = jnp.zeros_like(dwv_ref)
            dgq_ref[...] = jnp.zeros_like(dgq_ref)
            dgkv_ref[...] = jnp.zeros_like(dgkv_ref)

        cq = h_ref[:, 0:Q_RANK]
        ckv = h_ref[:, Q_RANK:Q_RANK + KV_RANK]
        c, s1, s2 = c_ref[...], s1_ref[...], s2_ref[...]
        cqh, rq = _rms(cq)
        ckvh, rkv = _rms(ckv)
        gq, gkv = gq_ref[...], gkv_ref[...]
        cqn = _mx(cqh * gq)
        ckvn = _mx(ckvh * gkv)
        dcqn = jnp.zeros((tm, Q_RANK), F32)
        dckvn = jnp.zeros((tm, KV_RANK), F32)
        dkr = jnp.zeros((tm, QK_PAD), F32)
        for hd in range(HEADS):
            dqh = _mx(_unrope(dq_ref[hd], c, s1, s2))
            dcqn = dcqn + _dot(dqh, wqt_ref[hd])
            dwq_ref[hd] += _dot_tn(cqn, dqh)
            dkh = dk_ref[hd]
            dkr = dkr + dkh
            dkh = _mx(dkh)
            dckvn = dckvn + _dot(dkh, wkt_ref[hd])
            dwk_ref[hd] += _dot_tn(ckvn, dkh)
            dvh = _mx(dv_ref[hd])
            dckvn = dckvn + _dot(dvh, wvt_ref[hd])
            dwv_ref[hd] += _dot_tn(ckvn, dvh)
        dgq_ref[...] += jnp.sum(dcqn * cqh, axis=0, keepdims=True)
        dgkv_ref[...] += jnp.sum(dckvn * ckvh, axis=0, keepdims=True)
        gd = dcqn * gq
        dh_ref[:, 0:Q_RANK] = rq * (gd - cqh * jnp.mean(gd * cqh, axis=-1, keepdims=True))
        gd = dckvn * gkv
        dh_ref[:, Q_RANK:Q_RANK + KV_RANK] = rkv * (gd - ckvh * jnp.mean(gd * ckvh, axis=-1, keepdims=True))
        dh_ref[:, Q_RANK + KV_RANK:Q_RANK + KV_RANK + QK_PAD] = _unrope(dkr, c, s1, s2) * mask_ref[...]

    full = lambda shp: pl.BlockSpec(shp, lambda i: (0,) * len(shp))
    row = lambda w: pl.BlockSpec((tm, w), lambda i: (i, 0))
    hrow = lambda w: pl.BlockSpec((HEADS, tm, w), lambda i: (0, i, 0))
    return pl.pallas_call(
        body, name="mla_prep_bwd",
        out_shape=(jax.ShapeDtypeStruct((s, 4 * LANES), F32),
                   jax.ShapeDtypeStruct((HEADS, Q_RANK, QK_PAD), F32),
                   jax.ShapeDtypeStruct((HEADS, KV_RANK, QK_PAD), F32),
                   jax.ShapeDtypeStruct((HEADS, KV_RANK, HEAD_DIM), F32),
                   jax.ShapeDtypeStruct((1, Q_RANK), F32),
                   jax.ShapeDtypeStruct((1, KV_RANK), F32)),
        grid=(s // tm,),
        in_specs=[row(4 * LANES), hrow(QK_PAD), hrow(QK_PAD), hrow(HEAD_DIM),
                  full((1, Q_RANK)), full((1, KV_RANK)),
                  full((HEADS, QK_PAD, Q_RANK)), full((HEADS, QK_PAD, KV_RANK)), full((HEADS, HEAD_DIM, KV_RANK)),
                  row(LANES), row(LANES), row(LANES), row(LANES)],
        out_specs=(row(4 * LANES), full((HEADS, Q_RANK, QK_PAD)), full((HEADS, KV_RANK, QK_PAD)),
                   full((HEADS, KV_RANK, HEAD_DIM)), full((1, Q_RANK)), full((1, KV_RANK))),
        compiler_params=_params("arbitrary"),
    )(h, dq, dk, dv, g_cq, g_ckv, wq_t, wk_t, wv_t, c_t, s1_t, s2_t, mask_t)


def _bdot(a, b, ca, cb):
    return lax.dot_general(a, b, (((ca,), (cb,)), ((0,), (0,))), preferred_element_type=F32)


def _causal_mask_t(t):
    kk = lax.broadcasted_iota(jnp.int32, (t, t), 0)
    qq = lax.broadcasted_iota(jnp.int32, (t, t), 1)
    return (qq >= kk)[None]


def _mla_attn_fwd(q, k, v_t, *, t, g, late=None):
    hds, s, _ = q.shape
    n = s // t
    n_groups = hds // g

    nl = 0 if late is None else len(late)

    def body(*refs):
        q_ref, k_ref, vt_ref = refs[:3]
        wp_refs = refs[3:3 + nl]
        o_ref, lse_ref = refs[3 + nl:5 + nl]
        wout_refs = refs[5 + nl:5 + 2 * nl]
        m_sc, l_sc, acc_sc = refs[5 + 2 * nl:8 + 2 * nl]
        hg, qi, ki = pl.program_id(0), pl.program_id(1), pl.program_id(2)
        if nl:
            send_sems, recv_sems = refs[8 + 2 * nl:]
            tail = jnp.logical_and(hg == n_groups - 1, qi == n - 1)
            _gather_in_steps(wp_refs, wout_refs, send_sems, recv_sems,
                             first=jnp.logical_and(hg == 0, jnp.logical_and(qi == 0, ki == 0)),
                             mid=jnp.logical_and(tail, ki == 0), last=jnp.logical_and(tail, ki == n - 1))

        @pl.when(ki == 0)
        def _():
            m_sc[...] = jnp.full_like(m_sc, NEG)
            l_sc[...] = jnp.zeros_like(l_sc)
            acc_sc[...] = jnp.zeros_like(acc_sc)

        def step(masked):
            sc = _bdot(k_ref[...], q_ref[...], 2, 2)
            if masked:
                sc = jnp.where(_causal_mask_t(t), sc, NEG)
            m_prev = m_sc[...]
            m_new = jnp.maximum(m_prev, jnp.max(sc, axis=1, keepdims=True))
            p = jnp.exp2((sc - m_new) * (MLA_SCALE * LOG2_E))
            a = jnp.exp2((m_prev - m_new) * (MLA_SCALE * LOG2_E))
            l_sc[...] = a * l_sc[...] + jnp.sum(p, axis=1, keepdims=True)
            acc_sc[...] = a * acc_sc[...] + _bdot(vt_ref[...], _mx(p), 2, 1)
            m_sc[...] = m_new

        @pl.when(ki < qi)
        def _():
            step(False)

        @pl.when(ki == qi)
        def _():
            step(True)
            o_ref[...] = acc_sc[...] / l_sc[...]
            lse_ref[...] = m_sc[...] * MLA_SCALE + jnp.log(l_sc[...])

    qspec = pl.BlockSpec((g, t, QK_PAD), lambda h, i, j: (h, i, 0))
    kspec = pl.BlockSpec((g, t, QK_PAD), lambda h, i, j: (h, jnp.minimum(i, j), 0))
    vspec = pl.BlockSpec((g, HEAD_DIM, t), lambda h, i, j: (h, 0, jnp.minimum(i, j)))
    out_shape = [jax.ShapeDtypeStruct((hds, HEAD_DIM, s), F32), jax.ShapeDtypeStruct((hds, 1, s), F32)]
    in_specs = [qspec, kspec, vspec]
    out_specs = [pl.BlockSpec((g, HEAD_DIM, t), lambda h, i, j: (h, 0, i)), pl.BlockSpec((g, 1, t), lambda h, i, j: (h, 0, i))]
    scratch = [pltpu.VMEM((g, 1, t), F32), pltpu.VMEM((g, 1, t), F32), pltpu.VMEM((g, HEAD_DIM, t), F32)]
    args = [q, k, v_t]
    if nl:
        out_shape += [jax.ShapeDtypeStruct((N_CHIPS,) + a.shape, a.dtype) for a in late]
        in_specs += [ANY] * nl
        out_specs += [ANY] * nl
        scratch += [pltpu.SemaphoreType.DMA((6 * nl,)), pltpu.SemaphoreType.DMA((6 * nl,))]
        args += list(late)
    return pl.pallas_call(
        body, name="mla_attn_fwd",
        out_shape=tuple(out_shape), grid=(n_groups, n, n),
        in_specs=in_specs, out_specs=tuple(out_specs), scratch_shapes=scratch,
        compiler_params=pltpu.CompilerParams(dimension_semantics=("arbitrary",) * 3, vmem_limit_bytes=VMEM_LIMIT_BYTES,
                                             has_side_effects=nl > 0),
    )(*args)


def _mla_attn_bwd(q, k, v, do, lse, dd, *, t, g, early=()):
    hds, s, _ = q.shape
    n = s // t
    n_groups = hds // g
    ne = len(early)

    def body(*refs):
        q_ref, k_ref, v_ref, do_ref, lse_ref, dd_ref = refs[:6]
        ps_refs = refs[6:6 + ne]
        dq_ref, dk_ref, dv_ref = refs[6 + ne:9 + ne]
        ss_refs = refs[9 + ne:9 + 2 * ne]
        dq_sc, dk_sc, dv_sc = refs[9 + 2 * ne:12 + 2 * ne]
        hg, ki, qi = pl.program_id(0), pl.program_id(1), pl.program_id(2)
        if ne:
            send_sems, recv_sems = refs[12 + 2 * ne:]
            _exchange_in_steps(ps_refs, ss_refs, send_sems, recv_sems,
                               first=jnp.logical_and(hg == 0, jnp.logical_and(ki == 0, qi == 0)),
                               last=jnp.logical_and(hg == n_groups - 1, jnp.logical_and(ki == n - 1, qi == n - 1)))

        @pl.when(jnp.logical_and(ki == 0, qi == 0))
        def _():
            dq_sc[...] = jnp.zeros_like(dq_sc)

        @pl.when(qi == 0)
        def _():
            dk_sc[...] = jnp.zeros_like(dk_sc)
            dv_sc[...] = jnp.zeros_like(dv_sc)

        def step(masked):
            qb, kb, dob = q_ref[...], k_ref[...], do_ref[...]
            sc = _bdot(kb, qb, 2, 2) * MLA_SCALE
            if masked:
                sc = jnp.where(_causal_mask_t(t), sc, NEG)
            p = jnp.exp(sc - lse_ref[...])
            dv_sc[...] += _bdot(_mx(p), dob, 2, 1)
            dp = _bdot(v_ref[...], dob, 2, 2)
            ds = _mx(p * (dp - dd_ref[...]) * MLA_SCALE)
            dk_sc[...] += _bdot(ds, qb, 2, 1)
            dq_sc[qi] += _bdot(ds, kb, 1, 1)

        @pl.when(qi == ki)
        def _():
            step(True)

        @pl.when(qi > ki)
        def _():
            step(False)

        @pl.when(qi == n - 1)
        def _():
            dk_ref[...] = dk_sc[...]
            dv_ref[...] = dv_sc[...]

        @pl.when(jnp.logical_and(ki == n - 1, qi == n - 1))
        def _():
            for j in range(n):
                dq_ref[:, j * t:(j + 1) * t, :] = dq_sc[j]

    qs = lambda w: pl.BlockSpec((g, t, w), lambda h, j, i: (h, jnp.maximum(i, j), 0))
    ks = lambda w: pl.BlockSpec((g, t, w), lambda h, j, i: (h, j, 0))
    rowq = pl.BlockSpec((g, 1, t), lambda h, j, i: (h, 0, jnp.maximum(i, j)))
    scratch = [pltpu.VMEM((n, g, t, QK_PAD), F32), pltpu.VMEM((g, t, QK_PAD), F32), pltpu.VMEM((g, t, HEAD_DIM), F32)]
    if ne:
        scratch += [pltpu.SemaphoreType.DMA((3 * ne,)), pltpu.SemaphoreType.DMA((3 * ne,))]
    return pl.pallas_call(
        body, name="mla_attn_bwd",
        out_shape=(jax.ShapeDtypeStruct((hds, s, QK_PAD), F32), jax.ShapeDtypeStruct((hds, s, QK_PAD), F32),
                   jax.ShapeDtypeStruct((hds, s, HEAD_DIM), F32)) + tuple(jax.ShapeDtypeStruct(a.shape, a.dtype) for a in early),
        grid=(n_groups, n, n),
        in_specs=[qs(QK_PAD), ks(QK_PAD), ks(HEAD_DIM), qs(HEAD_DIM), rowq, rowq] + [ANY] * ne,
        out_specs=(pl.BlockSpec((g, s, QK_PAD), lambda h, j, i: (h, 0, 0)), ks(QK_PAD), ks(HEAD_DIM)) + (ANY,) * ne,
        scratch_shapes=scratch,
        compiler_params=pltpu.CompilerParams(dimension_semantics=("arbitrary",) * 3, vmem_limit_bytes=VMEM_LIMIT_BYTES,
                                             has_side_effects=ne > 0),
    )(q, k, v, do, lse, dd, *early)


def _perm_row(a, dil):
    if dil == 1:
        return a
    hds, _, s = a.shape
    return a.reshape(hds, s // dil, dil).transpose(0, 2, 1).reshape(hds, 1, s)


def _unperm_row(a, dil):
    if dil == 1:
        return a
    hds, _, s = a.shape
    return a.reshape(hds, dil, s // dil).transpose(0, 2, 1).reshape(hds, 1, s)


def _dil_bias(dil):
    slopes = 2.0 ** (-8.0 * jnp.arange(1, HEADS + 1, dtype=F32) / HEADS)
    ik = jnp.arange(DIL_BLOCK)[:, None]
    iq = jnp.arange(DIL_BLOCK)[None, :]
    off_c = iq - ik
    off_p = iq - ik + DIL_BLOCK
    b_c = -slopes[:, None, None] * (off_c * dil).astype(F32)[None]
    b_p = -slopes[:, None, None] * (off_p * dil).astype(F32)[None]
    b_c = jnp.where((off_c >= 0)[None], b_c, NEG)
    b_p = jnp.where((off_p <= DIL_BLOCK)[None], b_p, NEG)
    return b_c, b_p


def _dil_fwd(q, k, v, dil, *, name):
    hds, s, e = q.shape
    blk = DIL_BLOCK
    nblk = s // blk
    nb = nblk // dil
    pair = 2 if nb % 2 == 0 else 1
    b_c, b_p = _dil_bias(dil)

    def body(q_ref, k_ref, kp_ref, v_ref, vp_ref, bc_ref, bp_ref, o_ref, lse_ref):
        first = ((pair * pl.program_id(0)) % nb) == 0
        bc, bp = bc_ref[...], bp_ref[...]
        for j in range(pair):
            rows = slice(j * blk, (j + 1) * blk)
            qb = q_ref[:, rows, :]
            if j == 0:
                kp, vp = kp_ref[...], vp_ref[...]
            else:
                kp, vp = k_ref[:, (j - 1) * blk:j * blk, :], v_ref[:, (j - 1) * blk:j * blk, :]
            s_c = _bdot(k_ref[:, rows, :], qb, 2, 2) * DIL_SCALE + bc
            s_p = _bdot(kp, qb, 2, 2) * DIL_SCALE + bp
            if j == 0:
                s_p = jnp.where(first, NEG, s_p)
            m = jnp.maximum(jnp.max(s_c, axis=1, keepdims=True), jnp.max(s_p, axis=1, keepdims=True))
            p_c = jnp.exp(s_c - m)
            p_p = jnp.exp(s_p - m)
            l = jnp.sum(p_c, axis=1, keepdims=True) + jnp.sum(p_p, axis=1, keepdims=True)
            o = _bdot(_mx(p_c), v_ref[:, rows, :], 1, 1) + _bdot(_mx(p_p), vp, 1, 1)
            o_ref[:, rows, :] = o / jnp.swapaxes(l, 1, 2)
            lse_ref[:, :, rows] = m + jnp.log(l)

    cur = lambda w: pl.BlockSpec((hds, pair * blk, w), lambda b: (0, b, 0))
    prev = lambda w: pl.BlockSpec((hds, blk, w), lambda b: (0, jnp.maximum(pair * b - 1, 0), 0))
    bias = pl.BlockSpec((hds, blk, blk), lambda b: (0, 0, 0))
    return pl.pallas_call(
        body, name=name,
        out_shape=(jax.ShapeDtypeStruct((hds, s, e), F32), jax.ShapeDtypeStruct((hds, 1, s), F32)),
        grid=(nblk // pair,),
        in_specs=[cur(e), cur(e), prev(e), cur(e), prev(e), bias, bias],
        out_specs=(cur(e), pl.BlockSpec((hds, 1, pair * blk), lambda b: (0, 0, b))),
        compiler_params=_params("parallel"),
    )(q, k, k, v, v, b_c, b_p)


def _dil_combine(os_, lses, *, ts):
    hds, s, e = os_[0].shape
    dils = [d for _, d in DIL_PAIRS]

    def body(o0, o1, o2, l0, l1, l2, o_ref, l_ref, sc1, sc2):
        _load_token_order(sc1, o1, dils[1], ts)
        _load_token_order(sc2, o2, dils[2], ts)
        a0, a1, a2 = l0[...], l1[...], l2[...]
        m = jnp.maximum(jnp.maximum(a0, a1), a2)
        e0, e1, e2 = jnp.exp(a0 - m), jnp.exp(a1 - m), jnp.exp(a2 - m)
        tot = e0 + e1 + e2
        col = lambda w: jnp.swapaxes(w, 1, 2)
        res = (col(e0 / tot) * o0[...] + col(e1 / tot) * sc1[...]) + col(e2 / tot) * sc2[...]
        for hd in range(hds):
            o_ref[:, hd * e:(hd + 1) * e] = res[hd]
        l_ref[...] = m + jnp.log(tot)

    _, specs = _residue_major_outs(s, ts, dils, F32)
    view = lambda a, d: a if d == 1 else a.reshape(hds, d, s // d, e)
    rspec = pl.BlockSpec((hds, 1, ts), lambda i: (0, 0, i))
    return pl.pallas_call(
        body, name="dil_combine",
        out_shape=(jax.ShapeDtypeStruct((s, hds * e), F32), jax.ShapeDtypeStruct((hds, 1, s), F32)),
        grid=(s // ts,),
        in_specs=list(specs) + [rspec] * 3,
        out_specs=(pl.BlockSpec((ts, hds * e), lambda i: (i, 0)), rspec),
        scratch_shapes=[pltpu.VMEM((hds, ts, e), F32), pltpu.VMEM((hds, ts, e), F32)],
        compiler_params=_params("parallel"),
    )(*[view(a, d) for a, d in zip(os_, dils)], *lses)


def _dil_bwd(q, k, v, do, lj, dd, dil, *, name):
    hds, s, e = q.shape
    blk = DIL_BLOCK
    nblk = s // blk
    nb = nblk // dil
    pair = 2 if nb % 2 == 0 else 1
    b_c, b_p = _dil_bias(dil)

    def body(q_ref, qn_ref, k_ref, kp_ref, v_ref, vp_ref, do_ref, don_ref, l_ref, ln_ref, d_ref, dn_ref,
             bc_ref, bp_ref, dq_ref, dk_ref, dv_ref):
        b0 = pair * pl.program_id(0)
        first = (b0 % nb) == 0
        nxt = jnp.logical_and(b0 + pair < nblk, ((b0 + pair) % nb) != 0)
        bc, bp = bc_ref[...], bp_ref[...]
        for j in range(pair):
            rows = slice(j * blk, (j + 1) * blk)
            qb, kc, vc = q_ref[:, rows, :], k_ref[:, rows, :], v_ref[:, rows, :]
            dob, l, d = _mx(do_ref[:, rows, :]), l_ref[:, :, rows], d_ref[:, :, rows]
            if j == 0:
                kp, vp = kp_ref[...], vp_ref[...]
            else:
                kp, vp = k_ref[:, (j - 1) * blk:j * blk, :], v_ref[:, (j - 1) * blk:j * blk, :]
            p_c = jnp.exp(_bdot(kc, qb, 2, 2) * DIL_SCALE + bc - l)
            p_p = jnp.exp(_bdot(kp, qb, 2, 2) * DIL_SCALE + bp - l)
            if j == 0:
                p_p = jnp.where(first, 0.0, p_p)
            ds_c = _mx(p_c * (_bdot(vc, dob, 2, 2) - d) * DIL_SCALE)
            ds_p = _mx(p_p * (_bdot(vp, dob, 2, 2) - d) * DIL_SCALE)
            dq_ref[:, rows, :] = _bdot(ds_c, kc, 1, 1) + _bdot(ds_p, kp, 1, 1)
            if j < pair - 1:
                nrows = slice((j + 1) * blk, (j + 2) * blk)
                qn, donb, ln, dn = q_ref[:, nrows, :], _mx(do_ref[:, nrows, :]), l_ref[:, :, nrows], d_ref[:, :, nrows]
            else:
                qn, donb, ln, dn = qn_ref[...], _mx(don_ref[...]), ln_ref[...], dn_ref[...]
            p_n = jnp.exp(_bdot(kc, qn, 2, 2) * DIL_SCALE + bp - ln)
            if j == pair - 1:
                p_n = jnp.where(nxt, p_n, 0.0)
            ds_n = _mx(p_n * (_bdot(vc, donb, 2, 2) - dn) * DIL_SCALE)
            dk_ref[:, rows, :] = _bdot(ds_c, qb, 2, 1) + _bdot(ds_n, qn, 2, 1)
            dv_ref[:, rows, :] = _bdot(_mx(p_c), dob, 2, 1) + _bdot(_mx(p_n), donb, 2, 1)

    cur = lambda w: pl.BlockSpec((hds, pair * blk, w), lambda b: (0, b, 0))
    prev = lambda w: pl.BlockSpec((hds, blk, w), lambda b: (0, jnp.maximum(pair * b - 1, 0), 0))
    nxt_ = lambda w: pl.BlockSpec((hds, blk, w), lambda b: (0, jnp.minimum(pair * (b + 1), nblk - 1), 0))
    rcur = pl.BlockSpec((hds, 1, pair * blk), lambda b: (0, 0, b))
    rnxt = pl.BlockSpec((hds, 1, blk), lambda b: (0, 0, jnp.minimum(pair * (b + 1), nblk - 1)))
    bias = pl.BlockSpec((hds, blk, blk), lambda b: (0, 0, 0))
    out = jax.ShapeDtypeStruct((hds, s, e), F32)
    return pl.pallas_call(
        body, name=name,
        out_shape=(out, out, out),
        grid=(nblk // pair,),
        in_specs=[cur(e), nxt_(e), cur(e), prev(e), cur(e), prev(e), cur(e), nxt_(e),
                  rcur, rnxt, rcur, rnxt, bias, bias],
        out_specs=(cur(e), cur(e), cur(e)),
        compiler_params=_params("parallel"),
    )(q, q, k, k, v, v, do, do, lj, lj, dd, dd, b_c, b_p)


def _ln_fwd(z, g, b):
    mu = jnp.mean(z, axis=-1, keepdims=True)
    zc = z - mu
    var = jnp.mean(zc * zc, axis=-1, keepdims=True)
    rstd = lax.rsqrt(var + LN_EPS)
    xhat = zc * rstd
    return xhat * g + b, xhat, rstd


def _ln_bwd(dy, xhat, rstd, g):
    dxh = dy * g
    return rstd * (dxh - jnp.mean(dxh, axis=-1, keepdims=True) - xhat * jnp.mean(dxh * xhat, axis=-1, keepdims=True))


def _out_ln1(a_mla, a_dil, w_o, x, g, b, *, tm):
    s = x.shape[0]
    half = HEADS * HEAD_DIM

    def body(am_ref, ad_ref, w_ref, x_ref, g_ref, b_ref, x1_ref, xh_ref, r_ref):
        mix = _dot(_mx(am_ref[...]), w_ref[0:half, :]) + _dot(_mx(ad_ref[...]), w_ref[half:2 * half, :])
        z = DN_ALPHA * x_ref[...] + mix
        y, xhat, rstd = _ln_fwd(z, g_ref[...], b_ref[...])
        x1_ref[...] = y
        xh_ref[...] = xhat
        r_ref[...] = rstd

    row = lambda w: pl.BlockSpec((tm, w), lambda i: (i, 0))
    full = lambda shp: pl.BlockSpec(shp, lambda i: (0,) * len(shp))
    act = jax.ShapeDtypeStruct((s, D_MODEL), F32)
    return pl.pallas_call(
        body, name="out_ln1",
        out_shape=(act, act, jax.ShapeDtypeStruct((s, 1), F32)),
        grid=(s // tm,),
        in_specs=[row(half), row(half), full((D_MODEL, D_MODEL)), row(D_MODEL), full((1, D_MODEL)), full((1, D_MODEL))],
        out_specs=(row(D_MODEL), row(D_MODEL), row(1)),
        compiler_params=_params("parallel"),
    )(a_mla, a_dil, w_o, x, g, b)


def _down_ln2_loss(act, w_down, x1, g, b, target, *, tm):
    s = x1.shape[0]

    def body(a_ref, w_ref, x1_ref, g_ref, b_ref, t_ref, dz_ref, loss_ref, dg_ref, db_ref):
        i = pl.program_id(0)

        @pl.when(i == 0)
        def _():
            loss_ref[...] = jnp.zeros_like(loss_ref)
            dg_ref[...] = jnp.zeros_like(dg_ref)
            db_ref[...] = jnp.zeros_like(db_ref)

        gam = g_ref[...]
        z = DN_ALPHA * x1_ref[...] + _dot(a_ref[...], w_ref[...])
        y, xhat, rstd = _ln_fwd(z, gam, b_ref[...])
        err = y - t_ref[...]
        loss_ref[...] += 0.5 * jnp.sum(jnp.mean(err * err, axis=-1, keepdims=True))
        dy = err * (1.0 / D_MODEL)
        dg_ref[...] += jnp.sum(dy * xhat, axis=0, keepdims=True)
        db_ref[...] += jnp.sum(dy, axis=0, keepdims=True)
        dz_ref[...] = _ln_bwd(dy, xhat, rstd, gam)

    row = lambda w: pl.BlockSpec((tm, w), lambda i: (i, 0))
    full = lambda shp: pl.BlockSpec(shp, lambda i: (0,) * len(shp))
    vec = jax.ShapeDtypeStruct((1, D_MODEL), F32)
    return pl.pallas_call(
        body, name="down_ln2_loss",
        out_shape=(jax.ShapeDtypeStruct((s, D_MODEL), F32), jax.ShapeDtypeStruct((1, LANES), F32), vec, vec),
        grid=(s // tm,),
        in_specs=[row(D_FF), full((D_FF, D_MODEL)), row(D_MODEL), full((1, D_MODEL)), full((1, D_MODEL)), row(D_MODEL)],
        out_specs=(row(D_MODEL), full((1, LANES)), full((1, D_MODEL)), full((1, D_MODEL))),
        compiler_params=_params("arbitrary"),
    )(act, w_down, x1, g, b, target)


def _up_bwd_ln1(du_a, du_g, w_up_t, dz2, xhat1, rstd1, g, *, tm):
    s = dz2.shape[0]

    def body(dua_ref, dug_ref, wa_ref, wg_ref, dz2_ref, xh_ref, r_ref, g_ref, dz1_ref, dg_ref, db_ref):
        i = pl.program_id(0)

        @pl.when(i == 0)
        def _():
            dg_ref[...] = jnp.zeros_like(dg_ref)
            db_ref[...] = jnp.zeros_like(db_ref)

        dx1 = DN_ALPHA * dz2_ref[...] + (_dot(dua_ref[...], wa_ref[...]) + _dot(dug_ref[...], wg_ref[...]))
        xhat = xh_ref[...]
        dg_ref[...] += jnp.sum(dx1 * xhat, axis=0, keepdims=True)
        db_ref[...] += jnp.sum(dx1, axis=0, keepdims=True)
        dz1_ref[...] = _ln_bwd(dx1, xhat, r_ref[...], g_ref[...])

    row = lambda w: pl.BlockSpec((tm, w), lambda i: (i, 0))
    full = lambda shp: pl.BlockSpec(shp, lambda i: (0,) * len(shp))
    vec = jax.ShapeDtypeStruct((1, D_MODEL), F32)
    return pl.pallas_call(
        body, name="up_bwd_ln1",
        out_shape=(jax.ShapeDtypeStruct((s, D_MODEL), F32), vec, vec),
        grid=(s // tm,),
        in_specs=[row(D_FF), row(D_FF),
                  pl.BlockSpec((D_FF, D_MODEL), lambda i: (0, 0)), pl.BlockSpec((D_FF, D_MODEL), lambda i: (1, 0)),
                  row(D_MODEL), row(D_MODEL), row(1), full((1, D_MODEL))],
        out_specs=(row(D_MODEL), full((1, D_MODEL)), full((1, D_MODEL))),
        compiler_params=_params("arbitrary"),
    )(du_a, du_g, w_up_t, w_up_t, dz2, xhat1, rstd1, g)


GELU_C = math.sqrt(2.0 / math.pi)


def _gelu(x):
    cdf = 0.5 * (1.0 + jnp.tanh(GELU_C * (x + 0.044715 * (x * x * x))))
    return x * cdf


def _gelu_grad(x):
    t = jnp.tanh(GELU_C * (x + 0.044715 * (x * x * x)))
    return 0.5 * (1.0 + t) + 0.5 * x * (1.0 - t * t) * (GELU_C * (1.0 + 3.0 * 0.044715 * (x * x)))


def _shift_down(u, halo):
    r1, r2 = pltpu.roll(u, 1, 0), pltpu.roll(u, 2, 0)
    row = lax.broadcasted_iota(jnp.int32, (SUBLANES, u.shape[1]), 0)
    h7, h6 = halo[7:8, :], halo[6:7, :]
    head1 = jnp.where(row == 0, h7, r1[:SUBLANES])
    head2 = jnp.where(row == 0, h6, jnp.where(row == 1, h7, r2[:SUBLANES]))
    return (jnp.concatenate([head1, r1[SUBLANES:]], axis=0), jnp.concatenate([head2, r2[SUBLANES:]], axis=0))


def _shift_up(d, nxt):
    t = d.shape[0]
    r1, r2 = pltpu.roll(d, t - 1, 0), pltpu.roll(d, t - 2, 0)
    row = lax.broadcasted_iota(jnp.int32, (SUBLANES, d.shape[1]), 0)
    n0, n1 = nxt[0:1, :], nxt[1:2, :]
    last = t - SUBLANES
    tail1 = jnp.where(row == SUBLANES - 1, n0, r1[last:])
    tail2 = jnp.where(row == SUBLANES - 1, n1, jnp.where(row == SUBLANES - 2, n0, r2[last:]))
    return (jnp.concatenate([r1[:last], tail1], axis=0), jnp.concatenate([r2[:last], tail2], axis=0))


def _conv(u, s1, s2, w, b):
    return ((b + w[0:1, :] * s2) + w[1:2, :] * s1) + w[2:3, :] * u


def _up_gate_fwd(x1, w_up, conv_w, conv_b, *, tm, tn):
    s = x1.shape[0]
    nj = D_FF // tn
    hb = tm // SUBLANES

    def body(x_ref, xh_ref, wua_ref, wug_ref, wa_ref, wg_ref, ba_ref, bg_ref,
             ua_ref, ug_ref, o_ref, a_ref, ge_ref, gd_ref):
        keep = pl.program_id(1) > 0
        xb, xh = _mx(x_ref[...]), _mx(xh_ref[...])
        wua, wug = wua_ref[...], wug_ref[...]
        ua, ug = _dot(xb, wua), _dot(xb, wug)
        ha = jnp.where(keep, _dot(xh, wua), 0.0)
        hg = jnp.where(keep, _dot(xh, wug), 0.0)
        ua_ref[...] = ua
        ug_ref[...] = ug
        a = _conv(ua, *_shift_down(ua, ha), wa_ref[...], ba_ref[...])
        g = _conv(ug, *_shift_down(ug, hg), wg_ref[...], bg_ref[...])
        ge = _gelu(g)
        o_ref[...] = (ge * a).astype(o_ref.dtype)
        a_ref[...] = a
        ge_ref[...] = ge
        gd_ref[...] = _gelu_grad(g)

    main = lambda off: pl.BlockSpec((tm, tn), lambda j, i: (i, j + off))
    wspec = lambda r, off: pl.BlockSpec((r, tn), lambda j, i: (0, j + off))
    if w_up.ndim == 3:
        wu = lambda off: pl.BlockSpec((None, D_MODEL, tn), lambda j, i: (j + off, 0, 0))
    else:
        wu = lambda off: pl.BlockSpec((D_MODEL, tn), lambda j, i: (0, j + off))
    keep_f32 = jax.ShapeDtypeStruct((s, D_FF), F32)
    return pl.pallas_call(
        body, name="up_gate_fwd",
        out_shape=(keep_f32, keep_f32, jax.ShapeDtypeStruct((s, D_FF), MXU_DTYPE), keep_f32, keep_f32, keep_f32),
        grid=(nj, s // tm),
        in_specs=[pl.BlockSpec((tm, D_MODEL), lambda j, i: (i, 0)),
                  pl.BlockSpec((SUBLANES, D_MODEL), lambda j, i: (jnp.maximum(i * hb - 1, 0), 0)),
                  wu(0), wu(nj), wspec(3, 0), wspec(3, nj), wspec(1, 0), wspec(1, nj)],
        out_specs=(main(0),) * 6,
        compiler_params=_params("parallel", "parallel"),
    )(x1, x1, w_up, w_up, conv_w, conv_w, conv_b, conv_b)


def _gate_bwd(u_a, u_g, dz2, w_down_t, a, ge, gd, conv_w, *, tm, tn):
    s = u_a.shape[0]
    nj = D_FF // tn
    ni = s // tm
    hb = tm // SUBLANES

    def body(ua_ref, ug_ref, ha_ref, hg_ref, dz_ref, dzn_ref, wd_ref, a_ref, an_ref, ge_ref, gen_ref, gd_ref, gdn_ref,
             wa_ref, wg_ref, dua_ref, dug_ref, dwa_ref, dwg_ref, dba_ref, dbg_ref):
        i = pl.program_id(1)

        @pl.when(i == 0)
        def _():
            for r in (dwa_ref, dwg_ref, dba_ref, dbg_ref):
                r[...] = jnp.zeros_like(r)

        wa, wg = wa_ref[...], wg_ref[...]
        ua, ug = ua_ref[...], ug_ref[...]
        ha = jnp.where(i > 0, ha_ref[...], 0.0)
        hg = jnp.where(i > 0, hg_ref[...], 0.0)
        sa1, sa2 = _shift_down(ua, ha)
        sg1, sg2 = _shift_down(ug, hg)
        wd = wd_ref[...]
        d = _dot(_mx(dz_ref[...]), wd)
        dya = d * ge_ref[...]
        dyg = d * a_ref[...] * gd_ref[...]
        dn = jnp.where(i < ni - 1, _dot(_mx(dzn_ref[...]), wd), 0.0)
        dya_n = dn * gen_ref[...]
        dyg_n = dn * an_ref[...] * gdn_ref[...]
        da1, da2 = _shift_up(dya, dya_n)
        dg1, dg2 = _shift_up(dyg, dyg_n)
        dua_ref[...] = (wa[2:3, :] * dya + wa[1:2, :] * da1 + wa[0:1, :] * da2).astype(dua_ref.dtype)
        dug_ref[...] = (wg[2:3, :] * dyg + wg[1:2, :] * dg1 + wg[0:1, :] * dg2).astype(dug_ref.dtype)
        ssum = lambda v: jnp.sum(v, axis=0, keepdims=True)
        dwa_ref[...] += jnp.concatenate([ssum(dya * sa2), ssum(dya * sa1), ssum(dya * ua)], axis=0)
        dwg_ref[...] += jnp.concatenate([ssum(dyg * sg2), ssum(dyg * sg1), ssum(dyg * ug)], axis=0)
        dba_ref[...] += ssum(dya)
        dbg_ref[...] += ssum(dyg)

    main = pl.BlockSpec((tm, tn), lambda j, i: (i, j))
    halo = pl.BlockSpec((SUBLANES, tn), lambda j, i: (jnp.maximum(i * hb - 1, 0), j))
    next_row = lambda j, i: jnp.minimum((i + 1) * hb, s // SUBLANES - 1)
    nxt = pl.BlockSpec((SUBLANES, tn), lambda j, i: (next_row(j, i), j))
    wspec = lambda r, off: pl.BlockSpec((r, tn), lambda j, i: (0, j + off))
    return pl.pallas_call(
        body, name="gate_bwd",
        out_shape=(jax.ShapeDtypeStruct((s, D_FF), MXU_DTYPE), jax.ShapeDtypeStruct((s, D_FF), MXU_DTYPE),
                   jax.ShapeDtypeStruct((3, D_FF), F32), jax.ShapeDtypeStruct((3, D_FF), F32),
                   jax.ShapeDtypeStruct((1, D_FF), F32), jax.ShapeDtypeStruct((1, D_FF), F32)),
        grid=(nj, ni),
        in_specs=[main, main, halo, halo,
                  pl.BlockSpec((tm, D_MODEL), lambda j, i: (i, 0)),
                  pl.BlockSpec((SUBLANES, D_MODEL), lambda j, i: (next_row(j, i), 0)),
                  pl.BlockSpec((D_MODEL, tn), lambda j, i: (0, j))]
        + [main, nxt] * 3 + [wspec(3, 0), wspec(3, nj)],
        out_specs=(main, main, wspec(3, 0), wspec(3, 0), wspec(1, 0), wspec(1, 0)),
        compiler_params=_params("parallel", "arbitrary"),
    )(u_a, u_g, u_a, u_g, dz2, dz2, w_down_t, a, a, ge, ge, gd, gd, conv_w, conv_w)


def _prep_weights(w_in, w_uq, w_uk, w_uv, w_o, w_up, w_down):
    return {**_prep_weights_first(w_in, w_uq, w_uk, w_uv), **_prep_weights_late(w_o, w_up, w_down)}


def _prep_weights_late(w_o, w_up, w_down):
    w_o, w_up, w_down = _mx(w_o), _mx(w_up), _mx(w_down)
    w_up_t = w_up.T if w_up.ndim == 2 else w_up.transpose(0, 2, 1).reshape(2 * D_FF, D_MODEL)
    return dict(w_o=w_o, w_o_t=w_o.T, w_up=w_up, w_up_t=w_up_t, w_down=w_down, w_down_t=w_down.T)


def _prep_weights_first(w_in, w_uq, w_uk, w_uv):
    c = lambda a: a.astype(MXU_DTYPE)
    w_in = c(w_in)
    z = lambda w: jnp.zeros((D_MODEL, w), MXU_DTYPE)
    r0 = Q_RANK + KV_RANK
    w_in_ext = jnp.concatenate([w_in[:, :r0], z(NOPE), w_in[:, r0:r0 + ROPE], z(32), w_in[:, r0 + ROPE:]], axis=1)
    wq = jnp.pad(c(w_uq).transpose(1, 0, 2), ((0, 0), (0, 0), (0, QK_PAD - NOPE - ROPE)))
    wk = jnp.pad(c(w_uk).transpose(1, 0, 2), ((0, 0), (0, 0), (0, QK_PAD - NOPE)))
    wv = c(w_uv).transpose(1, 0, 2)
    t3 = lambda a: a.transpose(0, 2, 1)
    return dict(w_in=w_in_ext, w_in_t=w_in_ext.T, wq=wq, wq_t=t3(wq), wk=wk, wk_t=t3(wk), wv=wv, wv_t=t3(wv))


def _local_step(x, target, w, g_cq, g_ckv, ln1_g, ln1_b, conv_w, conv_b, ln2_g, ln2_b, comm=None):
    s = x.shape[0]
    tabs = _rope_tables(s)
    r2 = lambda a: a.reshape(1, -1)
    cb = r2(conv_b)
    dils = [d for _, d in DIL_PAIRS]

    h, qp, kp, vp = _in_proj(x, w["w_in"], tm=256)
    q, k, v, v_t = _mla_prep_fwd(h, r2(g_cq), r2(g_ckv), w["wq"], w["wk"], w["wv"], w["wv_t"], tabs, tm=256)
    if comm is None:
        o_mla_t, lse_mla = _mla_attn_fwd(q, k, v_t, t=512, g=HEADS)
    else:
        o_mla_t, lse_mla, *gathered = _mla_attn_fwd(q, k, v_t, t=512, g=HEADS, late=comm["late"])
        w = {**w, **comm["finish"](gathered)}
    o_bs, lse_bs = [], []
    for i, d in enumerate(dils):
        o_b, l_b = _dil_fwd(qp[i], kp[i], vp[i], d, name=f"dil_fwd_{d}")
        o_bs.append(o_b)
        lse_bs.append(_unperm_row(l_b, d))
    o_dil, lj = _dil_combine(o_bs, lse_bs, ts=512)
    o_mla = o_mla_t.transpose(2, 0, 1).reshape(s, HEADS * HEAD_DIM)
    x1, xhat1, rstd1 = _out_ln1(o_mla, o_dil, w["w_o"], x, r2(ln1_g), r2(ln1_b), tm=256)
    u_a, u_g, act, conv_a, gelu_g, gelu_dg = _up_gate_fwd(x1, w["w_up"], conv_w, cb, tm=256, tn=1408)
    dz2, loss, dg2, db2 = _down_ln2_loss(act, w["w_down"], x1, r2(ln2_g), r2(ln2_b), target, tm=256)

    dw_down = _mm_tn(act, dz2, name="dw_down", tm=1408, tn=D_MODEL, ts=DW_TOKENS)
    du_a, du_g, dcw_a, dcw_g, dcb_a, dcb_g = _gate_bwd(u_a, u_g, dz2, w["w_down_t"], conv_a, gelu_g, gelu_dg, conv_w,
                                                       tm=256, tn=1408)
    dz1, dg1, db1 = _up_bwd_ln1(du_a, du_g, w["w_up_t"], dz2, xhat1, rstd1, r2(ln1_g), tm=256)
    dw_up = jnp.concatenate([_mm_tn(x1, du_a, name="dw_up_a", tm=D_MODEL, tn=1408, ts=DW_TOKENS),
                             _mm_tn(x1, du_g, name="dw_up_g", tm=D_MODEL, tn=1408, ts=DW_TOKENS)], axis=1)
    named_early = [("w_up", dw_up), ("w_down", dw_down)]
    swap = () if comm is None else comm["blocked"](named_early)
    do_mla, do_dil, dd_all, received = _attn_bwd_heads(dz1, w["w_o_t"], o_mla, o_dil, tm=256, swap=swap)
    dw_o = jnp.concatenate([_mm_tn(o_mla, dz1, name="dw_o_mla", tm=512, tn=D_MODEL, ts=DW_TOKENS),
                            _mm_tn(o_dil, dz1, name="dw_o_dil", tm=512, tn=D_MODEL, ts=DW_TOKENS)], axis=0)
    dd_all = dd_all.T
    dd_mla, dd_dil = dd_all[:HEADS].reshape(HEADS, 1, s), dd_all[HEADS:].reshape(HEADS, 1, s)
    early = () if comm is None else tuple(comm["add_halves"](named_early, swap, received))
    dq, dk, dv, *early_slots = _mla_attn_bwd(q, k, v, do_mla, lse_mla, dd_mla, t=512, g=4, early=early)
    parts = []
    for i, d in enumerate(dils):
        parts.append(_dil_bwd(qp[i], kp[i], vp[i], do_dil[i], _perm_row(lj, d), _perm_row(dd_dil, d), d, name=f"dil_bwd_{d}"))
    dh_dil = _dil_merge(parts, ts=512)
    dh_mla, dwq, dwk, dwv, dgq, dgkv = _mla_prep_bwd(h, dq, dk, dv, r2(g_cq), r2(g_ckv),
                                                     w["wq_t"], w["wk_t"], w["wv_t"], tabs, tm=256)
    mla_w = 4 * LANES
    w_in_t = w["w_in_t"]
    grad_x = _mm_nn(dh_mla, w_in_t[:mla_w], name="in_bwd_mla", tm=512, tn=D_MODEL, tk=mla_w, add=dz1, add_scale=DN_ALPHA)
    grad_x = _mm_nn(dh_dil, w_in_t[mla_w:], name="in_bwd_dil", tm=512, tn=D_MODEL, tk=3 * HEADS * HEAD_DIM, add=grad_x)
    dw_mla = _mm_tn(x, dh_mla, name="dw_in_mla", tm=D_MODEL, tn=mla_w, ts=DW_TOKENS)
    dw_dil = _mm_tn(x, dh_dil, name="dw_in_dil", tm=D_MODEL, tn=3 * HEADS * HEAD_DIM, ts=DW_TOKENS)
    r0 = Q_RANK + KV_RANK
    grads = dict(
        w_in=jnp.concatenate([dw_mla[:, :r0], dw_mla[:, r0 + NOPE:r0 + NOPE + ROPE], dw_dil], axis=1),
        g_cq=dgq[0], g_ckv=dgkv[0],
        w_uq=dwq[:, :, :NOPE + ROPE].transpose(1, 0, 2),
        w_uk=dwk[:, :, :NOPE].transpose(1, 0, 2),
        w_uv=dwv.transpose(1, 0, 2),
        w_o=dw_o, ln1_g=dg1[0], ln1_b=db1[0], w_up=dw_up,
        conv_w=jnp.concatenate([dcw_a, dcw_g], axis=1), conv_b=jnp.concatenate([dcb_a, dcb_g], axis=1)[0],
        w_down=dw_down, ln2_g=dg2[0], ln2_b=db2[0])
    if comm is not None:
        grads["early"] = (early, tuple(early_slots))
    return loss[0, 0], grad_x, grads


N_CHIPS = 4
SHARDED = ("w_in", "w_uq", "w_o", "w_up", "conv_w", "w_down")
COL_SHARDED = ("w_in", "w_up", "conv_w")
SHARD_SHAPE = dict(w_in=(D_MODEL, IN_WIDTH // 4), w_uq=(Q_RANK // 4, HEADS, NOPE + ROPE), w_o=(D_MODEL // 4, D_MODEL),
                   w_up=(D_MODEL, 2 * D_FF // 4), conv_w=(3, 2 * D_FF // 4), w_down=(D_FF // 4, D_MODEL))
SMALL = ("g_cq", "g_ckv", "w_uk", "w_uv", "ln1_g", "ln1_b", "conv_b", "ln2_g", "ln2_b")
SMALL_SHAPE = dict(g_cq=(Q_RANK,), g_ckv=(KV_RANK,), w_uk=(KV_RANK, HEADS, NOPE), w_uv=(KV_RANK, HEADS, HEAD_DIM),
                   ln1_g=(D_MODEL,), ln1_b=(D_MODEL,), conv_b=(2 * D_FF,), ln2_g=(D_MODEL,), ln2_b=(D_MODEL,))
BIG = ("w_in", "w_uq", "w_o", "w_up", "w_down")
BIG_2D = dict(w_in=(D_MODEL, IN_WIDTH // 4), w_uq=(Q_RANK // 4, HEADS * (NOPE + ROPE)), w_o=(D_MODEL // 4, D_MODEL),
              w_up=(D_MODEL, 2 * D_FF // 4), w_down=(D_FF // 4, D_MODEL))
SMALL_G = SMALL + ("conv_w",)
SMALL_G_SHAPE = {**SMALL_SHAPE, "conv_w": (3, 2 * D_FF)}
SMALL_U_SHAPE = {**SMALL_SHAPE, "conv_w": (3, 2 * D_FF // 4)}


def _size(shape):
    return math.prod(shape)


def _padded_rows(n_elems, mult):
    return -(-n_elems // (LANES * mult)) * mult


SHARD_ROWS = {n: _padded_rows(_size(SHARD_SHAPE[n]), SUBLANES) for n in SHARDED}
R_SMALL = -(-sum(_size(SMALL_G_SHAPE[n]) for n in SMALL_G) // (LANES * LANES)) * LANES
GATHER_FIRST = ("w_in", "w_uq")
GATHER_LATE = ("w_o", "w_up", "w_down")
REDUCED_EARLY = ("w_up", "w_down")
REDUCED_LAST = ("w_in", "w_uq", "w_o")


def _rows(a, rows=None):
    flat = a.reshape(-1)
    rows = -(-flat.shape[0] // LANES) if rows is None else rows
    return jnp.pad(flat, (0, rows * LANES - flat.shape[0])).reshape(rows, LANES)


def _blocked(name, g):
    r, c = BIG_2D[name]
    a = g.reshape(r, N_CHIPS, c).transpose(1, 0, 2) if name in COL_SHARDED else g.reshape(N_CHIPS, r, c)
    return a.reshape(N_CHIPS, 2, r // 2, c)


def _pack_flat(t, names):
    return _rows(jnp.concatenate([t[n].astype(F32).reshape(-1) for n in names]), R_SMALL)


def _unpack_flat(buf, names, shapes):
    flat, out, r = buf.reshape(-1), {}, 0
    for n in names:
        out[n] = flat[r:r + _size(shapes[n])].reshape(shapes[n])
        r += _size(shapes[n])
    return out


def _from_chip_blocks(name, blocks):
    shp = SHARD_SHAPE[name]
    a = blocks.reshape(N_CHIPS, -1)[:, :_size(shp)].reshape((N_CHIPS,) + shp)
    if name in COL_SHARDED:
        return a.transpose(1, 0, 2).reshape(shp[0], N_CHIPS * shp[1])
    return a.reshape((N_CHIPS * shp[0],) + shp[1:])


ANY = pl.BlockSpec(memory_space=pl.ANY)
COMM_PARAMS = pltpu.CompilerParams(has_side_effects=True)


def _coords():
    return lax.axis_index("x"), lax.axis_index("y"), lax.axis_index("c")


def _other_chips(x, y):
    return [(1 - x, y), (x, 1 - y), (1 - x, 1 - y)]


def _remote(src, dst, send_sems, recv_sems, k, to):
    return pltpu.make_async_remote_copy(src_ref=src, dst_ref=dst, send_sem=send_sems.at[k], recv_sem=recv_sems.at[k],
                                        device_id=to, device_id_type=MESH)


def _gather_in_steps(wp_refs, wout_refs, send_sems, recv_sems, *, first, mid, last):
    x, y, c = _coords()
    me = 2 * x + y
    sib = (x, y, 1 - c)
    chips = _other_chips(x, y)
    n = len(wp_refs)
    pairs = [(j, t, px, py) for j, (px, py) in enumerate(chips) for t in range(n)]
    ici = [_remote(wp_refs[t].at[c], wout_refs[t].at[me, c], send_sems, recv_sems, j * n + t, (px, py, c))
           for j, t, px, py in pairs]
    fwd = [_remote(wout_refs[t].at[2 * px + py, c], wout_refs[t].at[2 * px + py, c], send_sems, recv_sems, (3 + j) * n + t, sib)
           for j, t, px, py in pairs]

    @pl.when(first)
    def _():
        for cp in ici:
            cp.start()

    @pl.when(mid)
    def _():
        for i, (j, t, px, py) in enumerate(pairs):
            _remote(wp_refs[t].at[c], wout_refs[t].at[2 * px + py, c], send_sems, recv_sems, j * n + t, (px, py, c)).wait_recv()
            fwd[i].start()

    @pl.when(last)
    def _():
        for j, t, px, py in pairs:
            k = 2 * px + py
            _remote(wout_refs[t].at[k, 1 - c], wout_refs[t].at[k, 1 - c], send_sems, recv_sems, (3 + j) * n + t, sib).wait_recv()
        for cp in ici + fwd:
            cp.wait_send()


def _swap_halves_in_steps(gs_refs, os_refs, send_sems, recv_sems, *, first, last):
    x, y, c = _coords()
    sib = (x, y, 1 - c)
    cps = [_remote(gs_refs[t].at[k, 1 - c], os_refs[t].at[k], send_sems, recv_sems, t * N_CHIPS + k, sib)
           for t in range(len(gs_refs)) for k in range(N_CHIPS)]

    @pl.when(first)
    def _():
        for cp in cps:
            cp.start()

    @pl.when(last)
    def _():
        for cp in cps:
            cp.wait_recv()
        for cp in cps:
            cp.wait_send()


def _exchange_in_steps(ps_refs, ss_refs, send_sems, recv_sems, *, first, last):
    x, y, c = _coords()
    me = 2 * x + y
    chips = _other_chips(x, y)
    n = len(ps_refs)
    sends = [_remote(ps_refs[t].at[2 * px + py], ss_refs[t].at[me], send_sems, recv_sems, j * n + t, (px, py, c))
             for j, (px, py) in enumerate(chips) for t in range(n)]

    @pl.when(first)
    def _():
        for cp in sends:
            cp.start()

    @pl.when(last)
    def _():
        for j, (px, py) in enumerate(chips):
            for t in range(n):
                _remote(ps_refs[t].at[me], ss_refs[t].at[2 * px + py], send_sems, recv_sems, j * n + t, (px, py, c)).wait_recv()
        for cp in sends:
            cp.wait_send()


def _gather_weights(wp, cwp):
    def body(wp_ref, cw_ref, wout_ref, cwout_ref, send_sems, recv_sems):
        x, y, c = _coords()
        me = 2 * x + y
        sib = (x, y, 1 - c)
        chips = _other_chips(x, y)
        sends = [_remote(wp_ref.at[c], wout_ref.at[me, c], send_sems, recv_sems, j, (px, py, c))
                 for j, (px, py) in enumerate(chips)]
        sends += [_remote(cw_ref, cwout_ref.at[me], send_sems, recv_sems, 3 + j, (px, py, c))
                  for j, (px, py) in enumerate(chips)]
        for cp in sends:
            cp.start()
        for j, (px, py) in enumerate(chips):
            k = 2 * px + py
            _remote(wp_ref.at[c], wout_ref.at[k, c], send_sems, recv_sems, j, (px, py, c)).wait_recv()
            fwd = _remote(wout_ref.at[k, c], wout_ref.at[k, c], send_sems, recv_sems, 6 + j, sib)
            fwd.start()
            sends.append(fwd)
        for j, (px, py) in enumerate(chips):
            k = 2 * px + py
            _remote(cw_ref, cwout_ref.at[k], send_sems, recv_sems, 3 + j, (px, py, c)).wait_recv()
            _remote(wout_ref.at[k, 1 - c], wout_ref.at[k, 1 - c], send_sems, recv_sems, 6 + j, sib).wait_recv()
        for cp in sends:
            cp.wait_send()

    return pl.pallas_call(
        body, name="gather_weights",
        out_shape=(jax.ShapeDtypeStruct((N_CHIPS,) + wp.shape, wp.dtype), jax.ShapeDtypeStruct((N_CHIPS,) + cwp.shape, cwp.dtype)),
        in_specs=[ANY, ANY], out_specs=(ANY, ANY),
        scratch_shapes=[pltpu.SemaphoreType.DMA((9,)), pltpu.SemaphoreType.DMA((9,))],
        compiler_params=COMM_PARAMS,
    )(wp, cwp)


def _exchange_sibling_halves(gs, whole, *, name):
    n, nw = len(gs), len(whole)

    def body(*refs):
        gs_refs, wh_refs = refs[:n], refs[n:n + nw]
        os_refs, ow_refs = refs[n + nw:2 * n + nw], refs[2 * n + nw:2 * (n + nw)]
        send_sems, recv_sems = refs[2 * (n + nw):]
        x, y, c = _coords()
        sib = (x, y, 1 - c)
        cps = [_remote(gs_refs[t].at[k, 1 - c], os_refs[t].at[k], send_sems, recv_sems, t * N_CHIPS + k, sib)
               for t in range(n) for k in range(N_CHIPS)]
        cps += [_remote(wh_refs[t], ow_refs[t], send_sems, recv_sems, n * N_CHIPS + t, sib) for t in range(nw)]
        for cp in cps:
            cp.start()
        for cp in cps:
            cp.wait_recv()
        for cp in cps:
            cp.wait_send()

    n_sem = n * N_CHIPS + nw
    return pl.pallas_call(
        body, name=name,
        out_shape=tuple(jax.ShapeDtypeStruct((N_CHIPS,) + a.shape[2:], F32) for a in gs)
        + tuple(jax.ShapeDtypeStruct(a.shape, F32) for a in whole),
        in_specs=[ANY] * (n + nw), out_specs=(ANY,) * (n + nw),
        scratch_shapes=[pltpu.SemaphoreType.DMA((n_sem,)), pltpu.SemaphoreType.DMA((n_sem,))],
        compiler_params=COMM_PARAMS,
    )(*gs, *whole)


def _exchange_chips(ps, pr):
    n = len(ps)

    def body(*refs):
        ps_refs, pr_ref, ss_refs, sr_ref = refs[:n], refs[n], refs[n + 1:2 * n + 1], refs[2 * n + 1]
        send_sems, recv_sems = refs[2 * n + 2:]
        x, y, c = _coords()
        me = 2 * x + y
        chips = _other_chips(x, y)
        sends = []
        for j, (px, py) in enumerate(chips):
            to = (px, py, c)
            for t in range(n):
                sends.append(_remote(ps_refs[t].at[2 * px + py], ss_refs[t].at[me], send_sems, recv_sems, j * (n + 1) + t, to))
            sends.append(_remote(pr_ref, sr_ref.at[me], send_sems, recv_sems, j * (n + 1) + n, to))
        for cp in sends:
            cp.start()
        for j, (px, py) in enumerate(chips):
            k, to = 2 * px + py, (px, py, c)
            for t in range(n):
                _remote(ps_refs[t].at[me], ss_refs[t].at[k], send_sems, recv_sems, j * (n + 1) + t, to).wait_recv()
            _remote(pr_ref, sr_ref.at[k], send_sems, recv_sems, j * (n + 1) + n, to).wait_recv()
        for cp in sends:
            cp.wait_send()

    n_sem = 3 * (n + 1)
    return pl.pallas_call(
        body, name="exchange_chips",
        out_shape=tuple(jax.ShapeDtypeStruct(a.shape, a.dtype) for a in ps) + (jax.ShapeDtypeStruct((N_CHIPS,) + pr.shape, F32),),
        in_specs=[ANY] * (n + 1), out_specs=(ANY,) * (n + 1),
        scratch_shapes=[pltpu.SemaphoreType.DMA((n_sem,)), pltpu.SemaphoreType.DMA((n_sem,))],
        compiler_params=COMM_PARAMS,
    )(*ps, pr)


def _exchange_sibling_result(gh):
    n = len(gh)

    def body(*refs):
        gh_refs, out_refs, (send_sems, recv_sems) = refs[:n], refs[n:2 * n], refs[2 * n:]
        x, y, c = _coords()
        cps = [_remote(gh_refs[t], out_refs[t], send_sems, recv_sems, t, (x, y, 1 - c)) for t in range(n)]
        for cp in cps:
            cp.start()
        for cp in cps:
            cp.wait_recv()
        for cp in cps:
            cp.wait_send()

    return pl.pallas_call(
        body, name="exchange_sibling_result",
        out_shape=tuple(jax.ShapeDtypeStruct(a.shape, F32) for a in gh),
        in_specs=[ANY] * n, out_specs=(ANY,) * n,
        scratch_shapes=[pltpu.SemaphoreType.DMA((n,)), pltpu.SemaphoreType.DMA((n,))],
        compiler_params=COMM_PARAMS,
    )(*gh)


def _add_own_half(gs, recv, c_arr, *, name):
    _, rows, cols = recv.shape

    def body(c_ref, a_ref, b_ref, o_ref):
        o_ref[0] = (a_ref[0, 0] + b_ref[0]).astype(o_ref.dtype)

    return pl.pallas_call(
        body, name=name,
        out_shape=jax.ShapeDtypeStruct(recv.shape, GRAD_WIRE_DTYPE),
        grid_spec=pltpu.PrefetchScalarGridSpec(
            num_scalar_prefetch=1, grid=(N_CHIPS,),
            in_specs=[pl.BlockSpec((1, 1, rows, cols), lambda k, c_ref: (k, c_ref[0], 0, 0)),
                      pl.BlockSpec((1, rows, cols), lambda k, c_ref: (k, 0, 0))],
            out_specs=pl.BlockSpec((1, rows, cols), lambda k, c_ref: (k, 0, 0))),
        compiler_params=_params("parallel"),
    )(c_arr, gs, recv)


def _add2(a, b, *, name):
    def body(a_ref, b_ref, o_ref):
        o_ref[...] = a_ref[...] + b_ref[...]

    return pl.pallas_call(body, name=name, out_shape=jax.ShapeDtypeStruct(a.shape, F32))(a, b)


def _sum_slots(slots, *, tr, name):
    _, r, c = slots.shape

    def body(s_ref, o_ref):
        f = lambda k: s_ref[k].astype(F32)
        o_ref[...] = ((f(0) + f(1)) + f(2)) + f(3)

    return pl.pallas_call(
        body, name=name,
        out_shape=jax.ShapeDtypeStruct((r, c), F32),
        grid=(r // tr,),
        in_specs=[pl.BlockSpec((N_CHIPS, tr, c), lambda i: (0, i, 0))],
        out_specs=pl.BlockSpec((tr, c), lambda i: (i, 0)),
        compiler_params=_params("parallel"),
    )(slots)


def _adamw(w, g, m, v, *, tr, name):
    r, cols = w.shape

    def body(w_ref, g_ref, m_ref, v_ref, d_ref, nm_ref, nv_ref):
        g_ = g_ref[...]
        m_ = ADAM_B1 * m_ref[...] + (1.0 - ADAM_B1) * g_
        v_ = ADAM_B2 * v_ref[...] + (1.0 - ADAM_B2) * (g_ * g_)
        m_hat = m_ / (1.0 - ADAM_B1 ** ADAM_STEP)
        v_hat = v_ / (1.0 - ADAM_B2 ** ADAM_STEP)
        d_ref[...] = -ADAM_LR * (m_hat / (jnp.sqrt(v_hat) + ADAM_EPS) + ADAM_WD * w_ref[...])
        nm_ref[...] = m_
        nv_ref[...] = v_

    spec = pl.BlockSpec((tr, cols), lambda i: (i, 0))
    out = jax.ShapeDtypeStruct((r, cols), F32)
    return pl.pallas_call(
        body, name=name, out_shape=(out, out, out), grid=(r // tr,),
        in_specs=[spec] * 4, out_specs=(spec,) * 3,
        compiler_params=_params("parallel"),
    )(w, g, m, v)


WEIGHTS = ("w_in", "g_cq", "g_ckv", "w_uq", "w_uk", "w_uv", "w_o", "ln1_g", "ln1_b", "w_up", "conv_w", "conv_b",
           "w_down", "ln2_g", "ln2_b")


def kernel(x, w_in, g_cq, g_ckv, w_uq, w_uk, w_uv, w_o, ln1_g, ln1_b, w_up, conv_w, conv_b, w_down, ln2_g, ln2_b, loss_target, m_w_in, m_g_cq, m_g_ckv, m_w_uq, m_w_uk, m_w_uv, m_w_o, m_ln1_g, m_ln1_b, m_w_up, m_conv_w, m_conv_b, m_w_down, m_ln2_g, m_ln2_b, v_w_in, v_g_cq, v_g_ckv, v_w_uq, v_w_uk, v_w_uv, v_w_o, v_ln1_g, v_ln1_b, v_w_up, v_conv_w, v_conv_b, v_w_down, v_ln2_g, v_ln2_b):
    wts = dict(zip(WEIGHTS, (w_in, g_cq, g_ckv, w_uq, w_uk, w_uv, w_o, ln1_g, ln1_b, w_up, conv_w, conv_b, w_down, ln2_g, ln2_b)))
    mom = dict(zip(WEIGHTS, (m_w_in, m_g_cq, m_g_ckv, m_w_uq, m_w_uk, m_w_uv, m_w_o, m_ln1_g, m_ln1_b, m_w_up, m_conv_w, m_conv_b, m_w_down, m_ln2_g, m_ln2_b)))
    var = dict(zip(WEIGHTS, (v_w_in, v_g_cq, v_g_ckv, v_w_uq, v_w_uk, v_w_uv, v_w_o, v_ln1_g, v_ln1_b, v_w_up, v_conv_w, v_conv_b, v_w_down, v_ln2_g, v_ln2_b)))

    me = 2 * lax.axis_index("x") + lax.axis_index("y")
    my_c = lax.axis_index("c")
    c_arr = my_c.astype(jnp.int32).reshape(1)
    own = lambda slots, mine: lax.dynamic_update_index_in_dim(slots, mine, me, 0)

    def pack(names):
        return jnp.concatenate([_rows(_mx(wts[n]), SHARD_ROWS[n]) for n in names], axis=0).reshape(2, -1, LANES)

    def unpack(names, gathered, mine):
        buf, full, r = own(gathered, mine).reshape(N_CHIPS, -1, LANES), {}, 0
        for n in names:
            full[n] = _from_chip_blocks(n, buf[:, r:r + SHARD_ROWS[n]])
            r += SHARD_ROWS[n]
        return full

    wp_first = pack(GATHER_FIRST)
    cwp = _rows(conv_w, SHARD_ROWS["conv_w"])
    gathered, cwfull = _gather_weights(wp_first, cwp)
    full = unpack(GATHER_FIRST, gathered, wp_first)
    conv_w_full = _from_chip_blocks("conv_w", own(cwfull, cwp))
    w = _prep_weights_first(full["w_in"], full["w_uq"], w_uk, w_uv)
    late_halves = [_mx(wts[n]).reshape(2, BIG_2D[n][0] // 2, BIG_2D[n][1]) for n in GATHER_LATE]

    def finish(gathered_late):
        w_o_b, w_up_b, w_down_b = (own(a, mine).reshape((N_CHIPS,) + BIG_2D[n])
                                   for a, mine, n in zip(gathered_late, late_halves, GATHER_LATE))
        return _prep_weights_late(w_o_b.reshape(D_MODEL, D_MODEL), w_up_b, w_down_b.reshape(D_FF, D_MODEL))

    def blocked(named):
        return [_blocked(n, a) for n, a in named]

    def add_halves(named, gb, recv):
        return [_add_own_half(gb[i], recv[i], c_arr, name=f"add_half_{n}") for i, (n, _) in enumerate(named)]

    def halve(named, whole):
        gb = blocked(named)
        recv = _exchange_sibling_halves(gb, list(whole), name="exchange_sibling_halves")
        return add_halves(named, gb, recv) + [_add2(a, recv[len(gb) + i], name=f"add_whole_{i}") for i, a in enumerate(whole)]

    comm = dict(late=late_halves, finish=finish, blocked=blocked, add_halves=add_halves)
    loss, grad_x, g = _local_step(x[0], loss_target[0], w, g_cq, g_ckv, ln1_g, ln1_b, conv_w_full, conv_b, ln2_g, ln2_b, comm=comm)

    ps_early, slots_early = g.pop("early")
    g["loss"] = loss.reshape(1)
    *ps_rest, pr = halve([(n, g[n]) for n in REDUCED_LAST], whole=[_pack_flat(g, SMALL_G + ("loss",))])
    *slots_rest, slots_r = _exchange_chips(ps_rest, pr)
    ps = {**dict(zip(REDUCED_LAST, ps_rest)), **dict(zip(REDUCED_EARLY, ps_early))}
    slots = {**dict(zip(REDUCED_LAST, slots_rest)), **dict(zip(REDUCED_EARLY, slots_early))}
    slots = [own(slots[n], lax.dynamic_index_in_dim(ps[n], me, 0, keepdims=False)) for n in BIG]
    slots_r = own(slots_r, pr)
    g_half = [_sum_slots(slots[i], tr=slots[i].shape[1] // 2, name=f"sum_chips_{n}") for i, n in enumerate(BIG)]
    g_small = _unpack_flat(_sum_slots(slots_r, tr=R_SMALL, name="sum_chips_small"), SMALL_G + ("loss",),
                           {**SMALL_G_SHAPE, "loss": (1,)})
    loss = g_small.pop("loss")[0]
    g_other = _exchange_sibling_result(g_half)
    grads = {n: jnp.where(my_c == 0, jnp.concatenate([g_half[i], g_other[i]]), jnp.concatenate([g_other[i], g_half[i]]))
             for i, n in enumerate(BIG)}
    g_small["conv_w"] = lax.dynamic_slice_in_dim(g_small["conv_w"], me * SHARD_SHAPE["conv_w"][1], SHARD_SHAPE["conv_w"][1], 1)
    grads.update(g_small)

    res = {}
    for n in BIG:
        as2d = lambda a: a.reshape(BIG_2D[n])
        d, m, v = _adamw(as2d(wts[n]), grads[n], as2d(mom[n]), as2d(var[n]), tr=BIG_2D[n][0] // 4, name=f"adamw_{n}")
        res[n] = [a.reshape(SHARD_SHAPE[n]) for a in (grads[n], d, m, v)]
    flat = lambda t: _pack_flat(t, SMALL_G)
    dmv = _adamw(flat(wts), flat(g_small), flat(mom), flat(var), tr=R_SMALL, name="adamw_small")
    dmv = [_unpack_flat(a, SMALL_G, SMALL_U_SHAPE) for a in dmv]
    for n in SMALL_G:
        res[n] = [g_small[n]] + [t[n] for t in dmv]
    outs = [res[n][j] for j in range(4) for n in WEIGHTS]
    return (loss, grad_x[None], *outs)
```

```python
import math

import jax
import jax.numpy as jnp
from jax import lax
from jax.experimental import pallas as pl
from jax.experimental.pallas import tpu as pltpu

F32 = jnp.float32
MXU_DTYPE = jnp.bfloat16
GRAD_WIRE_DTYPE = jnp.bfloat16
NEG = -1e30

D_MODEL = 1024
HEADS = 8
HEAD_DIM = 64
Q_RANK = 256
KV_RANK = 128
NOPE = 64
ROPE = 32
QK_PAD = 128
IN_WIDTH = 1952
IN_EXT = 2048
D_FF = 2816
DIL_PAIRS = ((128, 1), (512, 4), (2048, 16))
DIL_BLOCK = 128
ROPE_THETA = 10000.0
DN_ALPHA = 2.0 ** 0.25
LN_EPS = 1e-5
RMS_EPS = 1e-6
MLA_SCALE = 1.0 / math.sqrt(NOPE + ROPE)
LOG2_E = math.log2(math.e)
DIL_SCALE = 1.0 / math.sqrt(HEAD_DIM)

ADAM_LR = 0.001
ADAM_B1 = 0.9
ADAM_B2 = 0.999
ADAM_EPS = 1e-08
ADAM_WD = 0.01
ADAM_STEP = 10

LANES = 128
SUBLANES = 8
VMEM_LIMIT_BYTES = 56 * 1024 * 1024
DW_TOKENS = 1024

MESH = pl.DeviceIdType.MESH


def _params(*sem):
    return pltpu.CompilerParams(dimension_semantics=sem, vmem_limit_bytes=VMEM_LIMIT_BYTES)


def _dot(a, b):
    return jnp.dot(a, b, preferred_element_type=F32)


def _dot_nt(a, b):
    return lax.dot_general(a, b, (((1,), (1,)), ((), ())), preferred_element_type=F32)


def _dot_tn(a, b):
    return lax.dot_general(a, b, (((0,), (0,)), ((), ())), preferred_element_type=F32)


def _mx(a):
    return a.astype(MXU_DTYPE)


def _mm_nn(a, b, *, name, tm, tn, tk, out_dtype=F32, add=None, add_scale=1.0):
    m, kdim = a.shape
    blocked = b.ndim == 3
    n = b.shape[0] * b.shape[2] if blocked else b.shape[1]
    nk = kdim // tk

    def body(*refs):
        if add is None:
            a_ref, b_ref, o_ref, acc = refs
        else:
            a_ref, b_ref, c_ref, o_ref, acc = refs
        k = pl.program_id(2)

        @pl.when(k == 0)
        def _():
            acc[...] = jnp.zeros_like(acc)

        acc[...] += _dot(_mx(a_ref[...]), _mx(b_ref[...]))

        @pl.when(k == nk - 1)
        def _():
            r = acc[...]
            if add is not None:
                r = r + add_scale * c_ref[...]
            o_ref[...] = r.astype(out_dtype)

    b_spec = (pl.BlockSpec((None, tk, tn), lambda i, j, k: (j, k, 0)) if blocked
              else pl.BlockSpec((tk, tn), lambda i, j, k: (k, j)))
    in_specs = [pl.BlockSpec((tm, tk), lambda i, j, k: (i, k)), b_spec]
    args = [a, b]
    if add is not None:
        in_specs.append(pl.BlockSpec((tm, tn), lambda i, j, k: (i, j)))
        args.append(add)
    return pl.pallas_call(
        body, name=name,
        out_shape=jax.ShapeDtypeStruct((m, n), out_dtype),
        grid=(m // tm, n // tn, nk),
        in_specs=in_specs,
        out_specs=pl.BlockSpec((tm, tn), lambda i, j, k: (i, j)),
        scratch_shapes=[pltpu.VMEM((tm, tn), F32)],
        compiler_params=_params("parallel", "parallel", "arbitrary"),
    )(*args)


def _mm_tn(a, b, *, name, tm, tn, ts, out_dtype=F32):
    s, m = a.shape
    n = b.shape[1]
    ns = s // ts

    def body(a_ref, b_ref, o_ref, acc):
        k = pl.program_id(2)

        @pl.when(k == 0)
        def _():
            acc[...] = jnp.zeros_like(acc)

        acc[...] += _dot_tn(_mx(a_ref[...]), _mx(b_ref[...]))

        @pl.when(k == ns - 1)
        def _():
            o_ref[...] = acc[...].astype(out_dtype)

    return pl.pallas_call(
        body, name=name,
        out_shape=jax.ShapeDtypeStruct((m, n), out_dtype),
        grid=(m // tm, n // tn, ns),
        in_specs=[pl.BlockSpec((ts, tm), lambda i, j, k: (k, i)),
                  pl.BlockSpec((ts, tn), lambda i, j, k: (k, j))],
        out_specs=pl.BlockSpec((tm, tn), lambda i, j, k: (i, j)),
        scratch_shapes=[pltpu.VMEM((tm, tn), F32)],
        compiler_params=_params("parallel", "parallel", "arbitrary"),
    )(a, b)


def _in_proj(x, w_in_ext, *, tm):
    s = x.shape[0]
    mla_w = 4 * LANES
    dil_w = HEADS * HEAD_DIM
    dils = [d for _, d in DIL_PAIRS]

    def body(x_ref, w_ref, h_ref, *rest):
        outs, sc = rest[:-1], rest[-1]
        xb = _mx(x_ref[...])
        h_ref[...] = _dot(xb, w_ref[:, 0:mla_w])
        for j in range(3):
            part = _dot(xb, w_ref[:, mla_w + j * dil_w:mla_w + (j + 1) * dil_w])
            for hd in range(HEADS):
                sc[hd] = part[:, hd * HEAD_DIM:(hd + 1) * HEAD_DIM]
            for b, d in enumerate(dils):
                _store_residue_major(outs[3 * j + b], sc, d, tm)

    shapes, specs = _residue_major_outs(s, tm, dils, MXU_DTYPE)
    res = pl.pallas_call(
        body, name="in_proj",
        out_shape=(jax.ShapeDtypeStruct((s, mla_w), F32),) + shapes * 3,
        grid=(s // tm,),
        in_specs=[pl.BlockSpec((tm, D_MODEL), lambda i: (i, 0)), pl.BlockSpec((D_MODEL, IN_EXT), lambda i: (0, 0))],
        out_specs=(pl.BlockSpec((tm, mla_w), lambda i: (i, 0)),) + specs * 3,
        scratch_shapes=[pltpu.VMEM((HEADS, tm, HEAD_DIM), F32)],
        compiler_params=_params("parallel"),
    )(x, w_in_ext)
    hm = lambda a: a.reshape(HEADS, s, HEAD_DIM)
    return res[0], [hm(a) for a in res[1:4]], [hm(a) for a in res[4:7]], [hm(a) for a in res[7:10]]


def _residue_major_outs(s, tm, dils, dtype):
    shapes, specs = [], []
    for d in dils:
        if d == 1:
            shapes.append(jax.ShapeDtypeStruct((HEADS, s, HEAD_DIM), dtype))
            specs.append(pl.BlockSpec((HEADS, tm, HEAD_DIM), lambda i: (0, i, 0)))
        else:
            shapes.append(jax.ShapeDtypeStruct((HEADS, d, s // d, HEAD_DIM), dtype))
            specs.append(pl.BlockSpec((HEADS, d, tm // d, HEAD_DIM), lambda i: (0, 0, i, 0)))
    return tuple(shapes), tuple(specs)


def _store_residue_major(o_ref, src_ref, d, tm):
    if d == 1:
        o_ref[...] = src_ref[...].astype(o_ref.dtype)
    else:
        for r in range(d):
            o_ref[:, r] = src_ref[:, pl.ds(r, tm // d, stride=d), :].astype(o_ref.dtype)


def _load_token_order(dst_ref, src_ref, d, tm, accumulate=False):
    if d == 1:
        dst_ref[...] = dst_ref[...] + src_ref[...] if accumulate else src_ref[...]
    else:
        for r in range(d):
            rows = pl.ds(r, tm // d, stride=d)
            dst_ref[:, rows, :] = dst_ref[:, rows, :] + src_ref[:, r] if accumulate else src_ref[:, r]


def _attn_bwd_heads(dz1, w_o_t, a_mla, a_dil, *, tm, swap=()):
    s = dz1.shape[0]
    half = HEADS * HEAD_DIM
    dils = [d for _, d in DIL_PAIRS]
    nsw = len(swap)
    n_steps = s // tm

    def body(*refs):
        dz_ref, w_ref, am_ref, ad_ref = refs[:4]
        gs_refs = refs[4:4 + nsw]
        dom_ref, dd_ref = refs[4 + nsw:6 + nsw]
        dod_refs = refs[6 + nsw:6 + nsw + len(dils)]
        os_refs = refs[6 + nsw + len(dils):6 + 2 * nsw + len(dils)]
        if nsw:
            send_sems, recv_sems = refs[6 + 2 * nsw + len(dils):]
            i = pl.program_id(0)
            _swap_halves_in_steps(gs_refs, os_refs, send_sems, recv_sems, first=i == 0, last=i == n_steps - 1)
        dzb = _mx(dz_ref[...])
        for j, (a_ref, o_ref) in enumerate(((am_ref, dom_ref), (ad_ref, dod_refs[0]))):
            da = _dot(dzb, w_ref[:, j * half:(j + 1) * half])
            prod = da * a_ref[...]
            for hd in range(HEADS):
                sl = slice(hd * HEAD_DIM, (hd + 1) * HEAD_DIM)
                o_ref[hd] = da[:, sl].astype(o_ref.dtype)
                dd_ref[:, j * HEADS + hd:j * HEADS + hd + 1] = jnp.sum(prod[:, sl], axis=-1, keepdims=True)
        for b, d in enumerate(dils[1:]):
            _store_residue_major(dod_refs[1 + b], dod_refs[0], d, tm)

    hspec = pl.BlockSpec((HEADS, tm, HEAD_DIM), lambda i: (0, i, 0))
    row = lambda w: pl.BlockSpec((tm, w), lambda i: (i, 0))
    shapes, specs = _residue_major_outs(s, tm, dils, F32)
    n_sem = nsw * N_CHIPS
    do_mla, dd, *rest = pl.pallas_call(
        body, name="attn_bwd_heads",
        out_shape=(jax.ShapeDtypeStruct((HEADS, s, HEAD_DIM), MXU_DTYPE), jax.ShapeDtypeStruct((s, 2 * HEADS), F32)) + shapes
        + tuple(jax.ShapeDtypeStruct((N_CHIPS,) + a.shape[2:], F32) for a in swap),
        grid=(n_steps,),
        in_specs=[row(D_MODEL), pl.BlockSpec((D_MODEL, D_MODEL), lambda i: (0, 0)), row(half), row(half)] + [ANY] * nsw,
        out_specs=(hspec, row(2 * HEADS)) + specs + (ANY,) * nsw,
        scratch_shapes=[pltpu.SemaphoreType.DMA((n_sem,)), pltpu.SemaphoreType.DMA((n_sem,))] if nsw else [],
        compiler_params=pltpu.CompilerParams(dimension_semantics=("arbitrary",), vmem_limit_bytes=VMEM_LIMIT_BYTES,
                                             has_side_effects=nsw > 0),
    )(dz1, w_o_t, a_mla, a_dil, *swap)
    do_dil, received = rest[:len(dils)], rest[len(dils):]
    return do_mla, [a.reshape(HEADS, s, HEAD_DIM) for a in do_dil], dd, received


def _dil_merge(parts, *, ts):
    hds, s, e = parts[0][0].shape
    dils = [d for _, d in DIL_PAIRS]

    def body(*refs):
        o_ref, sc = refs[9], refs[10]
        for j in range(3):
            for b, d in enumerate(dils):
                _load_token_order(sc, refs[3 * b + j], d, ts, accumulate=b > 0)
            tot = sc[...]
            for hd in range(hds):
                col = j * hds * e + hd * e
                o_ref[:, col:col + e] = tot[hd].astype(o_ref.dtype)

    _, specs = _residue_major_outs(s, ts, dils, F32)
    view = lambda a, d: a if d == 1 else a.reshape(hds, d, s // d, e)
    return pl.pallas_call(
        body, name="dil_merge",
        out_shape=jax.ShapeDtypeStruct((s, 3 * hds * e), MXU_DTYPE),
        grid=(s // ts,),
        in_specs=[specs[b] for b in range(3) for _ in range(3)],
        out_specs=pl.BlockSpec((ts, 3 * hds * e), lambda i: (i, 0)),
        scratch_shapes=[pltpu.VMEM((hds, ts, e), F32)],
        compiler_params=_params("parallel"),
    )(*[view(parts[b][j], dils[b]) for b in range(3) for j in range(3)])


def _rope_tables(s):
    half = ROPE // 2
    freqs = ROPE_THETA ** (-jnp.arange(half, dtype=F32) / half)
    ang = jnp.arange(s).astype(F32)[:, None] * freqs[None, :]
    cos, sin = jnp.cos(ang), jnp.sin(ang)
    z = lambda w: jnp.zeros((s, w), F32)
    c = jnp.concatenate([jnp.ones((s, NOPE), F32), cos, cos, z(32)], axis=1)
    s1 = jnp.concatenate([z(NOPE + half), sin, z(32)], axis=1)
    s2 = jnp.concatenate([z(NOPE), -sin, z(half + 32)], axis=1)
    mask = jnp.concatenate([z(NOPE), jnp.ones((s, ROPE), F32), z(32)], axis=1)
    return c, s1, s2, mask


def _rope(x, c, s1, s2):
    return x * c + pltpu.roll(x, 16, 1) * s1 + pltpu.roll(x, LANES - 16, 1) * s2


def _unrope(dy, c, s1, s2):
    return dy * c + pltpu.roll(dy * s1, LANES - 16, 1) + pltpu.roll(dy * s2, 16, 1)


def _rms(x):
    r = lax.rsqrt(jnp.mean(x * x, axis=-1, keepdims=True) + RMS_EPS)
    return x * r, r


def _mla_prep_fwd(h, g_cq, g_ckv, wq, wk, wv, wv_t, tabs, *, tm):
    s = h.shape[0]
    c_t, s1_t, s2_t, _ = tabs

    def body(h_ref, gq_ref, gkv_ref, wq_ref, wk_ref, wv_ref, wvt_ref, c_ref, s1_ref, s2_ref,
             q_ref, k_ref, v_ref, vt_ref):
        cq = h_ref[:, 0:Q_RANK]
        ckv = h_ref[:, Q_RANK:Q_RANK + KV_RANK]
        kr = h_ref[:, Q_RANK + KV_RANK:Q_RANK + KV_RANK + QK_PAD]
        c, s1, s2 = c_ref[...], s1_ref[...], s2_ref[...]
        cqn = _mx(_rms(cq)[0] * gq_ref[...])
        ckvn = _mx(_rms(ckv)[0] * gkv_ref[...])
        kr_rot = _rope(kr, c, s1, s2)
        for hd in range(HEADS):
            q_ref[hd] = _rope(_dot(cqn, wq_ref[hd]), c, s1, s2).astype(q_ref.dtype)
            k_ref[hd] = (_dot(ckvn, wk_ref[hd]) + kr_rot).astype(k_ref.dtype)
            v_ref[hd] = _dot(ckvn, wv_ref[hd]).astype(v_ref.dtype)
            vt_ref[hd] = _dot_nt(wvt_ref[hd], ckvn).astype(vt_ref.dtype)

    full = lambda shp: pl.BlockSpec(shp, lambda i: (0,) * len(shp))
    row = lambda w: pl.BlockSpec((tm, w), lambda i: (i, 0))
    return pl.pallas_call(
        body, name="mla_prep_fwd",
        out_shape=(jax.ShapeDtypeStruct((HEADS, s, QK_PAD), MXU_DTYPE),
                   jax.ShapeDtypeStruct((HEADS, s, QK_PAD), MXU_DTYPE),
                   jax.ShapeDtypeStruct((HEADS, s, HEAD_DIM), MXU_DTYPE),
                   jax.ShapeDtypeStruct((HEADS, HEAD_DIM, s), MXU_DTYPE)),
        grid=(s // tm,),
        in_specs=[row(4 * LANES), full((1, Q_RANK)), full((1, KV_RANK)),
                  full((HEADS, Q_RANK, QK_PAD)), full((HEADS, KV_RANK, QK_PAD)), full((HEADS, KV_RANK, HEAD_DIM)),
                  full((HEADS, HEAD_DIM, KV_RANK)), row(LANES), row(LANES), row(LANES)],
        out_specs=(pl.BlockSpec((HEADS, tm, QK_PAD), lambda i: (0, i, 0)),
                   pl.BlockSpec((HEADS, tm, QK_PAD), lambda i: (0, i, 0)),
                   pl.BlockSpec((HEADS, tm, HEAD_DIM), lambda i: (0, i, 0)),
                   pl.BlockSpec((HEADS, HEAD_DIM, tm), lambda i: (0, 0, i))),
        compiler_params=_params("parallel"),
    )(h, g_cq, g_ckv, wq, wk, wv, wv_t, c_t, s1_t, s2_t)


def _mla_prep_bwd(h, dq, dk, dv, g_cq, g_ckv, wq_t, wk_t, wv_t, tabs, *, tm):
    s = h.shape[0]
    c_t, s1_t, s2_t, mask_t = tabs

    def body(h_ref, dq_ref, dk_ref, dv_ref, gq_ref, gkv_ref, wqt_ref, wkt_ref, wvt_ref,
             c_ref, s1_ref, s2_ref, mask_ref, dh_ref, dwq_ref, dwk_ref, dwv_ref, dgq_ref, dgkv_ref):
        i = pl.program_id(0)

        @pl.when(i == 0)
        def _():
            dwq_ref[...] = jnp.zeros_like(dwq_ref)
            dwk_ref[...] = jnp.zeros_like(dwk_ref)
            dwv_ref[...] = jnp.zeros_like(dwv_ref)
            dgq_ref[...] = jnp.zeros_like(dgq_ref)
            dgkv_ref[...] = jnp.zeros_like(dgkv_ref)

        cq = h_ref[:, 0:Q_RANK]
        ckv = h_ref[:, Q_RANK:Q_RANK + KV_RANK]
        c, s1, s2 = c_ref[...], s1_ref[...], s2_ref[...]
        cqh, rq = _rms(cq)
        ckvh, rkv = _rms(ckv)
        gq, gkv = gq_ref[...], gkv_ref[...]
        cqn = _mx(cqh * gq)
        ckvn = _mx(ckvh * gkv)
        dcqn = jnp.zeros((tm, Q_RANK), F32)
        dckvn = jnp.zeros((tm, KV_RANK), F32)
        dkr = jnp.zeros((tm, QK_PAD), F32)
        for hd in range(HEADS):
            dqh = _mx(_unrope(dq_ref[hd], c, s1, s2))
            dcqn = dcqn + _dot(dqh, wqt_ref[hd])
            dwq_ref[hd] += _dot_tn(cqn, dqh)
            dkh = dk_ref[hd]
            dkr = dkr + dkh
            dkh = _mx(dkh)
            dckvn = dckvn + _dot(dkh, wkt_ref[hd])
            dwk_ref[hd] += _dot_tn(ckvn, dkh)
            dvh = _mx(dv_ref[hd])
            dckvn = dckvn + _dot(dvh, wvt_ref[hd])
            dwv_ref[hd] += _dot_tn(ckvn, dvh)
        dgq_ref[...] += jnp.sum(dcqn * cqh, axis=0, keepdims=True)
        dgkv_ref[...] += jnp.sum(dckvn * ckvh, axis=0, keepdims=True)
        gd = dcqn * gq
        dh_ref[:, 0:Q_RANK] = rq * (gd - cqh * jnp.mean(gd * cqh, axis=-1, keepdims=True))
        gd = dckvn * gkv
        dh_ref[:, Q_RANK:Q_RANK + KV_RANK] = rkv * (gd - ckvh * jnp.mean(gd * ckvh, axis=-1, keepdims=True))
        dh_ref[:, Q_RANK + KV_RANK:Q_RANK + KV_RANK + QK_PAD] = _unrope(dkr, c, s1, s2) * mask_ref[...]

    full = lambda shp: pl.BlockSpec(shp, lambda i: (0,) * len(shp))
    row = lambda w: pl.BlockSpec((tm, w), lambda i: (i, 0))
    hrow = lambda w: pl.BlockSpec((HEADS, tm, w), lambda i: (0, i, 0))
    return pl.pallas_call(
        body, name="mla_prep_bwd",
        out_shape=(jax.ShapeDtypeStruct((s, 4 * LANES), F32),
                   jax.ShapeDtypeStruct((HEADS, Q_RANK, QK_PAD), F32),
                   jax.ShapeDtypeStruct((HEADS, KV_RANK, QK_PAD), F32),
                   jax.ShapeDtypeStruct((HEADS, KV_RANK, HEAD_DIM), F32),
                   jax.ShapeDtypeStruct((1, Q_RANK), F32),
                   jax.ShapeDtypeStruct((1, KV_RANK), F32)),
        grid=(s // tm,),
        in_specs=[row(4 * LANES), hrow(QK_PAD), hrow(QK_PAD), hrow(HEAD_DIM),
                  full((1, Q_RANK)), full((1, KV_RANK)),
                  full((HEADS, QK_PAD, Q_RANK)), full((HEADS, QK_PAD, KV_RANK)), full((HEADS, HEAD_DIM, KV_RANK)),
                  row(LANES), row(LANES), row(LANES), row(LANES)],
        out_specs=(row(4 * LANES), full((HEADS, Q_RANK, QK_PAD)), full((HEADS, KV_RANK, QK_PAD)),
                   full((HEADS, KV_RANK, HEAD_DIM)), full((1, Q_RANK)), full((1, KV_RANK))),
        compiler_params=_params("arbitrary"),
    )(h, dq, dk, dv, g_cq, g_ckv, wq_t, wk_t, wv_t, c_t, s1_t, s2_t, mask_t)


def _bdot(a, b, ca, cb):
    return lax.dot_general(a, b, (((ca,), (cb,)), ((0,), (0,))), preferred_element_type=F32)


def _causal_mask_t(t):
    kk = lax.broadcasted_iota(jnp.int32, (t, t), 0)
    qq = lax.broadcasted_iota(jnp.int32, (t, t), 1)
    return (qq >= kk)[None]


def _mla_attn_fwd(q, k, v_t, *, t, g, late=None):
    hds, s, _ = q.shape
    n = s // t
    n_groups = hds // g

    nl = 0 if late is None else len(late)

    def body(*refs):
        q_ref, k_ref, vt_ref = refs[:3]
        wp_refs = refs[3:3 + nl]
        o_ref, lse_ref = refs[3 + nl:5 + nl]
        wout_refs = refs[5 + nl:5 + 2 * nl]
        m_sc, l_sc, acc_sc = refs[5 + 2 * nl:8 + 2 * nl]
        hg, qi, ki = pl.program_id(0), pl.program_id(1), pl.program_id(2)
        if nl:
            send_sems, recv_sems = refs[8 + 2 * nl:]
            tail = jnp.logical_and(hg == n_groups - 1, qi == n - 1)
            _gather_in_steps(wp_refs, wout_refs, send_sems, recv_sems,
                             first=jnp.logical_and(hg == 0, jnp.logical_and(qi == 0, ki == 0)),
                             mid=jnp.logical_and(tail, ki == 0), last=jnp.logical_and(tail, ki == n - 1))

        @pl.when(ki == 0)
        def _():
            m_sc[...] = jnp.full_like(m_sc, NEG)
            l_sc[...] = jnp.zeros_like(l_sc)
            acc_sc[...] = jnp.zeros_like(acc_sc)

        def step(masked):
            sc = _bdot(k_ref[...], q_ref[...], 2, 2)
            if masked:
                sc = jnp.where(_causal_mask_t(t), sc, NEG)
            m_prev = m_sc[...]
            m_new = jnp.maximum(m_prev, jnp.max(sc, axis=1, keepdims=True))
            p = jnp.exp2((sc - m_new) * (MLA_SCALE * LOG2_E))
            a = jnp.exp2((m_prev - m_new) * (MLA_SCALE * LOG2_E))
            l_sc[...] = a * l_sc[...] + jnp.sum(p, axis=1, keepdims=True)
            acc_sc[...] = a * acc_sc[...] + _bdot(vt_ref[...], _mx(p), 2, 1)
            m_sc[...] = m_new

        @pl.when(ki < qi)
        def _():
            step(False)

        @pl.when(ki == qi)
        def _():
            step(True)
            o_ref[...] = acc_sc[...] / l_sc[...]
            lse_ref[...] = m_sc[...] * MLA_SCALE + jnp.log(l_sc[...])

    qspec = pl.BlockSpec((g, t, QK_PAD), lambda h, i, j: (h, i, 0))
    kspec = pl.BlockSpec((g, t, QK_PAD), lambda h, i, j: (h, jnp.minimum(i, j), 0))
    vspec = pl.BlockSpec((g, HEAD_DIM, t), lambda h, i, j: (h, 0, jnp.minimum(i, j)))
    out_shape = [jax.ShapeDtypeStruct((hds, HEAD_DIM, s), F32), jax.ShapeDtypeStruct((hds, 1, s), F32)]
    in_specs = [qspec, kspec, vspec]
    out_specs = [pl.BlockSpec((g, HEAD_DIM, t), lambda h, i, j: (h, 0, i)), pl.BlockSpec((g, 1, t), lambda h, i, j: (h, 0, i))]
    scratch = [pltpu.VMEM((g, 1, t), F32), pltpu.VMEM((g, 1, t), F32), pltpu.VMEM((g, HEAD_DIM, t), F32)]
    args = [q, k, v_t]
    if nl:
        out_shape += [jax.ShapeDtypeStruct((N_CHIPS,) + a.shape, a.dtype) for a in late]
        in_specs += [ANY] * nl
        out_specs += [ANY] * nl
        scratch += [pltpu.SemaphoreType.DMA((6 * nl,)), pltpu.SemaphoreType.DMA((6 * nl,))]
        args += list(late)
    return pl.pallas_call(
        body, name="mla_attn_fwd",
        out_shape=tuple(out_shape), grid=(n_groups, n, n),
        in_specs=in_specs, out_specs=tuple(out_specs), scratch_shapes=scratch,
        compiler_params=pltpu.CompilerParams(dimension_semantics=("arbitrary",) * 3, vmem_limit_bytes=VMEM_LIMIT_BYTES,
                                             has_side_effects=nl > 0),
    )(*args)


def _mla_attn_bwd(q, k, v, do, lse, dd, *, t, g, early=()):
    hds, s, _ = q.shape
    n = s // t
    n_groups = hds // g
    ne = len(early)

    def body(*refs):
        q_ref, k_ref, v_ref, do_ref, lse_ref, dd_ref = refs[:6]
        ps_refs = refs[6:6 + ne]
        dq_ref, dk_ref, dv_ref = refs[6 + ne:9 + ne]
        ss_refs = refs[9 + ne:9 + 2 * ne]
        dq_sc, dk_sc, dv_sc = refs[9 + 2 * ne:12 + 2 * ne]
        hg, ki, qi = pl.program_id(0), pl.program_id(1), pl.program_id(2)
        if ne:
            send_sems, recv_sems = refs[12 + 2 * ne:]
            _exchange_in_steps(ps_refs, ss_refs, send_sems, recv_sems,
                               first=jnp.logical_and(hg == 0, jnp.logical_and(ki == 0, qi == 0)),
                               last=jnp.logical_and(hg == n_groups - 1, jnp.logical_and(ki == n - 1, qi == n - 1)))

        @pl.when(jnp.logical_and(ki == 0, qi == 0))
        def _():
            dq_sc[...] = jnp.zeros_like(dq_sc)

        @pl.when(qi == 0)
        def _():
            dk_sc[...] = jnp.zeros_like(dk_sc)
            dv_sc[...] = jnp.zeros_like(dv_sc)

        def step(masked):
            qb, kb, dob = q_ref[...], k_ref[...], do_ref[...]
            sc = _bdot(kb, qb, 2, 2) * MLA_SCALE
            if masked:
                sc = jnp.where(_causal_mask_t(t), sc, NEG)
            p = jnp.exp(sc - lse_ref[...])
            dv_sc[...] += _bdot(_mx(p), dob, 2, 1)
            dp = _bdot(v_ref[...], dob, 2, 2)
            ds = _mx(p * (dp - dd_ref[...]) * MLA_SCALE)
            dk_sc[...] += _bdot(ds, qb, 2, 1)
            dq_sc[qi] += _bdot(ds, kb, 1, 1)

        @pl.when(qi == ki)
        def _():
            step(True)

        @pl.when(qi > ki)
        def _():
            step(False)

        @pl.when(qi == n - 1)
        def _():
            dk_ref[...] = dk_sc[...]
            dv_ref[...] = dv_sc[...]

        @pl.when(jnp.logical_and(ki == n - 1, qi == n - 1))
        def _():
            for j in range(n):
                dq_ref[:, j * t:(j + 1) * t, :] = dq_sc[j]

    qs = lambda w: pl.BlockSpec((g, t, w), lambda h, j, i: (h, jnp.maximum(i, j), 0))
    ks = lambda w: pl.BlockSpec((g, t, w), lambda h, j, i: (h, j, 0))
    rowq = pl.BlockSpec((g, 1, t), lambda h, j, i: (h, 0, jnp.maximum(i, j)))
    scratch = [pltpu.VMEM((n, g, t, QK_PAD), F32), pltpu.VMEM((g, t, QK_PAD), F32), pltpu.VMEM((g, t, HEAD_DIM), F32)]
    if ne:
        scratch += [pltpu.SemaphoreType.DMA((3 * ne,)), pltpu.SemaphoreType.DMA((3 * ne,))]
    return pl.pallas_call(
        body, name="mla_attn_bwd",
        out_shape=(jax.ShapeDtypeStruct((hds, s, QK_PAD), F32), jax.ShapeDtypeStruct((hds, s, QK_PAD), F32),
                   jax.ShapeDtypeStruct((hds, s, HEAD_DIM), F32)) + tuple(jax.ShapeDtypeStruct(a.shape, a.dtype) for a in early),
        grid=(n_groups, n, n),
        in_specs=[qs(QK_PAD), ks(QK_PAD), ks(HEAD_DIM), qs(HEAD_DIM), rowq, rowq] + [ANY] * ne,
        out_specs=(pl.BlockSpec((g, s, QK_PAD), lambda h, j, i: (h, 0, 0)), ks(QK_PAD), ks(HEAD_DIM)) + (ANY,) * ne,
        scratch_shapes=scratch,
        compiler_params=pltpu.CompilerParams(dimension_semantics=("arbitrary",) * 3, vmem_limit_bytes=VMEM_LIMIT_BYTES,
                                             has_side_effects=ne > 0),
    )(q, k, v, do, lse, dd, *early)


def _perm_row(a, dil):
    if dil == 1:
        return a
    hds, _, s = a.shape
    return a.reshape(hds, s // dil, dil).transpose(0, 2, 1).reshape(hds, 1, s)


def _unperm_row(a, dil):
    if dil == 1:
        return a
    hds, _, s = a.shape
    return a.reshape(hds, dil, s // dil).transpose(0, 2, 1).reshape(hds, 1, s)


def _dil_bias(dil):
    slopes = 2.0 ** (-8.0 * jnp.arange(1, HEADS + 1, dtype=F32) / HEADS)
    ik = jnp.arange(DIL_BLOCK)[:, None]
    iq = jnp.arange(DIL_BLOCK)[None, :]
    off_c = iq - ik
    off_p = iq - ik + DIL_BLOCK
    b_c = -slopes[:, None, None] * (off_c * dil).astype(F32)[None]
    b_p = -slopes[:, None, None] * (off_p * dil).astype(F32)[None]
    b_c = jnp.where((off_c >= 0)[None], b_c, NEG)
    b_p = jnp.where((off_p <= DIL_BLOCK)[None], b_p, NEG)
    return b_c, b_p


def _dil_fwd(q, k, v, dil, *, name):
    hds, s, e = q.shape
    blk = DIL_BLOCK
    nblk = s // blk
    nb = nblk // dil
    pair = 2 if nb % 2 == 0 else 1
    b_c, b_p = _dil_bias(dil)

    def body(q_ref, k_ref, kp_ref, v_ref, vp_ref, bc_ref, bp_ref, o_ref, lse_ref):
        first = ((pair * pl.program_id(0)) % nb) == 0
        bc, bp = bc_ref[...], bp_ref[...]
        for j in range(pair):
            rows = slice(j * blk, (j + 1) * blk)
            qb = q_ref[:, rows, :]
            if j == 0:
                kp, vp = kp_ref[...], vp_ref[...]
            else:
                kp, vp = k_ref[:, (j - 1) * blk:j * blk, :], v_ref[:, (j - 1) * blk:j * blk, :]
            s_c = _bdot(k_ref[:, rows, :], qb, 2, 2) * DIL_SCALE + bc
            s_p = _bdot(kp, qb, 2, 2) * DIL_SCALE + bp
            if j == 0:
                s_p = jnp.where(first, NEG, s_p)
            m = jnp.maximum(jnp.max(s_c, axis=1, keepdims=True), jnp.max(s_p, axis=1, keepdims=True))
            p_c = jnp.exp(s_c - m)
            p_p = jnp.exp(s_p - m)
            l = jnp.sum(p_c, axis=1, keepdims=True) + jnp.sum(p_p, axis=1, keepdims=True)
            o = _bdot(_mx(p_c), v_ref[:, rows, :], 1, 1) + _bdot(_mx(p_p), vp, 1, 1)
            o_ref[:, rows, :] = o / jnp.swapaxes(l, 1, 2)
            lse_ref[:, :, rows] = m + jnp.log(l)

    cur = lambda w: pl.BlockSpec((hds, pair * blk, w), lambda b: (0, b, 0))
    prev = lambda w: pl.BlockSpec((hds, blk, w), lambda b: (0, jnp.maximum(pair * b - 1, 0), 0))
    bias = pl.BlockSpec((hds, blk, blk), lambda b: (0, 0, 0))
    return pl.pallas_call(
        body, name=name,
        out_shape=(jax.ShapeDtypeStruct((hds, s, e), F32), jax.ShapeDtypeStruct((hds, 1, s), F32)),
        grid=(nblk // pair,),
        in_specs=[cur(e), cur(e), prev(e), cur(e), prev(e), bias, bias],
        out_specs=(cur(e), pl.BlockSpec((hds, 1, pair * blk), lambda b: (0, 0, b))),
        compiler_params=_params("parallel"),
    )(q, k, k, v, v, b_c, b_p)


def _dil_combine(os_, lses, *, ts):
    hds, s, e = os_[0].shape
    dils = [d for _, d in DIL_PAIRS]

    def body(o0, o1, o2, l0, l1, l2, o_ref, l_ref, sc1, sc2):
        _load_token_order(sc1, o1, dils[1], ts)
        _load_token_order(sc2, o2, dils[2], ts)
        a0, a1, a2 = l0[...], l1[...], l2[...]
        m = jnp.maximum(jnp.maximum(a0, a1), a2)
        e0, e1, e2 = jnp.exp(a0 - m), jnp.exp(a1 - m), jnp.exp(a2 - m)
        tot = e0 + e1 + e2
        col = lambda w: jnp.swapaxes(w, 1, 2)
        res = (col(e0 / tot) * o0[...] + col(e1 / tot) * sc1[...]) + col(e2 / tot) * sc2[...]
        for hd in range(hds):
            o_ref[:, hd * e:(hd + 1) * e] = res[hd]
        l_ref[...] = m + jnp.log(tot)

    _, specs = _residue_major_outs(s, ts, dils, F32)
    view = lambda a, d: a if d == 1 else a.reshape(hds, d, s // d, e)
    rspec = pl.BlockSpec((hds, 1, ts), lambda i: (0, 0, i))
    return pl.pallas_call(
        body, name="dil_combine",
        out_shape=(jax.ShapeDtypeStruct((s, hds * e), F32), jax.ShapeDtypeStruct((hds, 1, s), F32)),
        grid=(s // ts,),
        in_specs=list(specs) + [rspec] * 3,
        out_specs=(pl.BlockSpec((ts, hds * e), lambda i: (i, 0)), rspec),
        scratch_shapes=[pltpu.VMEM((hds, ts, e), F32), pltpu.VMEM((hds, ts, e), F32)],
        compiler_params=_params("parallel"),
    )(*[view(a, d) for a, d in zip(os_, dils)], *lses)


def _dil_bwd(q, k, v, do, lj, dd, dil, *, name):
    hds, s, e = q.shape
    blk = DIL_BLOCK
    nblk = s // blk
    nb = nblk // dil
    pair = 2 if nb % 2 == 0 else 1
    b_c, b_p = _dil_bias(dil)

    def body(q_ref, qn_ref, k_ref, kp_ref, v_ref, vp_ref, do_ref, don_ref, l_ref, ln_ref, d_ref, dn_ref,
             bc_ref, bp_ref, dq_ref, dk_ref, dv_ref):
        b0 = pair * pl.program_id(0)
        first = (b0 % nb) == 0
        nxt = jnp.logical_and(b0 + pair < nblk, ((b0 + pair) % nb) != 0)
        bc, bp = bc_ref[...], bp_ref[...]
        for j in range(pair):
            rows = slice(j * blk, (j + 1) * blk)
            qb, kc, vc = q_ref[:, rows, :], k_ref[:, rows, :], v_ref[:, rows, :]
            dob, l, d = _mx(do_ref[:, rows, :]), l_ref[:, :, rows], d_ref[:, :, rows]
            if j == 0:
                kp, vp = kp_ref[...], vp_ref[...]
            else:
                kp, vp = k_ref[:, (j - 1) * blk:j * blk, :], v_ref[:, (j - 1) * blk:j * blk, :]
            p_c = jnp.exp(_bdot(kc, qb, 2, 2) * DIL_SCALE + bc - l)
            p_p = jnp.exp(_bdot(kp, qb, 2, 2) * DIL_SCALE + bp - l)
            if j == 0:
                p_p = jnp.where(first, 0.0, p_p)
            ds_c = _mx(p_c * (_bdot(vc, dob, 2, 2) - d) * DIL_SCALE)
            ds_p = _mx(p_p * (_bdot(vp, dob, 2, 2) - d) * DIL_SCALE)
            dq_ref[:, rows, :] = _bdot(ds_c, kc, 1, 1) + _bdot(ds_p, kp, 1, 1)
            if j < pair - 1:
                nrows = slice((j + 1) * blk, (j + 2) * blk)
                qn, donb, ln, dn = q_ref[:, nrows, :], _mx(do_ref[:, nrows, :]), l_ref[:, :, nrows], d_ref[:, :, nrows]
            else:
                qn, donb, ln, dn = qn_ref[...], _mx(don_ref[...]), ln_ref[...], dn_ref[...]
            p_n = jnp.exp(_bdot(kc, qn, 2, 2) * DIL_SCALE + bp - ln)
            if j == pair - 1:
                p_n = jnp.where(nxt, p_n, 0.0)
            ds_n = _mx(p_n * (_bdot(vc, donb, 2, 2) - dn) * DIL_SCALE)
            dk_ref[:, rows, :] = _bdot(ds_c, qb, 2, 1) + _bdot(ds_n, qn, 2, 1)
            dv_ref[:, rows, :] = _bdot(_mx(p_c), dob, 2, 1) + _bdot(_mx(p_n), donb, 2, 1)

    cur = lambda w: pl.BlockSpec((hds, pair * blk, w), lambda b: (0, b, 0))
    prev = lambda w: pl.BlockSpec((hds, blk, w), lambda b: (0, jnp.maximum(pair * b - 1, 0), 0))
    nxt_ = lambda w: pl.BlockSpec((hds, blk, w), lambda b: (0, jnp.minimum(pair * (b + 1), nblk - 1), 0))
    rcur = pl.BlockSpec((hds, 1, pair * blk), lambda b: (0, 0, b))
    rnxt = pl.BlockSpec((hds, 1, blk), lambda b: (0, 0, jnp.minimum(pair * (b + 1), nblk - 1)))
    bias = pl.BlockSpec((hds, blk, blk), lambda b: (0, 0, 0))
    out = jax.ShapeDtypeStruct((hds, s, e), F32)
    return pl.pallas_call(
        body, name=name,
        out_shape=(out, out, out),
        grid=(nblk // pair,),
        in_specs=[cur(e), nxt_(e), cur(e), prev(e), cur(e), prev(e), cur(e), nxt_(e),
                  rcur, rnxt, rcur, rnxt, bias, bias],
        out_specs=(cur(e), cur(e), cur(e)),
        compiler_params=_params("parallel"),
    )(q, q, k, k, v, v, do, do, lj, lj, dd, dd, b_c, b_p)


def _ln_fwd(z, g, b):
    mu = jnp.mean(z, axis=-1, keepdims=True)
    zc = z - mu
    var = jnp.mean(zc * zc, axis=-1, keepdims=True)
    rstd = lax.rsqrt(var + LN_EPS)
    xhat = zc * rstd
    return xhat * g + b, xhat, rstd


def _ln_bwd(dy, xhat, rstd, g):
    dxh = dy * g
    return rstd * (dxh - jnp.mean(dxh, axis=-1, keepdims=True) - xhat * jnp.mean(dxh * xhat, axis=-1, keepdims=True))


def _out_ln1(a_mla, a_dil, w_o, x, g, b, *, tm):
    s = x.shape[0]
    half = HEADS * HEAD_DIM

    def body(am_ref, ad_ref, w_ref, x_ref, g_ref, b_ref, x1_ref, xh_ref, r_ref):
        mix = _dot(_mx(am_ref[...]), w_ref[0:half, :]) + _dot(_mx(ad_ref[...]), w_ref[half:2 * half, :])
        z = DN_ALPHA * x_ref[...] + mix
        y, xhat, rstd = _ln_fwd(z, g_ref[...], b_ref[...])
        x1_ref[...] = y
        xh_ref[...] = xhat
        r_ref[...] = rstd

    row = lambda w: pl.BlockSpec((tm, w), lambda i: (i, 0))
    full = lambda shp: pl.BlockSpec(shp, lambda i: (0,) * len(shp))
    act = jax.ShapeDtypeStruct((s, D_MODEL), F32)
    return pl.pallas_call(
        body, name="out_ln1",
        out_shape=(act, act, jax.ShapeDtypeStruct((s, 1), F32)),
        grid=(s // tm,),
        in_specs=[row(half), row(half), full((D_MODEL, D_MODEL)), row(D_MODEL), full((1, D_MODEL)), full((1, D_MODEL))],
        out_specs=(row(D_MODEL), row(D_MODEL), row(1)),
        compiler_params=_params("parallel"),
    )(a_mla, a_dil, w_o, x, g, b)


def _down_ln2_loss(act, w_down, x1, g, b, target, *, tm):
    s = x1.shape[0]

    def body(a_ref, w_ref, x1_ref, g_ref, b_ref, t_ref, dz_ref, loss_ref, dg_ref, db_ref):
        i = pl.program_id(0)

        @pl.when(i == 0)
        def _():
            loss_ref[...] = jnp.zeros_like(loss_ref)
            dg_ref[...] = jnp.zeros_like(dg_ref)
            db_ref[...] = jnp.zeros_like(db_ref)

        gam = g_ref[...]
        z = DN_ALPHA * x1_ref[...] + _dot(a_ref[...], w_ref[...])
        y, xhat, rstd = _ln_fwd(z, gam, b_ref[...])
        err = y - t_ref[...]
        loss_ref[...] += 0.5 * jnp.sum(jnp.mean(err * err, axis=-1, keepdims=True))
        dy = err * (1.0 / D_MODEL)
        dg_ref[...] += jnp.sum(dy * xhat, axis=0, keepdims=True)
        db_ref[...] += jnp.sum(dy, axis=0, keepdims=True)
        dz_ref[...] = _ln_bwd(dy, xhat, rstd, gam)

    row = lambda w: pl.BlockSpec((tm, w), lambda i: (i, 0))
    full = lambda shp: pl.BlockSpec(shp, lambda i: (0,) * len(shp))
    vec = jax.ShapeDtypeStruct((1, D_MODEL), F32)
    return pl.pallas_call(
        body, name="down_ln2_loss",
        out_shape=(jax.ShapeDtypeStruct((s, D_MODEL), F32), jax.ShapeDtypeStruct((1, LANES), F32), vec, vec),
        grid=(s // tm,),
        in_specs=[row(D_FF), full((D_FF, D_MODEL)), row(D_MODEL), full((1, D_MODEL)), full((1, D_MODEL)), row(D_MODEL)],
        out_specs=(row(D_MODEL), full((1, LANES)), full((1, D_MODEL)), full((1, D_MODEL))),
        compiler_params=_params("arbitrary"),
    )(act, w_down, x1, g, b, target)


def _up_bwd_ln1(du_a, du_g, w_up_t, dz2, xhat1, rstd1, g, *, tm):
    s = dz2.shape[0]

    def body(dua_ref, dug_ref, wa_ref, wg_ref, dz2_ref, xh_ref, r_ref, g_ref, dz1_ref, dg_ref, db_ref):
        i = pl.program_id(0)

        @pl.when(i == 0)
        def _():
            dg_ref[...] = jnp.zeros_like(dg_ref)
            db_ref[...] = jnp.zeros_like(db_ref)

        dx1 = DN_ALPHA * dz2_ref[...] + (_dot(dua_ref[...], wa_ref[...]) + _dot(dug_ref[...], wg_ref[...]))
        xhat = xh_ref[...]
        dg_ref[...] += jnp.sum(dx1 * xhat, axis=0, keepdims=True)
        db_ref[...] += jnp.sum(dx1, axis=0, keepdims=True)
        dz1_ref[...] = _ln_bwd(dx1, xhat, r_ref[...], g_ref[...])

    row = lambda w: pl.BlockSpec((tm, w), lambda i: (i, 0))
    full = lambda shp: pl.BlockSpec(shp, lambda i: (0,) * len(shp))
    vec = jax.ShapeDtypeStruct((1, D_MODEL), F32)
    return pl.pallas_call(
        body, name="up_bwd_ln1",
        out_shape=(jax.ShapeDtypeStruct((s, D_MODEL), F32), vec, vec),
        grid=(s // tm,),
        in_specs=[row(D_FF), row(D_FF),
                  pl.BlockSpec((D_FF, D_MODEL), lambda i: (0, 0)), pl.BlockSpec((D_FF, D_MODEL), lambda i: (1, 0)),
                  row(D_MODEL), row(D_MODEL), row(1), full((1, D_MODEL))],
        out_specs=(row(D_MODEL), full((1, D_MODEL)), full((1, D_MODEL))),
        compiler_params=_params("arbitrary"),
    )(du_a, du_g, w_up_t, w_up_t, dz2, xhat1, rstd1, g)


GELU_C = math.sqrt(2.0 / math.pi)


def _gelu(x):
    cdf = 0.5 * (1.0 + jnp.tanh(GELU_C * (x + 0.044715 * (x * x * x))))
    return x * cdf


def _gelu_grad(x):
    t = jnp.tanh(GELU_C * (x + 0.044715 * (x * x * x)))
    return 0.5 * (1.0 + t) + 0.5 * x * (1.0 - t * t) * (GELU_C * (1.0 + 3.0 * 0.044715 * (x * x)))


def _shift_down(u, halo):
    r1, r2 = pltpu.roll(u, 1, 0), pltpu.roll(u, 2, 0)
    row = lax.broadcasted_iota(jnp.int32, (SUBLANES, u.shape[1]), 0)
    h7, h6 = halo[7:8, :], halo[6:7, :]
    head1 = jnp.where(row == 0, h7, r1[:SUBLANES])
    head2 = jnp.where(row == 0, h6, jnp.where(row == 1, h7, r2[:SUBLANES]))
    return (jnp.concatenate([head1, r1[SUBLANES:]], axis=0), jnp.concatenate([head2, r2[SUBLANES:]], axis=0))


def _shift_up(d, nxt):
    t = d.shape[0]
    r1, r2 = pltpu.roll(d, t - 1, 0), pltpu.roll(d, t - 2, 0)
    row = lax.broadcasted_iota(jnp.int32, (SUBLANES, d.shape[1]), 0)
    n0, n1 = nxt[0:1, :], nxt[1:2, :]
    last = t - SUBLANES
    tail1 = jnp.where(row == SUBLANES - 1, n0, r1[last:])
    tail2 = jnp.where(row == SUBLANES - 1, n1, jnp.where(row == SUBLANES - 2, n0, r2[last:]))
    return (jnp.concatenate([r1[:last], tail1], axis=0), jnp.concatenate([r2[:last], tail2], axis=0))


def _conv(u, s1, s2, w, b):
    return ((b + w[0:1, :] * s2) + w[1:2, :] * s1) + w[2:3, :] * u


def _up_gate_fwd(x1, w_up, conv_w, conv_b, *, tm, tn):
    s = x1.shape[0]
    nj = D_FF // tn
    hb = tm // SUBLANES

    def body(x_ref, xh_ref, wua_ref, wug_ref, wa_ref, wg_ref, ba_ref, bg_ref,
             ua_ref, ug_ref, o_ref, a_ref, ge_ref, gd_ref):
        keep = pl.program_id(1) > 0
        xb, xh = _mx(x_ref[...]), _mx(xh_ref[...])
        wua, wug = wua_ref[...], wug_ref[...]
        ua, ug = _dot(xb, wua), _dot(xb, wug)
        ha = jnp.where(keep, _dot(xh, wua), 0.0)
        hg = jnp.where(keep, _dot(xh, wug), 0.0)
        ua_ref[...] = ua
        ug_ref[...] = ug
        a = _conv(ua, *_shift_down(ua, ha), wa_ref[...], ba_ref[...])
        g = _conv(ug, *_shift_down(ug, hg), wg_ref[...], bg_ref[...])
        ge = _gelu(g)
        o_ref[...] = (ge * a).astype(o_ref.dtype)
        a_ref[...] = a
        ge_ref[...] = ge
        gd_ref[...] = _gelu_grad(g)

    main = lambda off: pl.BlockSpec((tm, tn), lambda j, i: (i, j + off))
    wspec = lambda r, off: pl.BlockSpec((r, tn), lambda j, i: (0, j + off))
    if w_up.ndim == 3:
        wu = lambda off: pl.BlockSpec((None, D_MODEL, tn), lambda j, i: (j + off, 0, 0))
    else:
        wu = lambda off: pl.BlockSpec((D_MODEL, tn), lambda j, i: (0, j + off))
    keep_f32 = jax.ShapeDtypeStruct((s, D_FF), F32)
    return pl.pallas_call(
        body, name="up_gate_fwd",
        out_shape=(keep_f32, keep_f32, jax.ShapeDtypeStruct((s, D_FF), MXU_DTYPE), keep_f32, keep_f32, keep_f32),
        grid=(nj, s // tm),
        in_specs=[pl.BlockSpec((tm, D_MODEL), lambda j, i: (i, 0)),
                  pl.BlockSpec((SUBLANES, D_MODEL), lambda j, i: (jnp.maximum(i * hb - 1, 0), 0)),
                  wu(0), wu(nj), wspec(3, 0), wspec(3, nj), wspec(1, 0), wspec(1, nj)],
        out_specs=(main(0),) * 6,
        compiler_params=_params("parallel", "parallel"),
    )(x1, x1, w_up, w_up, conv_w, conv_w, conv_b, conv_b)


def _gate_bwd(u_a, u_g, dz2, w_down_t, a, ge, gd, conv_w, *, tm, tn):
    s = u_a.shape[0]
    nj = D_FF // tn
    ni = s // tm
    hb = tm // SUBLANES

    def body(ua_ref, ug_ref, ha_ref, hg_ref, dz_ref, dzn_ref, wd_ref, a_ref, an_ref, ge_ref, gen_ref, gd_ref, gdn_ref,
             wa_ref, wg_ref, dua_ref, dug_ref, dwa_ref, dwg_ref, dba_ref, dbg_ref):
        i = pl.program_id(1)

        @pl.when(i == 0)
        def _():
            for r in (dwa_ref, dwg_ref, dba_ref, dbg_ref):
                r[...] = jnp.zeros_like(r)

        wa, wg = wa_ref[...], wg_ref[...]
        ua, ug = ua_ref[...], ug_ref[...]
        ha = jnp.where(i > 0, ha_ref[...], 0.0)
        hg = jnp.where(i > 0, hg_ref[...], 0.0)
        sa1, sa2 = _shift_down(ua, ha)
        sg1, sg2 = _shift_down(ug, hg)
        wd = wd_ref[...]
        d = _dot(_mx(dz_ref[...]), wd)
        dya = d * ge_ref[...]
        dyg = d * a_ref[...] * gd_ref[...]
        dn = jnp.where(i < ni - 1, _dot(_mx(dzn_ref[...]), wd), 0.0)
        dya_n = dn * gen_ref[...]
        dyg_n = dn * an_ref[...] * gdn_ref[...]
        da1, da2 = _shift_up(dya, dya_n)
        dg1, dg2 = _shift_up(dyg, dyg_n)
        dua_ref[...] = (wa[2:3, :] * dya + wa[1:2, :] * da1 + wa[0:1, :] * da2).astype(dua_ref.dtype)
        dug_ref[...] = (wg[2:3, :] * dyg + wg[1:2, :] * dg1 + wg[0:1, :] * dg2).astype(dug_ref.dtype)
        ssum = lambda v: jnp.sum(v, axis=0, keepdims=True)
        dwa_ref[...] += jnp.concatenate([ssum(dya * sa2), ssum(dya * sa1), ssum(dya * ua)], axis=0)
        dwg_ref[...] += jnp.concatenate([ssum(dyg * sg2), ssum(dyg * sg1), ssum(dyg * ug)], axis=0)
        dba_ref[...] += ssum(dya)
        dbg_ref[...] += ssum(dyg)

    main = pl.BlockSpec((tm, tn), lambda j, i: (i, j))
    halo = pl.BlockSpec((SUBLANES, tn), lambda j, i: (jnp.maximum(i * hb - 1, 0), j))
    next_row = lambda j, i: jnp.minimum((i + 1) * hb, s // SUBLANES - 1)
    nxt = pl.BlockSpec((SUBLANES, tn), lambda j, i: (next_row(j, i), j))
    wspec = lambda r, off: pl.BlockSpec((r, tn), lambda j, i: (0, j + off))
    return pl.pallas_call(
        body, name="gate_bwd",
        out_shape=(jax.ShapeDtypeStruct((s, D_FF), MXU_DTYPE), jax.ShapeDtypeStruct((s, D_FF), MXU_DTYPE),
                   jax.ShapeDtypeStruct((3, D_FF), F32), jax.ShapeDtypeStruct((3, D_FF), F32),
                   jax.ShapeDtypeStruct((1, D_FF), F32), jax.ShapeDtypeStruct((1, D_FF), F32)),
        grid=(nj, ni),
        in_specs=[main, main, halo, halo,
                  pl.BlockSpec((tm, D_MODEL), lambda j, i: (i, 0)),
                  pl.BlockSpec((SUBLANES, D_MODEL), lambda j, i: (next_row(j, i), 0)),
                  pl.BlockSpec((D_MODEL, tn), lambda j, i: (0, j))]
        + [main, nxt] * 3 + [wspec(3, 0), wspec(3, nj)],
        out_specs=(main, main, wspec(3, 0), wspec(3, 0), wspec(1, 0), wspec(1, 0)),
        compiler_params=_params("parallel", "arbitrary"),
    )(u_a, u_g, u_a, u_g, dz2, dz2, w_down_t, a, a, ge, ge, gd, gd, conv_w, conv_w)


def _prep_weights(w_in, w_uq, w_uk, w_uv, w_o, w_up, w_down):
    return {**_prep_weights_first(w_in, w_uq, w_uk, w_uv), **_prep_weights_late(w_o, w_up, w_down)}


def _prep_weights_late(w_o, w_up, w_down):
    w_o, w_up, w_down = _mx(w_o), _mx(w_up), _mx(w_down)
    w_up_t = w_up.T if w_up.ndim == 2 else w_up.transpose(0, 2, 1).reshape(2 * D_FF, D_MODEL)
    return dict(w_o=w_o, w_o_t=w_o.T, w_up=w_up, w_up_t=w_up_t, w_down=w_down, w_down_t=w_down.T)


def _prep_weights_first(w_in, w_uq, w_uk, w_uv):
    c = lambda a: a.astype(MXU_DTYPE)
    w_in = c(w_in)
    z = lambda w: jnp.zeros((D_MODEL, w), MXU_DTYPE)
    r0 = Q_RANK + KV_RANK
    w_in_ext = jnp.concatenate([w_in[:, :r0], z(NOPE), w_in[:, r0:r0 + ROPE], z(32), w_in[:, r0 + ROPE:]], axis=1)
    wq = jnp.pad(c(w_uq).transpose(1, 0, 2), ((0, 0), (0, 0), (0, QK_PAD - NOPE - ROPE)))
    wk = jnp.pad(c(w_uk).transpose(1, 0, 2), ((0, 0), (0, 0), (0, QK_PAD - NOPE)))
    wv = c(w_uv).transpose(1, 0, 2)
    t3 = lambda a: a.transpose(0, 2, 1)
    return dict(w_in=w_in_ext, w_in_t=w_in_ext.T, wq=wq, wq_t=t3(wq), wk=wk, wk_t=t3(wk), wv=wv, wv_t=t3(wv))


def _local_step(x, target, w, g_cq, g_ckv, ln1_g, ln1_b, conv_w, conv_b, ln2_g, ln2_b, comm=None):
    s = x.shape[0]
    tabs = _rope_tables(s)
    r2 = lambda a: a.reshape(1, -1)
    cb = r2(conv_b)
    dils = [d for _, d in DIL_PAIRS]

    h, qp, kp, vp = _in_proj(x, w["w_in"], tm=256)
    q, k, v, v_t = _mla_prep_fwd(h, r2(g_cq), r2(g_ckv), w["wq"], w["wk"], w["wv"], w["wv_t"], tabs, tm=256)
    if comm is None:
        o_mla_t, lse_mla = _mla_attn_fwd(q, k, v_t, t=512, g=HEADS)
    else:
        o_mla_t, lse_mla, *gathered = _mla_attn_fwd(q, k, v_t, t=512, g=HEADS, late=comm["late"])
        w = {**w, **comm["finish"](gathered)}
    o_bs, lse_bs = [], []
    for i, d in enumerate(dils):
        o_b, l_b = _dil_fwd(qp[i], kp[i], vp[i], d, name=f"dil_fwd_{d}")
        o_bs.append(o_b)
        lse_bs.append(_unperm_row(l_b, d))
    o_dil, lj = _dil_combine(o_bs, lse_bs, ts=512)
    o_mla = o_mla_t.transpose(2, 0, 1).reshape(s, HEADS * HEAD_DIM)
    x1, xhat1, rstd1 = _out_ln1(o_mla, o_dil, w["w_o"], x, r2(ln1_g), r2(ln1_b), tm=256)
    u_a, u_g, act, conv_a, gelu_g, gelu_dg = _up_gate_fwd(x1, w["w_up"], conv_w, cb, tm=256, tn=1408)
    dz2, loss, dg2, db2 = _down_ln2_loss(act, w["w_down"], x1, r2(ln2_g), r2(ln2_b), target, tm=256)

    dw_down = _mm_tn(act, dz2, name="dw_down", tm=1408, tn=D_MODEL, ts=DW_TOKENS)
    du_a, du_g, dcw_a, dcw_g, dcb_a, dcb_g = _gate_bwd(u_a, u_g, dz2, w["w_down_t"], conv_a, gelu_g, gelu_dg, conv_w,
                                                       tm=256, tn=1408)
    dz1, dg1, db1 = _up_bwd_ln1(du_a, du_g, w["w_up_t"], dz2, xhat1, rstd1, r2(ln1_g), tm=256)
    dw_up = jnp.concatenate([_mm_tn(x1, du_a, name="dw_up_a", tm=D_MODEL, tn=1408, ts=DW_TOKENS),
                             _mm_tn(x1, du_g, name="dw_up_g", tm=D_MODEL, tn=1408, ts=DW_TOKENS)], axis=1)
    named_early = [("w_up", dw_up), ("w_down", dw_down)]
    swap = () if comm is None else comm["blocked"](named_early)
    do_mla, do_dil, dd_all, received = _attn_bwd_heads(dz1, w["w_o_t"], o_mla, o_dil, tm=256, swap=swap)
    dw_o = jnp.concatenate([_mm_tn(o_mla, dz1, name="dw_o_mla", tm=512, tn=D_MODEL, ts=DW_TOKENS),
                            _mm_tn(o_dil, dz1, name="dw_o_dil", tm=512, tn=D_MODEL, ts=DW_TOKENS)], axis=0)
    dd_all = dd_all.T
    dd_mla, dd_dil = dd_all[:HEADS].reshape(HEADS, 1, s), dd_all[HEADS:].reshape(HEADS, 1, s)
    early = () if comm is None else tuple(comm["add_halves"](named_early, swap, received))
    dq, dk, dv, *early_slots = _mla_attn_bwd(q, k, v, do_mla, lse_mla, dd_mla, t=512, g=4, early=early)
    parts = []
    for i, d in enumerate(dils):
        parts.append(_dil_bwd(qp[i], kp[i], vp[i], do_dil[i], _perm_row(lj, d), _perm_row(dd_dil, d), d, name=f"dil_bwd_{d}"))
    dh_dil = _dil_merge(parts, ts=512)
    dh_mla, dwq, dwk, dwv, dgq, dgkv = _mla_prep_bwd(h, dq, dk, dv, r2(g_cq), r2(g_ckv),
                                                     w["wq_t"], w["wk_t"], w["wv_t"], tabs, tm=256)
    mla_w = 4 * LANES
    w_in_t = w["w_in_t"]
    grad_x = _mm_nn(dh_mla, w_in_t[:mla_w], name="in_bwd_mla", tm=512, tn=D_MODEL, tk=mla_w, add=dz1, add_scale=DN_ALPHA)
    grad_x = _mm_nn(dh_dil, w_in_t[mla_w:], name="in_bwd_dil", tm=512, tn=D_MODEL, tk=3 * HEADS * HEAD_DIM, add=grad_x)
    dw_mla = _mm_tn(x, dh_mla, name="dw_in_mla", tm=D_MODEL, tn=mla_w, ts=DW_TOKENS)
    dw_dil = _mm_tn(x, dh_dil, name="dw_in_dil", tm=D_MODEL, tn=3 * HEADS * HEAD_DIM, ts=DW_TOKENS)
    r0 = Q_RANK + KV_RANK
    grads = dict(
        w_in=jnp.concatenate([dw_mla[:, :r0], dw_mla[:, r0 + NOPE:r0 + NOPE + ROPE], dw_dil], axis=1),
        g_cq=dgq[0], g_ckv=dgkv[0],
        w_uq=dwq[:, :, :NOPE + ROPE].transpose(1, 0, 2),
        w_uk=dwk[:, :, :NOPE].transpose(1, 0, 2),
        w_uv=dwv.transpose(1, 0, 2),
        w_o=dw_o, ln1_g=dg1[0], ln1_b=db1[0], w_up=dw_up,
        conv_w=jnp.concatenate([dcw_a, dcw_g], axis=1), conv_b=jnp.concatenate([dcb_a, dcb_g], axis=1)[0],
        w_down=dw_down, ln2_g=dg2[0], ln2_b=db2[0])
    if comm is not None:
        grads["early"] = (early, tuple(early_slots))
    return loss[0, 0], grad_x, grads


N_CHIPS = 4
SHARDED = ("w_in", "w_uq", "w_o", "w_up", "conv_w", "w_down")
COL_SHARDED = ("w_in", "w_up", "conv_w")
SHARD_SHAPE = dict(w_in=(D_MODEL, IN_WIDTH // 4), w_uq=(Q_RANK // 4, HEADS, NOPE + ROPE), w_o=(D_MODEL // 4, D_MODEL),
                   w_up=(D_MODEL, 2 * D_FF // 4), conv_w=(3, 2 * D_FF // 4), w_down=(D_FF // 4, D_MODEL))
SMALL = ("g_cq", "g_ckv", "w_uk", "w_uv", "ln1_g", "ln1_b", "conv_b", "ln2_g", "ln2_b")
SMALL_SHAPE = dict(g_cq=(Q_RANK,), g_ckv=(KV_RANK,), w_uk=(KV_RANK, HEADS, NOPE), w_uv=(KV_RANK, HEADS, HEAD_DIM),
                   ln1_g=(D_MODEL,), ln1_b=(D_MODEL,), conv_b=(2 * D_FF,), ln2_g=(D_MODEL,), ln2_b=(D_MODEL,))
BIG = ("w_in", "w_uq", "w_o", "w_up", "w_down")
BIG_2D = dict(w_in=(D_MODEL, IN_WIDTH // 4), w_uq=(Q_RANK // 4, HEADS * (NOPE + ROPE)), w_o=(D_MODEL // 4, D_MODEL),
              w_up=(D_MODEL, 2 * D_FF // 4), w_down=(D_FF // 4, D_MODEL))
SMALL_G = SMALL + ("conv_w",)
SMALL_WIDE = ("w_uk", "w_uv")
SMALL_G_SHAPE = {**SMALL_SHAPE, "conv_w": (3, 2 * D_FF)}
SMALL_U_SHAPE = {**SMALL_SHAPE, "conv_w": (3, 2 * D_FF // 4)}


def _size(shape):
    return math.prod(shape)


def _padded_rows(n_elems, mult):
    return -(-n_elems // (LANES * mult)) * mult


SHARD_ROWS = {n: _padded_rows(_size(SHARD_SHAPE[n]), SUBLANES) for n in SHARDED}
R_SMALL = -(-sum(_size(SMALL_G_SHAPE[n]) for n in SMALL_G) // (LANES * LANES)) * LANES
GATHER_FIRST = ("w_in", "w_uq")
GATHER_LATE = ("w_o", "w_up", "w_down")
REDUCED_EARLY = ("w_up", "w_down")
REDUCED_LAST = ("w_in", "w_uq", "w_o")


def _rows(a, rows=None):
    flat = a.reshape(-1)
    rows = -(-flat.shape[0] // LANES) if rows is None else rows
    return jnp.pad(flat, (0, rows * LANES - flat.shape[0])).reshape(rows, LANES)


def _blocked(name, g):
    r, c = BIG_2D[name]
    a = g.reshape(r, N_CHIPS, c).transpose(1, 0, 2) if name in COL_SHARDED else g.reshape(N_CHIPS, r, c)
    return a.reshape(N_CHIPS, 2, r // 2, c)


def _pack_flat(t, names, rows=None):
    flat = jnp.concatenate([t[n].astype(F32).reshape(-1) for n in names])
    return _rows(flat, R_SMALL if rows is None else rows)


def _unpack_flat(buf, names, shapes):
    flat, out, r = buf.reshape(-1), {}, 0
    for n in names:
        out[n] = flat[r:r + _size(shapes[n])].reshape(shapes[n])
        r += _size(shapes[n])
    return out


def _from_chip_blocks(name, blocks):
    shp = SHARD_SHAPE[name]
    a = blocks.reshape(N_CHIPS, -1)[:, :_size(shp)].reshape((N_CHIPS,) + shp)
    if name in COL_SHARDED:
        return a.transpose(1, 0, 2).reshape(shp[0], N_CHIPS * shp[1])
    return a.reshape((N_CHIPS * shp[0],) + shp[1:])


ANY = pl.BlockSpec(memory_space=pl.ANY)
COMM_PARAMS = pltpu.CompilerParams(has_side_effects=True)


def _coords():
    return lax.axis_index("x"), lax.axis_index("y"), lax.axis_index("c")


def _other_chips(x, y):
    return [(1 - x, y), (x, 1 - y), (1 - x, 1 - y)]


def _remote(src, dst, send_sems, recv_sems, k, to):
    return pltpu.make_async_remote_copy(src_ref=src, dst_ref=dst, send_sem=send_sems.at[k], recv_sem=recv_sems.at[k],
                                        device_id=to, device_id_type=MESH)


def _gather_in_steps(wp_refs, wout_refs, send_sems, recv_sems, *, first, mid, last):
    x, y, c = _coords()
    me = 2 * x + y
    sib = (x, y, 1 - c)
    chips = _other_chips(x, y)
    n = len(wp_refs)
    pairs = [(j, t, px, py) for j, (px, py) in enumerate(chips) for t in range(n)]
    ici = [_remote(wp_refs[t].at[c], wout_refs[t].at[me, c], send_sems, recv_sems, j * n + t, (px, py, c))
           for j, t, px, py in pairs]
    fwd = [_remote(wout_refs[t].at[2 * px + py, c], wout_refs[t].at[2 * px + py, c], send_sems, recv_sems, (3 + j) * n + t, sib)
           for j, t, px, py in pairs]

    @pl.when(first)
    def _():
        for cp in ici:
            cp.start()

    @pl.when(mid)
    def _():
        for i, (j, t, px, py) in enumerate(pairs):
            _remote(wp_refs[t].at[c], wout_refs[t].at[2 * px + py, c], send_sems, recv_sems, j * n + t, (px, py, c)).wait_recv()
            fwd[i].start()

    @pl.when(last)
    def _():
        for j, t, px, py in pairs:
            k = 2 * px + py
            _remote(wout_refs[t].at[k, 1 - c], wout_refs[t].at[k, 1 - c], send_sems, recv_sems, (3 + j) * n + t, sib).wait_recv()
        for cp in ici + fwd:
            cp.wait_send()


def _swap_halves_in_steps(gs_refs, os_refs, send_sems, recv_sems, *, first, last):
    x, y, c = _coords()
    sib = (x, y, 1 - c)
    cps = [_remote(gs_refs[t].at[k, 1 - c], os_refs[t].at[k], send_sems, recv_sems, t * N_CHIPS + k, sib)
           for t in range(len(gs_refs)) for k in range(N_CHIPS)]

    @pl.when(first)
    def _():
        for cp in cps:
            cp.start()

    @pl.when(last)
    def _():
        for cp in cps:
            cp.wait_recv()
        for cp in cps:
            cp.wait_send()


def _exchange_in_steps(ps_refs, ss_refs, send_sems, recv_sems, *, first, last):
    x, y, c = _coords()
    me = 2 * x + y
    chips = _other_chips(x, y)
    n = len(ps_refs)
    sends = [_remote(ps_refs[t].at[2 * px + py], ss_refs[t].at[me], send_sems, recv_sems, j * n + t, (px, py, c))
             for j, (px, py) in enumerate(chips) for t in range(n)]

    @pl.when(first)
    def _():
        for cp in sends:
            cp.start()

    @pl.when(last)
    def _():
        for j, (px, py) in enumerate(chips):
            for t in range(n):
                _remote(ps_refs[t].at[me], ss_refs[t].at[2 * px + py], send_sems, recv_sems, j * n + t, (px, py, c)).wait_recv()
        for cp in sends:
            cp.wait_send()


def _gather_weights(wp, cwp):
    def body(wp_ref, cw_ref, wout_ref, cwout_ref, send_sems, recv_sems):
        x, y, c = _coords()
        me = 2 * x + y
        sib = (x, y, 1 - c)
        chips = _other_chips(x, y)
        sends = [_remote(wp_ref.at[c], wout_ref.at[me, c], send_sems, recv_sems, j, (px, py, c))
                 for j, (px, py) in enumerate(chips)]
        sends += [_remote(cw_ref, cwout_ref.at[me], send_sems, recv_sems, 3 + j, (px, py, c))
                  for j, (px, py) in enumerate(chips)]
        for cp in sends:
            cp.start()
        for j, (px, py) in enumerate(chips):
            k = 2 * px + py
            _remote(wp_ref.at[c], wout_ref.at[k, c], send_sems, recv_sems, j, (px, py, c)).wait_recv()
            fwd = _remote(wout_ref.at[k, c], wout_ref.at[k, c], send_sems, recv_sems, 6 + j, sib)
            fwd.start()
            sends.append(fwd)
        for j, (px, py) in enumerate(chips):
            k = 2 * px + py
            _remote(cw_ref, cwout_ref.at[k], send_sems, recv_sems, 3 + j, (px, py, c)).wait_recv()
            _remote(wout_ref.at[k, 1 - c], wout_ref.at[k, 1 - c], send_sems, recv_sems, 6 + j, sib).wait_recv()
        for cp in sends:
            cp.wait_send()

    return pl.pallas_call(
        body, name="gather_weights",
        out_shape=(jax.ShapeDtypeStruct((N_CHIPS,) + wp.shape, wp.dtype), jax.ShapeDtypeStruct((N_CHIPS,) + cwp.shape, cwp.dtype)),
        in_specs=[ANY, ANY], out_specs=(ANY, ANY),
        scratch_shapes=[pltpu.SemaphoreType.DMA((9,)), pltpu.SemaphoreType.DMA((9,))],
        compiler_params=COMM_PARAMS,
    )(wp, cwp)


def _exchange_sibling_halves(gs, whole, *, name):
    n, nw = len(gs), len(whole)

    def body(*refs):
        gs_refs, wh_refs = refs[:n], refs[n:n + nw]
        os_refs, ow_refs = refs[n + nw:2 * n + nw], refs[2 * n + nw:2 * (n + nw)]
        send_sems, recv_sems = refs[2 * (n + nw):]
        x, y, c = _coords()
        sib = (x, y, 1 - c)
        cps = [_remote(gs_refs[t].at[k, 1 - c], os_refs[t].at[k], send_sems, recv_sems, t * N_CHIPS + k, sib)
               for t in range(n) for k in range(N_CHIPS)]
        cps += [_remote(wh_refs[t], ow_refs[t], send_sems, recv_sems, n * N_CHIPS + t, sib) for t in range(nw)]
        for cp in cps:
            cp.start()
        for cp in cps:
            cp.wait_recv()
        for cp in cps:
            cp.wait_send()

    n_sem = n * N_CHIPS + nw
    return pl.pallas_call(
        body, name=name,
        out_shape=tuple(jax.ShapeDtypeStruct((N_CHIPS,) + a.shape[2:], F32) for a in gs)
        + tuple(jax.ShapeDtypeStruct(a.shape, F32) for a in whole),
        in_specs=[ANY] * (n + nw), out_specs=(ANY,) * (n + nw),
        scratch_shapes=[pltpu.SemaphoreType.DMA((n_sem,)), pltpu.SemaphoreType.DMA((n_sem,))],
        compiler_params=COMM_PARAMS,
    )(*gs, *whole)


def _exchange_chips(ps, whole):
    n, nw = len(ps), len(whole)
    per_chip = n + nw

    def body(*refs):
        ps_refs, wh_refs = refs[:n], refs[n:per_chip]
        ss_refs, sw_refs = refs[per_chip:per_chip + n], refs[per_chip + n:2 * per_chip]
        send_sems, recv_sems = refs[2 * per_chip:]
        x, y, c = _coords()
        me = 2 * x + y
        chips = _other_chips(x, y)
        sends = []
        for j, (px, py) in enumerate(chips):
            to = (px, py, c)
            for t in range(n):
                sends.append(_remote(ps_refs[t].at[2 * px + py], ss_refs[t].at[me], send_sems, recv_sems, j * per_chip + t, to))
            for t in range(nw):
                sends.append(_remote(wh_refs[t], sw_refs[t].at[me], send_sems, recv_sems, j * per_chip + n + t, to))
        for cp in sends:
            cp.start()
        for j, (px, py) in enumerate(chips):
            k, to = 2 * px + py, (px, py, c)
            for t in range(n):
                _remote(ps_refs[t].at[me], ss_refs[t].at[k], send_sems, recv_sems, j * per_chip + t, to).wait_recv()
            for t in range(nw):
                _remote(wh_refs[t], sw_refs[t].at[k], send_sems, recv_sems, j * per_chip + n + t, to).wait_recv()
        for cp in sends:
            cp.wait_send()

    n_sem = 3 * per_chip
    return pl.pallas_call(
        body, name="exchange_chips",
        out_shape=tuple(jax.ShapeDtypeStruct(a.shape, a.dtype) for a in ps)
        + tuple(jax.ShapeDtypeStruct((N_CHIPS,) + a.shape, a.dtype) for a in whole),
        in_specs=[ANY] * per_chip, out_specs=(ANY,) * per_chip,
        scratch_shapes=[pltpu.SemaphoreType.DMA((n_sem,)), pltpu.SemaphoreType.DMA((n_sem,))],
        compiler_params=COMM_PARAMS,
    )(*ps, *whole)


def _exchange_sibling_result(gh):
    n = len(gh)

    def body(*refs):
        gh_refs, out_refs, (send_sems, recv_sems) = refs[:n], refs[n:2 * n], refs[2 * n:]
        x, y, c = _coords()
        cps = [_remote(gh_refs[t], out_refs[t], send_sems, recv_sems, t, (x, y, 1 - c)) for t in range(n)]
        for cp in cps:
            cp.start()
        for cp in cps:
            cp.wait_recv()
        for cp in cps:
            cp.wait_send()

    return pl.pallas_call(
        body, name="exchange_sibling_result",
        out_shape=tuple(jax.ShapeDtypeStruct(a.shape, F32) for a in gh),
        in_specs=[ANY] * n, out_specs=(ANY,) * n,
        scratch_shapes=[pltpu.SemaphoreType.DMA((n,)), pltpu.SemaphoreType.DMA((n,))],
        compiler_params=COMM_PARAMS,
    )(*gh)


def _add_own_half(gs, recv, c_arr, *, name):
    _, rows, cols = recv.shape

    def body(c_ref, a_ref, b_ref, o_ref):
        o_ref[0] = (a_ref[0, 0] + b_ref[0]).astype(o_ref.dtype)

    return pl.pallas_call(
        body, name=name,
        out_shape=jax.ShapeDtypeStruct(recv.shape, GRAD_WIRE_DTYPE),
        grid_spec=pltpu.PrefetchScalarGridSpec(
            num_scalar_prefetch=1, grid=(N_CHIPS,),
            in_specs=[pl.BlockSpec((1, 1, rows, cols), lambda k, c_ref: (k, c_ref[0], 0, 0)),
                      pl.BlockSpec((1, rows, cols), lambda k, c_ref: (k, 0, 0))],
            out_specs=pl.BlockSpec((1, rows, cols), lambda k, c_ref: (k, 0, 0))),
        compiler_params=_params("parallel"),
    )(c_arr, gs, recv)


def _add2(a, b, *, name, out_dtype=F32):
    def body(a_ref, b_ref, o_ref):
        o_ref[...] = (a_ref[...] + b_ref[...]).astype(o_ref.dtype)

    return pl.pallas_call(body, name=name, out_shape=jax.ShapeDtypeStruct(a.shape, out_dtype))(a, b)


def _sum_slots(slots, *, tr, name):
    _, r, c = slots.shape

    def body(s_ref, o_ref):
        f = lambda k: s_ref[k].astype(F32)
        o_ref[...] = ((f(0) + f(1)) + f(2)) + f(3)

    return pl.pallas_call(
        body, name=name,
        out_shape=jax.ShapeDtypeStruct((r, c), F32),
        grid=(r // tr,),
        in_specs=[pl.BlockSpec((N_CHIPS, tr, c), lambda i: (0, i, 0))],
        out_specs=pl.BlockSpec((tr, c), lambda i: (i, 0)),
        compiler_params=_params("parallel"),
    )(slots)


def _adamw(w, g, m, v, *, tr, name):
    r, cols = w.shape

    def body(w_ref, g_ref, m_ref, v_ref, d_ref, nm_ref, nv_ref):
        g_ = g_ref[...]
        m_ = ADAM_B1 * m_ref[...] + (1.0 - ADAM_B1) * g_
        v_ = ADAM_B2 * v_ref[...] + (1.0 - ADAM_B2) * (g_ * g_)
        m_hat = m_ / (1.0 - ADAM_B1 ** ADAM_STEP)
        v_hat = v_ / (1.0 - ADAM_B2 ** ADAM_STEP)
        d_ref[...] = -ADAM_LR * (m_hat / (jnp.sqrt(v_hat) + ADAM_EPS) + ADAM_WD * w_ref[...])
        nm_ref[...] = m_
        nv_ref[...] = v_

    spec = pl.BlockSpec((tr, cols), lambda i: (i, 0))
    out = jax.ShapeDtypeStruct((r, cols), F32)
    return pl.pallas_call(
        body, name=name, out_shape=(out, out, out), grid=(r // tr,),
        in_specs=[spec] * 4, out_specs=(spec,) * 3,
        compiler_params=_params("parallel"),
    )(w, g, m, v)


WEIGHTS = ("w_in", "g_cq", "g_ckv", "w_uq", "w_uk", "w_uv", "w_o", "ln1_g", "ln1_b", "w_up", "conv_w", "conv_b",
           "w_down", "ln2_g", "ln2_b")


def kernel(x, w_in, g_cq, g_ckv, w_uq, w_uk, w_uv, w_o, ln1_g, ln1_b, w_up, conv_w, conv_b, w_down, ln2_g, ln2_b, loss_target, m_w_in, m_g_cq, m_g_ckv, m_w_uq, m_w_uk, m_w_uv, m_w_o, m_ln1_g, m_ln1_b, m_w_up, m_conv_w, m_conv_b, m_w_down, m_ln2_g, m_ln2_b, v_w_in, v_g_cq, v_g_ckv, v_w_uq, v_w_uk, v_w_uv, v_w_o, v_ln1_g, v_ln1_b, v_w_up, v_conv_w, v_conv_b, v_w_down, v_ln2_g, v_ln2_b):
    wts = dict(zip(WEIGHTS, (w_in, g_cq, g_ckv, w_uq, w_uk, w_uv, w_o, ln1_g, ln1_b, w_up, conv_w, conv_b, w_down, ln2_g, ln2_b)))
    mom = dict(zip(WEIGHTS, (m_w_in, m_g_cq, m_g_ckv, m_w_uq, m_w_uk, m_w_uv, m_w_o, m_ln1_g, m_ln1_b, m_w_up, m_conv_w, m_conv_b, m_w_down, m_ln2_g, m_ln2_b)))
    var = dict(zip(WEIGHTS, (v_w_in, v_g_cq, v_g_ckv, v_w_uq, v_w_uk, v_w_uv, v_w_o, v_ln1_g, v_ln1_b, v_w_up, v_conv_w, v_conv_b, v_w_down, v_ln2_g, v_ln2_b)))

    me = 2 * lax.axis_index("x") + lax.axis_index("y")
    my_c = lax.axis_index("c")
    c_arr = my_c.astype(jnp.int32).reshape(1)
    own = lambda slots, mine: lax.dynamic_update_index_in_dim(slots, mine, me, 0)

    def pack(names):
        return jnp.concatenate([_rows(_mx(wts[n]), SHARD_ROWS[n]) for n in names], axis=0).reshape(2, -1, LANES)

    def unpack(names, gathered, mine):
        buf, full, r = own(gathered, mine).reshape(N_CHIPS, -1, LANES), {}, 0
        for n in names:
            full[n] = _from_chip_blocks(n, buf[:, r:r + SHARD_ROWS[n]])
            r += SHARD_ROWS[n]
        return full

    wp_first = pack(GATHER_FIRST)
    cwp = _rows(conv_w, SHARD_ROWS["conv_w"])
    gathered, cwfull = _gather_weights(wp_first, cwp)
    full = unpack(GATHER_FIRST, gathered, wp_first)
    conv_w_full = _from_chip_blocks("conv_w", own(cwfull, cwp))
    w = _prep_weights_first(full["w_in"], full["w_uq"], w_uk, w_uv)
    late_halves = [_mx(wts[n]).reshape(2, BIG_2D[n][0] // 2, BIG_2D[n][1]) for n in GATHER_LATE]

    def finish(gathered_late):
        w_o_b, w_up_b, w_down_b = (own(a, mine).reshape((N_CHIPS,) + BIG_2D[n])
                                   for a, mine, n in zip(gathered_late, late_halves, GATHER_LATE))
        return _prep_weights_late(w_o_b.reshape(D_MODEL, D_MODEL), w_up_b, w_down_b.reshape(D_FF, D_MODEL))

    def blocked(named):
        return [_blocked(n, a) for n, a in named]

    def add_halves(named, gb, recv):
        return [_add_own_half(gb[i], recv[i], c_arr, name=f"add_half_{n}") for i, (n, _) in enumerate(named)]

    def halve(named, whole, wire):
        gb = blocked(named)
        recv = _exchange_sibling_halves(gb, list(whole), name="exchange_sibling_halves")
        return add_halves(named, gb, recv) + [_add2(a, recv[len(gb) + i], name=f"add_whole_{i}", out_dtype=wire[i])
                                              for i, a in enumerate(whole)]

    comm = dict(late=late_halves, finish=finish, blocked=blocked, add_halves=add_halves)
    loss, grad_x, g = _local_step(x[0], loss_target[0], w, g_cq, g_ckv, ln1_g, ln1_b, conv_w_full, conv_b, ln2_g, ln2_b, comm=comm)

    ps_early, slots_early = g.pop("early")
    g["loss"] = loss.reshape(1)
    narrow = tuple(n for n in SMALL_G if n not in SMALL_WIDE) + ("loss",)
    narrow_shape = {**SMALL_G_SHAPE, "loss": (1,)}
    r_narrow = _padded_rows(sum(_size(narrow_shape[n]) for n in narrow), SUBLANES)
    r_wide = _padded_rows(sum(_size(SMALL_G_SHAPE[n]) for n in SMALL_WIDE), 2 * SUBLANES)
    *ps_rest, pr, pw = halve([(n, g[n]) for n in REDUCED_LAST],
                             whole=[_pack_flat(g, narrow, r_narrow), _pack_flat(g, SMALL_WIDE, r_wide)],
                             wire=[F32, GRAD_WIRE_DTYPE])
    *slots_rest, slots_r, slots_w = _exchange_chips(ps_rest, [pr, pw])
    ps = {**dict(zip(REDUCED_LAST, ps_rest)), **dict(zip(REDUCED_EARLY, ps_early))}
    slots = {**dict(zip(REDUCED_LAST, slots_rest)), **dict(zip(REDUCED_EARLY, slots_early))}
    slots = [own(slots[n], lax.dynamic_index_in_dim(ps[n], me, 0, keepdims=False)) for n in BIG]
    g_half = [_sum_slots(slots[i], tr=slots[i].shape[1] // 2, name=f"sum_chips_{n}") for i, n in enumerate(BIG)]
    g_small = {**_unpack_flat(_sum_slots(own(slots_r, pr), tr=r_narrow, name="sum_chips_narrow"), narrow, narrow_shape),
               **_unpack_flat(_sum_slots(own(slots_w, pw), tr=r_wide, name="sum_chips_wide"), SMALL_WIDE, SMALL_G_SHAPE)}
    loss = g_small.pop("loss")[0]
    g_other = _exchange_sibling_result(g_half)
    grads = {n: jnp.where(my_c == 0, jnp.concatenate([g_half[i], g_other[i]]), jnp.concatenate([g_other[i], g_half[i]]))
             for i, n in enumerate(BIG)}
    g_small["conv_w"] = lax.dynamic_slice_in_dim(g_small["conv_w"], me * SHARD_SHAPE["conv_w"][1], SHARD_SHAPE["conv_w"][1], 1)
    grads.update(g_small)

    res = {}
    for n in BIG:
        as2d = lambda a: a.reshape(BIG_2D[n])
        d, m, v = _adamw(as2d(wts[n]), grads[n], as2d(mom[n]), as2d(var[n]), tr=BIG_2D[n][0] // 4, name=f"adamw_{n}")
        res[n] = [a.reshape(SHARD_SHAPE[n]) for a in (grads[n], d, m, v)]
    flat = lambda t: _pack_flat(t, SMALL_G)
    dmv = _adamw(flat(wts), flat(g_small), flat(mom), flat(var), tr=R_SMALL, name="adamw_small")
    dmv = [_unpack_flat(a, SMALL_G, SMALL_U_SHAPE) for a in dmv]
    for n in SMALL_G:
        res[n] = [g_small[n]] + [t[n] for t in dmv]
    outs = [res[n][j] for j in range(4) for n in WEIGHTS]
    return (loss, grad_x[None], *outs)
```

```python
import math

import jax
import jax.numpy as jnp
from jax import lax
from jax.experimental import pallas as pl
from jax.experimental.pallas import tpu as pltpu

F32 = jnp.float32
MXU_DTYPE = jnp.bfloat16
GRAD_WIRE_DTYPE = jnp.bfloat16
NEG = -1e30

D_MODEL = 1024
HEADS = 8
HEAD_DIM = 64
Q_RANK = 256
KV_RANK = 128
NOPE = 64
ROPE = 32
QK_PAD = 128
IN_WIDTH = 1952
IN_EXT = 2048
D_FF = 2816
DIL_PAIRS = ((128, 1), (512, 4), (2048, 16))
DIL_BLOCK = 128
ROPE_THETA = 10000.0
DN_ALPHA = 2.0 ** 0.25
LN_EPS = 1e-5
RMS_EPS = 1e-6
MLA_SCALE = 1.0 / math.sqrt(NOPE + ROPE)
LOG2_E = math.log2(math.e)
DIL_SCALE = 1.0 / math.sqrt(HEAD_DIM)

ADAM_LR = 0.001
ADAM_B1 = 0.9
ADAM_B2 = 0.999
ADAM_EPS = 1e-08
ADAM_WD = 0.01
ADAM_STEP = 10

LANES = 128
SUBLANES = 8
VMEM_LIMIT_BYTES = 56 * 1024 * 1024
DW_TOKENS = 1024

MESH = pl.DeviceIdType.MESH


def _params(*sem):
    return pltpu.CompilerParams(dimension_semantics=sem, vmem_limit_bytes=VMEM_LIMIT_BYTES)


def _dot(a, b):
    return jnp.dot(a, b, preferred_element_type=F32)


def _dot_nt(a, b):
    return lax.dot_general(a, b, (((1,), (1,)), ((), ())), preferred_element_type=F32)


def _dot_tn(a, b):
    return lax.dot_general(a, b, (((0,), (0,)), ((), ())), preferred_element_type=F32)


def _mx(a):
    return a.astype(MXU_DTYPE)


def _mm_nn(a, b, *, name, tm, tn, tk, out_dtype=F32, add=None, add_scale=1.0):
    m, kdim = a.shape
    blocked = b.ndim == 3
    n = b.shape[0] * b.shape[2] if blocked else b.shape[1]
    nk = kdim // tk

    def body(*refs):
        if add is None:
            a_ref, b_ref, o_ref, acc = refs
        else:
            a_ref, b_ref, c_ref, o_ref, acc = refs
        k = pl.program_id(2)

        @pl.when(k == 0)
        def _():
            acc[...] = jnp.zeros_like(acc)

        acc[...] += _dot(_mx(a_ref[...]), _mx(b_ref[...]))

        @pl.when(k == nk - 1)
        def _():
            r = acc[...]
            if add is not None:
                r = r + add_scale * c_ref[...]
            o_ref[...] = r.astype(out_dtype)

    b_spec = (pl.BlockSpec((None, tk, tn), lambda i, j, k: (j, k, 0)) if blocked
              else pl.BlockSpec((tk, tn), lambda i, j, k: (k, j)))
    in_specs = [pl.BlockSpec((tm, tk), lambda i, j, k: (i, k)), b_spec]
    args = [a, b]
    if add is not None:
        in_specs.append(pl.BlockSpec((tm, tn), lambda i, j, k: (i, j)))
        args.append(add)
    return pl.pallas_call(
        body, name=name,
        out_shape=jax.ShapeDtypeStruct((m, n), out_dtype),
        grid=(m // tm, n // tn, nk),
        in_specs=in_specs,
        out_specs=pl.BlockSpec((tm, tn), lambda i, j, k: (i, j)),
        scratch_shapes=[pltpu.VMEM((tm, tn), F32)],
        compiler_params=_params("parallel", "parallel", "arbitrary"),
    )(*args)


def _mm_tn(a, b, *, name, tm, tn, ts, out_dtype=F32):
    s, m = a.shape
    n = b.shape[1]
    ns = s // ts

    def body(a_ref, b_ref, o_ref, acc):
        k = pl.program_id(2)

        @pl.when(k == 0)
        def _():
            acc[...] = jnp.zeros_like(acc)

        acc[...] += _dot_tn(_mx(a_ref[...]), _mx(b_ref[...]))

        @pl.when(k == ns - 1)
        def _():
            o_ref[...] = acc[...].astype(out_dtype)

    return pl.pallas_call(
        body, name=name,
        out_shape=jax.ShapeDtypeStruct((m, n), out_dtype),
        grid=(m // tm, n // tn, ns),
        in_specs=[pl.BlockSpec((ts, tm), lambda i, j, k: (k, i)),
                  pl.BlockSpec((ts, tn), lambda i, j, k: (k, j))],
        out_specs=pl.BlockSpec((tm, tn), lambda i, j, k: (i, j)),
        scratch_shapes=[pltpu.VMEM((tm, tn), F32)],
        compiler_params=_params("parallel", "parallel", "arbitrary"),
    )(a, b)


def _in_proj(x, w_in_ext, *, tm):
    s = x.shape[0]
    mla_w = 4 * LANES
    dil_w = HEADS * HEAD_DIM
    dils = [d for _, d in DIL_PAIRS]

    def body(x_ref, w_ref, h_ref, *rest):
        outs, sc = rest[:-1], rest[-1]
        xb = _mx(x_ref[...])
        h_ref[...] = _dot(xb, w_ref[:, 0:mla_w])
        for j in range(3):
            part = _dot(xb, w_ref[:, mla_w + j * dil_w:mla_w + (j + 1) * dil_w])
            for hd in range(HEADS):
                sc[hd] = part[:, hd * HEAD_DIM:(hd + 1) * HEAD_DIM]
            for b, d in enumerate(dils):
                _store_residue_major(outs[3 * j + b], sc, d, tm)

    shapes, specs = _residue_major_outs(s, tm, dils, MXU_DTYPE)
    res = pl.pallas_call(
        body, name="in_proj",
        out_shape=(jax.ShapeDtypeStruct((s, mla_w), F32),) + shapes * 3,
        grid=(s // tm,),
        in_specs=[pl.BlockSpec((tm, D_MODEL), lambda i: (i, 0)), pl.BlockSpec((D_MODEL, IN_EXT), lambda i: (0, 0))],
        out_specs=(pl.BlockSpec((tm, mla_w), lambda i: (i, 0)),) + specs * 3,
        scratch_shapes=[pltpu.VMEM((HEADS, tm, HEAD_DIM), F32)],
        compiler_params=_params("parallel"),
    )(x, w_in_ext)
    hm = lambda a: a.reshape(HEADS, s, HEAD_DIM)
    return res[0], [hm(a) for a in res[1:4]], [hm(a) for a in res[4:7]], [hm(a) for a in res[7:10]]


def _residue_major_outs(s, tm, dils, dtype):
    shapes, specs = [], []
    for d in dils:
        if d == 1:
            shapes.append(jax.ShapeDtypeStruct((HEADS, s, HEAD_DIM), dtype))
            specs.append(pl.BlockSpec((HEADS, tm, HEAD_DIM), lambda i: (0, i, 0)))
        else:
            shapes.append(jax.ShapeDtypeStruct((HEADS, d, s // d, HEAD_DIM), dtype))
            specs.append(pl.BlockSpec((HEADS, d, tm // d, HEAD_DIM), lambda i: (0, 0, i, 0)))
    return tuple(shapes), tuple(specs)


def _store_residue_major(o_ref, src_ref, d, tm):
    if d == 1:
        o_ref[...] = src_ref[...].astype(o_ref.dtype)
    else:
        for r in range(d):
            o_ref[:, r] = src_ref[:, pl.ds(r, tm // d, stride=d), :].astype(o_ref.dtype)


def _load_token_order(dst_ref, src_ref, d, tm, accumulate=False):
    if d == 1:
        dst_ref[...] = dst_ref[...] + src_ref[...] if accumulate else src_ref[...]
    else:
        for r in range(d):
            rows = pl.ds(r, tm // d, stride=d)
            dst_ref[:, rows, :] = dst_ref[:, rows, :] + src_ref[:, r] if accumulate else src_ref[:, r]


def _attn_bwd_heads(dz1, w_o_t, a_mla, a_dil, *, tm, swap=()):
    s = dz1.shape[0]
    half = HEADS * HEAD_DIM
    dils = [d for _, d in DIL_PAIRS]
    nsw = len(swap)
    n_steps = s // tm

    def body(*refs):
        dz_ref, w_ref, am_ref, ad_ref = refs[:4]
        gs_refs = refs[4:4 + nsw]
        dom_ref, dd_ref = refs[4 + nsw:6 + nsw]
        dod_refs = refs[6 + nsw:6 + nsw + len(dils)]
        os_refs = refs[6 + nsw + len(dils):6 + 2 * nsw + len(dils)]
        if nsw:
            send_sems, recv_sems = refs[6 + 2 * nsw + len(dils):]
            i = pl.program_id(0)
            _swap_halves_in_steps(gs_refs, os_refs, send_sems, recv_sems, first=i == 0, last=i == n_steps - 1)
        dzb = _mx(dz_ref[...])
        for j, (a_ref, o_ref) in enumerate(((am_ref, dom_ref), (ad_ref, dod_refs[0]))):
            da = _dot(dzb, w_ref[:, j * half:(j + 1) * half])
            prod = da * a_ref[...]
            for hd in range(HEADS):
                sl = slice(hd * HEAD_DIM, (hd + 1) * HEAD_DIM)
                o_ref[hd] = da[:, sl].astype(o_ref.dtype)
                dd_ref[:, j * HEADS + hd:j * HEADS + hd + 1] = jnp.sum(prod[:, sl], axis=-1, keepdims=True)
        for b, d in enumerate(dils[1:]):
            _store_residue_major(dod_refs[1 + b], dod_refs[0], d, tm)

    hspec = pl.BlockSpec((HEADS, tm, HEAD_DIM), lambda i: (0, i, 0))
    row = lambda w: pl.BlockSpec((tm, w), lambda i: (i, 0))
    shapes, specs = _residue_major_outs(s, tm, dils, F32)
    n_sem = nsw * N_CHIPS
    do_mla, dd, *rest = pl.pallas_call(
        body, name="attn_bwd_heads",
        out_shape=(jax.ShapeDtypeStruct((HEADS, s, HEAD_DIM), MXU_DTYPE), jax.ShapeDtypeStruct((s, 2 * HEADS), F32)) + shapes
        + tuple(jax.ShapeDtypeStruct((N_CHIPS,) + a.shape[2:], F32) for a in swap),
        grid=(n_steps,),
        in_specs=[row(D_MODEL), pl.BlockSpec((D_MODEL, D_MODEL), lambda i: (0, 0)), row(half), row(half)] + [ANY] * nsw,
        out_specs=(hspec, row(2 * HEADS)) + specs + (ANY,) * nsw,
        scratch_shapes=[pltpu.SemaphoreType.DMA((n_sem,)), pltpu.SemaphoreType.DMA((n_sem,))] if nsw else [],
        compiler_params=pltpu.CompilerParams(dimension_semantics=("arbitrary",), vmem_limit_bytes=VMEM_LIMIT_BYTES,
                                             has_side_effects=nsw > 0),
    )(dz1, w_o_t, a_mla, a_dil, *swap)
    do_dil, received = rest[:len(dils)], rest[len(dils):]
    return do_mla, [a.reshape(HEADS, s, HEAD_DIM) for a in do_dil], dd, received


def _dil_merge(parts, *, ts):
    hds, s, e = parts[0][0].shape
    dils = [d for _, d in DIL_PAIRS]

    def body(*refs):
        o_ref, sc = refs[9], refs[10]
        for j in range(3):
            for b, d in enumerate(dils):
                _load_token_order(sc, refs[3 * b + j], d, ts, accumulate=b > 0)
            tot = sc[...]
            for hd in range(hds):
                col = j * hds * e + hd * e
                o_ref[:, col:col + e] = tot[hd].astype(o_ref.dtype)

    _, specs = _residue_major_outs(s, ts, dils, F32)
    view = lambda a, d: a if d == 1 else a.reshape(hds, d, s // d, e)
    return pl.pallas_call(
        body, name="dil_merge",
        out_shape=jax.ShapeDtypeStruct((s, 3 * hds * e), MXU_DTYPE),
        grid=(s // ts,),
        in_specs=[specs[b] for b in range(3) for _ in range(3)],
        out_specs=pl.BlockSpec((ts, 3 * hds * e), lambda i: (i, 0)),
        scratch_shapes=[pltpu.VMEM((hds, ts, e), F32)],
        compiler_params=_params("parallel"),
    )(*[view(parts[b][j], dils[b]) for b in range(3) for j in range(3)])


def _rope_tables(s):
    half = ROPE // 2
    freqs = ROPE_THETA ** (-jnp.arange(half, dtype=F32) / half)
    ang = jnp.arange(s).astype(F32)[:, None] * freqs[None, :]
    cos, sin = jnp.cos(ang), jnp.sin(ang)
    z = lambda w: jnp.zeros((s, w), F32)
    c = jnp.concatenate([jnp.ones((s, NOPE), F32), cos, cos, z(32)], axis=1)
    s1 = jnp.concatenate([z(NOPE + half), sin, z(32)], axis=1)
    s2 = jnp.concatenate([z(NOPE), -sin, z(half + 32)], axis=1)
    mask = jnp.concatenate([z(NOPE), jnp.ones((s, ROPE), F32), z(32)], axis=1)
    return c, s1, s2, mask


def _rope(x, c, s1, s2):
    return x * c + pltpu.roll(x, 16, 1) * s1 + pltpu.roll(x, LANES - 16, 1) * s2


def _unrope(dy, c, s1, s2):
    return dy * c + pltpu.roll(dy * s1, LANES - 16, 1) + pltpu.roll(dy * s2, 16, 1)


def _rms(x):
    r = lax.rsqrt(jnp.mean(x * x, axis=-1, keepdims=True) + RMS_EPS)
    return x * r, r


def _mla_prep_fwd(h, g_cq, g_ckv, wq, wk, wv, wv_t, tabs, *, tm):
    s = h.shape[0]
    c_t, s1_t, s2_t, _ = tabs

    def body(h_ref, gq_ref, gkv_ref, wq_ref, wk_ref, wv_ref, wvt_ref, c_ref, s1_ref, s2_ref,
             q_ref, k_ref, v_ref, vt_ref):
        cq = h_ref[:, 0:Q_RANK]
        ckv = h_ref[:, Q_RANK:Q_RANK + KV_RANK]
        kr = h_ref[:, Q_RANK + KV_RANK:Q_RANK + KV_RANK + QK_PAD]
        c, s1, s2 = c_ref[...], s1_ref[...], s2_ref[...]
        cqn = _mx(_rms(cq)[0] * gq_ref[...])
        ckvn = _mx(_rms(ckv)[0] * gkv_ref[...])
        kr_rot = _rope(kr, c, s1, s2)
        for hd in range(HEADS):
            q_ref[hd] = _rope(_dot(cqn, wq_ref[hd]), c, s1, s2).astype(q_ref.dtype)
            k_ref[hd] = (_dot(ckvn, wk_ref[hd]) + kr_rot).astype(k_ref.dtype)
            v_ref[hd] = _dot(ckvn, wv_ref[hd]).astype(v_ref.dtype)
            vt_ref[hd] = _dot_nt(wvt_ref[hd], ckvn).astype(vt_ref.dtype)

    full = lambda shp: pl.BlockSpec(shp, lambda i: (0,) * len(shp))
    row = lambda w: pl.BlockSpec((tm, w), lambda i: (i, 0))
    return pl.pallas_call(
        body, name="mla_prep_fwd",
        out_shape=(jax.ShapeDtypeStruct((HEADS, s, QK_PAD), MXU_DTYPE),
                   jax.ShapeDtypeStruct((HEADS, s, QK_PAD), MXU_DTYPE),
                   jax.ShapeDtypeStruct((HEADS, s, HEAD_DIM), MXU_DTYPE),
                   jax.ShapeDtypeStruct((HEADS, HEAD_DIM, s), MXU_DTYPE)),
        grid=(s // tm,),
        in_specs=[row(4 * LANES), full((1, Q_RANK)), full((1, KV_RANK)),
                  full((HEADS, Q_RANK, QK_PAD)), full((HEADS, KV_RANK, QK_PAD)), full((HEADS, KV_RANK, HEAD_DIM)),
                  full((HEADS, HEAD_DIM, KV_RANK)), row(LANES), row(LANES), row(LANES)],
        out_specs=(pl.BlockSpec((HEADS, tm, QK_PAD), lambda i: (0, i, 0)),
                   pl.BlockSpec((HEADS, tm, QK_PAD), lambda i: (0, i, 0)),
                   pl.BlockSpec((HEADS, tm, HEAD_DIM), lambda i: (0, i, 0)),
                   pl.BlockSpec((HEADS, HEAD_DIM, tm), lambda i: (0, 0, i))),
        compiler_params=_params("parallel"),
    )(h, g_cq, g_ckv, wq, wk, wv, wv_t, c_t, s1_t, s2_t)


def _mla_prep_bwd(h, dq, dk, dv, g_cq, g_ckv, wq_t, wk_t, wv_t, tabs, *, tm):
    s = h.shape[0]
    c_t, s1_t, s2_t, mask_t = tabs

    def body(h_ref, dq_ref, dk_ref, dv_ref, gq_ref, gkv_ref, wqt_ref, wkt_ref, wvt_ref,
             c_ref, s1_ref, s2_ref, mask_ref, dh_ref, dwq_ref, dwk_ref, dwv_ref, dgq_ref, dgkv_ref):
        i = pl.program_id(0)

        @pl.when(i == 0)
        def _():
            dwq_ref[...] = jnp.zeros_like(dwq_ref)
            dwk_ref[...] = jnp.zeros_like(dwk_ref)
            dwv_ref[...] = jnp.zeros_like(dwv_ref)
            dgq_ref[...] = jnp.zeros_like(dgq_ref)
            dgkv_ref[...] = jnp.zeros_like(dgkv_ref)

        cq = h_ref[:, 0:Q_RANK]
        ckv = h_ref[:, Q_RANK:Q_RANK + KV_RANK]
        c, s1, s2 = c_ref[...], s1_ref[...], s2_ref[...]
        cqh, rq = _rms(cq)
        ckvh, rkv = _rms(ckv)
        gq, gkv = gq_ref[...], gkv_ref[...]
        cqn = _mx(cqh * gq)
        ckvn = _mx(ckvh * gkv)
        dcqn = jnp.zeros((tm, Q_RANK), F32)
        dckvn = jnp.zeros((tm, KV_RANK), F32)
        dkr = jnp.zeros((tm, QK_PAD), F32)
        for hd in range(HEADS):
            dqh = _mx(_unrope(dq_ref[hd], c, s1, s2))
            dcqn = dcqn + _dot(dqh, wqt_ref[hd])
            dwq_ref[hd] += _dot_tn(cqn, dqh)
            dkh = dk_ref[hd]
            dkr = dkr + dkh
            dkh = _mx(dkh)
            dckvn = dckvn + _dot(dkh, wkt_ref[hd])
            dwk_ref[hd] += _dot_tn(ckvn, dkh)
            dvh = _mx(dv_ref[hd])
            dckvn = dckvn + _dot(dvh, wvt_ref[hd])
            dwv_ref[hd] += _dot_tn(ckvn, dvh)
        dgq_ref[...] += jnp.sum(dcqn * cqh, axis=0, keepdims=True)
        dgkv_ref[...] += jnp.sum(dckvn * ckvh, axis=0, keepdims=True)
        gd = dcqn * gq
        dh_ref[:, 0:Q_RANK] = rq * (gd - cqh * jnp.mean(gd * cqh, axis=-1, keepdims=True))
        gd = dckvn * gkv
        dh_ref[:, Q_RANK:Q_RANK + KV_RANK] = rkv * (gd - ckvh * jnp.mean(gd * ckvh, axis=-1, keepdims=True))
        dh_ref[:, Q_RANK + KV_RANK:Q_RANK + KV_RANK + QK_PAD] = _unrope(dkr, c, s1, s2) * mask_ref[...]

    full = lambda shp: pl.BlockSpec(shp, lambda i: (0,) * len(shp))
    row = lambda w: pl.BlockSpec((tm, w), lambda i: (i, 0))
    hrow = lambda w: pl.BlockSpec((HEADS, tm, w), lambda i: (0, i, 0))
    return pl.pallas_call(
        body, name="mla_prep_bwd",
        out_shape=(jax.ShapeDtypeStruct((s, 4 * LANES), F32),
                   jax.ShapeDtypeStruct((HEADS, Q_RANK, QK_PAD), F32),
                   jax.ShapeDtypeStruct((HEADS, KV_RANK, QK_PAD), F32),
                   jax.ShapeDtypeStruct((HEADS, KV_RANK, HEAD_DIM), F32),
                   jax.ShapeDtypeStruct((1, Q_RANK), F32),
                   jax.ShapeDtypeStruct((1, KV_RANK), F32)),
        grid=(s // tm,),
        in_specs=[row(4 * LANES), hrow(QK_PAD), hrow(QK_PAD), hrow(HEAD_DIM),
                  full((1, Q_RANK)), full((1, KV_RANK)),
                  full((HEADS, QK_PAD, Q_RANK)), full((HEADS, QK_PAD, KV_RANK)), full((HEADS, HEAD_DIM, KV_RANK)),
                  row(LANES), row(LANES), row(LANES), row(LANES)],
        out_specs=(row(4 * LANES), full((HEADS, Q_RANK, QK_PAD)), full((HEADS, KV_RANK, QK_PAD)),
                   full((HEADS, KV_RANK, HEAD_DIM)), full((1, Q_RANK)), full((1, KV_RANK))),
        compiler_params=_params("arbitrary"),
    )(h, dq, dk, dv, g_cq, g_ckv, wq_t, wk_t, wv_t, c_t, s1_t, s2_t, mask_t)


def _bdot(a, b, ca, cb):
    return lax.dot_general(a, b, (((ca,), (cb,)), ((0,), (0,))), preferred_element_type=F32)


def _causal_mask_t(t):
    kk = lax.broadcasted_iota(jnp.int32, (t, t), 0)
    qq = lax.broadcasted_iota(jnp.int32, (t, t), 1)
    return (qq >= kk)[None]


def _mla_attn_fwd(q, k, v_t, *, t, g, late=None):
    hds, s, _ = q.shape
    n = s // t
    n_groups = hds // g

    nl = 0 if late is None else len(late)

    def body(*refs):
        q_ref, k_ref, vt_ref = refs[:3]
        wp_refs = refs[3:3 + nl]
        o_ref, lse_ref = refs[3 + nl:5 + nl]
        wout_refs = refs[5 + nl:5 + 2 * nl]
        m_sc, l_sc, acc_sc = refs[5 + 2 * nl:8 + 2 * nl]
        hg, qi, ki = pl.program_id(0), pl.program_id(1), pl.program_id(2)
        if nl:
            send_sems, recv_sems = refs[8 + 2 * nl:]
            tail = jnp.logical_and(hg == n_groups - 1, qi == n - 1)
            _gather_in_steps(wp_refs, wout_refs, send_sems, recv_sems,
                             first=jnp.logical_and(hg == 0, jnp.logical_and(qi == 0, ki == 0)),
                             mid=jnp.logical_and(tail, ki == 0), last=jnp.logical_and(tail, ki == n - 1))

        @pl.when(ki == 0)
        def _():
            m_sc[...] = jnp.full_like(m_sc, NEG)
            l_sc[...] = jnp.zeros_like(l_sc)
            acc_sc[...] = jnp.zeros_like(acc_sc)

        def step(masked):
            sc = _bdot(k_ref[...], q_ref[...], 2, 2)
            if masked:
                sc = jnp.where(_causal_mask_t(t), sc, NEG)
            m_prev = m_sc[...]
            m_new = jnp.maximum(m_prev, jnp.max(sc, axis=1, keepdims=True))
            p = jnp.exp2((sc - m_new) * (MLA_SCALE * LOG2_E))
            a = jnp.exp2((m_prev - m_new) * (MLA_SCALE * LOG2_E))
            l_sc[...] = a * l_sc[...] + jnp.sum(p, axis=1, keepdims=True)
            acc_sc[...] = a * acc_sc[...] + _bdot(vt_ref[...], _mx(p), 2, 1)
            m_sc[...] = m_new

        @pl.when(ki < qi)
        def _():
            step(False)

        @pl.when(ki == qi)
        def _():
            step(True)
            o_ref[...] = acc_sc[...] / l_sc[...]
            lse_ref[...] = m_sc[...] * MLA_SCALE + jnp.log(l_sc[...])

    qspec = pl.BlockSpec((g, t, QK_PAD), lambda h, i, j: (h, i, 0))
    kspec = pl.BlockSpec((g, t, QK_PAD), lambda h, i, j: (h, jnp.minimum(i, j), 0))
    vspec = pl.BlockSpec((g, HEAD_DIM, t), lambda h, i, j: (h, 0, jnp.minimum(i, j)))
    out_shape = [jax.ShapeDtypeStruct((hds, HEAD_DIM, s), F32), jax.ShapeDtypeStruct((hds, 1, s), F32)]
    in_specs = [qspec, kspec, vspec]
    out_specs = [pl.BlockSpec((g, HEAD_DIM, t), lambda h, i, j: (h, 0, i)), pl.BlockSpec((g, 1, t), lambda h, i, j: (h, 0, i))]
    scratch = [pltpu.VMEM((g, 1, t), F32), pltpu.VMEM((g, 1, t), F32), pltpu.VMEM((g, HEAD_DIM, t), F32)]
    args = [q, k, v_t]
    if nl:
        out_shape += [jax.ShapeDtypeStruct((N_CHIPS,) + a.shape, a.dtype) for a in late]
        in_specs += [ANY] * nl
        out_specs += [ANY] * nl
        scratch += [pltpu.SemaphoreType.DMA((6 * nl,)), pltpu.SemaphoreType.DMA((6 * nl,))]
        args += list(late)
    return pl.pallas_call(
        body, name="mla_attn_fwd",
        out_shape=tuple(out_shape), grid=(n_groups, n, n),
        in_specs=in_specs, out_specs=tuple(out_specs), scratch_shapes=scratch,
        compiler_params=pltpu.CompilerParams(dimension_semantics=("arbitrary",) * 3, vmem_limit_bytes=VMEM_LIMIT_BYTES,
                                             has_side_effects=nl > 0),
    )(*args)


def _mla_attn_bwd(q, k, v, do, lse, dd, *, t, g, early=()):
    hds, s, _ = q.shape
    n = s // t
    n_groups = hds // g
    ne = len(early)

    def body(*refs):
        q_ref, k_ref, v_ref, do_ref, lse_ref, dd_ref = refs[:6]
        ps_refs = refs[6:6 + ne]
        dq_ref, dk_ref, dv_ref = refs[6 + ne:9 + ne]
        ss_refs = refs[9 + ne:9 + 2 * ne]
        dq_sc, dk_sc, dv_sc = refs[9 + 2 * ne:12 + 2 * ne]
        hg, ki, qi = pl.program_id(0), pl.program_id(1), pl.program_id(2)
        if ne:
            send_sems, recv_sems = refs[12 + 2 * ne:]
            _exchange_in_steps(ps_refs, ss_refs, send_sems, recv_sems,
                               first=jnp.logical_and(hg == 0, jnp.logical_and(ki == 0, qi == 0)),
                               last=jnp.logical_and(hg == n_groups - 1, jnp.logical_and(ki == n - 1, qi == n - 1)))

        @pl.when(jnp.logical_and(ki == 0, qi == 0))
        def _():
            dq_sc[...] = jnp.zeros_like(dq_sc)

        @pl.when(qi == 0)
        def _():
            dk_sc[...] = jnp.zeros_like(dk_sc)
            dv_sc[...] = jnp.zeros_like(dv_sc)

        def step(masked):
            qb, kb, dob = q_ref[...], k_ref[...], do_ref[...]
            sc = _bdot(kb, qb, 2, 2) * MLA_SCALE
            if masked:
                sc = jnp.where(_causal_mask_t(t), sc, NEG)
            p = jnp.exp(sc - lse_ref[...])
            dv_sc[...] += _bdot(_mx(p), dob, 2, 1)
            dp = _bdot(v_ref[...], dob, 2, 2)
            ds = _mx(p * (dp - dd_ref[...]) * MLA_SCALE)
            dk_sc[...] += _bdot(ds, qb, 2, 1)
            dq_sc[qi] += _bdot(ds, kb, 1, 1)

        @pl.when(qi == ki)
        def _():
            step(True)

        @pl.when(qi > ki)
        def _():
            step(False)

        @pl.when(qi == n - 1)
        def _():
            dk_ref[...] = dk_sc[...]
            dv_ref[...] = dv_sc[...]

        @pl.when(jnp.logical_and(ki == n - 1, qi == n - 1))
        def _():
            for j in range(n):
                dq_ref[:, j * t:(j + 1) * t, :] = dq_sc[j]

    qs = lambda w: pl.BlockSpec((g, t, w), lambda h, j, i: (h, jnp.maximum(i, j), 0))
    ks = lambda w: pl.BlockSpec((g, t, w), lambda h, j, i: (h, j, 0))
    rowq = pl.BlockSpec((g, 1, t), lambda h, j, i: (h, 0, jnp.maximum(i, j)))
    scratch = [pltpu.VMEM((n, g, t, QK_PAD), F32), pltpu.VMEM((g, t, QK_PAD), F32), pltpu.VMEM((g, t, HEAD_DIM), F32)]
    if ne:
        scratch += [pltpu.SemaphoreType.DMA((3 * ne,)), pltpu.SemaphoreType.DMA((3 * ne,))]
    return pl.pallas_call(
        body, name="mla_attn_bwd",
        out_shape=(jax.ShapeDtypeStruct((hds, s, QK_PAD), F32), jax.ShapeDtypeStruct((hds, s, QK_PAD), F32),
                   jax.ShapeDtypeStruct((hds, s, HEAD_DIM), F32)) + tuple(jax.ShapeDtypeStruct(a.shape, a.dtype) for a in early),
        grid=(n_groups, n, n),
        in_specs=[qs(QK_PAD), ks(QK_PAD), ks(HEAD_DIM), qs(HEAD_DIM), rowq, rowq] + [ANY] * ne,
        out_specs=(pl.BlockSpec((g, s, QK_PAD), lambda h, j, i: (h, 0, 0)), ks(QK_PAD), ks(HEAD_DIM)) + (ANY,) * ne,
        scratch_shapes=scratch,
        compiler_params=pltpu.CompilerParams(dimension_semantics=("arbitrary",) * 3, vmem_limit_bytes=VMEM_LIMIT_BYTES,
                                             has_side_effects=ne > 0),
    )(q, k, v, do, lse, dd, *early)


def _perm_row(a, dil):
    if dil == 1:
        return a
    hds, _, s = a.shape
    return a.reshape(hds, s // dil, dil).transpose(0, 2, 1).reshape(hds, 1, s)


def _unperm_row(a, dil):
    if dil == 1:
        return a
    hds, _, s = a.shape
    return a.reshape(hds, dil, s // dil).transpose(0, 2, 1).reshape(hds, 1, s)


def _dil_bias(dil):
    slopes = 2.0 ** (-8.0 * jnp.arange(1, HEADS + 1, dtype=F32) / HEADS)
    ik = jnp.arange(DIL_BLOCK)[:, None]
    iq = jnp.arange(DIL_BLOCK)[None, :]
    off_c = iq - ik
    off_p = iq - ik + DIL_BLOCK
    b_c = -slopes[:, None, None] * (off_c * dil).astype(F32)[None]
    b_p = -slopes[:, None, None] * (off_p * dil).astype(F32)[None]
    b_c = jnp.where((off_c >= 0)[None], b_c, NEG)
    b_p = jnp.where((off_p <= DIL_BLOCK)[None], b_p, NEG)
    return b_c, b_p


def _dil_fwd(q, k, v, dil, *, name):
    hds, s, e = q.shape
    blk = DIL_BLOCK
    nblk = s // blk
    nb = nblk // dil
    pair = 2 if nb % 2 == 0 else 1
    b_c, b_p = _dil_bias(dil)

    def body(q_ref, k_ref, kp_ref, v_ref, vp_ref, bc_ref, bp_ref, o_ref, lse_ref):
        first = ((pair * pl.program_id(0)) % nb) == 0
        bc, bp = bc_ref[...], bp_ref[...]
        for j in range(pair):
            rows = slice(j * blk, (j + 1) * blk)
            qb = q_ref[:, rows, :]
            if j == 0:
                kp, vp = kp_ref[...], vp_ref[...]
            else:
                kp, vp = k_ref[:, (j - 1) * blk:j * blk, :], v_ref[:, (j - 1) * blk:j * blk, :]
            s_c = _bdot(k_ref[:, rows, :], qb, 2, 2) * DIL_SCALE + bc
            s_p = _bdot(kp, qb, 2, 2) * DIL_SCALE + bp
            if j == 0:
                s_p = jnp.where(first, NEG, s_p)
            m = jnp.maximum(jnp.max(s_c, axis=1, keepdims=True), jnp.max(s_p, axis=1, keepdims=True))
            p_c = jnp.exp(s_c - m)
            p_p = jnp.exp(s_p - m)
            l = jnp.sum(p_c, axis=1, keepdims=True) + jnp.sum(p_p, axis=1, keepdims=True)
            o = _bdot(_mx(p_c), v_ref[:, rows, :], 1, 1) + _bdot(_mx(p_p), vp, 1, 1)
            o_ref[:, rows, :] = o / jnp.swapaxes(l, 1, 2)
            lse_ref[:, :, rows] = m + jnp.log(l)

    cur = lambda w: pl.BlockSpec((hds, pair * blk, w), lambda b: (0, b, 0))
    prev = lambda w: pl.BlockSpec((hds, blk, w), lambda b: (0, jnp.maximum(pair * b - 1, 0), 0))
    bias = pl.BlockSpec((hds, blk, blk), lambda b: (0, 0, 0))
    return pl.pallas_call(
        body, name=name,
        out_shape=(jax.ShapeDtypeStruct((hds, s, e), F32), jax.ShapeDtypeStruct((hds, 1, s), F32)),
        grid=(nblk // pair,),
        in_specs=[cur(e), cur(e), prev(e), cur(e), prev(e), bias, bias],
        out_specs=(cur(e), pl.BlockSpec((hds, 1, pair * blk), lambda b: (0, 0, b))),
        compiler_params=_params("parallel"),
    )(q, k, k, v, v, b_c, b_p)


def _dil_combine(os_, lses, *, ts):
    hds, s, e = os_[0].shape
    dils = [d for _, d in DIL_PAIRS]

    def body(o0, o1, o2, l0, l1, l2, o_ref, l_ref, sc1, sc2):
        _load_token_order(sc1, o1, dils[1], ts)
        _load_token_order(sc2, o2, dils[2], ts)
        a0, a1, a2 = l0[...], l1[...], l2[...]
        m = jnp.maximum(jnp.maximum(a0, a1), a2)
        e0, e1, e2 = jnp.exp(a0 - m), jnp.exp(a1 - m), jnp.exp(a2 - m)
        tot = e0 + e1 + e2
        col = lambda w: jnp.swapaxes(w, 1, 2)
        res = (col(e0 / tot) * o0[...] + col(e1 / tot) * sc1[...]) + col(e2 / tot) * sc2[...]
        for hd in range(hds):
            o_ref[:, hd * e:(hd + 1) * e] = res[hd]
        l_ref[...] = m + jnp.log(tot)

    _, specs = _residue_major_outs(s, ts, dils, F32)
    view = lambda a, d: a if d == 1 else a.reshape(hds, d, s // d, e)
    rspec = pl.BlockSpec((hds, 1, ts), lambda i: (0, 0, i))
    return pl.pallas_call(
        body, name="dil_combine",
        out_shape=(jax.ShapeDtypeStruct((s, hds * e), F32), jax.ShapeDtypeStruct((hds, 1, s), F32)),
        grid=(s // ts,),
        in_specs=list(specs) + [rspec] * 3,
        out_specs=(pl.BlockSpec((ts, hds * e), lambda i: (i, 0)), rspec),
        scratch_shapes=[pltpu.VMEM((hds, ts, e), F32), pltpu.VMEM((hds, ts, e), F32)],
        compiler_params=_params("parallel"),
    )(*[view(a, d) for a, d in zip(os_, dils)], *lses)


def _dil_bwd(q, k, v, do, lj, dd, dil, *, name):
    hds, s, e = q.shape
    blk = DIL_BLOCK
    nblk = s // blk
    nb = nblk // dil
    pair = 2 if nb % 2 == 0 else 1
    b_c, b_p = _dil_bias(dil)

    def body(q_ref, qn_ref, k_ref, kp_ref, v_ref, vp_ref, do_ref, don_ref, l_ref, ln_ref, d_ref, dn_ref,
             bc_ref, bp_ref, dq_ref, dk_ref, dv_ref):
        b0 = pair * pl.program_id(0)
        first = (b0 % nb) == 0
        nxt = jnp.logical_and(b0 + pair < nblk, ((b0 + pair) % nb) != 0)
        bc, bp = bc_ref[...], bp_ref[...]
        for j in range(pair):
            rows = slice(j * blk, (j + 1) * blk)
            qb, kc, vc = q_ref[:, rows, :], k_ref[:, rows, :], v_ref[:, rows, :]
            dob, l, d = _mx(do_ref[:, rows, :]), l_ref[:, :, rows], d_ref[:, :, rows]
            if j == 0:
                kp, vp = kp_ref[...], vp_ref[...]
            else:
                kp, vp = k_ref[:, (j - 1) * blk:j * blk, :], v_ref[:, (j - 1) * blk:j * blk, :]
            p_c = jnp.exp(_bdot(kc, qb, 2, 2) * DIL_SCALE + bc - l)
            p_p = jnp.exp(_bdot(kp, qb, 2, 2) * DIL_SCALE + bp - l)
            if j == 0:
                p_p = jnp.where(first, 0.0, p_p)
            ds_c = _mx(p_c * (_bdot(vc, dob, 2, 2) - d) * DIL_SCALE)
            ds_p = _mx(p_p * (_bdot(vp, dob, 2, 2) - d) * DIL_SCALE)
            dq_ref[:, rows, :] = _bdot(ds_c, kc, 1, 1) + _bdot(ds_p, kp, 1, 1)
            if j < pair - 1:
                nrows = slice((j + 1) * blk, (j + 2) * blk)
                qn, donb, ln, dn = q_ref[:, nrows, :], _mx(do_ref[:, nrows, :]), l_ref[:, :, nrows], d_ref[:, :, nrows]
            else:
                qn, donb, ln, dn = qn_ref[...], _mx(don_ref[...]), ln_ref[...], dn_ref[...]
            p_n = jnp.exp(_bdot(kc, qn, 2, 2) * DIL_SCALE + bp - ln)
            if j == pair - 1:
                p_n = jnp.where(nxt, p_n, 0.0)
            ds_n = _mx(p_n * (_bdot(vc, donb, 2, 2) - dn) * DIL_SCALE)
            dk_ref[:, rows, :] = _bdot(ds_c, qb, 2, 1) + _bdot(ds_n, qn, 2, 1)
            dv_ref[:, rows, :] = _bdot(_mx(p_c), dob, 2, 1) + _bdot(_mx(p_n), donb, 2, 1)

    cur = lambda w: pl.BlockSpec((hds, pair * blk, w), lambda b: (0, b, 0))
    prev = lambda w: pl.BlockSpec((hds, blk, w), lambda b: (0, jnp.maximum(pair * b - 1, 0), 0))
    nxt_ = lambda w: pl.BlockSpec((hds, blk, w), lambda b: (0, jnp.minimum(pair * (b + 1), nblk - 1), 0))
    rcur = pl.BlockSpec((hds, 1, pair * blk), lambda b: (0, 0, b))
    rnxt = pl.BlockSpec((hds, 1, blk), lambda b: (0, 0, jnp.minimum(pair * (b + 1), nblk - 1)))
    bias = pl.BlockSpec((hds, blk, blk), lambda b: (0, 0, 0))
    out = jax.ShapeDtypeStruct((hds, s, e), F32)
    return pl.pallas_call(
        body, name=name,
        out_shape=(out, out, out),
        grid=(nblk // pair,),
        in_specs=[cur(e), nxt_(e), cur(e), prev(e), cur(e), prev(e), cur(e), nxt_(e),
                  rcur, rnxt, rcur, rnxt, bias, bias],
        out_specs=(cur(e), cur(e), cur(e)),
        compiler_params=_params("parallel"),
    )(q, q, k, k, v, v, do, do, lj, lj, dd, dd, b_c, b_p)


def _ln_fwd(z, g, b):
    mu = jnp.mean(z, axis=-1, keepdims=True)
    zc = z - mu
    var = jnp.mean(zc * zc, axis=-1, keepdims=True)
    rstd = lax.rsqrt(var + LN_EPS)
    xhat = zc * rstd
    return xhat * g + b, xhat, rstd


def _ln_bwd(dy, xhat, rstd, g):
    dxh = dy * g
    return rstd * (dxh - jnp.mean(dxh, axis=-1, keepdims=True) - xhat * jnp.mean(dxh * xhat, axis=-1, keepdims=True))


def _out_ln1(a_mla, a_dil, w_o, x, g, b, *, tm):
    s = x.shape[0]
    half = HEADS * HEAD_DIM

    def body(am_ref, ad_ref, w_ref, x_ref, g_ref, b_ref, x1_ref, xh_ref, r_ref):
        mix = _dot(_mx(am_ref[...]), w_ref[0:half, :]) + _dot(_mx(ad_ref[...]), w_ref[half:2 * half, :])
        z = DN_ALPHA * x_ref[...] + mix
        y, xhat, rstd = _ln_fwd(z, g_ref[...], b_ref[...])
        x1_ref[...] = y
        xh_ref[...] = xhat
        r_ref[...] = rstd

    row = lambda w: pl.BlockSpec((tm, w), lambda i: (i, 0))
    full = lambda shp: pl.BlockSpec(shp, lambda i: (0,) * len(shp))
    act = jax.ShapeDtypeStruct((s, D_MODEL), F32)
    return pl.pallas_call(
        body, name="out_ln1",
        out_shape=(act, act, jax.ShapeDtypeStruct((s, 1), F32)),
        grid=(s // tm,),
        in_specs=[row(half), row(half), full((D_MODEL, D_MODEL)), row(D_MODEL), full((1, D_MODEL)), full((1, D_MODEL))],
        out_specs=(row(D_MODEL), row(D_MODEL), row(1)),
        compiler_params=_params("parallel"),
    )(a_mla, a_dil, w_o, x, g, b)


def _down_ln2_loss(act, w_down, x1, g, b, target, *, tm):
    s = x1.shape[0]

    def body(a_ref, w_ref, x1_ref, g_ref, b_ref, t_ref, dz_ref, loss_ref, dg_ref, db_ref):
        i = pl.program_id(0)

        @pl.when(i == 0)
        def _():
            loss_ref[...] = jnp.zeros_like(loss_ref)
            dg_ref[...] = jnp.zeros_like(dg_ref)
            db_ref[...] = jnp.zeros_like(db_ref)

        gam = g_ref[...]
        z = DN_ALPHA * x1_ref[...] + _dot(a_ref[...], w_ref[...])
        y, xhat, rstd = _ln_fwd(z, gam, b_ref[...])
        err = y - t_ref[...]
        loss_ref[...] += 0.5 * jnp.sum(jnp.mean(err * err, axis=-1, keepdims=True))
        dy = err * (1.0 / D_MODEL)
        dg_ref[...] += jnp.sum(dy * xhat, axis=0, keepdims=True)
        db_ref[...] += jnp.sum(dy, axis=0, keepdims=True)
        dz_ref[...] = _ln_bwd(dy, xhat, rstd, gam)

    row = lambda w: pl.BlockSpec((tm, w), lambda i: (i, 0))
    full = lambda shp: pl.BlockSpec(shp, lambda i: (0,) * len(shp))
    vec = jax.ShapeDtypeStruct((1, D_MODEL), F32)
    return pl.pallas_call(
        body, name="down_ln2_loss",
        out_shape=(jax.ShapeDtypeStruct((s, D_MODEL), F32), jax.ShapeDtypeStruct((1, LANES), F32), vec, vec),
        grid=(s // tm,),
        in_specs=[row(D_FF), full((D_FF, D_MODEL)), row(D_MODEL), full((1, D_MODEL)), full((1, D_MODEL)), row(D_MODEL)],
        out_specs=(row(D_MODEL), full((1, LANES)), full((1, D_MODEL)), full((1, D_MODEL))),
        compiler_params=_params("arbitrary"),
    )(act, w_down, x1, g, b, target)


def _up_bwd_ln1(du_a, du_g, w_up_t, dz2, xhat1, rstd1, g, *, tm):
    s = dz2.shape[0]

    def body(dua_ref, dug_ref, wa_ref, wg_ref, dz2_ref, xh_ref, r_ref, g_ref, dz1_ref, dg_ref, db_ref):
        i = pl.program_id(0)

        @pl.when(i == 0)
        def _():
            dg_ref[...] = jnp.zeros_like(dg_ref)
            db_ref[...] = jnp.zeros_like(db_ref)

        dx1 = DN_ALPHA * dz2_ref[...] + (_dot(dua_ref[...], wa_ref[...]) + _dot(dug_ref[...], wg_ref[...]))
        xhat = xh_ref[...]
        dg_ref[...] += jnp.sum(dx1 * xhat, axis=0, keepdims=True)
        db_ref[...] += jnp.sum(dx1, axis=0, keepdims=True)
        dz1_ref[...] = _ln_bwd(dx1, xhat, r_ref[...], g_ref[...])

    row = lambda w: pl.BlockSpec((tm, w), lambda i: (i, 0))
    full = lambda shp: pl.BlockSpec(shp, lambda i: (0,) * len(shp))
    vec = jax.ShapeDtypeStruct((1, D_MODEL), F32)
    return pl.pallas_call(
        body, name="up_bwd_ln1",
        out_shape=(jax.ShapeDtypeStruct((s, D_MODEL), F32), vec, vec),
        grid=(s // tm,),
        in_specs=[row(D_FF), row(D_FF),
                  pl.BlockSpec((D_FF, D_MODEL), lambda i: (0, 0)), pl.BlockSpec((D_FF, D_MODEL), lambda i: (1, 0)),
                  row(D_MODEL), row(D_MODEL), row(1), full((1, D_MODEL))],
        out_specs=(row(D_MODEL), full((1, D_MODEL)), full((1, D_MODEL))),
        compiler_params=_params("arbitrary"),
    )(du_a, du_g, w_up_t, w_up_t, dz2, xhat1, rstd1, g)


GELU_C = math.sqrt(2.0 / math.pi)


def _gelu(x):
    cdf = 0.5 * (1.0 + jnp.tanh(GELU_C * (x + 0.044715 * (x * x * x))))
    return x * cdf


def _gelu_grad(x):
    t = jnp.tanh(GELU_C * (x + 0.044715 * (x * x * x)))
    return 0.5 * (1.0 + t) + 0.5 * x * (1.0 - t * t) * (GELU_C * (1.0 + 3.0 * 0.044715 * (x * x)))


def _shift_down(u, halo):
    r1, r2 = pltpu.roll(u, 1, 0), pltpu.roll(u, 2, 0)
    row = lax.broadcasted_iota(jnp.int32, (SUBLANES, u.shape[1]), 0)
    h7, h6 = halo[7:8, :], halo[6:7, :]
    head1 = jnp.where(row == 0, h7, r1[:SUBLANES])
    head2 = jnp.where(row == 0, h6, jnp.where(row == 1, h7, r2[:SUBLANES]))
    return (jnp.concatenate([head1, r1[SUBLANES:]], axis=0), jnp.concatenate([head2, r2[SUBLANES:]], axis=0))


def _shift_up(d, nxt):
    t = d.shape[0]
    r1, r2 = pltpu.roll(d, t - 1, 0), pltpu.roll(d, t - 2, 0)
    row = lax.broadcasted_iota(jnp.int32, (SUBLANES, d.shape[1]), 0)
    n0, n1 = nxt[0:1, :], nxt[1:2, :]
    last = t - SUBLANES
    tail1 = jnp.where(row == SUBLANES - 1, n0, r1[last:])
    tail2 = jnp.where(row == SUBLANES - 1, n1, jnp.where(row == SUBLANES - 2, n0, r2[last:]))
    return (jnp.concatenate([r1[:last], tail1], axis=0), jnp.concatenate([r2[:last], tail2], axis=0))


def _conv(u, s1, s2, w, b):
    return ((b + w[0:1, :] * s2) + w[1:2, :] * s1) + w[2:3, :] * u


def _up_gate_fwd(x1, w_up, conv_w, conv_b, *, tm, tn):
    s = x1.shape[0]
    nj = D_FF // tn
    hb = tm // SUBLANES

    def body(x_ref, xh_ref, wua_ref, wug_ref, wa_ref, wg_ref, ba_ref, bg_ref,
             ua_ref, ug_ref, o_ref, a_ref, ge_ref, gd_ref):
        keep = pl.program_id(1) > 0
        xb, xh = _mx(x_ref[...]), _mx(xh_ref[...])
        wua, wug = wua_ref[...], wug_ref[...]
        ua, ug = _dot(xb, wua), _dot(xb, wug)
        ha = jnp.where(keep, _dot(xh, wua), 0.0)
        hg = jnp.where(keep, _dot(xh, wug), 0.0)
        ua_ref[...] = ua
        ug_ref[...] = ug
        a = _conv(ua, *_shift_down(ua, ha), wa_ref[...], ba_ref[...])
        g = _conv(ug, *_shift_down(ug, hg), wg_ref[...], bg_ref[...])
        ge = _gelu(g)
        o_ref[...] = (ge * a).astype(o_ref.dtype)
        a_ref[...] = a
        ge_ref[...] = ge
        gd_ref[...] = _gelu_grad(g)

    main = lambda off: pl.BlockSpec((tm, tn), lambda j, i: (i, j + off))
    wspec = lambda r, off: pl.BlockSpec((r, tn), lambda j, i: (0, j + off))
    if w_up.ndim == 3:
        wu = lambda off: pl.BlockSpec((None, D_MODEL, tn), lambda j, i: (j + off, 0, 0))
    else:
        wu = lambda off: pl.BlockSpec((D_MODEL, tn), lambda j, i: (0, j + off))
    keep_f32 = jax.ShapeDtypeStruct((s, D_FF), F32)
    return pl.pallas_call(
        body, name="up_gate_fwd",
        out_shape=(keep_f32, keep_f32, jax.ShapeDtypeStruct((s, D_FF), MXU_DTYPE), keep_f32, keep_f32, keep_f32),
        grid=(nj, s // tm),
        in_specs=[pl.BlockSpec((tm, D_MODEL), lambda j, i: (i, 0)),
                  pl.BlockSpec((SUBLANES, D_MODEL), lambda j, i: (jnp.maximum(i * hb - 1, 0), 0)),
                  wu(0), wu(nj), wspec(3, 0), wspec(3, nj), wspec(1, 0), wspec(1, nj)],
        out_specs=(main(0),) * 6,
        compiler_params=_params("parallel", "parallel"),
    )(x1, x1, w_up, w_up, conv_w, conv_w, conv_b, conv_b)


def _gate_bwd(u_a, u_g, dz2, w_down_t, a, ge, gd, conv_w, *, tm, tn):
    s = u_a.shape[0]
    nj = D_FF // tn
    ni = s // tm
    hb = tm // SUBLANES

    def body(ua_ref, ug_ref, ha_ref, hg_ref, dz_ref, dzn_ref, wd_ref, a_ref, an_ref, ge_ref, gen_ref, gd_ref, gdn_ref,
             wa_ref, wg_ref, dua_ref, dug_ref, dwa_ref, dwg_ref, dba_ref, dbg_ref):
        i = pl.program_id(1)

        @pl.when(i == 0)
        def _():
            for r in (dwa_ref, dwg_ref, dba_ref, dbg_ref):
                r[...] = jnp.zeros_like(r)

        wa, wg = wa_ref[...], wg_ref[...]
        ua, ug = ua_ref[...], ug_ref[...]
        ha = jnp.where(i > 0, ha_ref[...], 0.0)
        hg = jnp.where(i > 0, hg_ref[...], 0.0)
        sa1, sa2 = _shift_down(ua, ha)
        sg1, sg2 = _shift_down(ug, hg)
        wd = wd_ref[...]
        d = _dot(_mx(dz_ref[...]), wd)
        dya = d * ge_ref[...]
        dyg = d * a_ref[...] * gd_ref[...]
        dn = jnp.where(i < ni - 1, _dot(_mx(dzn_ref[...]), wd), 0.0)
        dya_n = dn * gen_ref[...]
        dyg_n = dn * an_ref[...] * gdn_ref[...]
        da1, da2 = _shift_up(dya, dya_n)
        dg1, dg2 = _shift_up(dyg, dyg_n)
        dua_ref[...] = (wa[2:3, :] * dya + wa[1:2, :] * da1 + wa[0:1, :] * da2).astype(dua_ref.dtype)
        dug_ref[...] = (wg[2:3, :] * dyg + wg[1:2, :] * dg1 + wg[0:1, :] * dg2).astype(dug_ref.dtype)
        ssum = lambda v: jnp.sum(v, axis=0, keepdims=True)
        dwa_ref[...] += jnp.concatenate([ssum(dya * sa2), ssum(dya * sa1), ssum(dya * ua)], axis=0)
        dwg_ref[...] += jnp.concatenate([ssum(dyg * sg2), ssum(dyg * sg1), ssum(dyg * ug)], axis=0)
        dba_ref[...] += ssum(dya)
        dbg_ref[...] += ssum(dyg)

    main = pl.BlockSpec((tm, tn), lambda j, i: (i, j))
    halo = pl.BlockSpec((SUBLANES, tn), lambda j, i: (jnp.maximum(i * hb - 1, 0), j))
    next_row = lambda j, i: jnp.minimum((i + 1) * hb, s // SUBLANES - 1)
    nxt = pl.BlockSpec((SUBLANES, tn), lambda j, i: (next_row(j, i), j))
    wspec = lambda r, off: pl.BlockSpec((r, tn), lambda j, i: (0, j + off))
    return pl.pallas_call(
        body, name="gate_bwd",
        out_shape=(jax.ShapeDtypeStruct((s, D_FF), MXU_DTYPE), jax.ShapeDtypeStruct((s, D_FF), MXU_DTYPE),
                   jax.ShapeDtypeStruct((3, D_FF), F32), jax.ShapeDtypeStruct((3, D_FF), F32),
                   jax.ShapeDtypeStruct((1, D_FF), F32), jax.ShapeDtypeStruct((1, D_FF), F32)),
        grid=(nj, ni),
        in_specs=[main, main, halo, halo,
                  pl.BlockSpec((tm, D_MODEL), lambda j, i: (i, 0)),
                  pl.BlockSpec((SUBLANES, D_MODEL), lambda j, i: (next_row(j, i), 0)),
                  pl.BlockSpec((D_MODEL, tn), lambda j, i: (0, j))]
        + [main, nxt] * 3 + [wspec(3, 0), wspec(3, nj)],
        out_specs=(main, main, wspec(3, 0), wspec(3, 0), wspec(1, 0), wspec(1, 0)),
        compiler_params=_params("parallel", "arbitrary"),
    )(u_a, u_g, u_a, u_g, dz2, dz2, w_down_t, a, a, ge, ge, gd, gd, conv_w, conv_w)


def _prep_weights(w_in, w_uq, w_uk, w_uv, w_o, w_up, w_down):
    return {**_prep_weights_first(w_in, w_uq, w_uk, w_uv), **_prep_weights_late(w_o, w_up, w_down)}


def _prep_weights_late(w_o, w_up, w_down):
    w_o, w_up, w_down = _mx(w_o), _mx(w_up), _mx(w_down)
    w_up_t = w_up.T if w_up.ndim == 2 else w_up.transpose(0, 2, 1).reshape(2 * D_FF, D_MODEL)
    return dict(w_o=w_o, w_o_t=w_o.T, w_up=w_up, w_up_t=w_up_t, w_down=w_down, w_down_t=w_down.T)


def _prep_weights_first(w_in, w_uq, w_uk, w_uv):
    c = lambda a: a.astype(MXU_DTYPE)
    w_in = c(w_in)
    z = lambda w: jnp.zeros((D_MODEL, w), MXU_DTYPE)
    r0 = Q_RANK + KV_RANK
    w_in_ext = jnp.concatenate([w_in[:, :r0], z(NOPE), w_in[:, r0:r0 + ROPE], z(32), w_in[:, r0 + ROPE:]], axis=1)
    wq = jnp.pad(c(w_uq).transpose(1, 0, 2), ((0, 0), (0, 0), (0, QK_PAD - NOPE - ROPE)))
    wk = jnp.pad(c(w_uk).transpose(1, 0, 2), ((0, 0), (0, 0), (0, QK_PAD - NOPE)))
    wv = c(w_uv).transpose(1, 0, 2)
    t3 = lambda a: a.transpose(0, 2, 1)
    return dict(w_in=w_in_ext, w_in_t=w_in_ext.T, wq=wq, wq_t=t3(wq), wk=wk, wk_t=t3(wk), wv=wv, wv_t=t3(wv))


def _local_step(x, target, w, g_cq, g_ckv, ln1_g, ln1_b, conv_w, conv_b, ln2_g, ln2_b, comm=None):
    s = x.shape[0]
    tabs = _rope_tables(s)
    r2 = lambda a: a.reshape(1, -1)
    cb = r2(conv_b)
    dils = [d for _, d in DIL_PAIRS]

    h, qp, kp, vp = _in_proj(x, w["w_in"], tm=256)
    q, k, v, v_t = _mla_prep_fwd(h, r2(g_cq), r2(g_ckv), w["wq"], w["wk"], w["wv"], w["wv_t"], tabs, tm=256)
    if comm is None:
        o_mla_t, lse_mla = _mla_attn_fwd(q, k, v_t, t=512, g=HEADS)
    else:
        o_mla_t, lse_mla, *gathered = _mla_attn_fwd(q, k, v_t, t=512, g=HEADS, late=comm["late"])
        w = {**w, **comm["finish"](gathered)}
    o_bs, lse_bs = [], []
    for i, d in enumerate(dils):
        o_b, l_b = _dil_fwd(qp[i], kp[i], vp[i], d, name=f"dil_fwd_{d}")
        o_bs.append(o_b)
        lse_bs.append(_unperm_row(l_b, d))
    o_dil, lj = _dil_combine(o_bs, lse_bs, ts=512)
    o_mla = o_mla_t.transpose(2, 0, 1).reshape(s, HEADS * HEAD_DIM)
    x1, xhat1, rstd1 = _out_ln1(o_mla, o_dil, w["w_o"], x, r2(ln1_g), r2(ln1_b), tm=256)
    u_a, u_g, act, conv_a, gelu_g, gelu_dg = _up_gate_fwd(x1, w["w_up"], conv_w, cb, tm=256, tn=1408)
    dz2, loss, dg2, db2 = _down_ln2_loss(act, w["w_down"], x1, r2(ln2_g), r2(ln2_b), target, tm=256)

    dw_down = _mm_tn(act, dz2, name="dw_down", tm=1408, tn=D_MODEL, ts=DW_TOKENS)
    du_a, du_g, dcw_a, dcw_g, dcb_a, dcb_g = _gate_bwd(u_a, u_g, dz2, w["w_down_t"], conv_a, gelu_g, gelu_dg, conv_w,
                                                       tm=256, tn=1408)
    dz1, dg1, db1 = _up_bwd_ln1(du_a, du_g, w["w_up_t"], dz2, xhat1, rstd1, r2(ln1_g), tm=256)
    dw_up = jnp.concatenate([_mm_tn(x1, du_a, name="dw_up_a", tm=D_MODEL, tn=1408, ts=DW_TOKENS),
                             _mm_tn(x1, du_g, name="dw_up_g", tm=D_MODEL, tn=1408, ts=DW_TOKENS)], axis=1)
    dw_o = jnp.concatenate([_mm_tn(o_mla, dz1, name="dw_o_mla", tm=512, tn=D_MODEL, ts=DW_TOKENS),
                            _mm_tn(o_dil, dz1, name="dw_o_dil", tm=512, tn=D_MODEL, ts=DW_TOKENS)], axis=0)
    named_early = [("w_up", dw_up), ("w_down", dw_down), ("w_o", dw_o)]
    swap = () if comm is None else comm["blocked"](named_early)
    do_mla, do_dil, dd_all, received = _attn_bwd_heads(dz1, w["w_o_t"], o_mla, o_dil, tm=256, swap=swap)
    dd_all = dd_all.T
    dd_mla, dd_dil = dd_all[:HEADS].reshape(HEADS, 1, s), dd_all[HEADS:].reshape(HEADS, 1, s)
    early = () if comm is None else tuple(comm["add_halves"](named_early, swap, received))
    dq, dk, dv, *early_slots = _mla_attn_bwd(q, k, v, do_mla, lse_mla, dd_mla, t=512, g=4, early=early)
    parts = []
    for i, d in enumerate(dils):
        parts.append(_dil_bwd(qp[i], kp[i], vp[i], do_dil[i], _perm_row(lj, d), _perm_row(dd_dil, d), d, name=f"dil_bwd_{d}"))
    dh_dil = _dil_merge(parts, ts=512)
    dh_mla, dwq, dwk, dwv, dgq, dgkv = _mla_prep_bwd(h, dq, dk, dv, r2(g_cq), r2(g_ckv),
                                                     w["wq_t"], w["wk_t"], w["wv_t"], tabs, tm=256)
    mla_w = 4 * LANES
    w_in_t = w["w_in_t"]
    grad_x = _mm_nn(dh_mla, w_in_t[:mla_w], name="in_bwd_mla", tm=512, tn=D_MODEL, tk=mla_w, add=dz1, add_scale=DN_ALPHA)
    grad_x = _mm_nn(dh_dil, w_in_t[mla_w:], name="in_bwd_dil", tm=512, tn=D_MODEL, tk=3 * HEADS * HEAD_DIM, add=grad_x)
    dw_mla = _mm_tn(x, dh_mla, name="dw_in_mla", tm=D_MODEL, tn=mla_w, ts=DW_TOKENS)
    dw_dil = _mm_tn(x, dh_dil, name="dw_in_dil", tm=D_MODEL, tn=3 * HEADS * HEAD_DIM, ts=DW_TOKENS)
    r0 = Q_RANK + KV_RANK
    grads = dict(
        w_in=jnp.concatenate([dw_mla[:, :r0], dw_mla[:, r0 + NOPE:r0 + NOPE + ROPE], dw_dil], axis=1),
        g_cq=dgq[0], g_ckv=dgkv[0],
        w_uq=dwq[:, :, :NOPE + ROPE].transpose(1, 0, 2),
        w_uk=dwk[:, :, :NOPE].transpose(1, 0, 2),
        w_uv=dwv.transpose(1, 0, 2),
        w_o=dw_o, ln1_g=dg1[0], ln1_b=db1[0], w_up=dw_up,
        conv_w=jnp.concatenate([dcw_a, dcw_g], axis=1), conv_b=jnp.concatenate([dcb_a, dcb_g], axis=1)[0],
        w_down=dw_down, ln2_g=dg2[0], ln2_b=db2[0])
    if comm is not None:
        grads["early"] = (early, tuple(early_slots))
    return loss[0, 0], grad_x, grads


N_CHIPS = 4
SHARDED = ("w_in", "w_uq", "w_o", "w_up", "conv_w", "w_down")
COL_SHARDED = ("w_in", "w_up", "conv_w")
SHARD_SHAPE = dict(w_in=(D_MODEL, IN_WIDTH // 4), w_uq=(Q_RANK // 4, HEADS, NOPE + ROPE), w_o=(D_MODEL // 4, D_MODEL),
                   w_up=(D_MODEL, 2 * D_FF // 4), conv_w=(3, 2 * D_FF // 4), w_down=(D_FF // 4, D_MODEL))
SMALL = ("g_cq", "g_ckv", "w_uk", "w_uv", "ln1_g", "ln1_b", "conv_b", "ln2_g", "ln2_b")
SMALL_SHAPE = dict(g_cq=(Q_RANK,), g_ckv=(KV_RANK,), w_uk=(KV_RANK, HEADS, NOPE), w_uv=(KV_RANK, HEADS, HEAD_DIM),
                   ln1_g=(D_MODEL,), ln1_b=(D_MODEL,), conv_b=(2 * D_FF,), ln2_g=(D_MODEL,), ln2_b=(D_MODEL,))
BIG = ("w_in", "w_uq", "w_o", "w_up", "w_down")
BIG_2D = dict(w_in=(D_MODEL, IN_WIDTH // 4), w_uq=(Q_RANK // 4, HEADS * (NOPE + ROPE)), w_o=(D_MODEL // 4, D_MODEL),
              w_up=(D_MODEL, 2 * D_FF // 4), w_down=(D_FF // 4, D_MODEL))
SMALL_G = SMALL + ("conv_w",)
SMALL_WIDE = ("w_uk", "w_uv")
SMALL_G_SHAPE = {**SMALL_SHAPE, "conv_w": (3, 2 * D_FF)}
SMALL_U_SHAPE = {**SMALL_SHAPE, "conv_w": (3, 2 * D_FF // 4)}


def _size(shape):
    return math.prod(shape)


def _padded_rows(n_elems, mult):
    return -(-n_elems // (LANES * mult)) * mult


SHARD_ROWS = {n: _padded_rows(_size(SHARD_SHAPE[n]), SUBLANES) for n in SHARDED}
R_SMALL = -(-sum(_size(SMALL_G_SHAPE[n]) for n in SMALL_G) // (LANES * LANES)) * LANES
GATHER_FIRST = ("w_in", "w_uq")
GATHER_LATE = ("w_o", "w_up", "w_down")
REDUCED_EARLY = ("w_up", "w_down", "w_o")
REDUCED_LAST = ("w_in", "w_uq")


def _rows(a, rows=None):
    flat = a.reshape(-1)
    rows = -(-flat.shape[0] // LANES) if rows is None else rows
    return jnp.pad(flat, (0, rows * LANES - flat.shape[0])).reshape(rows, LANES)


def _blocked(name, g):
    r, c = BIG_2D[name]
    a = g.reshape(r, N_CHIPS, c).transpose(1, 0, 2) if name in COL_SHARDED else g.reshape(N_CHIPS, r, c)
    return a.reshape(N_CHIPS, 2, r // 2, c)


def _pack_flat(t, names, rows=None):
    flat = jnp.concatenate([t[n].astype(F32).reshape(-1) for n in names])
    return _rows(flat, R_SMALL if rows is None else rows)


def _unpack_flat(buf, names, shapes):
    flat, out, r = buf.reshape(-1), {}, 0
    for n in names:
        out[n] = flat[r:r + _size(shapes[n])].reshape(shapes[n])
        r += _size(shapes[n])
    return out


def _from_chip_blocks(name, blocks):
    shp = SHARD_SHAPE[name]
    a = blocks.reshape(N_CHIPS, -1)[:, :_size(shp)].reshape((N_CHIPS,) + shp)
    if name in COL_SHARDED:
        return a.transpose(1, 0, 2).reshape(shp[0], N_CHIPS * shp[1])
    return a.reshape((N_CHIPS * shp[0],) + shp[1:])


ANY = pl.BlockSpec(memory_space=pl.ANY)
COMM_PARAMS = pltpu.CompilerParams(has_side_effects=True)


def _coords():
    return lax.axis_index("x"), lax.axis_index("y"), lax.axis_index("c")


def _other_chips(x, y):
    return [(1 - x, y), (x, 1 - y), (1 - x, 1 - y)]


def _remote(src, dst, send_sems, recv_sems, k, to):
    return pltpu.make_async_remote_copy(src_ref=src, dst_ref=dst, send_sem=send_sems.at[k], recv_sem=recv_sems.at[k],
                                        device_id=to, device_id_type=MESH)


def _gather_in_steps(wp_refs, wout_refs, send_sems, recv_sems, *, first, mid, last):
    x, y, c = _coords()
    me = 2 * x + y
    sib = (x, y, 1 - c)
    chips = _other_chips(x, y)
    n = len(wp_refs)
    pairs = [(j, t, px, py) for j, (px, py) in enumerate(chips) for t in range(n)]
    ici = [_remote(wp_refs[t].at[c], wout_refs[t].at[me, c], send_sems, recv_sems, j * n + t, (px, py, c))
           for j, t, px, py in pairs]
    fwd = [_remote(wout_refs[t].at[2 * px + py, c], wout_refs[t].at[2 * px + py, c], send_sems, recv_sems, (3 + j) * n + t, sib)
           for j, t, px, py in pairs]

    @pl.when(first)
    def _():
        for cp in ici:
            cp.start()

    @pl.when(mid)
    def _():
        for i, (j, t, px, py) in enumerate(pairs):
            _remote(wp_refs[t].at[c], wout_refs[t].at[2 * px + py, c], send_sems, recv_sems, j * n + t, (px, py, c)).wait_recv()
            fwd[i].start()

    @pl.when(last)
    def _():
        for j, t, px, py in pairs:
            k = 2 * px + py
            _remote(wout_refs[t].at[k, 1 - c], wout_refs[t].at[k, 1 - c], send_sems, recv_sems, (3 + j) * n + t, sib).wait_recv()
        for cp in ici + fwd:
            cp.wait_send()


def _swap_halves_in_steps(gs_refs, os_refs, send_sems, recv_sems, *, first, last):
    x, y, c = _coords()
    sib = (x, y, 1 - c)
    cps = [_remote(gs_refs[t].at[k, 1 - c], os_refs[t].at[k], send_sems, recv_sems, t * N_CHIPS + k, sib)
           for t in range(len(gs_refs)) for k in range(N_CHIPS)]

    @pl.when(first)
    def _():
        for cp in cps:
            cp.start()

    @pl.when(last)
    def _():
        for cp in cps:
            cp.wait_recv()
        for cp in cps:
            cp.wait_send()


def _exchange_in_steps(ps_refs, ss_refs, send_sems, recv_sems, *, first, last):
    x, y, c = _coords()
    me = 2 * x + y
    chips = _other_chips(x, y)
    n = len(ps_refs)
    sends = [_remote(ps_refs[t].at[2 * px + py], ss_refs[t].at[me], send_sems, recv_sems, j * n + t, (px, py, c))
             for j, (px, py) in enumerate(chips) for t in range(n)]

    @pl.when(first)
    def _():
        for cp in sends:
            cp.start()

    @pl.when(last)
    def _():
        for j, (px, py) in enumerate(chips):
            for t in range(n):
                _remote(ps_refs[t].at[me], ss_refs[t].at[2 * px + py], send_sems, recv_sems, j * n + t, (px, py, c)).wait_recv()
        for cp in sends:
            cp.wait_send()


def _gather_weights(wp, cwp):
    def body(wp_ref, cw_ref, wout_ref, cwout_ref, send_sems, recv_sems):
        x, y, c = _coords()
        me = 2 * x + y
        sib = (x, y, 1 - c)
        chips = _other_chips(x, y)
        sends = [_remote(wp_ref.at[c], wout_ref.at[me, c], send_sems, recv_sems, j, (px, py, c))
                 for j, (px, py) in enumerate(chips)]
        sends += [_remote(cw_ref, cwout_ref.at[me], send_sems, recv_sems, 3 + j, (px, py, c))
                  for j, (px, py) in enumerate(chips)]
        for cp in sends:
            cp.start()
        for j, (px, py) in enumerate(chips):
            k = 2 * px + py
            _remote(wp_ref.at[c], wout_ref.at[k, c], send_sems, recv_sems, j, (px, py, c)).wait_recv()
            fwd = _remote(wout_ref.at[k, c], wout_ref.at[k, c], send_sems, recv_sems, 6 + j, sib)
            fwd.start()
            sends.append(fwd)
        for j, (px, py) in enumerate(chips):
            k = 2 * px + py
            _remote(cw_ref, cwout_ref.at[k], send_sems, recv_sems, 3 + j, (px, py, c)).wait_recv()
            _remote(wout_ref.at[k, 1 - c], wout_ref.at[k, 1 - c], send_sems, recv_sems, 6 + j, sib).wait_recv()
        for cp in sends:
            cp.wait_send()

    return pl.pallas_call(
        body, name="gather_weights",
        out_shape=(jax.ShapeDtypeStruct((N_CHIPS,) + wp.shape, wp.dtype), jax.ShapeDtypeStruct((N_CHIPS,) + cwp.shape, cwp.dtype)),
        in_specs=[ANY, ANY], out_specs=(ANY, ANY),
        scratch_shapes=[pltpu.SemaphoreType.DMA((9,)), pltpu.SemaphoreType.DMA((9,))],
        compiler_params=COMM_PARAMS,
    )(wp, cwp)


def _exchange_sibling_halves(gs, whole, *, name):
    n, nw = len(gs), len(whole)

    def body(*refs):
        gs_refs, wh_refs = refs[:n], refs[n:n + nw]
        os_refs, ow_refs = refs[n + nw:2 * n + nw], refs[2 * n + nw:2 * (n + nw)]
        send_sems, recv_sems = refs[2 * (n + nw):]
        x, y, c = _coords()
        sib = (x, y, 1 - c)
        cps = [_remote(gs_refs[t].at[k, 1 - c], os_refs[t].at[k], send_sems, recv_sems, t * N_CHIPS + k, sib)
               for t in range(n) for k in range(N_CHIPS)]
        cps += [_remote(wh_refs[t], ow_refs[t], send_sems, recv_sems, n * N_CHIPS + t, sib) for t in range(nw)]
        for cp in cps:
            cp.start()
        for cp in cps:
            cp.wait_recv()
        for cp in cps:
            cp.wait_send()

    n_sem = n * N_CHIPS + nw
    return pl.pallas_call(
        body, name=name,
        out_shape=tuple(jax.ShapeDtypeStruct((N_CHIPS,) + a.shape[2:], F32) for a in gs)
        + tuple(jax.ShapeDtypeStruct(a.shape, F32) for a in whole),
        in_specs=[ANY] * (n + nw), out_specs=(ANY,) * (n + nw),
        scratch_shapes=[pltpu.SemaphoreType.DMA((n_sem,)), pltpu.SemaphoreType.DMA((n_sem,))],
        compiler_params=COMM_PARAMS,
    )(*gs, *whole)


def _exchange_chips(ps, whole):
    n, nw = len(ps), len(whole)
    per_chip = n + nw

    def body(*refs):
        ps_refs, wh_refs = refs[:n], refs[n:per_chip]
        ss_refs, sw_refs = refs[per_chip:per_chip + n], refs[per_chip + n:2 * per_chip]
        send_sems, recv_sems = refs[2 * per_chip:]
        x, y, c = _coords()
        me = 2 * x + y
        chips = _other_chips(x, y)
        sends = []
        for j, (px, py) in enumerate(chips):
            to = (px, py, c)
            for t in range(n):
                sends.append(_remote(ps_refs[t].at[2 * px + py], ss_refs[t].at[me], send_sems, recv_sems, j * per_chip + t, to))
            for t in range(nw):
                sends.append(_remote(wh_refs[t], sw_refs[t].at[me], send_sems, recv_sems, j * per_chip + n + t, to))
        for cp in sends:
            cp.start()
        for j, (px, py) in enumerate(chips):
            k, to = 2 * px + py, (px, py, c)
            for t in range(n):
                _remote(ps_refs[t].at[me], ss_refs[t].at[k], send_sems, recv_sems, j * per_chip + t, to).wait_recv()
            for t in range(nw):
                _remote(wh_refs[t], sw_refs[t].at[k], send_sems, recv_sems, j * per_chip + n + t, to).wait_recv()
        for cp in sends:
            cp.wait_send()

    n_sem = 3 * per_chip
    return pl.pallas_call(
        body, name="exchange_chips",
        out_shape=tuple(jax.ShapeDtypeStruct(a.shape, a.dtype) for a in ps)
        + tuple(jax.ShapeDtypeStruct((N_CHIPS,) + a.shape, a.dtype) for a in whole),
        in_specs=[ANY] * per_chip, out_specs=(ANY,) * per_chip,
        scratch_shapes=[pltpu.SemaphoreType.DMA((n_sem,)), pltpu.SemaphoreType.DMA((n_sem,))],
        compiler_params=COMM_PARAMS,
    )(*ps, *whole)


def _exchange_sibling_result(gh):
    n = len(gh)

    def body(*refs):
        gh_refs, out_refs, (send_sems, recv_sems) = refs[:n], refs[n:2 * n], refs[2 * n:]
        x, y, c = _coords()
        cps = [_remote(gh_refs[t], out_refs[t], send_sems, recv_sems, t, (x, y, 1 - c)) for t in range(n)]
        for cp in cps:
            cp.start()
        for cp in cps:
            cp.wait_recv()
        for cp in cps:
            cp.wait_send()

    return pl.pallas_call(
        body, name="exchange_sibling_result",
        out_shape=tuple(jax.ShapeDtypeStruct(a.shape, F32) for a in gh),
        in_specs=[ANY] * n, out_specs=(ANY,) * n,
        scratch_shapes=[pltpu.SemaphoreType.DMA((n,)), pltpu.SemaphoreType.DMA((n,))],
        compiler_params=COMM_PARAMS,
    )(*gh)


def _add_own_half(gs, recv, c_arr, *, name):
    _, rows, cols = recv.shape

    def body(c_ref, a_ref, b_ref, o_ref):
        o_ref[0] = (a_ref[0, 0] + b_ref[0]).astype(o_ref.dtype)

    return pl.pallas_call(
        body, name=name,
        out_shape=jax.ShapeDtypeStruct(recv.shape, GRAD_WIRE_DTYPE),
        grid_spec=pltpu.PrefetchScalarGridSpec(
            num_scalar_prefetch=1, grid=(N_CHIPS,),
            in_specs=[pl.BlockSpec((1, 1, rows, cols), lambda k, c_ref: (k, c_ref[0], 0, 0)),
                      pl.BlockSpec((1, rows, cols), lambda k, c_ref: (k, 0, 0))],
            out_specs=pl.BlockSpec((1, rows, cols), lambda k, c_ref: (k, 0, 0))),
        compiler_params=_params("parallel"),
    )(c_arr, gs, recv)


def _add2(a, b, *, name, out_dtype=F32):
    def body(a_ref, b_ref, o_ref):
        o_ref[...] = (a_ref[...] + b_ref[...]).astype(o_ref.dtype)

    return pl.pallas_call(body, name=name, out_shape=jax.ShapeDtypeStruct(a.shape, out_dtype))(a, b)


def _sum_slots(slots, *, tr, name):
    _, r, c = slots.shape

    def body(s_ref, o_ref):
        f = lambda k: s_ref[k].astype(F32)
        o_ref[...] = ((f(0) + f(1)) + f(2)) + f(3)

    return pl.pallas_call(
        body, name=name,
        out_shape=jax.ShapeDtypeStruct((r, c), F32),
        grid=(r // tr,),
        in_specs=[pl.BlockSpec((N_CHIPS, tr, c), lambda i: (0, i, 0))],
        out_specs=pl.BlockSpec((tr, c), lambda i: (i, 0)),
        compiler_params=_params("parallel"),
    )(slots)


def _adamw(w, g, m, v, *, tr, name):
    r, cols = w.shape

    def body(w_ref, g_ref, m_ref, v_ref, d_ref, nm_ref, nv_ref):
        g_ = g_ref[...]
        m_ = ADAM_B1 * m_ref[...] + (1.0 - ADAM_B1) * g_
        v_ = ADAM_B2 * v_ref[...] + (1.0 - ADAM_B2) * (g_ * g_)
        m_hat = m_ / (1.0 - ADAM_B1 ** ADAM_STEP)
        v_hat = v_ / (1.0 - ADAM_B2 ** ADAM_STEP)
        d_ref[...] = -ADAM_LR * (m_hat / (jnp.sqrt(v_hat) + ADAM_EPS) + ADAM_WD * w_ref[...])
        nm_ref[...] = m_
        nv_ref[...] = v_

    spec = pl.BlockSpec((tr, cols), lambda i: (i, 0))
    out = jax.ShapeDtypeStruct((r, cols), F32)
    return pl.pallas_call(
        body, name=name, out_shape=(out, out, out), grid=(r // tr,),
        in_specs=[spec] * 4, out_specs=(spec,) * 3,
        compiler_params=_params("parallel"),
    )(w, g, m, v)


WEIGHTS = ("w_in", "g_cq", "g_ckv", "w_uq", "w_uk", "w_uv", "w_o", "ln1_g", "ln1_b", "w_up", "conv_w", "conv_b",
           "w_down", "ln2_g", "ln2_b")


def kernel(x, w_in, g_cq, g_ckv, w_uq, w_uk, w_uv, w_o, ln1_g, ln1_b, w_up, conv_w, conv_b, w_down, ln2_g, ln2_b, loss_target, m_w_in, m_g_cq, m_g_ckv, m_w_uq, m_w_uk, m_w_uv, m_w_o, m_ln1_g, m_ln1_b, m_w_up, m_conv_w, m_conv_b, m_w_down, m_ln2_g, m_ln2_b, v_w_in, v_g_cq, v_g_ckv, v_w_uq, v_w_uk, v_w_uv, v_w_o, v_ln1_g, v_ln1_b, v_w_up, v_conv_w, v_conv_b, v_w_down, v_ln2_g, v_ln2_b):
    wts = dict(zip(WEIGHTS, (w_in, g_cq, g_ckv, w_uq, w_uk, w_uv, w_o, ln1_g, ln1_b, w_up, conv_w, conv_b, w_down, ln2_g, ln2_b)))
    mom = dict(zip(WEIGHTS, (m_w_in, m_g_cq, m_g_ckv, m_w_uq, m_w_uk, m_w_uv, m_w_o, m_ln1_g, m_ln1_b, m_w_up, m_conv_w, m_conv_b, m_w_down, m_ln2_g, m_ln2_b)))
    var = dict(zip(WEIGHTS, (v_w_in, v_g_cq, v_g_ckv, v_w_uq, v_w_uk, v_w_uv, v_w_o, v_ln1_g, v_ln1_b, v_w_up, v_conv_w, v_conv_b, v_w_down, v_ln2_g, v_ln2_b)))

    me = 2 * lax.axis_index("x") + lax.axis_index("y")
    my_c = lax.axis_index("c")
    c_arr = my_c.astype(jnp.int32).reshape(1)
    own = lambda slots, mine: lax.dynamic_update_index_in_dim(slots, mine, me, 0)

    def pack(names):
        return jnp.concatenate([_rows(_mx(wts[n]), SHARD_ROWS[n]) for n in names], axis=0).reshape(2, -1, LANES)

    def unpack(names, gathered, mine):
        buf, full, r = own(gathered, mine).reshape(N_CHIPS, -1, LANES), {}, 0
        for n in names:
            full[n] = _from_chip_blocks(n, buf[:, r:r + SHARD_ROWS[n]])
            r += SHARD_ROWS[n]
        return full

    wp_first = pack(GATHER_FIRST)
    cwp = _rows(conv_w, SHARD_ROWS["conv_w"])
    gathered, cwfull = _gather_weights(wp_first, cwp)
    full = unpack(GATHER_FIRST, gathered, wp_first)
    conv_w_full = _from_chip_blocks("conv_w", own(cwfull, cwp))
    w = _prep_weights_first(full["w_in"], full["w_uq"], w_uk, w_uv)
    late_halves = [_mx(wts[n]).reshape(2, BIG_2D[n][0] // 2, BIG_2D[n][1]) for n in GATHER_LATE]

    def finish(gathered_late):
        w_o_b, w_up_b, w_down_b = (own(a, mine).reshape((N_CHIPS,) + BIG_2D[n])
                                   for a, mine, n in zip(gathered_late, late_halves, GATHER_LATE))
        return _prep_weights_late(w_o_b.reshape(D_MODEL, D_MODEL), w_up_b, w_down_b.reshape(D_FF, D_MODEL))

    def blocked(named):
        return [_blocked(n, a) for n, a in named]

    def add_halves(named, gb, recv):
        return [_add_own_half(gb[i], recv[i], c_arr, name=f"add_half_{n}") for i, (n, _) in enumerate(named)]

    def halve(named, whole, wire):
        gb = blocked(named)
        recv = _exchange_sibling_halves(gb, list(whole), name="exchange_sibling_halves")
        return add_halves(named, gb, recv) + [_add2(a, recv[len(gb) + i], name=f"add_whole_{i}", out_dtype=wire[i])
                                              for i, a in enumerate(whole)]

    comm = dict(late=late_halves, finish=finish, blocked=blocked, add_halves=add_halves)
    loss, grad_x, g = _local_step(x[0], loss_target[0], w, g_cq, g_ckv, ln1_g, ln1_b, conv_w_full, conv_b, ln2_g, ln2_b, comm=comm)

    ps_early, slots_early = g.pop("early")
    g["loss"] = loss.reshape(1)
    narrow = tuple(n for n in SMALL_G if n not in SMALL_WIDE) + ("loss",)
    narrow_shape = {**SMALL_G_SHAPE, "loss": (1,)}
    r_narrow = _padded_rows(sum(_size(narrow_shape[n]) for n in narrow), SUBLANES)
    r_wide = _padded_rows(sum(_size(SMALL_G_SHAPE[n]) for n in SMALL_WIDE), 2 * SUBLANES)
    *ps_rest, pr, pw = halve([(n, g[n]) for n in REDUCED_LAST],
                             whole=[_pack_flat(g, narrow, r_narrow), _pack_flat(g, SMALL_WIDE, r_wide)],
                             wire=[F32, GRAD_WIRE_DTYPE])
    *slots_rest, slots_r, slots_w = _exchange_chips(ps_rest, [pr, pw])
    ps = {**dict(zip(REDUCED_LAST, ps_rest)), **dict(zip(REDUCED_EARLY, ps_early))}
    slots = {**dict(zip(REDUCED_LAST, slots_rest)), **dict(zip(REDUCED_EARLY, slots_early))}
    slots = [own(slots[n], lax.dynamic_index_in_dim(ps[n], me, 0, keepdims=False)) for n in BIG]
    g_half = [_sum_slots(slots[i], tr=slots[i].shape[1] // 2, name=f"sum_chips_{n}") for i, n in enumerate(BIG)]
    g_small = {**_unpack_flat(_sum_slots(own(slots_r, pr), tr=r_narrow, name="sum_chips_narrow"), narrow, narrow_shape),
               **_unpack_flat(_sum_slots(own(slots_w, pw), tr=r_wide, name="sum_chips_wide"), SMALL_WIDE, SMALL_G_SHAPE)}
    loss = g_small.pop("loss")[0]
    g_other = _exchange_sibling_result(g_half)
    grads = {n: jnp.where(my_c == 0, jnp.concatenate([g_half[i], g_other[i]]), jnp.concatenate([g_other[i], g_half[i]]))
             for i, n in enumerate(BIG)}
    g_small["conv_w"] = lax.dynamic_slice_in_dim(g_small["conv_w"], me * SHARD_SHAPE["conv_w"][1], SHARD_SHAPE["conv_w"][1], 1)
    grads.update(g_small)

    res = {}
    for n in BIG:
        as2d = lambda a: a.reshape(BIG_2D[n])
        d, m, v = _adamw(as2d(wts[n]), grads[n], as2d(mom[n]), as2d(var[n]), tr=BIG_2D[n][0] // 4, name=f"adamw_{n}")
        res[n] = [a.reshape(SHARD_SHAPE[n]) for a in (grads[n], d, m, v)]
    flat = lambda t: _pack_flat(t, SMALL_G)
    dmv = _adamw(flat(wts), flat(g_small), flat(mom), flat(var), tr=R_SMALL, name="adamw_small")
    dmv = [_unpack_flat(a, SMALL_G, SMALL_U_SHAPE) for a in dmv]
    for n in SMALL_G:
        res[n] = [g_small[n]] + [t[n] for t in dmv]
    outs = [res[n][j] for j in range(4) for n in WEIGHTS]
    return (loss, grad_x[None], *outs)
```

```python
import math

import jax
import jax.numpy as jnp
from jax import lax
from jax.experimental import pallas as pl
from jax.experimental.pallas import tpu as pltpu

F32 = jnp.float32
MXU_DTYPE = jnp.bfloat16
GRAD_WIRE_DTYPE = jnp.bfloat16
NEG = -1e30

D_MODEL = 1024
HEADS = 8
HEAD_DIM = 64
Q_RANK = 256
KV_RANK = 128
NOPE = 64
ROPE = 32
QK_PAD = 128
IN_WIDTH = 1952
IN_EXT = 2048
D_FF = 2816
DIL_PAIRS = ((128, 1), (512, 4), (2048, 16))
DIL_BLOCK = 128
ROPE_THETA = 10000.0
DN_ALPHA = 2.0 ** 0.25
LN_EPS = 1e-5
RMS_EPS = 1e-6
MLA_SCALE = 1.0 / math.sqrt(NOPE + ROPE)
LOG2_E = math.log2(math.e)
DIL_SCALE = 1.0 / math.sqrt(HEAD_DIM)

ADAM_LR = 0.001
ADAM_B1 = 0.9
ADAM_B2 = 0.999
ADAM_EPS = 1e-08
ADAM_WD = 0.01
ADAM_STEP = 10

LANES = 128
SUBLANES = 8
VMEM_LIMIT_BYTES = 56 * 1024 * 1024
DW_TOKENS = 2048

MESH = pl.DeviceIdType.MESH


def _params(*sem):
    return pltpu.CompilerParams(dimension_semantics=sem, vmem_limit_bytes=VMEM_LIMIT_BYTES)


def _dot(a, b):
    return jnp.dot(a, b, preferred_element_type=F32)


def _dot_nt(a, b):
    return lax.dot_general(a, b, (((1,), (1,)), ((), ())), preferred_element_type=F32)


def _dot_tn(a, b):
    return lax.dot_general(a, b, (((0,), (0,)), ((), ())), preferred_element_type=F32)


def _mx(a):
    return a.astype(MXU_DTYPE)


def _mm_nn(a, b, *, name, tm, tn, tk, out_dtype=F32, add=None, add_scale=1.0):
    m, kdim = a.shape
    blocked = b.ndim == 3
    n = b.shape[0] * b.shape[2] if blocked else b.shape[1]
    nk = kdim // tk

    def body(*refs):
        if add is None:
            a_ref, b_ref, o_ref, acc = refs
        else:
            a_ref, b_ref, c_ref, o_ref, acc = refs
        k = pl.program_id(2)

        @pl.when(k == 0)
        def _():
            acc[...] = jnp.zeros_like(acc)

        acc[...] += _dot(_mx(a_ref[...]), _mx(b_ref[...]))

        @pl.when(k == nk - 1)
        def _():
            r = acc[...]
            if add is not None:
                r = r + add_scale * c_ref[...]
            o_ref[...] = r.astype(out_dtype)

    b_spec = (pl.BlockSpec((None, tk, tn), lambda i, j, k: (j, k, 0)) if blocked
              else pl.BlockSpec((tk, tn), lambda i, j, k: (k, j)))
    in_specs = [pl.BlockSpec((tm, tk), lambda i, j, k: (i, k)), b_spec]
    args = [a, b]
    if add is not None:
        in_specs.append(pl.BlockSpec((tm, tn), lambda i, j, k: (i, j)))
        args.append(add)
    return pl.pallas_call(
        body, name=name,
        out_shape=jax.ShapeDtypeStruct((m, n), out_dtype),
        grid=(m // tm, n // tn, nk),
        in_specs=in_specs,
        out_specs=pl.BlockSpec((tm, tn), lambda i, j, k: (i, j)),
        scratch_shapes=[pltpu.VMEM((tm, tn), F32)],
        compiler_params=_params("parallel", "parallel", "arbitrary"),
    )(*args)


def _mm_tn(a, b, *, name, tm, tn, ts, out_dtype=F32):
    s, m = a.shape
    n = b.shape[1]
    ns = s // ts

    def body(a_ref, b_ref, o_ref, acc):
        k = pl.program_id(2)

        @pl.when(k == 0)
        def _():
            acc[...] = jnp.zeros_like(acc)

        acc[...] += _dot_tn(_mx(a_ref[...]), _mx(b_ref[...]))

        @pl.when(k == ns - 1)
        def _():
            o_ref[...] = acc[...].astype(out_dtype)

    return pl.pallas_call(
        body, name=name,
        out_shape=jax.ShapeDtypeStruct((m, n), out_dtype),
        grid=(m // tm, n // tn, ns),
        in_specs=[pl.BlockSpec((ts, tm), lambda i, j, k: (k, i)),
                  pl.BlockSpec((ts, tn), lambda i, j, k: (k, j))],
        out_specs=pl.BlockSpec((tm, tn), lambda i, j, k: (i, j)),
        scratch_shapes=[pltpu.VMEM((tm, tn), F32)],
        compiler_params=_params("parallel", "parallel", "arbitrary"),
    )(a, b)


def _in_proj(x, w_in_ext, *, tm):
    s = x.shape[0]
    mla_w = 4 * LANES
    dil_w = HEADS * HEAD_DIM
    dils = [d for _, d in DIL_PAIRS]

    def body(x_ref, w_ref, h_ref, *rest):
        outs, sc = rest[:-1], rest[-1]
        xb = _mx(x_ref[...])
        h_ref[...] = _dot(xb, w_ref[:, 0:mla_w])
        for j in range(3):
            part = _dot(xb, w_ref[:, mla_w + j * dil_w:mla_w + (j + 1) * dil_w])
            for hd in range(HEADS):
                sc[hd] = part[:, hd * HEAD_DIM:(hd + 1) * HEAD_DIM]
            for b, d in enumerate(dils):
                _store_residue_major(outs[3 * j + b], sc, d, tm)

    shapes, specs = _residue_major_outs(s, tm, dils, MXU_DTYPE)
    res = pl.pallas_call(
        body, name="in_proj",
        out_shape=(jax.ShapeDtypeStruct((s, mla_w), F32),) + shapes * 3,
        grid=(s // tm,),
        in_specs=[pl.BlockSpec((tm, D_MODEL), lambda i: (i, 0)), pl.BlockSpec((D_MODEL, IN_EXT), lambda i: (0, 0))],
        out_specs=(pl.BlockSpec((tm, mla_w), lambda i: (i, 0)),) + specs * 3,
        scratch_shapes=[pltpu.VMEM((HEADS, tm, HEAD_DIM), F32)],
        compiler_params=_params("parallel"),
    )(x, w_in_ext)
    hm = lambda a: a.reshape(HEADS, s, HEAD_DIM)
    return res[0], [hm(a) for a in res[1:4]], [hm(a) for a in res[4:7]], [hm(a) for a in res[7:10]]


def _residue_major_outs(s, tm, dils, dtype):
    shapes, specs = [], []
    for d in dils:
        if d == 1:
            shapes.append(jax.ShapeDtypeStruct((HEADS, s, HEAD_DIM), dtype))
            specs.append(pl.BlockSpec((HEADS, tm, HEAD_DIM), lambda i: (0, i, 0)))
        else:
            shapes.append(jax.ShapeDtypeStruct((HEADS, d, s // d, HEAD_DIM), dtype))
            specs.append(pl.BlockSpec((HEADS, d, tm // d, HEAD_DIM), lambda i: (0, 0, i, 0)))
    return tuple(shapes), tuple(specs)


def _store_residue_major(o_ref, src_ref, d, tm):
    if d == 1:
        o_ref[...] = src_ref[...].astype(o_ref.dtype)
    else:
        for r in range(d):
            o_ref[:, r] = src_ref[:, pl.ds(r, tm // d, stride=d), :].astype(o_ref.dtype)


def _load_token_order(dst_ref, src_ref, d, tm, accumulate=False):
    if d == 1:
        dst_ref[...] = dst_ref[...] + src_ref[...] if accumulate else src_ref[...]
    else:
        for r in range(d):
            rows = pl.ds(r, tm // d, stride=d)
            dst_ref[:, rows, :] = dst_ref[:, rows, :] + src_ref[:, r] if accumulate else src_ref[:, r]


def _attn_bwd_heads(dz1, w_o_t, a_mla, a_dil, *, tm, swap=()):
    s = dz1.shape[0]
    half = HEADS * HEAD_DIM
    dils = [d for _, d in DIL_PAIRS]
    nsw = len(swap)
    n_steps = s // tm

    def body(*refs):
        dz_ref, w_ref, am_ref, ad_ref = refs[:4]
        gs_refs = refs[4:4 + nsw]
        dom_ref, dd_ref = refs[4 + nsw:6 + nsw]
        dod_refs = refs[6 + nsw:6 + nsw + len(dils)]
        os_refs = refs[6 + nsw + len(dils):6 + 2 * nsw + len(dils)]
        if nsw:
            send_sems, recv_sems = refs[6 + 2 * nsw + len(dils):]
            i = pl.program_id(0)
            _swap_halves_in_steps(gs_refs, os_refs, send_sems, recv_sems, first=i == 0, last=i == n_steps - 1)
        dzb = _mx(dz_ref[...])
        for j, (a_ref, o_ref) in enumerate(((am_ref, dom_ref), (ad_ref, dod_refs[0]))):
            da = _dot(dzb, w_ref[:, j * half:(j + 1) * half])
            prod = da * a_ref[...]
            for hd in range(HEADS):
                sl = slice(hd * HEAD_DIM, (hd + 1) * HEAD_DIM)
                o_ref[hd] = da[:, sl].astype(o_ref.dtype)
                dd_ref[:, j * HEADS + hd:j * HEADS + hd + 1] = jnp.sum(prod[:, sl], axis=-1, keepdims=True)
        for b, d in enumerate(dils[1:]):
            _store_residue_major(dod_refs[1 + b], dod_refs[0], d, tm)

    hspec = pl.BlockSpec((HEADS, tm, HEAD_DIM), lambda i: (0, i, 0))
    row = lambda w: pl.BlockSpec((tm, w), lambda i: (i, 0))
    shapes, specs = _residue_major_outs(s, tm, dils, F32)
    n_sem = nsw * N_CHIPS
    do_mla, dd, *rest = pl.pallas_call(
        body, name="attn_bwd_heads",
        out_shape=(jax.ShapeDtypeStruct((HEADS, s, HEAD_DIM), MXU_DTYPE), jax.ShapeDtypeStruct((s, 2 * HEADS), F32)) + shapes
        + tuple(jax.ShapeDtypeStruct((N_CHIPS,) + a.shape[2:], F32) for a in swap),
        grid=(n_steps,),
        in_specs=[row(D_MODEL), pl.BlockSpec((D_MODEL, D_MODEL), lambda i: (0, 0)), row(half), row(half)] + [ANY] * nsw,
        out_specs=(hspec, row(2 * HEADS)) + specs + (ANY,) * nsw,
        scratch_shapes=[pltpu.SemaphoreType.DMA((n_sem,)), pltpu.SemaphoreType.DMA((n_sem,))] if nsw else [],
        compiler_params=pltpu.CompilerParams(dimension_semantics=("arbitrary",), vmem_limit_bytes=VMEM_LIMIT_BYTES,
                                             has_side_effects=nsw > 0),
    )(dz1, w_o_t, a_mla, a_dil, *swap)
    do_dil, received = rest[:len(dils)], rest[len(dils):]
    return do_mla, [a.reshape(HEADS, s, HEAD_DIM) for a in do_dil], dd, received


def _dil_merge(parts, *, ts):
    hds, s, e = parts[0][0].shape
    dils = [d for _, d in DIL_PAIRS]

    def body(*refs):
        o_ref, sc = refs[9], refs[10]
        for j in range(3):
            for b, d in enumerate(dils):
                _load_token_order(sc, refs[3 * b + j], d, ts, accumulate=b > 0)
            tot = sc[...]
            for hd in range(hds):
                col = j * hds * e + hd * e
                o_ref[:, col:col + e] = tot[hd].astype(o_ref.dtype)

    _, specs = _residue_major_outs(s, ts, dils, F32)
    view = lambda a, d: a if d == 1 else a.reshape(hds, d, s // d, e)
    return pl.pallas_call(
        body, name="dil_merge",
        out_shape=jax.ShapeDtypeStruct((s, 3 * hds * e), MXU_DTYPE),
        grid=(s // ts,),
        in_specs=[specs[b] for b in range(3) for _ in range(3)],
        out_specs=pl.BlockSpec((ts, 3 * hds * e), lambda i: (i, 0)),
        scratch_shapes=[pltpu.VMEM((hds, ts, e), F32)],
        compiler_params=_params("parallel"),
    )(*[view(parts[b][j], dils[b]) for b in range(3) for j in range(3)])


def _rope_tables(s):
    half = ROPE // 2
    freqs = ROPE_THETA ** (-jnp.arange(half, dtype=F32) / half)
    ang = jnp.arange(s).astype(F32)[:, None] * freqs[None, :]
    cos, sin = jnp.cos(ang), jnp.sin(ang)
    z = lambda w: jnp.zeros((s, w), F32)
    c = jnp.concatenate([jnp.ones((s, NOPE), F32), cos, cos, z(32)], axis=1)
    s1 = jnp.concatenate([z(NOPE + half), sin, z(32)], axis=1)
    s2 = jnp.concatenate([z(NOPE), -sin, z(half + 32)], axis=1)
    mask = jnp.concatenate([z(NOPE), jnp.ones((s, ROPE), F32), z(32)], axis=1)
    return c, s1, s2, mask


def _rope(x, c, s1, s2):
    return x * c + pltpu.roll(x, 16, 1) * s1 + pltpu.roll(x, LANES - 16, 1) * s2


def _unrope(dy, c, s1, s2):
    return dy * c + pltpu.roll(dy * s1, LANES - 16, 1) + pltpu.roll(dy * s2, 16, 1)


def _rms(x):
    r = lax.rsqrt(jnp.mean(x * x, axis=-1, keepdims=True) + RMS_EPS)
    return x * r, r


def _mla_prep_fwd(h, g_cq, g_ckv, wq, wk, wv, wv_t, tabs, *, tm):
    s = h.shape[0]
    c_t, s1_t, s2_t, _ = tabs

    def body(h_ref, gq_ref, gkv_ref, wq_ref, wk_ref, wv_ref, wvt_ref, c_ref, s1_ref, s2_ref,
             q_ref, k_ref, v_ref, vt_ref):
        cq = h_ref[:, 0:Q_RANK]
        ckv = h_ref[:, Q_RANK:Q_RANK + KV_RANK]
        kr = h_ref[:, Q_RANK + KV_RANK:Q_RANK + KV_RANK + QK_PAD]
        c, s1, s2 = c_ref[...], s1_ref[...], s2_ref[...]
        cqn = _mx(_rms(cq)[0] * gq_ref[...])
        ckvn = _mx(_rms(ckv)[0] * gkv_ref[...])
        kr_rot = _rope(kr, c, s1, s2)
        for hd in range(HEADS):
            q_ref[hd] = _rope(_dot(cqn, wq_ref[hd]), c, s1, s2).astype(q_ref.dtype)
            k_ref[hd] = (_dot(ckvn, wk_ref[hd]) + kr_rot).astype(k_ref.dtype)
            v_ref[hd] = _dot(ckvn, wv_ref[hd]).astype(v_ref.dtype)
            vt_ref[hd] = _dot_nt(wvt_ref[hd], ckvn).astype(vt_ref.dtype)

    full = lambda shp: pl.BlockSpec(shp, lambda i: (0,) * len(shp))
    row = lambda w: pl.BlockSpec((tm, w), lambda i: (i, 0))
    return pl.pallas_call(
        body, name="mla_prep_fwd",
        out_shape=(jax.ShapeDtypeStruct((HEADS, s, QK_PAD), MXU_DTYPE),
                   jax.ShapeDtypeStruct((HEADS, s, QK_PAD), MXU_DTYPE),
                   jax.ShapeDtypeStruct((HEADS, s, HEAD_DIM), MXU_DTYPE),
                   jax.ShapeDtypeStruct((HEADS, HEAD_DIM, s), MXU_DTYPE)),
        grid=(s // tm,),
        in_specs=[row(4 * LANES), full((1, Q_RANK)), full((1, KV_RANK)),
                  full((HEADS, Q_RANK, QK_PAD)), full((HEADS, KV_RANK, QK_PAD)), full((HEADS, KV_RANK, HEAD_DIM)),
                  full((HEADS, HEAD_DIM, KV_RANK)), row(LANES), row(LANES), row(LANES)],
        out_specs=(pl.BlockSpec((HEADS, tm, QK_PAD), lambda i: (0, i, 0)),
                   pl.BlockSpec((HEADS, tm, QK_PAD), lambda i: (0, i, 0)),
                   pl.BlockSpec((HEADS, tm, HEAD_DIM), lambda i: (0, i, 0)),
                   pl.BlockSpec((HEADS, HEAD_DIM, tm), lambda i: (0, 0, i))),
        compiler_params=_params("parallel"),
    )(h, g_cq, g_ckv, wq, wk, wv, wv_t, c_t, s1_t, s2_t)


def _mla_prep_bwd(h, dq, dk, dv, g_cq, g_ckv, wq_t, wk_t, wv_t, tabs, *, tm):
    s = h.shape[0]
    c_t, s1_t, s2_t, mask_t = tabs

    def body(h_ref, dq_ref, dk_ref, dv_ref, gq_ref, gkv_ref, wqt_ref, wkt_ref, wvt_ref,
             c_ref, s1_ref, s2_ref, mask_ref, dh_ref, dwq_ref, dwk_ref, dwv_ref, dgq_ref, dgkv_ref):
        i = pl.program_id(0)

        @pl.when(i == 0)
        def _():
            dwq_ref[...] = jnp.zeros_like(dwq_ref)
            dwk_ref[...] = jnp.zeros_like(dwk_ref)
            dwv_ref[...] = jnp.zeros_like(dwv_ref)
            dgq_ref[...] = jnp.zeros_like(dgq_ref)
            dgkv_ref[...] = jnp.zeros_like(dgkv_ref)

        cq = h_ref[:, 0:Q_RANK]
        ckv = h_ref[:, Q_RANK:Q_RANK + KV_RANK]
        c, s1, s2 = c_ref[...], s1_ref[...], s2_ref[...]
        cqh, rq = _rms(cq)
        ckvh, rkv = _rms(ckv)
        gq, gkv = gq_ref[...], gkv_ref[...]
        cqn = _mx(cqh * gq)
        ckvn = _mx(ckvh * gkv)
        dcqn = jnp.zeros((tm, Q_RANK), F32)
        dckvn = jnp.zeros((tm, KV_RANK), F32)
        dkr = jnp.zeros((tm, QK_PAD), F32)
        for hd in range(HEADS):
            dqh = _mx(_unrope(dq_ref[hd], c, s1, s2))
            dcqn = dcqn + _dot(dqh, wqt_ref[hd])
            dwq_ref[hd] += _dot_tn(cqn, dqh)
            dkh = dk_ref[hd]
            dkr = dkr + dkh
            dkh = _mx(dkh)
            dckvn = dckvn + _dot(dkh, wkt_ref[hd])
            dwk_ref[hd] += _dot_tn(ckvn, dkh)
            dvh = _mx(dv_ref[hd])
            dckvn = dckvn + _dot(dvh, wvt_ref[hd])
            dwv_ref[hd] += _dot_tn(ckvn, dvh)
        dgq_ref[...] += jnp.sum(dcqn * cqh, axis=0, keepdims=True)
        dgkv_ref[...] += jnp.sum(dckvn * ckvh, axis=0, keepdims=True)
        gd = dcqn * gq
        dh_ref[:, 0:Q_RANK] = rq * (gd - cqh * jnp.mean(gd * cqh, axis=-1, keepdims=True))
        gd = dckvn * gkv
        dh_ref[:, Q_RANK:Q_RANK + KV_RANK] = rkv * (gd - ckvh * jnp.mean(gd * ckvh, axis=-1, keepdims=True))
        dh_ref[:, Q_RANK + KV_RANK:Q_RANK + KV_RANK + QK_PAD] = _unrope(dkr, c, s1, s2) * mask_ref[...]

    full = lambda shp: pl.BlockSpec(shp, lambda i: (0,) * len(shp))
    row = lambda w: pl.BlockSpec((tm, w), lambda i: (i, 0))
    hrow = lambda w: pl.BlockSpec((HEADS, tm, w), lambda i: (0, i, 0))
    return pl.pallas_call(
        body, name="mla_prep_bwd",
        out_shape=(jax.ShapeDtypeStruct((s, 4 * LANES), F32),
                   jax.ShapeDtypeStruct((HEADS, Q_RANK, QK_PAD), F32),
                   jax.ShapeDtypeStruct((HEADS, KV_RANK, QK_PAD), F32),
                   jax.ShapeDtypeStruct((HEADS, KV_RANK, HEAD_DIM), F32),
                   jax.ShapeDtypeStruct((1, Q_RANK), F32),
                   jax.ShapeDtypeStruct((1, KV_RANK), F32)),
        grid=(s // tm,),
        in_specs=[row(4 * LANES), hrow(QK_PAD), hrow(QK_PAD), hrow(HEAD_DIM),
                  full((1, Q_RANK)), full((1, KV_RANK)),
                  full((HEADS, QK_PAD, Q_RANK)), full((HEADS, QK_PAD, KV_RANK)), full((HEADS, HEAD_DIM, KV_RANK)),
                  row(LANES), row(LANES), row(LANES), row(LANES)],
        out_specs=(row(4 * LANES), full((HEADS, Q_RANK, QK_PAD)), full((HEADS, KV_RANK, QK_PAD)),
                   full((HEADS, KV_RANK, HEAD_DIM)), full((1, Q_RANK)), full((1, KV_RANK))),
        compiler_params=_params("arbitrary"),
    )(h, dq, dk, dv, g_cq, g_ckv, wq_t, wk_t, wv_t, c_t, s1_t, s2_t, mask_t)


def _bdot(a, b, ca, cb):
    return lax.dot_general(a, b, (((ca,), (cb,)), ((0,), (0,))), preferred_element_type=F32)


def _causal_mask_t(t):
    kk = lax.broadcasted_iota(jnp.int32, (t, t), 0)
    qq = lax.broadcasted_iota(jnp.int32, (t, t), 1)
    return (qq >= kk)[None]


def _mla_attn_fwd(q, k, v_t, *, t, g, late=None):
    hds, s, _ = q.shape
    n = s // t
    n_groups = hds // g

    nl = 0 if late is None else len(late)

    def body(*refs):
        q_ref, k_ref, vt_ref = refs[:3]
        wp_refs = refs[3:3 + nl]
        o_ref, lse_ref = refs[3 + nl:5 + nl]
        wout_refs = refs[5 + nl:5 + 2 * nl]
        m_sc, l_sc, acc_sc = refs[5 + 2 * nl:8 + 2 * nl]
        hg, qi, ki = pl.program_id(0), pl.program_id(1), pl.program_id(2)
        if nl:
            send_sems, recv_sems = refs[8 + 2 * nl:]
            tail = jnp.logical_and(hg == n_groups - 1, qi == n - 1)
            _gather_in_steps(wp_refs, wout_refs, send_sems, recv_sems,
                             first=jnp.logical_and(hg == 0, jnp.logical_and(qi == 0, ki == 0)),
                             mid=jnp.logical_and(tail, ki == 0), last=jnp.logical_and(tail, ki == n - 1))

        @pl.when(ki == 0)
        def _():
            m_sc[...] = jnp.full_like(m_sc, NEG)
            l_sc[...] = jnp.zeros_like(l_sc)
            acc_sc[...] = jnp.zeros_like(acc_sc)

        def step(masked):
            sc = _bdot(k_ref[...], q_ref[...], 2, 2)
            if masked:
                sc = jnp.where(_causal_mask_t(t), sc, NEG)
            m_prev = m_sc[...]
            m_new = jnp.maximum(m_prev, jnp.max(sc, axis=1, keepdims=True))
            p = jnp.exp2((sc - m_new) * (MLA_SCALE * LOG2_E))
            a = jnp.exp2((m_prev - m_new) * (MLA_SCALE * LOG2_E))
            l_sc[...] = a * l_sc[...] + jnp.sum(p, axis=1, keepdims=True)
            acc_sc[...] = a * acc_sc[...] + _bdot(vt_ref[...], _mx(p), 2, 1)
            m_sc[...] = m_new

        @pl.when(ki < qi)
        def _():
            step(False)

        @pl.when(ki == qi)
        def _():
            step(True)
            o_ref[...] = acc_sc[...] / l_sc[...]
            lse_ref[...] = m_sc[...] * MLA_SCALE + jnp.log(l_sc[...])

    qspec = pl.BlockSpec((g, t, QK_PAD), lambda h, i, j: (h, i, 0))
    kspec = pl.BlockSpec((g, t, QK_PAD), lambda h, i, j: (h, jnp.minimum(i, j), 0))
    vspec = pl.BlockSpec((g, HEAD_DIM, t), lambda h, i, j: (h, 0, jnp.minimum(i, j)))
    out_shape = [jax.ShapeDtypeStruct((hds, HEAD_DIM, s), F32), jax.ShapeDtypeStruct((hds, 1, s), F32)]
    in_specs = [qspec, kspec, vspec]
    out_specs = [pl.BlockSpec((g, HEAD_DIM, t), lambda h, i, j: (h, 0, i)), pl.BlockSpec((g, 1, t), lambda h, i, j: (h, 0, i))]
    scratch = [pltpu.VMEM((g, 1, t), F32), pltpu.VMEM((g, 1, t), F32), pltpu.VMEM((g, HEAD_DIM, t), F32)]
    args = [q, k, v_t]
    if nl:
        out_shape += [jax.ShapeDtypeStruct((N_CHIPS,) + a.shape, a.dtype) for a in late]
        in_specs += [ANY] * nl
        out_specs += [ANY] * nl
        scratch += [pltpu.SemaphoreType.DMA((6 * nl,)), pltpu.SemaphoreType.DMA((6 * nl,))]
        args += list(late)
    return pl.pallas_call(
        body, name="mla_attn_fwd",
        out_shape=tuple(out_shape), grid=(n_groups, n, n),
        in_specs=in_specs, out_specs=tuple(out_specs), scratch_shapes=scratch,
        compiler_params=pltpu.CompilerParams(dimension_semantics=("arbitrary",) * 3, vmem_limit_bytes=VMEM_LIMIT_BYTES,
                                             has_side_effects=nl > 0),
    )(*args)


def _mla_attn_bwd(q, k, v, do, lse, dd, *, t, g, early=()):
    hds, s, _ = q.shape
    n = s // t
    n_groups = hds // g
    ne = len(early)

    def body(*refs):
        q_ref, k_ref, v_ref, do_ref, lse_ref, dd_ref = refs[:6]
        ps_refs = refs[6:6 + ne]
        dq_ref, dk_ref, dv_ref = refs[6 + ne:9 + ne]
        ss_refs = refs[9 + ne:9 + 2 * ne]
        dq_sc, dk_sc, dv_sc = refs[9 + 2 * ne:12 + 2 * ne]
        hg, ki, qi = pl.program_id(0), pl.program_id(1), pl.program_id(2)
        if ne:
            send_sems, recv_sems = refs[12 + 2 * ne:]
            _exchange_in_steps(ps_refs, ss_refs, send_sems, recv_sems,
                               first=jnp.logical_and(hg == 0, jnp.logical_and(ki == 0, qi == 0)),
                               last=jnp.logical_and(hg == n_groups - 1, jnp.logical_and(ki == n - 1, qi == n - 1)))

        @pl.when(jnp.logical_and(ki == 0, qi == 0))
        def _():
            dq_sc[...] = jnp.zeros_like(dq_sc)

        @pl.when(qi == 0)
        def _():
            dk_sc[...] = jnp.zeros_like(dk_sc)
            dv_sc[...] = jnp.zeros_like(dv_sc)

        def step(masked):
            qb, kb, dob = q_ref[...], k_ref[...], do_ref[...]
            sc = _bdot(kb, qb, 2, 2) * MLA_SCALE
            if masked:
                sc = jnp.where(_causal_mask_t(t), sc, NEG)
            p = jnp.exp(sc - lse_ref[...])
            dv_sc[...] += _bdot(_mx(p), dob, 2, 1)
            dp = _bdot(v_ref[...], dob, 2, 2)
            ds = _mx(p * (dp - dd_ref[...]) * MLA_SCALE)
            dk_sc[...] += _bdot(ds, qb, 2, 1)
            dq_sc[qi] += _bdot(ds, kb, 1, 1)

        @pl.when(qi == ki)
        def _():
            step(True)

        @pl.when(qi > ki)
        def _():
            step(False)

        @pl.when(qi == n - 1)
        def _():
            dk_ref[...] = dk_sc[...]
            dv_ref[...] = dv_sc[...]

        @pl.when(jnp.logical_and(ki == n - 1, qi == n - 1))
        def _():
            for j in range(n):
                dq_ref[:, j * t:(j + 1) * t, :] = dq_sc[j]

    qs = lambda w: pl.BlockSpec((g, t, w), lambda h, j, i: (h, jnp.maximum(i, j), 0))
    ks = lambda w: pl.BlockSpec((g, t, w), lambda h, j, i: (h, j, 0))
    rowq = pl.BlockSpec((g, 1, t), lambda h, j, i: (h, 0, jnp.maximum(i, j)))
    scratch = [pltpu.VMEM((n, g, t, QK_PAD), F32), pltpu.VMEM((g, t, QK_PAD), F32), pltpu.VMEM((g, t, HEAD_DIM), F32)]
    if ne:
        scratch += [pltpu.SemaphoreType.DMA((3 * ne,)), pltpu.SemaphoreType.DMA((3 * ne,))]
    return pl.pallas_call(
        body, name="mla_attn_bwd",
        out_shape=(jax.ShapeDtypeStruct((hds, s, QK_PAD), F32), jax.ShapeDtypeStruct((hds, s, QK_PAD), F32),
                   jax.ShapeDtypeStruct((hds, s, HEAD_DIM), F32)) + tuple(jax.ShapeDtypeStruct(a.shape, a.dtype) for a in early),
        grid=(n_groups, n, n),
        in_specs=[qs(QK_PAD), ks(QK_PAD), ks(HEAD_DIM), qs(HEAD_DIM), rowq, rowq] + [ANY] * ne,
        out_specs=(pl.BlockSpec((g, s, QK_PAD), lambda h, j, i: (h, 0, 0)), ks(QK_PAD), ks(HEAD_DIM)) + (ANY,) * ne,
        scratch_shapes=scratch,
        compiler_params=pltpu.CompilerParams(dimension_semantics=("arbitrary",) * 3, vmem_limit_bytes=VMEM_LIMIT_BYTES,
                                             has_side_effects=ne > 0),
    )(q, k, v, do, lse, dd, *early)


def _perm_row(a, dil):
    if dil == 1:
        return a
    hds, _, s = a.shape
    return a.reshape(hds, s // dil, dil).transpose(0, 2, 1).reshape(hds, 1, s)


def _unperm_row(a, dil):
    if dil == 1:
        return a
    hds, _, s = a.shape
    return a.reshape(hds, dil, s // dil).transpose(0, 2, 1).reshape(hds, 1, s)


def _dil_bias(dil):
    slopes = 2.0 ** (-8.0 * jnp.arange(1, HEADS + 1, dtype=F32) / HEADS)
    ik = jnp.arange(DIL_BLOCK)[:, None]
    iq = jnp.arange(DIL_BLOCK)[None, :]
    off_c = iq - ik
    off_p = iq - ik + DIL_BLOCK
    b_c = -slopes[:, None, None] * (off_c * dil).astype(F32)[None]
    b_p = -slopes[:, None, None] * (off_p * dil).astype(F32)[None]
    b_c = jnp.where((off_c >= 0)[None], b_c, NEG)
    b_p = jnp.where((off_p <= DIL_BLOCK)[None], b_p, NEG)
    return b_c, b_p


def _dil_fwd(q, k, v, dil, *, name):
    hds, s, e = q.shape
    blk = DIL_BLOCK
    nblk = s // blk
    nb = nblk // dil
    pair = 2 if nb % 2 == 0 else 1
    b_c, b_p = _dil_bias(dil)

    def body(q_ref, k_ref, kp_ref, v_ref, vp_ref, bc_ref, bp_ref, o_ref, lse_ref):
        first = ((pair * pl.program_id(0)) % nb) == 0
        bc, bp = bc_ref[...], bp_ref[...]
        for j in range(pair):
            rows = slice(j * blk, (j + 1) * blk)
            qb = q_ref[:, rows, :]
            if j == 0:
                kp, vp = kp_ref[...], vp_ref[...]
            else:
                kp, vp = k_ref[:, (j - 1) * blk:j * blk, :], v_ref[:, (j - 1) * blk:j * blk, :]
            s_c = _bdot(k_ref[:, rows, :], qb, 2, 2) * DIL_SCALE + bc
            s_p = _bdot(kp, qb, 2, 2) * DIL_SCALE + bp
            if j == 0:
                s_p = jnp.where(first, NEG, s_p)
            m = jnp.maximum(jnp.max(s_c, axis=1, keepdims=True), jnp.max(s_p, axis=1, keepdims=True))
            p_c = jnp.exp(s_c - m)
            p_p = jnp.exp(s_p - m)
            l = jnp.sum(p_c, axis=1, keepdims=True) + jnp.sum(p_p, axis=1, keepdims=True)
            o = _bdot(_mx(p_c), v_ref[:, rows, :], 1, 1) + _bdot(_mx(p_p), vp, 1, 1)
            o_ref[:, rows, :] = o / jnp.swapaxes(l, 1, 2)
            lse_ref[:, :, rows] = m + jnp.log(l)

    cur = lambda w: pl.BlockSpec((hds, pair * blk, w), lambda b: (0, b, 0))
    prev = lambda w: pl.BlockSpec((hds, blk, w), lambda b: (0, jnp.maximum(pair * b - 1, 0), 0))
    bias = pl.BlockSpec((hds, blk, blk), lambda b: (0, 0, 0))
    return pl.pallas_call(
        body, name=name,
        out_shape=(jax.ShapeDtypeStruct((hds, s, e), F32), jax.ShapeDtypeStruct((hds, 1, s), F32)),
        grid=(nblk // pair,),
        in_specs=[cur(e), cur(e), prev(e), cur(e), prev(e), bias, bias],
        out_specs=(cur(e), pl.BlockSpec((hds, 1, pair * blk), lambda b: (0, 0, b))),
        compiler_params=_params("parallel"),
    )(q, k, k, v, v, b_c, b_p)


def _dil_combine(os_, lses, *, ts):
    hds, s, e = os_[0].shape
    dils = [d for _, d in DIL_PAIRS]

    def body(o0, o1, o2, l0, l1, l2, o_ref, l_ref, sc1, sc2):
        _load_token_order(sc1, o1, dils[1], ts)
        _load_token_order(sc2, o2, dils[2], ts)
        a0, a1, a2 = l0[...], l1[...], l2[...]
        m = jnp.maximum(jnp.maximum(a0, a1), a2)
        e0, e1, e2 = jnp.exp(a0 - m), jnp.exp(a1 - m), jnp.exp(a2 - m)
        tot = e0 + e1 + e2
        col = lambda w: jnp.swapaxes(w, 1, 2)
        res = (col(e0 / tot) * o0[...] + col(e1 / tot) * sc1[...]) + col(e2 / tot) * sc2[...]
        for hd in range(hds):
            o_ref[:, hd * e:(hd + 1) * e] = res[hd]
        l_ref[...] = m + jnp.log(tot)

    _, specs = _residue_major_outs(s, ts, dils, F32)
    view = lambda a, d: a if d == 1 else a.reshape(hds, d, s // d, e)
    rspec = pl.BlockSpec((hds, 1, ts), lambda i: (0, 0, i))
    return pl.pallas_call(
        body, name="dil_combine",
        out_shape=(jax.ShapeDtypeStruct((s, hds * e), F32), jax.ShapeDtypeStruct((hds, 1, s), F32)),
        grid=(s // ts,),
        in_specs=list(specs) + [rspec] * 3,
        out_specs=(pl.BlockSpec((ts, hds * e), lambda i: (i, 0)), rspec),
        scratch_shapes=[pltpu.VMEM((hds, ts, e), F32), pltpu.VMEM((hds, ts, e), F32)],
        compiler_params=_params("parallel"),
    )(*[view(a, d) for a, d in zip(os_, dils)], *lses)


def _dil_bwd(q, k, v, do, lj, dd, dil, *, name):
    hds, s, e = q.shape
    blk = DIL_BLOCK
    nblk = s // blk
    nb = nblk // dil
    pair = 2 if nb % 2 == 0 else 1
    b_c, b_p = _dil_bias(dil)

    def body(q_ref, qn_ref, k_ref, kp_ref, v_ref, vp_ref, do_ref, don_ref, l_ref, ln_ref, d_ref, dn_ref,
             bc_ref, bp_ref, dq_ref, dk_ref, dv_ref):
        b0 = pair * pl.program_id(0)
        first = (b0 % nb) == 0
        nxt = jnp.logical_and(b0 + pair < nblk, ((b0 + pair) % nb) != 0)
        bc, bp = bc_ref[...], bp_ref[...]
        for j in range(pair):
            rows = slice(j * blk, (j + 1) * blk)
            qb, kc, vc = q_ref[:, rows, :], k_ref[:, rows, :], v_ref[:, rows, :]
            dob, l, d = _mx(do_ref[:, rows, :]), l_ref[:, :, rows], d_ref[:, :, rows]
            if j == 0:
                kp, vp = kp_ref[...], vp_ref[...]
            else:
                kp, vp = k_ref[:, (j - 1) * blk:j * blk, :], v_ref[:, (j - 1) * blk:j * blk, :]
            p_c = jnp.exp(_bdot(kc, qb, 2, 2) * DIL_SCALE + bc - l)
            p_p = jnp.exp(_bdot(kp, qb, 2, 2) * DIL_SCALE + bp - l)
            if j == 0:
                p_p = jnp.where(first, 0.0, p_p)
            ds_c = _mx(p_c * (_bdot(vc, dob, 2, 2) - d) * DIL_SCALE)
            ds_p = _mx(p_p * (_bdot(vp, dob, 2, 2) - d) * DIL_SCALE)
            dq_ref[:, rows, :] = _bdot(ds_c, kc, 1, 1) + _bdot(ds_p, kp, 1, 1)
            if j < pair - 1:
                nrows = slice((j + 1) * blk, (j + 2) * blk)
                qn, donb, ln, dn = q_ref[:, nrows, :], _mx(do_ref[:, nrows, :]), l_ref[:, :, nrows], d_ref[:, :, nrows]
            else:
                qn, donb, ln, dn = qn_ref[...], _mx(don_ref[...]), ln_ref[...], dn_ref[...]
            p_n = jnp.exp(_bdot(kc, qn, 2, 2) * DIL_SCALE + bp - ln)
            if j == pair - 1:
                p_n = jnp.where(nxt, p_n, 0.0)
            ds_n = _mx(p_n * (_bdot(vc, donb, 2, 2) - dn) * DIL_SCALE)
            dk_ref[:, rows, :] = _bdot(ds_c, qb, 2, 1) + _bdot(ds_n, qn, 2, 1)
            dv_ref[:, rows, :] = _bdot(_mx(p_c), dob, 2, 1) + _bdot(_mx(p_n), donb, 2, 1)

    cur = lambda w: pl.BlockSpec((hds, pair * blk, w), lambda b: (0, b, 0))
    prev = lambda w: pl.BlockSpec((hds, blk, w), lambda b: (0, jnp.maximum(pair * b - 1, 0), 0))
    nxt_ = lambda w: pl.BlockSpec((hds, blk, w), lambda b: (0, jnp.minimum(pair * (b + 1), nblk - 1), 0))
    rcur = pl.BlockSpec((hds, 1, pair * blk), lambda b: (0, 0, b))
    rnxt = pl.BlockSpec((hds, 1, blk), lambda b: (0, 0, jnp.minimum(pair * (b + 1), nblk - 1)))
    bias = pl.BlockSpec((hds, blk, blk), lambda b: (0, 0, 0))
    out = jax.ShapeDtypeStruct((hds, s, e), F32)
    return pl.pallas_call(
        body, name=name,
        out_shape=(out, out, out),
        grid=(nblk // pair,),
        in_specs=[cur(e), nxt_(e), cur(e), prev(e), cur(e), prev(e), cur(e), nxt_(e),
                  rcur, rnxt, rcur, rnxt, bias, bias],
        out_specs=(cur(e), cur(e), cur(e)),
        compiler_params=_params("parallel"),
    )(q, q, k, k, v, v, do, do, lj, lj, dd, dd, b_c, b_p)


def _ln_fwd(z, g, b):
    mu = jnp.mean(z, axis=-1, keepdims=True)
    zc = z - mu
    var = jnp.mean(zc * zc, axis=-1, keepdims=True)
    rstd = lax.rsqrt(var + LN_EPS)
    xhat = zc * rstd
    return xhat * g + b, xhat, rstd


def _ln_bwd(dy, xhat, rstd, g):
    dxh = dy * g
    return rstd * (dxh - jnp.mean(dxh, axis=-1, keepdims=True) - xhat * jnp.mean(dxh * xhat, axis=-1, keepdims=True))


def _out_ln1(a_mla, a_dil, w_o, x, g, b, *, tm):
    s = x.shape[0]
    half = HEADS * HEAD_DIM

    def body(am_ref, ad_ref, w_ref, x_ref, g_ref, b_ref, x1_ref, xh_ref, r_ref):
        mix = _dot(_mx(am_ref[...]), w_ref[0:half, :]) + _dot(_mx(ad_ref[...]), w_ref[half:2 * half, :])
        z = DN_ALPHA * x_ref[...] + mix
        y, xhat, rstd = _ln_fwd(z, g_ref[...], b_ref[...])
        x1_ref[...] = y
        xh_ref[...] = xhat
        r_ref[...] = rstd

    row = lambda w: pl.BlockSpec((tm, w), lambda i: (i, 0))
    full = lambda shp: pl.BlockSpec(shp, lambda i: (0,) * len(shp))
    act = jax.ShapeDtypeStruct((s, D_MODEL), F32)
    return pl.pallas_call(
        body, name="out_ln1",
        out_shape=(act, act, jax.ShapeDtypeStruct((s, 1), F32)),
        grid=(s // tm,),
        in_specs=[row(half), row(half), full((D_MODEL, D_MODEL)), row(D_MODEL), full((1, D_MODEL)), full((1, D_MODEL))],
        out_specs=(row(D_MODEL), row(D_MODEL), row(1)),
        compiler_params=_params("parallel"),
    )(a_mla, a_dil, w_o, x, g, b)


def _down_ln2_loss(act, w_down, x1, g, b, target, *, tm):
    s = x1.shape[0]

    def body(a_ref, w_ref, x1_ref, g_ref, b_ref, t_ref, dz_ref, loss_ref, dg_ref, db_ref):
        i = pl.program_id(0)

        @pl.when(i == 0)
        def _():
            loss_ref[...] = jnp.zeros_like(loss_ref)
            dg_ref[...] = jnp.zeros_like(dg_ref)
            db_ref[...] = jnp.zeros_like(db_ref)

        gam = g_ref[...]
        z = DN_ALPHA * x1_ref[...] + _dot(a_ref[...], w_ref[...])
        y, xhat, rstd = _ln_fwd(z, gam, b_ref[...])
        err = y - t_ref[...]
        loss_ref[...] += 0.5 * jnp.sum(jnp.mean(err * err, axis=-1, keepdims=True))
        dy = err * (1.0 / D_MODEL)
        dg_ref[...] += jnp.sum(dy * xhat, axis=0, keepdims=True)
        db_ref[...] += jnp.sum(dy, axis=0, keepdims=True)
        dz_ref[...] = _ln_bwd(dy, xhat, rstd, gam)

    row = lambda w: pl.BlockSpec((tm, w), lambda i: (i, 0))
    full = lambda shp: pl.BlockSpec(shp, lambda i: (0,) * len(shp))
    vec = jax.ShapeDtypeStruct((1, D_MODEL), F32)
    return pl.pallas_call(
        body, name="down_ln2_loss",
        out_shape=(jax.ShapeDtypeStruct((s, D_MODEL), F32), jax.ShapeDtypeStruct((1, LANES), F32), vec, vec),
        grid=(s // tm,),
        in_specs=[row(D_FF), full((D_FF, D_MODEL)), row(D_MODEL), full((1, D_MODEL)), full((1, D_MODEL)), row(D_MODEL)],
        out_specs=(row(D_MODEL), full((1, LANES)), full((1, D_MODEL)), full((1, D_MODEL))),
        compiler_params=_params("arbitrary"),
    )(act, w_down, x1, g, b, target)


def _up_bwd_ln1(du_a, du_g, w_up_t, dz2, xhat1, rstd1, g, *, tm):
    s = dz2.shape[0]

    def body(dua_ref, dug_ref, wa_ref, wg_ref, dz2_ref, xh_ref, r_ref, g_ref, dz1_ref, dg_ref, db_ref):
        i = pl.program_id(0)

        @pl.when(i == 0)
        def _():
            dg_ref[...] = jnp.zeros_like(dg_ref)
            db_ref[...] = jnp.zeros_like(db_ref)

        dx1 = DN_ALPHA * dz2_ref[...] + (_dot(dua_ref[...], wa_ref[...]) + _dot(dug_ref[...], wg_ref[...]))
        xhat = xh_ref[...]
        dg_ref[...] += jnp.sum(dx1 * xhat, axis=0, keepdims=True)
        db_ref[...] += jnp.sum(dx1, axis=0, keepdims=True)
        dz1_ref[...] = _ln_bwd(dx1, xhat, r_ref[...], g_ref[...])

    row = lambda w: pl.BlockSpec((tm, w), lambda i: (i, 0))
    full = lambda shp: pl.BlockSpec(shp, lambda i: (0,) * len(shp))
    vec = jax.ShapeDtypeStruct((1, D_MODEL), F32)
    return pl.pallas_call(
        body, name="up_bwd_ln1",
        out_shape=(jax.ShapeDtypeStruct((s, D_MODEL), F32), vec, vec),
        grid=(s // tm,),
        in_specs=[row(D_FF), row(D_FF),
                  pl.BlockSpec((D_FF, D_MODEL), lambda i: (0, 0)), pl.BlockSpec((D_FF, D_MODEL), lambda i: (1, 0)),
                  row(D_MODEL), row(D_MODEL), row(1), full((1, D_MODEL))],
        out_specs=(row(D_MODEL), full((1, D_MODEL)), full((1, D_MODEL))),
        compiler_params=_params("arbitrary"),
    )(du_a, du_g, w_up_t, w_up_t, dz2, xhat1, rstd1, g)


GELU_C = math.sqrt(2.0 / math.pi)


def _gelu(x):
    cdf = 0.5 * (1.0 + jnp.tanh(GELU_C * (x + 0.044715 * (x * x * x))))
    return x * cdf


def _gelu_grad(x):
    t = jnp.tanh(GELU_C * (x + 0.044715 * (x * x * x)))
    return 0.5 * (1.0 + t) + 0.5 * x * (1.0 - t * t) * (GELU_C * (1.0 + 3.0 * 0.044715 * (x * x)))


def _shift_down(u, halo):
    r1, r2 = pltpu.roll(u, 1, 0), pltpu.roll(u, 2, 0)
    row = lax.broadcasted_iota(jnp.int32, (SUBLANES, u.shape[1]), 0)
    h7, h6 = halo[7:8, :], halo[6:7, :]
    head1 = jnp.where(row == 0, h7, r1[:SUBLANES])
    head2 = jnp.where(row == 0, h6, jnp.where(row == 1, h7, r2[:SUBLANES]))
    return (jnp.concatenate([head1, r1[SUBLANES:]], axis=0), jnp.concatenate([head2, r2[SUBLANES:]], axis=0))


def _shift_up(d, nxt):
    t = d.shape[0]
    r1, r2 = pltpu.roll(d, t - 1, 0), pltpu.roll(d, t - 2, 0)
    row = lax.broadcasted_iota(jnp.int32, (SUBLANES, d.shape[1]), 0)
    n0, n1 = nxt[0:1, :], nxt[1:2, :]
    last = t - SUBLANES
    tail1 = jnp.where(row == SUBLANES - 1, n0, r1[last:])
    tail2 = jnp.where(row == SUBLANES - 1, n1, jnp.where(row == SUBLANES - 2, n0, r2[last:]))
    return (jnp.concatenate([r1[:last], tail1], axis=0), jnp.concatenate([r2[:last], tail2], axis=0))


def _conv(u, s1, s2, w, b):
    return ((b + w[0:1, :] * s2) + w[1:2, :] * s1) + w[2:3, :] * u


def _up_gate_fwd(x1, w_up, conv_w, conv_b, *, tm, tn):
    s = x1.shape[0]
    nj = D_FF // tn
    hb = tm // SUBLANES

    def body(x_ref, xh_ref, wua_ref, wug_ref, wa_ref, wg_ref, ba_ref, bg_ref,
             ua_ref, ug_ref, o_ref, a_ref, ge_ref, gd_ref):
        keep = pl.program_id(1) > 0
        xb, xh = _mx(x_ref[...]), _mx(xh_ref[...])
        wua, wug = wua_ref[...], wug_ref[...]
        ua, ug = _dot(xb, wua), _dot(xb, wug)
        ha = jnp.where(keep, _dot(xh, wua), 0.0)
        hg = jnp.where(keep, _dot(xh, wug), 0.0)
        ua_ref[...] = ua
        ug_ref[...] = ug
        a = _conv(ua, *_shift_down(ua, ha), wa_ref[...], ba_ref[...])
        g = _conv(ug, *_shift_down(ug, hg), wg_ref[...], bg_ref[...])
        ge = _gelu(g)
        o_ref[...] = (ge * a).astype(o_ref.dtype)
        a_ref[...] = a
        ge_ref[...] = ge
        gd_ref[...] = _gelu_grad(g)

    main = lambda off: pl.BlockSpec((tm, tn), lambda j, i: (i, j + off))
    wspec = lambda r, off: pl.BlockSpec((r, tn), lambda j, i: (0, j + off))
    if w_up.ndim == 3:
        wu = lambda off: pl.BlockSpec((None, D_MODEL, tn), lambda j, i: (j + off, 0, 0))
    else:
        wu = lambda off: pl.BlockSpec((D_MODEL, tn), lambda j, i: (0, j + off))
    keep_f32 = jax.ShapeDtypeStruct((s, D_FF), F32)
    return pl.pallas_call(
        body, name="up_gate_fwd",
        out_shape=(keep_f32, keep_f32, jax.ShapeDtypeStruct((s, D_FF), MXU_DTYPE), keep_f32, keep_f32, keep_f32),
        grid=(nj, s // tm),
        in_specs=[pl.BlockSpec((tm, D_MODEL), lambda j, i: (i, 0)),
                  pl.BlockSpec((SUBLANES, D_MODEL), lambda j, i: (jnp.maximum(i * hb - 1, 0), 0)),
                  wu(0), wu(nj), wspec(3, 0), wspec(3, nj), wspec(1, 0), wspec(1, nj)],
        out_specs=(main(0),) * 6,
        compiler_params=_params("parallel", "parallel"),
    )(x1, x1, w_up, w_up, conv_w, conv_w, conv_b, conv_b)


def _gate_bwd(u_a, u_g, dz2, w_down_t, a, ge, gd, conv_w, *, tm, tn):
    s = u_a.shape[0]
    nj = D_FF // tn
    ni = s // tm
    hb = tm // SUBLANES

    def body(ua_ref, ug_ref, ha_ref, hg_ref, dz_ref, dzn_ref, wd_ref, a_ref, an_ref, ge_ref, gen_ref, gd_ref, gdn_ref,
             wa_ref, wg_ref, dua_ref, dug_ref, dwa_ref, dwg_ref, dba_ref, dbg_ref):
        i = pl.program_id(1)

        @pl.when(i == 0)
        def _():
            for r in (dwa_ref, dwg_ref, dba_ref, dbg_ref):
                r[...] = jnp.zeros_like(r)

        wa, wg = wa_ref[...], wg_ref[...]
        ua, ug = ua_ref[...], ug_ref[...]
        ha = jnp.where(i > 0, ha_ref[...], 0.0)
        hg = jnp.where(i > 0, hg_ref[...], 0.0)
        sa1, sa2 = _shift_down(ua, ha)
        sg1, sg2 = _shift_down(ug, hg)
        wd = wd_ref[...]
        d = _dot(_mx(dz_ref[...]), wd)
        dya = d * ge_ref[...]
        dyg = d * a_ref[...] * gd_ref[...]
        dn = jnp.where(i < ni - 1, _dot(_mx(dzn_ref[...]), wd), 0.0)
        dya_n = dn * gen_ref[...]
        dyg_n = dn * an_ref[...] * gdn_ref[...]
        da1, da2 = _shift_up(dya, dya_n)
        dg1, dg2 = _shift_up(dyg, dyg_n)
        dua_ref[...] = (wa[2:3, :] * dya + wa[1:2, :] * da1 + wa[0:1, :] * da2).astype(dua_ref.dtype)
        dug_ref[...] = (wg[2:3, :] * dyg + wg[1:2, :] * dg1 + wg[0:1, :] * dg2).astype(dug_ref.dtype)
        ssum = lambda v: jnp.sum(v, axis=0, keepdims=True)
        dwa_ref[...] += jnp.concatenate([ssum(dya * sa2), ssum(dya * sa1), ssum(dya * ua)], axis=0)
        dwg_ref[...] += jnp.concatenate([ssum(dyg * sg2), ssum(dyg * sg1), ssum(dyg * ug)], axis=0)
        dba_ref[...] += ssum(dya)
        dbg_ref[...] += ssum(dyg)

    main = pl.BlockSpec((tm, tn), lambda j, i: (i, j))
    halo = pl.BlockSpec((SUBLANES, tn), lambda j, i: (jnp.maximum(i * hb - 1, 0), j))
    next_row = lambda j, i: jnp.minimum((i + 1) * hb, s // SUBLANES - 1)
    nxt = pl.BlockSpec((SUBLANES, tn), lambda j, i: (next_row(j, i), j))
    wspec = lambda r, off: pl.BlockSpec((r, tn), lambda j, i: (0, j + off))
    return pl.pallas_call(
        body, name="gate_bwd",
        out_shape=(jax.ShapeDtypeStruct((s, D_FF), MXU_DTYPE), jax.ShapeDtypeStruct((s, D_FF), MXU_DTYPE),
                   jax.ShapeDtypeStruct((3, D_FF), F32), jax.ShapeDtypeStruct((3, D_FF), F32),
                   jax.ShapeDtypeStruct((1, D_FF), F32), jax.ShapeDtypeStruct((1, D_FF), F32)),
        grid=(nj, ni),
        in_specs=[main, main, halo, halo,
                  pl.BlockSpec((tm, D_MODEL), lambda j, i: (i, 0)),
                  pl.BlockSpec((SUBLANES, D_MODEL), lambda j, i: (next_row(j, i), 0)),
                  pl.BlockSpec((D_MODEL, tn), lambda j, i: (0, j))]
        + [main, nxt] * 3 + [wspec(3, 0), wspec(3, nj)],
        out_specs=(main, main, wspec(3, 0), wspec(3, 0), wspec(1, 0), wspec(1, 0)),
        compiler_params=_params("parallel", "arbitrary"),
    )(u_a, u_g, u_a, u_g, dz2, dz2, w_down_t, a, a, ge, ge, gd, gd, conv_w, conv_w)


def _prep_weights(w_in, w_uq, w_uk, w_uv, w_o, w_up, w_down):
    return {**_prep_weights_first(w_in, w_uq, w_uk, w_uv), **_prep_weights_late(w_o, w_up, w_down)}


def _prep_weights_late(w_o, w_up, w_down):
    w_o, w_up, w_down = _mx(w_o), _mx(w_up), _mx(w_down)
    w_up_t = w_up.T if w_up.ndim == 2 else w_up.transpose(0, 2, 1).reshape(2 * D_FF, D_MODEL)
    return dict(w_o=w_o, w_o_t=w_o.T, w_up=w_up, w_up_t=w_up_t, w_down=w_down, w_down_t=w_down.T)


def _prep_weights_first(w_in, w_uq, w_uk, w_uv):
    c = lambda a: a.astype(MXU_DTYPE)
    w_in = c(w_in)
    z = lambda w: jnp.zeros((D_MODEL, w), MXU_DTYPE)
    r0 = Q_RANK + KV_RANK
    w_in_ext = jnp.concatenate([w_in[:, :r0], z(NOPE), w_in[:, r0:r0 + ROPE], z(32), w_in[:, r0 + ROPE:]], axis=1)
    wq = jnp.pad(c(w_uq).transpose(1, 0, 2), ((0, 0), (0, 0), (0, QK_PAD - NOPE - ROPE)))
    wk = jnp.pad(c(w_uk).transpose(1, 0, 2), ((0, 0), (0, 0), (0, QK_PAD - NOPE)))
    wv = c(w_uv).transpose(1, 0, 2)
    t3 = lambda a: a.transpose(0, 2, 1)
    return dict(w_in=w_in_ext, w_in_t=w_in_ext.T, wq=wq, wq_t=t3(wq), wk=wk, wk_t=t3(wk), wv=wv, wv_t=t3(wv))


def _local_step(x, target, w, g_cq, g_ckv, ln1_g, ln1_b, conv_w, conv_b, ln2_g, ln2_b, comm=None):
    s = x.shape[0]
    tabs = _rope_tables(s)
    r2 = lambda a: a.reshape(1, -1)
    cb = r2(conv_b)
    dils = [d for _, d in DIL_PAIRS]

    h, qp, kp, vp = _in_proj(x, w["w_in"], tm=512)
    q, k, v, v_t = _mla_prep_fwd(h, r2(g_cq), r2(g_ckv), w["wq"], w["wk"], w["wv"], w["wv_t"], tabs, tm=256)
    if comm is None:
        o_mla_t, lse_mla = _mla_attn_fwd(q, k, v_t, t=512, g=HEADS)
    else:
        o_mla_t, lse_mla, *gathered = _mla_attn_fwd(q, k, v_t, t=512, g=HEADS, late=comm["late"])
        w = {**w, **comm["finish"](gathered)}
    o_bs, lse_bs = [], []
    for i, d in enumerate(dils):
        o_b, l_b = _dil_fwd(qp[i], kp[i], vp[i], d, name=f"dil_fwd_{d}")
        o_bs.append(o_b)
        lse_bs.append(_unperm_row(l_b, d))
    o_dil, lj = _dil_combine(o_bs, lse_bs, ts=512)
    o_mla = o_mla_t.transpose(2, 0, 1).reshape(s, HEADS * HEAD_DIM)
    x1, xhat1, rstd1 = _out_ln1(o_mla, o_dil, w["w_o"], x, r2(ln1_g), r2(ln1_b), tm=256)
    u_a, u_g, act, conv_a, gelu_g, gelu_dg = _up_gate_fwd(x1, w["w_up"], conv_w, cb, tm=256, tn=1408)
    dz2, loss, dg2, db2 = _down_ln2_loss(act, w["w_down"], x1, r2(ln2_g), r2(ln2_b), target, tm=256)

    dw_down = _mm_tn(act, dz2, name="dw_down", tm=1408, tn=D_MODEL, ts=DW_TOKENS)
    du_a, du_g, dcw_a, dcw_g, dcb_a, dcb_g = _gate_bwd(u_a, u_g, dz2, w["w_down_t"], conv_a, gelu_g, gelu_dg, conv_w,
                                                       tm=256, tn=1408)
    dz1, dg1, db1 = _up_bwd_ln1(du_a, du_g, w["w_up_t"], dz2, xhat1, rstd1, r2(ln1_g), tm=256)
    dw_up = jnp.concatenate([_mm_tn(x1, du_a, name="dw_up_a", tm=D_MODEL, tn=1408, ts=DW_TOKENS),
                             _mm_tn(x1, du_g, name="dw_up_g", tm=D_MODEL, tn=1408, ts=DW_TOKENS)], axis=1)
    named_early = [("w_up", dw_up), ("w_down", dw_down)]
    swap = () if comm is None else comm["blocked"](named_early)
    do_mla, do_dil, dd_all, received = _attn_bwd_heads(dz1, w["w_o_t"], o_mla, o_dil, tm=512, swap=swap)
    dw_o = jnp.concatenate([_mm_tn(o_mla, dz1, name="dw_o_mla", tm=512, tn=D_MODEL, ts=DW_TOKENS),
                            _mm_tn(o_dil, dz1, name="dw_o_dil", tm=512, tn=D_MODEL, ts=DW_TOKENS)], axis=0)
    dd_all = dd_all.T
    dd_mla, dd_dil = dd_all[:HEADS].reshape(HEADS, 1, s), dd_all[HEADS:].reshape(HEADS, 1, s)
    early = () if comm is None else tuple(comm["add_halves"](named_early, swap, received))
    dq, dk, dv, *early_slots = _mla_attn_bwd(q, k, v, do_mla, lse_mla, dd_mla, t=512, g=4, early=early)
    parts = []
    for i, d in enumerate(dils):
        parts.append(_dil_bwd(qp[i], kp[i], vp[i], do_dil[i], _perm_row(lj, d), _perm_row(dd_dil, d), d, name=f"dil_bwd_{d}"))
    dh_dil = _dil_merge(parts, ts=512)
    dh_mla, dwq, dwk, dwv, dgq, dgkv = _mla_prep_bwd(h, dq, dk, dv, r2(g_cq), r2(g_ckv),
                                                     w["wq_t"], w["wk_t"], w["wv_t"], tabs, tm=256)
    mla_w = 4 * LANES
    w_in_t = w["w_in_t"]
    grad_x = _mm_nn(dh_mla, w_in_t[:mla_w], name="in_bwd_mla", tm=512, tn=D_MODEL, tk=mla_w, add=dz1, add_scale=DN_ALPHA)
    grad_x = _mm_nn(dh_dil, w_in_t[mla_w:], name="in_bwd_dil", tm=512, tn=D_MODEL, tk=3 * HEADS * HEAD_DIM, add=grad_x)
    dw_mla = _mm_tn(x, dh_mla, name="dw_in_mla", tm=D_MODEL, tn=mla_w, ts=DW_TOKENS)
    dw_dil = _mm_tn(x, dh_dil, name="dw_in_dil", tm=D_MODEL, tn=3 * HEADS * HEAD_DIM, ts=DW_TOKENS)
    r0 = Q_RANK + KV_RANK
    grads = dict(
        w_in=jnp.concatenate([dw_mla[:, :r0], dw_mla[:, r0 + NOPE:r0 + NOPE + ROPE], dw_dil], axis=1),
        g_cq=dgq[0], g_ckv=dgkv[0],
        w_uq=dwq[:, :, :NOPE + ROPE].transpose(1, 0, 2),
        w_uk=dwk[:, :, :NOPE].transpose(1, 0, 2),
        w_uv=dwv.transpose(1, 0, 2),
        w_o=dw_o, ln1_g=dg1[0], ln1_b=db1[0], w_up=dw_up,
        conv_w=jnp.concatenate([dcw_a, dcw_g], axis=1), conv_b=jnp.concatenate([dcb_a, dcb_g], axis=1)[0],
        w_down=dw_down, ln2_g=dg2[0], ln2_b=db2[0])
    if comm is not None:
        grads["early"] = (early, tuple(early_slots))
    return loss[0, 0], grad_x, grads


N_CHIPS = 4
SHARDED = ("w_in", "w_uq", "w_o", "w_up", "conv_w", "w_down")
COL_SHARDED = ("w_in", "w_up", "conv_w")
SHARD_SHAPE = dict(w_in=(D_MODEL, IN_WIDTH // 4), w_uq=(Q_RANK // 4, HEADS, NOPE + ROPE), w_o=(D_MODEL // 4, D_MODEL),
                   w_up=(D_MODEL, 2 * D_FF // 4), conv_w=(3, 2 * D_FF // 4), w_down=(D_FF // 4, D_MODEL))
SMALL = ("g_cq", "g_ckv", "w_uk", "w_uv", "ln1_g", "ln1_b", "conv_b", "ln2_g", "ln2_b")
SMALL_SHAPE = dict(g_cq=(Q_RANK,), g_ckv=(KV_RANK,), w_uk=(KV_RANK, HEADS, NOPE), w_uv=(KV_RANK, HEADS, HEAD_DIM),
                   ln1_g=(D_MODEL,), ln1_b=(D_MODEL,), conv_b=(2 * D_FF,), ln2_g=(D_MODEL,), ln2_b=(D_MODEL,))
BIG = ("w_in", "w_uq", "w_o", "w_up", "w_down")
BIG_2D = dict(w_in=(D_MODEL, IN_WIDTH // 4), w_uq=(Q_RANK // 4, HEADS * (NOPE + ROPE)), w_o=(D_MODEL // 4, D_MODEL),
              w_up=(D_MODEL, 2 * D_FF // 4), w_down=(D_FF // 4, D_MODEL))
SMALL_G = SMALL + ("conv_w",)
SMALL_WIDE = ("w_uk", "w_uv")
SMALL_G_SHAPE = {**SMALL_SHAPE, "conv_w": (3, 2 * D_FF)}
SMALL_U_SHAPE = {**SMALL_SHAPE, "conv_w": (3, 2 * D_FF // 4)}


def _size(shape):
    return math.prod(shape)


def _padded_rows(n_elems, mult):
    return -(-n_elems // (LANES * mult)) * mult


SHARD_ROWS = {n: _padded_rows(_size(SHARD_SHAPE[n]), SUBLANES) for n in SHARDED}
R_SMALL = -(-sum(_size(SMALL_G_SHAPE[n]) for n in SMALL_G) // (LANES * LANES)) * LANES
GATHER_FIRST = ("w_in", "w_uq")
GATHER_LATE = ("w_o", "w_up", "w_down")
REDUCED_EARLY = ("w_up", "w_down")
REDUCED_LAST = ("w_in", "w_uq", "w_o")


def _rows(a, rows=None):
    flat = a.reshape(-1)
    rows = -(-flat.shape[0] // LANES) if rows is None else rows
    return jnp.pad(flat, (0, rows * LANES - flat.shape[0])).reshape(rows, LANES)


def _blocked(name, g):
    r, c = BIG_2D[name]
    a = g.reshape(r, N_CHIPS, c).transpose(1, 0, 2) if name in COL_SHARDED else g.reshape(N_CHIPS, r, c)
    return a.reshape(N_CHIPS, 2, r // 2, c)


def _pack_flat(t, names, rows=None):
    flat = jnp.concatenate([t[n].astype(F32).reshape(-1) for n in names])
    return _rows(flat, R_SMALL if rows is None else rows)


def _unpack_flat(buf, names, shapes):
    flat, out, r = buf.reshape(-1), {}, 0
    for n in names:
        out[n] = flat[r:r + _size(shapes[n])].reshape(shapes[n])
        r += _size(shapes[n])
    return out


def _from_chip_blocks(name, blocks):
    shp = SHARD_SHAPE[name]
    a = blocks.reshape(N_CHIPS, -1)[:, :_size(shp)].reshape((N_CHIPS,) + shp)
    if name in COL_SHARDED:
        return a.transpose(1, 0, 2).reshape(shp[0], N_CHIPS * shp[1])
    return a.reshape((N_CHIPS * shp[0],) + shp[1:])


ANY = pl.BlockSpec(memory_space=pl.ANY)
COMM_PARAMS = pltpu.CompilerParams(has_side_effects=True)


def _coords():
    return lax.axis_index("x"), lax.axis_index("y"), lax.axis_index("c")


def _other_chips(x, y):
    return [(1 - x, y), (x, 1 - y), (1 - x, 1 - y)]


def _remote(src, dst, send_sems, recv_sems, k, to):
    return pltpu.make_async_remote_copy(src_ref=src, dst_ref=dst, send_sem=send_sems.at[k], recv_sem=recv_sems.at[k],
                                        device_id=to, device_id_type=MESH)


def _gather_in_steps(wp_refs, wout_refs, send_sems, recv_sems, *, first, mid, last):
    x, y, c = _coords()
    me = 2 * x + y
    sib = (x, y, 1 - c)
    chips = _other_chips(x, y)
    n = len(wp_refs)
    pairs = [(j, t, px, py) for j, (px, py) in enumerate(chips) for t in range(n)]
    ici = [_remote(wp_refs[t].at[c], wout_refs[t].at[me, c], send_sems, recv_sems, j * n + t, (px, py, c))
           for j, t, px, py in pairs]
    fwd = [_remote(wout_refs[t].at[2 * px + py, c], wout_refs[t].at[2 * px + py, c], send_sems, recv_sems, (3 + j) * n + t, sib)
           for j, t, px, py in pairs]

    @pl.when(first)
    def _():
        for cp in ici:
            cp.start()

    @pl.when(mid)
    def _():
        for i, (j, t, px, py) in enumerate(pairs):
            _remote(wp_refs[t].at[c], wout_refs[t].at[2 * px + py, c], send_sems, recv_sems, j * n + t, (px, py, c)).wait_recv()
            fwd[i].start()

    @pl.when(last)
    def _():
        for j, t, px, py in pairs:
            k = 2 * px + py
            _remote(wout_refs[t].at[k, 1 - c], wout_refs[t].at[k, 1 - c], send_sems, recv_sems, (3 + j) * n + t, sib).wait_recv()
        for cp in ici + fwd:
            cp.wait_send()


def _swap_halves_in_steps(gs_refs, os_refs, send_sems, recv_sems, *, first, last):
    x, y, c = _coords()
    sib = (x, y, 1 - c)
    cps = [_remote(gs_refs[t].at[k, 1 - c], os_refs[t].at[k], send_sems, recv_sems, t * N_CHIPS + k, sib)
           for t in range(len(gs_refs)) for k in range(N_CHIPS)]

    @pl.when(first)
    def _():
        for cp in cps:
            cp.start()

    @pl.when(last)
    def _():
        for cp in cps:
            cp.wait_recv()
        for cp in cps:
            cp.wait_send()


def _exchange_in_steps(ps_refs, ss_refs, send_sems, recv_sems, *, first, last):
    x, y, c = _coords()
    me = 2 * x + y
    chips = _other_chips(x, y)
    n = len(ps_refs)
    sends = [_remote(ps_refs[t].at[2 * px + py], ss_refs[t].at[me], send_sems, recv_sems, j * n + t, (px, py, c))
             for j, (px, py) in enumerate(chips) for t in range(n)]

    @pl.when(first)
    def _():
        for cp in sends:
            cp.start()

    @pl.when(last)
    def _():
        for j, (px, py) in enumerate(chips):
            for t in range(n):
                _remote(ps_refs[t].at[me], ss_refs[t].at[2 * px + py], send_sems, recv_sems, j * n + t, (px, py, c)).wait_recv()
        for cp in sends:
            cp.wait_send()


def _gather_weights(wp, cwp):
    def body(wp_ref, cw_ref, wout_ref, cwout_ref, send_sems, recv_sems):
        x, y, c = _coords()
        me = 2 * x + y
        sib = (x, y, 1 - c)
        chips = _other_chips(x, y)
        sends = [_remote(wp_ref.at[c], wout_ref.at[me, c], send_sems, recv_sems, j, (px, py, c))
                 for j, (px, py) in enumerate(chips)]
        sends += [_remote(cw_ref, cwout_ref.at[me], send_sems, recv_sems, 3 + j, (px, py, c))
                  for j, (px, py) in enumerate(chips)]
        for cp in sends:
            cp.start()
        for j, (px, py) in enumerate(chips):
            k = 2 * px + py
            _remote(wp_ref.at[c], wout_ref.at[k, c], send_sems, recv_sems, j, (px, py, c)).wait_recv()
            fwd = _remote(wout_ref.at[k, c], wout_ref.at[k, c], send_sems, recv_sems, 6 + j, sib)
            fwd.start()
            sends.append(fwd)
        for j, (px, py) in enumerate(chips):
            k = 2 * px + py
            _remote(cw_ref, cwout_ref.at[k], send_sems, recv_sems, 3 + j, (px, py, c)).wait_recv()
            _remote(wout_ref.at[k, 1 - c], wout_ref.at[k, 1 - c], send_sems, recv_sems, 6 + j, sib).wait_recv()
        for cp in sends:
            cp.wait_send()

    return pl.pallas_call(
        body, name="gather_weights",
        out_shape=(jax.ShapeDtypeStruct((N_CHIPS,) + wp.shape, wp.dtype), jax.ShapeDtypeStruct((N_CHIPS,) + cwp.shape, cwp.dtype)),
        in_specs=[ANY, ANY], out_specs=(ANY, ANY),
        scratch_shapes=[pltpu.SemaphoreType.DMA((9,)), pltpu.SemaphoreType.DMA((9,))],
        compiler_params=COMM_PARAMS,
    )(wp, cwp)


def _exchange_sibling_halves(gs, whole, *, name):
    n, nw = len(gs), len(whole)

    def body(*refs):
        gs_refs, wh_refs = refs[:n], refs[n:n + nw]
        os_refs, ow_refs = refs[n + nw:2 * n + nw], refs[2 * n + nw:2 * (n + nw)]
        send_sems, recv_sems = refs[2 * (n + nw):]
        x, y, c = _coords()
        sib = (x, y, 1 - c)
        cps = [_remote(gs_refs[t].at[k, 1 - c], os_refs[t].at[k], send_sems, recv_sems, t * N_CHIPS + k, sib)
               for t in range(n) for k in range(N_CHIPS)]
        cps += [_remote(wh_refs[t], ow_refs[t], send_sems, recv_sems, n * N_CHIPS + t, sib) for t in range(nw)]
        for cp in cps:
            cp.start()
        for cp in cps:
            cp.wait_recv()
        for cp in cps:
            cp.wait_send()

    n_sem = n * N_CHIPS + nw
    return pl.pallas_call(
        body, name=name,
        out_shape=tuple(jax.ShapeDtypeStruct((N_CHIPS,) + a.shape[2:], F32) for a in gs)
        + tuple(jax.ShapeDtypeStruct(a.shape, F32) for a in whole),
        in_specs=[ANY] * (n + nw), out_specs=(ANY,) * (n + nw),
        scratch_shapes=[pltpu.SemaphoreType.DMA((n_sem,)), pltpu.SemaphoreType.DMA((n_sem,))],
        compiler_params=COMM_PARAMS,
    )(*gs, *whole)


def _exchange_chips(ps, whole):
    n, nw = len(ps), len(whole)
    per_chip = n + nw

    def body(*refs):
        ps_refs, wh_refs = refs[:n], refs[n:per_chip]
        ss_refs, sw_refs = refs[per_chip:per_chip + n], refs[per_chip + n:2 * per_chip]
        send_sems, recv_sems = refs[2 * per_chip:]
        x, y, c = _coords()
        me = 2 * x + y
        chips = _other_chips(x, y)
        sends = []
        for j, (px, py) in enumerate(chips):
            to = (px, py, c)
            for t in range(n):
                sends.append(_remote(ps_refs[t].at[2 * px + py], ss_refs[t].at[me], send_sems, recv_sems, j * per_chip + t, to))
            for t in range(nw):
                sends.append(_remote(wh_refs[t], sw_refs[t].at[me], send_sems, recv_sems, j * per_chip + n + t, to))
        for cp in sends:
            cp.start()
        for j, (px, py) in enumerate(chips):
            k, to = 2 * px + py, (px, py, c)
            for t in range(n):
                _remote(ps_refs[t].at[me], ss_refs[t].at[k], send_sems, recv_sems, j * per_chip + t, to).wait_recv()
            for t in range(nw):
                _remote(wh_refs[t], sw_refs[t].at[k], send_sems, recv_sems, j * per_chip + n + t, to).wait_recv()
        for cp in sends:
            cp.wait_send()

    n_sem = 3 * per_chip
    return pl.pallas_call(
        body, name="exchange_chips",
        out_shape=tuple(jax.ShapeDtypeStruct(a.shape, a.dtype) for a in ps)
        + tuple(jax.ShapeDtypeStruct((N_CHIPS,) + a.shape, a.dtype) for a in whole),
        in_specs=[ANY] * per_chip, out_specs=(ANY,) * per_chip,
        scratch_shapes=[pltpu.SemaphoreType.DMA((n_sem,)), pltpu.SemaphoreType.DMA((n_sem,))],
        compiler_params=COMM_PARAMS,
    )(*ps, *whole)


def _exchange_sibling_result(gh):
    n = len(gh)

    def body(*refs):
        gh_refs, out_refs, (send_sems, recv_sems) = refs[:n], refs[n:2 * n], refs[2 * n:]
        x, y, c = _coords()
        cps = [_remote(gh_refs[t], out_refs[t], send_sems, recv_sems, t, (x, y, 1 - c)) for t in range(n)]
        for cp in cps:
            cp.start()
        for cp in cps:
            cp.wait_recv()
        for cp in cps:
            cp.wait_send()

    return pl.pallas_call(
        body, name="exchange_sibling_result",
        out_shape=tuple(jax.ShapeDtypeStruct(a.shape, F32) for a in gh),
        in_specs=[ANY] * n, out_specs=(ANY,) * n,
        scratch_shapes=[pltpu.SemaphoreType.DMA((n,)), pltpu.SemaphoreType.DMA((n,))],
        compiler_params=COMM_PARAMS,
    )(*gh)


def _add_own_half(gs, recv, c_arr, *, name):
    _, rows, cols = recv.shape

    def body(c_ref, a_ref, b_ref, o_ref):
        o_ref[0] = (a_ref[0, 0] + b_ref[0]).astype(o_ref.dtype)

    return pl.pallas_call(
        body, name=name,
        out_shape=jax.ShapeDtypeStruct(recv.shape, GRAD_WIRE_DTYPE),
        grid_spec=pltpu.PrefetchScalarGridSpec(
            num_scalar_prefetch=1, grid=(N_CHIPS,),
            in_specs=[pl.BlockSpec((1, 1, rows, cols), lambda k, c_ref: (k, c_ref[0], 0, 0)),
                      pl.BlockSpec((1, rows, cols), lambda k, c_ref: (k, 0, 0))],
            out_specs=pl.BlockSpec((1, rows, cols), lambda k, c_ref: (k, 0, 0))),
        compiler_params=_params("parallel"),
    )(c_arr, gs, recv)


def _add2(a, b, *, name, out_dtype=F32):
    def body(a_ref, b_ref, o_ref):
        o_ref[...] = (a_ref[...] + b_ref[...]).astype(o_ref.dtype)

    return pl.pallas_call(body, name=name, out_shape=jax.ShapeDtypeStruct(a.shape, out_dtype))(a, b)


def _sum_slots(slots, *, tr, name):
    _, r, c = slots.shape

    def body(s_ref, o_ref):
        f = lambda k: s_ref[k].astype(F32)
        o_ref[...] = ((f(0) + f(1)) + f(2)) + f(3)

    return pl.pallas_call(
        body, name=name,
        out_shape=jax.ShapeDtypeStruct((r, c), F32),
        grid=(r // tr,),
        in_specs=[pl.BlockSpec((N_CHIPS, tr, c), lambda i: (0, i, 0))],
        out_specs=pl.BlockSpec((tr, c), lambda i: (i, 0)),
        compiler_params=_params("parallel"),
    )(slots)


def _adamw(w, g, m, v, *, tr, name):
    r, cols = w.shape

    def body(w_ref, g_ref, m_ref, v_ref, d_ref, nm_ref, nv_ref):
        g_ = g_ref[...]
        m_ = ADAM_B1 * m_ref[...] + (1.0 - ADAM_B1) * g_
        v_ = ADAM_B2 * v_ref[...] + (1.0 - ADAM_B2) * (g_ * g_)
        m_hat = m_ / (1.0 - ADAM_B1 ** ADAM_STEP)
        v_hat = v_ / (1.0 - ADAM_B2 ** ADAM_STEP)
        d_ref[...] = -ADAM_LR * (m_hat / (jnp.sqrt(v_hat) + ADAM_EPS) + ADAM_WD * w_ref[...])
        nm_ref[...] = m_
        nv_ref[...] = v_

    spec = pl.BlockSpec((tr, cols), lambda i: (i, 0))
    out = jax.ShapeDtypeStruct((r, cols), F32)
    return pl.pallas_call(
        body, name=name, out_shape=(out, out, out), grid=(r // tr,),
        in_specs=[spec] * 4, out_specs=(spec,) * 3,
        compiler_params=_params("parallel"),
    )(w, g, m, v)


WEIGHTS = ("w_in", "g_cq", "g_ckv", "w_uq", "w_uk", "w_uv", "w_o", "ln1_g", "ln1_b", "w_up", "conv_w", "conv_b",
           "w_down", "ln2_g", "ln2_b")


def kernel(x, w_in, g_cq, g_ckv, w_uq, w_uk, w_uv, w_o, ln1_g, ln1_b, w_up, conv_w, conv_b, w_down, ln2_g, ln2_b, loss_target, m_w_in, m_g_cq, m_g_ckv, m_w_uq, m_w_uk, m_w_uv, m_w_o, m_ln1_g, m_ln1_b, m_w_up, m_conv_w, m_conv_b, m_w_down, m_ln2_g, m_ln2_b, v_w_in, v_g_cq, v_g_ckv, v_w_uq, v_w_uk, v_w_uv, v_w_o, v_ln1_g, v_ln1_b, v_w_up, v_conv_w, v_conv_b, v_w_down, v_ln2_g, v_ln2_b):
    wts = dict(zip(WEIGHTS, (w_in, g_cq, g_ckv, w_uq, w_uk, w_uv, w_o, ln1_g, ln1_b, w_up, conv_w, conv_b, w_down, ln2_g, ln2_b)))
    mom = dict(zip(WEIGHTS, (m_w_in, m_g_cq, m_g_ckv, m_w_uq, m_w_uk, m_w_uv, m_w_o, m_ln1_g, m_ln1_b, m_w_up, m_conv_w, m_conv_b, m_w_down, m_ln2_g, m_ln2_b)))
    var = dict(zip(WEIGHTS, (v_w_in, v_g_cq, v_g_ckv, v_w_uq, v_w_uk, v_w_uv, v_w_o, v_ln1_g, v_ln1_b, v_w_up, v_conv_w, v_conv_b, v_w_down, v_ln2_g, v_ln2_b)))

    me = 2 * lax.axis_index("x") + lax.axis_index("y")
    my_c = lax.axis_index("c")
    c_arr = my_c.astype(jnp.int32).reshape(1)
    own = lambda slots, mine: lax.dynamic_update_index_in_dim(slots, mine, me, 0)

    def pack(names):
        return jnp.concatenate([_rows(_mx(wts[n]), SHARD_ROWS[n]) for n in names], axis=0).reshape(2, -1, LANES)

    def unpack(names, gathered, mine):
        buf, full, r = own(gathered, mine).reshape(N_CHIPS, -1, LANES), {}, 0
        for n in names:
            full[n] = _from_chip_blocks(n, buf[:, r:r + SHARD_ROWS[n]])
            r += SHARD_ROWS[n]
        return full

    wp_first = pack(GATHER_FIRST)
    cwp = _rows(conv_w, SHARD_ROWS["conv_w"])
    gathered, cwfull = _gather_weights(wp_first, cwp)
    full = unpack(GATHER_FIRST, gathered, wp_first)
    conv_w_full = _from_chip_blocks("conv_w", own(cwfull, cwp))
    w = _prep_weights_first(full["w_in"], full["w_uq"], w_uk, w_uv)
    late_halves = [_mx(wts[n]).reshape(2, BIG_2D[n][0] // 2, BIG_2D[n][1]) for n in GATHER_LATE]

    def finish(gathered_late):
        w_o_b, w_up_b, w_down_b = (own(a, mine).reshape((N_CHIPS,) + BIG_2D[n])
                                   for a, mine, n in zip(gathered_late, late_halves, GATHER_LATE))
        return _prep_weights_late(w_o_b.reshape(D_MODEL, D_MODEL), w_up_b, w_down_b.reshape(D_FF, D_MODEL))

    def blocked(named):
        return [_blocked(n, a) for n, a in named]

    def add_halves(named, gb, recv):
        return [_add_own_half(gb[i], recv[i], c_arr, name=f"add_half_{n}") for i, (n, _) in enumerate(named)]

    def halve(named, whole, wire):
        gb = blocked(named)
        recv = _exchange_sibling_halves(gb, list(whole), name="exchange_sibling_halves")
        return add_halves(named, gb, recv) + [_add2(a, recv[len(gb) + i], name=f"add_whole_{i}", out_dtype=wire[i])
                                              for i, a in enumerate(whole)]

    comm = dict(late=late_halves, finish=finish, blocked=blocked, add_halves=add_halves)
    loss, grad_x, g = _local_step(x[0], loss_target[0], w, g_cq, g_ckv, ln1_g, ln1_b, conv_w_full, conv_b, ln2_g, ln2_b, comm=comm)

    ps_early, slots_early = g.pop("early")
    g["loss"] = loss.reshape(1)
    narrow = tuple(n for n in SMALL_G if n not in SMALL_WIDE) + ("loss",)
    narrow_shape = {**SMALL_G_SHAPE, "loss": (1,)}
    r_narrow = _padded_rows(sum(_size(narrow_shape[n]) for n in narrow), SUBLANES)
    r_wide = _padded_rows(sum(_size(SMALL_G_SHAPE[n]) for n in SMALL_WIDE), 2 * SUBLANES)
    *ps_rest, pr, pw = halve([(n, g[n]) for n in REDUCED_LAST],
                             whole=[_pack_flat(g, narrow, r_narrow), _pack_flat(g, SMALL_WIDE, r_wide)],
                             wire=[F32, GRAD_WIRE_DTYPE])
    *slots_rest, slots_r, slots_w = _exchange_chips(ps_rest, [pr, pw])
    ps = {**dict(zip(REDUCED_LAST, ps_rest)), **dict(zip(REDUCED_EARLY, ps_early))}
    slots = {**dict(zip(REDUCED_LAST, slots_rest)), **dict(zip(REDUCED_EARLY, slots_early))}
    slots = [own(slots[n], lax.dynamic_index_in_dim(ps[n], me, 0, keepdims=False)) for n in BIG]
    g_half = [_sum_slots(slots[i], tr=slots[i].shape[1] // 2, name=f"sum_chips_{n}") for i, n in enumerate(BIG)]
    g_small = {**_unpack_flat(_sum_slots(own(slots_r, pr), tr=r_narrow, name="sum_chips_narrow"), narrow, narrow_shape),
               **_unpack_flat(_sum_slots(own(slots_w, pw), tr=r_wide, name="sum_chips_wide"), SMALL_WIDE, SMALL_G_SHAPE)}
    loss = g_small.pop("loss")[0]
    g_other = _exchange_sibling_result(g_half)
    grads = {n: jnp.where(my_c == 0, jnp.concatenate([g_half[i], g_other[i]]), jnp.concatenate([g_other[i], g_half[i]]))
             for i, n in enumerate(BIG)}
    g_small["conv_w"] = lax.dynamic_slice_in_dim(g_small["conv_w"], me * SHARD_SHAPE["conv_w"][1], SHARD_SHAPE["conv_w"][1], 1)
    grads.update(g_small)

    res = {}
    for n in BIG:
        as2d = lambda a: a.reshape(BIG_2D[n])
        d, m, v = _adamw(as2d(wts[n]), grads[n], as2d(mom[n]), as2d(var[n]), tr=BIG_2D[n][0] // 4, name=f"adamw_{n}")
        res[n] = [a.reshape(SHARD_SHAPE[n]) for a in (grads[n], d, m, v)]
    flat = lambda t: _pack_flat(t, SMALL_G)
    dmv = _adamw(flat(wts), flat(g_small), flat(mom), flat(var), tr=R_SMALL, name="adamw_small")
    dmv = [_unpack_flat(a, SMALL_G, SMALL_U_SHAPE) for a in dmv]
    for n in SMALL_G:
        res[n] = [g_small[n]] + [t[n] for t in dmv]
    outs = [res[n][j] for j in range(4) for n in WEIGHTS]
    return (loss, grad_x[None], *outs)
```

```python
import math

import jax
import jax.numpy as jnp
from jax import lax
from jax.experimental import pallas as pl
from jax.experimental.pallas import tpu as pltpu

F32 = jnp.float32
MXU_DTYPE = jnp.bfloat16
GRAD_WIRE_DTYPE = jnp.bfloat16
NEG = -1e30

D_MODEL = 1024
HEADS = 8
HEAD_DIM = 64
Q_RANK = 256
KV_RANK = 128
NOPE = 64
ROPE = 32
QK_PAD = 128
IN_WIDTH = 1952
IN_EXT = 2048
D_FF = 2816
DIL_PAIRS = ((128, 1), (512, 4), (2048, 16))
DIL_BLOCK = 128
ROPE_THETA = 10000.0
DN_ALPHA = 2.0 ** 0.25
LN_EPS = 1e-5
RMS_EPS = 1e-6
MLA_SCALE = 1.0 / math.sqrt(NOPE + ROPE)
LOG2_E = math.log2(math.e)
DIL_SCALE = 1.0 / math.sqrt(HEAD_DIM)

ADAM_LR = 0.001
ADAM_B1 = 0.9
ADAM_B2 = 0.999
ADAM_EPS = 1e-08
ADAM_WD = 0.01
ADAM_STEP = 10

LANES = 128
SUBLANES = 8
VMEM_LIMIT_BYTES = 56 * 1024 * 1024
DW_TOKENS = 2048

MESH = pl.DeviceIdType.MESH


def _params(*sem):
    return pltpu.CompilerParams(dimension_semantics=sem, vmem_limit_bytes=VMEM_LIMIT_BYTES)


def _dot(a, b):
    return jnp.dot(a, b, preferred_element_type=F32)


def _dot_nt(a, b):
    return lax.dot_general(a, b, (((1,), (1,)), ((), ())), preferred_element_type=F32)


def _dot_tn(a, b):
    return lax.dot_general(a, b, (((0,), (0,)), ((), ())), preferred_element_type=F32)


def _mx(a):
    return a.astype(MXU_DTYPE)


def _mm_nn(a, b, *, name, tm, tn, tk, out_dtype=F32, add=None, add_scale=1.0):
    m, kdim = a.shape
    blocked = b.ndim == 3
    n = b.shape[0] * b.shape[2] if blocked else b.shape[1]
    nk = kdim // tk

    def body(*refs):
        if add is None:
            a_ref, b_ref, o_ref, acc = refs
        else:
            a_ref, b_ref, c_ref, o_ref, acc = refs
        k = pl.program_id(2)

        @pl.when(k == 0)
        def _():
            acc[...] = jnp.zeros_like(acc)

        acc[...] += _dot(_mx(a_ref[...]), _mx(b_ref[...]))

        @pl.when(k == nk - 1)
        def _():
            r = acc[...]
            if add is not None:
                r = r + add_scale * c_ref[...]
            o_ref[...] = r.astype(out_dtype)

    b_spec = (pl.BlockSpec((None, tk, tn), lambda i, j, k: (j, k, 0)) if blocked
              else pl.BlockSpec((tk, tn), lambda i, j, k: (k, j)))
    in_specs = [pl.BlockSpec((tm, tk), lambda i, j, k: (i, k)), b_spec]
    args = [a, b]
    if add is not None:
        in_specs.append(pl.BlockSpec((tm, tn), lambda i, j, k: (i, j)))
        args.append(add)
    return pl.pallas_call(
        body, name=name,
        out_shape=jax.ShapeDtypeStruct((m, n), out_dtype),
        grid=(m // tm, n // tn, nk),
        in_specs=in_specs,
        out_specs=pl.BlockSpec((tm, tn), lambda i, j, k: (i, j)),
        scratch_shapes=[pltpu.VMEM((tm, tn), F32)],
        compiler_params=_params("parallel", "parallel", "arbitrary"),
    )(*args)


def _mm_tn(a, b, *, name, tm, tn, ts, out_dtype=F32):
    s, m = a.shape
    n = b.shape[1]
    ns = s // ts

    def body(a_ref, b_ref, o_ref, acc):
        k = pl.program_id(2)

        @pl.when(k == 0)
        def _():
            acc[...] = jnp.zeros_like(acc)

        acc[...] += _dot_tn(_mx(a_ref[...]), _mx(b_ref[...]))

        @pl.when(k == ns - 1)
        def _():
            o_ref[...] = acc[...].astype(out_dtype)

    return pl.pallas_call(
        body, name=name,
        out_shape=jax.ShapeDtypeStruct((m, n), out_dtype),
        grid=(m // tm, n // tn, ns),
        in_specs=[pl.BlockSpec((ts, tm), lambda i, j, k: (k, i)),
                  pl.BlockSpec((ts, tn), lambda i, j, k: (k, j))],
        out_specs=pl.BlockSpec((tm, tn), lambda i, j, k: (i, j)),
        scratch_shapes=[pltpu.VMEM((tm, tn), F32)],
        compiler_params=_params("parallel", "parallel", "arbitrary"),
    )(a, b)


def _in_proj(x, w_in_ext, *, tm):
    s = x.shape[0]
    mla_w = 4 * LANES
    dil_w = HEADS * HEAD_DIM
    dils = [d for _, d in DIL_PAIRS]

    def body(x_ref, w_ref, h_ref, *rest):
        outs, sc = rest[:-1], rest[-1]
        xb = _mx(x_ref[...])
        h_ref[...] = _dot(xb, w_ref[:, 0:mla_w])
        for j in range(3):
            part = _dot(xb, w_ref[:, mla_w + j * dil_w:mla_w + (j + 1) * dil_w])
            for hd in range(HEADS):
                sc[hd] = part[:, hd * HEAD_DIM:(hd + 1) * HEAD_DIM]
            for b, d in enumerate(dils):
                _store_residue_major(outs[3 * j + b], sc, d, tm)

    shapes, specs = _residue_major_outs(s, tm, dils, MXU_DTYPE)
    res = pl.pallas_call(
        body, name="in_proj",
        out_shape=(jax.ShapeDtypeStruct((s, mla_w), F32),) + shapes * 3,
        grid=(s // tm,),
        in_specs=[pl.BlockSpec((tm, D_MODEL), lambda i: (i, 0)), pl.BlockSpec((D_MODEL, IN_EXT), lambda i: (0, 0))],
        out_specs=(pl.BlockSpec((tm, mla_w), lambda i: (i, 0)),) + specs * 3,
        scratch_shapes=[pltpu.VMEM((HEADS, tm, HEAD_DIM), F32)],
        compiler_params=_params("parallel"),
    )(x, w_in_ext)
    hm = lambda a: a.reshape(HEADS, s, HEAD_DIM)
    return res[0], [hm(a) for a in res[1:4]], [hm(a) for a in res[4:7]], [hm(a) for a in res[7:10]]


def _residue_major_outs(s, tm, dils, dtype):
    shapes, specs = [], []
    for d in dils:
        if d == 1:
            shapes.append(jax.ShapeDtypeStruct((HEADS, s, HEAD_DIM), dtype))
            specs.append(pl.BlockSpec((HEADS, tm, HEAD_DIM), lambda i: (0, i, 0)))
        else:
            shapes.append(jax.ShapeDtypeStruct((HEADS, d, s // d, HEAD_DIM), dtype))
            specs.append(pl.BlockSpec((HEADS, d, tm // d, HEAD_DIM), lambda i: (0, 0, i, 0)))
    return tuple(shapes), tuple(specs)


def _store_residue_major(o_ref, src_ref, d, tm):
    if d == 1:
        o_ref[...] = src_ref[...].astype(o_ref.dtype)
    else:
        for r in range(d):
            o_ref[:, r] = src_ref[:, pl.ds(r, tm // d, stride=d), :].astype(o_ref.dtype)


def _load_token_order(dst_ref, src_ref, d, tm, accumulate=False):
    if d == 1:
        dst_ref[...] = dst_ref[...] + src_ref[...] if accumulate else src_ref[...]
    else:
        for r in range(d):
            rows = pl.ds(r, tm // d, stride=d)
            dst_ref[:, rows, :] = dst_ref[:, rows, :] + src_ref[:, r] if accumulate else src_ref[:, r]


def _attn_bwd_heads(dz1, w_o_t, a_mla, a_dil, *, tm, swap=()):
    s = dz1.shape[0]
    half = HEADS * HEAD_DIM
    dils = [d for _, d in DIL_PAIRS]
    nsw = len(swap)
    n_steps = s // tm

    def body(*refs):
        dz_ref, w_ref, am_ref, ad_ref = refs[:4]
        gs_refs = refs[4:4 + nsw]
        dom_ref, dd_ref = refs[4 + nsw:6 + nsw]
        dod_refs = refs[6 + nsw:6 + nsw + len(dils)]
        os_refs = refs[6 + nsw + len(dils):6 + 2 * nsw + len(dils)]
        if nsw:
            send_sems, recv_sems = refs[6 + 2 * nsw + len(dils):]
            i = pl.program_id(0)
            _swap_halves_in_steps(gs_refs, os_refs, send_sems, recv_sems, first=i == 0, last=i == n_steps - 1)
        dzb = _mx(dz_ref[...])
        for j, (a_ref, o_ref) in enumerate(((am_ref, dom_ref), (ad_ref, dod_refs[0]))):
            da = _dot(dzb, w_ref[:, j * half:(j + 1) * half])
            prod = da * a_ref[...]
            for hd in range(HEADS):
                sl = slice(hd * HEAD_DIM, (hd + 1) * HEAD_DIM)
                o_ref[hd] = da[:, sl].astype(o_ref.dtype)
                dd_ref[:, j * HEADS + hd:j * HEADS + hd + 1] = jnp.sum(prod[:, sl], axis=-1, keepdims=True)
        for b, d in enumerate(dils[1:]):
            _store_residue_major(dod_refs[1 + b], dod_refs[0], d, tm)

    hspec = pl.BlockSpec((HEADS, tm, HEAD_DIM), lambda i: (0, i, 0))
    row = lambda w: pl.BlockSpec((tm, w), lambda i: (i, 0))
    shapes, specs = _residue_major_outs(s, tm, dils, F32)
    n_sem = nsw * N_CHIPS
    do_mla, dd, *rest = pl.pallas_call(
        body, name="attn_bwd_heads",
        out_shape=(jax.ShapeDtypeStruct((HEADS, s, HEAD_DIM), MXU_DTYPE), jax.ShapeDtypeStruct((s, 2 * HEADS), F32)) + shapes
        + tuple(jax.ShapeDtypeStruct((N_CHIPS,) + a.shape[2:], F32) for a in swap),
        grid=(n_steps,),
        in_specs=[row(D_MODEL), pl.BlockSpec((D_MODEL, D_MODEL), lambda i: (0, 0)), row(half), row(half)] + [ANY] * nsw,
        out_specs=(hspec, row(2 * HEADS)) + specs + (ANY,) * nsw,
        scratch_shapes=[pltpu.SemaphoreType.DMA((n_sem,)), pltpu.SemaphoreType.DMA((n_sem,))] if nsw else [],
        compiler_params=pltpu.CompilerParams(dimension_semantics=("arbitrary",), vmem_limit_bytes=VMEM_LIMIT_BYTES,
                                             has_side_effects=nsw > 0),
    )(dz1, w_o_t, a_mla, a_dil, *swap)
    do_dil, received = rest[:len(dils)], rest[len(dils):]
    return do_mla, [a.reshape(HEADS, s, HEAD_DIM) for a in do_dil], dd, received


def _dil_merge(parts, *, ts):
    hds, s, e = parts[0][0].shape
    dils = [d for _, d in DIL_PAIRS]

    def body(*refs):
        o_ref, sc = refs[9], refs[10]
        for j in range(3):
            for b, d in enumerate(dils):
                _load_token_order(sc, refs[3 * b + j], d, ts, accumulate=b > 0)
            tot = sc[...]
            for hd in range(hds):
                col = j * hds * e + hd * e
                o_ref[:, col:col + e] = tot[hd].astype(o_ref.dtype)

    _, specs = _residue_major_outs(s, ts, dils, F32)
    view = lambda a, d: a if d == 1 else a.reshape(hds, d, s // d, e)
    return pl.pallas_call(
        body, name="dil_merge",
        out_shape=jax.ShapeDtypeStruct((s, 3 * hds * e), MXU_DTYPE),
        grid=(s // ts,),
        in_specs=[specs[b] for b in range(3) for _ in range(3)],
        out_specs=pl.BlockSpec((ts, 3 * hds * e), lambda i: (i, 0)),
        scratch_shapes=[pltpu.VMEM((hds, ts, e), F32)],
        compiler_params=_params("parallel"),
    )(*[view(parts[b][j], dils[b]) for b in range(3) for j in range(3)])


def _rope_tables(s):
    half = ROPE // 2
    freqs = ROPE_THETA ** (-jnp.arange(half, dtype=F32) / half)
    ang = jnp.arange(s).astype(F32)[:, None] * freqs[None, :]
    cos, sin = jnp.cos(ang), jnp.sin(ang)
    z = lambda w: jnp.zeros((s, w), F32)
    c = jnp.concatenate([jnp.ones((s, NOPE), F32), cos, cos, z(32)], axis=1)
    s1 = jnp.concatenate([z(NOPE + half), sin, z(32)], axis=1)
    s2 = jnp.concatenate([z(NOPE), -sin, z(half + 32)], axis=1)
    mask = jnp.concatenate([z(NOPE), jnp.ones((s, ROPE), F32), z(32)], axis=1)
    return c, s1, s2, mask


def _rope(x, c, s1, s2):
    return x * c + pltpu.roll(x, 16, 1) * s1 + pltpu.roll(x, LANES - 16, 1) * s2


def _unrope(dy, c, s1, s2):
    return dy * c + pltpu.roll(dy * s1, LANES - 16, 1) + pltpu.roll(dy * s2, 16, 1)


def _rms(x):
    r = lax.rsqrt(jnp.mean(x * x, axis=-1, keepdims=True) + RMS_EPS)
    return x * r, r


def _mla_prep_fwd(h, g_cq, g_ckv, wq, wk, wv, wv_t, tabs, *, tm):
    s = h.shape[0]
    c_t, s1_t, s2_t, _ = tabs

    def body(h_ref, gq_ref, gkv_ref, wq_ref, wk_ref, wv_ref, wvt_ref, c_ref, s1_ref, s2_ref,
             q_ref, k_ref, v_ref, vt_ref):
        cq = h_ref[:, 0:Q_RANK]
        ckv = h_ref[:, Q_RANK:Q_RANK + KV_RANK]
        kr = h_ref[:, Q_RANK + KV_RANK:Q_RANK + KV_RANK + QK_PAD]
        c, s1, s2 = c_ref[...], s1_ref[...], s2_ref[...]
        cqn = _mx(_rms(cq)[0] * gq_ref[...])
        ckvn = _mx(_rms(ckv)[0] * gkv_ref[...])
        kr_rot = _rope(kr, c, s1, s2)
        for hd in range(HEADS):
            q_ref[hd] = _rope(_dot(cqn, wq_ref[hd]), c, s1, s2).astype(q_ref.dtype)
            k_ref[hd] = (_dot(ckvn, wk_ref[hd]) + kr_rot).astype(k_ref.dtype)
            v_ref[hd] = _dot(ckvn, wv_ref[hd]).astype(v_ref.dtype)
            vt_ref[hd] = _dot_nt(wvt_ref[hd], ckvn).astype(vt_ref.dtype)

    full = lambda shp: pl.BlockSpec(shp, lambda i: (0,) * len(shp))
    row = lambda w: pl.BlockSpec((tm, w), lambda i: (i, 0))
    return pl.pallas_call(
        body, name="mla_prep_fwd",
        out_shape=(jax.ShapeDtypeStruct((HEADS, s, QK_PAD), MXU_DTYPE),
                   jax.ShapeDtypeStruct((HEADS, s, QK_PAD), MXU_DTYPE),
                   jax.ShapeDtypeStruct((HEADS, s, HEAD_DIM), MXU_DTYPE),
                   jax.ShapeDtypeStruct((HEADS, HEAD_DIM, s), MXU_DTYPE)),
        grid=(s // tm,),
        in_specs=[row(4 * LANES), full((1, Q_RANK)), full((1, KV_RANK)),
                  full((HEADS, Q_RANK, QK_PAD)), full((HEADS, KV_RANK, QK_PAD)), full((HEADS, KV_RANK, HEAD_DIM)),
                  full((HEADS, HEAD_DIM, KV_RANK)), row(LANES), row(LANES), row(LANES)],
        out_specs=(pl.BlockSpec((HEADS, tm, QK_PAD), lambda i: (0, i, 0)),
                   pl.BlockSpec((HEADS, tm, QK_PAD), lambda i: (0, i, 0)),
                   pl.BlockSpec((HEADS, tm, HEAD_DIM), lambda i: (0, i, 0)),
                   pl.BlockSpec((HEADS, HEAD_DIM, tm), lambda i: (0, 0, i))),
        compiler_params=_params("parallel"),
    )(h, g_cq, g_ckv, wq, wk, wv, wv_t, c_t, s1_t, s2_t)


def _mla_prep_bwd(h, dq, dk, dv, g_cq, g_ckv, wq_t, wk_t, wv_t, tabs, *, tm):
    s = h.shape[0]
    c_t, s1_t, s2_t, mask_t = tabs

    def body(h_ref, dq_ref, dk_ref, dv_ref, gq_ref, gkv_ref, wqt_ref, wkt_ref, wvt_ref,
             c_ref, s1_ref, s2_ref, mask_ref, dh_ref, dwq_ref, dwk_ref, dwv_ref, dgq_ref, dgkv_ref):
        i = pl.program_id(0)

        @pl.when(i == 0)
        def _():
            dwq_ref[...] = jnp.zeros_like(dwq_ref)
            dwk_ref[...] = jnp.zeros_like(dwk_ref)
            dwv_ref[...] = jnp.zeros_like(dwv_ref)
            dgq_ref[...] = jnp.zeros_like(dgq_ref)
            dgkv_ref[...] = jnp.zeros_like(dgkv_ref)

        cq = h_ref[:, 0:Q_RANK]
        ckv = h_ref[:, Q_RANK:Q_RANK + KV_RANK]
        c, s1, s2 = c_ref[...], s1_ref[...], s2_ref[...]
        cqh, rq = _rms(cq)
        ckvh, rkv = _rms(ckv)
        gq, gkv = gq_ref[...], gkv_ref[...]
        cqn = _mx(cqh * gq)
        ckvn = _mx(ckvh * gkv)
        dcqn = jnp.zeros((tm, Q_RANK), F32)
        dckvn = jnp.zeros((tm, KV_RANK), F32)
        dkr = jnp.zeros((tm, QK_PAD), F32)
        for hd in range(HEADS):
            dqh = _mx(_unrope(dq_ref[hd], c, s1, s2))
            dcqn = dcqn + _dot(dqh, wqt_ref[hd])
            dwq_ref[hd] += _dot_tn(cqn, dqh)
            dkh = dk_ref[hd]
            dkr = dkr + dkh
            dkh = _mx(dkh)
            dckvn = dckvn + _dot(dkh, wkt_ref[hd])
            dwk_ref[hd] += _dot_tn(ckvn, dkh)
            dvh = _mx(dv_ref[hd])
            dckvn = dckvn + _dot(dvh, wvt_ref[hd])
            dwv_ref[hd] += _dot_tn(ckvn, dvh)
        dgq_ref[...] += jnp.sum(dcqn * cqh, axis=0, keepdims=True)
        dgkv_ref[...] += jnp.sum(dckvn * ckvh, axis=0, keepdims=True)
        gd = dcqn * gq
        dh_ref[:, 0:Q_RANK] = rq * (gd - cqh * jnp.mean(gd * cqh, axis=-1, keepdims=True))
        gd = dckvn * gkv
        dh_ref[:, Q_RANK:Q_RANK + KV_RANK] = rkv * (gd - ckvh * jnp.mean(gd * ckvh, axis=-1, keepdims=True))
        dh_ref[:, Q_RANK + KV_RANK:Q_RANK + KV_RANK + QK_PAD] = _unrope(dkr, c, s1, s2) * mask_ref[...]

    full = lambda shp: pl.BlockSpec(shp, lambda i: (0,) * len(shp))
    row = lambda w: pl.BlockSpec((tm, w), lambda i: (i, 0))
    hrow = lambda w: pl.BlockSpec((HEADS, tm, w), lambda i: (0, i, 0))
    return pl.pallas_call(
        body, name="mla_prep_bwd",
        out_shape=(jax.ShapeDtypeStruct((s, 4 * LANES), F32),
                   jax.ShapeDtypeStruct((HEADS, Q_RANK, QK_PAD), F32),
                   jax.ShapeDtypeStruct((HEADS, KV_RANK, QK_PAD), F32),
                   jax.ShapeDtypeStruct((HEADS, KV_RANK, HEAD_DIM), F32),
                   jax.ShapeDtypeStruct((1, Q_RANK), F32),
                   jax.ShapeDtypeStruct((1, KV_RANK), F32)),
        grid=(s // tm,),
        in_specs=[row(4 * LANES), hrow(QK_PAD), hrow(QK_PAD), hrow(HEAD_DIM),
                  full((1, Q_RANK)), full((1, KV_RANK)),
                  full((HEADS, QK_PAD, Q_RANK)), full((HEADS, QK_PAD, KV_RANK)), full((HEADS, HEAD_DIM, KV_RANK)),
                  row(LANES), row(LANES), row(LANES), row(LANES)],
        out_specs=(row(4 * LANES), full((HEADS, Q_RANK, QK_PAD)), full((HEADS, KV_RANK, QK_PAD)),
                   full((HEADS, KV_RANK, HEAD_DIM)), full((1, Q_RANK)), full((1, KV_RANK))),
        compiler_params=_params("arbitrary"),
    )(h, dq, dk, dv, g_cq, g_ckv, wq_t, wk_t, wv_t, c_t, s1_t, s2_t, mask_t)


def _bdot(a, b, ca, cb):
    return lax.dot_general(a, b, (((ca,), (cb,)), ((0,), (0,))), preferred_element_type=F32)


def _causal_mask_t(t):
    kk = lax.broadcasted_iota(jnp.int32, (t, t), 0)
    qq = lax.broadcasted_iota(jnp.int32, (t, t), 1)
    return (qq >= kk)[None]


def _mla_attn_fwd(q, k, v_t, *, t, g, late=None):
    hds, s, _ = q.shape
    n = s // t
    n_groups = hds // g

    nl = 0 if late is None else len(late)

    def body(*refs):
        q_ref, k_ref, vt_ref = refs[:3]
        wp_refs = refs[3:3 + nl]
        o_ref, lse_ref = refs[3 + nl:5 + nl]
        wout_refs = refs[5 + nl:5 + 2 * nl]
        m_sc, l_sc, acc_sc = refs[5 + 2 * nl:8 + 2 * nl]
        hg, qi, ki = pl.program_id(0), pl.program_id(1), pl.program_id(2)
        if nl:
            send_sems, recv_sems = refs[8 + 2 * nl:]
            tail = jnp.logical_and(hg == n_groups - 1, qi == n - 1)
            _gather_in_steps(wp_refs, wout_refs, send_sems, recv_sems,
                             first=jnp.logical_and(hg == 0, jnp.logical_and(qi == 0, ki == 0)),
                             mid=jnp.logical_and(tail, ki == 0), last=jnp.logical_and(tail, ki == n - 1))

        @pl.when(ki == 0)
        def _():
            m_sc[...] = jnp.full_like(m_sc, NEG)
            l_sc[...] = jnp.zeros_like(l_sc)
            acc_sc[...] = jnp.zeros_like(acc_sc)

        def step(masked):
            sc = _bdot(k_ref[...], q_ref[...], 2, 2)
            if masked:
                sc = jnp.where(_causal_mask_t(t), sc, NEG)
            m_prev = m_sc[...]
            m_new = jnp.maximum(m_prev, jnp.max(sc, axis=1, keepdims=True))
            p = jnp.exp2((sc - m_new) * (MLA_SCALE * LOG2_E))
            a = jnp.exp2((m_prev - m_new) * (MLA_SCALE * LOG2_E))
            l_sc[...] = a * l_sc[...] + jnp.sum(p, axis=1, keepdims=True)
            acc_sc[...] = a * acc_sc[...] + _bdot(vt_ref[...], _mx(p), 2, 1)
            m_sc[...] = m_new

        @pl.when(ki < qi)
        def _():
            step(False)

        @pl.when(ki == qi)
        def _():
            step(True)
            o_ref[...] = acc_sc[...] / l_sc[...]
            lse_ref[...] = m_sc[...] * MLA_SCALE + jnp.log(l_sc[...])

    qspec = pl.BlockSpec((g, t, QK_PAD), lambda h, i, j: (h, i, 0))
    kspec = pl.BlockSpec((g, t, QK_PAD), lambda h, i, j: (h, jnp.minimum(i, j), 0))
    vspec = pl.BlockSpec((g, HEAD_DIM, t), lambda h, i, j: (h, 0, jnp.minimum(i, j)))
    out_shape = [jax.ShapeDtypeStruct((hds, HEAD_DIM, s), F32), jax.ShapeDtypeStruct((hds, 1, s), F32)]
    in_specs = [qspec, kspec, vspec]
    out_specs = [pl.BlockSpec((g, HEAD_DIM, t), lambda h, i, j: (h, 0, i)), pl.BlockSpec((g, 1, t), lambda h, i, j: (h, 0, i))]
    scratch = [pltpu.VMEM((g, 1, t), F32), pltpu.VMEM((g, 1, t), F32), pltpu.VMEM((g, HEAD_DIM, t), F32)]
    args = [q, k, v_t]
    if nl:
        out_shape += [jax.ShapeDtypeStruct((N_CHIPS,) + a.shape, a.dtype) for a in late]
        in_specs += [ANY] * nl
        out_specs += [ANY] * nl
        scratch += [pltpu.SemaphoreType.DMA((6 * nl,)), pltpu.SemaphoreType.DMA((6 * nl,))]
        args += list(late)
    return pl.pallas_call(
        body, name="mla_attn_fwd",
        out_shape=tuple(out_shape), grid=(n_groups, n, n),
        in_specs=in_specs, out_specs=tuple(out_specs), scratch_shapes=scratch,
        compiler_params=pltpu.CompilerParams(dimension_semantics=("arbitrary",) * 3, vmem_limit_bytes=VMEM_LIMIT_BYTES,
                                             has_side_effects=nl > 0),
    )(*args)


def _mla_attn_bwd(q, k, v, do, lse, dd, *, t, g, early=()):
    hds, s, _ = q.shape
    n = s // t
    n_groups = hds // g
    ne = len(early)

    def body(*refs):
        q_ref, k_ref, v_ref, do_ref, lse_ref, dd_ref = refs[:6]
        ps_refs = refs[6:6 + ne]
        dq_ref, dk_ref, dv_ref = refs[6 + ne:9 + ne]
        ss_refs = refs[9 + ne:9 + 2 * ne]
        dq_sc, dk_sc, dv_sc = refs[9 + 2 * ne:12 + 2 * ne]
        hg, ki, qi = pl.program_id(0), pl.program_id(1), pl.program_id(2)
        if ne:
            send_sems, recv_sems = refs[12 + 2 * ne:]
            _exchange_in_steps(ps_refs, ss_refs, send_sems, recv_sems,
                               first=jnp.logical_and(hg == 0, jnp.logical_and(ki == 0, qi == 0)),
                               last=jnp.logical_and(hg == n_groups - 1, jnp.logical_and(ki == n - 1, qi == n - 1)))

        @pl.when(jnp.logical_and(ki == 0, qi == 0))
        def _():
            dq_sc[...] = jnp.zeros_like(dq_sc)

        @pl.when(qi == 0)
        def _():
            dk_sc[...] = jnp.zeros_like(dk_sc)
            dv_sc[...] = jnp.zeros_like(dv_sc)

        def step(masked):
            qb, kb, dob = q_ref[...], k_ref[...], do_ref[...]
            sc = _bdot(kb, qb, 2, 2) * MLA_SCALE
            if masked:
                sc = jnp.where(_causal_mask_t(t), sc, NEG)
            p = jnp.exp(sc - lse_ref[...])
            dv_sc[...] += _bdot(_mx(p), dob, 2, 1)
            dp = _bdot(v_ref[...], dob, 2, 2)
            ds = _mx(p * (dp - dd_ref[...]) * MLA_SCALE)
            dk_sc[...] += _bdot(ds, qb, 2, 1)
            dq_sc[qi] += _bdot(ds, kb, 1, 1)

        @pl.when(qi == ki)
        def _():
            step(True)

        @pl.when(qi > ki)
        def _():
            step(False)

        @pl.when(qi == n - 1)
        def _():
            dk_ref[...] = dk_sc[...]
            dv_ref[...] = dv_sc[...]

        @pl.when(jnp.logical_and(ki == n - 1, qi == n - 1))
        def _():
            for j in range(n):
                dq_ref[:, j * t:(j + 1) * t, :] = dq_sc[j]

    qs = lambda w: pl.BlockSpec((g, t, w), lambda h, j, i: (h, jnp.maximum(i, j), 0))
    ks = lambda w: pl.BlockSpec((g, t, w), lambda h, j, i: (h, j, 0))
    rowq = pl.BlockSpec((g, 1, t), lambda h, j, i: (h, 0, jnp.maximum(i, j)))
    scratch = [pltpu.VMEM((n, g, t, QK_PAD), F32), pltpu.VMEM((g, t, QK_PAD), F32), pltpu.VMEM((g, t, HEAD_DIM), F32)]
    if ne:
        scratch += [pltpu.SemaphoreType.DMA((3 * ne,)), pltpu.SemaphoreType.DMA((3 * ne,))]
    return pl.pallas_call(
        body, name="mla_attn_bwd",
        out_shape=(jax.ShapeDtypeStruct((hds, s, QK_PAD), F32), jax.ShapeDtypeStruct((hds, s, QK_PAD), F32),
                   jax.ShapeDtypeStruct((hds, s, HEAD_DIM), F32)) + tuple(jax.ShapeDtypeStruct(a.shape, a.dtype) for a in early),
        grid=(n_groups, n, n),
        in_specs=[qs(QK_PAD), ks(QK_PAD), ks(HEAD_DIM), qs(HEAD_DIM), rowq, rowq] + [ANY] * ne,
        out_specs=(pl.BlockSpec((g, s, QK_PAD), lambda h, j, i: (h, 0, 0)), ks(QK_PAD), ks(HEAD_DIM)) + (ANY,) * ne,
        scratch_shapes=scratch,
        compiler_params=pltpu.CompilerParams(dimension_semantics=("arbitrary",) * 3, vmem_limit_bytes=VMEM_LIMIT_BYTES,
                                             has_side_effects=ne > 0),
    )(q, k, v, do, lse, dd, *early)


def _perm_row(a, dil):
    if dil == 1:
        return a
    hds, _, s = a.shape
    return a.reshape(hds, s // dil, dil).transpose(0, 2, 1).reshape(hds, 1, s)


def _unperm_row(a, dil):
    if dil == 1:
        return a
    hds, _, s = a.shape
    return a.reshape(hds, dil, s // dil).transpose(0, 2, 1).reshape(hds, 1, s)


def _dil_bias(dil):
    slopes = 2.0 ** (-8.0 * jnp.arange(1, HEADS + 1, dtype=F32) / HEADS)
    ik = jnp.arange(DIL_BLOCK)[:, None]
    iq = jnp.arange(DIL_BLOCK)[None, :]
    off_c = iq - ik
    off_p = iq - ik + DIL_BLOCK
    b_c = -slopes[:, None, None] * (off_c * dil).astype(F32)[None]
    b_p = -slopes[:, None, None] * (off_p * dil).astype(F32)[None]
    b_c = jnp.where((off_c >= 0)[None], b_c, NEG)
    b_p = jnp.where((off_p <= DIL_BLOCK)[None], b_p, NEG)
    return b_c, b_p


def _dil_fwd(q, k, v, dil, *, name):
    hds, s, e = q.shape
    blk = DIL_BLOCK
    nblk = s // blk
    nb = nblk // dil
    pair = 2 if nb % 2 == 0 else 1
    b_c, b_p = _dil_bias(dil)

    def body(q_ref, k_ref, kp_ref, v_ref, vp_ref, bc_ref, bp_ref, o_ref, lse_ref):
        first = ((pair * pl.program_id(0)) % nb) == 0
        bc, bp = bc_ref[...], bp_ref[...]
        for j in range(pair):
            rows = slice(j * blk, (j + 1) * blk)
            qb = q_ref[:, rows, :]
            if j == 0:
                kp, vp = kp_ref[...], vp_ref[...]
            else:
                kp, vp = k_ref[:, (j - 1) * blk:j * blk, :], v_ref[:, (j - 1) * blk:j * blk, :]
            s_c = _bdot(k_ref[:, rows, :], qb, 2, 2) * DIL_SCALE + bc
            s_p = _bdot(kp, qb, 2, 2) * DIL_SCALE + bp
            if j == 0:
                s_p = jnp.where(first, NEG, s_p)
            m = jnp.maximum(jnp.max(s_c, axis=1, keepdims=True), jnp.max(s_p, axis=1, keepdims=True))
            p_c = jnp.exp(s_c - m)
            p_p = jnp.exp(s_p - m)
            l = jnp.sum(p_c, axis=1, keepdims=True) + jnp.sum(p_p, axis=1, keepdims=True)
            o = _bdot(_mx(p_c), v_ref[:, rows, :], 1, 1) + _bdot(_mx(p_p), vp, 1, 1)
            o_ref[:, rows, :] = o / jnp.swapaxes(l, 1, 2)
            lse_ref[:, :, rows] = m + jnp.log(l)

    cur = lambda w: pl.BlockSpec((hds, pair * blk, w), lambda b: (0, b, 0))
    prev = lambda w: pl.BlockSpec((hds, blk, w), lambda b: (0, jnp.maximum(pair * b - 1, 0), 0))
    bias = pl.BlockSpec((hds, blk, blk), lambda b: (0, 0, 0))
    return pl.pallas_call(
        body, name=name,
        out_shape=(jax.ShapeDtypeStruct((hds, s, e), F32), jax.ShapeDtypeStruct((hds, 1, s), F32)),
        grid=(nblk // pair,),
        in_specs=[cur(e), cur(e), prev(e), cur(e), prev(e), bias, bias],
        out_specs=(cur(e), pl.BlockSpec((hds, 1, pair * blk), lambda b: (0, 0, b))),
        compiler_params=_params("parallel"),
    )(q, k, k, v, v, b_c, b_p)


def _dil_combine(os_, lses, *, ts):
    hds, s, e = os_[0].shape
    dils = [d for _, d in DIL_PAIRS]

    def body(o0, o1, o2, l0, l1, l2, o_ref, l_ref, sc1, sc2):
        _load_token_order(sc1, o1, dils[1], ts)
        _load_token_order(sc2, o2, dils[2], ts)
        a0, a1, a2 = l0[...], l1[...], l2[...]
        m = jnp.maximum(jnp.maximum(a0, a1), a2)
        e0, e1, e2 = jnp.exp(a0 - m), jnp.exp(a1 - m), jnp.exp(a2 - m)
        tot = e0 + e1 + e2
        col = lambda w: jnp.swapaxes(w, 1, 2)
        res = (col(e0 / tot) * o0[...] + col(e1 / tot) * sc1[...]) + col(e2 / tot) * sc2[...]
        for hd in range(hds):
            o_ref[:, hd * e:(hd + 1) * e] = res[hd]
        l_ref[...] = m + jnp.log(tot)

    _, specs = _residue_major_outs(s, ts, dils, F32)
    view = lambda a, d: a if d == 1 else a.reshape(hds, d, s // d, e)
    rspec = pl.BlockSpec((hds, 1, ts), lambda i: (0, 0, i))
    return pl.pallas_call(
        body, name="dil_combine",
        out_shape=(jax.ShapeDtypeStruct((s, hds * e), F32), jax.ShapeDtypeStruct((hds, 1, s), F32)),
        grid=(s // ts,),
        in_specs=list(specs) + [rspec] * 3,
        out_specs=(pl.BlockSpec((ts, hds * e), lambda i: (i, 0)), rspec),
        scratch_shapes=[pltpu.VMEM((hds, ts, e), F32), pltpu.VMEM((hds, ts, e), F32)],
        compiler_params=_params("parallel"),
    )(*[view(a, d) for a, d in zip(os_, dils)], *lses)


def _dil_bwd(q, k, v, do, lj, dd, dil, *, name):
    hds, s, e = q.shape
    blk = DIL_BLOCK
    nblk = s // blk
    nb = nblk // dil
    pair = 2 if nb % 2 == 0 else 1
    b_c, b_p = _dil_bias(dil)

    def body(q_ref, qn_ref, k_ref, kp_ref, v_ref, vp_ref, do_ref, don_ref, l_ref, ln_ref, d_ref, dn_ref,
             bc_ref, bp_ref, dq_ref, dk_ref, dv_ref):
        b0 = pair * pl.program_id(0)
        first = (b0 % nb) == 0
        nxt = jnp.logical_and(b0 + pair < nblk, ((b0 + pair) % nb) != 0)
        bc, bp = bc_ref[...], bp_ref[...]
        for j in range(pair):
            rows = slice(j * blk, (j + 1) * blk)
            qb, kc, vc = q_ref[:, rows, :], k_ref[:, rows, :], v_ref[:, rows, :]
            dob, l, d = _mx(do_ref[:, rows, :]), l_ref[:, :, rows], d_ref[:, :, rows]
            if j == 0:
                kp, vp = kp_ref[...], vp_ref[...]
            else:
                kp, vp = k_ref[:, (j - 1) * blk:j * blk, :], v_ref[:, (j - 1) * blk:j * blk, :]
            p_c = jnp.exp(_bdot(kc, qb, 2, 2) * DIL_SCALE + bc - l)
            p_p = jnp.exp(_bdot(kp, qb, 2, 2) * DIL_SCALE + bp - l)
            if j == 0:
                p_p = jnp.where(first, 0.0, p_p)
            ds_c = _mx(p_c * (_bdot(vc, dob, 2, 2) - d) * DIL_SCALE)
            ds_p = _mx(p_p * (_bdot(vp, dob, 2, 2) - d) * DIL_SCALE)
            dq_ref[:, rows, :] = _bdot(ds_c, kc, 1, 1) + _bdot(ds_p, kp, 1, 1)
            if j < pair - 1:
                nrows = slice((j + 1) * blk, (j + 2) * blk)
                qn, donb, ln, dn = q_ref[:, nrows, :], _mx(do_ref[:, nrows, :]), l_ref[:, :, nrows], d_ref[:, :, nrows]
            else:
                qn, donb, ln, dn = qn_ref[...], _mx(don_ref[...]), ln_ref[...], dn_ref[...]
            p_n = jnp.exp(_bdot(kc, qn, 2, 2) * DIL_SCALE + bp - ln)
            if j == pair - 1:
                p_n = jnp.where(nxt, p_n, 0.0)
            ds_n = _mx(p_n * (_bdot(vc, donb, 2, 2) - dn) * DIL_SCALE)
            dk_ref[:, rows, :] = _bdot(ds_c, qb, 2, 1) + _bdot(ds_n, qn, 2, 1)
            dv_ref[:, rows, :] = _bdot(_mx(p_c), dob, 2, 1) + _bdot(_mx(p_n), donb, 2, 1)

    cur = lambda w: pl.BlockSpec((hds, pair * blk, w), lambda b: (0, b, 0))
    prev = lambda w: pl.BlockSpec((hds, blk, w), lambda b: (0, jnp.maximum(pair * b - 1, 0), 0))
    nxt_ = lambda w: pl.BlockSpec((hds, blk, w), lambda b: (0, jnp.minimum(pair * (b + 1), nblk - 1), 0))
    rcur = pl.BlockSpec((hds, 1, pair * blk), lambda b: (0, 0, b))
    rnxt = pl.BlockSpec((hds, 1, blk), lambda b: (0, 0, jnp.minimum(pair * (b + 1), nblk - 1)))
    bias = pl.BlockSpec((hds, blk, blk), lambda b: (0, 0, 0))
    out = jax.ShapeDtypeStruct((hds, s, e), F32)
    return pl.pallas_call(
        body, name=name,
        out_shape=(out, out, out),
        grid=(nblk // pair,),
        in_specs=[cur(e), nxt_(e), cur(e), prev(e), cur(e), prev(e), cur(e), nxt_(e),
                  rcur, rnxt, rcur, rnxt, bias, bias],
        out_specs=(cur(e), cur(e), cur(e)),
        compiler_params=_params("parallel"),
    )(q, q, k, k, v, v, do, do, lj, lj, dd, dd, b_c, b_p)


def _ln_fwd(z, g, b):
    mu = jnp.mean(z, axis=-1, keepdims=True)
    zc = z - mu
    var = jnp.mean(zc * zc, axis=-1, keepdims=True)
    rstd = lax.rsqrt(var + LN_EPS)
    xhat = zc * rstd
    return xhat * g + b, xhat, rstd


def _ln_bwd(dy, xhat, rstd, g):
    dxh = dy * g
    return rstd * (dxh - jnp.mean(dxh, axis=-1, keepdims=True) - xhat * jnp.mean(dxh * xhat, axis=-1, keepdims=True))


def _out_ln1(a_mla, a_dil, w_o, x, g, b, *, tm):
    s = x.shape[0]
    half = HEADS * HEAD_DIM

    def body(am_ref, ad_ref, w_ref, x_ref, g_ref, b_ref, x1_ref, xh_ref, r_ref):
        mix = _dot(_mx(am_ref[...]), w_ref[0:half, :]) + _dot(_mx(ad_ref[...]), w_ref[half:2 * half, :])
        z = DN_ALPHA * x_ref[...] + mix
        y, xhat, rstd = _ln_fwd(z, g_ref[...], b_ref[...])
        x1_ref[...] = y
        xh_ref[...] = xhat
        r_ref[...] = rstd

    row = lambda w: pl.BlockSpec((tm, w), lambda i: (i, 0))
    full = lambda shp: pl.BlockSpec(shp, lambda i: (0,) * len(shp))
    act = jax.ShapeDtypeStruct((s, D_MODEL), F32)
    return pl.pallas_call(
        body, name="out_ln1",
        out_shape=(act, act, jax.ShapeDtypeStruct((s, 1), F32)),
        grid=(s // tm,),
        in_specs=[row(half), row(half), full((D_MODEL, D_MODEL)), row(D_MODEL), full((1, D_MODEL)), full((1, D_MODEL))],
        out_specs=(row(D_MODEL), row(D_MODEL), row(1)),
        compiler_params=_params("parallel"),
    )(a_mla, a_dil, w_o, x, g, b)


def _down_ln2_loss(act, w_down, x1, g, b, target, *, tm):
    s = x1.shape[0]

    def body(a_ref, w_ref, x1_ref, g_ref, b_ref, t_ref, dz_ref, loss_ref, dg_ref, db_ref):
        i = pl.program_id(0)

        @pl.when(i == 0)
        def _():
            loss_ref[...] = jnp.zeros_like(loss_ref)
            dg_ref[...] = jnp.zeros_like(dg_ref)
            db_ref[...] = jnp.zeros_like(db_ref)

        gam = g_ref[...]
        z = DN_ALPHA * x1_ref[...] + _dot(a_ref[...], w_ref[...])
        y, xhat, rstd = _ln_fwd(z, gam, b_ref[...])
        err = y - t_ref[...]
        loss_ref[...] += 0.5 * jnp.sum(jnp.mean(err * err, axis=-1, keepdims=True))
        dy = err * (1.0 / D_MODEL)
        dg_ref[...] += jnp.sum(dy * xhat, axis=0, keepdims=True)
        db_ref[...] += jnp.sum(dy, axis=0, keepdims=True)
        dz_ref[...] = _ln_bwd(dy, xhat, rstd, gam)

    row = lambda w: pl.BlockSpec((tm, w), lambda i: (i, 0))
    full = lambda shp: pl.BlockSpec(shp, lambda i: (0,) * len(shp))
    vec = jax.ShapeDtypeStruct((1, D_MODEL), F32)
    return pl.pallas_call(
        body, name="down_ln2_loss",
        out_shape=(jax.ShapeDtypeStruct((s, D_MODEL), F32), jax.ShapeDtypeStruct((1, LANES), F32), vec, vec),
        grid=(s // tm,),
        in_specs=[row(D_FF), full((D_FF, D_MODEL)), row(D_MODEL), full((1, D_MODEL)), full((1, D_MODEL)), row(D_MODEL)],
        out_specs=(row(D_MODEL), full((1, LANES)), full((1, D_MODEL)), full((1, D_MODEL))),
        compiler_params=_params("arbitrary"),
    )(act, w_down, x1, g, b, target)


def _up_bwd_ln1(du_a, du_g, w_up_t, dz2, xhat1, rstd1, g, *, tm):
    s = dz2.shape[0]

    def body(dua_ref, dug_ref, wa_ref, wg_ref, dz2_ref, xh_ref, r_ref, g_ref, dz1_ref, dg_ref, db_ref):
        i = pl.program_id(0)

        @pl.when(i == 0)
        def _():
            dg_ref[...] = jnp.zeros_like(dg_ref)
            db_ref[...] = jnp.zeros_like(db_ref)

        dx1 = DN_ALPHA * dz2_ref[...] + (_dot(dua_ref[...], wa_ref[...]) + _dot(dug_ref[...], wg_ref[...]))
        xhat = xh_ref[...]
        dg_ref[...] += jnp.sum(dx1 * xhat, axis=0, keepdims=True)
        db_ref[...] += jnp.sum(dx1, axis=0, keepdims=True)
        dz1_ref[...] = _ln_bwd(dx1, xhat, r_ref[...], g_ref[...])

    row = lambda w: pl.BlockSpec((tm, w), lambda i: (i, 0))
    full = lambda shp: pl.BlockSpec(shp, lambda i: (0,) * len(shp))
    vec = jax.ShapeDtypeStruct((1, D_MODEL), F32)
    return pl.pallas_call(
        body, name="up_bwd_ln1",
        out_shape=(jax.ShapeDtypeStruct((s, D_MODEL), F32), vec, vec),
        grid=(s // tm,),
        in_specs=[row(D_FF), row(D_FF),
                  pl.BlockSpec((D_FF, D_MODEL), lambda i: (0, 0)), pl.BlockSpec((D_FF, D_MODEL), lambda i: (1, 0)),
                  row(D_MODEL), row(D_MODEL), row(1), full((1, D_MODEL))],
        out_specs=(row(D_MODEL), full((1, D_MODEL)), full((1, D_MODEL))),
        compiler_params=_params("arbitrary"),
    )(du_a, du_g, w_up_t, w_up_t, dz2, xhat1, rstd1, g)


GELU_C = math.sqrt(2.0 / math.pi)


def _gelu(x):
    cdf = 0.5 * (1.0 + jnp.tanh(GELU_C * (x + 0.044715 * (x * x * x))))
    return x * cdf


def _gelu_grad(x):
    t = jnp.tanh(GELU_C * (x + 0.044715 * (x * x * x)))
    return 0.5 * (1.0 + t) + 0.5 * x * (1.0 - t * t) * (GELU_C * (1.0 + 3.0 * 0.044715 * (x * x)))


def _shift_down(u, halo):
    r1, r2 = pltpu.roll(u, 1, 0), pltpu.roll(u, 2, 0)
    row = lax.broadcasted_iota(jnp.int32, (SUBLANES, u.shape[1]), 0)
    h7, h6 = halo[7:8, :], halo[6:7, :]
    head1 = jnp.where(row == 0, h7, r1[:SUBLANES])
    head2 = jnp.where(row == 0, h6, jnp.where(row == 1, h7, r2[:SUBLANES]))
    return (jnp.concatenate([head1, r1[SUBLANES:]], axis=0), jnp.concatenate([head2, r2[SUBLANES:]], axis=0))


def _shift_up(d, nxt):
    t = d.shape[0]
    r1, r2 = pltpu.roll(d, t - 1, 0), pltpu.roll(d, t - 2, 0)
    row = lax.broadcasted_iota(jnp.int32, (SUBLANES, d.shape[1]), 0)
    n0, n1 = nxt[0:1, :], nxt[1:2, :]
    last = t - SUBLANES
    tail1 = jnp.where(row == SUBLANES - 1, n0, r1[last:])
    tail2 = jnp.where(row == SUBLANES - 1, n1, jnp.where(row == SUBLANES - 2, n0, r2[last:]))
    return (jnp.concatenate([r1[:last], tail1], axis=0), jnp.concatenate([r2[:last], tail2], axis=0))


def _conv(u, s1, s2, w, b):
    return ((b + w[0:1, :] * s2) + w[1:2, :] * s1) + w[2:3, :] * u


def _up_gate_fwd(x1, w_up, conv_w, conv_b, *, tm, tn):
    s = x1.shape[0]
    nj = D_FF // tn
    hb = tm // SUBLANES

    def body(x_ref, xh_ref, wua_ref, wug_ref, wa_ref, wg_ref, ba_ref, bg_ref,
             ua_ref, ug_ref, o_ref, a_ref, ge_ref, gd_ref):
        keep = pl.program_id(1) > 0
        xb, xh = _mx(x_ref[...]), _mx(xh_ref[...])
        wua, wug = wua_ref[...], wug_ref[...]
        ua, ug = _dot(xb, wua), _dot(xb, wug)
        ha = jnp.where(keep, _dot(xh, wua), 0.0)
        hg = jnp.where(keep, _dot(xh, wug), 0.0)
        ua_ref[...] = ua
        ug_ref[...] = ug
        a = _conv(ua, *_shift_down(ua, ha), wa_ref[...], ba_ref[...])
        g = _conv(ug, *_shift_down(ug, hg), wg_ref[...], bg_ref[...])
        ge = _gelu(g)
        o_ref[...] = (ge * a).astype(o_ref.dtype)
        a_ref[...] = a
        ge_ref[...] = ge
        gd_ref[...] = _gelu_grad(g)

    main = lambda off: pl.BlockSpec((tm, tn), lambda j, i: (i, j + off))
    wspec = lambda r, off: pl.BlockSpec((r, tn), lambda j, i: (0, j + off))
    if w_up.ndim == 3:
        wu = lambda off: pl.BlockSpec((None, D_MODEL, tn), lambda j, i: (j + off, 0, 0))
    else:
        wu = lambda off: pl.BlockSpec((D_MODEL, tn), lambda j, i: (0, j + off))
    keep_f32 = jax.ShapeDtypeStruct((s, D_FF), F32)
    return pl.pallas_call(
        body, name="up_gate_fwd",
        out_shape=(keep_f32, keep_f32, jax.ShapeDtypeStruct((s, D_FF), MXU_DTYPE), keep_f32, keep_f32, keep_f32),
        grid=(nj, s // tm),
        in_specs=[pl.BlockSpec((tm, D_MODEL), lambda j, i: (i, 0)),
                  pl.BlockSpec((SUBLANES, D_MODEL), lambda j, i: (jnp.maximum(i * hb - 1, 0), 0)),
                  wu(0), wu(nj), wspec(3, 0), wspec(3, nj), wspec(1, 0), wspec(1, nj)],
        out_specs=(main(0),) * 6,
        compiler_params=_params("parallel", "parallel"),
    )(x1, x1, w_up, w_up, conv_w, conv_w, conv_b, conv_b)


def _gate_bwd(u_a, u_g, dz2, w_down_t, a, ge, gd, conv_w, *, tm, tn):
    s = u_a.shape[0]
    nj = D_FF // tn
    ni = s // tm
    hb = tm // SUBLANES

    def body(ua_ref, ug_ref, ha_ref, hg_ref, dz_ref, dzn_ref, wd_ref, a_ref, an_ref, ge_ref, gen_ref, gd_ref, gdn_ref,
             wa_ref, wg_ref, dua_ref, dug_ref, dwa_ref, dwg_ref, dba_ref, dbg_ref):
        i = pl.program_id(1)

        @pl.when(i == 0)
        def _():
            for r in (dwa_ref, dwg_ref, dba_ref, dbg_ref):
                r[...] = jnp.zeros_like(r)

        wa, wg = wa_ref[...], wg_ref[...]
        ua, ug = ua_ref[...], ug_ref[...]
        ha = jnp.where(i > 0, ha_ref[...], 0.0)
        hg = jnp.where(i > 0, hg_ref[...], 0.0)
        sa1, sa2 = _shift_down(ua, ha)
        sg1, sg2 = _shift_down(ug, hg)
        wd = wd_ref[...]
        d = _dot(_mx(dz_ref[...]), wd)
        dya = d * ge_ref[...]
        dyg = d * a_ref[...] * gd_ref[...]
        dn = jnp.where(i < ni - 1, _dot(_mx(dzn_ref[...]), wd), 0.0)
        dya_n = dn * gen_ref[...]
        dyg_n = dn * an_ref[...] * gdn_ref[...]
        da1, da2 = _shift_up(dya, dya_n)
        dg1, dg2 = _shift_up(dyg, dyg_n)
        dua_ref[...] = (wa[2:3, :] * dya + wa[1:2, :] * da1 + wa[0:1, :] * da2).astype(dua_ref.dtype)
        dug_ref[...] = (wg[2:3, :] * dyg + wg[1:2, :] * dg1 + wg[0:1, :] * dg2).astype(dug_ref.dtype)
        ssum = lambda v: jnp.sum(v, axis=0, keepdims=True)
        dwa_ref[...] += jnp.concatenate([ssum(dya * sa2), ssum(dya * sa1), ssum(dya * ua)], axis=0)
        dwg_ref[...] += jnp.concatenate([ssum(dyg * sg2), ssum(dyg * sg1), ssum(dyg * ug)], axis=0)
        dba_ref[...] += ssum(dya)
        dbg_ref[...] += ssum(dyg)

    main = pl.BlockSpec((tm, tn), lambda j, i: (i, j))
    halo = pl.BlockSpec((SUBLANES, tn), lambda j, i: (jnp.maximum(i * hb - 1, 0), j))
    next_row = lambda j, i: jnp.minimum((i + 1) * hb, s // SUBLANES - 1)
    nxt = pl.BlockSpec((SUBLANES, tn), lambda j, i: (next_row(j, i), j))
    wspec = lambda r, off: pl.BlockSpec((r, tn), lambda j, i: (0, j + off))
    return pl.pallas_call(
        body, name="gate_bwd",
        out_shape=(jax.ShapeDtypeStruct((s, D_FF), MXU_DTYPE), jax.ShapeDtypeStruct((s, D_FF), MXU_DTYPE),
                   jax.ShapeDtypeStruct((3, D_FF), F32), jax.ShapeDtypeStruct((3, D_FF), F32),
                   jax.ShapeDtypeStruct((1, D_FF), F32), jax.ShapeDtypeStruct((1, D_FF), F32)),
        grid=(nj, ni),
        in_specs=[main, main, halo, halo,
                  pl.BlockSpec((tm, D_MODEL), lambda j, i: (i, 0)),
                  pl.BlockSpec((SUBLANES, D_MODEL), lambda j, i: (next_row(j, i), 0)),
                  pl.BlockSpec((D_MODEL, tn), lambda j, i: (0, j))]
        + [main, nxt] * 3 + [wspec(3, 0), wspec(3, nj)],
        out_specs=(main, main, wspec(3, 0), wspec(3, 0), wspec(1, 0), wspec(1, 0)),
        compiler_params=_params("parallel", "arbitrary"),
    )(u_a, u_g, u_a, u_g, dz2, dz2, w_down_t, a, a, ge, ge, gd, gd, conv_w, conv_w)


def _prep_weights(w_in, w_uq, w_uk, w_uv, w_o, w_up, w_down):
    return {**_prep_weights_first(w_in, w_uq, w_uk, w_uv), **_prep_weights_late(w_o, w_up, w_down)}


def _prep_weights_late(w_o, w_up, w_down):
    w_o, w_up, w_down = _mx(w_o), _mx(w_up), _mx(w_down)
    w_up_t = w_up.T if w_up.ndim == 2 else w_up.transpose(0, 2, 1).reshape(2 * D_FF, D_MODEL)
    return dict(w_o=w_o, w_o_t=w_o.T, w_up=w_up, w_up_t=w_up_t, w_down=w_down, w_down_t=w_down.T)


def _prep_weights_first(w_in, w_uq, w_uk, w_uv):
    c = lambda a: a.astype(MXU_DTYPE)
    w_in = c(w_in)
    z = lambda w: jnp.zeros((D_MODEL, w), MXU_DTYPE)
    r0 = Q_RANK + KV_RANK
    w_in_ext = jnp.concatenate([w_in[:, :r0], z(NOPE), w_in[:, r0:r0 + ROPE], z(32), w_in[:, r0 + ROPE:]], axis=1)
    wq = jnp.pad(c(w_uq).transpose(1, 0, 2), ((0, 0), (0, 0), (0, QK_PAD - NOPE - ROPE)))
    wk = jnp.pad(c(w_uk).transpose(1, 0, 2), ((0, 0), (0, 0), (0, QK_PAD - NOPE)))
    wv = c(w_uv).transpose(1, 0, 2)
    t3 = lambda a: a.transpose(0, 2, 1)
    return dict(w_in=w_in_ext, w_in_t=w_in_ext.T, wq=wq, wq_t=t3(wq), wk=wk, wk_t=t3(wk), wv=wv, wv_t=t3(wv))


def _local_step(x, target, w, g_cq, g_ckv, ln1_g, ln1_b, conv_w, conv_b, ln2_g, ln2_b, comm=None):
    s = x.shape[0]
    tabs = _rope_tables(s)
    r2 = lambda a: a.reshape(1, -1)
    cb = r2(conv_b)
    dils = [d for _, d in DIL_PAIRS]

    h, qp, kp, vp = _in_proj(x, w["w_in"], tm=512)
    q, k, v, v_t = _mla_prep_fwd(h, r2(g_cq), r2(g_ckv), w["wq"], w["wk"], w["wv"], w["wv_t"], tabs, tm=256)
    if comm is None:
        o_mla_t, lse_mla = _mla_attn_fwd(q, k, v_t, t=512, g=HEADS)
    else:
        o_mla_t, lse_mla, *gathered = _mla_attn_fwd(q, k, v_t, t=512, g=HEADS, late=comm["late"])
        w = {**w, **comm["finish"](gathered)}
    o_bs, lse_bs = [], []
    for i, d in enumerate(dils):
        o_b, l_b = _dil_fwd(qp[i], kp[i], vp[i], d, name=f"dil_fwd_{d}")
        o_bs.append(o_b)
        lse_bs.append(_unperm_row(l_b, d))
    o_dil, lj = _dil_combine(o_bs, lse_bs, ts=512)
    o_mla = o_mla_t.transpose(2, 0, 1).reshape(s, HEADS * HEAD_DIM)
    x1, xhat1, rstd1 = _out_ln1(o_mla, o_dil, w["w_o"], x, r2(ln1_g), r2(ln1_b), tm=512)
    u_a, u_g, act, conv_a, gelu_g, gelu_dg = _up_gate_fwd(x1, w["w_up"], conv_w, cb, tm=256, tn=1408)
    dz2, loss, dg2, db2 = _down_ln2_loss(act, w["w_down"], x1, r2(ln2_g), r2(ln2_b), target, tm=512)

    dw_down = _mm_tn(act, dz2, name="dw_down", tm=1408, tn=D_MODEL, ts=DW_TOKENS)
    du_a, du_g, dcw_a, dcw_g, dcb_a, dcb_g = _gate_bwd(u_a, u_g, dz2, w["w_down_t"], conv_a, gelu_g, gelu_dg, conv_w,
                                                       tm=256, tn=1408)
    dz1, dg1, db1 = _up_bwd_ln1(du_a, du_g, w["w_up_t"], dz2, xhat1, rstd1, r2(ln1_g), tm=256)
    dw_up = jnp.concatenate([_mm_tn(x1, du_a, name="dw_up_a", tm=D_MODEL, tn=1408, ts=DW_TOKENS),
                             _mm_tn(x1, du_g, name="dw_up_g", tm=D_MODEL, tn=1408, ts=DW_TOKENS)], axis=1)
    named_early = [("w_up", dw_up), ("w_down", dw_down)]
    swap = () if comm is None else comm["blocked"](named_early)
    do_mla, do_dil, dd_all, received = _attn_bwd_heads(dz1, w["w_o_t"], o_mla, o_dil, tm=512, swap=swap)
    dw_o = jnp.concatenate([_mm_tn(o_mla, dz1, name="dw_o_mla", tm=512, tn=D_MODEL, ts=DW_TOKENS),
                            _mm_tn(o_dil, dz1, name="dw_o_dil", tm=512, tn=D_MODEL, ts=DW_TOKENS)], axis=0)
    dd_all = dd_all.T
    dd_mla, dd_dil = dd_all[:HEADS].reshape(HEADS, 1, s), dd_all[HEADS:].reshape(HEADS, 1, s)
    early = () if comm is None else tuple(comm["add_halves"](named_early, swap, received))
    dq, dk, dv, *early_slots = _mla_attn_bwd(q, k, v, do_mla, lse_mla, dd_mla, t=512, g=4, early=early)
    parts = []
    for i, d in enumerate(dils):
        parts.append(_dil_bwd(qp[i], kp[i], vp[i], do_dil[i], _perm_row(lj, d), _perm_row(dd_dil, d), d, name=f"dil_bwd_{d}"))
    dh_dil = _dil_merge(parts, ts=512)
    dh_mla, dwq, dwk, dwv, dgq, dgkv = _mla_prep_bwd(h, dq, dk, dv, r2(g_cq), r2(g_ckv),
                                                     w["wq_t"], w["wk_t"], w["wv_t"], tabs, tm=256)
    mla_w = 4 * LANES
    w_in_t = w["w_in_t"]
    grad_x = _mm_nn(dh_mla, w_in_t[:mla_w], name="in_bwd_mla", tm=512, tn=D_MODEL, tk=mla_w, add=dz1, add_scale=DN_ALPHA)
    grad_x = _mm_nn(dh_dil, w_in_t[mla_w:], name="in_bwd_dil", tm=512, tn=D_MODEL, tk=3 * HEADS * HEAD_DIM, add=grad_x)
    dw_mla = _mm_tn(x, dh_mla, name="dw_in_mla", tm=D_MODEL, tn=mla_w, ts=DW_TOKENS)
    dw_dil = _mm_tn(x, dh_dil, name="dw_in_dil", tm=D_MODEL, tn=3 * HEADS * HEAD_DIM, ts=DW_TOKENS)
    r0 = Q_RANK + KV_RANK
    grads = dict(
        w_in=jnp.concatenate([dw_mla[:, :r0], dw_mla[:, r0 + NOPE:r0 + NOPE + ROPE], dw_dil], axis=1),
        g_cq=dgq[0], g_ckv=dgkv[0],
        w_uq=dwq[:, :, :NOPE + ROPE].transpose(1, 0, 2),
        w_uk=dwk[:, :, :NOPE].transpose(1, 0, 2),
        w_uv=dwv.transpose(1, 0, 2),
        w_o=dw_o, ln1_g=dg1[0], ln1_b=db1[0], w_up=dw_up,
        conv_w=jnp.concatenate([dcw_a, dcw_g], axis=1), conv_b=jnp.concatenate([dcb_a, dcb_g], axis=1)[0],
        w_down=dw_down, ln2_g=dg2[0], ln2_b=db2[0])
    if comm is not None:
        grads["early"] = (early, tuple(early_slots))
    return loss[0, 0], grad_x, grads


N_CHIPS = 4
SHARDED = ("w_in", "w_uq", "w_o", "w_up", "conv_w", "w_down")
COL_SHARDED = ("w_in", "w_up", "conv_w")
SHARD_SHAPE = dict(w_in=(D_MODEL, IN_WIDTH // 4), w_uq=(Q_RANK // 4, HEADS, NOPE + ROPE), w_o=(D_MODEL // 4, D_MODEL),
                   w_up=(D_MODEL, 2 * D_FF // 4), conv_w=(3, 2 * D_FF // 4), w_down=(D_FF // 4, D_MODEL))
SMALL = ("g_cq", "g_ckv", "w_uk", "w_uv", "ln1_g", "ln1_b", "conv_b", "ln2_g", "ln2_b")
SMALL_SHAPE = dict(g_cq=(Q_RANK,), g_ckv=(KV_RANK,), w_uk=(KV_RANK, HEADS, NOPE), w_uv=(KV_RANK, HEADS, HEAD_DIM),
                   ln1_g=(D_MODEL,), ln1_b=(D_MODEL,), conv_b=(2 * D_FF,), ln2_g=(D_MODEL,), ln2_b=(D_MODEL,))
BIG = ("w_in", "w_uq", "w_o", "w_up", "w_down")
BIG_2D = dict(w_in=(D_MODEL, IN_WIDTH // 4), w_uq=(Q_RANK // 4, HEADS * (NOPE + ROPE)), w_o=(D_MODEL // 4, D_MODEL),
              w_up=(D_MODEL, 2 * D_FF // 4), w_down=(D_FF // 4, D_MODEL))
SMALL_G = SMALL + ("conv_w",)
SMALL_WIDE = ("w_uk", "w_uv")
SMALL_G_SHAPE = {**SMALL_SHAPE, "conv_w": (3, 2 * D_FF)}
SMALL_U_SHAPE = {**SMALL_SHAPE, "conv_w": (3, 2 * D_FF // 4)}


def _size(shape):
    return math.prod(shape)


def _padded_rows(n_elems, mult):
    return -(-n_elems // (LANES * mult)) * mult


SHARD_ROWS = {n: _padded_rows(_size(SHARD_SHAPE[n]), SUBLANES) for n in SHARDED}
R_SMALL = -(-sum(_size(SMALL_G_SHAPE[n]) for n in SMALL_G) // (LANES * LANES)) * LANES
GATHER_FIRST = ("w_in", "w_uq")
GATHER_LATE = ("w_o", "w_up", "w_down")
REDUCED_EARLY = ("w_up", "w_down")
REDUCED_LAST = ("w_in", "w_uq", "w_o")


def _rows(a, rows=None):
    flat = a.reshape(-1)
    rows = -(-flat.shape[0] // LANES) if rows is None else rows
    return jnp.pad(flat, (0, rows * LANES - flat.shape[0])).reshape(rows, LANES)


def _blocked(name, g):
    r, c = BIG_2D[name]
    a = g.reshape(r, N_CHIPS, c).transpose(1, 0, 2) if name in COL_SHARDED else g.reshape(N_CHIPS, r, c)
    return a.reshape(N_CHIPS, 2, r // 2, c)


def _pack_flat(t, names, rows=None):
    flat = jnp.concatenate([t[n].astype(F32).reshape(-1) for n in names])
    return _rows(flat, R_SMALL if rows is None else rows)


def _unpack_flat(buf, names, shapes):
    flat, out, r = buf.reshape(-1), {}, 0
    for n in names:
        out[n] = flat[r:r + _size(shapes[n])].reshape(shapes[n])
        r += _size(shapes[n])
    return out


def _from_chip_blocks(name, blocks):
    shp = SHARD_SHAPE[name]
    a = blocks.reshape(N_CHIPS, -1)[:, :_size(shp)].reshape((N_CHIPS,) + shp)
    if name in COL_SHARDED:
        return a.transpose(1, 0, 2).reshape(shp[0], N_CHIPS * shp[1])
    return a.reshape((N_CHIPS * shp[0],) + shp[1:])


ANY = pl.BlockSpec(memory_space=pl.ANY)
COMM_PARAMS = pltpu.CompilerParams(has_side_effects=True)


def _coords():
    return lax.axis_index("x"), lax.axis_index("y"), lax.axis_index("c")


def _other_chips(x, y):
    return [(1 - x, y), (x, 1 - y), (1 - x, 1 - y)]


def _remote(src, dst, send_sems, recv_sems, k, to):
    return pltpu.make_async_remote_copy(src_ref=src, dst_ref=dst, send_sem=send_sems.at[k], recv_sem=recv_sems.at[k],
                                        device_id=to, device_id_type=MESH)


def _gather_in_steps(wp_refs, wout_refs, send_sems, recv_sems, *, first, mid, last):
    x, y, c = _coords()
    me = 2 * x + y
    sib = (x, y, 1 - c)
    chips = _other_chips(x, y)
    n = len(wp_refs)
    pairs = [(j, t, px, py) for j, (px, py) in enumerate(chips) for t in range(n)]
    ici = [_remote(wp_refs[t].at[c], wout_refs[t].at[me, c], send_sems, recv_sems, j * n + t, (px, py, c))
           for j, t, px, py in pairs]
    fwd = [_remote(wout_refs[t].at[2 * px + py, c], wout_refs[t].at[2 * px + py, c], send_sems, recv_sems, (3 + j) * n + t, sib)
           for j, t, px, py in pairs]

    @pl.when(first)
    def _():
        for cp in ici:
            cp.start()

    @pl.when(mid)
    def _():
        for i, (j, t, px, py) in enumerate(pairs):
            _remote(wp_refs[t].at[c], wout_refs[t].at[2 * px + py, c], send_sems, recv_sems, j * n + t, (px, py, c)).wait_recv()
            fwd[i].start()

    @pl.when(last)
    def _():
        for j, t, px, py in pairs:
            k = 2 * px + py
            _remote(wout_refs[t].at[k, 1 - c], wout_refs[t].at[k, 1 - c], send_sems, recv_sems, (3 + j) * n + t, sib).wait_recv()
        for cp in ici + fwd:
            cp.wait_send()


def _swap_halves_in_steps(gs_refs, os_refs, send_sems, recv_sems, *, first, last):
    x, y, c = _coords()
    sib = (x, y, 1 - c)
    cps = [_remote(gs_refs[t].at[k, 1 - c], os_refs[t].at[k], send_sems, recv_sems, t * N_CHIPS + k, sib)
           for t in range(len(gs_refs)) for k in range(N_CHIPS)]

    @pl.when(first)
    def _():
        for cp in cps:
            cp.start()

    @pl.when(last)
    def _():
        for cp in cps:
            cp.wait_recv()
        for cp in cps:
            cp.wait_send()


def _exchange_in_steps(ps_refs, ss_refs, send_sems, recv_sems, *, first, last):
    x, y, c = _coords()
    me = 2 * x + y
    chips = _other_chips(x, y)
    n = len(ps_refs)
    sends = [_remote(ps_refs[t].at[2 * px + py], ss_refs[t].at[me], send_sems, recv_sems, j * n + t, (px, py, c))
             for j, (px, py) in enumerate(chips) for t in range(n)]

    @pl.when(first)
    def _():
        for cp in sends:
            cp.start()

    @pl.when(last)
    def _():
        for j, (px, py) in enumerate(chips):
            for t in range(n):
                _remote(ps_refs[t].at[me], ss_refs[t].at[2 * px + py], send_sems, recv_sems, j * n + t, (px, py, c)).wait_recv()
        for cp in sends:
            cp.wait_send()


def _gather_weights(wp, cwp):
    def body(wp_ref, cw_ref, wout_ref, cwout_ref, send_sems, recv_sems):
        x, y, c = _coords()
        me = 2 * x + y
        sib = (x, y, 1 - c)
        chips = _other_chips(x, y)
        sends = [_remote(wp_ref.at[c], wout_ref.at[me, c], send_sems, recv_sems, j, (px, py, c))
                 for j, (px, py) in enumerate(chips)]
        sends += [_remote(cw_ref, cwout_ref.at[me], send_sems, recv_sems, 3 + j, (px, py, c))
                  for j, (px, py) in enumerate(chips)]
        for cp in sends:
            cp.start()
        for j, (px, py) in enumerate(chips):
            k = 2 * px + py
            _remote(wp_ref.at[c], wout_ref.at[k, c], send_sems, recv_sems, j, (px, py, c)).wait_recv()
            fwd = _remote(wout_ref.at[k, c], wout_ref.at[k, c], send_sems, recv_sems, 6 + j, sib)
            fwd.start()
            sends.append(fwd)
        for j, (px, py) in enumerate(chips):
            k = 2 * px + py
            _remote(cw_ref, cwout_ref.at[k], send_sems, recv_sems, 3 + j, (px, py, c)).wait_recv()
            _remote(wout_ref.at[k, 1 - c], wout_ref.at[k, 1 - c], send_sems, recv_sems, 6 + j, sib).wait_recv()
        for cp in sends:
            cp.wait_send()

    return pl.pallas_call(
        body, name="gather_weights",
        out_shape=(jax.ShapeDtypeStruct((N_CHIPS,) + wp.shape, wp.dtype), jax.ShapeDtypeStruct((N_CHIPS,) + cwp.shape, cwp.dtype)),
        in_specs=[ANY, ANY], out_specs=(ANY, ANY),
        scratch_shapes=[pltpu.SemaphoreType.DMA((9,)), pltpu.SemaphoreType.DMA((9,))],
        compiler_params=COMM_PARAMS,
    )(wp, cwp)


def _exchange_sibling_halves(gs, whole, *, name):
    n, nw = len(gs), len(whole)

    def body(*refs):
        gs_refs, wh_refs = refs[:n], refs[n:n + nw]
        os_refs, ow_refs = refs[n + nw:2 * n + nw], refs[2 * n + nw:2 * (n + nw)]
        send_sems, recv_sems = refs[2 * (n + nw):]
        x, y, c = _coords()
        sib = (x, y, 1 - c)
        cps = [_remote(gs_refs[t].at[k, 1 - c], os_refs[t].at[k], send_sems, recv_sems, t * N_CHIPS + k, sib)
               for t in range(n) for k in range(N_CHIPS)]
        cps += [_remote(wh_refs[t], ow_refs[t], send_sems, recv_sems, n * N_CHIPS + t, sib) for t in range(nw)]
        for cp in cps:
            cp.start()
        for cp in cps:
            cp.wait_recv()
        for cp in cps:
            cp.wait_send()

    n_sem = n * N_CHIPS + nw
    return pl.pallas_call(
        body, name=name,
        out_shape=tuple(jax.ShapeDtypeStruct((N_CHIPS,) + a.shape[2:], F32) for a in gs)
        + tuple(jax.ShapeDtypeStruct(a.shape, F32) for a in whole),
        in_specs=[ANY] * (n + nw), out_specs=(ANY,) * (n + nw),
        scratch_shapes=[pltpu.SemaphoreType.DMA((n_sem,)), pltpu.SemaphoreType.DMA((n_sem,))],
        compiler_params=COMM_PARAMS,
    )(*gs, *whole)


def _exchange_chips(ps, whole):
    n, nw = len(ps), len(whole)
    per_chip = n + nw

    def body(*refs):
        ps_refs, wh_refs = refs[:n], refs[n:per_chip]
        ss_refs, sw_refs = refs[per_chip:per_chip + n], refs[per_chip + n:2 * per_chip]
        send_sems, recv_sems = refs[2 * per_chip:]
        x, y, c = _coords()
        me = 2 * x + y
        chips = _other_chips(x, y)
        sends = []
        for j, (px, py) in enumerate(chips):
            to = (px, py, c)
            for t in range(n):
                sends.append(_remote(ps_refs[t].at[2 * px + py], ss_refs[t].at[me], send_sems, recv_sems, j * per_chip + t, to))
            for t in range(nw):
                sends.append(_remote(wh_refs[t], sw_refs[t].at[me], send_sems, recv_sems, j * per_chip + n + t, to))
        for cp in sends:
            cp.start()
        for j, (px, py) in enumerate(chips):
            k, to = 2 * px + py, (px, py, c)
            for t in range(n):
                _remote(ps_refs[t].at[me], ss_refs[t].at[k], send_sems, recv_sems, j * per_chip + t, to).wait_recv()
            for t in range(nw):
                _remote(wh_refs[t], sw_refs[t].at[k], send_sems, recv_sems, j * per_chip + n + t, to).wait_recv()
        for cp in sends:
            cp.wait_send()

    n_sem = 3 * per_chip
    return pl.pallas_call(
        body, name="exchange_chips",
        out_shape=tuple(jax.ShapeDtypeStruct(a.shape, a.dtype) for a in ps)
        + tuple(jax.ShapeDtypeStruct((N_CHIPS,) + a.shape, a.dtype) for a in whole),
        in_specs=[ANY] * per_chip, out_specs=(ANY,) * per_chip,
        scratch_shapes=[pltpu.SemaphoreType.DMA((n_sem,)), pltpu.SemaphoreType.DMA((n_sem,))],
        compiler_params=COMM_PARAMS,
    )(*ps, *whole)


def _exchange_sibling_result(gh):
    n = len(gh)

    def body(*refs):
        gh_refs, out_refs, (send_sems, recv_sems) = refs[:n], refs[n:2 * n], refs[2 * n:]
        x, y, c = _coords()
        cps = [_remote(gh_refs[t], out_refs[t], send_sems, recv_sems, t, (x, y, 1 - c)) for t in range(n)]
        for cp in cps:
            cp.start()
        for cp in cps:
            cp.wait_recv()
        for cp in cps:
            cp.wait_send()

    return pl.pallas_call(
        body, name="exchange_sibling_result",
        out_shape=tuple(jax.ShapeDtypeStruct(a.shape, F32) for a in gh),
        in_specs=[ANY] * n, out_specs=(ANY,) * n,
        scratch_shapes=[pltpu.SemaphoreType.DMA((n,)), pltpu.SemaphoreType.DMA((n,))],
        compiler_params=COMM_PARAMS,
    )(*gh)


def _add_own_half(gs, recv, c_arr, *, name):
    _, rows, cols = recv.shape

    def body(c_ref, a_ref, b_ref, o_ref):
        o_ref[0] = (a_ref[0, 0] + b_ref[0]).astype(o_ref.dtype)

    return pl.pallas_call(
        body, name=name,
        out_shape=jax.ShapeDtypeStruct(recv.shape, GRAD_WIRE_DTYPE),
        grid_spec=pltpu.PrefetchScalarGridSpec(
            num_scalar_prefetch=1, grid=(N_CHIPS,),
            in_specs=[pl.BlockSpec((1, 1, rows, cols), lambda k, c_ref: (k, c_ref[0], 0, 0)),
                      pl.BlockSpec((1, rows, cols), lambda k, c_ref: (k, 0, 0))],
            out_specs=pl.BlockSpec((1, rows, cols), lambda k, c_ref: (k, 0, 0))),
        compiler_params=_params("parallel"),
    )(c_arr, gs, recv)


def _add2(a, b, *, name, out_dtype=F32):
    def body(a_ref, b_ref, o_ref):
        o_ref[...] = (a_ref[...] + b_ref[...]).astype(o_ref.dtype)

    return pl.pallas_call(body, name=name, out_shape=jax.ShapeDtypeStruct(a.shape, out_dtype))(a, b)


def _sum_slots(slots, *, tr, name):
    _, r, c = slots.shape

    def body(s_ref, o_ref):
        f = lambda k: s_ref[k].astype(F32)
        o_ref[...] = ((f(0) + f(1)) + f(2)) + f(3)

    return pl.pallas_call(
        body, name=name,
        out_shape=jax.ShapeDtypeStruct((r, c), F32),
        grid=(r // tr,),
        in_specs=[pl.BlockSpec((N_CHIPS, tr, c), lambda i: (0, i, 0))],
        out_specs=pl.BlockSpec((tr, c), lambda i: (i, 0)),
        compiler_params=_params("parallel"),
    )(slots)


def _adamw(w, g, m, v, *, tr, name):
    r, cols = w.shape

    def body(w_ref, g_ref, m_ref, v_ref, d_ref, nm_ref, nv_ref):
        g_ = g_ref[...]
        m_ = ADAM_B1 * m_ref[...] + (1.0 - ADAM_B1) * g_
        v_ = ADAM_B2 * v_ref[...] + (1.0 - ADAM_B2) * (g_ * g_)
        m_hat = m_ / (1.0 - ADAM_B1 ** ADAM_STEP)
        v_hat = v_ / (1.0 - ADAM_B2 ** ADAM_STEP)
        d_ref[...] = -ADAM_LR * (m_hat / (jnp.sqrt(v_hat) + ADAM_EPS) + ADAM_WD * w_ref[...])
        nm_ref[...] = m_
        nv_ref[...] = v_

    spec = pl.BlockSpec((tr, cols), lambda i: (i, 0))
    out = jax.ShapeDtypeStruct((r, cols), F32)
    return pl.pallas_call(
        body, name=name, out_shape=(out, out, out), grid=(r // tr,),
        in_specs=[spec] * 4, out_specs=(spec,) * 3,
        compiler_params=_params("parallel"),
    )(w, g, m, v)


WEIGHTS = ("w_in", "g_cq", "g_ckv", "w_uq", "w_uk", "w_uv", "w_o", "ln1_g", "ln1_b", "w_up", "conv_w", "conv_b",
           "w_down", "ln2_g", "ln2_b")


def kernel(x, w_in, g_cq, g_ckv, w_uq, w_uk, w_uv, w_o, ln1_g, ln1_b, w_up, conv_w, conv_b, w_down, ln2_g, ln2_b, loss_target, m_w_in, m_g_cq, m_g_ckv, m_w_uq, m_w_uk, m_w_uv, m_w_o, m_ln1_g, m_ln1_b, m_w_up, m_conv_w, m_conv_b, m_w_down, m_ln2_g, m_ln2_b, v_w_in, v_g_cq, v_g_ckv, v_w_uq, v_w_uk, v_w_uv, v_w_o, v_ln1_g, v_ln1_b, v_w_up, v_conv_w, v_conv_b, v_w_down, v_ln2_g, v_ln2_b):
    wts = dict(zip(WEIGHTS, (w_in, g_cq, g_ckv, w_uq, w_uk, w_uv, w_o, ln1_g, ln1_b, w_up, conv_w, conv_b, w_down, ln2_g, ln2_b)))
    mom = dict(zip(WEIGHTS, (m_w_in, m_g_cq, m_g_ckv, m_w_uq, m_w_uk, m_w_uv, m_w_o, m_ln1_g, m_ln1_b, m_w_up, m_conv_w, m_conv_b, m_w_down, m_ln2_g, m_ln2_b)))
    var = dict(zip(WEIGHTS, (v_w_in, v_g_cq, v_g_ckv, v_w_uq, v_w_uk, v_w_uv, v_w_o, v_ln1_g, v_ln1_b, v_w_up, v_conv_w, v_conv_b, v_w_down, v_ln2_g, v_ln2_b)))

    me = 2 * lax.axis_index("x") + lax.axis_index("y")
    my_c = lax.axis_index("c")
    c_arr = my_c.astype(jnp.int32).reshape(1)
    own = lambda slots, mine: lax.dynamic_update_index_in_dim(slots, mine, me, 0)

    def pack(names):
        return jnp.concatenate([_rows(_mx(wts[n]), SHARD_ROWS[n]) for n in names], axis=0).reshape(2, -1, LANES)

    def unpack(names, gathered, mine):
        buf, full, r = own(gathered, mine).reshape(N_CHIPS, -1, LANES), {}, 0
        for n in names:
            full[n] = _from_chip_blocks(n, buf[:, r:r + SHARD_ROWS[n]])
            r += SHARD_ROWS[n]
        return full

    wp_first = pack(GATHER_FIRST)
    cwp = _rows(conv_w, SHARD_ROWS["conv_w"])
    gathered, cwfull = _gather_weights(wp_first, cwp)
    full = unpack(GATHER_FIRST, gathered, wp_first)
    conv_w_full = _from_chip_blocks("conv_w", own(cwfull, cwp))
    w = _prep_weights_first(full["w_in"], full["w_uq"], w_uk, w_uv)
    late_halves = [_mx(wts[n]).reshape(2, BIG_2D[n][0] // 2, BIG_2D[n][1]) for n in GATHER_LATE]

    def finish(gathered_late):
        w_o_b, w_up_b, w_down_b = (own(a, mine).reshape((N_CHIPS,) + BIG_2D[n])
                                   for a, mine, n in zip(gathered_late, late_halves, GATHER_LATE))
        return _prep_weights_late(w_o_b.reshape(D_MODEL, D_MODEL), w_up_b, w_down_b.reshape(D_FF, D_MODEL))

    def blocked(named):
        return [_blocked(n, a) for n, a in named]

    def add_halves(named, gb, recv):
        return [_add_own_half(gb[i], recv[i], c_arr, name=f"add_half_{n}") for i, (n, _) in enumerate(named)]

    def halve(named, whole, wire):
        gb = blocked(named)
        recv = _exchange_sibling_halves(gb, list(whole), name="exchange_sibling_halves")
        return add_halves(named, gb, recv) + [_add2(a, recv[len(gb) + i], name=f"add_whole_{i}", out_dtype=wire[i])
                                              for i, a in enumerate(whole)]

    comm = dict(late=late_halves, finish=finish, blocked=blocked, add_halves=add_halves)
    loss, grad_x, g = _local_step(x[0], loss_target[0], w, g_cq, g_ckv, ln1_g, ln1_b, conv_w_full, conv_b, ln2_g, ln2_b, comm=comm)

    ps_early, slots_early = g.pop("early")
    g["loss"] = loss.reshape(1)
    narrow = tuple(n for n in SMALL_G if n not in SMALL_WIDE) + ("loss",)
    narrow_shape = {**SMALL_G_SHAPE, "loss": (1,)}
    r_narrow = _padded_rows(sum(_size(narrow_shape[n]) for n in narrow), SUBLANES)
    r_wide = _padded_rows(sum(_size(SMALL_G_SHAPE[n]) for n in SMALL_WIDE), 2 * SUBLANES)
    *ps_rest, pr, pw = halve([(n, g[n]) for n in REDUCED_LAST],
                             whole=[_pack_flat(g, narrow, r_narrow), _pack_flat(g, SMALL_WIDE, r_wide)],
                             wire=[F32, GRAD_WIRE_DTYPE])
    *slots_rest, slots_r, slots_w = _exchange_chips(ps_rest, [pr, pw])
    ps = {**dict(zip(REDUCED_LAST, ps_rest)), **dict(zip(REDUCED_EARLY, ps_early))}
    slots = {**dict(zip(REDUCED_LAST, slots_rest)), **dict(zip(REDUCED_EARLY, slots_early))}
    slots = [own(slots[n], lax.dynamic_index_in_dim(ps[n], me, 0, keepdims=False)) for n in BIG]
    g_half = [_sum_slots(slots[i], tr=slots[i].shape[1] // 2, name=f"sum_chips_{n}") for i, n in enumerate(BIG)]
    g_small = {**_unpack_flat(_sum_slots(own(slots_r, pr), tr=r_narrow, name="sum_chips_narrow"), narrow, narrow_shape),
               **_unpack_flat(_sum_slots(own(slots_w, pw), tr=r_wide, name="sum_chips_wide"), SMALL_WIDE, SMALL_G_SHAPE)}
    loss = g_small.pop("loss")[0]
    g_other = _exchange_sibling_result(g_half)
    grads = {n: jnp.where(my_c == 0, jnp.concatenate([g_half[i], g_other[i]]), jnp.concatenate([g_other[i], g_half[i]]))
             for i, n in enumerate(BIG)}
    g_small["conv_w"] = lax.dynamic_slice_in_dim(g_small["conv_w"], me * SHARD_SHAPE["conv_w"][1], SHARD_SHAPE["conv_w"][1], 1)
    grads.update(g_small)

    res = {}
    for n in BIG:
        as2d = lambda a: a.reshape(BIG_2D[n])
        d, m, v = _adamw(as2d(wts[n]), grads[n], as2d(mom[n]), as2d(var[n]), tr=BIG_2D[n][0] // 4, name=f"adamw_{n}")
        res[n] = [a.reshape(SHARD_SHAPE[n]) for a in (grads[n], d, m, v)]
    flat = lambda t: _pack_flat(t, SMALL_G)
    dmv = _adamw(flat(wts), flat(g_small), flat(mom), flat(var), tr=R_SMALL, name="adamw_small")
    dmv = [_unpack_flat(a, SMALL_G, SMALL_U_SHAPE) for a in dmv]
    for n in SMALL_G:
        res[n] = [g_small[n]] + [t[n] for t in dmv]
    outs = [res[n][j] for j in range(4) for n in WEIGHTS]
    return (loss, grad_x[None], *outs)
```

```python
import math

import jax
import jax.numpy as jnp
from jax import lax
from jax.experimental import pallas as pl
from jax.experimental.pallas import tpu as pltpu

F32 = jnp.float32
MXU_DTYPE = jnp.bfloat16
GRAD_WIRE_DTYPE = jnp.bfloat16
NEG = -1e30

D_MODEL = 1024
HEADS = 8
HEAD_DIM = 64
Q_RANK = 256
KV_RANK = 128
NOPE = 64
ROPE = 32
QK_PAD = 128
IN_WIDTH = 1952
IN_EXT = 2048
D_FF = 2816
DIL_PAIRS = ((128, 1), (512, 4), (2048, 16))
DIL_BLOCK = 128
ROPE_THETA = 10000.0
DN_ALPHA = 2.0 ** 0.25
LN_EPS = 1e-5
RMS_EPS = 1e-6
MLA_SCALE = 1.0 / math.sqrt(NOPE + ROPE)
LOG2_E = math.log2(math.e)
DIL_SCALE = 1.0 / math.sqrt(HEAD_DIM)

ADAM_LR = 0.001
ADAM_B1 = 0.9
ADAM_B2 = 0.999
ADAM_EPS = 1e-08
ADAM_WD = 0.01
ADAM_STEP = 10

LANES = 128
SUBLANES = 8
VMEM_LIMIT_BYTES = 56 * 1024 * 1024
DW_TOKENS = 2048

MESH = pl.DeviceIdType.MESH


def _params(*sem):
    return pltpu.CompilerParams(dimension_semantics=sem, vmem_limit_bytes=VMEM_LIMIT_BYTES)


def _dot(a, b):
    return jnp.dot(a, b, preferred_element_type=F32)


def _dot_nt(a, b):
    return lax.dot_general(a, b, (((1,), (1,)), ((), ())), preferred_element_type=F32)


def _dot_tn(a, b):
    return lax.dot_general(a, b, (((0,), (0,)), ((), ())), preferred_element_type=F32)


def _mx(a):
    return a.astype(MXU_DTYPE)


def _mm_nn(a, b, *, name, tm, tn, tk, out_dtype=F32, add=None, add_scale=1.0):
    m, kdim = a.shape
    blocked = b.ndim == 3
    n = b.shape[0] * b.shape[2] if blocked else b.shape[1]
    nk = kdim // tk

    def body(*refs):
        if add is None:
            a_ref, b_ref, o_ref, acc = refs
        else:
            a_ref, b_ref, c_ref, o_ref, acc = refs
        k = pl.program_id(2)

        @pl.when(k == 0)
        def _():
            acc[...] = jnp.zeros_like(acc)

        acc[...] += _dot(_mx(a_ref[...]), _mx(b_ref[...]))

        @pl.when(k == nk - 1)
        def _():
            r = acc[...]
            if add is not None:
                r = r + add_scale * c_ref[...]
            o_ref[...] = r.astype(out_dtype)

    b_spec = (pl.BlockSpec((None, tk, tn), lambda i, j, k: (j, k, 0)) if blocked
              else pl.BlockSpec((tk, tn), lambda i, j, k: (k, j)))
    in_specs = [pl.BlockSpec((tm, tk), lambda i, j, k: (i, k)), b_spec]
    args = [a, b]
    if add is not None:
        in_specs.append(pl.BlockSpec((tm, tn), lambda i, j, k: (i, j)))
        args.append(add)
    return pl.pallas_call(
        body, name=name,
        out_shape=jax.ShapeDtypeStruct((m, n), out_dtype),
        grid=(m // tm, n // tn, nk),
        in_specs=in_specs,
        out_specs=pl.BlockSpec((tm, tn), lambda i, j, k: (i, j)),
        scratch_shapes=[pltpu.VMEM((tm, tn), F32)],
        compiler_params=_params("parallel", "parallel", "arbitrary"),
    )(*args)


def _mm_tn(a, b, *, name, tm, tn, ts, out_dtype=F32):
    s, m = a.shape
    n = b.shape[1]
    ns = s // ts

    def body(a_ref, b_ref, o_ref, acc):
        k = pl.program_id(2)

        @pl.when(k == 0)
        def _():
            acc[...] = jnp.zeros_like(acc)

        acc[...] += _dot_tn(_mx(a_ref[...]), _mx(b_ref[...]))

        @pl.when(k == ns - 1)
        def _():
            o_ref[...] = acc[...].astype(out_dtype)

    return pl.pallas_call(
        body, name=name,
        out_shape=jax.ShapeDtypeStruct((m, n), out_dtype),
        grid=(m // tm, n // tn, ns),
        in_specs=[pl.BlockSpec((ts, tm), lambda i, j, k: (k, i)),
                  pl.BlockSpec((ts, tn), lambda i, j, k: (k, j))],
        out_specs=pl.BlockSpec((tm, tn), lambda i, j, k: (i, j)),
        scratch_shapes=[pltpu.VMEM((tm, tn), F32)],
        compiler_params=_params("parallel", "parallel", "arbitrary"),
    )(a, b)


def _in_proj(x, w_in_ext, *, tm):
    s = x.shape[0]
    mla_w = 4 * LANES
    dil_w = HEADS * HEAD_DIM
    dils = [d for _, d in DIL_PAIRS]

    def body(x_ref, w_ref, h_ref, *rest):
        outs, sc = rest[:-1], rest[-1]
        xb = _mx(x_ref[...])
        h_ref[...] = _dot(xb, w_ref[:, 0:mla_w])
        for j in range(3):
            part = _dot(xb, w_ref[:, mla_w + j * dil_w:mla_w + (j + 1) * dil_w])
            for hd in range(HEADS):
                sc[hd] = part[:, hd * HEAD_DIM:(hd + 1) * HEAD_DIM]
            for b, d in enumerate(dils):
                _store_residue_major(outs[3 * j + b], sc, d, tm)

    shapes, specs = _residue_major_outs(s, tm, dils, MXU_DTYPE)
    res = pl.pallas_call(
        body, name="in_proj",
        out_shape=(jax.ShapeDtypeStruct((s, mla_w), F32),) + shapes * 3,
        grid=(s // tm,),
        in_specs=[pl.BlockSpec((tm, D_MODEL), lambda i: (i, 0)), pl.BlockSpec((D_MODEL, IN_EXT), lambda i: (0, 0))],
        out_specs=(pl.BlockSpec((tm, mla_w), lambda i: (i, 0)),) + specs * 3,
        scratch_shapes=[pltpu.VMEM((HEADS, tm, HEAD_DIM), F32)],
        compiler_params=_params("parallel"),
    )(x, w_in_ext)
    hm = lambda a: a.reshape(HEADS, s, HEAD_DIM)
    return res[0], [hm(a) for a in res[1:4]], [hm(a) for a in res[4:7]], [hm(a) for a in res[7:10]]


def _residue_major_outs(s, tm, dils, dtype):
    shapes, specs = [], []
    for d in dils:
        if d == 1:
            shapes.append(jax.ShapeDtypeStruct((HEADS, s, HEAD_DIM), dtype))
            specs.append(pl.BlockSpec((HEADS, tm, HEAD_DIM), lambda i: (0, i, 0)))
        else:
            shapes.append(jax.ShapeDtypeStruct((HEADS, d, s // d, HEAD_DIM), dtype))
            specs.append(pl.BlockSpec((HEADS, d, tm // d, HEAD_DIM), lambda i: (0, 0, i, 0)))
    return tuple(shapes), tuple(specs)


def _store_residue_major(o_ref, src_ref, d, tm):
    if d == 1:
        o_ref[...] = src_ref[...].astype(o_ref.dtype)
    else:
        for r in range(d):
            o_ref[:, r] = src_ref[:, pl.ds(r, tm // d, stride=d), :].astype(o_ref.dtype)


def _load_token_order(dst_ref, src_ref, d, tm, accumulate=False):
    if d == 1:
        dst_ref[...] = dst_ref[...] + src_ref[...] if accumulate else src_ref[...]
    else:
        for r in range(d):
            rows = pl.ds(r, tm // d, stride=d)
            dst_ref[:, rows, :] = dst_ref[:, rows, :] + src_ref[:, r] if accumulate else src_ref[:, r]


def _attn_bwd_heads(dz1, w_o_t, a_mla, a_dil, *, tm, swap=()):
    s = dz1.shape[0]
    half = HEADS * HEAD_DIM
    dils = [d for _, d in DIL_PAIRS]
    nsw = len(swap)
    n_steps = s // tm

    def body(*refs):
        dz_ref, w_ref, am_ref, ad_ref = refs[:4]
        gs_refs = refs[4:4 + nsw]
        dom_ref, dd_ref = refs[4 + nsw:6 + nsw]
        dod_refs = refs[6 + nsw:6 + nsw + len(dils)]
        os_refs = refs[6 + nsw + len(dils):6 + 2 * nsw + len(dils)]
        if nsw:
            send_sems, recv_sems = refs[6 + 2 * nsw + len(dils):]
            i = pl.program_id(0)
            _swap_halves_in_steps(gs_refs, os_refs, send_sems, recv_sems, first=i == 0, last=i == n_steps - 1)
        dzb = _mx(dz_ref[...])
        for j, (a_ref, o_ref) in enumerate(((am_ref, dom_ref), (ad_ref, dod_refs[0]))):
            da = _dot(dzb, w_ref[:, j * half:(j + 1) * half])
            prod = da * a_ref[...]
            for hd in range(HEADS):
                sl = slice(hd * HEAD_DIM, (hd + 1) * HEAD_DIM)
                o_ref[hd] = da[:, sl].astype(o_ref.dtype)
                dd_ref[:, j * HEADS + hd:j * HEADS + hd + 1] = jnp.sum(prod[:, sl], axis=-1, keepdims=True)
        for b, d in enumerate(dils[1:]):
            _store_residue_major(dod_refs[1 + b], dod_refs[0], d, tm)

    hspec = pl.BlockSpec((HEADS, tm, HEAD_DIM), lambda i: (0, i, 0))
    row = lambda w: pl.BlockSpec((tm, w), lambda i: (i, 0))
    shapes, specs = _residue_major_outs(s, tm, dils, F32)
    n_sem = nsw * N_CHIPS
    do_mla, dd, *rest = pl.pallas_call(
        body, name="attn_bwd_heads",
        out_shape=(jax.ShapeDtypeStruct((HEADS, s, HEAD_DIM), MXU_DTYPE), jax.ShapeDtypeStruct((s, 2 * HEADS), F32)) + shapes
        + tuple(jax.ShapeDtypeStruct((N_CHIPS,) + a.shape[2:], F32) for a in swap),
        grid=(n_steps,),
        in_specs=[row(D_MODEL), pl.BlockSpec((D_MODEL, D_MODEL), lambda i: (0, 0)), row(half), row(half)] + [ANY] * nsw,
        out_specs=(hspec, row(2 * HEADS)) + specs + (ANY,) * nsw,
        scratch_shapes=[pltpu.SemaphoreType.DMA((n_sem,)), pltpu.SemaphoreType.DMA((n_sem,))] if nsw else [],
        compiler_params=pltpu.CompilerParams(dimension_semantics=("arbitrary",), vmem_limit_bytes=VMEM_LIMIT_BYTES,
                                             has_side_effects=nsw > 0),
    )(dz1, w_o_t, a_mla, a_dil, *swap)
    do_dil, received = rest[:len(dils)], rest[len(dils):]
    return do_mla, [a.reshape(HEADS, s, HEAD_DIM) for a in do_dil], dd, received


def _dil_merge(parts, *, ts):
    hds, s, e = parts[0][0].shape
    dils = [d for _, d in DIL_PAIRS]

    def body(*refs):
        o_ref, sc = refs[9], refs[10]
        for j in range(3):
            for b, d in enumerate(dils):
                _load_token_order(sc, refs[3 * b + j], d, ts, accumulate=b > 0)
            tot = sc[...]
            for hd in range(hds):
                col = j * hds * e + hd * e
                o_ref[:, col:col + e] = tot[hd].astype(o_ref.dtype)

    _, specs = _residue_major_outs(s, ts, dils, F32)
    view = lambda a, d: a if d == 1 else a.reshape(hds, d, s // d, e)
    return pl.pallas_call(
        body, name="dil_merge",
        out_shape=jax.ShapeDtypeStruct((s, 3 * hds * e), MXU_DTYPE),
        grid=(s // ts,),
        in_specs=[specs[b] for b in range(3) for _ in range(3)],
        out_specs=pl.BlockSpec((ts, 3 * hds * e), lambda i: (i, 0)),
        scratch_shapes=[pltpu.VMEM((hds, ts, e), F32)],
        compiler_params=_params("parallel"),
    )(*[view(parts[b][j], dils[b]) for b in range(3) for j in range(3)])


def _rope_tables(s):
    half = ROPE // 2
    freqs = ROPE_THETA ** (-jnp.arange(half, dtype=F32) / half)
    ang = jnp.arange(s).astype(F32)[:, None] * freqs[None, :]
    cos, sin = jnp.cos(ang), jnp.sin(ang)
    z = lambda w: jnp.zeros((s, w), F32)
    c = jnp.concatenate([jnp.ones((s, NOPE), F32), cos, cos, z(32)], axis=1)
    s1 = jnp.concatenate([z(NOPE + half), sin, z(32)], axis=1)
    s2 = jnp.concatenate([z(NOPE), -sin, z(half + 32)], axis=1)
    mask = jnp.concatenate([z(NOPE), jnp.ones((s, ROPE), F32), z(32)], axis=1)
    return c, s1, s2, mask


def _rope(x, c, s1, s2):
    return x * c + pltpu.roll(x, 16, 1) * s1 + pltpu.roll(x, LANES - 16, 1) * s2


def _unrope(dy, c, s1, s2):
    return dy * c + pltpu.roll(dy * s1, LANES - 16, 1) + pltpu.roll(dy * s2, 16, 1)


def _rms(x):
    r = lax.rsqrt(jnp.mean(x * x, axis=-1, keepdims=True) + RMS_EPS)
    return x * r, r


def _mla_prep_fwd(h, g_cq, g_ckv, wq, wk, wv, wv_t, tabs, *, tm):
    s = h.shape[0]
    c_t, s1_t, s2_t, _ = tabs

    def body(h_ref, gq_ref, gkv_ref, wq_ref, wk_ref, wv_ref, wvt_ref, c_ref, s1_ref, s2_ref,
             q_ref, k_ref, v_ref, vt_ref):
        cq = h_ref[:, 0:Q_RANK]
        ckv = h_ref[:, Q_RANK:Q_RANK + KV_RANK]
        kr = h_ref[:, Q_RANK + KV_RANK:Q_RANK + KV_RANK + QK_PAD]
        c, s1, s2 = c_ref[...], s1_ref[...], s2_ref[...]
        cqn = _mx(_rms(cq)[0] * gq_ref[...])
        ckvn = _mx(_rms(ckv)[0] * gkv_ref[...])
        kr_rot = _rope(kr, c, s1, s2)
        for hd in range(HEADS):
            q_ref[hd] = _rope(_dot(cqn, wq_ref[hd]), c, s1, s2).astype(q_ref.dtype)
            k_ref[hd] = (_dot(ckvn, wk_ref[hd]) + kr_rot).astype(k_ref.dtype)
            v_ref[hd] = _dot(ckvn, wv_ref[hd]).astype(v_ref.dtype)
            vt_ref[hd] = _dot_nt(wvt_ref[hd], ckvn).astype(vt_ref.dtype)

    full = lambda shp: pl.BlockSpec(shp, lambda i: (0,) * len(shp))
    row = lambda w: pl.BlockSpec((tm, w), lambda i: (i, 0))
    return pl.pallas_call(
        body, name="mla_prep_fwd",
        out_shape=(jax.ShapeDtypeStruct((HEADS, s, QK_PAD), MXU_DTYPE),
                   jax.ShapeDtypeStruct((HEADS, s, QK_PAD), MXU_DTYPE),
                   jax.ShapeDtypeStruct((HEADS, s, HEAD_DIM), MXU_DTYPE),
                   jax.ShapeDtypeStruct((HEADS, HEAD_DIM, s), MXU_DTYPE)),
        grid=(s // tm,),
        in_specs=[row(4 * LANES), full((1, Q_RANK)), full((1, KV_RANK)),
                  full((HEADS, Q_RANK, QK_PAD)), full((HEADS, KV_RANK, QK_PAD)), full((HEADS, KV_RANK, HEAD_DIM)),
                  full((HEADS, HEAD_DIM, KV_RANK)), row(LANES), row(LANES), row(LANES)],
        out_specs=(pl.BlockSpec((HEADS, tm, QK_PAD), lambda i: (0, i, 0)),
                   pl.BlockSpec((HEADS, tm, QK_PAD), lambda i: (0, i, 0)),
                   pl.BlockSpec((HEADS, tm, HEAD_DIM), lambda i: (0, i, 0)),
                   pl.BlockSpec((HEADS, HEAD_DIM, tm), lambda i: (0, 0, i))),
        compiler_params=_params("parallel"),
    )(h, g_cq, g_ckv, wq, wk, wv, wv_t, c_t, s1_t, s2_t)


def _mla_prep_bwd(h, dq, dk, dv, g_cq, g_ckv, wq_t, wk_t, wv_t, tabs, *, tm):
    s = h.shape[0]
    c_t, s1_t, s2_t, mask_t = tabs

    def body(h_ref, dq_ref, dk_ref, dv_ref, gq_ref, gkv_ref, wqt_ref, wkt_ref, wvt_ref,
             c_ref, s1_ref, s2_ref, mask_ref, dh_ref, dwq_ref, dwk_ref, dwv_ref, dgq_ref, dgkv_ref):
        i = pl.program_id(0)

        @pl.when(i == 0)
        def _():
            dwq_ref[...] = jnp.zeros_like(dwq_ref)
            dwk_ref[...] = jnp.zeros_like(dwk_ref)
            dwv_ref[...] = jnp.zeros_like(dwv_ref)
            dgq_ref[...] = jnp.zeros_like(dgq_ref)
            dgkv_ref[...] = jnp.zeros_like(dgkv_ref)

        cq = h_ref[:, 0:Q_RANK]
        ckv = h_ref[:, Q_RANK:Q_RANK + KV_RANK]
        c, s1, s2 = c_ref[...], s1_ref[...], s2_ref[...]
        cqh, rq = _rms(cq)
        ckvh, rkv = _rms(ckv)
        gq, gkv = gq_ref[...], gkv_ref[...]
        cqn = _mx(cqh * gq)
        ckvn = _mx(ckvh * gkv)
        dcqn = jnp.zeros((tm, Q_RANK), F32)
        dckvn = jnp.zeros((tm, KV_RANK), F32)
        dkr = jnp.zeros((tm, QK_PAD), F32)
        for hd in range(HEADS):
            dqh = _mx(_unrope(dq_ref[hd], c, s1, s2))
            dcqn = dcqn + _dot(dqh, wqt_ref[hd])
            dwq_ref[hd] += _dot_tn(cqn, dqh)
            dkh = dk_ref[hd]
            dkr = dkr + dkh
            dkh = _mx(dkh)
            dckvn = dckvn + _dot(dkh, wkt_ref[hd])
            dwk_ref[hd] += _dot_tn(ckvn, dkh)
            dvh = _mx(dv_ref[hd])
            dckvn = dckvn + _dot(dvh, wvt_ref[hd])
            dwv_ref[hd] += _dot_tn(ckvn, dvh)
        dgq_ref[...] += jnp.sum(dcqn * cqh, axis=0, keepdims=True)
        dgkv_ref[...] += jnp.sum(dckvn * ckvh, axis=0, keepdims=True)
        gd = dcqn * gq
        dh_ref[:, 0:Q_RANK] = rq * (gd - cqh * jnp.mean(gd * cqh, axis=-1, keepdims=True))
        gd = dckvn * gkv
        dh_ref[:, Q_RANK:Q_RANK + KV_RANK] = rkv * (gd - ckvh * jnp.mean(gd * ckvh, axis=-1, keepdims=True))
        dh_ref[:, Q_RANK + KV_RANK:Q_RANK + KV_RANK + QK_PAD] = _unrope(dkr, c, s1, s2) * mask_ref[...]

    full = lambda shp: pl.BlockSpec(shp, lambda i: (0,) * len(shp))
    row = lambda w: pl.BlockSpec((tm, w), lambda i: (i, 0))
    hrow = lambda w: pl.BlockSpec((HEADS, tm, w), lambda i: (0, i, 0))
    return pl.pallas_call(
        body, name="mla_prep_bwd",
        out_shape=(jax.ShapeDtypeStruct((s, 4 * LANES), F32),
                   jax.ShapeDtypeStruct((HEADS, Q_RANK, QK_PAD), F32),
                   jax.ShapeDtypeStruct((HEADS, KV_RANK, QK_PAD), F32),
                   jax.ShapeDtypeStruct((HEADS, KV_RANK, HEAD_DIM), F32),
                   jax.ShapeDtypeStruct((1, Q_RANK), F32),
                   jax.ShapeDtypeStruct((1, KV_RANK), F32)),
        grid=(s // tm,),
        in_specs=[row(4 * LANES), hrow(QK_PAD), hrow(QK_PAD), hrow(HEAD_DIM),
                  full((1, Q_RANK)), full((1, KV_RANK)),
                  full((HEADS, QK_PAD, Q_RANK)), full((HEADS, QK_PAD, KV_RANK)), full((HEADS, HEAD_DIM, KV_RANK)),
                  row(LANES), row(LANES), row(LANES), row(LANES)],
        out_specs=(row(4 * LANES), full((HEADS, Q_RANK, QK_PAD)), full((HEADS, KV_RANK, QK_PAD)),
                   full((HEADS, KV_RANK, HEAD_DIM)), full((1, Q_RANK)), full((1, KV_RANK))),
        compiler_params=_params("arbitrary"),
    )(h, dq, dk, dv, g_cq, g_ckv, wq_t, wk_t, wv_t, c_t, s1_t, s2_t, mask_t)


def _bdot(a, b, ca, cb):
    return lax.dot_general(a, b, (((ca,), (cb,)), ((0,), (0,))), preferred_element_type=F32)


def _causal_mask_t(t):
    kk = lax.broadcasted_iota(jnp.int32, (t, t), 0)
    qq = lax.broadcasted_iota(jnp.int32, (t, t), 1)
    return (qq >= kk)[None]


def _mla_attn_fwd(q, k, v_t, *, t, g, late=None):
    hds, s, _ = q.shape
    n = s // t
    n_groups = hds // g

    nl = 0 if late is None else len(late)

    def body(*refs):
        q_ref, k_ref, vt_ref = refs[:3]
        wp_refs = refs[3:3 + nl]
        o_ref, lse_ref = refs[3 + nl:5 + nl]
        wout_refs = refs[5 + nl:5 + 2 * nl]
        m_sc, l_sc, acc_sc = refs[5 + 2 * nl:8 + 2 * nl]
        hg, qi, ki = pl.program_id(0), pl.program_id(1), pl.program_id(2)
        if nl:
            send_sems, recv_sems = refs[8 + 2 * nl:]
            tail = jnp.logical_and(hg == n_groups - 1, qi == n - 1)
            _gather_in_steps(wp_refs, wout_refs, send_sems, recv_sems,
                             first=jnp.logical_and(hg == 0, jnp.logical_and(qi == 0, ki == 0)),
                             mid=jnp.logical_and(tail, ki == 0), last=jnp.logical_and(tail, ki == n - 1))

        @pl.when(ki == 0)
        def _():
            m_sc[...] = jnp.full_like(m_sc, NEG)
            l_sc[...] = jnp.zeros_like(l_sc)
            acc_sc[...] = jnp.zeros_like(acc_sc)

        def step(masked):
            sc = _bdot(k_ref[...], q_ref[...], 2, 2)
            if masked:
                sc = jnp.where(_causal_mask_t(t), sc, NEG)
            m_prev = m_sc[...]
            m_new = jnp.maximum(m_prev, jnp.max(sc, axis=1, keepdims=True))
            p = jnp.exp2((sc - m_new) * (MLA_SCALE * LOG2_E))
            a = jnp.exp2((m_prev - m_new) * (MLA_SCALE * LOG2_E))
            l_sc[...] = a * l_sc[...] + jnp.sum(p, axis=1, keepdims=True)
            acc_sc[...] = a * acc_sc[...] + _bdot(vt_ref[...], _mx(p), 2, 1)
            m_sc[...] = m_new

        @pl.when(ki < qi)
        def _():
            step(False)

        @pl.when(ki == qi)
        def _():
            step(True)
            o_ref[...] = acc_sc[...] / l_sc[...]
            lse_ref[...] = m_sc[...] * MLA_SCALE + jnp.log(l_sc[...])

    qspec = pl.BlockSpec((g, t, QK_PAD), lambda h, i, j: (h, i, 0))
    kspec = pl.BlockSpec((g, t, QK_PAD), lambda h, i, j: (h, jnp.minimum(i, j), 0))
    vspec = pl.BlockSpec((g, HEAD_DIM, t), lambda h, i, j: (h, 0, jnp.minimum(i, j)))
    out_shape = [jax.ShapeDtypeStruct((hds, HEAD_DIM, s), F32), jax.ShapeDtypeStruct((hds, 1, s), F32)]
    in_specs = [qspec, kspec, vspec]
    out_specs = [pl.BlockSpec((g, HEAD_DIM, t), lambda h, i, j: (h, 0, i)), pl.BlockSpec((g, 1, t), lambda h, i, j: (h, 0, i))]
    scratch = [pltpu.VMEM((g, 1, t), F32), pltpu.VMEM((g, 1, t), F32), pltpu.VMEM((g, HEAD_DIM, t), F32)]
    args = [q, k, v_t]
    if nl:
        out_shape += [jax.ShapeDtypeStruct((N_CHIPS,) + a.shape, a.dtype) for a in late]
        in_specs += [ANY] * nl
        out_specs += [ANY] * nl
        scratch += [pltpu.SemaphoreType.DMA((6 * nl,)), pltpu.SemaphoreType.DMA((6 * nl,))]
        args += list(late)
    return pl.pallas_call(
        body, name="mla_attn_fwd",
        out_shape=tuple(out_shape), grid=(n_groups, n, n),
        in_specs=in_specs, out_specs=tuple(out_specs), scratch_shapes=scratch,
        compiler_params=pltpu.CompilerParams(dimension_semantics=("arbitrary",) * 3, vmem_limit_bytes=VMEM_LIMIT_BYTES,
                                             has_side_effects=nl > 0),
    )(*args)


def _mla_attn_bwd(q, k, v, do, lse, dd, *, t, g, early=()):
    hds, s, _ = q.shape
    n = s // t
    n_groups = hds // g
    ne = len(early)

    def body(*refs):
        q_ref, k_ref, v_ref, do_ref, lse_ref, dd_ref = refs[:6]
        ps_refs = refs[6:6 + ne]
        dq_ref, dk_ref, dv_ref = refs[6 + ne:9 + ne]
        ss_refs = refs[9 + ne:9 + 2 * ne]
        dq_sc, dk_sc, dv_sc = refs[9 + 2 * ne:12 + 2 * ne]
        hg, ki, qi = pl.program_id(0), pl.program_id(1), pl.program_id(2)
        if ne:
            send_sems, recv_sems = refs[12 + 2 * ne:]
            _exchange_in_steps(ps_refs, ss_refs, send_sems, recv_sems,
                               first=jnp.logical_and(hg == 0, jnp.logical_and(ki == 0, qi == 0)),
                               last=jnp.logical_and(hg == n_groups - 1, jnp.logical_and(ki == n - 1, qi == n - 1)))

        @pl.when(jnp.logical_and(ki == 0, qi == 0))
        def _():
            dq_sc[...] = jnp.zeros_like(dq_sc)

        @pl.when(qi == 0)
        def _():
            dk_sc[...] = jnp.zeros_like(dk_sc)
            dv_sc[...] = jnp.zeros_like(dv_sc)

        def step(masked):
            qb, kb, dob = q_ref[...], k_ref[...], do_ref[...]
            sc = _bdot(kb, qb, 2, 2) * MLA_SCALE
            if masked:
                sc = jnp.where(_causal_mask_t(t), sc, NEG)
            p = jnp.exp(sc - lse_ref[...])
            dv_sc[...] += _bdot(_mx(p), dob, 2, 1)
            dp = _bdot(v_ref[...], dob, 2, 2)
            ds = _mx(p * (dp - dd_ref[...]) * MLA_SCALE)
            dk_sc[...] += _bdot(ds, qb, 2, 1)
            dq_sc[qi] += _bdot(ds, kb, 1, 1)

        @pl.when(qi == ki)
        def _():
            step(True)

        @pl.when(qi > ki)
        def _():
            step(False)

        @pl.when(qi == n - 1)
        def _():
            dk_ref[...] = dk_sc[...]
            dv_ref[...] = dv_sc[...]

        @pl.when(jnp.logical_and(ki == n - 1, qi == n - 1))
        def _():
            for j in range(n):
                dq_ref[:, j * t:(j + 1) * t, :] = dq_sc[j]

    qs = lambda w: pl.BlockSpec((g, t, w), lambda h, j, i: (h, jnp.maximum(i, j), 0))
    ks = lambda w: pl.BlockSpec((g, t, w), lambda h, j, i: (h, j, 0))
    rowq = pl.BlockSpec((g, 1, t), lambda h, j, i: (h, 0, jnp.maximum(i, j)))
    scratch = [pltpu.VMEM((n, g, t, QK_PAD), F32), pltpu.VMEM((g, t, QK_PAD), F32), pltpu.VMEM((g, t, HEAD_DIM), F32)]
    if ne:
        scratch += [pltpu.SemaphoreType.DMA((3 * ne,)), pltpu.SemaphoreType.DMA((3 * ne,))]
    return pl.pallas_call(
        body, name="mla_attn_bwd",
        out_shape=(jax.ShapeDtypeStruct((hds, s, QK_PAD), F32), jax.ShapeDtypeStruct((hds, s, QK_PAD), F32),
                   jax.ShapeDtypeStruct((hds, s, HEAD_DIM), F32)) + tuple(jax.ShapeDtypeStruct(a.shape, a.dtype) for a in early),
        grid=(n_groups, n, n),
        in_specs=[qs(QK_PAD), ks(QK_PAD), ks(HEAD_DIM), qs(HEAD_DIM), rowq, rowq] + [ANY] * ne,
        out_specs=(pl.BlockSpec((g, s, QK_PAD), lambda h, j, i: (h, 0, 0)), ks(QK_PAD), ks(HEAD_DIM)) + (ANY,) * ne,
        scratch_shapes=scratch,
        compiler_params=pltpu.CompilerParams(dimension_semantics=("arbitrary",) * 3, vmem_limit_bytes=VMEM_LIMIT_BYTES,
                                             has_side_effects=ne > 0),
    )(q, k, v, do, lse, dd, *early)


def _perm_row(a, dil):
    if dil == 1:
        return a
    hds, _, s = a.shape
    return a.reshape(hds, s // dil, dil).transpose(0, 2, 1).reshape(hds, 1, s)


def _unperm_row(a, dil):
    if dil == 1:
        return a
    hds, _, s = a.shape
    return a.reshape(hds, dil, s // dil).transpose(0, 2, 1).reshape(hds, 1, s)


def _dil_bias(dil):
    slopes = 2.0 ** (-8.0 * jnp.arange(1, HEADS + 1, dtype=F32) / HEADS)
    ik = jnp.arange(DIL_BLOCK)[:, None]
    iq = jnp.arange(DIL_BLOCK)[None, :]
    off_c = iq - ik
    off_p = iq - ik + DIL_BLOCK
    b_c = -slopes[:, None, None] * (off_c * dil).astype(F32)[None]
    b_p = -slopes[:, None, None] * (off_p * dil).astype(F32)[None]
    b_c = jnp.where((off_c >= 0)[None], b_c, NEG)
    b_p = jnp.where((off_p <= DIL_BLOCK)[None], b_p, NEG)
    return b_c, b_p


def _dil_fwd(q, k, v, dil, *, name):
    hds, s, e = q.shape
    blk = DIL_BLOCK
    nblk = s // blk
    nb = nblk // dil
    pair = next(p for p in (4, 2, 1) if nb % p == 0)
    b_c, b_p = _dil_bias(dil)

    def body(q_ref, k_ref, kp_ref, v_ref, vp_ref, bc_ref, bp_ref, o_ref, lse_ref):
        first = ((pair * pl.program_id(0)) % nb) == 0
        bc, bp = bc_ref[...], bp_ref[...]
        for j in range(pair):
            rows = slice(j * blk, (j + 1) * blk)
            qb = q_ref[:, rows, :]
            if j == 0:
                kp, vp = kp_ref[...], vp_ref[...]
            else:
                kp, vp = k_ref[:, (j - 1) * blk:j * blk, :], v_ref[:, (j - 1) * blk:j * blk, :]
            s_c = _bdot(k_ref[:, rows, :], qb, 2, 2) * DIL_SCALE + bc
            s_p = _bdot(kp, qb, 2, 2) * DIL_SCALE + bp
            if j == 0:
                s_p = jnp.where(first, NEG, s_p)
            m = jnp.maximum(jnp.max(s_c, axis=1, keepdims=True), jnp.max(s_p, axis=1, keepdims=True))
            p_c = jnp.exp(s_c - m)
            p_p = jnp.exp(s_p - m)
            l = jnp.sum(p_c, axis=1, keepdims=True) + jnp.sum(p_p, axis=1, keepdims=True)
            o = _bdot(_mx(p_c), v_ref[:, rows, :], 1, 1) + _bdot(_mx(p_p), vp, 1, 1)
            o_ref[:, rows, :] = o / jnp.swapaxes(l, 1, 2)
            lse_ref[:, :, rows] = m + jnp.log(l)

    cur = lambda w: pl.BlockSpec((hds, pair * blk, w), lambda b: (0, b, 0))
    prev = lambda w: pl.BlockSpec((hds, blk, w), lambda b: (0, jnp.maximum(pair * b - 1, 0), 0))
    bias = pl.BlockSpec((hds, blk, blk), lambda b: (0, 0, 0))
    return pl.pallas_call(
        body, name=name,
        out_shape=(jax.ShapeDtypeStruct((hds, s, e), F32), jax.ShapeDtypeStruct((hds, 1, s), F32)),
        grid=(nblk // pair,),
        in_specs=[cur(e), cur(e), prev(e), cur(e), prev(e), bias, bias],
        out_specs=(cur(e), pl.BlockSpec((hds, 1, pair * blk), lambda b: (0, 0, b))),
        compiler_params=_params("parallel"),
    )(q, k, k, v, v, b_c, b_p)


def _dil_combine(os_, lses, *, ts):
    hds, s, e = os_[0].shape
    dils = [d for _, d in DIL_PAIRS]

    def body(o0, o1, o2, l0, l1, l2, o_ref, l_ref, sc1, sc2):
        _load_token_order(sc1, o1, dils[1], ts)
        _load_token_order(sc2, o2, dils[2], ts)
        a0, a1, a2 = l0[...], l1[...], l2[...]
        m = jnp.maximum(jnp.maximum(a0, a1), a2)
        e0, e1, e2 = jnp.exp(a0 - m), jnp.exp(a1 - m), jnp.exp(a2 - m)
        tot = e0 + e1 + e2
        col = lambda w: jnp.swapaxes(w, 1, 2)
        res = (col(e0 / tot) * o0[...] + col(e1 / tot) * sc1[...]) + col(e2 / tot) * sc2[...]
        for hd in range(hds):
            o_ref[:, hd * e:(hd + 1) * e] = res[hd]
        l_ref[...] = m + jnp.log(tot)

    _, specs = _residue_major_outs(s, ts, dils, F32)
    view = lambda a, d: a if d == 1 else a.reshape(hds, d, s // d, e)
    rspec = pl.BlockSpec((hds, 1, ts), lambda i: (0, 0, i))
    return pl.pallas_call(
        body, name="dil_combine",
        out_shape=(jax.ShapeDtypeStruct((s, hds * e), F32), jax.ShapeDtypeStruct((hds, 1, s), F32)),
        grid=(s // ts,),
        in_specs=list(specs) + [rspec] * 3,
        out_specs=(pl.BlockSpec((ts, hds * e), lambda i: (i, 0)), rspec),
        scratch_shapes=[pltpu.VMEM((hds, ts, e), F32), pltpu.VMEM((hds, ts, e), F32)],
        compiler_params=_params("parallel"),
    )(*[view(a, d) for a, d in zip(os_, dils)], *lses)


def _dil_bwd(q, k, v, do, lj, dd, dil, *, name):
    hds, s, e = q.shape
    blk = DIL_BLOCK
    nblk = s // blk
    nb = nblk // dil
    pair = next(p for p in (4, 2, 1) if nb % p == 0)
    b_c, b_p = _dil_bias(dil)

    def body(q_ref, qn_ref, k_ref, kp_ref, v_ref, vp_ref, do_ref, don_ref, l_ref, ln_ref, d_ref, dn_ref,
             bc_ref, bp_ref, dq_ref, dk_ref, dv_ref):
        b0 = pair * pl.program_id(0)
        first = (b0 % nb) == 0
        nxt = jnp.logical_and(b0 + pair < nblk, ((b0 + pair) % nb) != 0)
        bc, bp = bc_ref[...], bp_ref[...]
        for j in range(pair):
            rows = slice(j * blk, (j + 1) * blk)
            qb, kc, vc = q_ref[:, rows, :], k_ref[:, rows, :], v_ref[:, rows, :]
            dob, l, d = _mx(do_ref[:, rows, :]), l_ref[:, :, rows], d_ref[:, :, rows]
            if j == 0:
                kp, vp = kp_ref[...], vp_ref[...]
            else:
                kp, vp = k_ref[:, (j - 1) * blk:j * blk, :], v_ref[:, (j - 1) * blk:j * blk, :]
            p_c = jnp.exp(_bdot(kc, qb, 2, 2) * DIL_SCALE + bc - l)
            p_p = jnp.exp(_bdot(kp, qb, 2, 2) * DIL_SCALE + bp - l)
            if j == 0:
                p_p = jnp.where(first, 0.0, p_p)
            ds_c = _mx(p_c * (_bdot(vc, dob, 2, 2) - d) * DIL_SCALE)
            ds_p = _mx(p_p * (_bdot(vp, dob, 2, 2) - d) * DIL_SCALE)
            dq_ref[:, rows, :] = _bdot(ds_c, kc, 1, 1) + _bdot(ds_p, kp, 1, 1)
            if j < pair - 1:
                nrows = slice((j + 1) * blk, (j + 2) * blk)
                qn, donb, ln, dn = q_ref[:, nrows, :], _mx(do_ref[:, nrows, :]), l_ref[:, :, nrows], d_ref[:, :, nrows]
            else:
                qn, donb, ln, dn = qn_ref[...], _mx(don_ref[...]), ln_ref[...], dn_ref[...]
            p_n = jnp.exp(_bdot(kc, qn, 2, 2) * DIL_SCALE + bp - ln)
            if j == pair - 1:
                p_n = jnp.where(nxt, p_n, 0.0)
            ds_n = _mx(p_n * (_bdot(vc, donb, 2, 2) - dn) * DIL_SCALE)
            dk_ref[:, rows, :] = _bdot(ds_c, qb, 2, 1) + _bdot(ds_n, qn, 2, 1)
            dv_ref[:, rows, :] = _bdot(_mx(p_c), dob, 2, 1) + _bdot(_mx(p_n), donb, 2, 1)

    cur = lambda w: pl.BlockSpec((hds, pair * blk, w), lambda b: (0, b, 0))
    prev = lambda w: pl.BlockSpec((hds, blk, w), lambda b: (0, jnp.maximum(pair * b - 1, 0), 0))
    nxt_ = lambda w: pl.BlockSpec((hds, blk, w), lambda b: (0, jnp.minimum(pair * (b + 1), nblk - 1), 0))
    rcur = pl.BlockSpec((hds, 1, pair * blk), lambda b: (0, 0, b))
    rnxt = pl.BlockSpec((hds, 1, blk), lambda b: (0, 0, jnp.minimum(pair * (b + 1), nblk - 1)))
    bias = pl.BlockSpec((hds, blk, blk), lambda b: (0, 0, 0))
    out = jax.ShapeDtypeStruct((hds, s, e), F32)
    return pl.pallas_call(
        body, name=name,
        out_shape=(out, out, out),
        grid=(nblk // pair,),
        in_specs=[cur(e), nxt_(e), cur(e), prev(e), cur(e), prev(e), cur(e), nxt_(e),
                  rcur, rnxt, rcur, rnxt, bias, bias],
        out_specs=(cur(e), cur(e), cur(e)),
        compiler_params=_params("parallel"),
    )(q, q, k, k, v, v, do, do, lj, lj, dd, dd, b_c, b_p)


def _ln_fwd(z, g, b):
    mu = jnp.mean(z, axis=-1, keepdims=True)
    zc = z - mu
    var = jnp.mean(zc * zc, axis=-1, keepdims=True)
    rstd = lax.rsqrt(var + LN_EPS)
    xhat = zc * rstd
    return xhat * g + b, xhat, rstd


def _ln_bwd(dy, xhat, rstd, g):
    dxh = dy * g
    return rstd * (dxh - jnp.mean(dxh, axis=-1, keepdims=True) - xhat * jnp.mean(dxh * xhat, axis=-1, keepdims=True))


def _out_ln1(a_mla, a_dil, w_o, x, g, b, *, tm):
    s = x.shape[0]
    half = HEADS * HEAD_DIM

    def body(am_ref, ad_ref, w_ref, x_ref, g_ref, b_ref, x1_ref, xh_ref, r_ref):
        mix = _dot(_mx(am_ref[...]), w_ref[0:half, :]) + _dot(_mx(ad_ref[...]), w_ref[half:2 * half, :])
        z = DN_ALPHA * x_ref[...] + mix
        y, xhat, rstd = _ln_fwd(z, g_ref[...], b_ref[...])
        x1_ref[...] = y
        xh_ref[...] = xhat
        r_ref[...] = rstd

    row = lambda w: pl.BlockSpec((tm, w), lambda i: (i, 0))
    full = lambda shp: pl.BlockSpec(shp, lambda i: (0,) * len(shp))
    act = jax.ShapeDtypeStruct((s, D_MODEL), F32)
    return pl.pallas_call(
        body, name="out_ln1",
        out_shape=(act, act, jax.ShapeDtypeStruct((s, 1), F32)),
        grid=(s // tm,),
        in_specs=[row(half), row(half), full((D_MODEL, D_MODEL)), row(D_MODEL), full((1, D_MODEL)), full((1, D_MODEL))],
        out_specs=(row(D_MODEL), row(D_MODEL), row(1)),
        compiler_params=_params("parallel"),
    )(a_mla, a_dil, w_o, x, g, b)


def _down_ln2_loss(act, w_down, x1, g, b, target, *, tm):
    s = x1.shape[0]

    def body(a_ref, w_ref, x1_ref, g_ref, b_ref, t_ref, dz_ref, loss_ref, dg_ref, db_ref):
        i = pl.program_id(0)

        @pl.when(i == 0)
        def _():
            loss_ref[...] = jnp.zeros_like(loss_ref)
            dg_ref[...] = jnp.zeros_like(dg_ref)
            db_ref[...] = jnp.zeros_like(db_ref)

        gam = g_ref[...]
        z = DN_ALPHA * x1_ref[...] + _dot(a_ref[...], w_ref[...])
        y, xhat, rstd = _ln_fwd(z, gam, b_ref[...])
        err = y - t_ref[...]
        loss_ref[...] += 0.5 * jnp.sum(jnp.mean(err * err, axis=-1, keepdims=True))
        dy = err * (1.0 / D_MODEL)
        dg_ref[...] += jnp.sum(dy * xhat, axis=0, keepdims=True)
        db_ref[...] += jnp.sum(dy, axis=0, keepdims=True)
        dz_ref[...] = _ln_bwd(dy, xhat, rstd, gam)

    row = lambda w: pl.BlockSpec((tm, w), lambda i: (i, 0))
    full = lambda shp: pl.BlockSpec(shp, lambda i: (0,) * len(shp))
    vec = jax.ShapeDtypeStruct((1, D_MODEL), F32)
    return pl.pallas_call(
        body, name="down_ln2_loss",
        out_shape=(jax.ShapeDtypeStruct((s, D_MODEL), F32), jax.ShapeDtypeStruct((1, LANES), F32), vec, vec),
        grid=(s // tm,),
        in_specs=[row(D_FF), full((D_FF, D_MODEL)), row(D_MODEL), full((1, D_MODEL)), full((1, D_MODEL)), row(D_MODEL)],
        out_specs=(row(D_MODEL), full((1, LANES)), full((1, D_MODEL)), full((1, D_MODEL))),
        compiler_params=_params("arbitrary"),
    )(act, w_down, x1, g, b, target)


def _up_bwd_ln1(du_a, du_g, w_up_t, dz2, xhat1, rstd1, g, *, tm):
    s = dz2.shape[0]

    def body(dua_ref, dug_ref, wa_ref, wg_ref, dz2_ref, xh_ref, r_ref, g_ref, dz1_ref, dg_ref, db_ref):
        i = pl.program_id(0)

        @pl.when(i == 0)
        def _():
            dg_ref[...] = jnp.zeros_like(dg_ref)
            db_ref[...] = jnp.zeros_like(db_ref)

        dx1 = DN_ALPHA * dz2_ref[...] + (_dot(dua_ref[...], wa_ref[...]) + _dot(dug_ref[...], wg_ref[...]))
        xhat = xh_ref[...]
        dg_ref[...] += jnp.sum(dx1 * xhat, axis=0, keepdims=True)
        db_ref[...] += jnp.sum(dx1, axis=0, keepdims=True)
        dz1_ref[...] = _ln_bwd(dx1, xhat, r_ref[...], g_ref[...])

    row = lambda w: pl.BlockSpec((tm, w), lambda i: (i, 0))
    full = lambda shp: pl.BlockSpec(shp, lambda i: (0,) * len(shp))
    vec = jax.ShapeDtypeStruct((1, D_MODEL), F32)
    return pl.pallas_call(
        body, name="up_bwd_ln1",
        out_shape=(jax.ShapeDtypeStruct((s, D_MODEL), F32), vec, vec),
        grid=(s // tm,),
        in_specs=[row(D_FF), row(D_FF),
                  pl.BlockSpec((D_FF, D_MODEL), lambda i: (0, 0)), pl.BlockSpec((D_FF, D_MODEL), lambda i: (1, 0)),
                  row(D_MODEL), row(D_MODEL), row(1), full((1, D_MODEL))],
        out_specs=(row(D_MODEL), full((1, D_MODEL)), full((1, D_MODEL))),
        compiler_params=_params("arbitrary"),
    )(du_a, du_g, w_up_t, w_up_t, dz2, xhat1, rstd1, g)


GELU_C = math.sqrt(2.0 / math.pi)


def _gelu(x):
    cdf = 0.5 * (1.0 + jnp.tanh(GELU_C * (x + 0.044715 * (x * x * x))))
    return x * cdf


def _gelu_grad(x):
    t = jnp.tanh(GELU_C * (x + 0.044715 * (x * x * x)))
    return 0.5 * (1.0 + t) + 0.5 * x * (1.0 - t * t) * (GELU_C * (1.0 + 3.0 * 0.044715 * (x * x)))


def _shift_down(u, halo):
    r1, r2 = pltpu.roll(u, 1, 0), pltpu.roll(u, 2, 0)
    row = lax.broadcasted_iota(jnp.int32, (SUBLANES, u.shape[1]), 0)
    h7, h6 = halo[7:8, :], halo[6:7, :]
    head1 = jnp.where(row == 0, h7, r1[:SUBLANES])
    head2 = jnp.where(row == 0, h6, jnp.where(row == 1, h7, r2[:SUBLANES]))
    return (jnp.concatenate([head1, r1[SUBLANES:]], axis=0), jnp.concatenate([head2, r2[SUBLANES:]], axis=0))


def _shift_up(d, nxt):
    t = d.shape[0]
    r1, r2 = pltpu.roll(d, t - 1, 0), pltpu.roll(d, t - 2, 0)
    row = lax.broadcasted_iota(jnp.int32, (SUBLANES, d.shape[1]), 0)
    n0, n1 = nxt[0:1, :], nxt[1:2, :]
    last = t - SUBLANES
    tail1 = jnp.where(row == SUBLANES - 1, n0, r1[last:])
    tail2 = jnp.where(row == SUBLANES - 1, n1, jnp.where(row == SUBLANES - 2, n0, r2[last:]))
    return (jnp.concatenate([r1[:last], tail1], axis=0), jnp.concatenate([r2[:last], tail2], axis=0))


def _conv(u, s1, s2, w, b):
    return ((b + w[0:1, :] * s2) + w[1:2, :] * s1) + w[2:3, :] * u


def _up_gate_fwd(x1, w_up, conv_w, conv_b, *, tm, tn):
    s = x1.shape[0]
    nj = D_FF // tn
    hb = tm // SUBLANES

    def body(x_ref, xh_ref, wua_ref, wug_ref, wa_ref, wg_ref, ba_ref, bg_ref,
             ua_ref, ug_ref, o_ref, a_ref, ge_ref, gd_ref):
        keep = pl.program_id(1) > 0
        xb, xh = _mx(x_ref[...]), _mx(xh_ref[...])
        wua, wug = wua_ref[...], wug_ref[...]
        ua, ug = _dot(xb, wua), _dot(xb, wug)
        ha = jnp.where(keep, _dot(xh, wua), 0.0)
        hg = jnp.where(keep, _dot(xh, wug), 0.0)
        ua_ref[...] = ua
        ug_ref[...] = ug
        a = _conv(ua, *_shift_down(ua, ha), wa_ref[...], ba_ref[...])
        g = _conv(ug, *_shift_down(ug, hg), wg_ref[...], bg_ref[...])
        ge = _gelu(g)
        o_ref[...] = (ge * a).astype(o_ref.dtype)
        a_ref[...] = a
        ge_ref[...] = ge
        gd_ref[...] = _gelu_grad(g)

    main = lambda off: pl.BlockSpec((tm, tn), lambda j, i: (i, j + off))
    wspec = lambda r, off: pl.BlockSpec((r, tn), lambda j, i: (0, j + off))
    if w_up.ndim == 3:
        wu = lambda off: pl.BlockSpec((None, D_MODEL, tn), lambda j, i: (j + off, 0, 0))
    else:
        wu = lambda off: pl.BlockSpec((D_MODEL, tn), lambda j, i: (0, j + off))
    keep_f32 = jax.ShapeDtypeStruct((s, D_FF), F32)
    return pl.pallas_call(
        body, name="up_gate_fwd",
        out_shape=(keep_f32, keep_f32, jax.ShapeDtypeStruct((s, D_FF), MXU_DTYPE), keep_f32, keep_f32, keep_f32),
        grid=(nj, s // tm),
        in_specs=[pl.BlockSpec((tm, D_MODEL), lambda j, i: (i, 0)),
                  pl.BlockSpec((SUBLANES, D_MODEL), lambda j, i: (jnp.maximum(i * hb - 1, 0), 0)),
                  wu(0), wu(nj), wspec(3, 0), wspec(3, nj), wspec(1, 0), wspec(1, nj)],
        out_specs=(main(0),) * 6,
        compiler_params=_params("parallel", "parallel"),
    )(x1, x1, w_up, w_up, conv_w, conv_w, conv_b, conv_b)


def _gate_bwd(u_a, u_g, dz2, w_down_t, a, ge, gd, conv_w, *, tm, tn):
    s = u_a.shape[0]
    nj = D_FF // tn
    ni = s // tm
    hb = tm // SUBLANES

    def body(ua_ref, ug_ref, ha_ref, hg_ref, dz_ref, dzn_ref, wd_ref, a_ref, an_ref, ge_ref, gen_ref, gd_ref, gdn_ref,
             wa_ref, wg_ref, dua_ref, dug_ref, dwa_ref, dwg_ref, dba_ref, dbg_ref):
        i = pl.program_id(1)

        @pl.when(i == 0)
        def _():
            for r in (dwa_ref, dwg_ref, dba_ref, dbg_ref):
                r[...] = jnp.zeros_like(r)

        wa, wg = wa_ref[...], wg_ref[...]
        ua, ug = ua_ref[...], ug_ref[...]
        ha = jnp.where(i > 0, ha_ref[...], 0.0)
        hg = jnp.where(i > 0, hg_ref[...], 0.0)
        sa1, sa2 = _shift_down(ua, ha)
        sg1, sg2 = _shift_down(ug, hg)
        wd = wd_ref[...]
        d = _dot(_mx(dz_ref[...]), wd)
        dya = d * ge_ref[...]
        dyg = d * a_ref[...] * gd_ref[...]
        dn = jnp.where(i < ni - 1, _dot(_mx(dzn_ref[...]), wd), 0.0)
        dya_n = dn * gen_ref[...]
        dyg_n = dn * an_ref[...] * gdn_ref[...]
        da1, da2 = _shift_up(dya, dya_n)
        dg1, dg2 = _shift_up(dyg, dyg_n)
        dua_ref[...] = (wa[2:3, :] * dya + wa[1:2, :] * da1 + wa[0:1, :] * da2).astype(dua_ref.dtype)
        dug_ref[...] = (wg[2:3, :] * dyg + wg[1:2, :] * dg1 + wg[0:1, :] * dg2).astype(dug_ref.dtype)
        ssum = lambda v: jnp.sum(v, axis=0, keepdims=True)
        dwa_ref[...] += jnp.concatenate([ssum(dya * sa2), ssum(dya * sa1), ssum(dya * ua)], axis=0)
        dwg_ref[...] += jnp.concatenate([ssum(dyg * sg2), ssum(dyg * sg1), ssum(dyg * ug)], axis=0)
        dba_ref[...] += ssum(dya)
        dbg_ref[...] += ssum(dyg)

    main = pl.BlockSpec((tm, tn), lambda j, i: (i, j))
    halo = pl.BlockSpec((SUBLANES, tn), lambda j, i: (jnp.maximum(i * hb - 1, 0), j))
    next_row = lambda j, i: jnp.minimum((i + 1) * hb, s // SUBLANES - 1)
    nxt = pl.BlockSpec((SUBLANES, tn), lambda j, i: (next_row(j, i), j))
    wspec = lambda r, off: pl.BlockSpec((r, tn), lambda j, i: (0, j + off))
    return pl.pallas_call(
        body, name="gate_bwd",
        out_shape=(jax.ShapeDtypeStruct((s, D_FF), MXU_DTYPE), jax.ShapeDtypeStruct((s, D_FF), MXU_DTYPE),
                   jax.ShapeDtypeStruct((3, D_FF), F32), jax.ShapeDtypeStruct((3, D_FF), F32),
                   jax.ShapeDtypeStruct((1, D_FF), F32), jax.ShapeDtypeStruct((1, D_FF), F32)),
        grid=(nj, ni),
        in_specs=[main, main, halo, halo,
                  pl.BlockSpec((tm, D_MODEL), lambda j, i: (i, 0)),
                  pl.BlockSpec((SUBLANES, D_MODEL), lambda j, i: (next_row(j, i), 0)),
                  pl.BlockSpec((D_MODEL, tn), lambda j, i: (0, j))]
        + [main, nxt] * 3 + [wspec(3, 0), wspec(3, nj)],
        out_specs=(main, main, wspec(3, 0), wspec(3, 0), wspec(1, 0), wspec(1, 0)),
        compiler_params=_params("parallel", "arbitrary"),
    )(u_a, u_g, u_a, u_g, dz2, dz2, w_down_t, a, a, ge, ge, gd, gd, conv_w, conv_w)


def _prep_weights(w_in, w_uq, w_uk, w_uv, w_o, w_up, w_down):
    return {**_prep_weights_first(w_in, w_uq, w_uk, w_uv), **_prep_weights_late(w_o, w_up, w_down)}


def _prep_weights_late(w_o, w_up, w_down):
    w_o, w_up, w_down = _mx(w_o), _mx(w_up), _mx(w_down)
    w_up_t = w_up.T if w_up.ndim == 2 else w_up.transpose(0, 2, 1).reshape(2 * D_FF, D_MODEL)
    return dict(w_o=w_o, w_o_t=w_o.T, w_up=w_up, w_up_t=w_up_t, w_down=w_down, w_down_t=w_down.T)


def _prep_weights_first(w_in, w_uq, w_uk, w_uv):
    c = lambda a: a.astype(MXU_DTYPE)
    w_in = c(w_in)
    z = lambda w: jnp.zeros((D_MODEL, w), MXU_DTYPE)
    r0 = Q_RANK + KV_RANK
    w_in_ext = jnp.concatenate([w_in[:, :r0], z(NOPE), w_in[:, r0:r0 + ROPE], z(32), w_in[:, r0 + ROPE:]], axis=1)
    wq = jnp.pad(c(w_uq).transpose(1, 0, 2), ((0, 0), (0, 0), (0, QK_PAD - NOPE - ROPE)))
    wk = jnp.pad(c(w_uk).transpose(1, 0, 2), ((0, 0), (0, 0), (0, QK_PAD - NOPE)))
    wv = c(w_uv).transpose(1, 0, 2)
    t3 = lambda a: a.transpose(0, 2, 1)
    return dict(w_in=w_in_ext, w_in_t=w_in_ext.T, wq=wq, wq_t=t3(wq), wk=wk, wk_t=t3(wk), wv=wv, wv_t=t3(wv))


def _local_step(x, target, w, g_cq, g_ckv, ln1_g, ln1_b, conv_w, conv_b, ln2_g, ln2_b, comm=None):
    s = x.shape[0]
    tabs = _rope_tables(s)
    r2 = lambda a: a.reshape(1, -1)
    cb = r2(conv_b)
    dils = [d for _, d in DIL_PAIRS]

    h, qp, kp, vp = _in_proj(x, w["w_in"], tm=512)
    q, k, v, v_t = _mla_prep_fwd(h, r2(g_cq), r2(g_ckv), w["wq"], w["wk"], w["wv"], w["wv_t"], tabs, tm=256)
    if comm is None:
        o_mla_t, lse_mla = _mla_attn_fwd(q, k, v_t, t=512, g=HEADS)
    else:
        o_mla_t, lse_mla, *gathered = _mla_attn_fwd(q, k, v_t, t=512, g=HEADS, late=comm["late"])
        w = {**w, **comm["finish"](gathered)}
    o_bs, lse_bs = [], []
    for i, d in enumerate(dils):
        o_b, l_b = _dil_fwd(qp[i], kp[i], vp[i], d, name=f"dil_fwd_{d}")
        o_bs.append(o_b)
        lse_bs.append(_unperm_row(l_b, d))
    o_dil, lj = _dil_combine(o_bs, lse_bs, ts=512)
    o_mla = o_mla_t.transpose(2, 0, 1).reshape(s, HEADS * HEAD_DIM)
    x1, xhat1, rstd1 = _out_ln1(o_mla, o_dil, w["w_o"], x, r2(ln1_g), r2(ln1_b), tm=512)
    u_a, u_g, act, conv_a, gelu_g, gelu_dg = _up_gate_fwd(x1, w["w_up"], conv_w, cb, tm=256, tn=1408)
    dz2, loss, dg2, db2 = _down_ln2_loss(act, w["w_down"], x1, r2(ln2_g), r2(ln2_b), target, tm=512)

    dw_down = _mm_tn(act, dz2, name="dw_down", tm=1408, tn=D_MODEL, ts=DW_TOKENS)
    du_a, du_g, dcw_a, dcw_g, dcb_a, dcb_g = _gate_bwd(u_a, u_g, dz2, w["w_down_t"], conv_a, gelu_g, gelu_dg, conv_w,
                                                       tm=256, tn=1408)
    dz1, dg1, db1 = _up_bwd_ln1(du_a, du_g, w["w_up_t"], dz2, xhat1, rstd1, r2(ln1_g), tm=256)
    dw_up = jnp.concatenate([_mm_tn(x1, du_a, name="dw_up_a", tm=D_MODEL, tn=1408, ts=DW_TOKENS),
                             _mm_tn(x1, du_g, name="dw_up_g", tm=D_MODEL, tn=1408, ts=DW_TOKENS)], axis=1)
    named_early = [("w_up", dw_up), ("w_down", dw_down)]
    swap = () if comm is None else comm["blocked"](named_early)
    do_mla, do_dil, dd_all, received = _attn_bwd_heads(dz1, w["w_o_t"], o_mla, o_dil, tm=512, swap=swap)
    dw_o = jnp.concatenate([_mm_tn(o_mla, dz1, name="dw_o_mla", tm=512, tn=D_MODEL, ts=DW_TOKENS),
                            _mm_tn(o_dil, dz1, name="dw_o_dil", tm=512, tn=D_MODEL, ts=DW_TOKENS)], axis=0)
    dd_all = dd_all.T
    dd_mla, dd_dil = dd_all[:HEADS].reshape(HEADS, 1, s), dd_all[HEADS:].reshape(HEADS, 1, s)
    early = () if comm is None else tuple(comm["add_halves"](named_early, swap, received))
    dq, dk, dv, *early_slots = _mla_attn_bwd(q, k, v, do_mla, lse_mla, dd_mla, t=512, g=4, early=early)
    parts = []
    for i, d in enumerate(dils):
        parts.append(_dil_bwd(qp[i], kp[i], vp[i], do_dil[i], _perm_row(lj, d), _perm_row(dd_dil, d), d, name=f"dil_bwd_{d}"))
    dh_dil = _dil_merge(parts, ts=512)
    dh_mla, dwq, dwk, dwv, dgq, dgkv = _mla_prep_bwd(h, dq, dk, dv, r2(g_cq), r2(g_ckv),
                                                     w["wq_t"], w["wk_t"], w["wv_t"], tabs, tm=256)
    mla_w = 4 * LANES
    w_in_t = w["w_in_t"]
    grad_x = _mm_nn(dh_mla, w_in_t[:mla_w], name="in_bwd_mla", tm=512, tn=D_MODEL, tk=mla_w, add=dz1, add_scale=DN_ALPHA)
    grad_x = _mm_nn(dh_dil, w_in_t[mla_w:], name="in_bwd_dil", tm=512, tn=D_MODEL, tk=3 * HEADS * HEAD_DIM, add=grad_x)
    dw_mla = _mm_tn(x, dh_mla, name="dw_in_mla", tm=D_MODEL, tn=mla_w, ts=DW_TOKENS)
    dw_dil = _mm_tn(x, dh_dil, name="dw_in_dil", tm=D_MODEL, tn=3 * HEADS * HEAD_DIM, ts=DW_TOKENS)
    r0 = Q_RANK + KV_RANK
    grads = dict(
        w_in=jnp.concatenate([dw_mla[:, :r0], dw_mla[:, r0 + NOPE:r0 + NOPE + ROPE], dw_dil], axis=1),
        g_cq=dgq[0], g_ckv=dgkv[0],
        w_uq=dwq[:, :, :NOPE + ROPE].transpose(1, 0, 2),
        w_uk=dwk[:, :, :NOPE].transpose(1, 0, 2),
        w_uv=dwv.transpose(1, 0, 2),
        w_o=dw_o, ln1_g=dg1[0], ln1_b=db1[0], w_up=dw_up,
        conv_w=jnp.concatenate([dcw_a, dcw_g], axis=1), conv_b=jnp.concatenate([dcb_a, dcb_g], axis=1)[0],
        w_down=dw_down, ln2_g=dg2[0], ln2_b=db2[0])
    if comm is not None:
        grads["early"] = (early, tuple(early_slots))
    return loss[0, 0], grad_x, grads


N_CHIPS = 4
SHARDED = ("w_in", "w_uq", "w_o", "w_up", "conv_w", "w_down")
COL_SHARDED = ("w_in", "w_up", "conv_w")
SHARD_SHAPE = dict(w_in=(D_MODEL, IN_WIDTH // 4), w_uq=(Q_RANK // 4, HEADS, NOPE + ROPE), w_o=(D_MODEL // 4, D_MODEL),
                   w_up=(D_MODEL, 2 * D_FF // 4), conv_w=(3, 2 * D_FF // 4), w_down=(D_FF // 4, D_MODEL))
SMALL = ("g_cq", "g_ckv", "w_uk", "w_uv", "ln1_g", "ln1_b", "conv_b", "ln2_g", "ln2_b")
SMALL_SHAPE = dict(g_cq=(Q_RANK,), g_ckv=(KV_RANK,), w_uk=(KV_RANK, HEADS, NOPE), w_uv=(KV_RANK, HEADS, HEAD_DIM),
                   ln1_g=(D_MODEL,), ln1_b=(D_MODEL,), conv_b=(2 * D_FF,), ln2_g=(D_MODEL,), ln2_b=(D_MODEL,))
BIG = ("w_in", "w_uq", "w_o", "w_up", "w_down")
BIG_2D = dict(w_in=(D_MODEL, IN_WIDTH // 4), w_uq=(Q_RANK // 4, HEADS * (NOPE + ROPE)), w_o=(D_MODEL // 4, D_MODEL),
              w_up=(D_MODEL, 2 * D_FF // 4), w_down=(D_FF // 4, D_MODEL))
SMALL_G = SMALL + ("conv_w",)
SMALL_WIDE = ("w_uk", "w_uv")
SMALL_G_SHAPE = {**SMALL_SHAPE, "conv_w": (3, 2 * D_FF)}
SMALL_U_SHAPE = {**SMALL_SHAPE, "conv_w": (3, 2 * D_FF // 4)}


def _size(shape):
    return math.prod(shape)


def _padded_rows(n_elems, mult):
    return -(-n_elems // (LANES * mult)) * mult


SHARD_ROWS = {n: _padded_rows(_size(SHARD_SHAPE[n]), SUBLANES) for n in SHARDED}
R_SMALL = -(-sum(_size(SMALL_G_SHAPE[n]) for n in SMALL_G) // (LANES * LANES)) * LANES
GATHER_FIRST = ("w_in", "w_uq")
GATHER_LATE = ("w_o", "w_up", "w_down")
REDUCED_EARLY = ("w_up", "w_down")
REDUCED_LAST = ("w_in", "w_uq", "w_o")


def _rows(a, rows=None):
    flat = a.reshape(-1)
    rows = -(-flat.shape[0] // LANES) if rows is None else rows
    return jnp.pad(flat, (0, rows * LANES - flat.shape[0])).reshape(rows, LANES)


def _blocked(name, g):
    r, c = BIG_2D[name]
    a = g.reshape(r, N_CHIPS, c).transpose(1, 0, 2) if name in COL_SHARDED else g.reshape(N_CHIPS, r, c)
    return a.reshape(N_CHIPS, 2, r // 2, c)


def _pack_flat(t, names, rows=None):
    flat = jnp.concatenate([t[n].astype(F32).reshape(-1) for n in names])
    return _rows(flat, R_SMALL if rows is None else rows)


def _unpack_flat(buf, names, shapes):
    flat, out, r = buf.reshape(-1), {}, 0
    for n in names:
        out[n] = flat[r:r + _size(shapes[n])].reshape(shapes[n])
        r += _size(shapes[n])
    return out


def _from_chip_blocks(name, blocks):
    shp = SHARD_SHAPE[name]
    a = blocks.reshape(N_CHIPS, -1)[:, :_size(shp)].reshape((N_CHIPS,) + shp)
    if name in COL_SHARDED:
        return a.transpose(1, 0, 2).reshape(shp[0], N_CHIPS * shp[1])
    return a.reshape((N_CHIPS * shp[0],) + shp[1:])


ANY = pl.BlockSpec(memory_space=pl.ANY)
COMM_PARAMS = pltpu.CompilerParams(has_side_effects=True)


def _coords():
    return lax.axis_index("x"), lax.axis_index("y"), lax.axis_index("c")


def _other_chips(x, y):
    return [(1 - x, y), (x, 1 - y), (1 - x, 1 - y)]


def _remote(src, dst, send_sems, recv_sems, k, to):
    return pltpu.make_async_remote_copy(src_ref=src, dst_ref=dst, send_sem=send_sems.at[k], recv_sem=recv_sems.at[k],
                                        device_id=to, device_id_type=MESH)


def _gather_in_steps(wp_refs, wout_refs, send_sems, recv_sems, *, first, mid, last):
    x, y, c = _coords()
    me = 2 * x + y
    sib = (x, y, 1 - c)
    chips = _other_chips(x, y)
    n = len(wp_refs)
    pairs = [(j, t, px, py) for j, (px, py) in enumerate(chips) for t in range(n)]
    ici = [_remote(wp_refs[t].at[c], wout_refs[t].at[me, c], send_sems, recv_sems, j * n + t, (px, py, c))
           for j, t, px, py in pairs]
    fwd = [_remote(wout_refs[t].at[2 * px + py, c], wout_refs[t].at[2 * px + py, c], send_sems, recv_sems, (3 + j) * n + t, sib)
           for j, t, px, py in pairs]

    @pl.when(first)
    def _():
        for cp in ici:
            cp.start()

    @pl.when(mid)
    def _():
        for i, (j, t, px, py) in enumerate(pairs):
            _remote(wp_refs[t].at[c], wout_refs[t].at[2 * px + py, c], send_sems, recv_sems, j * n + t, (px, py, c)).wait_recv()
            fwd[i].start()

    @pl.when(last)
    def _():
        for j, t, px, py in pairs:
            k = 2 * px + py
            _remote(wout_refs[t].at[k, 1 - c], wout_refs[t].at[k, 1 - c], send_sems, recv_sems, (3 + j) * n + t, sib).wait_recv()
        for cp in ici + fwd:
            cp.wait_send()


def _swap_halves_in_steps(gs_refs, os_refs, send_sems, recv_sems, *, first, last):
    x, y, c = _coords()
    sib = (x, y, 1 - c)
    cps = [_remote(gs_refs[t].at[k, 1 - c], os_refs[t].at[k], send_sems, recv_sems, t * N_CHIPS + k, sib)
           for t in range(len(gs_refs)) for k in range(N_CHIPS)]

    @pl.when(first)
    def _():
        for cp in cps:
            cp.start()

    @pl.when(last)
    def _():
        for cp in cps:
            cp.wait_recv()
        for cp in cps:
            cp.wait_send()


def _exchange_in_steps(ps_refs, ss_refs, send_sems, recv_sems, *, first, last):
    x, y, c = _coords()
    me = 2 * x + y
    chips = _other_chips(x, y)
    n = len(ps_refs)
    sends = [_remote(ps_refs[t].at[2 * px + py], ss_refs[t].at[me], send_sems, recv_sems, j * n + t, (px, py, c))
             for j, (px, py) in enumerate(chips) for t in range(n)]

    @pl.when(first)
    def _():
        for cp in sends:
            cp.start()

    @pl.when(last)
    def _():
        for j, (px, py) in enumerate(chips):
            for t in range(n):
                _remote(ps_refs[t].at[me], ss_refs[t].at[2 * px + py], send_sems, recv_sems, j * n + t, (px, py, c)).wait_recv()
        for cp in sends:
            cp.wait_send()


def _gather_weights(wp, cwp):
    def body(wp_ref, cw_ref, wout_ref, cwout_ref, send_sems, recv_sems):
        x, y, c = _coords()
        me = 2 * x + y
        sib = (x, y, 1 - c)
        chips = _other_chips(x, y)
        sends = [_remote(wp_ref.at[c], wout_ref.at[me, c], send_sems, recv_sems, j, (px, py, c))
                 for j, (px, py) in enumerate(chips)]
        sends += [_remote(cw_ref, cwout_ref.at[me], send_sems, recv_sems, 3 + j, (px, py, c))
                  for j, (px, py) in enumerate(chips)]
        for cp in sends:
            cp.start()
        for j, (px, py) in enumerate(chips):
            k = 2 * px + py
            _remote(wp_ref.at[c], wout_ref.at[k, c], send_sems, recv_sems, j, (px, py, c)).wait_recv()
            fwd = _remote(wout_ref.at[k, c], wout_ref.at[k, c], send_sems, recv_sems, 6 + j, sib)
            fwd.start()
            sends.append(fwd)
        for j, (px, py) in enumerate(chips):
            k = 2 * px + py
            _remote(cw_ref, cwout_ref.at[k], send_sems, recv_sems, 3 + j, (px, py, c)).wait_recv()
            _remote(wout_ref.at[k, 1 - c], wout_ref.at[k, 1 - c], send_sems, recv_sems, 6 + j, sib).wait_recv()
        for cp in sends:
            cp.wait_send()

    return pl.pallas_call(
        body, name="gather_weights",
        out_shape=(jax.ShapeDtypeStruct((N_CHIPS,) + wp.shape, wp.dtype), jax.ShapeDtypeStruct((N_CHIPS,) + cwp.shape, cwp.dtype)),
        in_specs=[ANY, ANY], out_specs=(ANY, ANY),
        scratch_shapes=[pltpu.SemaphoreType.DMA((9,)), pltpu.SemaphoreType.DMA((9,))],
        compiler_params=COMM_PARAMS,
    )(wp, cwp)


def _exchange_sibling_halves(gs, whole, *, name):
    n, nw = len(gs), len(whole)

    def body(*refs):
        gs_refs, wh_refs = refs[:n], refs[n:n + nw]
        os_refs, ow_refs = refs[n + nw:2 * n + nw], refs[2 * n + nw:2 * (n + nw)]
        send_sems, recv_sems = refs[2 * (n + nw):]
        x, y, c = _coords()
        sib = (x, y, 1 - c)
        cps = [_remote(gs_refs[t].at[k, 1 - c], os_refs[t].at[k], send_sems, recv_sems, t * N_CHIPS + k, sib)
               for t in range(n) for k in range(N_CHIPS)]
        cps += [_remote(wh_refs[t], ow_refs[t], send_sems, recv_sems, n * N_CHIPS + t, sib) for t in range(nw)]
        for cp in cps:
            cp.start()
        for cp in cps:
            cp.wait_recv()
        for cp in cps:
            cp.wait_send()

    n_sem = n * N_CHIPS + nw
    return pl.pallas_call(
        body, name=name,
        out_shape=tuple(jax.ShapeDtypeStruct((N_CHIPS,) + a.shape[2:], F32) for a in gs)
        + tuple(jax.ShapeDtypeStruct(a.shape, F32) for a in whole),
        in_specs=[ANY] * (n + nw), out_specs=(ANY,) * (n + nw),
        scratch_shapes=[pltpu.SemaphoreType.DMA((n_sem,)), pltpu.SemaphoreType.DMA((n_sem,))],
        compiler_params=COMM_PARAMS,
    )(*gs, *whole)


def _exchange_chips(ps, whole):
    n, nw = len(ps), len(whole)
    per_chip = n + nw

    def body(*refs):
        ps_refs, wh_refs = refs[:n], refs[n:per_chip]
        ss_refs, sw_refs = refs[per_chip:per_chip + n], refs[per_chip + n:2 * per_chip]
        send_sems, recv_sems = refs[2 * per_chip:]
        x, y, c = _coords()
        me = 2 * x + y
        chips = _other_chips(x, y)
        sends = []
        for j, (px, py) in enumerate(chips):
            to = (px, py, c)
            for t in range(n):
                sends.append(_remote(ps_refs[t].at[2 * px + py], ss_refs[t].at[me], send_sems, recv_sems, j * per_chip + t, to))
            for t in range(nw):
                sends.append(_remote(wh_refs[t], sw_refs[t].at[me], send_sems, recv_sems, j * per_chip + n + t, to))
        for cp in sends:
            cp.start()
        for j, (px, py) in enumerate(chips):
            k, to = 2 * px + py, (px, py, c)
            for t in range(n):
                _remote(ps_refs[t].at[me], ss_refs[t].at[k], send_sems, recv_sems, j * per_chip + t, to).wait_recv()
            for t in range(nw):
                _remote(wh_refs[t], sw_refs[t].at[k], send_sems, recv_sems, j * per_chip + n + t, to).wait_recv()
        for cp in sends:
            cp.wait_send()

    n_sem = 3 * per_chip
    return pl.pallas_call(
        body, name="exchange_chips",
        out_shape=tuple(jax.ShapeDtypeStruct(a.shape, a.dtype) for a in ps)
        + tuple(jax.ShapeDtypeStruct((N_CHIPS,) + a.shape, a.dtype) for a in whole),
        in_specs=[ANY] * per_chip, out_specs=(ANY,) * per_chip,
        scratch_shapes=[pltpu.SemaphoreType.DMA((n_sem,)), pltpu.SemaphoreType.DMA((n_sem,))],
        compiler_params=COMM_PARAMS,
    )(*ps, *whole)


def _exchange_sibling_result(gh):
    n = len(gh)

    def body(*refs):
        gh_refs, out_refs, (send_sems, recv_sems) = refs[:n], refs[n:2 * n], refs[2 * n:]
        x, y, c = _coords()
        cps = [_remote(gh_refs[t], out_refs[t], send_sems, recv_sems, t, (x, y, 1 - c)) for t in range(n)]
        for cp in cps:
            cp.start()
        for cp in cps:
            cp.wait_recv()
        for cp in cps:
            cp.wait_send()

    return pl.pallas_call(
        body, name="exchange_sibling_result",
        out_shape=tuple(jax.ShapeDtypeStruct(a.shape, F32) for a in gh),
        in_specs=[ANY] * n, out_specs=(ANY,) * n,
        scratch_shapes=[pltpu.SemaphoreType.DMA((n,)), pltpu.SemaphoreType.DMA((n,))],
        compiler_params=COMM_PARAMS,
    )(*gh)


def _add_own_half(gs, recv, c_arr, *, name):
    _, rows, cols = recv.shape

    def body(c_ref, a_ref, b_ref, o_ref):
        o_ref[0] = (a_ref[0, 0] + b_ref[0]).astype(o_ref.dtype)

    return pl.pallas_call(
        body, name=name,
        out_shape=jax.ShapeDtypeStruct(recv.shape, GRAD_WIRE_DTYPE),
        grid_spec=pltpu.PrefetchScalarGridSpec(
            num_scalar_prefetch=1, grid=(N_CHIPS,),
            in_specs=[pl.BlockSpec((1, 1, rows, cols), lambda k, c_ref: (k, c_ref[0], 0, 0)),
                      pl.BlockSpec((1, rows, cols), lambda k, c_ref: (k, 0, 0))],
            out_specs=pl.BlockSpec((1, rows, cols), lambda k, c_ref: (k, 0, 0))),
        compiler_params=_params("parallel"),
    )(c_arr, gs, recv)


def _add2(a, b, *, name, out_dtype=F32):
    def body(a_ref, b_ref, o_ref):
        o_ref[...] = (a_ref[...] + b_ref[...]).astype(o_ref.dtype)

    return pl.pallas_call(body, name=name, out_shape=jax.ShapeDtypeStruct(a.shape, out_dtype))(a, b)


def _sum_slots(slots, *, tr, name):
    _, r, c = slots.shape

    def body(s_ref, o_ref):
        f = lambda k: s_ref[k].astype(F32)
        o_ref[...] = ((f(0) + f(1)) + f(2)) + f(3)

    return pl.pallas_call(
        body, name=name,
        out_shape=jax.ShapeDtypeStruct((r, c), F32),
        grid=(r // tr,),
        in_specs=[pl.BlockSpec((N_CHIPS, tr, c), lambda i: (0, i, 0))],
        out_specs=pl.BlockSpec((tr, c), lambda i: (i, 0)),
        compiler_params=_params("parallel"),
    )(slots)


def _adamw(w, g, m, v, *, tr, name):
    r, cols = w.shape

    def body(w_ref, g_ref, m_ref, v_ref, d_ref, nm_ref, nv_ref):
        g_ = g_ref[...]
        m_ = ADAM_B1 * m_ref[...] + (1.0 - ADAM_B1) * g_
        v_ = ADAM_B2 * v_ref[...] + (1.0 - ADAM_B2) * (g_ * g_)
        m_hat = m_ / (1.0 - ADAM_B1 ** ADAM_STEP)
        v_hat = v_ / (1.0 - ADAM_B2 ** ADAM_STEP)
        d_ref[...] = -ADAM_LR * (m_hat / (jnp.sqrt(v_hat) + ADAM_EPS) + ADAM_WD * w_ref[...])
        nm_ref[...] = m_
        nv_ref[...] = v_

    spec = pl.BlockSpec((tr, cols), lambda i: (i, 0))
    out = jax.ShapeDtypeStruct((r, cols), F32)
    return pl.pallas_call(
        body, name=name, out_shape=(out, out, out), grid=(r // tr,),
        in_specs=[spec] * 4, out_specs=(spec,) * 3,
        compiler_params=_params("parallel"),
    )(w, g, m, v)


WEIGHTS = ("w_in", "g_cq", "g_ckv", "w_uq", "w_uk", "w_uv", "w_o", "ln1_g", "ln1_b", "w_up", "conv_w", "conv_b",
           "w_down", "ln2_g", "ln2_b")


def kernel(x, w_in, g_cq, g_ckv, w_uq, w_uk, w_uv, w_o, ln1_g, ln1_b, w_up, conv_w, conv_b, w_down, ln2_g, ln2_b, loss_target, m_w_in, m_g_cq, m_g_ckv, m_w_uq, m_w_uk, m_w_uv, m_w_o, m_ln1_g, m_ln1_b, m_w_up, m_conv_w, m_conv_b, m_w_down, m_ln2_g, m_ln2_b, v_w_in, v_g_cq, v_g_ckv, v_w_uq, v_w_uk, v_w_uv, v_w_o, v_ln1_g, v_ln1_b, v_w_up, v_conv_w, v_conv_b, v_w_down, v_ln2_g, v_ln2_b):
    wts = dict(zip(WEIGHTS, (w_in, g_cq, g_ckv, w_uq, w_uk, w_uv, w_o, ln1_g, ln1_b, w_up, conv_w, conv_b, w_down, ln2_g, ln2_b)))
    mom = dict(zip(WEIGHTS, (m_w_in, m_g_cq, m_g_ckv, m_w_uq, m_w_uk, m_w_uv, m_w_o, m_ln1_g, m_ln1_b, m_w_up, m_conv_w, m_conv_b, m_w_down, m_ln2_g, m_ln2_b)))
    var = dict(zip(WEIGHTS, (v_w_in, v_g_cq, v_g_ckv, v_w_uq, v_w_uk, v_w_uv, v_w_o, v_ln1_g, v_ln1_b, v_w_up, v_conv_w, v_conv_b, v_w_down, v_ln2_g, v_ln2_b)))

    me = 2 * lax.axis_index("x") + lax.axis_index("y")
    my_c = lax.axis_index("c")
    c_arr = my_c.astype(jnp.int32).reshape(1)
    own = lambda slots, mine: lax.dynamic_update_index_in_dim(slots, mine, me, 0)

    def pack(names):
        return jnp.concatenate([_rows(_mx(wts[n]), SHARD_ROWS[n]) for n in names], axis=0).reshape(2, -1, LANES)

    def unpack(names, gathered, mine):
        buf, full, r = own(gathered, mine).reshape(N_CHIPS, -1, LANES), {}, 0
        for n in names:
            full[n] = _from_chip_blocks(n, buf[:, r:r + SHARD_ROWS[n]])
            r += SHARD_ROWS[n]
        return full

    wp_first = pack(GATHER_FIRST)
    cwp = _rows(conv_w, SHARD_ROWS["conv_w"])
    gathered, cwfull = _gather_weights(wp_first, cwp)
    full = unpack(GATHER_FIRST, gathered, wp_first)
    conv_w_full = _from_chip_blocks("conv_w", own(cwfull, cwp))
    w = _prep_weights_first(full["w_in"], full["w_uq"], w_uk, w_uv)
    late_halves = [_mx(wts[n]).reshape(2, BIG_2D[n][0] // 2, BIG_2D[n][1]) for n in GATHER_LATE]

    def finish(gathered_late):
        w_o_b, w_up_b, w_down_b = (own(a, mine).reshape((N_CHIPS,) + BIG_2D[n])
                                   for a, mine, n in zip(gathered_late, late_halves, GATHER_LATE))
        return _prep_weights_late(w_o_b.reshape(D_MODEL, D_MODEL), w_up_b, w_down_b.reshape(D_FF, D_MODEL))

    def blocked(named):
        return [_blocked(n, a) for n, a in named]

    def add_halves(named, gb, recv):
        return [_add_own_half(gb[i], recv[i], c_arr, name=f"add_half_{n}") for i, (n, _) in enumerate(named)]

    def halve(named, whole, wire):
        gb = blocked(named)
        recv = _exchange_sibling_halves(gb, list(whole), name="exchange_sibling_halves")
        return add_halves(named, gb, recv) + [_add2(a, recv[len(gb) + i], name=f"add_whole_{i}", out_dtype=wire[i])
                                              for i, a in enumerate(whole)]

    comm = dict(late=late_halves, finish=finish, blocked=blocked, add_halves=add_halves)
    loss, grad_x, g = _local_step(x[0], loss_target[0], w, g_cq, g_ckv, ln1_g, ln1_b, conv_w_full, conv_b, ln2_g, ln2_b, comm=comm)

    ps_early, slots_early = g.pop("early")
    g["loss"] = loss.reshape(1)
    narrow = tuple(n for n in SMALL_G if n not in SMALL_WIDE) + ("loss",)
    narrow_shape = {**SMALL_G_SHAPE, "loss": (1,)}
    r_narrow = _padded_rows(sum(_size(narrow_shape[n]) for n in narrow), SUBLANES)
    r_wide = _padded_rows(sum(_size(SMALL_G_SHAPE[n]) for n in SMALL_WIDE), 2 * SUBLANES)
    *ps_rest, pr, pw = halve([(n, g[n]) for n in REDUCED_LAST],
                             whole=[_pack_flat(g, narrow, r_narrow), _pack_flat(g, SMALL_WIDE, r_wide)],
                             wire=[F32, GRAD_WIRE_DTYPE])
    *slots_rest, slots_r, slots_w = _exchange_chips(ps_rest, [pr, pw])
    ps = {**dict(zip(REDUCED_LAST, ps_rest)), **dict(zip(REDUCED_EARLY, ps_early))}
    slots = {**dict(zip(REDUCED_LAST, slots_rest)), **dict(zip(REDUCED_EARLY, slots_early))}
    slots = [own(slots[n], lax.dynamic_index_in_dim(ps[n], me, 0, keepdims=False)) for n in BIG]
    g_half = [_sum_slots(slots[i], tr=slots[i].shape[1] // 2, name=f"sum_chips_{n}") for i, n in enumerate(BIG)]
    g_small = {**_unpack_flat(_sum_slots(own(slots_r, pr), tr=r_narrow, name="sum_chips_narrow"), narrow, narrow_shape),
               **_unpack_flat(_sum_slots(own(slots_w, pw), tr=r_wide, name="sum_chips_wide"), SMALL_WIDE, SMALL_G_SHAPE)}
    loss = g_small.pop("loss")[0]
    g_other = _exchange_sibling_result(g_half)
    grads = {n: jnp.where(my_c == 0, jnp.concatenate([g_half[i], g_other[i]]), jnp.concatenate([g_other[i], g_half[i]]))
             for i, n in enumerate(BIG)}
    g_small["conv_w"] = lax.dynamic_slice_in_dim(g_small["conv_w"], me * SHARD_SHAPE["conv_w"][1], SHARD_SHAPE["conv_w"][1], 1)
    grads.update(g_small)

    res = {}
    for n in BIG:
        as2d = lambda a: a.reshape(BIG_2D[n])
        d, m, v = _adamw(as2d(wts[n]), grads[n], as2d(mom[n]), as2d(var[n]), tr=BIG_2D[n][0] // 4, name=f"adamw_{n}")
        res[n] = [a.reshape(SHARD_SHAPE[n]) for a in (grads[n], d, m, v)]
    flat = lambda t: _pack_flat(t, SMALL_G)
    dmv = _adamw(flat(wts), flat(g_small), flat(mom), flat(var), tr=R_SMALL, name="adamw_small")
    dmv = [_unpack_flat(a, SMALL_G, SMALL_U_SHAPE) for a in dmv]
    for n in SMALL_G:
        res[n] = [g_small[n]] + [t[n] for t in dmv]
    outs = [res[n][j] for j in range(4) for n in WEIGHTS]
    return (loss, grad_x[None], *outs)
```

```python
import math

import jax
import jax.numpy as jnp
from jax import lax
from jax.experimental import pallas as pl
from jax.experimental.pallas import tpu as pltpu

F32 = jnp.float32
MXU_DTYPE = jnp.bfloat16
GRAD_WIRE_DTYPE = jnp.bfloat16
NEG = -1e30

D_MODEL = 1024
HEADS = 8
HEAD_DIM = 64
Q_RANK = 256
KV_RANK = 128
NOPE = 64
ROPE = 32
QK_PAD = 128
IN_WIDTH = 1952
IN_EXT = 2048
D_FF = 2816
DIL_PAIRS = ((128, 1), (512, 4), (2048, 16))
DIL_BLOCK = 128
ROPE_THETA = 10000.0
DN_ALPHA = 2.0 ** 0.25
LN_EPS = 1e-5
RMS_EPS = 1e-6
MLA_SCALE = 1.0 / math.sqrt(NOPE + ROPE)
LOG2_E = math.log2(math.e)
DIL_SCALE = 1.0 / math.sqrt(HEAD_DIM)

ADAM_LR = 0.001
ADAM_B1 = 0.9
ADAM_B2 = 0.999
ADAM_EPS = 1e-08
ADAM_WD = 0.01
ADAM_STEP = 10

LANES = 128
SUBLANES = 8
VMEM_LIMIT_BYTES = 56 * 1024 * 1024
DW_TOKENS = 2048

MESH = pl.DeviceIdType.MESH


def _params(*sem):
    return pltpu.CompilerParams(dimension_semantics=sem, vmem_limit_bytes=VMEM_LIMIT_BYTES)


def _dot(a, b):
    return jnp.dot(a, b, preferred_element_type=F32)


def _dot_nt(a, b):
    return lax.dot_general(a, b, (((1,), (1,)), ((), ())), preferred_element_type=F32)


def _dot_tn(a, b):
    return lax.dot_general(a, b, (((0,), (0,)), ((), ())), preferred_element_type=F32)


def _mx(a):
    return a.astype(MXU_DTYPE)


def _mm_nn(a, b, *, name, tm, tn, tk, out_dtype=F32, add=None, add_scale=1.0):
    m, kdim = a.shape
    blocked = b.ndim == 3
    n = b.shape[0] * b.shape[2] if blocked else b.shape[1]
    nk = kdim // tk

    def body(*refs):
        if add is None:
            a_ref, b_ref, o_ref, acc = refs
        else:
            a_ref, b_ref, c_ref, o_ref, acc = refs
        k = pl.program_id(2)

        @pl.when(k == 0)
        def _():
            acc[...] = jnp.zeros_like(acc)

        acc[...] += _dot(_mx(a_ref[...]), _mx(b_ref[...]))

        @pl.when(k == nk - 1)
        def _():
            r = acc[...]
            if add is not None:
                r = r + add_scale * c_ref[...]
            o_ref[...] = r.astype(out_dtype)

    b_spec = (pl.BlockSpec((None, tk, tn), lambda i, j, k: (j, k, 0)) if blocked
              else pl.BlockSpec((tk, tn), lambda i, j, k: (k, j)))
    in_specs = [pl.BlockSpec((tm, tk), lambda i, j, k: (i, k)), b_spec]
    args = [a, b]
    if add is not None:
        in_specs.append(pl.BlockSpec((tm, tn), lambda i, j, k: (i, j)))
        args.append(add)
    return pl.pallas_call(
        body, name=name,
        out_shape=jax.ShapeDtypeStruct((m, n), out_dtype),
        grid=(m // tm, n // tn, nk),
        in_specs=in_specs,
        out_specs=pl.BlockSpec((tm, tn), lambda i, j, k: (i, j)),
        scratch_shapes=[pltpu.VMEM((tm, tn), F32)],
        compiler_params=_params("parallel", "parallel", "arbitrary"),
    )(*args)


def _mm_tn(a, b, *, name, tm, tn, ts, out_dtype=F32):
    s, m = a.shape
    n = b.shape[1]
    ns = s // ts

    def body(a_ref, b_ref, o_ref, acc):
        k = pl.program_id(2)

        @pl.when(k == 0)
        def _():
            acc[...] = jnp.zeros_like(acc)

        acc[...] += _dot_tn(_mx(a_ref[...]), _mx(b_ref[...]))

        @pl.when(k == ns - 1)
        def _():
            o_ref[...] = acc[...].astype(out_dtype)

    return pl.pallas_call(
        body, name=name,
        out_shape=jax.ShapeDtypeStruct((m, n), out_dtype),
        grid=(m // tm, n // tn, ns),
        in_specs=[pl.BlockSpec((ts, tm), lambda i, j, k: (k, i)),
                  pl.BlockSpec((ts, tn), lambda i, j, k: (k, j))],
        out_specs=pl.BlockSpec((tm, tn), lambda i, j, k: (i, j)),
        scratch_shapes=[pltpu.VMEM((tm, tn), F32)],
        compiler_params=_params("parallel", "parallel", "arbitrary"),
    )(a, b)


def _in_proj(x, w_in_ext, *, tm):
    s = x.shape[0]
    mla_w = 4 * LANES
    dil_w = HEADS * HEAD_DIM
    dils = [d for _, d in DIL_PAIRS]

    def body(x_ref, w_ref, h_ref, *rest):
        outs, sc = rest[:-1], rest[-1]
        xb = _mx(x_ref[...])
        h_ref[...] = _dot(xb, w_ref[:, 0:mla_w])
        for j in range(3):
            part = _dot(xb, w_ref[:, mla_w + j * dil_w:mla_w + (j + 1) * dil_w])
            for hd in range(HEADS):
                sc[hd] = part[:, hd * HEAD_DIM:(hd + 1) * HEAD_DIM]
            for b, d in enumerate(dils):
                _store_residue_major(outs[3 * j + b], sc, d, tm)

    shapes, specs = _residue_major_outs(s, tm, dils, MXU_DTYPE)
    res = pl.pallas_call(
        body, name="in_proj",
        out_shape=(jax.ShapeDtypeStruct((s, mla_w), F32),) + shapes * 3,
        grid=(s // tm,),
        in_specs=[pl.BlockSpec((tm, D_MODEL), lambda i: (i, 0)), pl.BlockSpec((D_MODEL, IN_EXT), lambda i: (0, 0))],
        out_specs=(pl.BlockSpec((tm, mla_w), lambda i: (i, 0)),) + specs * 3,
        scratch_shapes=[pltpu.VMEM((HEADS, tm, HEAD_DIM), F32)],
        compiler_params=_params("parallel"),
    )(x, w_in_ext)
    hm = lambda a: a.reshape(HEADS, s, HEAD_DIM)
    return res[0], [hm(a) for a in res[1:4]], [hm(a) for a in res[4:7]], [hm(a) for a in res[7:10]]


def _residue_major_outs(s, tm, dils, dtype):
    shapes, specs = [], []
    for d in dils:
        if d == 1:
            shapes.append(jax.ShapeDtypeStruct((HEADS, s, HEAD_DIM), dtype))
            specs.append(pl.BlockSpec((HEADS, tm, HEAD_DIM), lambda i: (0, i, 0)))
        else:
            shapes.append(jax.ShapeDtypeStruct((HEADS, d, s // d, HEAD_DIM), dtype))
            specs.append(pl.BlockSpec((HEADS, d, tm // d, HEAD_DIM), lambda i: (0, 0, i, 0)))
    return tuple(shapes), tuple(specs)


def _store_residue_major(o_ref, src_ref, d, tm):
    if d == 1:
        o_ref[...] = src_ref[...].astype(o_ref.dtype)
    else:
        for r in range(d):
            o_ref[:, r] = src_ref[:, pl.ds(r, tm // d, stride=d), :].astype(o_ref.dtype)


def _load_token_order(dst_ref, src_ref, d, tm, accumulate=False):
    if d == 1:
        dst_ref[...] = dst_ref[...] + src_ref[...] if accumulate else src_ref[...]
    else:
        for r in range(d):
            rows = pl.ds(r, tm // d, stride=d)
            dst_ref[:, rows, :] = dst_ref[:, rows, :] + src_ref[:, r] if accumulate else src_ref[:, r]


def _attn_bwd_heads(dz1, w_o_t, a_mla, a_dil, *, tm, swap=()):
    s = dz1.shape[0]
    half = HEADS * HEAD_DIM
    dils = [d for _, d in DIL_PAIRS]
    nsw = len(swap)
    n_steps = s // tm

    def body(*refs):
        dz_ref, w_ref, am_ref, ad_ref = refs[:4]
        gs_refs = refs[4:4 + nsw]
        dom_ref, dd_ref = refs[4 + nsw:6 + nsw]
        dod_refs = refs[6 + nsw:6 + nsw + len(dils)]
        os_refs = refs[6 + nsw + len(dils):6 + 2 * nsw + len(dils)]
        if nsw:
            send_sems, recv_sems = refs[6 + 2 * nsw + len(dils):]
            i = pl.program_id(0)
            _swap_halves_in_steps(gs_refs, os_refs, send_sems, recv_sems, first=i == 0, last=i == n_steps - 1)
        dzb = _mx(dz_ref[...])
        for j, (a_ref, o_ref) in enumerate(((am_ref, dom_ref), (ad_ref, dod_refs[0]))):
            da = _dot(dzb, w_ref[:, j * half:(j + 1) * half])
            prod = da * a_ref[...]
            for hd in range(HEADS):
                sl = slice(hd * HEAD_DIM, (hd + 1) * HEAD_DIM)
                o_ref[hd] = da[:, sl].astype(o_ref.dtype)
                dd_ref[:, j * HEADS + hd:j * HEADS + hd + 1] = jnp.sum(prod[:, sl], axis=-1, keepdims=True)
        for b, d in enumerate(dils[1:]):
            _store_residue_major(dod_refs[1 + b], dod_refs[0], d, tm)

    hspec = pl.BlockSpec((HEADS, tm, HEAD_DIM), lambda i: (0, i, 0))
    row = lambda w: pl.BlockSpec((tm, w), lambda i: (i, 0))
    shapes, specs = _residue_major_outs(s, tm, dils, F32)
    n_sem = nsw * N_CHIPS
    do_mla, dd, *rest = pl.pallas_call(
        body, name="attn_bwd_heads",
        out_shape=(jax.ShapeDtypeStruct((HEADS, s, HEAD_DIM), MXU_DTYPE), jax.ShapeDtypeStruct((s, 2 * HEADS), F32)) + shapes
        + tuple(jax.ShapeDtypeStruct((N_CHIPS,) + a.shape[2:], F32) for a in swap),
        grid=(n_steps,),
        in_specs=[row(D_MODEL), pl.BlockSpec((D_MODEL, D_MODEL), lambda i: (0, 0)), row(half), row(half)] + [ANY] * nsw,
        out_specs=(hspec, row(2 * HEADS)) + specs + (ANY,) * nsw,
        scratch_shapes=[pltpu.SemaphoreType.DMA((n_sem,)), pltpu.SemaphoreType.DMA((n_sem,))] if nsw else [],
        compiler_params=pltpu.CompilerParams(dimension_semantics=("arbitrary",), vmem_limit_bytes=VMEM_LIMIT_BYTES,
                                             has_side_effects=nsw > 0),
    )(dz1, w_o_t, a_mla, a_dil, *swap)
    do_dil, received = rest[:len(dils)], rest[len(dils):]
    return do_mla, [a.reshape(HEADS, s, HEAD_DIM) for a in do_dil], dd, received


def _dil_merge(parts, *, ts):
    hds, s, e = parts[0][0].shape
    dils = [d for _, d in DIL_PAIRS]

    def body(*refs):
        o_ref, sc = refs[9], refs[10]
        for j in range(3):
            for b, d in enumerate(dils):
                _load_token_order(sc, refs[3 * b + j], d, ts, accumulate=b > 0)
            tot = sc[...]
            for hd in range(hds):
                col = j * hds * e + hd * e
                o_ref[:, col:col + e] = tot[hd].astype(o_ref.dtype)

    _, specs = _residue_major_outs(s, ts, dils, F32)
    view = lambda a, d: a if d == 1 else a.reshape(hds, d, s // d, e)
    return pl.pallas_call(
        body, name="dil_merge",
        out_shape=jax.ShapeDtypeStruct((s, 3 * hds * e), MXU_DTYPE),
        grid=(s // ts,),
        in_specs=[specs[b] for b in range(3) for _ in range(3)],
        out_specs=pl.BlockSpec((ts, 3 * hds * e), lambda i: (i, 0)),
        scratch_shapes=[pltpu.VMEM((hds, ts, e), F32)],
        compiler_params=_params("parallel"),
    )(*[view(parts[b][j], dils[b]) for b in range(3) for j in range(3)])


def _rope_tables(s):
    half = ROPE // 2
    freqs = ROPE_THETA ** (-jnp.arange(half, dtype=F32) / half)
    ang = jnp.arange(s).astype(F32)[:, None] * freqs[None, :]
    cos, sin = jnp.cos(ang), jnp.sin(ang)
    z = lambda w: jnp.zeros((s, w), F32)
    c = jnp.concatenate([jnp.ones((s, NOPE), F32), cos, cos, z(32)], axis=1)
    s1 = jnp.concatenate([z(NOPE + half), sin, z(32)], axis=1)
    s2 = jnp.concatenate([z(NOPE), -sin, z(half + 32)], axis=1)
    mask = jnp.concatenate([z(NOPE), jnp.ones((s, ROPE), F32), z(32)], axis=1)
    return c, s1, s2, mask


def _rope(x, c, s1, s2):
    return x * c + pltpu.roll(x, 16, 1) * s1 + pltpu.roll(x, LANES - 16, 1) * s2


def _unrope(dy, c, s1, s2):
    return dy * c + pltpu.roll(dy * s1, LANES - 16, 1) + pltpu.roll(dy * s2, 16, 1)


def _rms(x):
    r = lax.rsqrt(jnp.mean(x * x, axis=-1, keepdims=True) + RMS_EPS)
    return x * r, r


def _mla_prep_fwd(h, g_cq, g_ckv, wq, wk, wv, wv_t, tabs, *, tm):
    s = h.shape[0]
    c_t, s1_t, s2_t, _ = tabs

    def body(h_ref, gq_ref, gkv_ref, wq_ref, wk_ref, wv_ref, wvt_ref, c_ref, s1_ref, s2_ref,
             q_ref, k_ref, v_ref, vt_ref):
        cq = h_ref[:, 0:Q_RANK]
        ckv = h_ref[:, Q_RANK:Q_RANK + KV_RANK]
        kr = h_ref[:, Q_RANK + KV_RANK:Q_RANK + KV_RANK + QK_PAD]
        c, s1, s2 = c_ref[...], s1_ref[...], s2_ref[...]
        cqn = _mx(_rms(cq)[0] * gq_ref[...])
        ckvn = _mx(_rms(ckv)[0] * gkv_ref[...])
        kr_rot = _rope(kr, c, s1, s2)
        for hd in range(HEADS):
            q_ref[hd] = _rope(_dot(cqn, wq_ref[hd]), c, s1, s2).astype(q_ref.dtype)
            k_ref[hd] = (_dot(ckvn, wk_ref[hd]) + kr_rot).astype(k_ref.dtype)
            v_ref[hd] = _dot(ckvn, wv_ref[hd]).astype(v_ref.dtype)
            vt_ref[hd] = _dot_nt(wvt_ref[hd], ckvn).astype(vt_ref.dtype)

    full = lambda shp: pl.BlockSpec(shp, lambda i: (0,) * len(shp))
    row = lambda w: pl.BlockSpec((tm, w), lambda i: (i, 0))
    return pl.pallas_call(
        body, name="mla_prep_fwd",
        out_shape=(jax.ShapeDtypeStruct((HEADS, s, QK_PAD), MXU_DTYPE),
                   jax.ShapeDtypeStruct((HEADS, s, QK_PAD), MXU_DTYPE),
                   jax.ShapeDtypeStruct((HEADS, s, HEAD_DIM), MXU_DTYPE),
                   jax.ShapeDtypeStruct((HEADS, HEAD_DIM, s), MXU_DTYPE)),
        grid=(s // tm,),
        in_specs=[row(4 * LANES), full((1, Q_RANK)), full((1, KV_RANK)),
                  full((HEADS, Q_RANK, QK_PAD)), full((HEADS, KV_RANK, QK_PAD)), full((HEADS, KV_RANK, HEAD_DIM)),
                  full((HEADS, HEAD_DIM, KV_RANK)), row(LANES), row(LANES), row(LANES)],
        out_specs=(pl.BlockSpec((HEADS, tm, QK_PAD), lambda i: (0, i, 0)),
                   pl.BlockSpec((HEADS, tm, QK_PAD), lambda i: (0, i, 0)),
                   pl.BlockSpec((HEADS, tm, HEAD_DIM), lambda i: (0, i, 0)),
                   pl.BlockSpec((HEADS, HEAD_DIM, tm), lambda i: (0, 0, i))),
        compiler_params=_params("parallel"),
    )(h, g_cq, g_ckv, wq, wk, wv, wv_t, c_t, s1_t, s2_t)


def _mla_prep_bwd(h, dq, dk, dv, g_cq, g_ckv, wq_t, wk_t, wv_t, tabs, *, tm):
    s = h.shape[0]
    c_t, s1_t, s2_t, mask_t = tabs

    def body(h_ref, dq_ref, dk_ref, dv_ref, gq_ref, gkv_ref, wqt_ref, wkt_ref, wvt_ref,
             c_ref, s1_ref, s2_ref, mask_ref, dh_ref, dwq_ref, dwk_ref, dwv_ref, dgq_ref, dgkv_ref):
        i = pl.program_id(0)

        @pl.when(i == 0)
        def _():
            dwq_ref[...] = jnp.zeros_like(dwq_ref)
            dwk_ref[...] = jnp.zeros_like(dwk_ref)
            dwv_ref[...] = jnp.zeros_like(dwv_ref)
            dgq_ref[...] = jnp.zeros_like(dgq_ref)
            dgkv_ref[...] = jnp.zeros_like(dgkv_ref)

        cq = h_ref[:, 0:Q_RANK]
        ckv = h_ref[:, Q_RANK:Q_RANK + KV_RANK]
        c, s1, s2 = c_ref[...], s1_ref[...], s2_ref[...]
        cqh, rq = _rms(cq)
        ckvh, rkv = _rms(ckv)
        gq, gkv = gq_ref[...], gkv_ref[...]
        cqn = _mx(cqh * gq)
        ckvn = _mx(ckvh * gkv)
        dcqn = jnp.zeros((tm, Q_RANK), F32)
        dckvn = jnp.zeros((tm, KV_RANK), F32)
        dkr = jnp.zeros((tm, QK_PAD), F32)
        for hd in range(HEADS):
            dqh = _mx(_unrope(dq_ref[hd], c, s1, s2))
            dcqn = dcqn + _dot(dqh, wqt_ref[hd])
            dwq_ref[hd] += _dot_tn(cqn, dqh)
            dkh = dk_ref[hd]
            dkr = dkr + dkh
            dkh = _mx(dkh)
            dckvn = dckvn + _dot(dkh, wkt_ref[hd])
            dwk_ref[hd] += _dot_tn(ckvn, dkh)
            dvh = _mx(dv_ref[hd])
            dckvn = dckvn + _dot(dvh, wvt_ref[hd])
            dwv_ref[hd] += _dot_tn(ckvn, dvh)
        dgq_ref[...] += jnp.sum(dcqn * cqh, axis=0, keepdims=True)
        dgkv_ref[...] += jnp.sum(dckvn * ckvh, axis=0, keepdims=True)
        gd = dcqn * gq
        dh_ref[:, 0:Q_RANK] = rq * (gd - cqh * jnp.mean(gd * cqh, axis=-1, keepdims=True))
        gd = dckvn * gkv
        dh_ref[:, Q_RANK:Q_RANK + KV_RANK] = rkv * (gd - ckvh * jnp.mean(gd * ckvh, axis=-1, keepdims=True))
        dh_ref[:, Q_RANK + KV_RANK:Q_RANK + KV_RANK + QK_PAD] = _unrope(dkr, c, s1, s2) * mask_ref[...]

    full = lambda shp: pl.BlockSpec(shp, lambda i: (0,) * len(shp))
    row = lambda w: pl.BlockSpec((tm, w), lambda i: (i, 0))
    hrow = lambda w: pl.BlockSpec((HEADS, tm, w), lambda i: (0, i, 0))
    return pl.pallas_call(
        body, name="mla_prep_bwd",
        out_shape=(jax.ShapeDtypeStruct((s, 4 * LANES), F32),
                   jax.ShapeDtypeStruct((HEADS, Q_RANK, QK_PAD), F32),
                   jax.ShapeDtypeStruct((HEADS, KV_RANK, QK_PAD), F32),
                   jax.ShapeDtypeStruct((HEADS, KV_RANK, HEAD_DIM), F32),
                   jax.ShapeDtypeStruct((1, Q_RANK), F32),
                   jax.ShapeDtypeStruct((1, KV_RANK), F32)),
        grid=(s // tm,),
        in_specs=[row(4 * LANES), hrow(QK_PAD), hrow(QK_PAD), hrow(HEAD_DIM),
                  full((1, Q_RANK)), full((1, KV_RANK)),
                  full((HEADS, QK_PAD, Q_RANK)), full((HEADS, QK_PAD, KV_RANK)), full((HEADS, HEAD_DIM, KV_RANK)),
                  row(LANES), row(LANES), row(LANES), row(LANES)],
        out_specs=(row(4 * LANES), full((HEADS, Q_RANK, QK_PAD)), full((HEADS, KV_RANK, QK_PAD)),
                   full((HEADS, KV_RANK, HEAD_DIM)), full((1, Q_RANK)), full((1, KV_RANK))),
        compiler_params=_params("arbitrary"),
    )(h, dq, dk, dv, g_cq, g_ckv, wq_t, wk_t, wv_t, c_t, s1_t, s2_t, mask_t)


def _bdot(a, b, ca, cb):
    return lax.dot_general(a, b, (((ca,), (cb,)), ((0,), (0,))), preferred_element_type=F32)


def _causal_mask_t(t):
    kk = lax.broadcasted_iota(jnp.int32, (t, t), 0)
    qq = lax.broadcasted_iota(jnp.int32, (t, t), 1)
    return (qq >= kk)[None]


def _mla_attn_fwd(q, k, v_t, *, t, g, late=None):
    hds, s, _ = q.shape
    n = s // t
    n_groups = hds // g

    nl = 0 if late is None else len(late)

    def body(*refs):
        q_ref, k_ref, vt_ref = refs[:3]
        wp_refs = refs[3:3 + nl]
        o_ref, lse_ref = refs[3 + nl:5 + nl]
        wout_refs = refs[5 + nl:5 + 2 * nl]
        m_sc, l_sc, acc_sc = refs[5 + 2 * nl:8 + 2 * nl]
        hg, qi, ki = pl.program_id(0), pl.program_id(1), pl.program_id(2)
        if nl:
            send_sems, recv_sems = refs[8 + 2 * nl:]
            tail = jnp.logical_and(hg == n_groups - 1, qi == n - 1)
            _gather_in_steps(wp_refs, wout_refs, send_sems, recv_sems,
                             first=jnp.logical_and(hg == 0, jnp.logical_and(qi == 0, ki == 0)),
                             mid=jnp.logical_and(tail, ki == 0), last=jnp.logical_and(tail, ki == n - 1))

        @pl.when(ki == 0)
        def _():
            m_sc[...] = jnp.full_like(m_sc, NEG)
            l_sc[...] = jnp.zeros_like(l_sc)
            acc_sc[...] = jnp.zeros_like(acc_sc)

        def step(masked):
            sc = _bdot(k_ref[...], q_ref[...], 2, 2)
            if masked:
                sc = jnp.where(_causal_mask_t(t), sc, NEG)
            m_prev = m_sc[...]
            m_new = jnp.maximum(m_prev, jnp.max(sc, axis=1, keepdims=True))
            p = jnp.exp2((sc - m_new) * (MLA_SCALE * LOG2_E))
            a = jnp.exp2((m_prev - m_new) * (MLA_SCALE * LOG2_E))
            l_sc[...] = a * l_sc[...] + jnp.sum(p, axis=1, keepdims=True)
            acc_sc[...] = a * acc_sc[...] + _bdot(vt_ref[...], _mx(p), 2, 1)
            m_sc[...] = m_new

        @pl.when(ki < qi)
        def _():
            step(False)

        @pl.when(ki == qi)
        def _():
            step(True)
            o_ref[...] = acc_sc[...] / l_sc[...]
            lse_ref[...] = m_sc[...] * MLA_SCALE + jnp.log(l_sc[...])

    qspec = pl.BlockSpec((g, t, QK_PAD), lambda h, i, j: (h, i, 0))
    kspec = pl.BlockSpec((g, t, QK_PAD), lambda h, i, j: (h, jnp.minimum(i, j), 0))
    vspec = pl.BlockSpec((g, HEAD_DIM, t), lambda h, i, j: (h, 0, jnp.minimum(i, j)))
    out_shape = [jax.ShapeDtypeStruct((hds, HEAD_DIM, s), F32), jax.ShapeDtypeStruct((hds, 1, s), F32)]
    in_specs = [qspec, kspec, vspec]
    out_specs = [pl.BlockSpec((g, HEAD_DIM, t), lambda h, i, j: (h, 0, i)), pl.BlockSpec((g, 1, t), lambda h, i, j: (h, 0, i))]
    scratch = [pltpu.VMEM((g, 1, t), F32), pltpu.VMEM((g, 1, t), F32), pltpu.VMEM((g, HEAD_DIM, t), F32)]
    args = [q, k, v_t]
    if nl:
        out_shape += [jax.ShapeDtypeStruct((N_CHIPS,) + a.shape, a.dtype) for a in late]
        in_specs += [ANY] * nl
        out_specs += [ANY] * nl
        scratch += [pltpu.SemaphoreType.DMA((6 * nl,)), pltpu.SemaphoreType.DMA((6 * nl,))]
        args += list(late)
    return pl.pallas_call(
        body, name="mla_attn_fwd",
        out_shape=tuple(out_shape), grid=(n_groups, n, n),
        in_specs=in_specs, out_specs=tuple(out_specs), scratch_shapes=scratch,
        compiler_params=pltpu.CompilerParams(dimension_semantics=("arbitrary",) * 3, vmem_limit_bytes=VMEM_LIMIT_BYTES,
                                             has_side_effects=nl > 0),
    )(*args)


def _mla_attn_bwd(q, k, v, do, lse, dd, *, t, g, early=()):
    hds, s, _ = q.shape
    n = s // t
    n_groups = hds // g
    ne = len(early)

    def body(*refs):
        q_ref, k_ref, v_ref, do_ref, lse_ref, dd_ref = refs[:6]
        ps_refs = refs[6:6 + ne]
        dq_ref, dk_ref, dv_ref = refs[6 + ne:9 + ne]
        ss_refs = refs[9 + ne:9 + 2 * ne]
        dq_sc, dk_sc, dv_sc = refs[9 + 2 * ne:12 + 2 * ne]
        hg, ki, qi = pl.program_id(0), pl.program_id(1), pl.program_id(2)
        if ne:
            send_sems, recv_sems = refs[12 + 2 * ne:]
            _exchange_in_steps(ps_refs, ss_refs, send_sems, recv_sems,
                               first=jnp.logical_and(hg == 0, jnp.logical_and(ki == 0, qi == 0)),
                               last=jnp.logical_and(hg == n_groups - 1, jnp.logical_and(ki == n - 1, qi == n - 1)))

        @pl.when(jnp.logical_and(ki == 0, qi == 0))
        def _():
            dq_sc[...] = jnp.zeros_like(dq_sc)

        @pl.when(qi == 0)
        def _():
            dk_sc[...] = jnp.zeros_like(dk_sc)
            dv_sc[...] = jnp.zeros_like(dv_sc)

        def step(masked):
            qb, kb, dob = q_ref[...], k_ref[...], do_ref[...]
            sc = _bdot(kb, qb, 2, 2) * MLA_SCALE
            if masked:
                sc = jnp.where(_causal_mask_t(t), sc, NEG)
            p = jnp.exp(sc - lse_ref[...])
            dv_sc[...] += _bdot(_mx(p), dob, 2, 1)
            dp = _bdot(v_ref[...], dob, 2, 2)
            ds = _mx(p * (dp - dd_ref[...]) * MLA_SCALE)
            dk_sc[...] += _bdot(ds, qb, 2, 1)
            dq_sc[qi] += _bdot(ds, kb, 1, 1)

        @pl.when(qi == ki)
        def _():
            step(True)

        @pl.when(qi > ki)
        def _():
            step(False)

        @pl.when(qi == n - 1)
        def _():
            dk_ref[...] = dk_sc[...]
            dv_ref[...] = dv_sc[...]

        @pl.when(jnp.logical_and(ki == n - 1, qi == n - 1))
        def _():
            for j in range(n):
                dq_ref[:, j * t:(j + 1) * t, :] = dq_sc[j]

    qs = lambda w: pl.BlockSpec((g, t, w), lambda h, j, i: (h, jnp.maximum(i, j), 0))
    ks = lambda w: pl.BlockSpec((g, t, w), lambda h, j, i: (h, j, 0))
    rowq = pl.BlockSpec((g, 1, t), lambda h, j, i: (h, 0, jnp.maximum(i, j)))
    scratch = [pltpu.VMEM((n, g, t, QK_PAD), F32), pltpu.VMEM((g, t, QK_PAD), F32), pltpu.VMEM((g, t, HEAD_DIM), F32)]
    if ne:
        scratch += [pltpu.SemaphoreType.DMA((3 * ne,)), pltpu.SemaphoreType.DMA((3 * ne,))]
    return pl.pallas_call(
        body, name="mla_attn_bwd",
        out_shape=(jax.ShapeDtypeStruct((hds, s, QK_PAD), F32), jax.ShapeDtypeStruct((hds, s, QK_PAD), F32),
                   jax.ShapeDtypeStruct((hds, s, HEAD_DIM), F32)) + tuple(jax.ShapeDtypeStruct(a.shape, a.dtype) for a in early),
        grid=(n_groups, n, n),
        in_specs=[qs(QK_PAD), ks(QK_PAD), ks(HEAD_DIM), qs(HEAD_DIM), rowq, rowq] + [ANY] * ne,
        out_specs=(pl.BlockSpec((g, s, QK_PAD), lambda h, j, i: (h, 0, 0)), ks(QK_PAD), ks(HEAD_DIM)) + (ANY,) * ne,
        scratch_shapes=scratch,
        compiler_params=pltpu.CompilerParams(dimension_semantics=("arbitrary",) * 3, vmem_limit_bytes=VMEM_LIMIT_BYTES,
                                             has_side_effects=ne > 0),
    )(q, k, v, do, lse, dd, *early)


def _perm_row(a, dil):
    if dil == 1:
        return a
    hds, _, s = a.shape
    return a.reshape(hds, s // dil, dil).transpose(0, 2, 1).reshape(hds, 1, s)


def _unperm_row(a, dil):
    if dil == 1:
        return a
    hds, _, s = a.shape
    return a.reshape(hds, dil, s // dil).transpose(0, 2, 1).reshape(hds, 1, s)


def _dil_bias(dil):
    slopes = 2.0 ** (-8.0 * jnp.arange(1, HEADS + 1, dtype=F32) / HEADS)
    ik = jnp.arange(DIL_BLOCK)[:, None]
    iq = jnp.arange(DIL_BLOCK)[None, :]
    off_c = iq - ik
    off_p = iq - ik + DIL_BLOCK
    b_c = -slopes[:, None, None] * (off_c * dil).astype(F32)[None]
    b_p = -slopes[:, None, None] * (off_p * dil).astype(F32)[None]
    b_c = jnp.where((off_c >= 0)[None], b_c, NEG)
    b_p = jnp.where((off_p <= DIL_BLOCK)[None], b_p, NEG)
    return b_c, b_p


def _dil_fwd(q, k, v, dil, *, name):
    hds, s, e = q.shape
    blk = DIL_BLOCK
    nblk = s // blk
    nb = nblk // dil
    pair = next(p for p in (4, 2, 1) if nb % p == 0)
    b_c, b_p = _dil_bias(dil)

    def body(q_ref, k_ref, kp_ref, v_ref, vp_ref, bc_ref, bp_ref, o_ref, lse_ref):
        first = ((pair * pl.program_id(0)) % nb) == 0
        bc, bp = bc_ref[...], bp_ref[...]
        for j in range(pair):
            rows = slice(j * blk, (j + 1) * blk)
            qb = q_ref[:, rows, :]
            if j == 0:
                kp, vp = kp_ref[...], vp_ref[...]
            else:
                kp, vp = k_ref[:, (j - 1) * blk:j * blk, :], v_ref[:, (j - 1) * blk:j * blk, :]
            s_c = _bdot(k_ref[:, rows, :], qb, 2, 2) * DIL_SCALE + bc
            s_p = _bdot(kp, qb, 2, 2) * DIL_SCALE + bp
            if j == 0:
                s_p = jnp.where(first, NEG, s_p)
            m = jnp.maximum(jnp.max(s_c, axis=1, keepdims=True), jnp.max(s_p, axis=1, keepdims=True))
            p_c = jnp.exp(s_c - m)
            p_p = jnp.exp(s_p - m)
            l = jnp.sum(p_c, axis=1, keepdims=True) + jnp.sum(p_p, axis=1, keepdims=True)
            o = _bdot(_mx(p_c), v_ref[:, rows, :], 1, 1) + _bdot(_mx(p_p), vp, 1, 1)
            o_ref[:, rows, :] = o / jnp.swapaxes(l, 1, 2)
            lse_ref[:, :, rows] = m + jnp.log(l)

    cur = lambda w: pl.BlockSpec((hds, pair * blk, w), lambda b: (0, b, 0))
    prev = lambda w: pl.BlockSpec((hds, blk, w), lambda b: (0, jnp.maximum(pair * b - 1, 0), 0))
    bias = pl.BlockSpec((hds, blk, blk), lambda b: (0, 0, 0))
    return pl.pallas_call(
        body, name=name,
        out_shape=(jax.ShapeDtypeStruct((hds, s, e), F32), jax.ShapeDtypeStruct((hds, 1, s), F32)),
        grid=(nblk // pair,),
        in_specs=[cur(e), cur(e), prev(e), cur(e), prev(e), bias, bias],
        out_specs=(cur(e), pl.BlockSpec((hds, 1, pair * blk), lambda b: (0, 0, b))),
        compiler_params=_params("parallel"),
    )(q, k, k, v, v, b_c, b_p)


def _dil_combine(os_, lses, *, ts):
    hds, s, e = os_[0].shape
    dils = [d for _, d in DIL_PAIRS]

    def body(o0, o1, o2, l0, l1, l2, o_ref, l_ref, sc1, sc2):
        _load_token_order(sc1, o1, dils[1], ts)
        _load_token_order(sc2, o2, dils[2], ts)
        a0, a1, a2 = l0[...], l1[...], l2[...]
        m = jnp.maximum(jnp.maximum(a0, a1), a2)
        e0, e1, e2 = jnp.exp(a0 - m), jnp.exp(a1 - m), jnp.exp(a2 - m)
        tot = e0 + e1 + e2
        col = lambda w: jnp.swapaxes(w, 1, 2)
        res = (col(e0 / tot) * o0[...] + col(e1 / tot) * sc1[...]) + col(e2 / tot) * sc2[...]
        for hd in range(hds):
            o_ref[:, hd * e:(hd + 1) * e] = res[hd]
        l_ref[...] = m + jnp.log(tot)

    _, specs = _residue_major_outs(s, ts, dils, F32)
    view = lambda a, d: a if d == 1 else a.reshape(hds, d, s // d, e)
    rspec = pl.BlockSpec((hds, 1, ts), lambda i: (0, 0, i))
    return pl.pallas_call(
        body, name="dil_combine",
        out_shape=(jax.ShapeDtypeStruct((s, hds * e), F32), jax.ShapeDtypeStruct((hds, 1, s), F32)),
        grid=(s // ts,),
        in_specs=list(specs) + [rspec] * 3,
        out_specs=(pl.BlockSpec((ts, hds * e), lambda i: (i, 0)), rspec),
        scratch_shapes=[pltpu.VMEM((hds, ts, e), F32), pltpu.VMEM((hds, ts, e), F32)],
        compiler_params=_params("parallel"),
    )(*[view(a, d) for a, d in zip(os_, dils)], *lses)


def _dil_bwd(q, k, v, do, lj, dd, dil, *, name):
    hds, s, e = q.shape
    blk = DIL_BLOCK
    nblk = s // blk
    nb = nblk // dil
    pair = next(p for p in (4, 2, 1) if nb % p == 0)
    b_c, b_p = _dil_bias(dil)

    def body(q_ref, qn_ref, k_ref, kp_ref, v_ref, vp_ref, do_ref, don_ref, l_ref, ln_ref, d_ref, dn_ref,
             bc_ref, bp_ref, dq_ref, dk_ref, dv_ref):
        b0 = pair * pl.program_id(0)
        first = (b0 % nb) == 0
        nxt = jnp.logical_and(b0 + pair < nblk, ((b0 + pair) % nb) != 0)
        bc, bp = bc_ref[...], bp_ref[...]
        for j in range(pair):
            rows = slice(j * blk, (j + 1) * blk)
            qb, kc, vc = q_ref[:, rows, :], k_ref[:, rows, :], v_ref[:, rows, :]
            dob, l, d = _mx(do_ref[:, rows, :]), l_ref[:, :, rows], d_ref[:, :, rows]
            if j == 0:
                kp, vp = kp_ref[...], vp_ref[...]
            else:
                kp, vp = k_ref[:, (j - 1) * blk:j * blk, :], v_ref[:, (j - 1) * blk:j * blk, :]
            p_c = jnp.exp(_bdot(kc, qb, 2, 2) * DIL_SCALE + bc - l)
            p_p = jnp.exp(_bdot(kp, qb, 2, 2) * DIL_SCALE + bp - l)
            if j == 0:
                p_p = jnp.where(first, 0.0, p_p)
            ds_c = _mx(p_c * (_bdot(vc, dob, 2, 2) - d) * DIL_SCALE)
            ds_p = _mx(p_p * (_bdot(vp, dob, 2, 2) - d) * DIL_SCALE)
            dq_ref[:, rows, :] = _bdot(ds_c, kc, 1, 1) + _bdot(ds_p, kp, 1, 1)
            if j < pair - 1:
                nrows = slice((j + 1) * blk, (j + 2) * blk)
                qn, donb, ln, dn = q_ref[:, nrows, :], _mx(do_ref[:, nrows, :]), l_ref[:, :, nrows], d_ref[:, :, nrows]
            else:
                qn, donb, ln, dn = qn_ref[...], _mx(don_ref[...]), ln_ref[...], dn_ref[...]
            p_n = jnp.exp(_bdot(kc, qn, 2, 2) * DIL_SCALE + bp - ln)
            if j == pair - 1:
                p_n = jnp.where(nxt, p_n, 0.0)
            ds_n = _mx(p_n * (_bdot(vc, donb, 2, 2) - dn) * DIL_SCALE)
            dk_ref[:, rows, :] = _bdot(ds_c, qb, 2, 1) + _bdot(ds_n, qn, 2, 1)
            dv_ref[:, rows, :] = _bdot(_mx(p_c), dob, 2, 1) + _bdot(_mx(p_n), donb, 2, 1)

    cur = lambda w: pl.BlockSpec((hds, pair * blk, w), lambda b: (0, b, 0))
    prev = lambda w: pl.BlockSpec((hds, blk, w), lambda b: (0, jnp.maximum(pair * b - 1, 0), 0))
    nxt_ = lambda w: pl.BlockSpec((hds, blk, w), lambda b: (0, jnp.minimum(pair * (b + 1), nblk - 1), 0))
    rcur = pl.BlockSpec((hds, 1, pair * blk), lambda b: (0, 0, b))
    rnxt = pl.BlockSpec((hds, 1, blk), lambda b: (0, 0, jnp.minimum(pair * (b + 1), nblk - 1)))
    bias = pl.BlockSpec((hds, blk, blk), lambda b: (0, 0, 0))
    out = jax.ShapeDtypeStruct((hds, s, e), F32)
    return pl.pallas_call(
        body, name=name,
        out_shape=(out, out, out),
        grid=(nblk // pair,),
        in_specs=[cur(e), nxt_(e), cur(e), prev(e), cur(e), prev(e), cur(e), nxt_(e),
                  rcur, rnxt, rcur, rnxt, bias, bias],
        out_specs=(cur(e), cur(e), cur(e)),
        compiler_params=_params("parallel"),
    )(q, q, k, k, v, v, do, do, lj, lj, dd, dd, b_c, b_p)


def _ln_fwd(z, g, b):
    mu = jnp.mean(z, axis=-1, keepdims=True)
    zc = z - mu
    var = jnp.mean(zc * zc, axis=-1, keepdims=True)
    rstd = lax.rsqrt(var + LN_EPS)
    xhat = zc * rstd
    return xhat * g + b, xhat, rstd


def _ln_bwd(dy, xhat, rstd, g):
    dxh = dy * g
    return rstd * (dxh - jnp.mean(dxh, axis=-1, keepdims=True) - xhat * jnp.mean(dxh * xhat, axis=-1, keepdims=True))


def _out_ln1(a_mla, a_dil, w_o, x, g, b, *, tm):
    s = x.shape[0]
    half = HEADS * HEAD_DIM

    def body(am_ref, ad_ref, w_ref, x_ref, g_ref, b_ref, x1_ref, xh_ref, r_ref, x1m_ref):
        mix = _dot(_mx(am_ref[...]), w_ref[0:half, :]) + _dot(_mx(ad_ref[...]), w_ref[half:2 * half, :])
        z = DN_ALPHA * x_ref[...] + mix
        y, xhat, rstd = _ln_fwd(z, g_ref[...], b_ref[...])
        x1_ref[...] = y
        xh_ref[...] = xhat
        r_ref[...] = rstd
        x1m_ref[...] = _mx(y)

    row = lambda w: pl.BlockSpec((tm, w), lambda i: (i, 0))
    full = lambda shp: pl.BlockSpec(shp, lambda i: (0,) * len(shp))
    act = jax.ShapeDtypeStruct((s, D_MODEL), F32)
    return pl.pallas_call(
        body, name="out_ln1",
        out_shape=(act, act, jax.ShapeDtypeStruct((s, 1), F32), jax.ShapeDtypeStruct((s, D_MODEL), MXU_DTYPE)),
        grid=(s // tm,),
        in_specs=[row(half), row(half), full((D_MODEL, D_MODEL)), row(D_MODEL), full((1, D_MODEL)), full((1, D_MODEL))],
        out_specs=(row(D_MODEL), row(D_MODEL), row(1), row(D_MODEL)),
        compiler_params=_params("parallel"),
    )(a_mla, a_dil, w_o, x, g, b)


def _down_ln2_loss(act, w_down, x1, g, b, target, *, tm):
    s = x1.shape[0]

    def body(a_ref, w_ref, x1_ref, g_ref, b_ref, t_ref, dz_ref, loss_ref, dg_ref, db_ref):
        i = pl.program_id(0)

        @pl.when(i == 0)
        def _():
            loss_ref[...] = jnp.zeros_like(loss_ref)
            dg_ref[...] = jnp.zeros_like(dg_ref)
            db_ref[...] = jnp.zeros_like(db_ref)

        gam = g_ref[...]
        z = DN_ALPHA * x1_ref[...] + _dot(a_ref[...], w_ref[...])
        y, xhat, rstd = _ln_fwd(z, gam, b_ref[...])
        err = y - t_ref[...]
        loss_ref[...] += 0.5 * jnp.sum(jnp.mean(err * err, axis=-1, keepdims=True))
        dy = err * (1.0 / D_MODEL)
        dg_ref[...] += jnp.sum(dy * xhat, axis=0, keepdims=True)
        db_ref[...] += jnp.sum(dy, axis=0, keepdims=True)
        dz_ref[...] = _ln_bwd(dy, xhat, rstd, gam)

    row = lambda w: pl.BlockSpec((tm, w), lambda i: (i, 0))
    full = lambda shp: pl.BlockSpec(shp, lambda i: (0,) * len(shp))
    vec = jax.ShapeDtypeStruct((1, D_MODEL), F32)
    return pl.pallas_call(
        body, name="down_ln2_loss",
        out_shape=(jax.ShapeDtypeStruct((s, D_MODEL), F32), jax.ShapeDtypeStruct((1, LANES), F32), vec, vec),
        grid=(s // tm,),
        in_specs=[row(D_FF), full((D_FF, D_MODEL)), row(D_MODEL), full((1, D_MODEL)), full((1, D_MODEL)), row(D_MODEL)],
        out_specs=(row(D_MODEL), full((1, LANES)), full((1, D_MODEL)), full((1, D_MODEL))),
        compiler_params=_params("arbitrary"),
    )(act, w_down, x1, g, b, target)


def _up_bwd_ln1(du_a, du_g, w_up_t, dz2, xhat1, rstd1, g, *, tm):
    s = dz2.shape[0]

    def body(dua_ref, dug_ref, wa_ref, wg_ref, dz2_ref, xh_ref, r_ref, g_ref, dz1_ref, dg_ref, db_ref):
        i = pl.program_id(0)

        @pl.when(i == 0)
        def _():
            dg_ref[...] = jnp.zeros_like(dg_ref)
            db_ref[...] = jnp.zeros_like(db_ref)

        dx1 = DN_ALPHA * dz2_ref[...] + (_dot(dua_ref[...], wa_ref[...]) + _dot(dug_ref[...], wg_ref[...]))
        xhat = xh_ref[...]
        dg_ref[...] += jnp.sum(dx1 * xhat, axis=0, keepdims=True)
        db_ref[...] += jnp.sum(dx1, axis=0, keepdims=True)
        dz1_ref[...] = _ln_bwd(dx1, xhat, r_ref[...], g_ref[...])

    row = lambda w: pl.BlockSpec((tm, w), lambda i: (i, 0))
    full = lambda shp: pl.BlockSpec(shp, lambda i: (0,) * len(shp))
    vec = jax.ShapeDtypeStruct((1, D_MODEL), F32)
    return pl.pallas_call(
        body, name="up_bwd_ln1",
        out_shape=(jax.ShapeDtypeStruct((s, D_MODEL), F32), vec, vec),
        grid=(s // tm,),
        in_specs=[row(D_FF), row(D_FF),
                  pl.BlockSpec((D_FF, D_MODEL), lambda i: (0, 0)), pl.BlockSpec((D_FF, D_MODEL), lambda i: (1, 0)),
                  row(D_MODEL), row(D_MODEL), row(1), full((1, D_MODEL))],
        out_specs=(row(D_MODEL), full((1, D_MODEL)), full((1, D_MODEL))),
        compiler_params=_params("arbitrary"),
    )(du_a, du_g, w_up_t, w_up_t, dz2, xhat1, rstd1, g)


GELU_C = math.sqrt(2.0 / math.pi)


def _gelu(x):
    cdf = 0.5 * (1.0 + jnp.tanh(GELU_C * (x + 0.044715 * (x * x * x))))
    return x * cdf


def _gelu_grad(x):
    t = jnp.tanh(GELU_C * (x + 0.044715 * (x * x * x)))
    return 0.5 * (1.0 + t) + 0.5 * x * (1.0 - t * t) * (GELU_C * (1.0 + 3.0 * 0.044715 * (x * x)))


def _shift_down(u, halo):
    r1, r2 = pltpu.roll(u, 1, 0), pltpu.roll(u, 2, 0)
    row = lax.broadcasted_iota(jnp.int32, (SUBLANES, u.shape[1]), 0)
    h7, h6 = halo[7:8, :], halo[6:7, :]
    head1 = jnp.where(row == 0, h7, r1[:SUBLANES])
    head2 = jnp.where(row == 0, h6, jnp.where(row == 1, h7, r2[:SUBLANES]))
    return (jnp.concatenate([head1, r1[SUBLANES:]], axis=0), jnp.concatenate([head2, r2[SUBLANES:]], axis=0))


def _shift_up(d, nxt):
    t = d.shape[0]
    r1, r2 = pltpu.roll(d, t - 1, 0), pltpu.roll(d, t - 2, 0)
    row = lax.broadcasted_iota(jnp.int32, (SUBLANES, d.shape[1]), 0)
    n0, n1 = nxt[0:1, :], nxt[1:2, :]
    last = t - SUBLANES
    tail1 = jnp.where(row == SUBLANES - 1, n0, r1[last:])
    tail2 = jnp.where(row == SUBLANES - 1, n1, jnp.where(row == SUBLANES - 2, n0, r2[last:]))
    return (jnp.concatenate([r1[:last], tail1], axis=0), jnp.concatenate([r2[:last], tail2], axis=0))


def _conv(u, s1, s2, w, b):
    return ((b + w[0:1, :] * s2) + w[1:2, :] * s1) + w[2:3, :] * u


def _up_gate_fwd(x1, x1_f32, w_up, conv_w, conv_b, *, tm, tn):
    s = x1.shape[0]
    nj = D_FF // tn
    hb = tm // SUBLANES

    def body(x_ref, xh_ref, wua_ref, wug_ref, wa_ref, wg_ref, ba_ref, bg_ref,
             ua_ref, ug_ref, o_ref, a_ref, ge_ref, gd_ref):
        keep = pl.program_id(1) > 0
        xb, xh = _mx(x_ref[...]), _mx(xh_ref[...])
        wua, wug = wua_ref[...], wug_ref[...]
        ua, ug = _dot(xb, wua), _dot(xb, wug)
        ha = jnp.where(keep, _dot(xh, wua), 0.0)
        hg = jnp.where(keep, _dot(xh, wug), 0.0)
        ua_ref[...] = ua
        ug_ref[...] = ug
        a = _conv(ua, *_shift_down(ua, ha), wa_ref[...], ba_ref[...])
        g = _conv(ug, *_shift_down(ug, hg), wg_ref[...], bg_ref[...])
        ge = _gelu(g)
        o_ref[...] = (ge * a).astype(o_ref.dtype)
        a_ref[...] = a
        ge_ref[...] = ge
        gd_ref[...] = _gelu_grad(g)

    main = lambda off: pl.BlockSpec((tm, tn), lambda j, i: (i, j + off))
    wspec = lambda r, off: pl.BlockSpec((r, tn), lambda j, i: (0, j + off))
    if w_up.ndim == 3:
        wu = lambda off: pl.BlockSpec((None, D_MODEL, tn), lambda j, i: (j + off, 0, 0))
    else:
        wu = lambda off: pl.BlockSpec((D_MODEL, tn), lambda j, i: (0, j + off))
    keep_f32 = jax.ShapeDtypeStruct((s, D_FF), F32)
    return pl.pallas_call(
        body, name="up_gate_fwd",
        out_shape=(keep_f32, keep_f32, jax.ShapeDtypeStruct((s, D_FF), MXU_DTYPE), keep_f32, keep_f32, keep_f32),
        grid=(nj, s // tm),
        in_specs=[pl.BlockSpec((tm, D_MODEL), lambda j, i: (i, 0)),
                  pl.BlockSpec((SUBLANES, D_MODEL), lambda j, i: (jnp.maximum(i * hb - 1, 0), 0)),
                  wu(0), wu(nj), wspec(3, 0), wspec(3, nj), wspec(1, 0), wspec(1, nj)],
        out_specs=(main(0),) * 6,
        compiler_params=_params("parallel", "parallel"),
    )(x1, x1_f32, w_up, w_up, conv_w, conv_w, conv_b, conv_b)


def _gate_bwd(u_a, u_g, dz2, w_down_t, a, ge, gd, conv_w, *, tm, tn):
    s = u_a.shape[0]
    nj = D_FF // tn
    ni = s // tm
    hb = tm // SUBLANES

    def body(ua_ref, ug_ref, ha_ref, hg_ref, dz_ref, dzn_ref, wd_ref, a_ref, an_ref, ge_ref, gen_ref, gd_ref, gdn_ref,
             wa_ref, wg_ref, dua_ref, dug_ref, dwa_ref, dwg_ref, dba_ref, dbg_ref):
        i = pl.program_id(1)

        @pl.when(i == 0)
        def _():
            for r in (dwa_ref, dwg_ref, dba_ref, dbg_ref):
                r[...] = jnp.zeros_like(r)

        wa, wg = wa_ref[...], wg_ref[...]
        ua, ug = ua_ref[...], ug_ref[...]
        ha = jnp.where(i > 0, ha_ref[...], 0.0)
        hg = jnp.where(i > 0, hg_ref[...], 0.0)
        sa1, sa2 = _shift_down(ua, ha)
        sg1, sg2 = _shift_down(ug, hg)
        wd = wd_ref[...]
        d = _dot(_mx(dz_ref[...]), wd)
        dya = d * ge_ref[...]
        dyg = d * a_ref[...] * gd_ref[...]
        dn = jnp.where(i < ni - 1, _dot(_mx(dzn_ref[...]), wd), 0.0)
        dya_n = dn * gen_ref[...]
        dyg_n = dn * an_ref[...] * gdn_ref[...]
        da1, da2 = _shift_up(dya, dya_n)
        dg1, dg2 = _shift_up(dyg, dyg_n)
        dua_ref[...] = (wa[2:3, :] * dya + wa[1:2, :] * da1 + wa[0:1, :] * da2).astype(dua_ref.dtype)
        dug_ref[...] = (wg[2:3, :] * dyg + wg[1:2, :] * dg1 + wg[0:1, :] * dg2).astype(dug_ref.dtype)
        ssum = lambda v: jnp.sum(v, axis=0, keepdims=True)
        dwa_ref[...] += jnp.concatenate([ssum(dya * sa2), ssum(dya * sa1), ssum(dya * ua)], axis=0)
        dwg_ref[...] += jnp.concatenate([ssum(dyg * sg2), ssum(dyg * sg1), ssum(dyg * ug)], axis=0)
        dba_ref[...] += ssum(dya)
        dbg_ref[...] += ssum(dyg)

    main = pl.BlockSpec((tm, tn), lambda j, i: (i, j))
    halo = pl.BlockSpec((SUBLANES, tn), lambda j, i: (jnp.maximum(i * hb - 1, 0), j))
    next_row = lambda j, i: jnp.minimum((i + 1) * hb, s // SUBLANES - 1)
    nxt = pl.BlockSpec((SUBLANES, tn), lambda j, i: (next_row(j, i), j))
    wspec = lambda r, off: pl.BlockSpec((r, tn), lambda j, i: (0, j + off))
    return pl.pallas_call(
        body, name="gate_bwd",
        out_shape=(jax.ShapeDtypeStruct((s, D_FF), MXU_DTYPE), jax.ShapeDtypeStruct((s, D_FF), MXU_DTYPE),
                   jax.ShapeDtypeStruct((3, D_FF), F32), jax.ShapeDtypeStruct((3, D_FF), F32),
                   jax.ShapeDtypeStruct((1, D_FF), F32), jax.ShapeDtypeStruct((1, D_FF), F32)),
        grid=(nj, ni),
        in_specs=[main, main, halo, halo,
                  pl.BlockSpec((tm, D_MODEL), lambda j, i: (i, 0)),
                  pl.BlockSpec((SUBLANES, D_MODEL), lambda j, i: (next_row(j, i), 0)),
                  pl.BlockSpec((D_MODEL, tn), lambda j, i: (0, j))]
        + [main, nxt] * 3 + [wspec(3, 0), wspec(3, nj)],
        out_specs=(main, main, wspec(3, 0), wspec(3, 0), wspec(1, 0), wspec(1, 0)),
        compiler_params=_params("parallel", "arbitrary"),
    )(u_a, u_g, u_a, u_g, dz2, dz2, w_down_t, a, a, ge, ge, gd, gd, conv_w, conv_w)


def _prep_weights(w_in, w_uq, w_uk, w_uv, w_o, w_up, w_down):
    return {**_prep_weights_first(w_in, w_uq, w_uk, w_uv), **_prep_weights_late(w_o, w_up, w_down)}


def _prep_weights_late(w_o, w_up, w_down):
    w_o, w_up, w_down = _mx(w_o), _mx(w_up), _mx(w_down)
    w_up_t = w_up.T if w_up.ndim == 2 else w_up.transpose(0, 2, 1).reshape(2 * D_FF, D_MODEL)
    return dict(w_o=w_o, w_o_t=w_o.T, w_up=w_up, w_up_t=w_up_t, w_down=w_down, w_down_t=w_down.T)


def _prep_weights_first(w_in, w_uq, w_uk, w_uv):
    c = lambda a: a.astype(MXU_DTYPE)
    w_in = c(w_in)
    z = lambda w: jnp.zeros((D_MODEL, w), MXU_DTYPE)
    r0 = Q_RANK + KV_RANK
    w_in_ext = jnp.concatenate([w_in[:, :r0], z(NOPE), w_in[:, r0:r0 + ROPE], z(32), w_in[:, r0 + ROPE:]], axis=1)
    wq = jnp.pad(c(w_uq).transpose(1, 0, 2), ((0, 0), (0, 0), (0, QK_PAD - NOPE - ROPE)))
    wk = jnp.pad(c(w_uk).transpose(1, 0, 2), ((0, 0), (0, 0), (0, QK_PAD - NOPE)))
    wv = c(w_uv).transpose(1, 0, 2)
    t3 = lambda a: a.transpose(0, 2, 1)
    return dict(w_in=w_in_ext, w_in_t=w_in_ext.T, wq=wq, wq_t=t3(wq), wk=wk, wk_t=t3(wk), wv=wv, wv_t=t3(wv))


def _local_step(x, target, w, g_cq, g_ckv, ln1_g, ln1_b, conv_w, conv_b, ln2_g, ln2_b, comm=None):
    s = x.shape[0]
    tabs = _rope_tables(s)
    r2 = lambda a: a.reshape(1, -1)
    cb = r2(conv_b)
    dils = [d for _, d in DIL_PAIRS]

    h, qp, kp, vp = _in_proj(x, w["w_in"], tm=512)
    q, k, v, v_t = _mla_prep_fwd(h, r2(g_cq), r2(g_ckv), w["wq"], w["wk"], w["wv"], w["wv_t"], tabs, tm=256)
    if comm is None:
        o_mla_t, lse_mla = _mla_attn_fwd(q, k, v_t, t=512, g=HEADS)
    else:
        o_mla_t, lse_mla, *gathered = _mla_attn_fwd(q, k, v_t, t=512, g=HEADS, late=comm["late"])
        w = {**w, **comm["finish"](gathered)}
    o_bs, lse_bs = [], []
    for i, d in enumerate(dils):
        o_b, l_b = _dil_fwd(qp[i], kp[i], vp[i], d, name=f"dil_fwd_{d}")
        o_bs.append(o_b)
        lse_bs.append(_unperm_row(l_b, d))
    o_dil, lj = _dil_combine(o_bs, lse_bs, ts=512)
    o_mla = o_mla_t.transpose(2, 0, 1).reshape(s, HEADS * HEAD_DIM)
    x1, xhat1, rstd1, x1_m = _out_ln1(o_mla, o_dil, w["w_o"], x, r2(ln1_g), r2(ln1_b), tm=512)
    u_a, u_g, act, conv_a, gelu_g, gelu_dg = _up_gate_fwd(x1_m, x1, w["w_up"], conv_w, cb, tm=256, tn=1408)
    dz2, loss, dg2, db2 = _down_ln2_loss(act, w["w_down"], x1, r2(ln2_g), r2(ln2_b), target, tm=512)

    dw_down = _mm_tn(act, dz2, name="dw_down", tm=1408, tn=D_MODEL, ts=DW_TOKENS)
    du_a, du_g, dcw_a, dcw_g, dcb_a, dcb_g = _gate_bwd(u_a, u_g, dz2, w["w_down_t"], conv_a, gelu_g, gelu_dg, conv_w,
                                                       tm=256, tn=1408)
    dz1, dg1, db1 = _up_bwd_ln1(du_a, du_g, w["w_up_t"], dz2, xhat1, rstd1, r2(ln1_g), tm=256)
    dw_up = jnp.concatenate([_mm_tn(x1_m, du_a, name="dw_up_a", tm=D_MODEL, tn=1408, ts=DW_TOKENS),
                             _mm_tn(x1_m, du_g, name="dw_up_g", tm=D_MODEL, tn=1408, ts=DW_TOKENS)], axis=1)
    named_early = [("w_up", dw_up), ("w_down", dw_down)]
    swap = () if comm is None else comm["blocked"](named_early)
    do_mla, do_dil, dd_all, received = _attn_bwd_heads(dz1, w["w_o_t"], o_mla, o_dil, tm=512, swap=swap)
    dw_o = jnp.concatenate([_mm_tn(o_mla, dz1, name="dw_o_mla", tm=512, tn=D_MODEL, ts=DW_TOKENS),
                            _mm_tn(o_dil, dz1, name="dw_o_dil", tm=512, tn=D_MODEL, ts=DW_TOKENS)], axis=0)
    dd_all = dd_all.T
    dd_mla, dd_dil = dd_all[:HEADS].reshape(HEADS, 1, s), dd_all[HEADS:].reshape(HEADS, 1, s)
    early = () if comm is None else tuple(comm["add_halves"](named_early, swap, received))
    dq, dk, dv, *early_slots = _mla_attn_bwd(q, k, v, do_mla, lse_mla, dd_mla, t=512, g=4, early=early)
    parts = []
    for i, d in enumerate(dils):
        parts.append(_dil_bwd(qp[i], kp[i], vp[i], do_dil[i], _perm_row(lj, d), _perm_row(dd_dil, d), d, name=f"dil_bwd_{d}"))
    dh_dil = _dil_merge(parts, ts=512)
    dh_mla, dwq, dwk, dwv, dgq, dgkv = _mla_prep_bwd(h, dq, dk, dv, r2(g_cq), r2(g_ckv),
                                                     w["wq_t"], w["wk_t"], w["wv_t"], tabs, tm=256)
    mla_w = 4 * LANES
    w_in_t = w["w_in_t"]
    grad_x = _mm_nn(dh_mla, w_in_t[:mla_w], name="in_bwd_mla", tm=512, tn=D_MODEL, tk=mla_w, add=dz1, add_scale=DN_ALPHA)
    grad_x = _mm_nn(dh_dil, w_in_t[mla_w:], name="in_bwd_dil", tm=512, tn=D_MODEL, tk=3 * HEADS * HEAD_DIM, add=grad_x)
    dw_mla = _mm_tn(x, dh_mla, name="dw_in_mla", tm=D_MODEL, tn=mla_w, ts=DW_TOKENS)
    dw_dil = _mm_tn(x, dh_dil, name="dw_in_dil", tm=D_MODEL, tn=3 * HEADS * HEAD_DIM, ts=DW_TOKENS)
    r0 = Q_RANK + KV_RANK
    grads = dict(
        w_in=jnp.concatenate([dw_mla[:, :r0], dw_mla[:, r0 + NOPE:r0 + NOPE + ROPE], dw_dil], axis=1),
        g_cq=dgq[0], g_ckv=dgkv[0],
        w_uq=dwq[:, :, :NOPE + ROPE].transpose(1, 0, 2),
        w_uk=dwk[:, :, :NOPE].transpose(1, 0, 2),
        w_uv=dwv.transpose(1, 0, 2),
        w_o=dw_o, ln1_g=dg1[0], ln1_b=db1[0], w_up=dw_up,
        conv_w=jnp.concatenate([dcw_a, dcw_g], axis=1), conv_b=jnp.concatenate([dcb_a, dcb_g], axis=1)[0],
        w_down=dw_down, ln2_g=dg2[0], ln2_b=db2[0])
    if comm is not None:
        grads["early"] = (early, tuple(early_slots))
    return loss[0, 0], grad_x, grads


N_CHIPS = 4
SHARDED = ("w_in", "w_uq", "w_o", "w_up", "conv_w", "w_down")
COL_SHARDED = ("w_in", "w_up", "conv_w")
SHARD_SHAPE = dict(w_in=(D_MODEL, IN_WIDTH // 4), w_uq=(Q_RANK // 4, HEADS, NOPE + ROPE), w_o=(D_MODEL // 4, D_MODEL),
                   w_up=(D_MODEL, 2 * D_FF // 4), conv_w=(3, 2 * D_FF // 4), w_down=(D_FF // 4, D_MODEL))
SMALL = ("g_cq", "g_ckv", "w_uk", "w_uv", "ln1_g", "ln1_b", "conv_b", "ln2_g", "ln2_b")
SMALL_SHAPE = dict(g_cq=(Q_RANK,), g_ckv=(KV_RANK,), w_uk=(KV_RANK, HEADS, NOPE), w_uv=(KV_RANK, HEADS, HEAD_DIM),
                   ln1_g=(D_MODEL,), ln1_b=(D_MODEL,), conv_b=(2 * D_FF,), ln2_g=(D_MODEL,), ln2_b=(D_MODEL,))
BIG = ("w_in", "w_uq", "w_o", "w_up", "w_down")
BIG_2D = dict(w_in=(D_MODEL, IN_WIDTH // 4), w_uq=(Q_RANK // 4, HEADS * (NOPE + ROPE)), w_o=(D_MODEL // 4, D_MODEL),
              w_up=(D_MODEL, 2 * D_FF // 4), w_down=(D_FF // 4, D_MODEL))
SMALL_G = SMALL + ("conv_w",)
SMALL_WIDE = ("w_uk", "w_uv")
SMALL_G_SHAPE = {**SMALL_SHAPE, "conv_w": (3, 2 * D_FF)}
SMALL_U_SHAPE = {**SMALL_SHAPE, "conv_w": (3, 2 * D_FF // 4)}


def _size(shape):
    return math.prod(shape)


def _padded_rows(n_elems, mult):
    return -(-n_elems // (LANES * mult)) * mult


SHARD_ROWS = {n: _padded_rows(_size(SHARD_SHAPE[n]), SUBLANES) for n in SHARDED}
R_SMALL = -(-sum(_size(SMALL_G_SHAPE[n]) for n in SMALL_G) // (LANES * LANES)) * LANES
GATHER_FIRST = ("w_in", "w_uq")
GATHER_LATE = ("w_o", "w_up", "w_down")
REDUCED_EARLY = ("w_up", "w_down")
REDUCED_LAST = ("w_in", "w_uq", "w_o")


def _rows(a, rows=None):
    flat = a.reshape(-1)
    rows = -(-flat.shape[0] // LANES) if rows is None else rows
    return jnp.pad(flat, (0, rows * LANES - flat.shape[0])).reshape(rows, LANES)


def _blocked(name, g):
    r, c = BIG_2D[name]
    a = g.reshape(r, N_CHIPS, c).transpose(1, 0, 2) if name in COL_SHARDED else g.reshape(N_CHIPS, r, c)
    return a.reshape(N_CHIPS, 2, r // 2, c)


def _pack_flat(t, names, rows=None):
    flat = jnp.concatenate([t[n].astype(F32).reshape(-1) for n in names])
    return _rows(flat, R_SMALL if rows is None else rows)


def _unpack_flat(buf, names, shapes):
    flat, out, r = buf.reshape(-1), {}, 0
    for n in names:
        out[n] = flat[r:r + _size(shapes[n])].reshape(shapes[n])
        r += _size(shapes[n])
    return out


def _from_chip_blocks(name, blocks):
    shp = SHARD_SHAPE[name]
    a = blocks.reshape(N_CHIPS, -1)[:, :_size(shp)].reshape((N_CHIPS,) + shp)
    if name in COL_SHARDED:
        return a.transpose(1, 0, 2).reshape(shp[0], N_CHIPS * shp[1])
    return a.reshape((N_CHIPS * shp[0],) + shp[1:])


ANY = pl.BlockSpec(memory_space=pl.ANY)
COMM_PARAMS = pltpu.CompilerParams(has_side_effects=True)


def _coords():
    return lax.axis_index("x"), lax.axis_index("y"), lax.axis_index("c")


def _other_chips(x, y):
    return [(1 - x, y), (x, 1 - y), (1 - x, 1 - y)]


def _remote(src, dst, send_sems, recv_sems, k, to):
    return pltpu.make_async_remote_copy(src_ref=src, dst_ref=dst, send_sem=send_sems.at[k], recv_sem=recv_sems.at[k],
                                        device_id=to, device_id_type=MESH)


def _gather_in_steps(wp_refs, wout_refs, send_sems, recv_sems, *, first, mid, last):
    x, y, c = _coords()
    me = 2 * x + y
    sib = (x, y, 1 - c)
    chips = _other_chips(x, y)
    n = len(wp_refs)
    pairs = [(j, t, px, py) for j, (px, py) in enumerate(chips) for t in range(n)]
    ici = [_remote(wp_refs[t].at[c], wout_refs[t].at[me, c], send_sems, recv_sems, j * n + t, (px, py, c))
           for j, t, px, py in pairs]
    fwd = [_remote(wout_refs[t].at[2 * px + py, c], wout_refs[t].at[2 * px + py, c], send_sems, recv_sems, (3 + j) * n + t, sib)
           for j, t, px, py in pairs]

    @pl.when(first)
    def _():
        for cp in ici:
            cp.start()

    @pl.when(mid)
    def _():
        for i, (j, t, px, py) in enumerate(pairs):
            _remote(wp_refs[t].at[c], wout_refs[t].at[2 * px + py, c], send_sems, recv_sems, j * n + t, (px, py, c)).wait_recv()
            fwd[i].start()

    @pl.when(last)
    def _():
        for j, t, px, py in pairs:
            k = 2 * px + py
            _remote(wout_refs[t].at[k, 1 - c], wout_refs[t].at[k, 1 - c], send_sems, recv_sems, (3 + j) * n + t, sib).wait_recv()
        for cp in ici + fwd:
            cp.wait_send()


def _swap_halves_in_steps(gs_refs, os_refs, send_sems, recv_sems, *, first, last):
    x, y, c = _coords()
    sib = (x, y, 1 - c)
    cps = [_remote(gs_refs[t].at[k, 1 - c], os_refs[t].at[k], send_sems, recv_sems, t * N_CHIPS + k, sib)
           for t in range(len(gs_refs)) for k in range(N_CHIPS)]

    @pl.when(first)
    def _():
        for cp in cps:
            cp.start()

    @pl.when(last)
    def _():
        for cp in cps:
            cp.wait_recv()
        for cp in cps:
            cp.wait_send()


def _exchange_in_steps(ps_refs, ss_refs, send_sems, recv_sems, *, first, last):
    x, y, c = _coords()
    me = 2 * x + y
    chips = _other_chips(x, y)
    n = len(ps_refs)
    sends = [_remote(ps_refs[t].at[2 * px + py], ss_refs[t].at[me], send_sems, recv_sems, j * n + t, (px, py, c))
             for j, (px, py) in enumerate(chips) for t in range(n)]

    @pl.when(first)
    def _():
        for cp in sends:
            cp.start()

    @pl.when(last)
    def _():
        for j, (px, py) in enumerate(chips):
            for t in range(n):
                _remote(ps_refs[t].at[me], ss_refs[t].at[2 * px + py], send_sems, recv_sems, j * n + t, (px, py, c)).wait_recv()
        for cp in sends:
            cp.wait_send()


def _gather_weights(wp, cwp):
    def body(wp_ref, cw_ref, wout_ref, cwout_ref, send_sems, recv_sems):
        x, y, c = _coords()
        me = 2 * x + y
        sib = (x, y, 1 - c)
        chips = _other_chips(x, y)
        sends = [_remote(wp_ref.at[c], wout_ref.at[me, c], send_sems, recv_sems, j, (px, py, c))
                 for j, (px, py) in enumerate(chips)]
        sends += [_remote(cw_ref, cwout_ref.at[me], send_sems, recv_sems, 3 + j, (px, py, c))
                  for j, (px, py) in enumerate(chips)]
        for cp in sends:
            cp.start()
        for j, (px, py) in enumerate(chips):
            k = 2 * px + py
            _remote(wp_ref.at[c], wout_ref.at[k, c], send_sems, recv_sems, j, (px, py, c)).wait_recv()
            fwd = _remote(wout_ref.at[k, c], wout_ref.at[k, c], send_sems, recv_sems, 6 + j, sib)
            fwd.start()
            sends.append(fwd)
        for j, (px, py) in enumerate(chips):
            k = 2 * px + py
            _remote(cw_ref, cwout_ref.at[k], send_sems, recv_sems, 3 + j, (px, py, c)).wait_recv()
            _remote(wout_ref.at[k, 1 - c], wout_ref.at[k, 1 - c], send_sems, recv_sems, 6 + j, sib).wait_recv()
        for cp in sends:
            cp.wait_send()

    return pl.pallas_call(
        body, name="gather_weights",
        out_shape=(jax.ShapeDtypeStruct((N_CHIPS,) + wp.shape, wp.dtype), jax.ShapeDtypeStruct((N_CHIPS,) + cwp.shape, cwp.dtype)),
        in_specs=[ANY, ANY], out_specs=(ANY, ANY),
        scratch_shapes=[pltpu.SemaphoreType.DMA((9,)), pltpu.SemaphoreType.DMA((9,))],
        compiler_params=COMM_PARAMS,
    )(wp, cwp)


def _exchange_sibling_halves(gs, whole, *, name):
    n, nw = len(gs), len(whole)

    def body(*refs):
        gs_refs, wh_refs = refs[:n], refs[n:n + nw]
        os_refs, ow_refs = refs[n + nw:2 * n + nw], refs[2 * n + nw:2 * (n + nw)]
        send_sems, recv_sems = refs[2 * (n + nw):]
        x, y, c = _coords()
        sib = (x, y, 1 - c)
        cps = [_remote(gs_refs[t].at[k, 1 - c], os_refs[t].at[k], send_sems, recv_sems, t * N_CHIPS + k, sib)
               for t in range(n) for k in range(N_CHIPS)]
        cps += [_remote(wh_refs[t], ow_refs[t], send_sems, recv_sems, n * N_CHIPS + t, sib) for t in range(nw)]
        for cp in cps:
            cp.start()
        for cp in cps:
            cp.wait_recv()
        for cp in cps:
            cp.wait_send()

    n_sem = n * N_CHIPS + nw
    return pl.pallas_call(
        body, name=name,
        out_shape=tuple(jax.ShapeDtypeStruct((N_CHIPS,) + a.shape[2:], F32) for a in gs)
        + tuple(jax.ShapeDtypeStruct(a.shape, F32) for a in whole),
        in_specs=[ANY] * (n + nw), out_specs=(ANY,) * (n + nw),
        scratch_shapes=[pltpu.SemaphoreType.DMA((n_sem,)), pltpu.SemaphoreType.DMA((n_sem,))],
        compiler_params=COMM_PARAMS,
    )(*gs, *whole)


def _exchange_chips(ps, whole):
    n, nw = len(ps), len(whole)
    per_chip = n + nw

    def body(*refs):
        ps_refs, wh_refs = refs[:n], refs[n:per_chip]
        ss_refs, sw_refs = refs[per_chip:per_chip + n], refs[per_chip + n:2 * per_chip]
        send_sems, recv_sems = refs[2 * per_chip:]
        x, y, c = _coords()
        me = 2 * x + y
        chips = _other_chips(x, y)
        sends = []
        for j, (px, py) in enumerate(chips):
            to = (px, py, c)
            for t in range(n):
                sends.append(_remote(ps_refs[t].at[2 * px + py], ss_refs[t].at[me], send_sems, recv_sems, j * per_chip + t, to))
            for t in range(nw):
                sends.append(_remote(wh_refs[t], sw_refs[t].at[me], send_sems, recv_sems, j * per_chip + n + t, to))
        for cp in sends:
            cp.start()
        for j, (px, py) in enumerate(chips):
            k, to = 2 * px + py, (px, py, c)
            for t in range(n):
                _remote(ps_refs[t].at[me], ss_refs[t].at[k], send_sems, recv_sems, j * per_chip + t, to).wait_recv()
            for t in range(nw):
                _remote(wh_refs[t], sw_refs[t].at[k], send_sems, recv_sems, j * per_chip + n + t, to).wait_recv()
        for cp in sends:
            cp.wait_send()

    n_sem = 3 * per_chip
    return pl.pallas_call(
        body, name="exchange_chips",
        out_shape=tuple(jax.ShapeDtypeStruct(a.shape, a.dtype) for a in ps)
        + tuple(jax.ShapeDtypeStruct((N_CHIPS,) + a.shape, a.dtype) for a in whole),
        in_specs=[ANY] * per_chip, out_specs=(ANY,) * per_chip,
        scratch_shapes=[pltpu.SemaphoreType.DMA((n_sem,)), pltpu.SemaphoreType.DMA((n_sem,))],
        compiler_params=COMM_PARAMS,
    )(*ps, *whole)


def _exchange_sibling_result(gh):
    n = len(gh)

    def body(*refs):
        gh_refs, out_refs, (send_sems, recv_sems) = refs[:n], refs[n:2 * n], refs[2 * n:]
        x, y, c = _coords()
        cps = [_remote(gh_refs[t], out_refs[t], send_sems, recv_sems, t, (x, y, 1 - c)) for t in range(n)]
        for cp in cps:
            cp.start()
        for cp in cps:
            cp.wait_recv()
        for cp in cps:
            cp.wait_send()

    return pl.pallas_call(
        body, name="exchange_sibling_result",
        out_shape=tuple(jax.ShapeDtypeStruct(a.shape, F32) for a in gh),
        in_specs=[ANY] * n, out_specs=(ANY,) * n,
        scratch_shapes=[pltpu.SemaphoreType.DMA((n,)), pltpu.SemaphoreType.DMA((n,))],
        compiler_params=COMM_PARAMS,
    )(*gh)


def _add_own_half(gs, recv, c_arr, *, name):
    _, rows, cols = recv.shape

    def body(c_ref, a_ref, b_ref, o_ref):
        o_ref[0] = (a_ref[0, 0] + b_ref[0]).astype(o_ref.dtype)

    return pl.pallas_call(
        body, name=name,
        out_shape=jax.ShapeDtypeStruct(recv.shape, GRAD_WIRE_DTYPE),
        grid_spec=pltpu.PrefetchScalarGridSpec(
            num_scalar_prefetch=1, grid=(N_CHIPS,),
            in_specs=[pl.BlockSpec((1, 1, rows, cols), lambda k, c_ref: (k, c_ref[0], 0, 0)),
                      pl.BlockSpec((1, rows, cols), lambda k, c_ref: (k, 0, 0))],
            out_specs=pl.BlockSpec((1, rows, cols), lambda k, c_ref: (k, 0, 0))),
        compiler_params=_params("parallel"),
    )(c_arr, gs, recv)


def _add2(a, b, *, name, out_dtype=F32):
    def body(a_ref, b_ref, o_ref):
        o_ref[...] = (a_ref[...] + b_ref[...]).astype(o_ref.dtype)

    return pl.pallas_call(body, name=name, out_shape=jax.ShapeDtypeStruct(a.shape, out_dtype))(a, b)


def _sum_slots(slots, *, tr, name):
    _, r, c = slots.shape

    def body(s_ref, o_ref):
        f = lambda k: s_ref[k].astype(F32)
        o_ref[...] = ((f(0) + f(1)) + f(2)) + f(3)

    return pl.pallas_call(
        body, name=name,
        out_shape=jax.ShapeDtypeStruct((r, c), F32),
        grid=(r // tr,),
        in_specs=[pl.BlockSpec((N_CHIPS, tr, c), lambda i: (0, i, 0))],
        out_specs=pl.BlockSpec((tr, c), lambda i: (i, 0)),
        compiler_params=_params("parallel"),
    )(slots)


def _adamw(w, g, m, v, *, tr, name):
    r, cols = w.shape

    def body(w_ref, g_ref, m_ref, v_ref, d_ref, nm_ref, nv_ref):
        g_ = g_ref[...]
        m_ = ADAM_B1 * m_ref[...] + (1.0 - ADAM_B1) * g_
        v_ = ADAM_B2 * v_ref[...] + (1.0 - ADAM_B2) * (g_ * g_)
        m_hat = m_ / (1.0 - ADAM_B1 ** ADAM_STEP)
        v_hat = v_ / (1.0 - ADAM_B2 ** ADAM_STEP)
        d_ref[...] = -ADAM_LR * (m_hat / (jnp.sqrt(v_hat) + ADAM_EPS) + ADAM_WD * w_ref[...])
        nm_ref[...] = m_
        nv_ref[...] = v_

    spec = pl.BlockSpec((tr, cols), lambda i: (i, 0))
    out = jax.ShapeDtypeStruct((r, cols), F32)
    return pl.pallas_call(
        body, name=name, out_shape=(out, out, out), grid=(r // tr,),
        in_specs=[spec] * 4, out_specs=(spec,) * 3,
        compiler_params=_params("parallel"),
    )(w, g, m, v)


WEIGHTS = ("w_in", "g_cq", "g_ckv", "w_uq", "w_uk", "w_uv", "w_o", "ln1_g", "ln1_b", "w_up", "conv_w", "conv_b",
           "w_down", "ln2_g", "ln2_b")


def kernel(x, w_in, g_cq, g_ckv, w_uq, w_uk, w_uv, w_o, ln1_g, ln1_b, w_up, conv_w, conv_b, w_down, ln2_g, ln2_b, loss_target, m_w_in, m_g_cq, m_g_ckv, m_w_uq, m_w_uk, m_w_uv, m_w_o, m_ln1_g, m_ln1_b, m_w_up, m_conv_w, m_conv_b, m_w_down, m_ln2_g, m_ln2_b, v_w_in, v_g_cq, v_g_ckv, v_w_uq, v_w_uk, v_w_uv, v_w_o, v_ln1_g, v_ln1_b, v_w_up, v_conv_w, v_conv_b, v_w_down, v_ln2_g, v_ln2_b):
    wts = dict(zip(WEIGHTS, (w_in, g_cq, g_ckv, w_uq, w_uk, w_uv, w_o, ln1_g, ln1_b, w_up, conv_w, conv_b, w_down, ln2_g, ln2_b)))
    mom = dict(zip(WEIGHTS, (m_w_in, m_g_cq, m_g_ckv, m_w_uq, m_w_uk, m_w_uv, m_w_o, m_ln1_g, m_ln1_b, m_w_up, m_conv_w, m_conv_b, m_w_down, m_ln2_g, m_ln2_b)))
    var = dict(zip(WEIGHTS, (v_w_in, v_g_cq, v_g_ckv, v_w_uq, v_w_uk, v_w_uv, v_w_o, v_ln1_g, v_ln1_b, v_w_up, v_conv_w, v_conv_b, v_w_down, v_ln2_g, v_ln2_b)))

    me = 2 * lax.axis_index("x") + lax.axis_index("y")
    my_c = lax.axis_index("c")
    c_arr = my_c.astype(jnp.int32).reshape(1)
    own = lambda slots, mine: lax.dynamic_update_index_in_dim(slots, mine, me, 0)

    def pack(names):
        return jnp.concatenate([_rows(_mx(wts[n]), SHARD_ROWS[n]) for n in names], axis=0).reshape(2, -1, LANES)

    def unpack(names, gathered, mine):
        buf, full, r = own(gathered, mine).reshape(N_CHIPS, -1, LANES), {}, 0
        for n in names:
            full[n] = _from_chip_blocks(n, buf[:, r:r + SHARD_ROWS[n]])
            r += SHARD_ROWS[n]
        return full

    wp_first = pack(GATHER_FIRST)
    cwp = _rows(conv_w, SHARD_ROWS["conv_w"])
    gathered, cwfull = _gather_weights(wp_first, cwp)
    full = unpack(GATHER_FIRST, gathered, wp_first)
    conv_w_full = _from_chip_blocks("conv_w", own(cwfull, cwp))
    w = _prep_weights_first(full["w_in"], full["w_uq"], w_uk, w_uv)
    late_halves = [_mx(wts[n]).reshape(2, BIG_2D[n][0] // 2, BIG_2D[n][1]) for n in GATHER_LATE]

    def finish(gathered_late):
        w_o_b, w_up_b, w_down_b = (own(a, mine).reshape((N_CHIPS,) + BIG_2D[n])
                                   for a, mine, n in zip(gathered_late, late_halves, GATHER_LATE))
        return _prep_weights_late(w_o_b.reshape(D_MODEL, D_MODEL), w_up_b, w_down_b.reshape(D_FF, D_MODEL))

    def blocked(named):
        return [_blocked(n, a) for n, a in named]

    def add_halves(named, gb, recv):
        return [_add_own_half(gb[i], recv[i], c_arr, name=f"add_half_{n}") for i, (n, _) in enumerate(named)]

    def halve(named, whole, wire):
        gb = blocked(named)
        recv = _exchange_sibling_halves(gb, list(whole), name="exchange_sibling_halves")
        return add_halves(named, gb, recv) + [_add2(a, recv[len(gb) + i], name=f"add_whole_{i}", out_dtype=wire[i])
                                              for i, a in enumerate(whole)]

    comm = dict(late=late_halves, finish=finish, blocked=blocked, add_halves=add_halves)
    loss, grad_x, g = _local_step(x[0], loss_target[0], w, g_cq, g_ckv, ln1_g, ln1_b, conv_w_full, conv_b, ln2_g, ln2_b, comm=comm)

    ps_early, slots_early = g.pop("early")
    g["loss"] = loss.reshape(1)
    narrow = tuple(n for n in SMALL_G if n not in SMALL_WIDE) + ("loss",)
    narrow_shape = {**SMALL_G_SHAPE, "loss": (1,)}
    r_narrow = _padded_rows(sum(_size(narrow_shape[n]) for n in narrow), SUBLANES)
    r_wide = _padded_rows(sum(_size(SMALL_G_SHAPE[n]) for n in SMALL_WIDE), 2 * SUBLANES)
    *ps_rest, pr, pw = halve([(n, g[n]) for n in REDUCED_LAST],
                             whole=[_pack_flat(g, narrow, r_narrow), _pack_flat(g, SMALL_WIDE, r_wide)],
                             wire=[F32, GRAD_WIRE_DTYPE])
    *slots_rest, slots_r, slots_w = _exchange_chips(ps_rest, [pr, pw])
    ps = {**dict(zip(REDUCED_LAST, ps_rest)), **dict(zip(REDUCED_EARLY, ps_early))}
    slots = {**dict(zip(REDUCED_LAST, slots_rest)), **dict(zip(REDUCED_EARLY, slots_early))}
    slots = [own(slots[n], lax.dynamic_index_in_dim(ps[n], me, 0, keepdims=False)) for n in BIG]
    g_half = [_sum_slots(slots[i], tr=slots[i].shape[1] // 2, name=f"sum_chips_{n}") for i, n in enumerate(BIG)]
    g_small = {**_unpack_flat(_sum_slots(own(slots_r, pr), tr=r_narrow, name="sum_chips_narrow"), narrow, narrow_shape),
               **_unpack_flat(_sum_slots(own(slots_w, pw), tr=r_wide, name="sum_chips_wide"), SMALL_WIDE, SMALL_G_SHAPE)}
    loss = g_small.pop("loss")[0]
    g_other = _exchange_sibling_result(g_half)
    grads = {n: jnp.where(my_c == 0, jnp.concatenate([g_half[i], g_other[i]]), jnp.concatenate([g_other[i], g_half[i]]))
             for i, n in enumerate(BIG)}
    g_small["conv_w"] = lax.dynamic_slice_in_dim(g_small["conv_w"], me * SHARD_SHAPE["conv_w"][1], SHARD_SHAPE["conv_w"][1], 1)
    grads.update(g_small)

    res = {}
    for n in BIG:
        as2d = lambda a: a.reshape(BIG_2D[n])
        d, m, v = _adamw(as2d(wts[n]), grads[n], as2d(mom[n]), as2d(var[n]), tr=BIG_2D[n][0] // 4, name=f"adamw_{n}")
        res[n] = [a.reshape(SHARD_SHAPE[n]) for a in (grads[n], d, m, v)]
    flat = lambda t: _pack_flat(t, SMALL_G)
    dmv = _adamw(flat(wts), flat(g_small), flat(mom), flat(var), tr=R_SMALL, name="adamw_small")
    dmv = [_unpack_flat(a, SMALL_G, SMALL_U_SHAPE) for a in dmv]
    for n in SMALL_G:
        res[n] = [g_small[n]] + [t[n] for t in dmv]
    outs = [res[n][j] for j in range(4) for n in WEIGHTS]
    return (loss, grad_x[None], *outs)
```
